```python
import jax, jax.numpy as jnp
from jax import lax
import numpy as np


D_MODEL = 1024
BATCH = 8
SEQ = 2048
DEPTH = 2
DEC_BATCH = 128
DEC_SEQ = 1
PAST_LEN = 16384
PAGE_SIZE = 128

GLA_HEADS = 4
GLA_DK = 48
GLA_DV = 96
GLA_LOWRANK = 16
GLA_TAU = 16.0
RET_HEADS = 4
RET_DK = 48
RET_DV = 96
ROPE_BASE = 10000.0
HGRN_HEADS = 4
HGRN_DK = 64
HGRN_DV = 64
MIX_WIDTH = GLA_HEADS * GLA_DV + RET_HEADS * RET_DV + HGRN_HEADS * HGRN_DV
IN_SIZES = (GLA_HEADS * GLA_DK, GLA_HEADS * GLA_DK, GLA_HEADS * GLA_DV, GLA_HEADS * GLA_DV, GLA_LOWRANK,
            RET_HEADS * RET_DK, RET_HEADS * RET_DK, RET_HEADS * RET_DV, RET_HEADS * RET_DV,
            HGRN_HEADS * HGRN_DK, HGRN_HEADS * HGRN_DK, HGRN_HEADS * HGRN_DV, HGRN_HEADS * HGRN_DV)
N_IN = sum(IN_SIZES)
CHUNK = 64
PEER_HEADS = 8
PEER_DQ = 256
N_KEYS = 128
N_EXPERTS = N_KEYS * N_KEYS
PEER_TOPK = 16
PEER_BLOCK = 128
PLE_DIM = 256
ALPHA = (2 * DEPTH) ** 0.25
BETA = (8 * DEPTH) ** -0.25
NORM_EPS = 1e-5

kernel_name = 'hymba_gla_retnet_hgrn2_peer_step'


def layer_norm(x, g, b):
    xf = x.astype(jnp.float32)
    mu = jnp.mean(xf, axis=-1, keepdims=True)
    var = jnp.mean(jnp.square(xf - mu), axis=-1, keepdims=True)
    return (xf - mu) * lax.rsqrt(var + NORM_EPS) * g + b


def head_rms_norm(o, g):
    of = o.astype(jnp.float32)
    return of * lax.rsqrt(jnp.mean(jnp.square(of), axis=-1, keepdims=True) + NORM_EPS) * g[:, None, :]


def head_group_norm(o, g):
    of = o.astype(jnp.float32)
    mu = jnp.mean(of, axis=-1, keepdims=True)
    var = jnp.mean(jnp.square(of - mu), axis=-1, keepdims=True)
    return (of - mu) * lax.rsqrt(var + NORM_EPS) * g[:, None, :]


def split_heads(t, n):
    B, L, _ = t.shape
    return t.reshape(B, L, n, -1).transpose(0, 2, 1, 3)


def merge_heads(o):
    B, H, L, d = o.shape
    return o.transpose(0, 2, 1, 3).reshape(B, L, H * d)


def rotary(t, pos):
    d = t.shape[-1]
    inv = 1.0 / (ROPE_BASE ** (jnp.arange(0, d, 2, dtype=jnp.float32) / d))
    ang = pos[:, None] * inv[None, :]
    cos, sin = jnp.cos(ang), jnp.sin(ang)
    t1, t2 = t[..., : d // 2], t[..., d // 2:]
    return jnp.concatenate([t1 * cos - t2 * sin, t1 * sin + t2 * cos], axis=-1)


def gated_linear_recurrence(q, k, v, log_decay, s0):
    B, H, L, _ = q.shape
    dv = v.shape[-1]
    C = CHUNK if L % CHUNK == 0 else L
    n = L // C

    def to_chunks(a):
        a = a.astype(jnp.float32)
        return a.reshape(B, H, n, C, a.shape[-1]).transpose(2, 0, 1, 3, 4)

    causal = jnp.tril(jnp.ones((C, C), dtype=bool))[:, :, None]

    def step(S, inp):
        qc, kc, vc, gc = inp
        b = jnp.cumsum(gc, axis=2)
        diff = b[:, :, :, None, :] - b[:, :, None, :, :]
        decay = jnp.where(causal, jnp.exp(jnp.where(causal, diff, 0.0)), 0.0)
        scores = jnp.sum(qc[:, :, :, None, :] * kc[:, :, None, :, :] * decay, axis=-1)
        o = (jnp.einsum('bhij,bhjv->bhiv', scores, vc)
             + jnp.einsum('bhik,bhkv->bhiv', qc * jnp.exp(b), S))
        b_last = b[:, :, -1:, :]
        S_new = (S * jnp.exp(b_last[:, :, 0, :, None])
                 + jnp.einsum('bhjk,bhjv->bhkv', kc * jnp.exp(b_last - b), vc))
        return S_new, o

    S, o = lax.scan(step, s0.astype(jnp.float32),
                    (to_chunks(q), to_chunks(k), to_chunks(v), to_chunks(log_decay)))
    return o.transpose(1, 2, 0, 3, 4).reshape(B, H, L, dv), S


def mixer_block(x, pos, s_gla, s_ret, s_hgrn, w_in, gla_w_gate, gla_b_gate, gla_norm,
                ret_norm, hgrn_lb, hgrn_norm, w_out):
    B, L, _ = x.shape
    z = jnp.einsum('bld,dn->bln', x, w_in)
    offsets = [int(c) for c in np.cumsum(IN_SIZES)[:-1]]
    (gq, gk, gv, gg, glr, rq, rk, rv, rg, hq, hf, hi, hg) = jnp.split(z, offsets, axis=-1)

    log_a = jax.nn.log_sigmoid((jnp.einsum('blr,rk->blk', glr, gla_w_gate) + gla_b_gate).astype(jnp.float32)) / GLA_TAU
    o_gla, s_gla_new = gated_linear_recurrence(
        split_heads(gq, GLA_HEADS) * GLA_DK ** -0.5, split_heads(gk, GLA_HEADS),
        split_heads(gv, GLA_HEADS), split_heads(log_a, GLA_HEADS), s_gla)
    o_gla = merge_heads(head_rms_norm(o_gla, gla_norm)) * jax.nn.silu(gg)

    log_gamma = jnp.log1p(-jnp.exp2(-5.0 - jnp.arange(RET_HEADS, dtype=jnp.float32)))
    g_ret = jnp.broadcast_to(log_gamma[None, :, None, None], (B, RET_HEADS, L, 1))
    o_ret, s_ret_new = gated_linear_recurrence(
        rotary(split_heads(rq, RET_HEADS), pos),
        rotary(split_heads(rk, RET_HEADS), pos) * RET_DK ** -0.5,
        split_heads(rv, RET_HEADS), g_ret, s_ret)
    o_ret = merge_heads(head_group_norm(o_ret, ret_norm)) * jax.nn.silu(rg)

    log_f = jnp.logaddexp(jnp.log(hgrn_lb), jnp.log1p(-hgrn_lb) + jax.nn.log_sigmoid(hf.astype(jnp.float32)))
    o_h, s_hgrn_new = gated_linear_recurrence(
        split_heads(jax.nn.silu(hq), HGRN_HEADS), split_heads(-jnp.expm1(log_f), HGRN_HEADS),
        split_heads(hi, HGRN_HEADS), split_heads(log_f, HGRN_HEADS), s_hgrn)
    o_h = merge_heads(head_rms_norm(o_h, hgrn_norm)) * jax.nn.silu(hg)

    o = jnp.concatenate([o_gla, o_ret, o_h], axis=-1)
    return jnp.einsum('blm,md->bld', o, w_out), s_gla_new, s_ret_new, s_hgrn_new


def peer_ffn(x, w_q, subkeys, u_tab, v_tab):
    B, L, D = x.shape
    T = B * L
    xt = x.reshape(T, D)
    q = jnp.einsum('td,dn->tn', xt, w_q).reshape(T, PEER_HEADS, PEER_DQ).astype(jnp.float32)
    half = PEER_DQ // 2
    s1 = jnp.einsum('thd,hnd->thn', q[..., :half], subkeys[:, 0].astype(jnp.float32))
    s2 = jnp.einsum('thd,hnd->thn', q[..., half:], subkeys[:, 1].astype(jnp.float32))
    v1, i1 = lax.top_k(s1, PEER_TOPK)
    v2, i2 = lax.top_k(s2, PEER_TOPK)
    cand = (v1[..., :, None] + v2[..., None, :]).reshape(T, PEER_HEADS, PEER_TOPK * PEER_TOPK)
    best, ci = lax.top_k(cand, PEER_TOPK)
    expert = (jnp.take_along_axis(i1, ci // PEER_TOPK, axis=-1) * N_KEYS
              + jnp.take_along_axis(i2, ci % PEER_TOPK, axis=-1))
    gate = jax.nn.softmax(best, axis=-1)
    nb = -(-T // PEER_BLOCK)
    pad = nb * PEER_BLOCK - T
    xb = jnp.pad(xt, ((0, pad), (0, 0))).reshape(nb, PEER_BLOCK, D)
    eb = jnp.pad(expert, ((0, pad), (0, 0), (0, 0))).reshape(nb, PEER_BLOCK, PEER_HEADS, PEER_TOPK)
    gb = jnp.pad(gate, ((0, pad), (0, 0), (0, 0))).reshape(nb, PEER_BLOCK, PEER_HEADS, PEER_TOPK)

    def expert_block(args):
        xc, ec, gc = args
        act = jax.nn.gelu(jnp.einsum('thkd,td->thk', u_tab[ec], xc), approximate=False)
        return jnp.einsum('thk,thkd->td', gc * act, v_tab[ec])

    out = lax.map(expert_block, (xb, eb, gb))
    return out.reshape(nb * PEER_BLOCK, D)[:T].reshape(B, L, D)


def trunk(x, p, pos, s_gla, s_ret, s_hgrn, weights):
    (w_in, gla_w_gate, gla_b_gate, gla_norm, ret_norm, hgrn_lb, hgrn_norm, w_out,
     ln1_g, ln1_b, ln2_g, ln2_b, peer_w_q, peer_subkeys, peer_u, peer_v, ple_proj, ple_gate) = weights
    new_gla, new_ret, new_hgrn = [], [], []
    for i in range(DEPTH):
        mix, sg, sr, sh = mixer_block(x, pos, s_gla[i], s_ret[i], s_hgrn[i], w_in[i], gla_w_gate[i],
                                      gla_b_gate[i], gla_norm[i], ret_norm[i], hgrn_lb[i],
                                      hgrn_norm[i], w_out[i])
        x = layer_norm(ALPHA * x + mix, ln1_g[i], ln1_b[i])
        x = layer_norm(ALPHA * x + peer_ffn(x, peer_w_q[i], peer_subkeys[i], peer_u[i], peer_v[i]),
                       ln2_g[i], ln2_b[i])
        x = x + jnp.einsum('blp,pd->bld', p[i], ple_proj[i]) * jax.nn.sigmoid(
            jnp.einsum('bld,de->ble', x, ple_gate[i]))
        new_gla.append(sg)
        new_ret.append(sr)
        new_hgrn.append(sh)
    return x, jnp.stack(new_gla), jnp.stack(new_ret), jnp.stack(new_hgrn)


def setup_inputs(seed: int = 0) -> dict:
    key = jax.random.key(seed)
    ks = jax.random.split(key, 26)

    def nrm(k, shape, s):
        return s * jax.random.normal(k, shape, jnp.float32)

    return {
        'x_prompt': nrm(ks[0], (BATCH, SEQ, D_MODEL), 1.0),
        'x_sample': nrm(ks[1], (DEC_BATCH, DEC_SEQ, D_MODEL), 1.0),
        'p_prompt': nrm(ks[2], (DEPTH, BATCH, SEQ, PLE_DIM), 1.0),
        'p_sample': nrm(ks[3], (DEPTH, DEC_BATCH, DEC_SEQ, PLE_DIM), 1.0),
        'state_gla': nrm(ks[4], (DEPTH, DEC_BATCH, GLA_HEADS, GLA_DK, GLA_DV), 1.0),
        'state_ret': nrm(ks[5], (DEPTH, DEC_BATCH, RET_HEADS, RET_DK, RET_DV), 1.0),
        'state_hgrn': nrm(ks[6], (DEPTH, DEC_BATCH, HGRN_HEADS, HGRN_DK, HGRN_DV), 1.0),
        'w_in': nrm(ks[7], (DEPTH, D_MODEL, N_IN), D_MODEL ** -0.5),
        'gla_w_gate': nrm(ks[8], (DEPTH, GLA_LOWRANK, GLA_HEADS * GLA_DK), GLA_LOWRANK ** -0.5),
        'gla_b_gate': nrm(ks[9], (DEPTH, GLA_HEADS * GLA_DK), 0.1),
        'gla_norm': 1.0 + nrm(ks[10], (DEPTH, GLA_HEADS, GLA_DV), 0.02),
        'ret_norm': 1.0 + nrm(ks[11], (DEPTH, RET_HEADS, RET_DV), 0.02),
        'hgrn_lb_logits': nrm(ks[12], (DEPTH, HGRN_HEADS * HGRN_DK), 0.5),
        'hgrn_norm': 1.0 + nrm(ks[13], (DEPTH, HGRN_HEADS, HGRN_DV), 0.02),
        'w_out': nrm(ks[14], (DEPTH, MIX_WIDTH, D_MODEL), BETA * MIX_WIDTH ** -0.5),
        'ln1_g': 1.0 + nrm(ks[15], (DEPTH, D_MODEL), 0.02),
        'ln1_b': nrm(ks[16], (DEPTH, D_MODEL), 0.02),
        'ln2_g': 1.0 + nrm(ks[17], (DEPTH, D_MODEL), 0.02),
        'ln2_b': nrm(ks[18], (DEPTH, D_MODEL), 0.02),
        'peer_w_q': nrm(ks[19], (DEPTH, D_MODEL, PEER_HEADS * PEER_DQ), D_MODEL ** -0.5),
        'peer_subkeys': nrm(ks[20], (DEPTH, PEER_HEADS, 2, N_KEYS, PEER_DQ // 2), (PEER_DQ // 2) ** -0.5),
        'peer_u': nrm(ks[21], (DEPTH, N_EXPERTS, D_MODEL), D_MODEL ** -0.5),
        'peer_v': nrm(ks[22], (DEPTH, N_EXPERTS, D_MODEL), BETA),
        'ple_proj': nrm(ks[23], (DEPTH, PLE_DIM, D_MODEL), PLE_DIM ** -0.5),
        'ple_gate': nrm(ks[24], (DEPTH, D_MODEL, D_MODEL), D_MODEL ** -0.5),
    }


def reference(x_prompt, x_sample, p_prompt, p_sample, state_gla, state_ret, state_hgrn,
              w_in, gla_w_gate, gla_b_gate, gla_norm, ret_norm, hgrn_lb_logits, hgrn_norm, w_out,
              ln1_g, ln1_b, ln2_g, ln2_b, peer_w_q, peer_subkeys, peer_u, peer_v, ple_proj, ple_gate):
    lb = jnp.cumsum(jax.nn.softmax(hgrn_lb_logits.astype(jnp.float32), axis=0), axis=0)
    lb = lb - lb[0:1]
    weights = (w_in, gla_w_gate, gla_b_gate, gla_norm, ret_norm, lb, hgrn_norm, w_out,
               ln1_g, ln1_b, ln2_g, ln2_b, peer_w_q, peer_subkeys, peer_u, peer_v, ple_proj, ple_gate)

    Bp, Lp = x_prompt.shape[0], x_prompt.shape[1]
    pos_prompt = jnp.arange(Lp, dtype=jnp.float32)
    y_prompt, gla_p, ret_p, hgrn_p = trunk(
        x_prompt, p_prompt, pos_prompt,
        jnp.zeros((DEPTH, Bp, GLA_HEADS, GLA_DK, GLA_DV), jnp.float32),
        jnp.zeros((DEPTH, Bp, RET_HEADS, RET_DK, RET_DV), jnp.float32),
        jnp.zeros((DEPTH, Bp, HGRN_HEADS, HGRN_DK, HGRN_DV), jnp.float32),
        weights)

    pos_sample = PAST_LEN + jnp.arange(x_sample.shape[1], dtype=jnp.float32)
    y_sample, gla_s, ret_s, hgrn_s = trunk(
        x_sample, p_sample, pos_sample, state_gla, state_ret, state_hgrn, weights)

    return (y_prompt, y_sample, gla_p, ret_p, hgrn_p, gla_s, ret_s, hgrn_s)
```

```python
import functools

import numpy as np
import jax
import jax.numpy as jnp
from jax import lax
from jax.experimental import pallas as pl
from jax.experimental.pallas import tpu as pltpu

F32 = jnp.float32
BF16 = jnp.bfloat16
HIGHEST = lax.Precision.HIGHEST

D_MODEL = 1024
DEPTH = 2
PAST_LEN = 16384
N_HEADS = 4
GLA_DK, GLA_DV = 48, 96
RET_DK, RET_DV = 48, 96
HGRN_DK, HGRN_DV = 64, 64
GLA_LOWRANK = 16
GLA_TAU = 16.0
ROPE_BASE = 10000.0
CHUNK = 64
PEER_HEADS = 8
N_KEYS = 128
PEER_TOPK = 16
PLE_DIM = 256
ALPHA = (2 * DEPTH) ** 0.25
NORM_EPS = 1e-5

LANE = 128
SUBLANE = 8
SLOT = LANE
GROUP_W = N_HEADS * SLOT
VMEM_LIMIT = 56 * 1024 * 1024
NEG_INF = float("-inf")


def _dot(a, b, precision=None):
    return jnp.dot(a, b, preferred_element_type=F32, precision=precision)


def _dot_nt(a, b):
    return lax.dot_general(a, b, (((1,), (1,)), ((), ())), preferred_element_type=F32)


def _dot_tn(a, b):
    return lax.dot_general(a, b, (((0,), (0,)), ((), ())), preferred_element_type=F32)


def _sigmoid(x):
    return jax.nn.sigmoid(x)


def _silu(x):
    return x * _sigmoid(x)


def _log_sigmoid(x):
    return jnp.minimum(x, 0.0) - jnp.log1p(jnp.exp(-jnp.abs(x)))


def _logaddexp(a, c):
    amax = jnp.maximum(a, c)
    delta = a - c
    return jnp.where(jnp.isnan(delta), a + c, amax + jnp.log1p(jnp.exp(-jnp.abs(delta))))


def _gelu(x):
    return 0.5 * x * (1.0 + lax.erf(x * np.float32(0.7071067811865476)))


def _lane_mask(n):
    return (lax.broadcasted_iota(jnp.int32, (1, LANE), 1) < n).astype(F32)


def _rms_head_norm(o, g_row, dv):
    ms = jnp.sum(o * o, axis=-1, keepdims=True) * (1.0 / dv)
    return o * lax.rsqrt(ms + NORM_EPS) * g_row


def _group_head_norm(o, g_row, dv):
    mask = _lane_mask(dv)
    mu = jnp.sum(o, axis=-1, keepdims=True) * (1.0 / dv)
    d = (o - mu) * mask
    var = jnp.sum(d * d, axis=-1, keepdims=True) * (1.0 / dv)
    return d * lax.rsqrt(var + NORM_EPS) * g_row


def _chunk_constants():
    i = np.arange(CHUNK)[:, None]
    t = np.arange(CHUNK)[None, :]
    mats = [t <= i, t > i]
    masks = [i == t]
    half = CHUNK // 2
    while half >= 1:
        blk = i // (2 * half)
        ref = blk * 2 * half + half - 1
        second = (i % (2 * half)) >= half
        mats.append(np.where(second, (t > ref) & (t <= i), (t > i) & (t <= ref)))
        masks.append(second & ((t % (2 * half)) < half) & (blk == t // (2 * half)))
        half //= 2
    return (np.concatenate(mats, 0).astype(np.float32), np.stack(masks).astype(np.float32))


N_LEVELS = 6


def _vector_decay_chunk(q, k, v, g, st, mall, masks_ref):
    args = _dot(mall, g, precision=HIGHEST)
    b = args[0:CHUNK]
    rem = args[CHUNK:2 * CHUNK]
    kb = k.astype(BF16)
    sc = masks_ref[0] * _dot_nt(q.astype(BF16), kb)
    for l in range(N_LEVELS):
        f = jnp.exp(args[(2 + l) * CHUNK:(3 + l) * CHUNK])
        sc = sc + masks_ref[1 + l] * _dot_nt((q * f).astype(BF16), (k * f).astype(BF16))
    vb = v.astype(BF16)
    o = _dot(sc.astype(BF16), vb) + _dot_nt((q * jnp.exp(b)).astype(BF16), st.astype(BF16))
    st_new = st * jnp.exp(b[CHUNK - 1:CHUNK]) + _dot_tn(vb, (k * jnp.exp(rem)).astype(BF16))
    return o, st_new


def _gla_prompt_kernel(x_ref, w_ref, wg_ref, bg_ref, nrm_ref, mall_ref, masks_ref,
                       o_ref, st_ref, z_scr, la_scr, *, seg):
    @pl.when(pl.program_id(1) == 0)
    def _():
        st_ref[...] = jnp.zeros_like(st_ref)

    z_scr[...] = _dot(x_ref[...].astype(BF16), w_ref[...])
    pre = _dot(z_scr[:, 4 * GROUP_W:4 * GROUP_W + SLOT], wg_ref[...], precision=HIGHEST) + bg_ref[...]
    la_scr[...] = _log_sigmoid(pre) * (1.0 / GLA_TAU)
    mall = mall_ref[...]

    def body(c, carry):
        r0 = pl.multiple_of(c * CHUNK, CHUNK)
        rows = pl.ds(r0, CHUNK)
        for h in range(N_HEADS):
            q = z_scr[rows, h * SLOT:(h + 1) * SLOT] * (GLA_DK ** -0.5)
            k = z_scr[rows, GROUP_W + h * SLOT:GROUP_W + (h + 1) * SLOT]
            v = z_scr[rows, 2 * GROUP_W + h * SLOT:2 * GROUP_W + (h + 1) * SLOT]
            gate = z_scr[rows, 3 * GROUP_W + h * SLOT:3 * GROUP_W + (h + 1) * SLOT]
            g = la_scr[rows, h * SLOT:(h + 1) * SLOT]
            o, st_new = _vector_decay_chunk(q, k, v, g, st_ref[0, h], mall, masks_ref)
            st_ref[0, h] = st_new
            on = _rms_head_norm(o, nrm_ref[h:h + 1, :], GLA_DV)
            o_ref[rows, h * SLOT:(h + 1) * SLOT] = (on * _silu(gate)).astype(BF16)
        return carry

    lax.fori_loop(0, seg // CHUNK, body, 0)


def _hgrn_prompt_kernel(x_ref, w_ref, loglb_ref, l1mlb_ref, nrm_ref, mall_ref, masks_ref,
                        o_ref, st_ref, z_scr, *, seg):
    @pl.when(pl.program_id(1) == 0)
    def _():
        st_ref[...] = jnp.zeros_like(st_ref)

    z_scr[...] = _dot(x_ref[...].astype(BF16), w_ref[...])
    mall = mall_ref[...]
    kmask = _lane_mask(HGRN_DK)

    def body(c, carry):
        r0 = pl.multiple_of(c * CHUNK, CHUNK)
        rows = pl.ds(r0, CHUNK)
        for h in range(N_HEADS):
            sl = slice(h * SLOT, (h + 1) * SLOT)
            q = _silu(z_scr[rows, h * SLOT:(h + 1) * SLOT])
            hf = z_scr[rows, GROUP_W + h * SLOT:GROUP_W + (h + 1) * SLOT]
            v = z_scr[rows, 2 * GROUP_W + h * SLOT:2 * GROUP_W + (h + 1) * SLOT]
            gate = z_scr[rows, 3 * GROUP_W + h * SLOT:3 * GROUP_W + (h + 1) * SLOT]
            log_f = _logaddexp(loglb_ref[:, sl], l1mlb_ref[:, sl] + _log_sigmoid(hf))
            k = (1.0 - jnp.exp(log_f)) * kmask
            o, st_new = _vector_decay_chunk(q, k, v, log_f, st_ref[0, h], mall, masks_ref)
            st_ref[0, h] = st_new
            on = _rms_head_norm(o, nrm_ref[h:h + 1, :], HGRN_DV)
            o_ref[rows, sl] = (on * _silu(gate)).astype(BF16)
        return carry

    lax.fori_loop(0, seg // CHUNK, body, 0)


def _rotate(t, cs, sn):
    return t * cs + pltpu.roll(t, SLOT // 2, 1) * sn


def _ret_prompt_kernel(x_ref, w_ref, cos_ref, sin_ref, nrm_ref, dmat_ref, qdec_ref, kdec_ref, sdec_ref,
                       o_ref, st_ref, z_scr, *, seg):
    @pl.when(pl.program_id(1) == 0)
    def _():
        st_ref[...] = jnp.zeros_like(st_ref)

    z_scr[...] = _dot(x_ref[...].astype(BF16), w_ref[...])

    def body(c, carry):
        r0 = pl.multiple_of(c * CHUNK, CHUNK)
        rows = pl.ds(r0, CHUNK)
        cs = cos_ref[rows, :]
        sn = sin_ref[rows, :]
        for h in range(N_HEADS):
            sl = slice(h * SLOT, (h + 1) * SLOT)
            q = _rotate(z_scr[rows, h * SLOT:(h + 1) * SLOT], cs, sn)
            k = _rotate(z_scr[rows, GROUP_W + h * SLOT:GROUP_W + (h + 1) * SLOT], cs, sn) * (RET_DK ** -0.5)
            v = z_scr[rows, 2 * GROUP_W + h * SLOT:2 * GROUP_W + (h + 1) * SLOT]
            gate = z_scr[rows, 3 * GROUP_W + h * SLOT:3 * GROUP_W + (h + 1) * SLOT]
            st = st_ref[0, h]
            vb = v.astype(BF16)
            sc = dmat_ref[h] * _dot_nt(q.astype(BF16), k.astype(BF16))
            o = _dot(sc.astype(BF16), vb) + _dot_nt((q * qdec_ref[h]).astype(BF16), st.astype(BF16))
            st_ref[0, h] = st * sdec_ref[h] + _dot_tn(vb, (k * kdec_ref[h]).astype(BF16))
            on = _group_head_norm(o, nrm_ref[h:h + 1, :], RET_DV)
            o_ref[rows, sl] = (on * _silu(gate)).astype(BF16)
        return carry

    lax.fori_loop(0, seg // CHUNK, body, 0)


def _const_spec(shape):
    nd = len(shape)
    return pl.BlockSpec(shape, lambda *_: (0,) * nd)


def _prompt_mixer_call(kernel, x2d, batch, seq, w, extras, scratch_widths, name):
    seg = min(512, seq)
    nseg = seq // seg
    in_specs = [pl.BlockSpec((seg, D_MODEL), lambda b, s: (b * nseg + s, 0)), _const_spec(w.shape)]
    args = [x2d, w]
    for e in extras:
        if isinstance(e, tuple):
            arr, _ = e
            in_specs.append(pl.BlockSpec((seg, arr.shape[1]), lambda b, s: (s, 0)))
            args.append(arr)
        else:
            in_specs.append(_const_spec(e.shape))
            args.append(e)
    return pl.pallas_call(
        functools.partial(kernel, seg=seg),
        grid=(batch, nseg),
        in_specs=in_specs,
        out_specs=[pl.BlockSpec((seg, GROUP_W), lambda b, s: (b * nseg + s, 0)),
                   pl.BlockSpec((1, N_HEADS, SLOT, SLOT), lambda b, s: (b, 0, 0, 0))],
        out_shape=[jax.ShapeDtypeStruct((batch * seq, GROUP_W), BF16),
                   jax.ShapeDtypeStruct((batch, N_HEADS, SLOT, SLOT), F32)],
        scratch_shapes=[pltpu.VMEM((seg, wd), F32) for wd in scratch_widths],
        compiler_params=pltpu.CompilerParams(dimension_semantics=("arbitrary", "arbitrary"),
                                             vmem_limit_bytes=VMEM_LIMIT),
        name=name,
    )(*args)


def _sample_step(q, k, eg, v, s_ref, ns_ref, tq, tk, te, tv, to, s_t, sn_t, dk, dv, row_of):
    tq[...] = q.T
    tk[...] = k.T
    te[...] = eg.T
    tv[...] = v.T
    s_t[0:dk * dv, :] = s_ref[...].T
    vt = tv[0:dv, :]

    def body(kk, oacc):
        kr = row_of(kk)
        r = pl.multiple_of(kk * dv, SUBLANE)
        sn = s_t[pl.ds(r, dv), :] * te[pl.ds(kr, 1), :] + tk[pl.ds(kr, 1), :] * vt
        sn_t[pl.ds(r, dv), :] = sn
        return oacc + tq[pl.ds(kr, 1), :] * sn

    o_t = lax.fori_loop(0, dk, body, jnp.zeros((dv, q.shape[0]), F32))
    ns_ref[...] = sn_t[0:dk * dv, :].T
    to[...] = jnp.zeros_like(to)
    to[0:dv, :] = o_t
    return to[...].T


def _sample_mixer_kernel(x_ref, wg_ref, wr_ref, wh_ref, wgate_ref, bgate_ref, gn_ref, rn_ref, hn_ref,
                         cos_ref, sin_ref, gam_ref, loglb_ref, l1mlb_ref, sg_ref, sr_ref, sh_ref,
                         o_ref, nsg_ref, nsr_ref, nsh_ref,
                         zg, zr, zh, la, tq, tk, te, tv, to, s_t, sn_t):
    h = pl.program_id(0)

    @pl.when(h == 0)
    def _():
        xb = x_ref[...].astype(BF16)
        zg[...] = _dot(xb, wg_ref[...])
        zr[...] = _dot(xb, wr_ref[...])
        zh[...] = _dot(xb, wh_ref[...])
        pre = _dot(zg[:, 4 * GROUP_W:4 * GROUP_W + SLOT], wgate_ref[...], precision=HIGHEST) + bgate_ref[...]
        la[...] = _log_sigmoid(pre) * (1.0 / GLA_TAU)

    off = pl.multiple_of(h * SLOT, SLOT)
    sl = pl.ds(off, SLOT)
    tr = (tq, tk, te, tv, to, s_t, sn_t)
    batch = x_ref.shape[0]

    q = zg[:, sl] * (GLA_DK ** -0.5)
    k = zg[:, pl.ds(GROUP_W + off, SLOT)]
    v = zg[:, pl.ds(2 * GROUP_W + off, SLOT)]
    gate = zg[:, pl.ds(3 * GROUP_W + off, SLOT)]
    o = _sample_step(q, k, jnp.exp(la[:, sl]), v, sg_ref, nsg_ref, *tr, GLA_DK, GLA_DV, lambda kk: kk)
    o_ref[:, sl] = (_rms_head_norm(o, gn_ref[pl.ds(h, 1), :], GLA_DV) * _silu(gate)).astype(BF16)

    cs = cos_ref[...]
    sn = sin_ref[...]
    q = _rotate(zr[:, sl], cs, sn)
    k = _rotate(zr[:, pl.ds(GROUP_W + off, SLOT)], cs, sn) * (RET_DK ** -0.5)
    v = zr[:, pl.ds(2 * GROUP_W + off, SLOT)]
    gate = zr[:, pl.ds(3 * GROUP_W + off, SLOT)]
    eg = jnp.broadcast_to(gam_ref[h], (batch, SLOT))
    half = RET_DK // 2
    o = _sample_step(q, k, eg, v, sr_ref, nsr_ref, *tr, RET_DK, RET_DV,
                     lambda kk: kk + jnp.where(kk >= half, SLOT // 2 - half, 0))
    o_ref[:, pl.ds(GROUP_W + off, SLOT)] = (
        _group_head_norm(o, rn_ref[pl.ds(h, 1), :], RET_DV) * _silu(gate)).astype(BF16)

    q = _silu(zh[:, sl])
    hf = zh[:, pl.ds(GROUP_W + off, SLOT)]
    v = zh[:, pl.ds(2 * GROUP_W + off, SLOT)]
    gate = zh[:, pl.ds(3 * GROUP_W + off, SLOT)]
    log_f = _logaddexp(loglb_ref[:, sl], l1mlb_ref[:, sl] + _log_sigmoid(hf))
    f = jnp.exp(log_f)
    o = _sample_step(q, (1.0 - f) * _lane_mask(HGRN_DK), f, v, sh_ref, nsh_ref, *tr,
                     HGRN_DK, HGRN_DV, lambda kk: kk)
    o_ref[:, pl.ds(2 * GROUP_W + off, SLOT)] = (
        _rms_head_norm(o, hn_ref[pl.ds(h, 1), :], HGRN_DV) * _silu(gate)).astype(BF16)


def _sample_mixer_call(x2d, lw, sg, sr, sh):
    batch = x2d.shape[0]
    gsz, hsz = GLA_DK * GLA_DV, HGRN_DK * HGRN_DV
    consts = [lw["w_gla"], lw["w_ret"], lw["w_hgrn"], lw["wgate"], lw["bgate"], lw["gla_norm"], lw["ret_norm"],
              lw["hgrn_norm"], lw["cos_s"], lw["sin_s"], lw["gamma"], lw["loglb"], lw["l1mlb"]]
    in_specs = ([_const_spec(x2d.shape)] + [_const_spec(c.shape) for c in consts]
                + [pl.BlockSpec((batch, gsz), lambda h: (0, h)),
                   pl.BlockSpec((batch, gsz), lambda h: (0, h)),
                   pl.BlockSpec((batch, hsz), lambda h: (0, h))])
    out_specs = [_const_spec((batch, 3 * GROUP_W)),
                 pl.BlockSpec((batch, gsz), lambda h: (0, h)),
                 pl.BlockSpec((batch, gsz), lambda h: (0, h)),
                 pl.BlockSpec((batch, hsz), lambda h: (0, h))]
    out_shape = [jax.ShapeDtypeStruct((batch, 3 * GROUP_W), BF16),
                 jax.ShapeDtypeStruct((batch, N_HEADS * gsz), F32),
                 jax.ShapeDtypeStruct((batch, N_HEADS * gsz), F32),
                 jax.ShapeDtypeStruct((batch, N_HEADS * hsz), F32)]
    scratch = [pltpu.VMEM((batch, lw["w_gla"].shape[1]), F32), pltpu.VMEM((batch, 4 * GROUP_W), F32),
               pltpu.VMEM((batch, 4 * GROUP_W), F32), pltpu.VMEM((batch, GROUP_W), F32)]
    scratch += [pltpu.VMEM((SLOT, batch), F32) for _ in range(5)]
    scratch += [pltpu.VMEM((gsz, batch), F32), pltpu.VMEM((gsz, batch), F32)]
    return pl.pallas_call(
        _sample_mixer_kernel,
        grid=(N_HEADS,),
        in_specs=in_specs, out_specs=out_specs, out_shape=out_shape, scratch_shapes=scratch,
        compiler_params=pltpu.CompilerParams(dimension_semantics=("arbitrary",), vmem_limit_bytes=VMEM_LIMIT),
        name="sample_mixer",
    )(x2d, *consts, sg.reshape(batch, -1), sr.reshape(batch, -1), sh.reshape(batch, -1))


def _out_kernel(og_ref, or_ref, oh_ref, x_ref, wo_ref, g_ref, b_ref, xt_ref):
    mix = _dot(og_ref[...], wo_ref[0]) + _dot(or_ref[...], wo_ref[1]) + _dot(oh_ref[...], wo_ref[2])
    y = ALPHA * x_ref[...] + mix
    mu = jnp.mean(y, axis=-1, keepdims=True)
    d = y - mu
    var = jnp.mean(d * d, axis=-1, keepdims=True)
    xt_ref[...] = (d * lax.rsqrt(var + NORM_EPS) * g_ref[...] + b_ref[...]).T


def _out_call(og, orr, oh, x2d, lw):
    t = x2d.shape[0]
    tm = min(512, t)
    row = lambda i: (i, 0)
    return pl.pallas_call(
        _out_kernel,
        grid=(t // tm,),
        in_specs=[pl.BlockSpec((tm, GROUP_W), row), pl.BlockSpec((tm, GROUP_W), row),
                  pl.BlockSpec((tm, GROUP_W), row), pl.BlockSpec((tm, D_MODEL), row),
                  _const_spec(lw["w_out"].shape), _const_spec((1, D_MODEL)), _const_spec((1, D_MODEL))],
        out_specs=pl.BlockSpec((D_MODEL, tm), lambda i: (0, i)),
        out_shape=jax.ShapeDtypeStruct((D_MODEL, t), F32),
        compiler_params=pltpu.CompilerParams(dimension_semantics=("arbitrary",), vmem_limit_bytes=VMEM_LIMIT),
        name="out_proj_ln",
    )(og, orr, oh, x2d, lw["w_out"], lw["ln1_g"], lw["ln1_b"])


_CAND_ROWS = [(0, r) for r in range(8)] + [(1, r) for r in range(8)]
for _r1 in range(1, 8):
    _CAND_ROWS += [(1 + _r1, r) for r in range(PEER_TOPK // (_r1 + 1))]
_CAND_ROWS += [(9, r) for r in range(8)]


def _top16_rows(a):
    rows = []
    cur = a
    for _ in range(PEER_TOPK):
        m = jnp.max(cur, axis=0, keepdims=True)
        rows.append(m)
        cur = jnp.where(cur == m, NEG_INF, cur)
    return rows


def _rows_to_block(rows):
    sub = lax.broadcasted_iota(jnp.int32, (SUBLANE, LANE), 0)
    blk = jnp.broadcast_to(rows[0], (SUBLANE, LANE))
    for r in range(1, SUBLANE):
        blk = jnp.where(sub == r, rows[r], blk)
    return blk


def _route_tile(a1, a2):
    v1 = _top16_rows(a1)
    v2 = _top16_rows(a2)
    sub = lax.broadcasted_iota(jnp.int32, (SUBLANE, LANE), 0)
    v2a = _rows_to_block(v2[0:8])
    v2b = _rows_to_block(v2[8:16])
    v1b = _rows_to_block(v1[8:16])
    blocks = [v1[0] + v2a, v1[0] + v2b]
    for r1 in range(1, 8):
        blocks.append(jnp.where(sub < PEER_TOPK // (r1 + 1), v1[r1] + v2a, NEG_INF))
    blocks.append(v1b + v2[0])
    thr = jnp.full((1, LANE), NEG_INF, F32)
    for (bi, r) in _CAND_ROWS:
        ci = blocks[bi][r:r + 1, :]
        cnt8 = jnp.zeros((SUBLANE, LANE), F32)
        for blk in blocks:
            cnt8 = cnt8 + jnp.where(blk >= ci, 1.0, 0.0)
        cnt = jnp.sum(cnt8, axis=0, keepdims=True)
        thr = jnp.maximum(thr, jnp.where(cnt >= float(PEER_TOPK), ci, NEG_INF))
    top = v1[0] + v2[0]
    z8 = jnp.zeros((SUBLANE, LANE), F32)
    for blk in blocks:
        z8 = z8 + jnp.where(blk >= thr, jnp.exp(blk - top), 0.0)
    z = jnp.sum(z8, axis=0, keepdims=True)
    e1 = jnp.exp(a1 - v1[0])
    e2 = jnp.exp(a2 - v2[0]) / z
    return thr, e1, e2


def _peer_kernel(xt_ref, wqt_ref, sk_ref, u_ref, vt_ref, p_ref, plet_ref, gatet_ref, g2_ref, b2_ref,
                 out_ref, xbf, q_scr, s1, s2, e1, e2, thr, h_scr, w_scr, acc, *, tn, te):
    j = pl.program_id(1)
    lane_tiles = tn // LANE

    @pl.when(j == 0)
    def _route():
        xbf[...] = xt_ref[...].astype(BF16)
        q_scr[...] = _dot(wqt_ref[...], xbf[...])

        def head_body(h, carry):
            r = pl.multiple_of(h * 2 * N_KEYS, 2 * N_KEYS)
            s1[h] = _dot(sk_ref[h, 0], q_scr[pl.ds(r, N_KEYS), :].astype(BF16))
            s2[h] = _dot(sk_ref[h, 1], q_scr[pl.ds(r + N_KEYS, N_KEYS), :].astype(BF16))
            for lt in range(lane_tiles):
                lanes = slice(lt * LANE, (lt + 1) * LANE)
                t_row, e1_t, e2_t = _route_tile(s1[h, :, lanes], s2[h, :, lanes])
                thr[h, :, lanes] = jnp.broadcast_to(t_row, (SUBLANE, LANE))
                e1[h, :, lanes] = e1_t
                e2[h, :, lanes] = e2_t
            return carry

        lax.fori_loop(0, PEER_HEADS, head_body, 0)
        acc[...] = jnp.zeros_like(acc)

    h_scr[...] = _dot(u_ref[...], xbf[...])
    na = te // N_KEYS

    assert na == SUBLANE
    a0 = pl.multiple_of(j * na, SUBLANE)

    def lane_body(lt, carry):
        lanes = pl.ds(pl.multiple_of(lt * LANE, LANE), LANE)
        for ai in range(na):
            gate = jnp.zeros((N_KEYS, LANE), F32)
            for h in range(PEER_HEADS):
                s1a = s1[h, pl.ds(a0, na), lanes][ai:ai + 1, :]
                e1a = e1[h, pl.ds(a0, na), lanes][ai:ai + 1, :]
                t = s1a + s2[h, :, lanes]
                gate = gate + jnp.where(t >= thr[h, 0:1, lanes], e1a * e2[h, :, lanes], 0.0)
            rows = slice(ai * N_KEYS, (ai + 1) * N_KEYS)
            w_scr[rows, lanes] = (gate * _gelu(h_scr[rows, lanes])).astype(BF16)
        return carry

    lax.fori_loop(0, lane_tiles, lane_body, 0)
    acc[...] += _dot(vt_ref[...], w_scr[...])

    @pl.when(j == pl.num_programs(1) - 1)
    def _finish():
        for lt in range(lane_tiles):
            lanes = slice(lt * LANE, (lt + 1) * LANE)
            y = ALPHA * xt_ref[:, lanes] + acc[:, lanes]
            mu = jnp.mean(y, axis=0, keepdims=True)
            d = y - mu
            var = jnp.mean(d * d, axis=0, keepdims=True)
            yn = d * lax.rsqrt(var + NORM_EPS) * g2_ref[...] + b2_ref[...]
            emb = _dot_nt(plet_ref[...], p_ref[lt * LANE:(lt + 1) * LANE, :].astype(BF16))
            gt = _dot(gatet_ref[...], yn.astype(BF16))
            out_ref[lt * LANE:(lt + 1) * LANE, :] = (yn + emb * _sigmoid(gt)).T


def _peer_call(xt, p2d, lw):
    t = xt.shape[1]
    tn = min(512, t)
    te = 1024
    n_exp = lw["u"].shape[0]
    once = dict(pipeline_mode=pl.Buffered(1))
    in_specs = [
        pl.BlockSpec((D_MODEL, tn), lambda i, j: (0, i)),
        pl.BlockSpec(lw["wqt"].shape, lambda i, j: (0, 0), **once),
        pl.BlockSpec(lw["subkeys"].shape, lambda i, j: (0, 0, 0, 0), **once),
        pl.BlockSpec((te, D_MODEL), lambda i, j: (j, 0)),
        pl.BlockSpec((D_MODEL, te), lambda i, j: (0, j)),
        pl.BlockSpec((tn, PLE_DIM), lambda i, j: (i, 0)),
        pl.BlockSpec(lw["plet"].shape, lambda i, j: (0, 0), **once),
        pl.BlockSpec(lw["gatet"].shape, lambda i, j: (0, 0), **once),
        pl.BlockSpec((D_MODEL, LANE), lambda i, j: (0, 0), **once),
        pl.BlockSpec((D_MODEL, LANE), lambda i, j: (0, 0), **once),
    ]
    route = (PEER_HEADS, N_KEYS, tn)
    scratch = [pltpu.VMEM((D_MODEL, tn), BF16), pltpu.VMEM((PEER_HEADS * 2 * N_KEYS, tn), F32),
               pltpu.VMEM(route, F32), pltpu.VMEM(route, F32), pltpu.VMEM(route, F32), pltpu.VMEM(route, F32),
               pltpu.VMEM((PEER_HEADS, SUBLANE, tn), F32), pltpu.VMEM((te, tn), F32), pltpu.VMEM((te, tn), BF16),
               pltpu.VMEM((D_MODEL, tn), F32)]
    return pl.pallas_call(
        functools.partial(_peer_kernel, tn=tn, te=te),
        grid=(t // tn, n_exp // te),
        in_specs=in_specs,
        out_specs=pl.BlockSpec((tn, D_MODEL), lambda i, j: (i, 0)),
        out_shape=jax.ShapeDtypeStruct((t, D_MODEL), F32),
        scratch_shapes=scratch,
        compiler_params=pltpu.CompilerParams(dimension_semantics=("arbitrary", "arbitrary"),
                                             vmem_limit_bytes=VMEM_LIMIT),
        name="peer_ffn_ln_ple",
    )(xt, lw["wqt"], lw["subkeys"], lw["u"], lw["vt"], p2d, lw["plet"], lw["gatet"], lw["ln2_g"], lw["ln2_b"])


def _head_slots(w, d):
    r = w.shape[0]
    return jnp.pad(w.reshape(r, N_HEADS, d), ((0, 0), (0, 0), (0, SLOT - d))).reshape(r, GROUP_W)


def _rope_slots(w, d):
    r = w.shape[0]
    half = d // 2
    w = w.reshape(r, N_HEADS, 2, half)
    return jnp.pad(w, ((0, 0), (0, 0), (0, 0), (0, SLOT // 2 - half))).reshape(r, GROUP_W)


def _rope_tables(pos):
    half = RET_DK // 2
    inv = 1.0 / (ROPE_BASE ** (jnp.arange(0, RET_DK, 2, dtype=F32) / RET_DK))
    ang = pos[:, None] * inv[None, :]
    pad = ((0, 0), (0, SLOT // 2 - half))
    cos = jnp.pad(jnp.cos(ang), pad)
    sin = jnp.pad(jnp.sin(ang), pad)
    return jnp.concatenate([cos, cos], axis=1), jnp.concatenate([-sin, sin], axis=1)


def _retention_constants():
    log_gamma = jnp.log1p(-jnp.exp2(-5.0 - jnp.arange(N_HEADS, dtype=F32)))
    i = jnp.arange(CHUNK, dtype=F32)
    diff = i[:, None] - i[None, :]
    lg = log_gamma[:, None, None]
    dmat = jnp.where(diff >= 0, jnp.exp(jnp.where(diff >= 0, diff, 0.0) * lg), 0.0)
    qdec = jnp.broadcast_to(jnp.exp((i[None, :, None] + 1.0) * lg), (N_HEADS, CHUNK, SLOT))
    kdec = jnp.broadcast_to(jnp.exp((CHUNK - 1.0 - i[None, :, None]) * lg), (N_HEADS, CHUNK, SLOT))
    sdec = jnp.broadcast_to(jnp.exp(CHUNK * lg), (N_HEADS, 1, SLOT))
    gamma = jnp.broadcast_to(jnp.exp(lg), (N_HEADS, 1, SLOT))
    return dmat, qdec, kdec, sdec, gamma


def _layer_weights(i, lb, w_in, gla_w_gate, gla_b_gate, gla_norm, ret_norm, hgrn_norm, w_out,
                   ln1_g, ln1_b, ln2_g, ln2_b, peer_w_q, peer_subkeys, peer_u, peer_v, ple_proj, ple_gate):
    sizes = (N_HEADS * GLA_DK, N_HEADS * GLA_DK, N_HEADS * GLA_DV, N_HEADS * GLA_DV, GLA_LOWRANK,
             N_HEADS * RET_DK, N_HEADS * RET_DK, N_HEADS * RET_DV, N_HEADS * RET_DV,
             N_HEADS * HGRN_DK, N_HEADS * HGRN_DK, N_HEADS * HGRN_DV, N_HEADS * HGRN_DV)
    offs = [int(c) for c in np.cumsum(sizes)[:-1]]
    gq, gk, gv, gg, glr, rq, rk, rv, rg, hq, hf, hi, hg = jnp.split(w_in[i], offs, axis=1)
    lw = {}
    lw["w_gla"] = jnp.concatenate(
        [_head_slots(gq, GLA_DK), _head_slots(gk, GLA_DK), _head_slots(gv, GLA_DV), _head_slots(gg, GLA_DV),
         jnp.pad(glr, ((0, 0), (0, SLOT - GLA_LOWRANK)))], axis=1).astype(BF16)
    lw["w_ret"] = jnp.concatenate(
        [_rope_slots(rq, RET_DK), _rope_slots(rk, RET_DK), _head_slots(rv, RET_DV), _head_slots(rg, RET_DV)],
        axis=1).astype(BF16)
    lw["w_hgrn"] = jnp.concatenate([_head_slots(w, HGRN_DK) for w in (hq, hf, hi, hg)], axis=1).astype(BF16)
    lw["wgate"] = jnp.pad(_head_slots(gla_w_gate[i], GLA_DK), ((0, SLOT - GLA_LOWRANK), (0, 0)))
    lw["bgate"] = _head_slots(gla_b_gate[i][None, :], GLA_DK)
    lw["gla_norm"] = jnp.pad(gla_norm[i], ((0, 0), (0, SLOT - GLA_DV)))
    lw["ret_norm"] = jnp.pad(ret_norm[i], ((0, 0), (0, SLOT - RET_DV)))
    lw["hgrn_norm"] = jnp.pad(hgrn_norm[i], ((0, 0), (0, SLOT - HGRN_DV)))
    lbi = lb[i].reshape(1, N_HEADS, HGRN_DK)
    pad = ((0, 0), (0, 0), (0, SLOT - HGRN_DK))
    lw["loglb"] = jnp.pad(jnp.log(lbi), pad, constant_values=-1.0).reshape(1, GROUP_W)
    lw["l1mlb"] = jnp.pad(jnp.log1p(-lbi), pad, constant_values=-1.0).reshape(1, GROUP_W)
    wo = w_out[i]
    g_rows, r_rows = N_HEADS * GLA_DV, N_HEADS * RET_DV
    lw["w_out"] = jnp.stack([
        _head_slots(wo[:g_rows].T, GLA_DV).T, _head_slots(wo[g_rows:g_rows + r_rows].T, RET_DV).T,
        _head_slots(wo[g_rows + r_rows:].T, HGRN_DV).T]).astype(BF16)
    lw["ln1_g"], lw["ln1_b"] = ln1_g[i][None, :], ln1_b[i][None, :]
    lw["ln2_g"] = jnp.broadcast_to(ln2_g[i][:, None], (D_MODEL, LANE))
    lw["ln2_b"] = jnp.broadcast_to(ln2_b[i][:, None], (D_MODEL, LANE))
    lw["wqt"] = peer_w_q[i].T.astype(BF16)
    lw["subkeys"] = peer_subkeys[i].astype(BF16)
    lw["u"] = peer_u[i].astype(BF16)
    lw["vt"] = peer_v[i].T.astype(BF16)
    lw["plet"] = ple_proj[i].T.astype(BF16)
    lw["gatet"] = ple_gate[i].T.astype(BF16)
    return lw


def _unslot_state(st, dk, dv, rope=False):
    if rope:
        half = dk // 2
        st = jnp.concatenate([st[..., :half], st[..., SLOT // 2:SLOT // 2 + half]], axis=-1)
    return jnp.swapaxes(st[:, :, :dv, :dk], 2, 3)


def kernel(x_prompt, x_sample, p_prompt, p_sample, state_gla, state_ret, state_hgrn, w_in, gla_w_gate,
           gla_b_gate, gla_norm, ret_norm, hgrn_lb_logits, hgrn_norm, w_out, ln1_g, ln1_b, ln2_g, ln2_b,
           peer_w_q, peer_subkeys, peer_u, peer_v, ple_proj, ple_gate):
    bp, lp, _ = x_prompt.shape
    bs = x_sample.shape[0]
    assert x_sample.shape[1] == 1 and lp % CHUNK == 0

    lb = jnp.cumsum(jax.nn.softmax(hgrn_lb_logits.astype(F32), axis=0), axis=0)
    lb = lb - lb[0:1]
    mall_np, masks_np = _chunk_constants()
    mall, masks = jnp.asarray(mall_np), jnp.asarray(masks_np)
    dmat, qdec, kdec, sdec, gamma = _retention_constants()
    cos_p, sin_p = _rope_tables(jnp.arange(lp, dtype=F32))
    cos_s, sin_s = _rope_tables(PAST_LEN + jnp.arange(1, dtype=F32))

    xp = x_prompt.reshape(bp * lp, D_MODEL)
    xs = x_sample.reshape(bs, D_MODEL)
    gla_p, ret_p, hgrn_p, gla_s, ret_s, hgrn_s = [], [], [], [], [], []
    for i in range(DEPTH):
        lw = _layer_weights(i, lb, w_in, gla_w_gate, gla_b_gate, gla_norm, ret_norm, hgrn_norm, w_out,
                            ln1_g, ln1_b, ln2_g, ln2_b, peer_w_q, peer_subkeys, peer_u, peer_v,
                            ple_proj, ple_gate)
        lw.update(cos_s=cos_s, sin_s=sin_s, gamma=gamma)

        og, sg = _prompt_mixer_call(
            _gla_prompt_kernel, xp, bp, lp, lw["w_gla"],
            [lw["wgate"], lw["bgate"], lw["gla_norm"], mall, masks],
            [lw["w_gla"].shape[1], GROUP_W], "gla_prompt")
        orr, sr = _prompt_mixer_call(
            _ret_prompt_kernel, xp, bp, lp, lw["w_ret"],
            [(cos_p, None), (sin_p, None), lw["ret_norm"], dmat, qdec, kdec, sdec],
            [4 * GROUP_W], "ret_prompt")
        oh, sh = _prompt_mixer_call(
            _hgrn_prompt_kernel, xp, bp, lp, lw["w_hgrn"],
            [lw["loglb"], lw["l1mlb"], lw["hgrn_norm"], mall, masks],
            [4 * GROUP_W], "hgrn_prompt")
        xp = _peer_call(_out_call(og, orr, oh, xp, lw), p_prompt[i].reshape(bp * lp, PLE_DIM), lw)
        gla_p.append(_unslot_state(sg, GLA_DK, GLA_DV))
        ret_p.append(_unslot_state(sr, RET_DK, RET_DV, rope=True))
        hgrn_p.append(_unslot_state(sh, HGRN_DK, HGRN_DV))

        o_s, nsg, nsr, nsh = _sample_mixer_call(xs, lw, state_gla[i], state_ret[i], state_hgrn[i])
        xs = _peer_call(
            _out_call(o_s[:, :GROUP_W], o_s[:, GROUP_W:2 * GROUP_W], o_s[:, 2 * GROUP_W:], xs, lw),
            p_sample[i].reshape(bs, PLE_DIM), lw)
        gla_s.append(nsg.reshape(bs, N_HEADS, GLA_DK, GLA_DV))
        ret_s.append(nsr.reshape(bs, N_HEADS, RET_DK, RET_DV))
        hgrn_s.append(nsh.reshape(bs, N_HEADS, HGRN_DK, HGRN_DV))

    return (xp.reshape(bp, lp, D_MODEL), xs.reshape(bs, 1, D_MODEL),
            jnp.stack(gla_p), jnp.stack(ret_p), jnp.stack(hgrn_p),
            jnp.stack(gla_s), jnp.stack(ret_s), jnp.stack(hgrn_s))
```

```python
import functools

import numpy as np
import jax
import jax.numpy as jnp
from jax import lax
from jax.experimental import pallas as pl
from jax.experimental.pallas import tpu as pltpu

F32 = jnp.float32
BF16 = jnp.bfloat16
HIGHEST = lax.Precision.HIGHEST

D_MODEL = 1024
DEPTH = 2
PAST_LEN = 16384
N_HEADS = 4
GLA_DK, GLA_DV = 48, 96
RET_DK, RET_DV = 48, 96
HGRN_DK, HGRN_DV = 64, 64
GLA_LOWRANK = 16
GLA_TAU = 16.0
ROPE_BASE = 10000.0
CHUNK = 64
PEER_HEADS = 8
N_KEYS = 128
PEER_TOPK = 16
PLE_DIM = 256
ALPHA = (2 * DEPTH) ** 0.25
NORM_EPS = 1e-5

LANE = 128
SUBLANE = 8
SLOT = LANE
GROUP_W = N_HEADS * SLOT
VMEM_LIMIT = 56 * 1024 * 1024
NEG_INF = float("-inf")


def _dot(a, b, precision=None):
    return jnp.dot(a, b, preferred_element_type=F32, precision=precision)


def _dot_nt(a, b):
    return lax.dot_general(a, b, (((1,), (1,)), ((), ())), preferred_element_type=F32)


def _dot_tn(a, b):
    return lax.dot_general(a, b, (((0,), (0,)), ((), ())), preferred_element_type=F32)


def _sigmoid(x):
    return jax.nn.sigmoid(x)


def _silu(x):
    return x * _sigmoid(x)


def _log_sigmoid(x):
    return jnp.minimum(x, 0.0) - jnp.log1p(jnp.exp(-jnp.abs(x)))


def _logaddexp(a, c):
    amax = jnp.maximum(a, c)
    delta = a - c
    return jnp.where(jnp.isnan(delta), a + c, amax + jnp.log1p(jnp.exp(-jnp.abs(delta))))


def _gelu(x):
    return 0.5 * x * (1.0 + lax.erf(x * np.float32(0.7071067811865476)))


def _lane_mask(n):
    return (lax.broadcasted_iota(jnp.int32, (1, LANE), 1) < n).astype(F32)


def _rms_head_norm(o, g_row, dv):
    ms = jnp.sum(o * o, axis=-1, keepdims=True) * (1.0 / dv)
    return o * lax.rsqrt(ms + NORM_EPS) * g_row


def _group_head_norm(o, g_row, dv):
    mask = _lane_mask(dv)
    mu = jnp.sum(o, axis=-1, keepdims=True) * (1.0 / dv)
    d = (o - mu) * mask
    var = jnp.sum(d * d, axis=-1, keepdims=True) * (1.0 / dv)
    return d * lax.rsqrt(var + NORM_EPS) * g_row


def _chunk_constants():
    i = np.arange(CHUNK)[:, None]
    t = np.arange(CHUNK)[None, :]
    mats = [t <= i, t > i]
    masks = [i == t]
    half = CHUNK // 2
    while half >= 1:
        blk = i // (2 * half)
        ref = blk * 2 * half + half - 1
        second = (i % (2 * half)) >= half
        mats.append(np.where(second, (t > ref) & (t <= i), (t > i) & (t <= ref)))
        masks.append(second & ((t % (2 * half)) < half) & (blk == t // (2 * half)))
        half //= 2
    return (np.concatenate(mats, 0).astype(np.float32), np.stack(masks).astype(np.float32))


N_LEVELS = 6


def _vector_decay_chunk(q, k, v, g, st, mall, masks_ref):
    args = _dot(mall, g, precision=HIGHEST)
    b = args[0:CHUNK]
    rem = args[CHUNK:2 * CHUNK]
    kb = k.astype(BF16)
    sc = masks_ref[0] * _dot_nt(q.astype(BF16), kb)
    for l in range(N_LEVELS):
        f = jnp.exp(args[(2 + l) * CHUNK:(3 + l) * CHUNK])
        sc = sc + masks_ref[1 + l] * _dot_nt((q * f).astype(BF16), (k * f).astype(BF16))
    vb = v.astype(BF16)
    o = _dot(sc.astype(BF16), vb) + _dot_nt((q * jnp.exp(b)).astype(BF16), st.astype(BF16))
    st_new = st * jnp.exp(b[CHUNK - 1:CHUNK]) + _dot_tn(vb, (k * jnp.exp(rem)).astype(BF16))
    return o, st_new


def _gla_prompt_kernel(x_ref, w_ref, wg_ref, bg_ref, nrm_ref, mall_ref, masks_ref,
                       o_ref, st_ref, z_scr, la_scr, *, seg):
    @pl.when(pl.program_id(1) == 0)
    def _():
        st_ref[...] = jnp.zeros_like(st_ref)

    z_scr[...] = _dot(x_ref[...].astype(BF16), w_ref[...])
    pre = _dot(z_scr[:, 4 * GROUP_W:4 * GROUP_W + SLOT], wg_ref[...], precision=HIGHEST) + bg_ref[...]
    la_scr[...] = _log_sigmoid(pre) * (1.0 / GLA_TAU)
    mall = mall_ref[...]

    def body(c, carry):
        r0 = pl.multiple_of(c * CHUNK, CHUNK)
        rows = pl.ds(r0, CHUNK)
        for h in range(N_HEADS):
            q = z_scr[rows, h * SLOT:(h + 1) * SLOT] * (GLA_DK ** -0.5)
            k = z_scr[rows, GROUP_W + h * SLOT:GROUP_W + (h + 1) * SLOT]
            v = z_scr[rows, 2 * GROUP_W + h * SLOT:2 * GROUP_W + (h + 1) * SLOT]
            gate = z_scr[rows, 3 * GROUP_W + h * SLOT:3 * GROUP_W + (h + 1) * SLOT]
            g = la_scr[rows, h * SLOT:(h + 1) * SLOT]
            o, st_new = _vector_decay_chunk(q, k, v, g, st_ref[0, h], mall, masks_ref)
            st_ref[0, h] = st_new
            on = _rms_head_norm(o, nrm_ref[h:h + 1, :], GLA_DV)
            o_ref[rows, h * SLOT:(h + 1) * SLOT] = (on * _silu(gate)).astype(BF16)
        return carry

    lax.fori_loop(0, seg // CHUNK, body, 0)


def _hgrn_prompt_kernel(x_ref, w_ref, loglb_ref, l1mlb_ref, nrm_ref, mall_ref, masks_ref,
                        o_ref, st_ref, z_scr, *, seg):
    @pl.when(pl.program_id(1) == 0)
    def _():
        st_ref[...] = jnp.zeros_like(st_ref)

    z_scr[...] = _dot(x_ref[...].astype(BF16), w_ref[...])
    mall = mall_ref[...]
    kmask = _lane_mask(HGRN_DK)

    def body(c, carry):
        r0 = pl.multiple_of(c * CHUNK, CHUNK)
        rows = pl.ds(r0, CHUNK)
        for h in range(N_HEADS):
            sl = slice(h * SLOT, (h + 1) * SLOT)
            q = _silu(z_scr[rows, h * SLOT:(h + 1) * SLOT])
            hf = z_scr[rows, GROUP_W + h * SLOT:GROUP_W + (h + 1) * SLOT]
            v = z_scr[rows, 2 * GROUP_W + h * SLOT:2 * GROUP_W + (h + 1) * SLOT]
            gate = z_scr[rows, 3 * GROUP_W + h * SLOT:3 * GROUP_W + (h + 1) * SLOT]
            log_f = _logaddexp(loglb_ref[:, sl], l1mlb_ref[:, sl] + _log_sigmoid(hf))
            k = (1.0 - jnp.exp(log_f)) * kmask
            o, st_new = _vector_decay_chunk(q, k, v, log_f, st_ref[0, h], mall, masks_ref)
            st_ref[0, h] = st_new
            on = _rms_head_norm(o, nrm_ref[h:h + 1, :], HGRN_DV)
            o_ref[rows, sl] = (on * _silu(gate)).astype(BF16)
        return carry

    lax.fori_loop(0, seg // CHUNK, body, 0)


def _rotate(t, cs, sn):
    return t * cs + pltpu.roll(t, SLOT // 2, 1) * sn


def _ret_prompt_kernel(x_ref, w_ref, cos_ref, sin_ref, nrm_ref, dmat_ref, qdec_ref, kdec_ref, sdec_ref,
                       o_ref, st_ref, z_scr, *, seg):
    @pl.when(pl.program_id(1) == 0)
    def _():
        st_ref[...] = jnp.zeros_like(st_ref)

    z_scr[...] = _dot(x_ref[...].astype(BF16), w_ref[...])

    def body(c, carry):
        r0 = pl.multiple_of(c * CHUNK, CHUNK)
        rows = pl.ds(r0, CHUNK)
        cs = cos_ref[rows, :]
        sn = sin_ref[rows, :]
        for h in range(N_HEADS):
            sl = slice(h * SLOT, (h + 1) * SLOT)
            q = _rotate(z_scr[rows, h * SLOT:(h + 1) * SLOT], cs, sn)
            k = _rotate(z_scr[rows, GROUP_W + h * SLOT:GROUP_W + (h + 1) * SLOT], cs, sn) * (RET_DK ** -0.5)
            v = z_scr[rows, 2 * GROUP_W + h * SLOT:2 * GROUP_W + (h + 1) * SLOT]
            gate = z_scr[rows, 3 * GROUP_W + h * SLOT:3 * GROUP_W + (h + 1) * SLOT]
            st = st_ref[0, h]
            vb = v.astype(BF16)
            sc = dmat_ref[h] * _dot_nt(q.astype(BF16), k.astype(BF16))
            o = _dot(sc.astype(BF16), vb) + _dot_nt((q * qdec_ref[h]).astype(BF16), st.astype(BF16))
            st_ref[0, h] = st * sdec_ref[h] + _dot_tn(vb, (k * kdec_ref[h]).astype(BF16))
            on = _group_head_norm(o, nrm_ref[h:h + 1, :], RET_DV)
            o_ref[rows, sl] = (on * _silu(gate)).astype(BF16)
        return carry

    lax.fori_loop(0, seg // CHUNK, body, 0)


def _const_spec(shape):
    nd = len(shape)
    return pl.BlockSpec(shape, lambda *_: (0,) * nd)


def _prompt_mixer_call(kernel, x2d, batch, seq, w, extras, scratch_widths, name):
    seg = min(512, seq)
    nseg = seq // seg
    in_specs = [pl.BlockSpec((seg, D_MODEL), lambda b, s: (b * nseg + s, 0)), _const_spec(w.shape)]
    args = [x2d, w]
    for e in extras:
        if isinstance(e, tuple):
            arr, _ = e
            in_specs.append(pl.BlockSpec((seg, arr.shape[1]), lambda b, s: (s, 0)))
            args.append(arr)
        else:
            in_specs.append(_const_spec(e.shape))
            args.append(e)
    return pl.pallas_call(
        functools.partial(kernel, seg=seg),
        grid=(batch, nseg),
        in_specs=in_specs,
        out_specs=[pl.BlockSpec((seg, GROUP_W), lambda b, s: (b * nseg + s, 0)),
                   pl.BlockSpec((1, N_HEADS, SLOT, SLOT), lambda b, s: (b, 0, 0, 0))],
        out_shape=[jax.ShapeDtypeStruct((batch * seq, GROUP_W), BF16),
                   jax.ShapeDtypeStruct((batch, N_HEADS, SLOT, SLOT), F32)],
        scratch_shapes=[pltpu.VMEM((seg, wd), F32) for wd in scratch_widths],
        compiler_params=pltpu.CompilerParams(dimension_semantics=("arbitrary", "arbitrary"),
                                             vmem_limit_bytes=VMEM_LIMIT),
        name=name,
    )(*args)


def _sample_step(q, k, eg, v, s_ref, ns_ref, tq, tk, te, tv, to, s_t, sn_t, dk, dv, row_of):
    tq[...] = q.T
    tk[...] = k.T
    te[...] = eg.T
    tv[...] = v.T
    s_t[0:dk * dv, :] = s_ref[...].T
    vt = tv[0:dv, :]

    def body(kk, oacc):
        kr = row_of(kk)
        r = pl.multiple_of(kk * dv, SUBLANE)
        sn = s_t[pl.ds(r, dv), :] * te[pl.ds(kr, 1), :] + tk[pl.ds(kr, 1), :] * vt
        sn_t[pl.ds(r, dv), :] = sn
        return oacc + tq[pl.ds(kr, 1), :] * sn

    o_t = lax.fori_loop(0, dk, body, jnp.zeros((dv, q.shape[0]), F32))
    ns_ref[...] = sn_t[0:dk * dv, :].T
    to[...] = jnp.zeros_like(to)
    to[0:dv, :] = o_t
    return to[...].T


def _sample_mixer_kernel(x_ref, wg_ref, wr_ref, wh_ref, wgate_ref, bgate_ref, gn_ref, rn_ref, hn_ref,
                         cos_ref, sin_ref, gam_ref, loglb_ref, l1mlb_ref, sg_ref, sr_ref, sh_ref,
                         o_ref, nsg_ref, nsr_ref, nsh_ref,
                         zg, zr, zh, la, tq, tk, te, tv, to, s_t, sn_t):
    h = pl.program_id(0)

    @pl.when(h == 0)
    def _():
        xb = x_ref[...].astype(BF16)
        zg[...] = _dot(xb, wg_ref[...])
        zr[...] = _dot(xb, wr_ref[...])
        zh[...] = _dot(xb, wh_ref[...])
        pre = _dot(zg[:, 4 * GROUP_W:4 * GROUP_W + SLOT], wgate_ref[...], precision=HIGHEST) + bgate_ref[...]
        la[...] = _log_sigmoid(pre) * (1.0 / GLA_TAU)

    off = pl.multiple_of(h * SLOT, SLOT)
    sl = pl.ds(off, SLOT)
    tr = (tq, tk, te, tv, to, s_t, sn_t)
    batch = x_ref.shape[0]

    q = zg[:, sl] * (GLA_DK ** -0.5)
    k = zg[:, pl.ds(GROUP_W + off, SLOT)]
    v = zg[:, pl.ds(2 * GROUP_W + off, SLOT)]
    gate = zg[:, pl.ds(3 * GROUP_W + off, SLOT)]
    o = _sample_step(q, k, jnp.exp(la[:, sl]), v, sg_ref, nsg_ref, *tr, GLA_DK, GLA_DV, lambda kk: kk)
    o_ref[:, sl] = (_rms_head_norm(o, gn_ref[pl.ds(h, 1), :], GLA_DV) * _silu(gate)).astype(BF16)

    cs = cos_ref[...]
    sn = sin_ref[...]
    q = _rotate(zr[:, sl], cs, sn)
    k = _rotate(zr[:, pl.ds(GROUP_W + off, SLOT)], cs, sn) * (RET_DK ** -0.5)
    v = zr[:, pl.ds(2 * GROUP_W + off, SLOT)]
    gate = zr[:, pl.ds(3 * GROUP_W + off, SLOT)]
    eg = jnp.broadcast_to(gam_ref[h], (batch, SLOT))
    half = RET_DK // 2
    o = _sample_step(q, k, eg, v, sr_ref, nsr_ref, *tr, RET_DK, RET_DV,
                     lambda kk: kk + jnp.where(kk >= half, SLOT // 2 - half, 0))
    o_ref[:, pl.ds(GROUP_W + off, SLOT)] = (
        _group_head_norm(o, rn_ref[pl.ds(h, 1), :], RET_DV) * _silu(gate)).astype(BF16)

    q = _silu(zh[:, sl])
    hf = zh[:, pl.ds(GROUP_W + off, SLOT)]
    v = zh[:, pl.ds(2 * GROUP_W + off, SLOT)]
    gate = zh[:, pl.ds(3 * GROUP_W + off, SLOT)]
    log_f = _logaddexp(loglb_ref[:, sl], l1mlb_ref[:, sl] + _log_sigmoid(hf))
    f = jnp.exp(log_f)
    o = _sample_step(q, (1.0 - f) * _lane_mask(HGRN_DK), f, v, sh_ref, nsh_ref, *tr,
                     HGRN_DK, HGRN_DV, lambda kk: kk)
    o_ref[:, pl.ds(2 * GROUP_W + off, SLOT)] = (
        _rms_head_norm(o, hn_ref[pl.ds(h, 1), :], HGRN_DV) * _silu(gate)).astype(BF16)


def _sample_mixer_call(x2d, lw, sg, sr, sh):
    batch = x2d.shape[0]
    gsz, hsz = GLA_DK * GLA_DV, HGRN_DK * HGRN_DV
    consts = [lw["w_gla"], lw["w_ret"], lw["w_hgrn"], lw["wgate"], lw["bgate"], lw["gla_norm"], lw["ret_norm"],
              lw["hgrn_norm"], lw["cos_s"], lw["sin_s"], lw["gamma"], lw["loglb"], lw["l1mlb"]]
    in_specs = ([_const_spec(x2d.shape)] + [_const_spec(c.shape) for c in consts]
                + [pl.BlockSpec((batch, gsz), lambda h: (0, h)),
                   pl.BlockSpec((batch, gsz), lambda h: (0, h)),
                   pl.BlockSpec((batch, hsz), lambda h: (0, h))])
    out_specs = [_const_spec((batch, 3 * GROUP_W)),
                 pl.BlockSpec((batch, gsz), lambda h: (0, h)),
                 pl.BlockSpec((batch, gsz), lambda h: (0, h)),
                 pl.BlockSpec((batch, hsz), lambda h: (0, h))]
    out_shape = [jax.ShapeDtypeStruct((batch, 3 * GROUP_W), BF16),
                 jax.ShapeDtypeStruct((batch, N_HEADS * gsz), F32),
                 jax.ShapeDtypeStruct((batch, N_HEADS * gsz), F32),
                 jax.ShapeDtypeStruct((batch, N_HEADS * hsz), F32)]
    scratch = [pltpu.VMEM((batch, lw["w_gla"].shape[1]), F32), pltpu.VMEM((batch, 4 * GROUP_W), F32),
               pltpu.VMEM((batch, 4 * GROUP_W), F32), pltpu.VMEM((batch, GROUP_W), F32)]
    scratch += [pltpu.VMEM((SLOT, batch), F32) for _ in range(5)]
    scratch += [pltpu.VMEM((gsz, batch), F32), pltpu.VMEM((gsz, batch), F32)]
    return pl.pallas_call(
        _sample_mixer_kernel,
        grid=(N_HEADS,),
        in_specs=in_specs, out_specs=out_specs, out_shape=out_shape, scratch_shapes=scratch,
        compiler_params=pltpu.CompilerParams(dimension_semantics=("arbitrary",), vmem_limit_bytes=VMEM_LIMIT),
        name="sample_mixer",
    )(x2d, *consts, sg.reshape(batch, -1), sr.reshape(batch, -1), sh.reshape(batch, -1))


def _out_kernel(og_ref, or_ref, oh_ref, x_ref, wo_ref, g_ref, b_ref, xt_ref):
    mix = _dot(og_ref[...], wo_ref[0]) + _dot(or_ref[...], wo_ref[1]) + _dot(oh_ref[...], wo_ref[2])
    y = ALPHA * x_ref[...] + mix
    mu = jnp.mean(y, axis=-1, keepdims=True)
    d = y - mu
    var = jnp.mean(d * d, axis=-1, keepdims=True)
    xt_ref[...] = (d * lax.rsqrt(var + NORM_EPS) * g_ref[...] + b_ref[...]).T


def _out_call(og, orr, oh, x2d, lw):
    t = x2d.shape[0]
    tm = min(512, t)
    row = lambda i: (i, 0)
    return pl.pallas_call(
        _out_kernel,
        grid=(t // tm,),
        in_specs=[pl.BlockSpec((tm, GROUP_W), row), pl.BlockSpec((tm, GROUP_W), row),
                  pl.BlockSpec((tm, GROUP_W), row), pl.BlockSpec((tm, D_MODEL), row),
                  _const_spec(lw["w_out"].shape), _const_spec((1, D_MODEL)), _const_spec((1, D_MODEL))],
        out_specs=pl.BlockSpec((D_MODEL, tm), lambda i: (0, i)),
        out_shape=jax.ShapeDtypeStruct((D_MODEL, t), F32),
        compiler_params=pltpu.CompilerParams(dimension_semantics=("arbitrary",), vmem_limit_bytes=VMEM_LIMIT),
        name="out_proj_ln",
    )(og, orr, oh, x2d, lw["w_out"], lw["ln1_g"], lw["ln1_b"])


_CAND_ROWS = [(0, r) for r in range(8)] + [(1, r) for r in range(8)]
for _r1 in range(1, 8):
    _CAND_ROWS += [(1 + _r1, r) for r in range(PEER_TOPK // (_r1 + 1))]
_CAND_ROWS += [(9, r) for r in range(8)]


def _top16_rows(a):
    rows = []
    cur = a
    for _ in range(PEER_TOPK):
        m = jnp.max(cur, axis=0, keepdims=True)
        rows.append(m)
        cur = jnp.where(cur == m, NEG_INF, cur)
    return rows


def _rows_to_block(rows):
    sub = lax.broadcasted_iota(jnp.int32, (SUBLANE, LANE), 0)
    blk = jnp.broadcast_to(rows[0], (SUBLANE, LANE))
    for r in range(1, SUBLANE):
        blk = jnp.where(sub == r, rows[r], blk)
    return blk


def _route_tile(a1, a2):
    v1 = _top16_rows(a1)
    v2 = _top16_rows(a2)
    sub = lax.broadcasted_iota(jnp.int32, (SUBLANE, LANE), 0)
    v2a = _rows_to_block(v2[0:8])
    v2b = _rows_to_block(v2[8:16])
    v1b = _rows_to_block(v1[8:16])
    blocks = [v1[0] + v2a, v1[0] + v2b]
    for r1 in range(1, 8):
        blocks.append(jnp.where(sub < PEER_TOPK // (r1 + 1), v1[r1] + v2a, NEG_INF))
    blocks.append(v1b + v2[0])
    thr = jnp.full((1, LANE), NEG_INF, F32)
    for (bi, r) in _CAND_ROWS:
        ci = blocks[bi][r:r + 1, :]
        cnt8 = jnp.zeros((SUBLANE, LANE), F32)
        for blk in blocks:
            cnt8 = cnt8 + jnp.where(blk >= ci, 1.0, 0.0)
        cnt = jnp.sum(cnt8, axis=0, keepdims=True)
        thr = jnp.maximum(thr, jnp.where(cnt >= float(PEER_TOPK), ci, NEG_INF))
    top = v1[0] + v2[0]
    z8 = jnp.zeros((SUBLANE, LANE), F32)
    for blk in blocks:
        z8 = z8 + jnp.where(blk >= thr, jnp.exp(blk - top), 0.0)
    z = jnp.sum(z8, axis=0, keepdims=True)
    e1 = jnp.exp(a1 - v1[0])
    e2 = jnp.exp(a2 - v2[0]) / z
    return thr, e1, e2


def _peer_kernel(xt_ref, wqt_ref, sk_ref, u0_ref, u_ref, vt_ref, p_ref, plet_ref, gatet_ref, g2_ref, b2_ref,
                 out_ref, xbf, q_scr, s1, s2, e1, e2, thr, h_scr, w_scr, acc, *, tn, te, nj):
    j = pl.program_id(1)
    lane_tiles = tn // LANE
    cur = j % 2
    nxt = (j + 1) % 2

    def put_h(slot, hval):
        for lt in range(lane_tiles):
            h_scr[slot, lt] = hval[:, lt * LANE:(lt + 1) * LANE]

    @pl.when(j == 0)
    def _route():
        xbf[...] = xt_ref[...].astype(BF16)
        q_scr[...] = _dot(wqt_ref[...], xbf[...])

        def head_body(h, carry):
            r = pl.multiple_of(h * 2 * N_KEYS, 2 * N_KEYS)
            s1h = _dot(sk_ref[h, 0], q_scr[pl.ds(r, N_KEYS), :].astype(BF16))
            s2h = _dot(sk_ref[h, 1], q_scr[pl.ds(r + N_KEYS, N_KEYS), :].astype(BF16))
            for lt in range(lane_tiles):
                a1 = s1h[:, lt * LANE:(lt + 1) * LANE]
                a2 = s2h[:, lt * LANE:(lt + 1) * LANE]
                t_row, e1_t, e2_t = _route_tile(a1, a2)
                thr[lt, h] = jnp.broadcast_to(t_row, (SUBLANE, LANE))
                s1[lt, h] = a1
                s2[lt, h] = a2
                e1[lt, h] = e1_t
                e2[lt, h] = e2_t
            return carry

        lax.fori_loop(0, PEER_HEADS, head_body, 0)
        acc[...] = jnp.zeros_like(acc)
        w_scr[1] = jnp.zeros(w_scr.shape[1:], BF16)
        put_h(0, _dot(u0_ref[...], xbf[...]))

    put_h(nxt, _dot(u_ref[...], xbf[...]))
    w_prev = jnp.concatenate([w_scr[nxt, lt] for lt in range(lane_tiles)], axis=1)
    acc[...] += _dot(vt_ref[...], w_prev)

    na = te // N_KEYS
    assert na == SUBLANE
    a0 = pl.multiple_of(jnp.minimum(j, nj - 1) * na, SUBLANE)
    for lt in range(lane_tiles):
        s1blk = [s1[lt, h, pl.ds(a0, na), :] for h in range(PEER_HEADS)]
        e1blk = [e1[lt, h, pl.ds(a0, na), :] for h in range(PEER_HEADS)]
        for ai in range(na):
            gate = jnp.zeros((N_KEYS, LANE), F32)
            for h in range(PEER_HEADS):
                t = s1blk[h][ai:ai + 1, :] + s2[lt, h]
                gate = gate + jnp.where(t >= thr[lt, h, 0:1, :], e1blk[h][ai:ai + 1, :] * e2[lt, h], 0.0)
            rows = slice(ai * N_KEYS, (ai + 1) * N_KEYS)
            w_scr[cur, lt, rows, :] = (gate * _gelu(h_scr[cur, lt, rows, :])).astype(BF16)

    @pl.when(j == nj)
    def _finish():
        for lt in range(lane_tiles):
            lanes = slice(lt * LANE, (lt + 1) * LANE)
            y = ALPHA * xt_ref[:, lanes] + acc[:, lanes]
            mu = jnp.mean(y, axis=0, keepdims=True)
            d = y - mu
            var = jnp.mean(d * d, axis=0, keepdims=True)
            yn = d * lax.rsqrt(var + NORM_EPS) * g2_ref[...] + b2_ref[...]
            emb = _dot_nt(plet_ref[...], p_ref[lt * LANE:(lt + 1) * LANE, :].astype(BF16))
            gt = _dot(gatet_ref[...], yn.astype(BF16))
            out_ref[lt * LANE:(lt + 1) * LANE, :] = (yn + emb * _sigmoid(gt)).T


def _peer_call(xt, p2d, lw):
    t = xt.shape[1]
    tn = min(512, t)
    te = 1024
    n_exp = lw["u"].shape[0]
    nj = n_exp // te
    lane_tiles = tn // LANE
    once = dict(pipeline_mode=pl.Buffered(1))
    in_specs = [
        pl.BlockSpec((D_MODEL, tn), lambda i, j: (0, i)),
        pl.BlockSpec(lw["wqt"].shape, lambda i, j: (0, 0), **once),
        pl.BlockSpec(lw["subkeys"].shape, lambda i, j: (0, 0, 0, 0), **once),
        pl.BlockSpec((te, D_MODEL), lambda i, j: (0, 0), **once),
        pl.BlockSpec((te, D_MODEL), lambda i, j: (jnp.minimum(j + 1, nj - 1), 0)),
        pl.BlockSpec((D_MODEL, te), lambda i, j: (0, jnp.maximum(j - 1, 0))),
        pl.BlockSpec((tn, PLE_DIM), lambda i, j: (i, 0)),
        pl.BlockSpec(lw["plet"].shape, lambda i, j: (0, 0), **once),
        pl.BlockSpec(lw["gatet"].shape, lambda i, j: (0, 0), **once),
        pl.BlockSpec((D_MODEL, LANE), lambda i, j: (0, 0), **once),
        pl.BlockSpec((D_MODEL, LANE), lambda i, j: (0, 0), **once),
    ]
    route = (lane_tiles, PEER_HEADS, N_KEYS, LANE)
    scratch = [pltpu.VMEM((D_MODEL, tn), BF16), pltpu.VMEM((PEER_HEADS * 2 * N_KEYS, tn), F32),
               pltpu.VMEM(route, F32), pltpu.VMEM(route, F32), pltpu.VMEM(route, F32), pltpu.VMEM(route, F32),
               pltpu.VMEM((lane_tiles, PEER_HEADS, SUBLANE, LANE), F32),
               pltpu.VMEM((2, lane_tiles, te, LANE), F32), pltpu.VMEM((2, lane_tiles, te, LANE), BF16),
               pltpu.VMEM((D_MODEL, tn), F32)]
    return pl.pallas_call(
        functools.partial(_peer_kernel, tn=tn, te=te, nj=nj),
        grid=(t // tn, nj + 1),
        in_specs=in_specs,
        out_specs=pl.BlockSpec((tn, D_MODEL), lambda i, j: (i, 0)),
        out_shape=jax.ShapeDtypeStruct((t, D_MODEL), F32),
        scratch_shapes=scratch,
        compiler_params=pltpu.CompilerParams(dimension_semantics=("arbitrary", "arbitrary"),
                                             vmem_limit_bytes=VMEM_LIMIT),
        name="peer_ffn_ln_ple",
    )(xt, lw["wqt"], lw["subkeys"], lw["u"], lw["u"], lw["vt"], p2d, lw["plet"], lw["gatet"], lw["ln2_g"], lw["ln2_b"])


def _head_slots(w, d):
    r = w.shape[0]
    return jnp.pad(w.reshape(r, N_HEADS, d), ((0, 0), (0, 0), (0, SLOT - d))).reshape(r, GROUP_W)


def _rope_slots(w, d):
    r = w.shape[0]
    half = d // 2
    w = w.reshape(r, N_HEADS, 2, half)
    return jnp.pad(w, ((0, 0), (0, 0), (0, 0), (0, SLOT // 2 - half))).reshape(r, GROUP_W)


def _rope_tables(pos):
    half = RET_DK // 2
    inv = 1.0 / (ROPE_BASE ** (jnp.arange(0, RET_DK, 2, dtype=F32) / RET_DK))
    ang = pos[:, None] * inv[None, :]
    pad = ((0, 0), (0, SLOT // 2 - half))
    cos = jnp.pad(jnp.cos(ang), pad)
    sin = jnp.pad(jnp.sin(ang), pad)
    return jnp.concatenate([cos, cos], axis=1), jnp.concatenate([-sin, sin], axis=1)


def _retention_constants():
    log_gamma = jnp.log1p(-jnp.exp2(-5.0 - jnp.arange(N_HEADS, dtype=F32)))
    i = jnp.arange(CHUNK, dtype=F32)
    diff = i[:, None] - i[None, :]
    lg = log_gamma[:, None, None]
    dmat = jnp.where(diff >= 0, jnp.exp(jnp.where(diff >= 0, diff, 0.0) * lg), 0.0)
    qdec = jnp.broadcast_to(jnp.exp((i[None, :, None] + 1.0) * lg), (N_HEADS, CHUNK, SLOT))
    kdec = jnp.broadcast_to(jnp.exp((CHUNK - 1.0 - i[None, :, None]) * lg), (N_HEADS, CHUNK, SLOT))
    sdec = jnp.broadcast_to(jnp.exp(CHUNK * lg), (N_HEADS, 1, SLOT))
    gamma = jnp.broadcast_to(jnp.exp(lg), (N_HEADS, 1, SLOT))
    return dmat, qdec, kdec, sdec, gamma


def _layer_weights(i, lb, w_in, gla_w_gate, gla_b_gate, gla_norm, ret_norm, hgrn_norm, w_out,
                   ln1_g, ln1_b, ln2_g, ln2_b, peer_w_q, peer_subkeys, peer_u, peer_v, ple_proj, ple_gate):
    sizes = (N_HEADS * GLA_DK, N_HEADS * GLA_DK, N_HEADS * GLA_DV, N_HEADS * GLA_DV, GLA_LOWRANK,
             N_HEADS * RET_DK, N_HEADS * RET_DK, N_HEADS * RET_DV, N_HEADS * RET_DV,
             N_HEADS * HGRN_DK, N_HEADS * HGRN_DK, N_HEADS * HGRN_DV, N_HEADS * HGRN_DV)
    offs = [int(c) for c in np.cumsum(sizes)[:-1]]
    gq, gk, gv, gg, glr, rq, rk, rv, rg, hq, hf, hi, hg = jnp.split(w_in[i], offs, axis=1)
    lw = {}
    lw["w_gla"] = jnp.concatenate(
        [_head_slots(gq, GLA_DK), _head_slots(gk, GLA_DK), _head_slots(gv, GLA_DV), _head_slots(gg, GLA_DV),
         jnp.pad(glr, ((0, 0), (0, SLOT - GLA_LOWRANK)))], axis=1).astype(BF16)
    lw["w_ret"] = jnp.concatenate(
        [_rope_slots(rq, RET_DK), _rope_slots(rk, RET_DK), _head_slots(rv, RET_DV), _head_slots(rg, RET_DV)],
        axis=1).astype(BF16)
    lw["w_hgrn"] = jnp.concatenate([_head_slots(w, HGRN_DK) for w in (hq, hf, hi, hg)], axis=1).astype(BF16)
    lw["wgate"] = jnp.pad(_head_slots(gla_w_gate[i], GLA_DK), ((0, SLOT - GLA_LOWRANK), (0, 0)))
    lw["bgate"] = _head_slots(gla_b_gate[i][None, :], GLA_DK)
    lw["gla_norm"] = jnp.pad(gla_norm[i], ((0, 0), (0, SLOT - GLA_DV)))
    lw["ret_norm"] = jnp.pad(ret_norm[i], ((0, 0), (0, SLOT - RET_DV)))
    lw["hgrn_norm"] = jnp.pad(hgrn_norm[i], ((0, 0), (0, SLOT - HGRN_DV)))
    lbi = lb[i].reshape(1, N_HEADS, HGRN_DK)
    pad = ((0, 0), (0, 0), (0, SLOT - HGRN_DK))
    lw["loglb"] = jnp.pad(jnp.log(lbi), pad, constant_values=-1.0).reshape(1, GROUP_W)
    lw["l1mlb"] = jnp.pad(jnp.log1p(-lbi), pad, constant_values=-1.0).reshape(1, GROUP_W)
    wo = w_out[i]
    g_rows, r_rows = N_HEADS * GLA_DV, N_HEADS * RET_DV
    lw["w_out"] = jnp.stack([
        _head_slots(wo[:g_rows].T, GLA_DV).T, _head_slots(wo[g_rows:g_rows + r_rows].T, RET_DV).T,
        _head_slots(wo[g_rows + r_rows:].T, HGRN_DV).T]).astype(BF16)
    lw["ln1_g"], lw["ln1_b"] = ln1_g[i][None, :], ln1_b[i][None, :]
    lw["ln2_g"] = jnp.broadcast_to(ln2_g[i][:, None], (D_MODEL, LANE))
    lw["ln2_b"] = jnp.broadcast_to(ln2_b[i][:, None], (D_MODEL, LANE))
    lw["wqt"] = peer_w_q[i].T.astype(BF16)
    lw["subkeys"] = peer_subkeys[i].astype(BF16)
    lw["u"] = peer_u[i].astype(BF16)
    lw["vt"] = peer_v[i].T.astype(BF16)
    lw["plet"] = ple_proj[i].T.astype(BF16)
    lw["gatet"] = ple_gate[i].T.astype(BF16)
    return lw


def _unslot_state(st, dk, dv, rope=False):
    if rope:
        half = dk // 2
        st = jnp.concatenate([st[..., :half], st[..., SLOT // 2:SLOT // 2 + half]], axis=-1)
    return jnp.swapaxes(st[:, :, :dv, :dk], 2, 3)


def kernel(x_prompt, x_sample, p_prompt, p_sample, state_gla, state_ret, state_hgrn, w_in, gla_w_gate,
           gla_b_gate, gla_norm, ret_norm, hgrn_lb_logits, hgrn_norm, w_out, ln1_g, ln1_b, ln2_g, ln2_b,
           peer_w_q, peer_subkeys, peer_u, peer_v, ple_proj, ple_gate):
    bp, lp, _ = x_prompt.shape
    bs = x_sample.shape[0]
    assert x_sample.shape[1] == 1 and lp % CHUNK == 0

    lb = jnp.cumsum(jax.nn.softmax(hgrn_lb_logits.astype(F32), axis=0), axis=0)
    lb = lb - lb[0:1]
    mall_np, masks_np = _chunk_constants()
    mall, masks = jnp.asarray(mall_np), jnp.asarray(masks_np)
    dmat, qdec, kdec, sdec, gamma = _retention_constants()
    cos_p, sin_p = _rope_tables(jnp.arange(lp, dtype=F32))
    cos_s, sin_s = _rope_tables(PAST_LEN + jnp.arange(1, dtype=F32))

    xp = x_prompt.reshape(bp * lp, D_MODEL)
    xs = x_sample.reshape(bs, D_MODEL)
    gla_p, ret_p, hgrn_p, gla_s, ret_s, hgrn_s = [], [], [], [], [], []
    for i in range(DEPTH):
        lw = _layer_weights(i, lb, w_in, gla_w_gate, gla_b_gate, gla_norm, ret_norm, hgrn_norm, w_out,
                            ln1_g, ln1_b, ln2_g, ln2_b, peer_w_q, peer_subkeys, peer_u, peer_v,
                            ple_proj, ple_gate)
        lw.update(cos_s=cos_s, sin_s=sin_s, gamma=gamma)

        og, sg = _prompt_mixer_call(
            _gla_prompt_kernel, xp, bp, lp, lw["w_gla"],
            [lw["wgate"], lw["bgate"], lw["gla_norm"], mall, masks],
            [lw["w_gla"].shape[1], GROUP_W], "gla_prompt")
        orr, sr = _prompt_mixer_call(
            _ret_prompt_kernel, xp, bp, lp, lw["w_ret"],
            [(cos_p, None), (sin_p, None), lw["ret_norm"], dmat, qdec, kdec, sdec],
            [4 * GROUP_W], "ret_prompt")
        oh, sh = _prompt_mixer_call(
            _hgrn_prompt_kernel, xp, bp, lp, lw["w_hgrn"],
            [lw["loglb"], lw["l1mlb"], lw["hgrn_norm"], mall, masks],
            [4 * GROUP_W], "hgrn_prompt")
        xp = _peer_call(_out_call(og, orr, oh, xp, lw), p_prompt[i].reshape(bp * lp, PLE_DIM), lw)
        gla_p.append(_unslot_state(sg, GLA_DK, GLA_DV))
        ret_p.append(_unslot_state(sr, RET_DK, RET_DV, rope=True))
        hgrn_p.append(_unslot_state(sh, HGRN_DK, HGRN_DV))

        o_s, nsg, nsr, nsh = _sample_mixer_call(xs, lw, state_gla[i], state_ret[i], state_hgrn[i])
        xs = _peer_call(
            _out_call(o_s[:, :GROUP_W], o_s[:, GROUP_W:2 * GROUP_W], o_s[:, 2 * GROUP_W:], xs, lw),
            p_sample[i].reshape(bs, PLE_DIM), lw)
        gla_s.append(nsg.reshape(bs, N_HEADS, GLA_DK, GLA_DV))
        ret_s.append(nsr.reshape(bs, N_HEADS, RET_DK, RET_DV))
        hgrn_s.append(nsh.reshape(bs, N_HEADS, HGRN_DK, HGRN_DV))

    return (xp.reshape(bp, lp, D_MODEL), xs.reshape(bs, 1, D_MODEL),
            jnp.stack(gla_p), jnp.stack(ret_p), jnp.stack(hgrn_p),
            jnp.stack(gla_s), jnp.stack(ret_s), jnp.stack(hgrn_s))
```

```python
import functools

import numpy as np
import jax
import jax.numpy as jnp
from jax import lax
from jax.experimental import pallas as pl
from jax.experimental.pallas import tpu as pltpu

F32 = jnp.float32
BF16 = jnp.bfloat16
HIGHEST = lax.Precision.HIGHEST

D_MODEL = 1024
DEPTH = 2
PAST_LEN = 16384
N_HEADS = 4
GLA_DK, GLA_DV = 48, 96
RET_DK, RET_DV = 48, 96
HGRN_DK, HGRN_DV = 64, 64
GLA_LOWRANK = 16
GLA_TAU = 16.0
ROPE_BASE = 10000.0
CHUNK = 64
CHUNK_UNROLL = 2
PEER_HEADS = 8
N_KEYS = 128
PEER_TOPK = 16
PLE_DIM = 256
ALPHA = (2 * DEPTH) ** 0.25
NORM_EPS = 1e-5

LANE = 128
SUBLANE = 8
SLOT = LANE
GROUP_W = N_HEADS * SLOT
VMEM_LIMIT = 56 * 1024 * 1024
NEG_INF = float("-inf")


def _dot(a, b, precision=None):
    return jnp.dot(a, b, preferred_element_type=F32, precision=precision)


def _dot_nt(a, b):
    return lax.dot_general(a, b, (((1,), (1,)), ((), ())), preferred_element_type=F32)


def _dot_tn(a, b):
    return lax.dot_general(a, b, (((0,), (0,)), ((), ())), preferred_element_type=F32)


def _sigmoid(x):
    return jax.nn.sigmoid(x)


def _silu(x):
    return x * _sigmoid(x)


def _log_sigmoid(x):
    return jnp.minimum(x, 0.0) - jnp.log1p(jnp.exp(-jnp.abs(x)))


def _logaddexp(a, c):
    amax = jnp.maximum(a, c)
    delta = a - c
    return jnp.where(jnp.isnan(delta), a + c, amax + jnp.log1p(jnp.exp(-jnp.abs(delta))))


def _gelu(x):
    return 0.5 * x * (1.0 + lax.erf(x * np.float32(0.7071067811865476)))


def _lane_mask(n):
    return (lax.broadcasted_iota(jnp.int32, (1, LANE), 1) < n).astype(F32)


def _rms_head_norm(o, g_row, dv):
    ms = jnp.sum(o * o, axis=-1, keepdims=True) * (1.0 / dv)
    return o * lax.rsqrt(ms + NORM_EPS) * g_row


def _group_head_norm(o, g_row, dv):
    mask = _lane_mask(dv)
    mu = jnp.sum(o, axis=-1, keepdims=True) * (1.0 / dv)
    d = (o - mu) * mask
    var = jnp.sum(d * d, axis=-1, keepdims=True) * (1.0 / dv)
    return d * lax.rsqrt(var + NORM_EPS) * g_row


def _chunk_constants():
    i = np.arange(CHUNK)[:, None]
    t = np.arange(CHUNK)[None, :]
    masks = [i == t]
    half = CHUNK // 2
    while half >= 1:
        blk = i // (2 * half)
        second = (i % (2 * half)) >= half
        masks.append(second & ((t % (2 * half)) < half) & (blk == t // (2 * half)))
        half //= 2
    return (t <= i).astype(np.float32), np.stack(masks).astype(np.float32)


def _cumsum_rows(tri_bf, g):
    hi = g.astype(BF16)
    r1 = g - hi.astype(F32)
    mid = r1.astype(BF16)
    lo = (r1 - mid.astype(F32)).astype(BF16)
    return _dot(tri_bf, hi) + _dot(tri_bf, mid) + _dot(tri_bf, lo)


def _level_factors(b, g):
    row = lax.broadcasted_iota(jnp.int32, b.shape, 0)
    sub = lax.broadcasted_iota(jnp.int32, (CHUNK // SUBLANE, SUBLANE, LANE), 1)
    b3 = b.reshape(CHUNK // SUBLANE, SUBLANE, LANE)

    def sub_ref(r):
        return jnp.broadcast_to(b3[:, r:r + 1, :], b3.shape)

    out = []
    half = CHUNK // 2
    while half >= SUBLANE:
        ref = jnp.concatenate(
            [jnp.broadcast_to(b[m * 2 * half + half - 1:m * 2 * half + half], (2 * half, LANE))
             for m in range(CHUNK // (2 * half))], axis=0)
        out.append(jnp.exp(-jnp.abs(b - ref)))
        half //= 2
    out.append(jnp.exp(-jnp.abs(b3 - sub_ref(3))).reshape(b.shape))
    out.append(jnp.exp(-jnp.abs(b3 - jnp.where(sub < 4, sub_ref(1), sub_ref(5)))).reshape(b.shape))
    out.append(jnp.exp(jnp.where(row % 2 == 1, g, 0.0)))
    return out


def _vector_decay_chunk(q, k, v, g, st, tri_bf, masks_ref):
    b = _cumsum_rows(tri_bf, g)
    b_last = b[CHUNK - 1:CHUNK]
    kb = k.astype(BF16)
    sc = masks_ref[0] * _dot_nt(q.astype(BF16), kb)
    for l, f in enumerate(_level_factors(b, g)):
        sc = sc + masks_ref[1 + l] * _dot_nt((q * f).astype(BF16), (k * f).astype(BF16))
    vb = v.astype(BF16)
    o = _dot(sc.astype(BF16), vb) + _dot_nt((q * jnp.exp(b)).astype(BF16), st.astype(BF16))
    st_new = st * jnp.exp(b_last) + _dot_tn(vb, (k * jnp.exp(b_last - b)).astype(BF16))
    return o, st_new


def _gla_prompt_kernel(x_ref, w_ref, wg_ref, bg_ref, nrm_ref, tri_ref, masks_ref,
                       o_ref, st_ref, z_scr, la_scr, *, seg):
    @pl.when(pl.program_id(1) == 0)
    def _():
        st_ref[...] = jnp.zeros_like(st_ref)

    z_scr[...] = _dot(x_ref[...].astype(BF16), w_ref[...])
    pre = _dot(z_scr[:, 4 * GROUP_W:4 * GROUP_W + SLOT], wg_ref[...], precision=HIGHEST) + bg_ref[...]
    la_scr[...] = _log_sigmoid(pre) * (1.0 / GLA_TAU)
    tri = tri_ref[...]

    def body(c, carry):
        r0 = pl.multiple_of(c * CHUNK, CHUNK)
        rows = pl.ds(r0, CHUNK)
        for h in range(N_HEADS):
            q = z_scr[rows, h * SLOT:(h + 1) * SLOT] * (GLA_DK ** -0.5)
            k = z_scr[rows, GROUP_W + h * SLOT:GROUP_W + (h + 1) * SLOT]
            v = z_scr[rows, 2 * GROUP_W + h * SLOT:2 * GROUP_W + (h + 1) * SLOT]
            gate = z_scr[rows, 3 * GROUP_W + h * SLOT:3 * GROUP_W + (h + 1) * SLOT]
            g = la_scr[rows, h * SLOT:(h + 1) * SLOT]
            o, st_new = _vector_decay_chunk(q, k, v, g, st_ref[0, h], tri, masks_ref)
            st_ref[0, h] = st_new
            on = _rms_head_norm(o, nrm_ref[h:h + 1, :], GLA_DV)
            o_ref[rows, h * SLOT:(h + 1) * SLOT] = (on * _silu(gate)).astype(BF16)
        return carry

    lax.fori_loop(0, seg // CHUNK, body, 0, unroll=CHUNK_UNROLL)


def _hgrn_prompt_kernel(x_ref, w_ref, loglb_ref, l1mlb_ref, nrm_ref, tri_ref, masks_ref,
                        o_ref, st_ref, z_scr, *, seg):
    @pl.when(pl.program_id(1) == 0)
    def _():
        st_ref[...] = jnp.zeros_like(st_ref)

    z_scr[...] = _dot(x_ref[...].astype(BF16), w_ref[...])
    tri = tri_ref[...]
    kmask = _lane_mask(HGRN_DK)

    def body(c, carry):
        r0 = pl.multiple_of(c * CHUNK, CHUNK)
        rows = pl.ds(r0, CHUNK)
        for h in range(N_HEADS):
            sl = slice(h * SLOT, (h + 1) * SLOT)
            q = _silu(z_scr[rows, h * SLOT:(h + 1) * SLOT])
            hf = z_scr[rows, GROUP_W + h * SLOT:GROUP_W + (h + 1) * SLOT]
            v = z_scr[rows, 2 * GROUP_W + h * SLOT:2 * GROUP_W + (h + 1) * SLOT]
            gate = z_scr[rows, 3 * GROUP_W + h * SLOT:3 * GROUP_W + (h + 1) * SLOT]
            log_f = _logaddexp(loglb_ref[:, sl], l1mlb_ref[:, sl] + _log_sigmoid(hf))
            k = (1.0 - jnp.exp(log_f)) * kmask
            o, st_new = _vector_decay_chunk(q, k, v, log_f, st_ref[0, h], tri, masks_ref)
            st_ref[0, h] = st_new
            on = _rms_head_norm(o, nrm_ref[h:h + 1, :], HGRN_DV)
            o_ref[rows, sl] = (on * _silu(gate)).astype(BF16)
        return carry

    lax.fori_loop(0, seg // CHUNK, body, 0, unroll=CHUNK_UNROLL)


def _rotate(t, cs, sn):
    return t * cs + pltpu.roll(t, SLOT // 2, 1) * sn


def _ret_prompt_kernel(x_ref, w_ref, cos_ref, sin_ref, nrm_ref, dmat_ref, qdec_ref, kdec_ref, sdec_ref,
                       o_ref, st_ref, z_scr, *, seg):
    @pl.when(pl.program_id(1) == 0)
    def _():
        st_ref[...] = jnp.zeros_like(st_ref)

    z_scr[...] = _dot(x_ref[...].astype(BF16), w_ref[...])

    def body(c, carry):
        r0 = pl.multiple_of(c * CHUNK, CHUNK)
        rows = pl.ds(r0, CHUNK)
        cs = cos_ref[rows, :]
        sn = sin_ref[rows, :]
        for h in range(N_HEADS):
            sl = slice(h * SLOT, (h + 1) * SLOT)
            q = _rotate(z_scr[rows, h * SLOT:(h + 1) * SLOT], cs, sn)
            k = _rotate(z_scr[rows, GROUP_W + h * SLOT:GROUP_W + (h + 1) * SLOT], cs, sn) * (RET_DK ** -0.5)
            v = z_scr[rows, 2 * GROUP_W + h * SLOT:2 * GROUP_W + (h + 1) * SLOT]
            gate = z_scr[rows, 3 * GROUP_W + h * SLOT:3 * GROUP_W + (h + 1) * SLOT]
            st = st_ref[0, h]
            vb = v.astype(BF16)
            sc = dmat_ref[h] * _dot_nt(q.astype(BF16), k.astype(BF16))
            o = _dot(sc.astype(BF16), vb) + _dot_nt((q * qdec_ref[h]).astype(BF16), st.astype(BF16))
            st_ref[0, h] = st * sdec_ref[h] + _dot_tn(vb, (k * kdec_ref[h]).astype(BF16))
            on = _group_head_norm(o, nrm_ref[h:h + 1, :], RET_DV)
            o_ref[rows, sl] = (on * _silu(gate)).astype(BF16)
        return carry

    lax.fori_loop(0, seg // CHUNK, body, 0, unroll=CHUNK_UNROLL)


def _const_spec(shape):
    nd = len(shape)
    return pl.BlockSpec(shape, lambda *_: (0,) * nd)


def _prompt_mixer_call(kernel, x2d, batch, seq, w, extras, scratch_widths, name):
    seg = min(512, seq)
    nseg = seq // seg
    in_specs = [pl.BlockSpec((seg, D_MODEL), lambda b, s: (b * nseg + s, 0)), _const_spec(w.shape)]
    args = [x2d, w]
    for e in extras:
        if isinstance(e, tuple):
            arr, _ = e
            in_specs.append(pl.BlockSpec((seg, arr.shape[1]), lambda b, s: (s, 0)))
            args.append(arr)
        else:
            in_specs.append(_const_spec(e.shape))
            args.append(e)
    return pl.pallas_call(
        functools.partial(kernel, seg=seg),
        grid=(batch, nseg),
        in_specs=in_specs,
        out_specs=[pl.BlockSpec((seg, GROUP_W), lambda b, s: (b * nseg + s, 0)),
                   pl.BlockSpec((1, N_HEADS, SLOT, SLOT), lambda b, s: (b, 0, 0, 0))],
        out_shape=[jax.ShapeDtypeStruct((batch * seq, GROUP_W), BF16),
                   jax.ShapeDtypeStruct((batch, N_HEADS, SLOT, SLOT), F32)],
        scratch_shapes=[pltpu.VMEM((seg, wd), F32) for wd in scratch_widths],
        compiler_params=pltpu.CompilerParams(dimension_semantics=("arbitrary", "arbitrary"),
                                             vmem_limit_bytes=VMEM_LIMIT),
        name=name,
    )(*args)


def _sample_step(q, k, eg, v, s_ref, ns_ref, tq, tk, te, tv, to, s_t, sn_t, dk, dv, row_of):
    tq[...] = q.T
    tk[...] = k.T
    te[...] = eg.T
    tv[...] = v.T
    s_t[0:dk * dv, :] = s_ref[...].T
    vt = tv[0:dv, :]

    def body(kk, oacc):
        kr = row_of(kk)
        r = pl.multiple_of(kk * dv, SUBLANE)
        sn = s_t[pl.ds(r, dv), :] * te[pl.ds(kr, 1), :] + tk[pl.ds(kr, 1), :] * vt
        sn_t[pl.ds(r, dv), :] = sn
        return oacc + tq[pl.ds(kr, 1), :] * sn

    o_t = lax.fori_loop(0, dk, body, jnp.zeros((dv, q.shape[0]), F32))
    ns_ref[...] = sn_t[0:dk * dv, :].T
    to[...] = jnp.zeros_like(to)
    to[0:dv, :] = o_t
    return to[...].T


def _sample_mixer_kernel(x_ref, wg_ref, wr_ref, wh_ref, wgate_ref, bgate_ref, gn_ref, rn_ref, hn_ref,
                         cos_ref, sin_ref, gam_ref, loglb_ref, l1mlb_ref, sg_ref, sr_ref, sh_ref,
                         o_ref, nsg_ref, nsr_ref, nsh_ref,
                         zg, zr, zh, la, tq, tk, te, tv, to, s_t, sn_t):
    h = pl.program_id(0)

    @pl.when(h == 0)
    def _():
        xb = x_ref[...].astype(BF16)
        zg[...] = _dot(xb, wg_ref[...])
        zr[...] = _dot(xb, wr_ref[...])
        zh[...] = _dot(xb, wh_ref[...])
        pre = _dot(zg[:, 4 * GROUP_W:4 * GROUP_W + SLOT], wgate_ref[...], precision=HIGHEST) + bgate_ref[...]
        la[...] = _log_sigmoid(pre) * (1.0 / GLA_TAU)

    off = pl.multiple_of(h * SLOT, SLOT)
    sl = pl.ds(off, SLOT)
    tr = (tq, tk, te, tv, to, s_t, sn_t)
    batch = x_ref.shape[0]

    q = zg[:, sl] * (GLA_DK ** -0.5)
    k = zg[:, pl.ds(GROUP_W + off, SLOT)]
    v = zg[:, pl.ds(2 * GROUP_W + off, SLOT)]
    gate = zg[:, pl.ds(3 * GROUP_W + off, SLOT)]
    o = _sample_step(q, k, jnp.exp(la[:, sl]), v, sg_ref, nsg_ref, *tr, GLA_DK, GLA_DV, lambda kk: kk)
    o_ref[:, sl] = (_rms_head_norm(o, gn_ref[pl.ds(h, 1), :], GLA_DV) * _silu(gate)).astype(BF16)

    cs = cos_ref[...]
    sn = sin_ref[...]
    q = _rotate(zr[:, sl], cs, sn)
    k = _rotate(zr[:, pl.ds(GROUP_W + off, SLOT)], cs, sn) * (RET_DK ** -0.5)
    v = zr[:, pl.ds(2 * GROUP_W + off, SLOT)]
    gate = zr[:, pl.ds(3 * GROUP_W + off, SLOT)]
    eg = jnp.broadcast_to(gam_ref[h], (batch, SLOT))
    half = RET_DK // 2
    o = _sample_step(q, k, eg, v, sr_ref, nsr_ref, *tr, RET_DK, RET_DV,
                     lambda kk: kk + jnp.where(kk >= half, SLOT // 2 - half, 0))
    o_ref[:, pl.ds(GROUP_W + off, SLOT)] = (
        _group_head_norm(o, rn_ref[pl.ds(h, 1), :], RET_DV) * _silu(gate)).astype(BF16)

    q = _silu(zh[:, sl])
    hf = zh[:, pl.ds(GROUP_W + off, SLOT)]
    v = zh[:, pl.ds(2 * GROUP_W + off, SLOT)]
    gate = zh[:, pl.ds(3 * GROUP_W + off, SLOT)]
    log_f = _logaddexp(loglb_ref[:, sl], l1mlb_ref[:, sl] + _log_sigmoid(hf))
    f = jnp.exp(log_f)
    o = _sample_step(q, (1.0 - f) * _lane_mask(HGRN_DK), f, v, sh_ref, nsh_ref, *tr,
                     HGRN_DK, HGRN_DV, lambda kk: kk)
    o_ref[:, pl.ds(2 * GROUP_W + off, SLOT)] = (
        _rms_head_norm(o, hn_ref[pl.ds(h, 1), :], HGRN_DV) * _silu(gate)).astype(BF16)


def _sample_mixer_call(x2d, lw, sg, sr, sh):
    batch = x2d.shape[0]
    gsz, hsz = GLA_DK * GLA_DV, HGRN_DK * HGRN_DV
    consts = [lw["w_gla"], lw["w_ret"], lw["w_hgrn"], lw["wgate"], lw["bgate"], lw["gla_norm"], lw["ret_norm"],
              lw["hgrn_norm"], lw["cos_s"], lw["sin_s"], lw["gamma"], lw["loglb"], lw["l1mlb"]]
    in_specs = ([_const_spec(x2d.shape)] + [_const_spec(c.shape) for c in consts]
                + [pl.BlockSpec((batch, gsz), lambda h: (0, h)),
                   pl.BlockSpec((batch, gsz), lambda h: (0, h)),
                   pl.BlockSpec((batch, hsz), lambda h: (0, h))])
    out_specs = [_const_spec((batch, 3 * GROUP_W)),
                 pl.BlockSpec((batch, gsz), lambda h: (0, h)),
                 pl.BlockSpec((batch, gsz), lambda h: (0, h)),
                 pl.BlockSpec((batch, hsz), lambda h: (0, h))]
    out_shape = [jax.ShapeDtypeStruct((batch, 3 * GROUP_W), BF16),
                 jax.ShapeDtypeStruct((batch, N_HEADS * gsz), F32),
                 jax.ShapeDtypeStruct((batch, N_HEADS * gsz), F32),
                 jax.ShapeDtypeStruct((batch, N_HEADS * hsz), F32)]
    scratch = [pltpu.VMEM((batch, lw["w_gla"].shape[1]), F32), pltpu.VMEM((batch, 4 * GROUP_W), F32),
               pltpu.VMEM((batch, 4 * GROUP_W), F32), pltpu.VMEM((batch, GROUP_W), F32)]
    scratch += [pltpu.VMEM((SLOT, batch), F32) for _ in range(5)]
    scratch += [pltpu.VMEM((gsz, batch), F32), pltpu.VMEM((gsz, batch), F32)]
    return pl.pallas_call(
        _sample_mixer_kernel,
        grid=(N_HEADS,),
        in_specs=in_specs, out_specs=out_specs, out_shape=out_shape, scratch_shapes=scratch,
        compiler_params=pltpu.CompilerParams(dimension_semantics=("arbitrary",), vmem_limit_bytes=VMEM_LIMIT),
        name="sample_mixer",
    )(x2d, *consts, sg.reshape(batch, -1), sr.reshape(batch, -1), sh.reshape(batch, -1))


def _out_kernel(og_ref, or_ref, oh_ref, x_ref, wo_ref, g_ref, b_ref, xt_ref):
    mix = _dot(og_ref[...], wo_ref[0]) + _dot(or_ref[...], wo_ref[1]) + _dot(oh_ref[...], wo_ref[2])
    y = ALPHA * x_ref[...] + mix
    mu = jnp.mean(y, axis=-1, keepdims=True)
    d = y - mu
    var = jnp.mean(d * d, axis=-1, keepdims=True)
    xt_ref[...] = (d * lax.rsqrt(var + NORM_EPS) * g_ref[...] + b_ref[...]).T


def _out_call(og, orr, oh, x2d, lw):
    t = x2d.shape[0]
    tm = min(512, t)
    row = lambda i: (i, 0)
    return pl.pallas_call(
        _out_kernel,
        grid=(t // tm,),
        in_specs=[pl.BlockSpec((tm, GROUP_W), row), pl.BlockSpec((tm, GROUP_W), row),
                  pl.BlockSpec((tm, GROUP_W), row), pl.BlockSpec((tm, D_MODEL), row),
                  _const_spec(lw["w_out"].shape), _const_spec((1, D_MODEL)), _const_spec((1, D_MODEL))],
        out_specs=pl.BlockSpec((D_MODEL, tm), lambda i: (0, i)),
        out_shape=jax.ShapeDtypeStruct((D_MODEL, t), F32),
        compiler_params=pltpu.CompilerParams(dimension_semantics=("arbitrary",), vmem_limit_bytes=VMEM_LIMIT),
        name="out_proj_ln",
    )(og, orr, oh, x2d, lw["w_out"], lw["ln1_g"], lw["ln1_b"])


_CAND_ROWS = [(0, r) for r in range(8)] + [(1, r) for r in range(8)]
for _r1 in range(1, 8):
    _CAND_ROWS += [(1 + _r1, r) for r in range(PEER_TOPK // (_r1 + 1))]
_CAND_ROWS += [(9, r) for r in range(8)]


def _top16_rows(a):
    rows = []
    cur = a
    for _ in range(PEER_TOPK):
        m = jnp.max(cur, axis=0, keepdims=True)
        rows.append(m)
        cur = jnp.where(cur == m, NEG_INF, cur)
    return rows


def _rows_to_block(rows):
    sub = lax.broadcasted_iota(jnp.int32, (SUBLANE, LANE), 0)
    blk = jnp.broadcast_to(rows[0], (SUBLANE, LANE))
    for r in range(1, SUBLANE):
        blk = jnp.where(sub == r, rows[r], blk)
    return blk


def _route_tile(a1, a2):
    v1 = _top16_rows(a1)
    v2 = _top16_rows(a2)
    sub = lax.broadcasted_iota(jnp.int32, (SUBLANE, LANE), 0)
    v2a = _rows_to_block(v2[0:8])
    v2b = _rows_to_block(v2[8:16])
    v1b = _rows_to_block(v1[8:16])
    blocks = [v1[0] + v2a, v1[0] + v2b]
    for r1 in range(1, 8):
        blocks.append(jnp.where(sub < PEER_TOPK // (r1 + 1), v1[r1] + v2a, NEG_INF))
    blocks.append(v1b + v2[0])
    thr = jnp.full((1, LANE), NEG_INF, F32)
    for (bi, r) in _CAND_ROWS:
        ci = blocks[bi][r:r + 1, :]
        cnt8 = jnp.zeros((SUBLANE, LANE), F32)
        for blk in blocks:
            cnt8 = cnt8 + jnp.where(blk >= ci, 1.0, 0.0)
        cnt = jnp.sum(cnt8, axis=0, keepdims=True)
        thr = jnp.maximum(thr, jnp.where(cnt >= float(PEER_TOPK), ci, NEG_INF))
    top = v1[0] + v2[0]
    z8 = jnp.zeros((SUBLANE, LANE), F32)
    for blk in blocks:
        z8 = z8 + jnp.where(blk >= thr, jnp.exp(blk - top), 0.0)
    z = jnp.sum(z8, axis=0, keepdims=True)
    e1 = jnp.exp(a1 - v1[0])
    e2 = jnp.exp(a2 - v2[0]) / z
    return thr, e1, e2


def _peer_kernel(xt_ref, wqt_ref, sk_ref, u0_ref, u_ref, vt_ref, p_ref, plet_ref, gatet_ref, g2_ref, b2_ref,
                 out_ref, xbf, q_scr, s1, s2, e1, e2, thr, h_scr, w_scr, acc, *, tn, te, nj):
    j = pl.program_id(1)
    lane_tiles = tn // LANE

    def put_h(slot, hval):
        for lt in range(lane_tiles):
            h_scr[slot, lt] = hval[:, lt * LANE:(lt + 1) * LANE]

    @pl.when(j == 0)
    def _route():
        xbf[...] = xt_ref[...].astype(BF16)
        q_scr[...] = _dot(wqt_ref[...], xbf[...])

        def head_body(h, carry):
            r = pl.multiple_of(h * 2 * N_KEYS, 2 * N_KEYS)
            s1h = _dot(sk_ref[h, 0], q_scr[pl.ds(r, N_KEYS), :].astype(BF16))
            s2h = _dot(sk_ref[h, 1], q_scr[pl.ds(r + N_KEYS, N_KEYS), :].astype(BF16))
            for lt in range(lane_tiles):
                a1 = s1h[:, lt * LANE:(lt + 1) * LANE]
                a2 = s2h[:, lt * LANE:(lt + 1) * LANE]
                t_row, e1_t, e2_t = _route_tile(a1, a2)
                thr[lt, h] = jnp.broadcast_to(t_row, (SUBLANE, LANE))
                s1[lt, h] = a1
                s2[lt, h] = a2
                e1[lt, h] = e1_t
                e2[lt, h] = e2_t
            return carry

        lax.fori_loop(0, PEER_HEADS, head_body, 0)
        acc[...] = jnp.zeros_like(acc)
        w_scr[1] = jnp.zeros(w_scr.shape[1:], BF16)
        put_h(0, _dot(u0_ref[...], xbf[...]))

    na = te // N_KEYS
    assert na == SUBLANE
    a0 = pl.multiple_of(jnp.minimum(j, nj - 1) * na, SUBLANE)

    cur = j % 2
    nxt = (j + 1) % 2
    put_h(nxt, _dot(u_ref[...], xbf[...]))
    w_prev = jnp.concatenate([w_scr[nxt, lt] for lt in range(lane_tiles)], axis=1)
    acc[...] += _dot(vt_ref[...], w_prev)

    for lt in range(lane_tiles):
        s1blk = [s1[lt, h, pl.ds(a0, na), :] for h in range(PEER_HEADS)]
        e1blk = [e1[lt, h, pl.ds(a0, na), :] for h in range(PEER_HEADS)]
        for ai in range(0, na, 2):
            gates = [jnp.zeros((N_KEYS, LANE), F32), jnp.zeros((N_KEYS, LANE), F32)]
            for h in range(PEER_HEADS):
                s2h = s2[lt, h]
                e2h = e2[lt, h]
                th = thr[lt, h, 0:1, :]
                for d in range(2):
                    t = s1blk[h][ai + d:ai + d + 1, :] + s2h
                    gates[d] = gates[d] + jnp.where(t >= th, e1blk[h][ai + d:ai + d + 1, :] * e2h, 0.0)
            for d in range(2):
                rows = slice((ai + d) * N_KEYS, (ai + d + 1) * N_KEYS)
                w_scr[cur, lt, rows, :] = (gates[d] * _gelu(h_scr[cur, lt, rows, :])).astype(BF16)

    @pl.when(j == nj)
    def _finish():
        for lt in range(lane_tiles):
            lanes = slice(lt * LANE, (lt + 1) * LANE)
            y = ALPHA * xt_ref[:, lanes] + acc[:, lanes]
            mu = jnp.mean(y, axis=0, keepdims=True)
            d = y - mu
            var = jnp.mean(d * d, axis=0, keepdims=True)
            yn = d * lax.rsqrt(var + NORM_EPS) * g2_ref[...] + b2_ref[...]
            emb = _dot_nt(plet_ref[...], p_ref[lt * LANE:(lt + 1) * LANE, :].astype(BF16))
            gt = _dot(gatet_ref[...], yn.astype(BF16))
            out_ref[lt * LANE:(lt + 1) * LANE, :] = (yn + emb * _sigmoid(gt)).T


def _peer_call(xt, p2d, lw):
    t = xt.shape[1]
    tn = min(512, t)
    te = 1024
    n_exp = lw["u"].shape[0]
    nj = n_exp // te
    lane_tiles = tn // LANE
    once = dict(pipeline_mode=pl.Buffered(1))
    in_specs = [
        pl.BlockSpec((D_MODEL, tn), lambda i, j: (0, i)),
        pl.BlockSpec(lw["wqt"].shape, lambda i, j: (0, 0), **once),
        pl.BlockSpec(lw["subkeys"].shape, lambda i, j: (0, 0, 0, 0), **once),
        pl.BlockSpec((te, D_MODEL), lambda i, j: (0, 0), **once),
        pl.BlockSpec((te, D_MODEL), lambda i, j: (jnp.minimum(j + 1, nj - 1), 0)),
        pl.BlockSpec((D_MODEL, te), lambda i, j: (0, jnp.maximum(j - 1, 0))),
        pl.BlockSpec((tn, PLE_DIM), lambda i, j: (i, 0)),
        pl.BlockSpec(lw["plet"].shape, lambda i, j: (0, 0), **once),
        pl.BlockSpec(lw["gatet"].shape, lambda i, j: (0, 0), **once),
        pl.BlockSpec((D_MODEL, LANE), lambda i, j: (0, 0), **once),
        pl.BlockSpec((D_MODEL, LANE), lambda i, j: (0, 0), **once),
    ]
    route = (lane_tiles, PEER_HEADS, N_KEYS, LANE)
    scratch = [pltpu.VMEM((D_MODEL, tn), BF16), pltpu.VMEM((PEER_HEADS * 2 * N_KEYS, tn), F32),
               pltpu.VMEM(route, F32), pltpu.VMEM(route, F32), pltpu.VMEM(route, F32), pltpu.VMEM(route, F32),
               pltpu.VMEM((lane_tiles, PEER_HEADS, SUBLANE, LANE), F32),
               pltpu.VMEM((2, lane_tiles, te, LANE), F32), pltpu.VMEM((2, lane_tiles, te, LANE), BF16),
               pltpu.VMEM((D_MODEL, tn), F32)]
    return pl.pallas_call(
        functools.partial(_peer_kernel, tn=tn, te=te, nj=nj),
        grid=(t // tn, nj + 1),
        in_specs=in_specs,
        out_specs=pl.BlockSpec((tn, D_MODEL), lambda i, j: (i, 0)),
        out_shape=jax.ShapeDtypeStruct((t, D_MODEL), F32),
        scratch_shapes=scratch,
        compiler_params=pltpu.CompilerParams(dimension_semantics=("arbitrary", "arbitrary"),
                                             vmem_limit_bytes=VMEM_LIMIT),
        name="peer_ffn_ln_ple",
    )(xt, lw["wqt"], lw["subkeys"], lw["u"], lw["u"], lw["vt"], p2d, lw["plet"], lw["gatet"], lw["ln2_g"], lw["ln2_b"])


def _head_slots(w, d):
    r = w.shape[0]
    return jnp.pad(w.reshape(r, N_HEADS, d), ((0, 0), (0, 0), (0, SLOT - d))).reshape(r, GROUP_W)


def _rope_slots(w, d):
    r = w.shape[0]
    half = d // 2
    w = w.reshape(r, N_HEADS, 2, half)
    return jnp.pad(w, ((0, 0), (0, 0), (0, 0), (0, SLOT // 2 - half))).reshape(r, GROUP_W)


def _rope_tables(pos):
    half = RET_DK // 2
    inv = 1.0 / (ROPE_BASE ** (jnp.arange(0, RET_DK, 2, dtype=F32) / RET_DK))
    ang = pos[:, None] * inv[None, :]
    pad = ((0, 0), (0, SLOT // 2 - half))
    cos = jnp.pad(jnp.cos(ang), pad)
    sin = jnp.pad(jnp.sin(ang), pad)
    return jnp.concatenate([cos, cos], axis=1), jnp.concatenate([-sin, sin], axis=1)


def _retention_constants():
    log_gamma = jnp.log1p(-jnp.exp2(-5.0 - jnp.arange(N_HEADS, dtype=F32)))
    i = jnp.arange(CHUNK, dtype=F32)
    diff = i[:, None] - i[None, :]
    lg = log_gamma[:, None, None]
    dmat = jnp.where(diff >= 0, jnp.exp(jnp.where(diff >= 0, diff, 0.0) * lg), 0.0)
    qdec = jnp.broadcast_to(jnp.exp((i[None, :, None] + 1.0) * lg), (N_HEADS, CHUNK, SLOT))
    kdec = jnp.broadcast_to(jnp.exp((CHUNK - 1.0 - i[None, :, None]) * lg), (N_HEADS, CHUNK, SLOT))
    sdec = jnp.broadcast_to(jnp.exp(CHUNK * lg), (N_HEADS, 1, SLOT))
    gamma = jnp.broadcast_to(jnp.exp(lg), (N_HEADS, 1, SLOT))
    return dmat, qdec, kdec, sdec, gamma


def _layer_weights(i, lb, w_in, gla_w_gate, gla_b_gate, gla_norm, ret_norm, hgrn_norm, w_out,
                   ln1_g, ln1_b, ln2_g, ln2_b, peer_w_q, peer_subkeys, peer_u, peer_v, ple_proj, ple_gate):
    sizes = (N_HEADS * GLA_DK, N_HEADS * GLA_DK, N_HEADS * GLA_DV, N_HEADS * GLA_DV, GLA_LOWRANK,
             N_HEADS * RET_DK, N_HEADS * RET_DK, N_HEADS * RET_DV, N_HEADS * RET_DV,
             N_HEADS * HGRN_DK, N_HEADS * HGRN_DK, N_HEADS * HGRN_DV, N_HEADS * HGRN_DV)
    offs = [int(c) for c in np.cumsum(sizes)[:-1]]
    gq, gk, gv, gg, glr, rq, rk, rv, rg, hq, hf, hi, hg = jnp.split(w_in[i], offs, axis=1)
    lw = {}
    lw["w_gla"] = jnp.concatenate(
        [_head_slots(gq, GLA_DK), _head_slots(gk, GLA_DK), _head_slots(gv, GLA_DV), _head_slots(gg, GLA_DV),
         jnp.pad(glr, ((0, 0), (0, SLOT - GLA_LOWRANK)))], axis=1).astype(BF16)
    lw["w_ret"] = jnp.concatenate(
        [_rope_slots(rq, RET_DK), _rope_slots(rk, RET_DK), _head_slots(rv, RET_DV), _head_slots(rg, RET_DV)],
        axis=1).astype(BF16)
    lw["w_hgrn"] = jnp.concatenate([_head_slots(w, HGRN_DK) for w in (hq, hf, hi, hg)], axis=1).astype(BF16)
    lw["wgate"] = jnp.pad(_head_slots(gla_w_gate[i], GLA_DK), ((0, SLOT - GLA_LOWRANK), (0, 0)))
    lw["bgate"] = _head_slots(gla_b_gate[i][None, :], GLA_DK)
    lw["gla_norm"] = jnp.pad(gla_norm[i], ((0, 0), (0, SLOT - GLA_DV)))
    lw["ret_norm"] = jnp.pad(ret_norm[i], ((0, 0), (0, SLOT - RET_DV)))
    lw["hgrn_norm"] = jnp.pad(hgrn_norm[i], ((0, 0), (0, SLOT - HGRN_DV)))
    lbi = lb[i].reshape(1, N_HEADS, HGRN_DK)
    pad = ((0, 0), (0, 0), (0, SLOT - HGRN_DK))
    lw["loglb"] = jnp.pad(jnp.log(lbi), pad, constant_values=-1.0).reshape(1, GROUP_W)
    lw["l1mlb"] = jnp.pad(jnp.log1p(-lbi), pad, constant_values=-1.0).reshape(1, GROUP_W)
    wo = w_out[i]
    g_rows, r_rows = N_HEADS * GLA_DV, N_HEADS * RET_DV
    lw["w_out"] = jnp.stack([
        _head_slots(wo[:g_rows].T, GLA_DV).T, _head_slots(wo[g_rows:g_rows + r_rows].T, RET_DV).T,
        _head_slots(wo[g_rows + r_rows:].T, HGRN_DV).T]).astype(BF16)
    lw["ln1_g"], lw["ln1_b"] = ln1_g[i][None, :], ln1_b[i][None, :]
    lw["ln2_g"] = jnp.broadcast_to(ln2_g[i][:, None], (D_MODEL, LANE))
    lw["ln2_b"] = jnp.broadcast_to(ln2_b[i][:, None], (D_MODEL, LANE))
    lw["wqt"] = peer_w_q[i].T.astype(BF16)
    lw["subkeys"] = peer_subkeys[i].astype(BF16)
    lw["u"] = peer_u[i].astype(BF16)
    lw["vt"] = peer_v[i].T.astype(BF16)
    lw["plet"] = ple_proj[i].T.astype(BF16)
    lw["gatet"] = ple_gate[i].T.astype(BF16)
    return lw


def _unslot_state(st, dk, dv, rope=False):
    if rope:
        half = dk // 2
        st = jnp.concatenate([st[..., :half], st[..., SLOT // 2:SLOT // 2 + half]], axis=-1)
    return jnp.swapaxes(st[:, :, :dv, :dk], 2, 3)


def kernel(x_prompt, x_sample, p_prompt, p_sample, state_gla, state_ret, state_hgrn, w_in, gla_w_gate,
           gla_b_gate, gla_norm, ret_norm, hgrn_lb_logits, hgrn_norm, w_out, ln1_g, ln1_b, ln2_g, ln2_b,
           peer_w_q, peer_subkeys, peer_u, peer_v, ple_proj, ple_gate):
    bp, lp, _ = x_prompt.shape
    bs = x_sample.shape[0]
    assert x_sample.shape[1] == 1 and lp % CHUNK == 0

    lb = jnp.cumsum(jax.nn.softmax(hgrn_lb_logits.astype(F32), axis=0), axis=0)
    lb = lb - lb[0:1]
    tri_np, masks_np = _chunk_constants()
    tri, masks = jnp.asarray(tri_np, BF16), jnp.asarray(masks_np)
    dmat, qdec, kdec, sdec, gamma = _retention_constants()
    cos_p, sin_p = _rope_tables(jnp.arange(lp, dtype=F32))
    cos_s, sin_s = _rope_tables(PAST_LEN + jnp.arange(1, dtype=F32))

    xp = x_prompt.reshape(bp * lp, D_MODEL)
    xs = x_sample.reshape(bs, D_MODEL)
    gla_p, ret_p, hgrn_p, gla_s, ret_s, hgrn_s = [], [], [], [], [], []
    for i in range(DEPTH):
        lw = _layer_weights(i, lb, w_in, gla_w_gate, gla_b_gate, gla_norm, ret_norm, hgrn_norm, w_out,
                            ln1_g, ln1_b, ln2_g, ln2_b, peer_w_q, peer_subkeys, peer_u, peer_v,
                            ple_proj, ple_gate)
        lw.update(cos_s=cos_s, sin_s=sin_s, gamma=gamma)

        og, sg = _prompt_mixer_call(
            _gla_prompt_kernel, xp, bp, lp, lw["w_gla"],
            [lw["wgate"], lw["bgate"], lw["gla_norm"], tri, masks],
            [lw["w_gla"].shape[1], GROUP_W], "gla_prompt")
        orr, sr = _prompt_mixer_call(
            _ret_prompt_kernel, xp, bp, lp, lw["w_ret"],
            [(cos_p, None), (sin_p, None), lw["ret_norm"], dmat, qdec, kdec, sdec],
            [4 * GROUP_W], "ret_prompt")
        oh, sh = _prompt_mixer_call(
            _hgrn_prompt_kernel, xp, bp, lp, lw["w_hgrn"],
            [lw["loglb"], lw["l1mlb"], lw["hgrn_norm"], tri, masks],
            [4 * GROUP_W], "hgrn_prompt")
        xp = _peer_call(_out_call(og, orr, oh, xp, lw), p_prompt[i].reshape(bp * lp, PLE_DIM), lw)
        gla_p.append(_unslot_state(sg, GLA_DK, GLA_DV))
        ret_p.append(_unslot_state(sr, RET_DK, RET_DV, rope=True))
        hgrn_p.append(_unslot_state(sh, HGRN_DK, HGRN_DV))

        o_s, nsg, nsr, nsh = _sample_mixer_call(xs, lw, state_gla[i], state_ret[i], state_hgrn[i])
        xs = _peer_call(
            _out_call(o_s[:, :GROUP_W], o_s[:, GROUP_W:2 * GROUP_W], o_s[:, 2 * GROUP_W:], xs, lw),
            p_sample[i].reshape(bs, PLE_DIM), lw)
        gla_s.append(nsg.reshape(bs, N_HEADS, GLA_DK, GLA_DV))
        ret_s.append(nsr.reshape(bs, N_HEADS, RET_DK, RET_DV))
        hgrn_s.append(nsh.reshape(bs, N_HEADS, HGRN_DK, HGRN_DV))

    return (xp.reshape(bp, lp, D_MODEL), xs.reshape(bs, 1, D_MODEL),
            jnp.stack(gla_p), jnp.stack(ret_p), jnp.stack(hgrn_p),
            jnp.stack(gla_s), jnp.stack(ret_s), jnp.stack(hgrn_s))
```

```python
import functools

import numpy as np
import jax
import jax.numpy as jnp
from jax import lax
from jax.experimental import pallas as pl
from jax.experimental.pallas import tpu as pltpu

F32 = jnp.float32
BF16 = jnp.bfloat16
HIGHEST = lax.Precision.HIGHEST

D_MODEL = 1024
DEPTH = 2
PAST_LEN = 16384
N_HEADS = 4
GLA_DK, GLA_DV = 48, 96
RET_DK, RET_DV = 48, 96
HGRN_DK, HGRN_DV = 64, 64
GLA_LOWRANK = 16
GLA_TAU = 16.0
ROPE_BASE = 10000.0
CHUNK = 64
CHUNK_UNROLL = 2
PEER_HEADS = 8
N_KEYS = 128
PEER_TOPK = 16
PLE_DIM = 256
ALPHA = (2 * DEPTH) ** 0.25
NORM_EPS = 1e-5

LANE = 128
SUBLANE = 8
SLOT = LANE
GROUP_W = N_HEADS * SLOT
VMEM_LIMIT = 56 * 1024 * 1024
NEG_INF = float("-inf")


def _dot(a, b, precision=None):
    return jnp.dot(a, b, preferred_element_type=F32, precision=precision)


def _dot_nt(a, b):
    return lax.dot_general(a, b, (((1,), (1,)), ((), ())), preferred_element_type=F32)


def _dot_tn(a, b):
    return lax.dot_general(a, b, (((0,), (0,)), ((), ())), preferred_element_type=F32)


def _sigmoid(x):
    return jax.nn.sigmoid(x)


def _silu(x):
    return x * _sigmoid(x)


def _log_sigmoid(x):
    return jnp.minimum(x, 0.0) - jnp.log1p(jnp.exp(-jnp.abs(x)))


def _logaddexp(a, c):
    amax = jnp.maximum(a, c)
    delta = a - c
    return jnp.where(jnp.isnan(delta), a + c, amax + jnp.log1p(jnp.exp(-jnp.abs(delta))))


def _gelu(x):
    return 0.5 * x * (1.0 + lax.erf(x * np.float32(0.7071067811865476)))


def _lane_mask(n):
    return (lax.broadcasted_iota(jnp.int32, (1, LANE), 1) < n).astype(F32)


def _rms_head_norm(o, g_row, dv):
    ms = jnp.sum(o * o, axis=-1, keepdims=True) * (1.0 / dv)
    return o * lax.rsqrt(ms + NORM_EPS) * g_row


def _group_head_norm(o, g_row, dv):
    mask = _lane_mask(dv)
    mu = jnp.sum(o, axis=-1, keepdims=True) * (1.0 / dv)
    d = (o - mu) * mask
    var = jnp.sum(d * d, axis=-1, keepdims=True) * (1.0 / dv)
    return d * lax.rsqrt(var + NORM_EPS) * g_row


def _chunk_constants():
    i = np.arange(CHUNK)[:, None]
    t = np.arange(CHUNK)[None, :]
    masks = [i == t]
    half = CHUNK // 2
    while half >= 1:
        blk = i // (2 * half)
        second = (i % (2 * half)) >= half
        masks.append(second & ((t % (2 * half)) < half) & (blk == t // (2 * half)))
        half //= 2
    return (t <= i).astype(np.float32), np.stack(masks).astype(np.float32)


def _cumsum_rows(tri_bf, g):
    hi = g.astype(BF16)
    r1 = g - hi.astype(F32)
    mid = r1.astype(BF16)
    lo = (r1 - mid.astype(F32)).astype(BF16)
    return _dot(tri_bf, hi) + _dot(tri_bf, mid) + _dot(tri_bf, lo)


def _level_factors(b, g):
    row = lax.broadcasted_iota(jnp.int32, b.shape, 0)
    sub = lax.broadcasted_iota(jnp.int32, (CHUNK // SUBLANE, SUBLANE, LANE), 1)
    b3 = b.reshape(CHUNK // SUBLANE, SUBLANE, LANE)

    def sub_ref(r):
        return jnp.broadcast_to(b3[:, r:r + 1, :], b3.shape)

    out = []
    half = CHUNK // 2
    while half >= SUBLANE:
        ref = jnp.concatenate(
            [jnp.broadcast_to(b[m * 2 * half + half - 1:m * 2 * half + half], (2 * half, LANE))
             for m in range(CHUNK // (2 * half))], axis=0)
        out.append(jnp.exp(-jnp.abs(b - ref)))
        half //= 2
    out.append(jnp.exp(-jnp.abs(b3 - sub_ref(3))).reshape(b.shape))
    out.append(jnp.exp(-jnp.abs(b3 - jnp.where(sub < 4, sub_ref(1), sub_ref(5)))).reshape(b.shape))
    out.append(jnp.exp(jnp.where(row % 2 == 1, g, 0.0)))
    return out


def _vector_decay_chunk(q, k, v, g, st, tri_bf, masks_ref):
    b = _cumsum_rows(tri_bf, g)
    b_last = b[CHUNK - 1:CHUNK]
    kb = k.astype(BF16)
    sc = masks_ref[0] * _dot_nt(q.astype(BF16), kb)
    for l, f in enumerate(_level_factors(b, g)):
        sc = sc + masks_ref[1 + l] * _dot_nt((q * f).astype(BF16), (k * f).astype(BF16))
    vb = v.astype(BF16)
    o = _dot(sc.astype(BF16), vb) + _dot_nt((q * jnp.exp(b)).astype(BF16), st.astype(BF16))
    st_new = st * jnp.exp(b_last) + _dot_tn(vb, (k * jnp.exp(b_last - b)).astype(BF16))
    return o, st_new


def _gla_prompt_kernel(x_ref, w_ref, wg_ref, bg_ref, nrm_ref, tri_ref, masks_ref,
                       o_ref, st_ref, z_scr, la_scr, *, seg):
    @pl.when(pl.program_id(1) == 0)
    def _():
        st_ref[...] = jnp.zeros_like(st_ref)

    z_scr[...] = _dot(x_ref[...].astype(BF16), w_ref[...])
    pre = _dot(z_scr[:, 4 * GROUP_W:4 * GROUP_W + SLOT], wg_ref[...], precision=HIGHEST) + bg_ref[...]
    la_scr[...] = _log_sigmoid(pre) * (1.0 / GLA_TAU)
    tri = tri_ref[...]

    def body(c, carry):
        r0 = pl.multiple_of(c * CHUNK, CHUNK)
        rows = pl.ds(r0, CHUNK)
        for h in range(N_HEADS):
            q = z_scr[rows, h * SLOT:(h + 1) * SLOT] * (GLA_DK ** -0.5)
            k = z_scr[rows, GROUP_W + h * SLOT:GROUP_W + (h + 1) * SLOT]
            v = z_scr[rows, 2 * GROUP_W + h * SLOT:2 * GROUP_W + (h + 1) * SLOT]
            gate = z_scr[rows, 3 * GROUP_W + h * SLOT:3 * GROUP_W + (h + 1) * SLOT]
            g = la_scr[rows, h * SLOT:(h + 1) * SLOT]
            o, st_new = _vector_decay_chunk(q, k, v, g, st_ref[0, h], tri, masks_ref)
            st_ref[0, h] = st_new
            on = _rms_head_norm(o, nrm_ref[h:h + 1, :], GLA_DV)
            o_ref[rows, h * SLOT:(h + 1) * SLOT] = (on * _silu(gate)).astype(BF16)
        return carry

    lax.fori_loop(0, seg // CHUNK, body, 0, unroll=CHUNK_UNROLL)


def _hgrn_prompt_kernel(x_ref, w_ref, loglb_ref, l1mlb_ref, nrm_ref, tri_ref, masks_ref,
                        o_ref, st_ref, z_scr, *, seg):
    @pl.when(pl.program_id(1) == 0)
    def _():
        st_ref[...] = jnp.zeros_like(st_ref)

    z_scr[...] = _dot(x_ref[...].astype(BF16), w_ref[...])
    tri = tri_ref[...]
    kmask = _lane_mask(HGRN_DK)

    def body(c, carry):
        r0 = pl.multiple_of(c * CHUNK, CHUNK)
        rows = pl.ds(r0, CHUNK)
        for h in range(N_HEADS):
            sl = slice(h * SLOT, (h + 1) * SLOT)
            q = _silu(z_scr[rows, h * SLOT:(h + 1) * SLOT])
            hf = z_scr[rows, GROUP_W + h * SLOT:GROUP_W + (h + 1) * SLOT]
            v = z_scr[rows, 2 * GROUP_W + h * SLOT:2 * GROUP_W + (h + 1) * SLOT]
            gate = z_scr[rows, 3 * GROUP_W + h * SLOT:3 * GROUP_W + (h + 1) * SLOT]
            log_f = _logaddexp(loglb_ref[:, sl], l1mlb_ref[:, sl] + _log_sigmoid(hf))
            k = (1.0 - jnp.exp(log_f)) * kmask
            o, st_new = _vector_decay_chunk(q, k, v, log_f, st_ref[0, h], tri, masks_ref)
            st_ref[0, h] = st_new
            on = _rms_head_norm(o, nrm_ref[h:h + 1, :], HGRN_DV)
            o_ref[rows, sl] = (on * _silu(gate)).astype(BF16)
        return carry

    lax.fori_loop(0, seg // CHUNK, body, 0, unroll=CHUNK_UNROLL)


def _rotate(t, cs, sn):
    return t * cs + pltpu.roll(t, SLOT // 2, 1) * sn


def _ret_prompt_kernel(x_ref, w_ref, cos_ref, sin_ref, nrm_ref, dmat_ref, qdec_ref, kdec_ref, sdec_ref,
                       o_ref, st_ref, z_scr, *, seg):
    @pl.when(pl.program_id(1) == 0)
    def _():
        st_ref[...] = jnp.zeros_like(st_ref)

    z_scr[...] = _dot(x_ref[...].astype(BF16), w_ref[...])

    def body(c, carry):
        r0 = pl.multiple_of(c * CHUNK, CHUNK)
        rows = pl.ds(r0, CHUNK)
        cs = cos_ref[rows, :]
        sn = sin_ref[rows, :]
        for h in range(N_HEADS):
            sl = slice(h * SLOT, (h + 1) * SLOT)
            q = _rotate(z_scr[rows, h * SLOT:(h + 1) * SLOT], cs, sn)
            k = _rotate(z_scr[rows, GROUP_W + h * SLOT:GROUP_W + (h + 1) * SLOT], cs, sn) * (RET_DK ** -0.5)
            v = z_scr[rows, 2 * GROUP_W + h * SLOT:2 * GROUP_W + (h + 1) * SLOT]
            gate = z_scr[rows, 3 * GROUP_W + h * SLOT:3 * GROUP_W + (h + 1) * SLOT]
            st = st_ref[0, h]
            vb = v.astype(BF16)
            sc = dmat_ref[h] * _dot_nt(q.astype(BF16), k.astype(BF16))
            o = _dot(sc.astype(BF16), vb) + _dot_nt((q * qdec_ref[h]).astype(BF16), st.astype(BF16))
            st_ref[0, h] = st * sdec_ref[h] + _dot_tn(vb, (k * kdec_ref[h]).astype(BF16))
            on = _group_head_norm(o, nrm_ref[h:h + 1, :], RET_DV)
            o_ref[rows, sl] = (on * _silu(gate)).astype(BF16)
        return carry

    lax.fori_loop(0, seg // CHUNK, body, 0, unroll=CHUNK_UNROLL)


def _const_spec(shape):
    nd = len(shape)
    return pl.BlockSpec(shape, lambda *_: (0,) * nd)


def _prompt_mixer_call(kernel, x2d, batch, seq, w, extras, scratch_widths, name):
    seg = min(512, seq)
    nseg = seq // seg
    in_specs = [pl.BlockSpec((seg, D_MODEL), lambda b, s: (b * nseg + s, 0)), _const_spec(w.shape)]
    args = [x2d, w]
    for e in extras:
        if isinstance(e, tuple):
            arr, _ = e
            in_specs.append(pl.BlockSpec((seg, arr.shape[1]), lambda b, s: (s, 0)))
            args.append(arr)
        else:
            in_specs.append(_const_spec(e.shape))
            args.append(e)
    return pl.pallas_call(
        functools.partial(kernel, seg=seg),
        grid=(batch, nseg),
        in_specs=in_specs,
        out_specs=[pl.BlockSpec((seg, GROUP_W), lambda b, s: (b * nseg + s, 0)),
                   pl.BlockSpec((1, N_HEADS, SLOT, SLOT), lambda b, s: (b, 0, 0, 0))],
        out_shape=[jax.ShapeDtypeStruct((batch * seq, GROUP_W), BF16),
                   jax.ShapeDtypeStruct((batch, N_HEADS, SLOT, SLOT), F32)],
        scratch_shapes=[pltpu.VMEM((seg, wd), F32) for wd in scratch_widths],
        compiler_params=pltpu.CompilerParams(dimension_semantics=("arbitrary", "arbitrary"),
                                             vmem_limit_bytes=VMEM_LIMIT),
        name=name,
    )(*args)


def _sample_step(q, k, eg, v, s_ref, ns_ref, tq, tk, te, tv, to, s_t, sn_t, dk, dv, row_of):
    tq[...] = q.T
    tk[...] = k.T
    te[...] = eg.T
    tv[...] = v.T
    s_t[0:dk * dv, :] = s_ref[...].T
    vt = tv[0:dv, :]

    def body(kk, oacc):
        kr = row_of(kk)
        r = pl.multiple_of(kk * dv, SUBLANE)
        sn = s_t[pl.ds(r, dv), :] * te[pl.ds(kr, 1), :] + tk[pl.ds(kr, 1), :] * vt
        sn_t[pl.ds(r, dv), :] = sn
        return oacc + tq[pl.ds(kr, 1), :] * sn

    o_t = lax.fori_loop(0, dk, body, jnp.zeros((dv, q.shape[0]), F32))
    ns_ref[...] = sn_t[0:dk * dv, :].T
    to[...] = jnp.zeros_like(to)
    to[0:dv, :] = o_t
    return to[...].T


def _sample_mixer_kernel(x_ref, wg_ref, wr_ref, wh_ref, wgate_ref, bgate_ref, gn_ref, rn_ref, hn_ref,
                         cos_ref, sin_ref, gam_ref, loglb_ref, l1mlb_ref, sg_ref, sr_ref, sh_ref,
                         o_ref, nsg_ref, nsr_ref, nsh_ref,
                         zg, zr, zh, la, tq, tk, te, tv, to, s_t, sn_t):
    h = pl.program_id(0)

    @pl.when(h == 0)
    def _():
        xb = x_ref[...].astype(BF16)
        zg[...] = _dot(xb, wg_ref[...])
        zr[...] = _dot(xb, wr_ref[...])
        zh[...] = _dot(xb, wh_ref[...])
        pre = _dot(zg[:, 4 * GROUP_W:4 * GROUP_W + SLOT], wgate_ref[...], precision=HIGHEST) + bgate_ref[...]
        la[...] = _log_sigmoid(pre) * (1.0 / GLA_TAU)

    off = pl.multiple_of(h * SLOT, SLOT)
    sl = pl.ds(off, SLOT)
    tr = (tq, tk, te, tv, to, s_t, sn_t)
    batch = x_ref.shape[0]

    q = zg[:, sl] * (GLA_DK ** -0.5)
    k = zg[:, pl.ds(GROUP_W + off, SLOT)]
    v = zg[:, pl.ds(2 * GROUP_W + off, SLOT)]
    gate = zg[:, pl.ds(3 * GROUP_W + off, SLOT)]
    o = _sample_step(q, k, jnp.exp(la[:, sl]), v, sg_ref, nsg_ref, *tr, GLA_DK, GLA_DV, lambda kk: kk)
    o_ref[:, sl] = (_rms_head_norm(o, gn_ref[pl.ds(h, 1), :], GLA_DV) * _silu(gate)).astype(BF16)

    cs = cos_ref[...]
    sn = sin_ref[...]
    q = _rotate(zr[:, sl], cs, sn)
    k = _rotate(zr[:, pl.ds(GROUP_W + off, SLOT)], cs, sn) * (RET_DK ** -0.5)
    v = zr[:, pl.ds(2 * GROUP_W + off, SLOT)]
    gate = zr[:, pl.ds(3 * GROUP_W + off, SLOT)]
    eg = jnp.broadcast_to(gam_ref[h], (batch, SLOT))
    half = RET_DK // 2
    o = _sample_step(q, k, eg, v, sr_ref, nsr_ref, *tr, RET_DK, RET_DV,
                     lambda kk: kk + jnp.where(kk >= half, SLOT // 2 - half, 0))
    o_ref[:, pl.ds(GROUP_W + off, SLOT)] = (
        _group_head_norm(o, rn_ref[pl.ds(h, 1), :], RET_DV) * _silu(gate)).astype(BF16)

    q = _silu(zh[:, sl])
    hf = zh[:, pl.ds(GROUP_W + off, SLOT)]
    v = zh[:, pl.ds(2 * GROUP_W + off, SLOT)]
    gate = zh[:, pl.ds(3 * GROUP_W + off, SLOT)]
    log_f = _logaddexp(loglb_ref[:, sl], l1mlb_ref[:, sl] + _log_sigmoid(hf))
    f = jnp.exp(log_f)
    o = _sample_step(q, (1.0 - f) * _lane_mask(HGRN_DK), f, v, sh_ref, nsh_ref, *tr,
                     HGRN_DK, HGRN_DV, lambda kk: kk)
    o_ref[:, pl.ds(2 * GROUP_W + off, SLOT)] = (
        _rms_head_norm(o, hn_ref[pl.ds(h, 1), :], HGRN_DV) * _silu(gate)).astype(BF16)


def _sample_mixer_call(x2d, lw, sg, sr, sh):
    batch = x2d.shape[0]
    gsz, hsz = GLA_DK * GLA_DV, HGRN_DK * HGRN_DV
    consts = [lw["w_gla"], lw["w_ret"], lw["w_hgrn"], lw["wgate"], lw["bgate"], lw["gla_norm"], lw["ret_norm"],
              lw["hgrn_norm"], lw["cos_s"], lw["sin_s"], lw["gamma"], lw["loglb"], lw["l1mlb"]]
    in_specs = ([_const_spec(x2d.shape)] + [_const_spec(c.shape) for c in consts]
                + [pl.BlockSpec((batch, gsz), lambda h: (0, h)),
                   pl.BlockSpec((batch, gsz), lambda h: (0, h)),
                   pl.BlockSpec((batch, hsz), lambda h: (0, h))])
    out_specs = [_const_spec((batch, 3 * GROUP_W)),
                 pl.BlockSpec((batch, gsz), lambda h: (0, h)),
                 pl.BlockSpec((batch, gsz), lambda h: (0, h)),
                 pl.BlockSpec((batch, hsz), lambda h: (0, h))]
    out_shape = [jax.ShapeDtypeStruct((batch, 3 * GROUP_W), BF16),
                 jax.ShapeDtypeStruct((batch, N_HEADS * gsz), F32),
                 jax.ShapeDtypeStruct((batch, N_HEADS * gsz), F32),
                 jax.ShapeDtypeStruct((batch, N_HEADS * hsz), F32)]
    scratch = [pltpu.VMEM((batch, lw["w_gla"].shape[1]), F32), pltpu.VMEM((batch, 4 * GROUP_W), F32),
               pltpu.VMEM((batch, 4 * GROUP_W), F32), pltpu.VMEM((batch, GROUP_W), F32)]
    scratch += [pltpu.VMEM((SLOT, batch), F32) for _ in range(5)]
    scratch += [pltpu.VMEM((gsz, batch), F32), pltpu.VMEM((gsz, batch), F32)]
    return pl.pallas_call(
        _sample_mixer_kernel,
        grid=(N_HEADS,),
        in_specs=in_specs, out_specs=out_specs, out_shape=out_shape, scratch_shapes=scratch,
        compiler_params=pltpu.CompilerParams(dimension_semantics=("arbitrary",), vmem_limit_bytes=VMEM_LIMIT),
        name="sample_mixer",
    )(x2d, *consts, sg.reshape(batch, -1), sr.reshape(batch, -1), sh.reshape(batch, -1))


def _out_kernel(og_ref, or_ref, oh_ref, x_ref, wo_ref, g_ref, b_ref, xt_ref):
    mix = _dot(og_ref[...], wo_ref[0]) + _dot(or_ref[...], wo_ref[1]) + _dot(oh_ref[...], wo_ref[2])
    y = ALPHA * x_ref[...] + mix
    mu = jnp.mean(y, axis=-1, keepdims=True)
    d = y - mu
    var = jnp.mean(d * d, axis=-1, keepdims=True)
    xt_ref[...] = (d * lax.rsqrt(var + NORM_EPS) * g_ref[...] + b_ref[...]).T


def _out_call(og, orr, oh, x2d, lw):
    t = x2d.shape[0]
    tm = min(512, t)
    row = lambda i: (i, 0)
    return pl.pallas_call(
        _out_kernel,
        grid=(t // tm,),
        in_specs=[pl.BlockSpec((tm, GROUP_W), row), pl.BlockSpec((tm, GROUP_W), row),
                  pl.BlockSpec((tm, GROUP_W), row), pl.BlockSpec((tm, D_MODEL), row),
                  _const_spec(lw["w_out"].shape), _const_spec((1, D_MODEL)), _const_spec((1, D_MODEL))],
        out_specs=pl.BlockSpec((D_MODEL, tm), lambda i: (0, i)),
        out_shape=jax.ShapeDtypeStruct((D_MODEL, t), F32),
        compiler_params=pltpu.CompilerParams(dimension_semantics=("arbitrary",), vmem_limit_bytes=VMEM_LIMIT),
        name="out_proj_ln",
    )(og, orr, oh, x2d, lw["w_out"], lw["ln1_g"], lw["ln1_b"])


def _oddeven_merge(lo, hi, r):
    step = r * 2
    if step < hi - lo:
        yield from _oddeven_merge(lo, hi, step)
        yield from _oddeven_merge(lo + r, hi, step)
        yield from [(i, i + r) for i in range(lo + r, hi - r, step)]
    else:
        yield (lo, lo + r)


def _oddeven_merge_sort(lo, hi):
    if hi - lo >= 1:
        mid = lo + (hi - lo) // 2
        yield from _oddeven_merge_sort(lo, mid)
        yield from _oddeven_merge_sort(mid + 1, hi)
        yield from _oddeven_merge(lo, hi, 1)


_SORT16 = tuple(_oddeven_merge_sort(0, PEER_TOPK - 1))
_BITONIC16 = tuple((i, i + d) for d in (8, 4, 2, 1) for i in range(PEER_TOPK) if i & d == 0)


def _compare_exchange(x, pairs):
    for i, j in pairs:
        x[i], x[j] = jnp.maximum(x[i], x[j]), jnp.minimum(x[i], x[j])
    return x


def _top16_sorted(a3):
    x = _compare_exchange([a3[i] for i in range(PEER_TOPK)], _SORT16)
    for shift in (4, 2, 1):
        y = [pltpu.roll(v, shift, 0) for v in x]
        x = _compare_exchange([jnp.maximum(x[i], y[PEER_TOPK - 1 - i]) for i in range(PEER_TOPK)], _BITONIC16)
    return x


def _sublane_block(rows):
    sub = lax.broadcasted_iota(jnp.int32, (SUBLANE, LANE), 0)
    blk = rows[0]
    for r in range(1, SUBLANE):
        blk = jnp.where(sub == r, rows[r], blk)
    return blk


def _route_tile(a1, a2):
    v1 = _top16_sorted(a1)
    v2 = _top16_sorted(a2)
    sub = lax.broadcasted_iota(jnp.int32, (SUBLANE, LANE), 0)
    v2a = _sublane_block(v2[0:8])
    v2b = _sublane_block(v2[8:16])
    v1b = _sublane_block(v1[8:16])
    cands = [v1[0] + v2a, v1[0] + v2b]
    for r1 in range(1, 8):
        cands.append(jnp.where(sub < PEER_TOPK // (r1 + 1), v1[r1] + v2a, NEG_INF))
    cands.append(v1b + v2[0])
    thr = jnp.full((SUBLANE, LANE), NEG_INF, F32)
    need = jnp.full((SUBLANE, LANE), float(PEER_TOPK), F32)
    cur = cands
    for _ in range(PEER_TOPK):
        m8 = cur[0]
        for blk in cur[1:]:
            m8 = jnp.maximum(m8, blk)
        m = jnp.broadcast_to(jnp.max(m8, axis=0, keepdims=True), (SUBLANE, LANE))
        hits = [blk == m for blk in cur]
        cnt8 = jnp.where(hits[0], 1.0, 0.0)
        for hit in hits[1:]:
            cnt8 = cnt8 + jnp.where(hit, 1.0, 0.0)
        thr = jnp.where(need > 0.0, m, thr)
        need = need - jnp.broadcast_to(jnp.sum(cnt8, axis=0, keepdims=True), (SUBLANE, LANE))
        cur = [jnp.where(hit, NEG_INF, blk) for hit, blk in zip(hits, cur)]
    top = v1[0] + v2[0]
    z8 = jnp.zeros((SUBLANE, LANE), F32)
    for blk in cands:
        z8 = z8 + jnp.where(blk >= thr, jnp.exp(blk - top), 0.0)
    z = jnp.broadcast_to(jnp.sum(z8, axis=0, keepdims=True), (SUBLANE, LANE))
    n1 = jnp.zeros(a1.shape, F32)
    rank2 = jnp.zeros(a2.shape, F32)
    for r in range(PEER_TOPK):
        n1 = n1 + jnp.where(a1 + v2[r] >= thr, 1.0, 0.0)
        rank2 = rank2 + jnp.where(v2[r] > a2, 1.0, 0.0)
    e1 = jnp.exp(a1 - v1[0])
    e2 = jnp.exp(a2 - v2[0]) / z
    return n1, rank2, e1, e2


def _peer_kernel(xt_ref, wqt_ref, sk_ref, u0_ref, u_ref, vt_ref, p_ref, plet_ref, gatet_ref, g2_ref, b2_ref,
                 out_ref, xbf, q_scr, n1, e1, rk2, e2, h_scr, w_scr, acc, *, tn, te, nj):
    j = pl.program_id(1)
    lane_tiles = tn // LANE

    def put_h(slot, hval):
        for lt in range(lane_tiles):
            h_scr[slot, lt] = hval[:, lt * LANE:(lt + 1) * LANE]

    @pl.when(j == 0)
    def _route():
        xbf[...] = xt_ref[...].astype(BF16)
        q_scr[...] = _dot(wqt_ref[...], xbf[...])

        def head_body(h, carry):
            r = pl.multiple_of(h * 2 * N_KEYS, 2 * N_KEYS)
            s1h = _dot(sk_ref[h, 0], q_scr[pl.ds(r, N_KEYS), :].astype(BF16))
            s2h = _dot(sk_ref[h, 1], q_scr[pl.ds(r + N_KEYS, N_KEYS), :].astype(BF16))
            for lt in range(lane_tiles):
                groups = (N_KEYS // SUBLANE, SUBLANE, LANE)
                a1 = s1h[:, lt * LANE:(lt + 1) * LANE].reshape(groups)
                a2 = s2h[:, lt * LANE:(lt + 1) * LANE].reshape(groups)
                n1_t, rank2_t, e1_t, e2_t = _route_tile(a1, a2)
                n1[lt, h] = n1_t.reshape(N_KEYS, LANE)
                e1[lt, h] = e1_t.reshape(N_KEYS, LANE)
                rk2[lt, h] = rank2_t.reshape(N_KEYS, LANE).astype(BF16).reshape(rk2.shape[2:])
                e2[lt, h] = e2_t.reshape(N_KEYS, LANE).astype(BF16).reshape(e2.shape[2:])
            return carry

        lax.fori_loop(0, PEER_HEADS, head_body, 0)
        acc[...] = jnp.zeros_like(acc)
        put_h(0, _dot(u0_ref[...], xbf[...]))

    na = te // N_KEYS
    assert na == SUBLANE
    a0 = pl.multiple_of(jnp.minimum(j, nj - 1) * na, SUBLANE)

    packed = rk2.shape[2:]
    row16 = (1, 2 * SUBLANE, LANE)

    def step(cur, nxt, next_h, prev_v, this_w):
        piece = te // lane_tiles
        for lt in range(lane_tiles):
            pr = slice(lt * piece, (lt + 1) * piece)
            if next_h:
                h_piece = _dot(u_ref[pr, :], xbf[...])
                for l2 in range(lane_tiles):
                    h_scr[nxt, l2, pr, :] = h_piece[:, l2 * LANE:(l2 + 1) * LANE]
            if prev_v:
                w_prev = jnp.concatenate([w_scr[nxt, l2, pr, :] for l2 in range(lane_tiles)], axis=1)
                acc[...] += _dot(vt_ref[:, pr], w_prev)
            if not this_w:
                continue
            n1blk = [n1[lt, h, pl.ds(a0, na), :] for h in range(PEER_HEADS)]
            e1blk = [e1[lt, h, pl.ds(a0, na), :] for h in range(PEER_HEADS)]
            for ai in range(0, na, 2):
                gates = [jnp.zeros(packed, BF16), jnp.zeros(packed, BF16)]
                for h in range(PEER_HEADS):
                    rk2h = rk2[lt, h]
                    e2h = e2[lt, h]
                    for d in range(2):
                        n1a = jnp.broadcast_to(n1blk[h][ai + d:ai + d + 1, :], row16[1:]).astype(BF16).reshape(row16)
                        e1a = jnp.broadcast_to(e1blk[h][ai + d:ai + d + 1, :], row16[1:]).astype(BF16).reshape(row16)
                        gates[d] = gates[d] + jnp.where(rk2h < n1a, e2h, jnp.zeros_like(e2h)) * e1a
                for d in range(2):
                    rows = slice((ai + d) * N_KEYS, (ai + d + 1) * N_KEYS)
                    act = _gelu(h_scr[cur, lt, rows, :]).astype(BF16).reshape(packed)
                    w_scr[cur, lt, rows, :] = (gates[d] * act).reshape(N_KEYS, LANE)

    assert nj % 2 == 0 and nj >= 4
    steady = jnp.logical_and(j > 0, j < nj - 1)
    pl.when(j == 0)(functools.partial(step, 0, 1, True, False, True))
    for parity in range(2):
        pl.when(jnp.logical_and(steady, j % 2 == parity))(
            functools.partial(step, parity, 1 - parity, True, True, True))
    pl.when(j == nj - 1)(functools.partial(step, 1, 0, False, True, True))
    pl.when(j == nj)(functools.partial(step, 0, 1, False, True, False))

    @pl.when(j == nj)
    def _finish():
        for lt in range(lane_tiles):
            lanes = slice(lt * LANE, (lt + 1) * LANE)
            y = ALPHA * xt_ref[:, lanes] + acc[:, lanes]
            mu = jnp.mean(y, axis=0, keepdims=True)
            d = y - mu
            var = jnp.mean(d * d, axis=0, keepdims=True)
            yn = d * lax.rsqrt(var + NORM_EPS) * g2_ref[...] + b2_ref[...]
            emb = _dot_nt(plet_ref[...], p_ref[lt * LANE:(lt + 1) * LANE, :].astype(BF16))
            gt = _dot(gatet_ref[...], yn.astype(BF16))
            out_ref[lt * LANE:(lt + 1) * LANE, :] = (yn + emb * _sigmoid(gt)).T


def _peer_call(xt, p2d, lw):
    t = xt.shape[1]
    tn = min(512, t)
    te = 1024
    n_exp = lw["u"].shape[0]
    nj = n_exp // te
    lane_tiles = tn // LANE
    once = dict(pipeline_mode=pl.Buffered(1))
    in_specs = [
        pl.BlockSpec((D_MODEL, tn), lambda i, j: (0, i)),
        pl.BlockSpec(lw["wqt"].shape, lambda i, j: (0, 0), **once),
        pl.BlockSpec(lw["subkeys"].shape, lambda i, j: (0, 0, 0, 0), **once),
        pl.BlockSpec((te, D_MODEL), lambda i, j: (0, 0), **once),
        pl.BlockSpec((te, D_MODEL), lambda i, j: (jnp.minimum(j + 1, nj - 1), 0)),
        pl.BlockSpec((D_MODEL, te), lambda i, j: (0, jnp.maximum(j - 1, 0))),
        pl.BlockSpec((tn, PLE_DIM), lambda i, j: (i, 0)),
        pl.BlockSpec(lw["plet"].shape, lambda i, j: (0, 0), **once),
        pl.BlockSpec(lw["gatet"].shape, lambda i, j: (0, 0), **once),
        pl.BlockSpec((D_MODEL, LANE), lambda i, j: (0, 0), **once),
        pl.BlockSpec((D_MODEL, LANE), lambda i, j: (0, 0), **once),
    ]
    route = (lane_tiles, PEER_HEADS, N_KEYS, LANE)
    route_packed = (lane_tiles, PEER_HEADS, N_KEYS // (2 * SUBLANE), 2 * SUBLANE, LANE)
    scratch = [pltpu.VMEM((D_MODEL, tn), BF16), pltpu.VMEM((PEER_HEADS * 2 * N_KEYS, tn), F32),
               pltpu.VMEM(route, F32), pltpu.VMEM(route, F32),
               pltpu.VMEM(route_packed, BF16), pltpu.VMEM(route_packed, BF16),
               pltpu.VMEM((2, lane_tiles, te, LANE), F32), pltpu.VMEM((2, lane_tiles, te, LANE), BF16),
               pltpu.VMEM((D_MODEL, tn), F32)]
    return pl.pallas_call(
        functools.partial(_peer_kernel, tn=tn, te=te, nj=nj),
        grid=(t // tn, nj + 1),
        in_specs=in_specs,
        out_specs=pl.BlockSpec((tn, D_MODEL), lambda i, j: (i, 0)),
        out_shape=jax.ShapeDtypeStruct((t, D_MODEL), F32),
        scratch_shapes=scratch,
        compiler_params=pltpu.CompilerParams(dimension_semantics=("arbitrary", "arbitrary"),
                                             vmem_limit_bytes=VMEM_LIMIT),
        name="peer_ffn_ln_ple",
    )(xt, lw["wqt"], lw["subkeys"], lw["u"], lw["u"], lw["vt"], p2d, lw["plet"], lw["gatet"], lw["ln2_g"], lw["ln2_b"])


def _head_slots(w, d):
    r = w.shape[0]
    return jnp.pad(w.reshape(r, N_HEADS, d), ((0, 0), (0, 0), (0, SLOT - d))).reshape(r, GROUP_W)


def _rope_slots(w, d):
    r = w.shape[0]
    half = d // 2
    w = w.reshape(r, N_HEADS, 2, half)
    return jnp.pad(w, ((0, 0), (0, 0), (0, 0), (0, SLOT // 2 - half))).reshape(r, GROUP_W)


def _rope_tables(pos):
    half = RET_DK // 2
    inv = 1.0 / (ROPE_BASE ** (jnp.arange(0, RET_DK, 2, dtype=F32) / RET_DK))
    ang = pos[:, None] * inv[None, :]
    pad = ((0, 0), (0, SLOT // 2 - half))
    cos = jnp.pad(jnp.cos(ang), pad)
    sin = jnp.pad(jnp.sin(ang), pad)
    return jnp.concatenate([cos, cos], axis=1), jnp.concatenate([-sin, sin], axis=1)


def _retention_constants():
    log_gamma = jnp.log1p(-jnp.exp2(-5.0 - jnp.arange(N_HEADS, dtype=F32)))
    i = jnp.arange(CHUNK, dtype=F32)
    diff = i[:, None] - i[None, :]
    lg = log_gamma[:, None, None]
    dmat = jnp.where(diff >= 0, jnp.exp(jnp.where(diff >= 0, diff, 0.0) * lg), 0.0)
    qdec = jnp.broadcast_to(jnp.exp((i[None, :, None] + 1.0) * lg), (N_HEADS, CHUNK, SLOT))
    kdec = jnp.broadcast_to(jnp.exp((CHUNK - 1.0 - i[None, :, None]) * lg), (N_HEADS, CHUNK, SLOT))
    sdec = jnp.broadcast_to(jnp.exp(CHUNK * lg), (N_HEADS, 1, SLOT))
    gamma = jnp.broadcast_to(jnp.exp(lg), (N_HEADS, 1, SLOT))
    return dmat, qdec, kdec, sdec, gamma


def _layer_weights(i, lb, w_in, gla_w_gate, gla_b_gate, gla_norm, ret_norm, hgrn_norm, w_out,
                   ln1_g, ln1_b, ln2_g, ln2_b, peer_w_q, peer_subkeys, peer_u, peer_v, ple_proj, ple_gate):
    sizes = (N_HEADS * GLA_DK, N_HEADS * GLA_DK, N_HEADS * GLA_DV, N_HEADS * GLA_DV, GLA_LOWRANK,
             N_HEADS * RET_DK, N_HEADS * RET_DK, N_HEADS * RET_DV, N_HEADS * RET_DV,
             N_HEADS * HGRN_DK, N_HEADS * HGRN_DK, N_HEADS * HGRN_DV, N_HEADS * HGRN_DV)
    offs = [int(c) for c in np.cumsum(sizes)[:-1]]
    gq, gk, gv, gg, glr, rq, rk, rv, rg, hq, hf, hi, hg = jnp.split(w_in[i], offs, axis=1)
    lw = {}
    lw["w_gla"] = jnp.concatenate(
        [_head_slots(gq, GLA_DK), _head_slots(gk, GLA_DK), _head_slots(gv, GLA_DV), _head_slots(gg, GLA_DV),
         jnp.pad(glr, ((0, 0), (0, SLOT - GLA_LOWRANK)))], axis=1).astype(BF16)
    lw["w_ret"] = jnp.concatenate(
        [_rope_slots(rq, RET_DK), _rope_slots(rk, RET_DK), _head_slots(rv, RET_DV), _head_slots(rg, RET_DV)],
        axis=1).astype(BF16)
    lw["w_hgrn"] = jnp.concatenate([_head_slots(w, HGRN_DK) for w in (hq, hf, hi, hg)], axis=1).astype(BF16)
    lw["wgate"] = jnp.pad(_head_slots(gla_w_gate[i], GLA_DK), ((0, SLOT - GLA_LOWRANK), (0, 0)))
    lw["bgate"] = _head_slots(gla_b_gate[i][None, :], GLA_DK)
    lw["gla_norm"] = jnp.pad(gla_norm[i], ((0, 0), (0, SLOT - GLA_DV)))
    lw["ret_norm"] = jnp.pad(ret_norm[i], ((0, 0), (0, SLOT - RET_DV)))
    lw["hgrn_norm"] = jnp.pad(hgrn_norm[i], ((0, 0), (0, SLOT - HGRN_DV)))
    lbi = lb[i].reshape(1, N_HEADS, HGRN_DK)
    pad = ((0, 0), (0, 0), (0, SLOT - HGRN_DK))
    lw["loglb"] = jnp.pad(jnp.log(lbi), pad, constant_values=-1.0).reshape(1, GROUP_W)
    lw["l1mlb"] = jnp.pad(jnp.log1p(-lbi), pad, constant_values=-1.0).reshape(1, GROUP_W)
    wo = w_out[i]
    g_rows, r_rows = N_HEADS * GLA_DV, N_HEADS * RET_DV
    lw["w_out"] = jnp.stack([
        _head_slots(wo[:g_rows].T, GLA_DV).T, _head_slots(wo[g_rows:g_rows + r_rows].T, RET_DV).T,
        _head_slots(wo[g_rows + r_rows:].T, HGRN_DV).T]).astype(BF16)
    lw["ln1_g"], lw["ln1_b"] = ln1_g[i][None, :], ln1_b[i][None, :]
    lw["ln2_g"] = jnp.broadcast_to(ln2_g[i][:, None], (D_MODEL, LANE))
    lw["ln2_b"] = jnp.broadcast_to(ln2_b[i][:, None], (D_MODEL, LANE))
    lw["wqt"] = peer_w_q[i].T.astype(BF16)
    lw["subkeys"] = peer_subkeys[i].astype(BF16)
    lw["u"] = peer_u[i].astype(BF16)
    lw["vt"] = peer_v[i].T.astype(BF16)
    lw["plet"] = ple_proj[i].T.astype(BF16)
    lw["gatet"] = ple_gate[i].T.astype(BF16)
    return lw


def _unslot_state(st, dk, dv, rope=False):
    if rope:
        half = dk // 2
        st = jnp.concatenate([st[..., :half], st[..., SLOT // 2:SLOT // 2 + half]], axis=-1)
    return jnp.swapaxes(st[:, :, :dv, :dk], 2, 3)


def kernel(x_prompt, x_sample, p_prompt, p_sample, state_gla, state_ret, state_hgrn, w_in, gla_w_gate,
           gla_b_gate, gla_norm, ret_norm, hgrn_lb_logits, hgrn_norm, w_out, ln1_g, ln1_b, ln2_g, ln2_b,
           peer_w_q, peer_subkeys, peer_u, peer_v, ple_proj, ple_gate):
    bp, lp, _ = x_prompt.shape
    bs = x_sample.shape[0]
    assert x_sample.shape[1] == 1 and lp % CHUNK == 0

    lb = jnp.cumsum(jax.nn.softmax(hgrn_lb_logits.astype(F32), axis=0), axis=0)
    lb = lb - lb[0:1]
    tri_np, masks_np = _chunk_constants()
    tri, masks = jnp.asarray(tri_np, BF16), jnp.asarray(masks_np)
    dmat, qdec, kdec, sdec, gamma = _retention_constants()
    cos_p, sin_p = _rope_tables(jnp.arange(lp, dtype=F32))
    cos_s, sin_s = _rope_tables(PAST_LEN + jnp.arange(1, dtype=F32))

    xp = x_prompt.reshape(bp * lp, D_MODEL)
    xs = x_sample.reshape(bs, D_MODEL)
    gla_p, ret_p, hgrn_p, gla_s, ret_s, hgrn_s = [], [], [], [], [], []
    for i in range(DEPTH):
        lw = _layer_weights(i, lb, w_in, gla_w_gate, gla_b_gate, gla_norm, ret_norm, hgrn_norm, w_out,
                            ln1_g, ln1_b, ln2_g, ln2_b, peer_w_q, peer_subkeys, peer_u, peer_v,
                            ple_proj, ple_gate)
        lw.update(cos_s=cos_s, sin_s=sin_s, gamma=gamma)

        og, sg = _prompt_mixer_call(
            _gla_prompt_kernel, xp, bp, lp, lw["w_gla"],
            [lw["wgate"], lw["bgate"], lw["gla_norm"], tri, masks],
            [lw["w_gla"].shape[1], GROUP_W], "gla_prompt")
        orr, sr = _prompt_mixer_call(
            _ret_prompt_kernel, xp, bp, lp, lw["w_ret"],
            [(cos_p, None), (sin_p, None), lw["ret_norm"], dmat, qdec, kdec, sdec],
            [4 * GROUP_W], "ret_prompt")
        oh, sh = _prompt_mixer_call(
            _hgrn_prompt_kernel, xp, bp, lp, lw["w_hgrn"],
            [lw["loglb"], lw["l1mlb"], lw["hgrn_norm"], tri, masks],
            [4 * GROUP_W], "hgrn_prompt")
        xp = _peer_call(_out_call(og, orr, oh, xp, lw), p_prompt[i].reshape(bp * lp, PLE_DIM), lw)
        gla_p.append(_unslot_state(sg, GLA_DK, GLA_DV))
        ret_p.append(_unslot_state(sr, RET_DK, RET_DV, rope=True))
        hgrn_p.append(_unslot_state(sh, HGRN_DK, HGRN_DV))

        o_s, nsg, nsr, nsh = _sample_mixer_call(xs, lw, state_gla[i], state_ret[i], state_hgrn[i])
        xs = _peer_call(
            _out_call(o_s[:, :GROUP_W], o_s[:, GROUP_W:2 * GROUP_W], o_s[:, 2 * GROUP_W:], xs, lw),
            p_sample[i].reshape(bs, PLE_DIM), lw)
        gla_s.append(nsg.reshape(bs, N_HEADS, GLA_DK, GLA_DV))
        ret_s.append(nsr.reshape(bs, N_HEADS, RET_DK, RET_DV))
        hgrn_s.append(nsh.reshape(bs, N_HEADS, HGRN_DK, HGRN_DV))

    return (xp.reshape(bp, lp, D_MODEL), xs.reshape(bs, 1, D_MODEL),
            jnp.stack(gla_p), jnp.stack(ret_p), jnp.stack(hgrn_p),
            jnp.stack(gla_s), jnp.stack(ret_s), jnp.stack(hgrn_s))
```

```python
import functools

import numpy as np
import jax
import jax.numpy as jnp
from jax import lax
from jax.experimental import pallas as pl
from jax.experimental.pallas import tpu as pltpu

F32 = jnp.float32
BF16 = jnp.bfloat16
HIGHEST = lax.Precision.HIGHEST

D_MODEL = 1024
DEPTH = 2
PAST_LEN = 16384
N_HEADS = 4
GLA_DK, GLA_DV = 48, 96
RET_DK, RET_DV = 48, 96
HGRN_DK, HGRN_DV = 64, 64
GLA_LOWRANK = 16
GLA_TAU = 16.0
ROPE_BASE = 10000.0
CHUNK = 64
CHUNK_UNROLL = 2
PEER_HEADS = 8
N_KEYS = 128
PEER_TOPK = 16
PEER_TE = 1024
PLE_DIM = 256
ALPHA = (2 * DEPTH) ** 0.25
NORM_EPS = 1e-5

LANE = 128
SUBLANE = 8
SLOT = LANE
GROUP_W = N_HEADS * SLOT
VMEM_LIMIT = 56 * 1024 * 1024
NEG_INF = float("-inf")


def _dot(a, b, precision=None):
    return jnp.dot(a, b, preferred_element_type=F32, precision=precision)


def _dot_nt(a, b):
    return lax.dot_general(a, b, (((1,), (1,)), ((), ())), preferred_element_type=F32)


def _dot_tn(a, b):
    return lax.dot_general(a, b, (((0,), (0,)), ((), ())), preferred_element_type=F32)


def _sigmoid(x):
    return jax.nn.sigmoid(x)


def _silu(x):
    return x * _sigmoid(x)


def _log_sigmoid(x):
    return jnp.minimum(x, 0.0) - jnp.log1p(jnp.exp(-jnp.abs(x)))


def _logaddexp(a, c):
    amax = jnp.maximum(a, c)
    delta = a - c
    return jnp.where(jnp.isnan(delta), a + c, amax + jnp.log1p(jnp.exp(-jnp.abs(delta))))


def _gelu(x):
    return 0.5 * x * (1.0 + lax.erf(x * np.float32(0.7071067811865476)))


def _lane_mask(n):
    return (lax.broadcasted_iota(jnp.int32, (1, LANE), 1) < n).astype(F32)


def _rms_head_norm(o, g_row, dv):
    ms = jnp.sum(o * o, axis=-1, keepdims=True) * (1.0 / dv)
    return o * lax.rsqrt(ms + NORM_EPS) * g_row


def _group_head_norm(o, g_row, dv):
    mask = _lane_mask(dv)
    mu = jnp.sum(o, axis=-1, keepdims=True) * (1.0 / dv)
    d = (o - mu) * mask
    var = jnp.sum(d * d, axis=-1, keepdims=True) * (1.0 / dv)
    return d * lax.rsqrt(var + NORM_EPS) * g_row


def _chunk_constants():
    i = np.arange(CHUNK)[:, None]
    t = np.arange(CHUNK)[None, :]
    masks = [i == t]
    half = CHUNK // 2
    while half >= 1:
        blk = i // (2 * half)
        second = (i % (2 * half)) >= half
        masks.append(second & ((t % (2 * half)) < half) & (blk == t // (2 * half)))
        half //= 2
    return (t <= i).astype(np.float32), np.stack(masks).astype(np.float32)


def _cumsum_rows(tri_bf, g):
    hi = g.astype(BF16)
    r1 = g - hi.astype(F32)
    mid = r1.astype(BF16)
    lo = (r1 - mid.astype(F32)).astype(BF16)
    return _dot(tri_bf, hi) + _dot(tri_bf, mid) + _dot(tri_bf, lo)


def _level_factors(b, g):
    row = lax.broadcasted_iota(jnp.int32, b.shape, 0)
    sub = lax.broadcasted_iota(jnp.int32, (CHUNK // SUBLANE, SUBLANE, LANE), 1)
    b3 = b.reshape(CHUNK // SUBLANE, SUBLANE, LANE)

    def sub_ref(r):
        return jnp.broadcast_to(b3[:, r:r + 1, :], b3.shape)

    out = []
    half = CHUNK // 2
    while half >= SUBLANE:
        ref = jnp.concatenate(
            [jnp.broadcast_to(b[m * 2 * half + half - 1:m * 2 * half + half], (2 * half, LANE))
             for m in range(CHUNK // (2 * half))], axis=0)
        out.append(jnp.exp(-jnp.abs(b - ref)))
        half //= 2
    out.append(jnp.exp(-jnp.abs(b3 - sub_ref(3))).reshape(b.shape))
    out.append(jnp.exp(-jnp.abs(b3 - jnp.where(sub < 4, sub_ref(1), sub_ref(5)))).reshape(b.shape))
    out.append(jnp.exp(jnp.where(row % 2 == 1, g, 0.0)))
    return out


def _vector_decay_chunk(q, k, v, g, st, tri_bf, masks_ref):
    b = _cumsum_rows(tri_bf, g)
    b_last = b[CHUNK - 1:CHUNK]
    kb = k.astype(BF16)
    sc = masks_ref[0] * _dot_nt(q.astype(BF16), kb)
    for l, f in enumerate(_level_factors(b, g)):
        sc = sc + masks_ref[1 + l] * _dot_nt((q * f).astype(BF16), (k * f).astype(BF16))
    vb = v.astype(BF16)
    o = _dot(sc.astype(BF16), vb) + _dot_nt((q * jnp.exp(b)).astype(BF16), st.astype(BF16))
    st_new = st * jnp.exp(b_last) + _dot_tn(vb, (k * jnp.exp(b_last - b)).astype(BF16))
    return o, st_new


def _gla_prompt_kernel(x_ref, w_ref, wg_ref, bg_ref, nrm_ref, tri_ref, masks_ref,
                       o_ref, st_ref, z_scr, la_scr, *, seg):
    @pl.when(pl.program_id(1) == 0)
    def _():
        st_ref[...] = jnp.zeros_like(st_ref)

    z_scr[...] = _dot(x_ref[...].astype(BF16), w_ref[...])
    pre = _dot(z_scr[:, 4 * GROUP_W:4 * GROUP_W + SLOT], wg_ref[...], precision=HIGHEST) + bg_ref[...]
    la_scr[...] = _log_sigmoid(pre) * (1.0 / GLA_TAU)
    tri = tri_ref[...]

    def body(c, carry):
        r0 = pl.multiple_of(c * CHUNK, CHUNK)
        rows = pl.ds(r0, CHUNK)
        for h in range(N_HEADS):
            q = z_scr[rows, h * SLOT:(h + 1) * SLOT] * (GLA_DK ** -0.5)
            k = z_scr[rows, GROUP_W + h * SLOT:GROUP_W + (h + 1) * SLOT]
            v = z_scr[rows, 2 * GROUP_W + h * SLOT:2 * GROUP_W + (h + 1) * SLOT]
            gate = z_scr[rows, 3 * GROUP_W + h * SLOT:3 * GROUP_W + (h + 1) * SLOT]
            g = la_scr[rows, h * SLOT:(h + 1) * SLOT]
            o, st_new = _vector_decay_chunk(q, k, v, g, st_ref[0, h], tri, masks_ref)
            st_ref[0, h] = st_new
            on = _rms_head_norm(o, nrm_ref[h:h + 1, :], GLA_DV)
            o_ref[rows, h * SLOT:(h + 1) * SLOT] = (on * _silu(gate)).astype(BF16)
        return carry

    lax.fori_loop(0, seg // CHUNK, body, 0, unroll=CHUNK_UNROLL)


def _hgrn_prompt_kernel(x_ref, w_ref, loglb_ref, l1mlb_ref, nrm_ref, tri_ref, masks_ref,
                        o_ref, st_ref, z_scr, *, seg):
    @pl.when(pl.program_id(1) == 0)
    def _():
        st_ref[...] = jnp.zeros_like(st_ref)

    z_scr[...] = _dot(x_ref[...].astype(BF16), w_ref[...])
    tri = tri_ref[...]
    kmask = _lane_mask(HGRN_DK)

    def body(c, carry):
        r0 = pl.multiple_of(c * CHUNK, CHUNK)
        rows = pl.ds(r0, CHUNK)
        for h in range(N_HEADS):
            sl = slice(h * SLOT, (h + 1) * SLOT)
            q = _silu(z_scr[rows, h * SLOT:(h + 1) * SLOT])
            hf = z_scr[rows, GROUP_W + h * SLOT:GROUP_W + (h + 1) * SLOT]
            v = z_scr[rows, 2 * GROUP_W + h * SLOT:2 * GROUP_W + (h + 1) * SLOT]
            gate = z_scr[rows, 3 * GROUP_W + h * SLOT:3 * GROUP_W + (h + 1) * SLOT]
            log_f = _logaddexp(loglb_ref[:, sl], l1mlb_ref[:, sl] + _log_sigmoid(hf))
            k = (1.0 - jnp.exp(log_f)) * kmask
            o, st_new = _vector_decay_chunk(q, k, v, log_f, st_ref[0, h], tri, masks_ref)
            st_ref[0, h] = st_new
            on = _rms_head_norm(o, nrm_ref[h:h + 1, :], HGRN_DV)
            o_ref[rows, sl] = (on * _silu(gate)).astype(BF16)
        return carry

    lax.fori_loop(0, seg // CHUNK, body, 0, unroll=CHUNK_UNROLL)


def _rotate(t, cs, sn):
    return t * cs + pltpu.roll(t, SLOT // 2, 1) * sn


def _ret_prompt_kernel(x_ref, w_ref, cos_ref, sin_ref, nrm_ref, dmat_ref, qdec_ref, kdec_ref, sdec_ref,
                       o_ref, st_ref, z_scr, *, seg):
    @pl.when(pl.program_id(1) == 0)
    def _():
        st_ref[...] = jnp.zeros_like(st_ref)

    z_scr[...] = _dot(x_ref[...].astype(BF16), w_ref[...])

    def body(c, carry):
        r0 = pl.multiple_of(c * CHUNK, CHUNK)
        rows = pl.ds(r0, CHUNK)
        cs = cos_ref[rows, :]
        sn = sin_ref[rows, :]
        for h in range(N_HEADS):
            sl = slice(h * SLOT, (h + 1) * SLOT)
            q = _rotate(z_scr[rows, h * SLOT:(h + 1) * SLOT], cs, sn)
            k = _rotate(z_scr[rows, GROUP_W + h * SLOT:GROUP_W + (h + 1) * SLOT], cs, sn) * (RET_DK ** -0.5)
            v = z_scr[rows, 2 * GROUP_W + h * SLOT:2 * GROUP_W + (h + 1) * SLOT]
            gate = z_scr[rows, 3 * GROUP_W + h * SLOT:3 * GROUP_W + (h + 1) * SLOT]
            st = st_ref[0, h]
            vb = v.astype(BF16)
            sc = dmat_ref[h] * _dot_nt(q.astype(BF16), k.astype(BF16))
            o = _dot(sc.astype(BF16), vb) + _dot_nt((q * qdec_ref[h]).astype(BF16), st.astype(BF16))
            st_ref[0, h] = st * sdec_ref[h] + _dot_tn(vb, (k * kdec_ref[h]).astype(BF16))
            on = _group_head_norm(o, nrm_ref[h:h + 1, :], RET_DV)
            o_ref[rows, sl] = (on * _silu(gate)).astype(BF16)
        return carry

    lax.fori_loop(0, seg // CHUNK, body, 0, unroll=CHUNK_UNROLL)


def _const_spec(shape):
    nd = len(shape)
    return pl.BlockSpec(shape, lambda *_: (0,) * nd)


def _prompt_mixer_call(kernel, x2d, batch, seq, w, extras, scratch_widths, name):
    seg = min(512, seq)
    nseg = seq // seg
    in_specs = [pl.BlockSpec((seg, D_MODEL), lambda b, s: (b * nseg + s, 0)), _const_spec(w.shape)]
    args = [x2d, w]
    for e in extras:
        if isinstance(e, tuple):
            arr, _ = e
            in_specs.append(pl.BlockSpec((seg, arr.shape[1]), lambda b, s: (s, 0)))
            args.append(arr)
        else:
            in_specs.append(_const_spec(e.shape))
            args.append(e)
    return pl.pallas_call(
        functools.partial(kernel, seg=seg),
        grid=(batch, nseg),
        in_specs=in_specs,
        out_specs=[pl.BlockSpec((seg, GROUP_W), lambda b, s: (b * nseg + s, 0)),
                   pl.BlockSpec((1, N_HEADS, SLOT, SLOT), lambda b, s: (b, 0, 0, 0))],
        out_shape=[jax.ShapeDtypeStruct((batch * seq, GROUP_W), BF16),
                   jax.ShapeDtypeStruct((batch, N_HEADS, SLOT, SLOT), F32)],
        scratch_shapes=[pltpu.VMEM((seg, wd), F32) for wd in scratch_widths],
        compiler_params=pltpu.CompilerParams(dimension_semantics=("arbitrary", "arbitrary"),
                                             vmem_limit_bytes=VMEM_LIMIT),
        name=name,
    )(*args)


def _sample_step(q, k, eg, v, s_ref, ns_ref, tq, tk, te, tv, to, s_t, sn_t, dk, dv, row_of):
    tq[...] = q.T
    tk[...] = k.T
    te[...] = eg.T
    tv[...] = v.T
    s_t[0:dk * dv, :] = s_ref[...].T
    vt = tv[0:dv, :]

    def body(kk, oacc):
        kr = row_of(kk)
        r = pl.multiple_of(kk * dv, SUBLANE)
        sn = s_t[pl.ds(r, dv), :] * te[pl.ds(kr, 1), :] + tk[pl.ds(kr, 1), :] * vt
        sn_t[pl.ds(r, dv), :] = sn
        return oacc + tq[pl.ds(kr, 1), :] * sn

    o_t = lax.fori_loop(0, dk, body, jnp.zeros((dv, q.shape[0]), F32))
    ns_ref[...] = sn_t[0:dk * dv, :].T
    to[...] = jnp.zeros_like(to)
    to[0:dv, :] = o_t
    return to[...].T


def _sample_mixer_kernel(x_ref, wg_ref, wr_ref, wh_ref, wgate_ref, bgate_ref, gn_ref, rn_ref, hn_ref,
                         cos_ref, sin_ref, gam_ref, loglb_ref, l1mlb_ref, sg_ref, sr_ref, sh_ref,
                         o_ref, nsg_ref, nsr_ref, nsh_ref,
                         zg, zr, zh, la, tq, tk, te, tv, to, s_t, sn_t):
    h = pl.program_id(0)

    @pl.when(h == 0)
    def _():
        xb = x_ref[...].astype(BF16)
        zg[...] = _dot(xb, wg_ref[...])
        zr[...] = _dot(xb, wr_ref[...])
        zh[...] = _dot(xb, wh_ref[...])
        pre = _dot(zg[:, 4 * GROUP_W:4 * GROUP_W + SLOT], wgate_ref[...], precision=HIGHEST) + bgate_ref[...]
        la[...] = _log_sigmoid(pre) * (1.0 / GLA_TAU)

    off = pl.multiple_of(h * SLOT, SLOT)
    sl = pl.ds(off, SLOT)
    tr = (tq, tk, te, tv, to, s_t, sn_t)
    batch = x_ref.shape[0]

    q = zg[:, sl] * (GLA_DK ** -0.5)
    k = zg[:, pl.ds(GROUP_W + off, SLOT)]
    v = zg[:, pl.ds(2 * GROUP_W + off, SLOT)]
    gate = zg[:, pl.ds(3 * GROUP_W + off, SLOT)]
    o = _sample_step(q, k, jnp.exp(la[:, sl]), v, sg_ref, nsg_ref, *tr, GLA_DK, GLA_DV, lambda kk: kk)
    o_ref[:, sl] = (_rms_head_norm(o, gn_ref[pl.ds(h, 1), :], GLA_DV) * _silu(gate)).astype(BF16)

    cs = cos_ref[...]
    sn = sin_ref[...]
    q = _rotate(zr[:, sl], cs, sn)
    k = _rotate(zr[:, pl.ds(GROUP_W + off, SLOT)], cs, sn) * (RET_DK ** -0.5)
    v = zr[:, pl.ds(2 * GROUP_W + off, SLOT)]
    gate = zr[:, pl.ds(3 * GROUP_W + off, SLOT)]
    eg = jnp.broadcast_to(gam_ref[h], (batch, SLOT))
    half = RET_DK // 2
    o = _sample_step(q, k, eg, v, sr_ref, nsr_ref, *tr, RET_DK, RET_DV,
                     lambda kk: kk + jnp.where(kk >= half, SLOT // 2 - half, 0))
    o_ref[:, pl.ds(GROUP_W + off, SLOT)] = (
        _group_head_norm(o, rn_ref[pl.ds(h, 1), :], RET_DV) * _silu(gate)).astype(BF16)

    q = _silu(zh[:, sl])
    hf = zh[:, pl.ds(GROUP_W + off, SLOT)]
    v = zh[:, pl.ds(2 * GROUP_W + off, SLOT)]
    gate = zh[:, pl.ds(3 * GROUP_W + off, SLOT)]
    log_f = _logaddexp(loglb_ref[:, sl], l1mlb_ref[:, sl] + _log_sigmoid(hf))
    f = jnp.exp(log_f)
    o = _sample_step(q, (1.0 - f) * _lane_mask(HGRN_DK), f, v, sh_ref, nsh_ref, *tr,
                     HGRN_DK, HGRN_DV, lambda kk: kk)
    o_ref[:, pl.ds(2 * GROUP_W + off, SLOT)] = (
        _rms_head_norm(o, hn_ref[pl.ds(h, 1), :], HGRN_DV) * _silu(gate)).astype(BF16)


def _sample_mixer_call(x2d, lw, sg, sr, sh):
    batch = x2d.shape[0]
    gsz, hsz = GLA_DK * GLA_DV, HGRN_DK * HGRN_DV
    consts = [lw["w_gla"], lw["w_ret"], lw["w_hgrn"], lw["wgate"], lw["bgate"], lw["gla_norm"], lw["ret_norm"],
              lw["hgrn_norm"], lw["cos_s"], lw["sin_s"], lw["gamma"], lw["loglb"], lw["l1mlb"]]
    in_specs = ([_const_spec(x2d.shape)] + [_const_spec(c.shape) for c in consts]
                + [pl.BlockSpec((batch, gsz), lambda h: (0, h)),
                   pl.BlockSpec((batch, gsz), lambda h: (0, h)),
                   pl.BlockSpec((batch, hsz), lambda h: (0, h))])
    out_specs = [_const_spec((batch, 3 * GROUP_W)),
                 pl.BlockSpec((batch, gsz), lambda h: (0, h)),
                 pl.BlockSpec((batch, gsz), lambda h: (0, h)),
                 pl.BlockSpec((batch, hsz), lambda h: (0, h))]
    out_shape = [jax.ShapeDtypeStruct((batch, 3 * GROUP_W), BF16),
                 jax.ShapeDtypeStruct((batch, N_HEADS * gsz), F32),
                 jax.ShapeDtypeStruct((batch, N_HEADS * gsz), F32),
                 jax.ShapeDtypeStruct((batch, N_HEADS * hsz), F32)]
    scratch = [pltpu.VMEM((batch, lw["w_gla"].shape[1]), F32), pltpu.VMEM((batch, 4 * GROUP_W), F32),
               pltpu.VMEM((batch, 4 * GROUP_W), F32), pltpu.VMEM((batch, GROUP_W), F32)]
    scratch += [pltpu.VMEM((SLOT, batch), F32) for _ in range(5)]
    scratch += [pltpu.VMEM((gsz, batch), F32), pltpu.VMEM((gsz, batch), F32)]
    return pl.pallas_call(
        _sample_mixer_kernel,
        grid=(N_HEADS,),
        in_specs=in_specs, out_specs=out_specs, out_shape=out_shape, scratch_shapes=scratch,
        compiler_params=pltpu.CompilerParams(dimension_semantics=("arbitrary",), vmem_limit_bytes=VMEM_LIMIT),
        name="sample_mixer",
    )(x2d, *consts, sg.reshape(batch, -1), sr.reshape(batch, -1), sh.reshape(batch, -1))


def _out_kernel(og_ref, or_ref, oh_ref, x_ref, wo_ref, g_ref, b_ref, xt_ref):
    mix = _dot(og_ref[...], wo_ref[0]) + _dot(or_ref[...], wo_ref[1]) + _dot(oh_ref[...], wo_ref[2])
    y = ALPHA * x_ref[...] + mix
    mu = jnp.mean(y, axis=-1, keepdims=True)
    d = y - mu
    var = jnp.mean(d * d, axis=-1, keepdims=True)
    xt_ref[...] = (d * lax.rsqrt(var + NORM_EPS) * g_ref[...] + b_ref[...]).T


def _out_call(og, orr, oh, x2d, lw):
    t = x2d.shape[0]
    tm = min(512, t)
    row = lambda i: (i, 0)
    return pl.pallas_call(
        _out_kernel,
        grid=(t // tm,),
        in_specs=[pl.BlockSpec((tm, GROUP_W), row), pl.BlockSpec((tm, GROUP_W), row),
                  pl.BlockSpec((tm, GROUP_W), row), pl.BlockSpec((tm, D_MODEL), row),
                  _const_spec(lw["w_out"].shape), _const_spec((1, D_MODEL)), _const_spec((1, D_MODEL))],
        out_specs=pl.BlockSpec((D_MODEL, tm), lambda i: (0, i)),
        out_shape=jax.ShapeDtypeStruct((D_MODEL, t), F32),
        compiler_params=pltpu.CompilerParams(dimension_semantics=("arbitrary",), vmem_limit_bytes=VMEM_LIMIT),
        name="out_proj_ln",
    )(og, orr, oh, x2d, lw["w_out"], lw["ln1_g"], lw["ln1_b"])


def _oddeven_merge(lo, hi, r):
    step = r * 2
    if step < hi - lo:
        yield from _oddeven_merge(lo, hi, step)
        yield from _oddeven_merge(lo + r, hi, step)
        yield from [(i, i + r) for i in range(lo + r, hi - r, step)]
    else:
        yield (lo, lo + r)


def _oddeven_merge_sort(lo, hi):
    if hi - lo >= 1:
        mid = lo + (hi - lo) // 2
        yield from _oddeven_merge_sort(lo, mid)
        yield from _oddeven_merge_sort(mid + 1, hi)
        yield from _oddeven_merge(lo, hi, 1)


_SORT16 = tuple(_oddeven_merge_sort(0, PEER_TOPK - 1))
_BITONIC16 = tuple((i, i + d) for d in (8, 4, 2, 1) for i in range(PEER_TOPK) if i & d == 0)


def _compare_exchange(x, pairs):
    for i, j in pairs:
        x[i], x[j] = jnp.maximum(x[i], x[j]), jnp.minimum(x[i], x[j])
    return x


def _top16_sorted(a3):
    x = _compare_exchange([a3[i] for i in range(PEER_TOPK)], _SORT16)
    for shift in (4, 2, 1):
        y = [pltpu.roll(v, shift, 0) for v in x]
        x = _compare_exchange([jnp.maximum(x[i], y[PEER_TOPK - 1 - i]) for i in range(PEER_TOPK)], _BITONIC16)
    return x


def _sublane_block(rows):
    sub = lax.broadcasted_iota(jnp.int32, (SUBLANE, LANE), 0)
    blk = rows[0]
    for r in range(1, SUBLANE):
        blk = jnp.where(sub == r, rows[r], blk)
    return blk


def _route_tile(a1, a2):
    v1 = _top16_sorted(a1)
    v2 = _top16_sorted(a2)
    sub = lax.broadcasted_iota(jnp.int32, (SUBLANE, LANE), 0)
    v2a = _sublane_block(v2[0:8])
    v2b = _sublane_block(v2[8:16])
    v1b = _sublane_block(v1[8:16])
    cands = [v1[0] + v2a, v1[0] + v2b]
    for r1 in range(1, 8):
        cands.append(jnp.where(sub < PEER_TOPK // (r1 + 1), v1[r1] + v2a, NEG_INF))
    cands.append(v1b + v2[0])
    thr = jnp.full((SUBLANE, LANE), NEG_INF, F32)
    need = jnp.full((SUBLANE, LANE), float(PEER_TOPK), F32)
    cur = cands
    for _ in range(PEER_TOPK):
        m8 = cur[0]
        for blk in cur[1:]:
            m8 = jnp.maximum(m8, blk)
        m = jnp.broadcast_to(jnp.max(m8, axis=0, keepdims=True), (SUBLANE, LANE))
        hits = [blk == m for blk in cur]
        cnt8 = jnp.where(hits[0], 1.0, 0.0)
        for hit in hits[1:]:
            cnt8 = cnt8 + jnp.where(hit, 1.0, 0.0)
        thr = jnp.where(need > 0.0, m, thr)
        need = need - jnp.broadcast_to(jnp.sum(cnt8, axis=0, keepdims=True), (SUBLANE, LANE))
        cur = [jnp.where(hit, NEG_INF, blk) for hit, blk in zip(hits, cur)]
    top = v1[0] + v2[0]
    z8 = jnp.zeros((SUBLANE, LANE), F32)
    for blk in cands:
        z8 = z8 + jnp.where(blk >= thr, jnp.exp(blk - top), 0.0)
    z = jnp.broadcast_to(jnp.sum(z8, axis=0, keepdims=True), (SUBLANE, LANE))
    n1 = jnp.zeros(a1.shape, F32)
    rank2 = jnp.zeros(a2.shape, F32)
    for r in range(PEER_TOPK):
        n1 = n1 + jnp.where(a1 + v2[r] >= thr, 1.0, 0.0)
        rank2 = rank2 + jnp.where(v2[r] > a2, 1.0, 0.0)
    e1 = jnp.exp(a1 - v1[0])
    e2 = jnp.exp(a2 - v2[0]) / z
    return n1, rank2, e1, e2


def _peer_kernel(xt_ref, wqt_ref, sk_ref, u0_ref, u_ref, vt_ref, p_ref, plet_ref, gatet_ref, g2_ref, b2_ref,
                 out_ref, xbf, n1, e1, rk2, e2, h_scr, w_scr, acc, *, tn, te, nj):
    j = pl.program_id(1)
    lane_tiles = tn // LANE

    def put_h(slot, hval):
        for lt in range(lane_tiles):
            h_scr[slot, lt] = hval[:, lt * LANE:(lt + 1) * LANE]

    @pl.when(j == 0)
    def _route():
        xbf[...] = xt_ref[...].astype(BF16)

        def head_body(h, carry):
            r = pl.multiple_of(h * 2 * N_KEYS, 2 * N_KEYS)
            qh = _dot(wqt_ref[pl.ds(r, 2 * N_KEYS), :], xbf[...]).astype(BF16)
            s1h = _dot(sk_ref[h, 0], qh[0:N_KEYS])
            s2h = _dot(sk_ref[h, 1], qh[N_KEYS:2 * N_KEYS])
            for lt in range(lane_tiles):
                groups = (N_KEYS // SUBLANE, SUBLANE, LANE)
                a1 = s1h[:, lt * LANE:(lt + 1) * LANE].reshape(groups)
                a2 = s2h[:, lt * LANE:(lt + 1) * LANE].reshape(groups)
                n1_t, rank2_t, e1_t, e2_t = _route_tile(a1, a2)
                n1[lt, h] = n1_t.reshape(N_KEYS, LANE)
                e1[lt, h] = e1_t.reshape(N_KEYS, LANE)
                rk2[lt, h] = rank2_t.reshape(N_KEYS, LANE).astype(BF16).reshape(rk2.shape[2:])
                e2[lt, h] = e2_t.reshape(N_KEYS, LANE).astype(BF16).reshape(e2.shape[2:])
            return carry

        lax.fori_loop(0, PEER_HEADS, head_body, 0)
        acc[...] = jnp.zeros_like(acc)
        put_h(0, _dot(u0_ref[...], xbf[...]))

    na = te // N_KEYS
    assert na == SUBLANE
    a0 = pl.multiple_of(jnp.minimum(j, nj - 1) * na, SUBLANE)

    packed = rk2.shape[2:]
    row16 = (1, 2 * SUBLANE, LANE)

    def step(cur, nxt, next_h, prev_v, this_w):
        piece = te // lane_tiles
        for lt in range(lane_tiles):
            pr = slice(lt * piece, (lt + 1) * piece)
            if next_h:
                h_piece = _dot(u_ref[pr, :], xbf[...])
                for l2 in range(lane_tiles):
                    h_scr[nxt, l2, pr, :] = h_piece[:, l2 * LANE:(l2 + 1) * LANE]
            if prev_v:
                w_prev = jnp.concatenate([w_scr[nxt, l2, pr, :] for l2 in range(lane_tiles)], axis=1)
                acc[...] += _dot(vt_ref[:, pr], w_prev)
            if not this_w:
                continue
            n1blk = [n1[lt, h, pl.ds(a0, na), :] for h in range(PEER_HEADS)]
            e1blk = [e1[lt, h, pl.ds(a0, na), :] for h in range(PEER_HEADS)]
            for ai in range(0, na, 2):
                gates = [jnp.zeros(packed, BF16), jnp.zeros(packed, BF16)]
                for h in range(PEER_HEADS):
                    rk2h = rk2[lt, h]
                    e2h = e2[lt, h]
                    for d in range(2):
                        n1a = jnp.broadcast_to(n1blk[h][ai + d:ai + d + 1, :], row16[1:]).astype(BF16).reshape(row16)
                        e1a = jnp.broadcast_to(e1blk[h][ai + d:ai + d + 1, :], row16[1:]).astype(BF16).reshape(row16)
                        gates[d] = gates[d] + jnp.where(rk2h < n1a, e2h, jnp.zeros_like(e2h)) * e1a
                for d in range(2):
                    rows = slice((ai + d) * N_KEYS, (ai + d + 1) * N_KEYS)
                    act = _gelu(h_scr[cur, lt, rows, :]).astype(BF16).reshape(packed)
                    w_scr[cur, lt, rows, :] = (gates[d] * act).reshape(N_KEYS, LANE)

    assert nj % 2 == 0 and nj >= 4
    steady = jnp.logical_and(j > 0, j < nj - 1)
    pl.when(j == 0)(functools.partial(step, 0, 1, True, False, True))
    for parity in range(2):
        pl.when(jnp.logical_and(steady, j % 2 == parity))(
            functools.partial(step, parity, 1 - parity, True, True, True))
    pl.when(j == nj - 1)(functools.partial(step, 1, 0, False, True, True))
    pl.when(j == nj)(functools.partial(step, 0, 1, False, True, False))

    @pl.when(j == nj)
    def _finish():
        for lt in range(lane_tiles):
            lanes = slice(lt * LANE, (lt + 1) * LANE)
            y = ALPHA * xt_ref[:, lanes] + acc[:, lanes]
            mu = jnp.mean(y, axis=0, keepdims=True)
            d = y - mu
            var = jnp.mean(d * d, axis=0, keepdims=True)
            yn = d * lax.rsqrt(var + NORM_EPS) * g2_ref[...] + b2_ref[...]
            emb = _dot_nt(plet_ref[...], p_ref[lt * LANE:(lt + 1) * LANE, :].astype(BF16))
            gt = _dot(gatet_ref[...], yn.astype(BF16))
            out_ref[lt * LANE:(lt + 1) * LANE, :] = (yn + emb * _sigmoid(gt)).T


def _peer_call(xt, p2d, lw):
    t = xt.shape[1]
    tn = min(512, t)
    te = PEER_TE
    n_exp = lw["u"].shape[0]
    nj = n_exp // te
    lane_tiles = tn // LANE
    once = dict(pipeline_mode=pl.Buffered(1))
    in_specs = [
        pl.BlockSpec((D_MODEL, tn), lambda i, j: (0, i)),
        pl.BlockSpec(lw["wqt"].shape, lambda i, j: (0, 0), **once),
        pl.BlockSpec(lw["subkeys"].shape, lambda i, j: (0, 0, 0, 0), **once),
        pl.BlockSpec((te, D_MODEL), lambda i, j: (0, 0), **once),
        pl.BlockSpec((te, D_MODEL), lambda i, j: (jnp.minimum(j + 1, nj - 1), 0)),
        pl.BlockSpec((None, D_MODEL, te), lambda i, j: (jnp.maximum(j - 1, 0), 0, 0)),
        pl.BlockSpec((tn, PLE_DIM), lambda i, j: (i, 0)),
        pl.BlockSpec(lw["plet"].shape, lambda i, j: (0, 0), **once),
        pl.BlockSpec(lw["gatet"].shape, lambda i, j: (0, 0), **once),
        pl.BlockSpec((D_MODEL, LANE), lambda i, j: (0, 0), **once),
        pl.BlockSpec((D_MODEL, LANE), lambda i, j: (0, 0), **once),
    ]
    route = (lane_tiles, PEER_HEADS, N_KEYS, LANE)
    route_packed = (lane_tiles, PEER_HEADS, N_KEYS // (2 * SUBLANE), 2 * SUBLANE, LANE)
    scratch = [pltpu.VMEM((D_MODEL, tn), BF16),
               pltpu.VMEM(route, F32), pltpu.VMEM(route, F32),
               pltpu.VMEM(route_packed, BF16), pltpu.VMEM(route_packed, BF16),
               pltpu.VMEM((2, lane_tiles, te, LANE), F32), pltpu.VMEM((2, lane_tiles, te, LANE), BF16),
               pltpu.VMEM((D_MODEL, tn), F32)]
    return pl.pallas_call(
        functools.partial(_peer_kernel, tn=tn, te=te, nj=nj),
        grid=(t // tn, nj + 1),
        in_specs=in_specs,
        out_specs=pl.BlockSpec((tn, D_MODEL), lambda i, j: (i, 0)),
        out_shape=jax.ShapeDtypeStruct((t, D_MODEL), F32),
        scratch_shapes=scratch,
        compiler_params=pltpu.CompilerParams(dimension_semantics=("arbitrary", "arbitrary"),
                                             vmem_limit_bytes=VMEM_LIMIT),
        name="peer_ffn_ln_ple",
    )(xt, lw["wqt"], lw["subkeys"], lw["u"], lw["u"], lw["vt"], p2d, lw["plet"], lw["gatet"], lw["ln2_g"], lw["ln2_b"])


def _head_slots(w, d):
    r = w.shape[0]
    return jnp.pad(w.reshape(r, N_HEADS, d), ((0, 0), (0, 0), (0, SLOT - d))).reshape(r, GROUP_W)


def _rope_slots(w, d):
    r = w.shape[0]
    half = d // 2
    w = w.reshape(r, N_HEADS, 2, half)
    return jnp.pad(w, ((0, 0), (0, 0), (0, 0), (0, SLOT // 2 - half))).reshape(r, GROUP_W)


def _rope_tables(pos):
    half = RET_DK // 2
    inv = 1.0 / (ROPE_BASE ** (jnp.arange(0, RET_DK, 2, dtype=F32) / RET_DK))
    ang = pos[:, None] * inv[None, :]
    pad = ((0, 0), (0, SLOT // 2 - half))
    cos = jnp.pad(jnp.cos(ang), pad)
    sin = jnp.pad(jnp.sin(ang), pad)
    return jnp.concatenate([cos, cos], axis=1), jnp.concatenate([-sin, sin], axis=1)


def _retention_constants():
    log_gamma = jnp.log1p(-jnp.exp2(-5.0 - jnp.arange(N_HEADS, dtype=F32)))
    i = jnp.arange(CHUNK, dtype=F32)
    diff = i[:, None] - i[None, :]
    lg = log_gamma[:, None, None]
    dmat = jnp.where(diff >= 0, jnp.exp(jnp.where(diff >= 0, diff, 0.0) * lg), 0.0)
    qdec = jnp.broadcast_to(jnp.exp((i[None, :, None] + 1.0) * lg), (N_HEADS, CHUNK, SLOT))
    kdec = jnp.broadcast_to(jnp.exp((CHUNK - 1.0 - i[None, :, None]) * lg), (N_HEADS, CHUNK, SLOT))
    sdec = jnp.broadcast_to(jnp.exp(CHUNK * lg), (N_HEADS, 1, SLOT))
    gamma = jnp.broadcast_to(jnp.exp(lg), (N_HEADS, 1, SLOT))
    return dmat, qdec, kdec, sdec, gamma


def _layer_weights(i, lb, w_in, gla_w_gate, gla_b_gate, gla_norm, ret_norm, hgrn_norm, w_out,
                   ln1_g, ln1_b, ln2_g, ln2_b, peer_w_q, peer_subkeys, peer_u, peer_v, ple_proj, ple_gate):
    sizes = (N_HEADS * GLA_DK, N_HEADS * GLA_DK, N_HEADS * GLA_DV, N_HEADS * GLA_DV, GLA_LOWRANK,
             N_HEADS * RET_DK, N_HEADS * RET_DK, N_HEADS * RET_DV, N_HEADS * RET_DV,
             N_HEADS * HGRN_DK, N_HEADS * HGRN_DK, N_HEADS * HGRN_DV, N_HEADS * HGRN_DV)
    offs = [int(c) for c in np.cumsum(sizes)[:-1]]
    gq, gk, gv, gg, glr, rq, rk, rv, rg, hq, hf, hi, hg = jnp.split(w_in[i], offs, axis=1)
    lw = {}
    lw["w_gla"] = jnp.concatenate(
        [_head_slots(gq, GLA_DK), _head_slots(gk, GLA_DK), _head_slots(gv, GLA_DV), _head_slots(gg, GLA_DV),
         jnp.pad(glr, ((0, 0), (0, SLOT - GLA_LOWRANK)))], axis=1).astype(BF16)
    lw["w_ret"] = jnp.concatenate(
        [_rope_slots(rq, RET_DK), _rope_slots(rk, RET_DK), _head_slots(rv, RET_DV), _head_slots(rg, RET_DV)],
        axis=1).astype(BF16)
    lw["w_hgrn"] = jnp.concatenate([_head_slots(w, HGRN_DK) for w in (hq, hf, hi, hg)], axis=1).astype(BF16)
    lw["wgate"] = jnp.pad(_head_slots(gla_w_gate[i], GLA_DK), ((0, SLOT - GLA_LOWRANK), (0, 0)))
    lw["bgate"] = _head_slots(gla_b_gate[i][None, :], GLA_DK)
    lw["gla_norm"] = jnp.pad(gla_norm[i], ((0, 0), (0, SLOT - GLA_DV)))
    lw["ret_norm"] = jnp.pad(ret_norm[i], ((0, 0), (0, SLOT - RET_DV)))
    lw["hgrn_norm"] = jnp.pad(hgrn_norm[i], ((0, 0), (0, SLOT - HGRN_DV)))
    lbi = lb[i].reshape(1, N_HEADS, HGRN_DK)
    pad = ((0, 0), (0, 0), (0, SLOT - HGRN_DK))
    lw["loglb"] = jnp.pad(jnp.log(lbi), pad, constant_values=-1.0).reshape(1, GROUP_W)
    lw["l1mlb"] = jnp.pad(jnp.log1p(-lbi), pad, constant_values=-1.0).reshape(1, GROUP_W)
    wo = w_out[i]
    g_rows, r_rows = N_HEADS * GLA_DV, N_HEADS * RET_DV
    lw["w_out"] = jnp.stack([
        _head_slots(wo[:g_rows].T, GLA_DV).T, _head_slots(wo[g_rows:g_rows + r_rows].T, RET_DV).T,
        _head_slots(wo[g_rows + r_rows:].T, HGRN_DV).T]).astype(BF16)
    lw["ln1_g"], lw["ln1_b"] = ln1_g[i][None, :], ln1_b[i][None, :]
    lw["ln2_g"] = jnp.broadcast_to(ln2_g[i][:, None], (D_MODEL, LANE))
    lw["ln2_b"] = jnp.broadcast_to(ln2_b[i][:, None], (D_MODEL, LANE))
    lw["wqt"] = peer_w_q[i].T.astype(BF16)
    lw["subkeys"] = peer_subkeys[i].astype(BF16)
    lw["u"] = peer_u[i].astype(BF16)
    lw["vt"] = jnp.swapaxes(peer_v[i].astype(BF16).reshape(-1, PEER_TE, D_MODEL), 1, 2)
    lw["plet"] = ple_proj[i].T.astype(BF16)
    lw["gatet"] = ple_gate[i].T.astype(BF16)
    return lw


def _unslot_state(st, dk, dv, rope=False):
    if rope:
        half = dk // 2
        st = jnp.concatenate([st[..., :half], st[..., SLOT // 2:SLOT // 2 + half]], axis=-1)
    return jnp.swapaxes(st[:, :, :dv, :dk], 2, 3)


def kernel(x_prompt, x_sample, p_prompt, p_sample, state_gla, state_ret, state_hgrn, w_in, gla_w_gate,
           gla_b_gate, gla_norm, ret_norm, hgrn_lb_logits, hgrn_norm, w_out, ln1_g, ln1_b, ln2_g, ln2_b,
           peer_w_q, peer_subkeys, peer_u, peer_v, ple_proj, ple_gate):
    bp, lp, _ = x_prompt.shape
    bs = x_sample.shape[0]
    assert x_sample.shape[1] == 1 and lp % CHUNK == 0

    lb = jnp.cumsum(jax.nn.softmax(hgrn_lb_logits.astype(F32), axis=0), axis=0)
    lb = lb - lb[0:1]
    tri_np, masks_np = _chunk_constants()
    tri, masks = jnp.asarray(tri_np, BF16), jnp.asarray(masks_np)
    dmat, qdec, kdec, sdec, gamma = _retention_constants()
    cos_p, sin_p = _rope_tables(jnp.arange(lp, dtype=F32))
    cos_s, sin_s = _rope_tables(PAST_LEN + jnp.arange(1, dtype=F32))

    xp = x_prompt.reshape(bp * lp, D_MODEL)
    xs = x_sample.reshape(bs, D_MODEL)
    gla_p, ret_p, hgrn_p, gla_s, ret_s, hgrn_s = [], [], [], [], [], []
    for i in range(DEPTH):
        lw = _layer_weights(i, lb, w_in, gla_w_gate, gla_b_gate, gla_norm, ret_norm, hgrn_norm, w_out,
                            ln1_g, ln1_b, ln2_g, ln2_b, peer_w_q, peer_subkeys, peer_u, peer_v,
                            ple_proj, ple_gate)
        lw.update(cos_s=cos_s, sin_s=sin_s, gamma=gamma)

        og, sg = _prompt_mixer_call(
            _gla_prompt_kernel, xp, bp, lp, lw["w_gla"],
            [lw["wgate"], lw["bgate"], lw["gla_norm"], tri, masks],
            [lw["w_gla"].shape[1], GROUP_W], "gla_prompt")
        orr, sr = _prompt_mixer_call(
            _ret_prompt_kernel, xp, bp, lp, lw["w_ret"],
            [(cos_p, None), (sin_p, None), lw["ret_norm"], dmat, qdec, kdec, sdec],
            [4 * GROUP_W], "ret_prompt")
        oh, sh = _prompt_mixer_call(
            _hgrn_prompt_kernel, xp, bp, lp, lw["w_hgrn"],
            [lw["loglb"], lw["l1mlb"], lw["hgrn_norm"], tri, masks],
            [4 * GROUP_W], "hgrn_prompt")
        xp = _peer_call(_out_call(og, orr, oh, xp, lw), p_prompt[i].reshape(bp * lp, PLE_DIM), lw)
        gla_p.append(_unslot_state(sg, GLA_DK, GLA_DV))
        ret_p.append(_unslot_state(sr, RET_DK, RET_DV, rope=True))
        hgrn_p.append(_unslot_state(sh, HGRN_DK, HGRN_DV))

        o_s, nsg, nsr, nsh = _sample_mixer_call(xs, lw, state_gla[i], state_ret[i], state_hgrn[i])
        xs = _peer_call(
            _out_call(o_s[:, :GROUP_W], o_s[:, GROUP_W:2 * GROUP_W], o_s[:, 2 * GROUP_W:], xs, lw),
            p_sample[i].reshape(bs, PLE_DIM), lw)
        gla_s.append(nsg.reshape(bs, N_HEADS, GLA_DK, GLA_DV))
        ret_s.append(nsr.reshape(bs, N_HEADS, RET_DK, RET_DV))
        hgrn_s.append(nsh.reshape(bs, N_HEADS, HGRN_DK, HGRN_DV))

    return (xp.reshape(bp, lp, D_MODEL), xs.reshape(bs, 1, D_MODEL),
            jnp.stack(gla_p), jnp.stack(ret_p), jnp.stack(hgrn_p),
            jnp.stack(gla_s), jnp.stack(ret_s), jnp.stack(hgrn_s))
```

```python
import functools

import numpy as np
import jax
import jax.numpy as jnp
from jax import lax
from jax.experimental import pallas as pl
from jax.experimental.pallas import tpu as pltpu

F32 = jnp.float32
BF16 = jnp.bfloat16
HIGHEST = lax.Precision.HIGHEST

D_MODEL = 1024
DEPTH = 2
PAST_LEN = 16384
N_HEADS = 4
GLA_DK, GLA_DV = 48, 96
RET_DK, RET_DV = 48, 96
HGRN_DK, HGRN_DV = 64, 64
GLA_LOWRANK = 16
GLA_TAU = 16.0
ROPE_BASE = 10000.0
CHUNK = 64
CHUNK_UNROLL = 2
PEER_HEADS = 8
N_KEYS = 128
PEER_TOPK = 16
PEER_TE = 1024
PLE_DIM = 256
ALPHA = (2 * DEPTH) ** 0.25
NORM_EPS = 1e-5

LANE = 128
SUBLANE = 8
SLOT = LANE
GROUP_W = N_HEADS * SLOT
VMEM_LIMIT = 56 * 1024 * 1024
NEG_INF = float("-inf")


def _dot(a, b, precision=None):
    return jnp.dot(a, b, preferred_element_type=F32, precision=precision)


def _dot_nt(a, b):
    return lax.dot_general(a, b, (((1,), (1,)), ((), ())), preferred_element_type=F32)


def _dot_tn(a, b):
    return lax.dot_general(a, b, (((0,), (0,)), ((), ())), preferred_element_type=F32)


def _sigmoid(x):
    return jax.nn.sigmoid(x)


def _silu(x):
    return x * _sigmoid(x)


def _log_sigmoid(x):
    return jnp.minimum(x, 0.0) - jnp.log1p(jnp.exp(-jnp.abs(x)))


def _logaddexp(a, c):
    amax = jnp.maximum(a, c)
    delta = a - c
    return jnp.where(jnp.isnan(delta), a + c, amax + jnp.log1p(jnp.exp(-jnp.abs(delta))))


def _gelu(x):
    return 0.5 * x * (1.0 + lax.erf(x * np.float32(0.7071067811865476)))


def _lane_mask(n):
    return (lax.broadcasted_iota(jnp.int32, (1, LANE), 1) < n).astype(F32)


def _rms_head_norm(o, g_row, dv):
    ms = jnp.sum(o * o, axis=-1, keepdims=True) * (1.0 / dv)
    return o * lax.rsqrt(ms + NORM_EPS) * g_row


def _group_head_norm(o, g_row, dv):
    mask = _lane_mask(dv)
    mu = jnp.sum(o, axis=-1, keepdims=True) * (1.0 / dv)
    d = (o - mu) * mask
    var = jnp.sum(d * d, axis=-1, keepdims=True) * (1.0 / dv)
    return d * lax.rsqrt(var + NORM_EPS) * g_row


def _chunk_constants():
    i = np.arange(CHUNK)[:, None]
    t = np.arange(CHUNK)[None, :]
    masks = [i == t]
    half = CHUNK // 2
    while half >= 1:
        blk = i // (2 * half)
        second = (i % (2 * half)) >= half
        masks.append(second & ((t % (2 * half)) < half) & (blk == t // (2 * half)))
        half //= 2
    return (t <= i).astype(np.float32), np.stack(masks).astype(np.float32)


def _cumsum_rows(tri_bf, g):
    hi = g.astype(BF16)
    r1 = g - hi.astype(F32)
    mid = r1.astype(BF16)
    lo = (r1 - mid.astype(F32)).astype(BF16)
    return _dot(tri_bf, hi) + _dot(tri_bf, mid) + _dot(tri_bf, lo)


def _level_factors(b, g):
    row = lax.broadcasted_iota(jnp.int32, b.shape, 0)
    sub = lax.broadcasted_iota(jnp.int32, (CHUNK // SUBLANE, SUBLANE, LANE), 1)
    b3 = b.reshape(CHUNK // SUBLANE, SUBLANE, LANE)

    def sub_ref(r):
        return jnp.broadcast_to(b3[:, r:r + 1, :], b3.shape)

    out = []
    half = CHUNK // 2
    while half >= SUBLANE:
        ref = jnp.concatenate(
            [jnp.broadcast_to(b[m * 2 * half + half - 1:m * 2 * half + half], (2 * half, LANE))
             for m in range(CHUNK // (2 * half))], axis=0)
        out.append(jnp.exp(-jnp.abs(b - ref)))
        half //= 2
    out.append(jnp.exp(-jnp.abs(b3 - sub_ref(3))).reshape(b.shape))
    out.append(jnp.exp(-jnp.abs(b3 - jnp.where(sub < 4, sub_ref(1), sub_ref(5)))).reshape(b.shape))
    out.append(jnp.exp(jnp.where(row % 2 == 1, g, 0.0)))
    return out


def _vector_decay_chunk(q, k, v, g, st, tri_bf, masks_ref):
    b = _cumsum_rows(tri_bf, g)
    b_last = b[CHUNK - 1:CHUNK]
    kb = k.astype(BF16)
    sc = masks_ref[0] * _dot_nt(q.astype(BF16), kb)
    for l, f in enumerate(_level_factors(b, g)):
        sc = sc + masks_ref[1 + l] * _dot_nt((q * f).astype(BF16), (k * f).astype(BF16))
    vb = v.astype(BF16)
    o = _dot(sc.astype(BF16), vb) + _dot_nt((q * jnp.exp(b)).astype(BF16), st.astype(BF16))
    st_new = st * jnp.exp(b_last) + _dot_tn(vb, (k * jnp.exp(b_last - b)).astype(BF16))
    return o, st_new


def _gla_prompt_kernel(x_ref, w_ref, wg_ref, bg_ref, nrm_ref, tri_ref, masks_ref,
                       o_ref, st_ref, z_scr, la_scr, *, seg):
    @pl.when(pl.program_id(1) == 0)
    def _():
        st_ref[...] = jnp.zeros_like(st_ref)

    z_scr[...] = _dot(x_ref[...].astype(BF16), w_ref[...])
    pre = _dot(z_scr[:, 4 * GROUP_W:4 * GROUP_W + SLOT], wg_ref[...], precision=HIGHEST) + bg_ref[...]
    la_scr[...] = _log_sigmoid(pre) * (1.0 / GLA_TAU)
    tri = tri_ref[...]

    def body(c, carry):
        r0 = pl.multiple_of(c * CHUNK, CHUNK)
        rows = pl.ds(r0, CHUNK)
        for h in range(N_HEADS):
            q = z_scr[rows, h * SLOT:(h + 1) * SLOT] * (GLA_DK ** -0.5)
            k = z_scr[rows, GROUP_W + h * SLOT:GROUP_W + (h + 1) * SLOT]
            v = z_scr[rows, 2 * GROUP_W + h * SLOT:2 * GROUP_W + (h + 1) * SLOT]
            gate = z_scr[rows, 3 * GROUP_W + h * SLOT:3 * GROUP_W + (h + 1) * SLOT]
            g = la_scr[rows, h * SLOT:(h + 1) * SLOT]
            o, st_new = _vector_decay_chunk(q, k, v, g, st_ref[0, h], tri, masks_ref)
            st_ref[0, h] = st_new
            on = _rms_head_norm(o, nrm_ref[h:h + 1, :], GLA_DV)
            o_ref[rows, h * SLOT:(h + 1) * SLOT] = (on * _silu(gate)).astype(BF16)
        return carry

    lax.fori_loop(0, seg // CHUNK, body, 0, unroll=CHUNK_UNROLL)


def _hgrn_prompt_kernel(x_ref, w_ref, loglb_ref, l1mlb_ref, nrm_ref, tri_ref, masks_ref,
                        o_ref, st_ref, z_scr, *, seg):
    @pl.when(pl.program_id(1) == 0)
    def _():
        st_ref[...] = jnp.zeros_like(st_ref)

    z_scr[...] = _dot(x_ref[...].astype(BF16), w_ref[...])
    tri = tri_ref[...]
    kmask = _lane_mask(HGRN_DK)

    def body(c, carry):
        r0 = pl.multiple_of(c * CHUNK, CHUNK)
        rows = pl.ds(r0, CHUNK)
        for h in range(N_HEADS):
            sl = slice(h * SLOT, (h + 1) * SLOT)
            q = _silu(z_scr[rows, h * SLOT:(h + 1) * SLOT])
            hf = z_scr[rows, GROUP_W + h * SLOT:GROUP_W + (h + 1) * SLOT]
            v = z_scr[rows, 2 * GROUP_W + h * SLOT:2 * GROUP_W + (h + 1) * SLOT]
            gate = z_scr[rows, 3 * GROUP_W + h * SLOT:3 * GROUP_W + (h + 1) * SLOT]
            log_f = _logaddexp(loglb_ref[:, sl], l1mlb_ref[:, sl] + _log_sigmoid(hf))
            k = (1.0 - jnp.exp(log_f)) * kmask
            o, st_new = _vector_decay_chunk(q, k, v, log_f, st_ref[0, h], tri, masks_ref)
            st_ref[0, h] = st_new
            on = _rms_head_norm(o, nrm_ref[h:h + 1, :], HGRN_DV)
            o_ref[rows, sl] = (on * _silu(gate)).astype(BF16)
        return carry

    lax.fori_loop(0, seg // CHUNK, body, 0, unroll=CHUNK_UNROLL)


def _rotate(t, cs, sn):
    return t * cs + pltpu.roll(t, SLOT // 2, 1) * sn


def _ret_prompt_kernel(x_ref, w_ref, cos_ref, sin_ref, nrm_ref, dmat_ref, qdec_ref, kdec_ref, sdec_ref,
                       o_ref, st_ref, z_scr, *, seg):
    @pl.when(pl.program_id(1) == 0)
    def _():
        st_ref[...] = jnp.zeros_like(st_ref)

    z_scr[...] = _dot(x_ref[...].astype(BF16), w_ref[...])

    def body(c, carry):
        r0 = pl.multiple_of(c * CHUNK, CHUNK)
        rows = pl.ds(r0, CHUNK)
        cs = cos_ref[rows, :]
        sn = sin_ref[rows, :]
        for h in range(N_HEADS):
            sl = slice(h * SLOT, (h + 1) * SLOT)
            q = _rotate(z_scr[rows, h * SLOT:(h + 1) * SLOT], cs, sn)
            k = _rotate(z_scr[rows, GROUP_W + h * SLOT:GROUP_W + (h + 1) * SLOT], cs, sn) * (RET_DK ** -0.5)
            v = z_scr[rows, 2 * GROUP_W + h * SLOT:2 * GROUP_W + (h + 1) * SLOT]
            gate = z_scr[rows, 3 * GROUP_W + h * SLOT:3 * GROUP_W + (h + 1) * SLOT]
            st = st_ref[0, h]
            vb = v.astype(BF16)
            sc = dmat_ref[h] * _dot_nt(q.astype(BF16), k.astype(BF16))
            o = _dot(sc.astype(BF16), vb) + _dot_nt((q * qdec_ref[h]).astype(BF16), st.astype(BF16))
            st_ref[0, h] = st * sdec_ref[h] + _dot_tn(vb, (k * kdec_ref[h]).astype(BF16))
            on = _group_head_norm(o, nrm_ref[h:h + 1, :], RET_DV)
            o_ref[rows, sl] = (on * _silu(gate)).astype(BF16)
        return carry

    lax.fori_loop(0, seg // CHUNK, body, 0, unroll=CHUNK_UNROLL)


def _const_spec(shape):
    nd = len(shape)
    return pl.BlockSpec(shape, lambda *_: (0,) * nd)


def _prompt_mixer_call(kernel, x2d, batch, seq, w, extras, scratch_widths, name):
    seg = min(512, seq)
    nseg = seq // seg
    in_specs = [pl.BlockSpec((seg, D_MODEL), lambda b, s: (b * nseg + s, 0)), _const_spec(w.shape)]
    args = [x2d, w]
    for e in extras:
        if isinstance(e, tuple):
            arr, _ = e
            in_specs.append(pl.BlockSpec((seg, arr.shape[1]), lambda b, s: (s, 0)))
            args.append(arr)
        else:
            in_specs.append(_const_spec(e.shape))
            args.append(e)
    return pl.pallas_call(
        functools.partial(kernel, seg=seg),
        grid=(batch, nseg),
        in_specs=in_specs,
        out_specs=[pl.BlockSpec((seg, GROUP_W), lambda b, s: (b * nseg + s, 0)),
                   pl.BlockSpec((1, N_HEADS, SLOT, SLOT), lambda b, s: (b, 0, 0, 0))],
        out_shape=[jax.ShapeDtypeStruct((batch * seq, GROUP_W), BF16),
                   jax.ShapeDtypeStruct((batch, N_HEADS, SLOT, SLOT), F32)],
        scratch_shapes=[pltpu.VMEM((seg, wd), F32) for wd in scratch_widths],
        compiler_params=pltpu.CompilerParams(dimension_semantics=("arbitrary", "arbitrary"),
                                             vmem_limit_bytes=VMEM_LIMIT),
        name=name,
    )(*args)


def _sample_step(q, k, eg, v, s_ref, ns_ref, tq, tk, te, tv, to, s_t, sn_t, dk, dv, row_of):
    tq[...] = q.T
    tk[...] = k.T
    te[...] = eg.T
    tv[...] = v.T
    s_t[0:dk * dv, :] = s_ref[...].T
    vt = tv[0:dv, :]

    def body(kk, oacc):
        kr = row_of(kk)
        r = pl.multiple_of(kk * dv, SUBLANE)
        sn = s_t[pl.ds(r, dv), :] * te[pl.ds(kr, 1), :] + tk[pl.ds(kr, 1), :] * vt
        sn_t[pl.ds(r, dv), :] = sn
        return oacc + tq[pl.ds(kr, 1), :] * sn

    o_t = lax.fori_loop(0, dk, body, jnp.zeros((dv, q.shape[0]), F32))
    ns_ref[...] = sn_t[0:dk * dv, :].T
    to[...] = jnp.zeros_like(to)
    to[0:dv, :] = o_t
    return to[...].T


def _sample_mixer_kernel(x_ref, wg_ref, wr_ref, wh_ref, wgate_ref, bgate_ref, gn_ref, rn_ref, hn_ref,
                         cos_ref, sin_ref, gam_ref, loglb_ref, l1mlb_ref, sg_ref, sr_ref, sh_ref,
                         o_ref, nsg_ref, nsr_ref, nsh_ref,
                         zg, zr, zh, la, tq, tk, te, tv, to, s_t, sn_t):
    h = pl.program_id(0)

    @pl.when(h == 0)
    def _():
        xb = x_ref[...].astype(BF16)
        zg[...] = _dot(xb, wg_ref[...])
        zr[...] = _dot(xb, wr_ref[...])
        zh[...] = _dot(xb, wh_ref[...])
        pre = _dot(zg[:, 4 * GROUP_W:4 * GROUP_W + SLOT], wgate_ref[...], precision=HIGHEST) + bgate_ref[...]
        la[...] = _log_sigmoid(pre) * (1.0 / GLA_TAU)

    off = pl.multiple_of(h * SLOT, SLOT)
    sl = pl.ds(off, SLOT)
    tr = (tq, tk, te, tv, to, s_t, sn_t)
    batch = x_ref.shape[0]

    q = zg[:, sl] * (GLA_DK ** -0.5)
    k = zg[:, pl.ds(GROUP_W + off, SLOT)]
    v = zg[:, pl.ds(2 * GROUP_W + off, SLOT)]
    gate = zg[:, pl.ds(3 * GROUP_W + off, SLOT)]
    o = _sample_step(q, k, jnp.exp(la[:, sl]), v, sg_ref, nsg_ref, *tr, GLA_DK, GLA_DV, lambda kk: kk)
    o_ref[:, sl] = (_rms_head_norm(o, gn_ref[pl.ds(h, 1), :], GLA_DV) * _silu(gate)).astype(BF16)

    cs = cos_ref[...]
    sn = sin_ref[...]
    q = _rotate(zr[:, sl], cs, sn)
    k = _rotate(zr[:, pl.ds(GROUP_W + off, SLOT)], cs, sn) * (RET_DK ** -0.5)
    v = zr[:, pl.ds(2 * GROUP_W + off, SLOT)]
    gate = zr[:, pl.ds(3 * GROUP_W + off, SLOT)]
    eg = jnp.broadcast_to(gam_ref[h], (batch, SLOT))
    half = RET_DK // 2
    o = _sample_step(q, k, eg, v, sr_ref, nsr_ref, *tr, RET_DK, RET_DV,
                     lambda kk: kk + jnp.where(kk >= half, SLOT // 2 - half, 0))
    o_ref[:, pl.ds(GROUP_W + off, SLOT)] = (
        _group_head_norm(o, rn_ref[pl.ds(h, 1), :], RET_DV) * _silu(gate)).astype(BF16)

    q = _silu(zh[:, sl])
    hf = zh[:, pl.ds(GROUP_W + off, SLOT)]
    v = zh[:, pl.ds(2 * GROUP_W + off, SLOT)]
    gate = zh[:, pl.ds(3 * GROUP_W + off, SLOT)]
    log_f = _logaddexp(loglb_ref[:, sl], l1mlb_ref[:, sl] + _log_sigmoid(hf))
    f = jnp.exp(log_f)
    o = _sample_step(q, (1.0 - f) * _lane_mask(HGRN_DK), f, v, sh_ref, nsh_ref, *tr,
                     HGRN_DK, HGRN_DV, lambda kk: kk)
    o_ref[:, pl.ds(2 * GROUP_W + off, SLOT)] = (
        _rms_head_norm(o, hn_ref[pl.ds(h, 1), :], HGRN_DV) * _silu(gate)).astype(BF16)


def _sample_mixer_call(x2d, lw, sg, sr, sh):
    batch = x2d.shape[0]
    gsz, hsz = GLA_DK * GLA_DV, HGRN_DK * HGRN_DV
    consts = [lw["w_gla"], lw["w_ret"], lw["w_hgrn"], lw["wgate"], lw["bgate"], lw["gla_norm"], lw["ret_norm"],
              lw["hgrn_norm"], lw["cos_s"], lw["sin_s"], lw["gamma"], lw["loglb"], lw["l1mlb"]]
    in_specs = ([_const_spec(x2d.shape)] + [_const_spec(c.shape) for c in consts]
                + [pl.BlockSpec((batch, gsz), lambda h: (0, h)),
                   pl.BlockSpec((batch, gsz), lambda h: (0, h)),
                   pl.BlockSpec((batch, hsz), lambda h: (0, h))])
    out_specs = [_const_spec((batch, 3 * GROUP_W)),
                 pl.BlockSpec((batch, gsz), lambda h: (0, h)),
                 pl.BlockSpec((batch, gsz), lambda h: (0, h)),
                 pl.BlockSpec((batch, hsz), lambda h: (0, h))]
    out_shape = [jax.ShapeDtypeStruct((batch, 3 * GROUP_W), BF16),
                 jax.ShapeDtypeStruct((batch, N_HEADS * gsz), F32),
                 jax.ShapeDtypeStruct((batch, N_HEADS * gsz), F32),
                 jax.ShapeDtypeStruct((batch, N_HEADS * hsz), F32)]
    scratch = [pltpu.VMEM((batch, lw["w_gla"].shape[1]), F32), pltpu.VMEM((batch, 4 * GROUP_W), F32),
               pltpu.VMEM((batch, 4 * GROUP_W), F32), pltpu.VMEM((batch, GROUP_W), F32)]
    scratch += [pltpu.VMEM((SLOT, batch), F32) for _ in range(5)]
    scratch += [pltpu.VMEM((gsz, batch), F32), pltpu.VMEM((gsz, batch), F32)]
    return pl.pallas_call(
        _sample_mixer_kernel,
        grid=(N_HEADS,),
        in_specs=in_specs, out_specs=out_specs, out_shape=out_shape, scratch_shapes=scratch,
        compiler_params=pltpu.CompilerParams(dimension_semantics=("arbitrary",), vmem_limit_bytes=VMEM_LIMIT),
        name="sample_mixer",
    )(x2d, *consts, sg.reshape(batch, -1), sr.reshape(batch, -1), sh.reshape(batch, -1))


def _out_kernel(og_ref, or_ref, oh_ref, x_ref, wo_ref, g_ref, b_ref, xt_ref):
    mix = _dot(og_ref[...], wo_ref[0]) + _dot(or_ref[...], wo_ref[1]) + _dot(oh_ref[...], wo_ref[2])
    y = ALPHA * x_ref[...] + mix
    mu = jnp.mean(y, axis=-1, keepdims=True)
    d = y - mu
    var = jnp.mean(d * d, axis=-1, keepdims=True)
    xt_ref[...] = (d * lax.rsqrt(var + NORM_EPS) * g_ref[...] + b_ref[...]).T


def _out_call(og, orr, oh, x2d, lw):
    t = x2d.shape[0]
    tm = min(512, t)
    row = lambda i: (i, 0)
    return pl.pallas_call(
        _out_kernel,
        grid=(t // tm,),
        in_specs=[pl.BlockSpec((tm, GROUP_W), row), pl.BlockSpec((tm, GROUP_W), row),
                  pl.BlockSpec((tm, GROUP_W), row), pl.BlockSpec((tm, D_MODEL), row),
                  _const_spec(lw["w_out"].shape), _const_spec((1, D_MODEL)), _const_spec((1, D_MODEL))],
        out_specs=pl.BlockSpec((D_MODEL, tm), lambda i: (0, i)),
        out_shape=jax.ShapeDtypeStruct((D_MODEL, t), F32),
        compiler_params=pltpu.CompilerParams(dimension_semantics=("arbitrary",), vmem_limit_bytes=VMEM_LIMIT),
        name="out_proj_ln",
    )(og, orr, oh, x2d, lw["w_out"], lw["ln1_g"], lw["ln1_b"])


def _oddeven_merge(lo, hi, r):
    step = r * 2
    if step < hi - lo:
        yield from _oddeven_merge(lo, hi, step)
        yield from _oddeven_merge(lo + r, hi, step)
        yield from [(i, i + r) for i in range(lo + r, hi - r, step)]
    else:
        yield (lo, lo + r)


def _oddeven_merge_sort(lo, hi):
    if hi - lo >= 1:
        mid = lo + (hi - lo) // 2
        yield from _oddeven_merge_sort(lo, mid)
        yield from _oddeven_merge_sort(mid + 1, hi)
        yield from _oddeven_merge(lo, hi, 1)


_SORT16 = tuple(_oddeven_merge_sort(0, PEER_TOPK - 1))
_BITONIC16 = tuple((i, i + d) for d in (8, 4, 2, 1) for i in range(PEER_TOPK) if i & d == 0)


def _compare_exchange(x, pairs):
    for i, j in pairs:
        x[i], x[j] = jnp.maximum(x[i], x[j]), jnp.minimum(x[i], x[j])
    return x


def _top16_sorted(a3):
    x = _compare_exchange([a3[i] for i in range(PEER_TOPK)], _SORT16)
    for shift in (4, 2, 1):
        y = [pltpu.roll(v, shift, 0) for v in x]
        x = _compare_exchange([jnp.maximum(x[i], y[PEER_TOPK - 1 - i]) for i in range(PEER_TOPK)], _BITONIC16)
    return x


def _sublane_block(rows):
    sub = lax.broadcasted_iota(jnp.int32, (SUBLANE, LANE), 0)
    blk = rows[0]
    for r in range(1, SUBLANE):
        blk = jnp.where(sub == r, rows[r], blk)
    return blk


def _route_tile(a1, a2):
    v1 = _top16_sorted(a1)
    v2 = _top16_sorted(a2)
    sub = lax.broadcasted_iota(jnp.int32, (SUBLANE, LANE), 0)
    v2a = _sublane_block(v2[0:8])
    v2b = _sublane_block(v2[8:16])
    v1b = _sublane_block(v1[8:16])
    cands = [v1[0] + v2a, v1[0] + v2b]
    for r1 in range(1, 8):
        cands.append(jnp.where(sub < PEER_TOPK // (r1 + 1), v1[r1] + v2a, NEG_INF))
    cands.append(v1b + v2[0])
    thr = jnp.full((SUBLANE, LANE), NEG_INF, F32)
    need = jnp.full((SUBLANE, LANE), float(PEER_TOPK), F32)
    cur = cands
    for _ in range(PEER_TOPK):
        m8 = cur[0]
        for blk in cur[1:]:
            m8 = jnp.maximum(m8, blk)
        m = jnp.broadcast_to(jnp.max(m8, axis=0, keepdims=True), (SUBLANE, LANE))
        hits = [blk == m for blk in cur]
        cnt8 = jnp.where(hits[0], 1.0, 0.0)
        for hit in hits[1:]:
            cnt8 = cnt8 + jnp.where(hit, 1.0, 0.0)
        thr = jnp.where(need > 0.0, m, thr)
        need = need - jnp.broadcast_to(jnp.sum(cnt8, axis=0, keepdims=True), (SUBLANE, LANE))
        cur = [jnp.where(hit, NEG_INF, blk) for hit, blk in zip(hits, cur)]
    top = v1[0] + v2[0]
    z8 = jnp.zeros((SUBLANE, LANE), F32)
    for blk in cands:
        z8 = z8 + jnp.where(blk >= thr, jnp.exp(blk - top), 0.0)
    z = jnp.broadcast_to(jnp.sum(z8, axis=0, keepdims=True), (SUBLANE, LANE))
    n1 = jnp.zeros(a1.shape, F32)
    rank2 = jnp.zeros(a2.shape, F32)
    for r in range(PEER_TOPK):
        n1 = n1 + jnp.where(a1 + v2[r] >= thr, 1.0, 0.0)
        rank2 = rank2 + jnp.where(v2[r] > a2, 1.0, 0.0)
    e1 = jnp.exp(a1 - v1[0])
    e2 = jnp.exp(a2 - v2[0]) / z
    return n1, rank2, e1, e2


def _peer_kernel(xt_ref, wqt_ref, sk_ref, u0_ref, u_ref, vt_ref, p_ref, plet_ref, gatet_ref, g2_ref, b2_ref,
                 out_ref, xbf, s_scr, n1, e1, rk2, e2, h_scr, w_scr, acc, *, tn, te, nj):
    j = pl.program_id(1)
    lane_tiles = tn // LANE
    groups = (N_KEYS // SUBLANE, SUBLANE, LANE)

    def lane_slice(lt):
        return pl.ds(pl.multiple_of(lt * LANE, LANE), LANE)

    def put_h(slot, hval):
        for l2 in range(lane_tiles):
            h_scr[slot, l2] = hval[:, l2 * LANE:(l2 + 1) * LANE]

    @pl.when(j == 0)
    def _route():
        xbf[...] = xt_ref[...].astype(BF16)

        def head_body(h, carry):
            r = pl.multiple_of(h * 2 * N_KEYS, 2 * N_KEYS)
            qh = _dot(wqt_ref[pl.ds(r, 2 * N_KEYS), :], xbf[...]).astype(BF16)
            s_scr[0] = _dot(sk_ref[h, 0], qh[0:N_KEYS])
            s_scr[1] = _dot(sk_ref[h, 1], qh[N_KEYS:2 * N_KEYS])

            def lane_body(lt, c):
                lanes = lane_slice(lt)
                n1_t, rank2_t, e1_t, e2_t = _route_tile(s_scr[0, :, lanes].reshape(groups),
                                                        s_scr[1, :, lanes].reshape(groups))
                n1[lt, h] = n1_t.reshape(N_KEYS, LANE)
                e1[lt, h] = e1_t.reshape(N_KEYS, LANE)
                rk2[lt, h] = rank2_t.reshape(N_KEYS, LANE).astype(BF16).reshape(rk2.shape[2:])
                e2[lt, h] = e2_t.reshape(N_KEYS, LANE).astype(BF16).reshape(e2.shape[2:])
                return c

            lax.fori_loop(0, lane_tiles, lane_body, 0)
            return carry

        lax.fori_loop(0, PEER_HEADS, head_body, 0)
        acc[...] = jnp.zeros_like(acc)

        put_h(0, _dot(u0_ref[...], xbf[...]))

    na = te // N_KEYS
    assert na == SUBLANE
    a0 = pl.multiple_of(jnp.minimum(j, nj - 1) * na, SUBLANE)
    cur = j % 2
    nxt = 1 - cur
    packed = rk2.shape[2:]
    row16 = (1, 2 * SUBLANE, LANE)

    @pl.when(j < nj - 1)
    def _next_h():
        put_h(nxt, _dot(u_ref[...], xbf[...]))

    @pl.when(j >= 1)
    def _prev_v():
        w_prev = jnp.concatenate([w_scr[nxt, l2] for l2 in range(lane_tiles)], axis=1)
        acc[...] += _dot(vt_ref[...], w_prev)

    @pl.when(j < nj)
    def _this_w():
        def lane_body(p, carry):
            n1blk = [n1[p, h, pl.ds(a0, na), :] for h in range(PEER_HEADS)]
            e1blk = [e1[p, h, pl.ds(a0, na), :] for h in range(PEER_HEADS)]
            for ai in range(0, na, 2):
                gates = [jnp.zeros(packed, BF16), jnp.zeros(packed, BF16)]
                for h in range(PEER_HEADS):
                    rk2h = rk2[p, h]
                    e2h = e2[p, h]
                    for d in range(2):
                        n1a = jnp.broadcast_to(n1blk[h][ai + d:ai + d + 1, :], row16[1:]).astype(BF16).reshape(row16)
                        e1a = jnp.broadcast_to(e1blk[h][ai + d:ai + d + 1, :], row16[1:]).astype(BF16).reshape(row16)
                        gates[d] = gates[d] + jnp.where(rk2h < n1a, e2h, jnp.zeros_like(e2h)) * e1a
                for d in range(2):
                    erows = slice((ai + d) * N_KEYS, (ai + d + 1) * N_KEYS)
                    act = _gelu(h_scr[cur, p, erows, :]).astype(BF16).reshape(packed)
                    w_scr[cur, p, erows, :] = (gates[d] * act).reshape(N_KEYS, LANE)
            return carry

        lax.fori_loop(0, lane_tiles, lane_body, 0)

    @pl.when(j == nj)
    def _finish():
        def lane_body(lt, c):
            lanes = lane_slice(lt)
            y = ALPHA * xt_ref[:, lanes] + acc[:, lanes]
            mu = jnp.mean(y, axis=0, keepdims=True)
            d = y - mu
            var = jnp.mean(d * d, axis=0, keepdims=True)
            yn = d * lax.rsqrt(var + NORM_EPS) * g2_ref[...] + b2_ref[...]
            emb = _dot_nt(plet_ref[...], p_ref[lanes, :].astype(BF16))
            gt = _dot(gatet_ref[...], yn.astype(BF16))
            out_ref[lanes, :] = (yn + emb * _sigmoid(gt)).T
            return c

        lax.fori_loop(0, lane_tiles, lane_body, 0)


def _peer_call(xt, p2d, lw):
    t = xt.shape[1]
    tn = min(512, t)
    te = PEER_TE
    n_exp = lw["u"].shape[0]
    nj = n_exp // te
    lane_tiles = tn // LANE
    once = dict(pipeline_mode=pl.Buffered(1))
    in_specs = [
        pl.BlockSpec((D_MODEL, tn), lambda i, j: (0, i)),
        pl.BlockSpec(lw["wqt"].shape, lambda i, j: (0, 0), **once),
        pl.BlockSpec(lw["subkeys"].shape, lambda i, j: (0, 0, 0, 0), **once),
        pl.BlockSpec((te, D_MODEL), lambda i, j: (0, 0), **once),
        pl.BlockSpec((te, D_MODEL), lambda i, j: (jnp.minimum(j + 1, nj - 1), 0)),
        pl.BlockSpec((None, D_MODEL, te), lambda i, j: (jnp.maximum(j - 1, 0), 0, 0)),
        pl.BlockSpec((tn, PLE_DIM), lambda i, j: (i, 0)),
        pl.BlockSpec(lw["plet"].shape, lambda i, j: (0, 0), **once),
        pl.BlockSpec(lw["gatet"].shape, lambda i, j: (0, 0), **once),
        pl.BlockSpec((D_MODEL, LANE), lambda i, j: (0, 0), **once),
        pl.BlockSpec((D_MODEL, LANE), lambda i, j: (0, 0), **once),
    ]
    route = (lane_tiles, PEER_HEADS, N_KEYS, LANE)
    route_packed = (lane_tiles, PEER_HEADS, N_KEYS // (2 * SUBLANE), 2 * SUBLANE, LANE)
    scratch = [pltpu.VMEM((D_MODEL, tn), BF16), pltpu.VMEM((2, N_KEYS, tn), F32),
               pltpu.VMEM(route, F32), pltpu.VMEM(route, F32),
               pltpu.VMEM(route_packed, BF16), pltpu.VMEM(route_packed, BF16),
               pltpu.VMEM((2, lane_tiles, te, LANE), F32), pltpu.VMEM((2, lane_tiles, te, LANE), BF16),
               pltpu.VMEM((D_MODEL, tn), F32)]
    return pl.pallas_call(
        functools.partial(_peer_kernel, tn=tn, te=te, nj=nj),
        grid=(t // tn, nj + 1),
        in_specs=in_specs,
        out_specs=pl.BlockSpec((tn, D_MODEL), lambda i, j: (i, 0)),
        out_shape=jax.ShapeDtypeStruct((t, D_MODEL), F32),
        scratch_shapes=scratch,
        compiler_params=pltpu.CompilerParams(dimension_semantics=("arbitrary", "arbitrary"),
                                             vmem_limit_bytes=VMEM_LIMIT),
        name="peer_ffn_ln_ple",
    )(xt, lw["wqt"], lw["subkeys"], lw["u"], lw["u"], lw["vt"], p2d, lw["plet"], lw["gatet"], lw["ln2_g"], lw["ln2_b"])


def _head_slots(w, d):
    r = w.shape[0]
    return jnp.pad(w.reshape(r, N_HEADS, d), ((0, 0), (0, 0), (0, SLOT - d))).reshape(r, GROUP_W)


def _rope_slots(w, d):
    r = w.shape[0]
    half = d // 2
    w = w.reshape(r, N_HEADS, 2, half)
    return jnp.pad(w, ((0, 0), (0, 0), (0, 0), (0, SLOT // 2 - half))).reshape(r, GROUP_W)


def _rope_tables(pos):
    half = RET_DK // 2
    inv = 1.0 / (ROPE_BASE ** (jnp.arange(0, RET_DK, 2, dtype=F32) / RET_DK))
    ang = pos[:, None] * inv[None, :]
    pad = ((0, 0), (0, SLOT // 2 - half))
    cos = jnp.pad(jnp.cos(ang), pad)
    sin = jnp.pad(jnp.sin(ang), pad)
    return jnp.concatenate([cos, cos], axis=1), jnp.concatenate([-sin, sin], axis=1)


def _retention_constants():
    log_gamma = jnp.log1p(-jnp.exp2(-5.0 - jnp.arange(N_HEADS, dtype=F32)))
    i = jnp.arange(CHUNK, dtype=F32)
    diff = i[:, None] - i[None, :]
    lg = log_gamma[:, None, None]
    dmat = jnp.where(diff >= 0, jnp.exp(jnp.where(diff >= 0, diff, 0.0) * lg), 0.0)
    qdec = jnp.broadcast_to(jnp.exp((i[None, :, None] + 1.0) * lg), (N_HEADS, CHUNK, SLOT))
    kdec = jnp.broadcast_to(jnp.exp((CHUNK - 1.0 - i[None, :, None]) * lg), (N_HEADS, CHUNK, SLOT))
    sdec = jnp.broadcast_to(jnp.exp(CHUNK * lg), (N_HEADS, 1, SLOT))
    gamma = jnp.broadcast_to(jnp.exp(lg), (N_HEADS, 1, SLOT))
    return dmat, qdec, kdec, sdec, gamma


def _layer_weights(i, lb, w_in, gla_w_gate, gla_b_gate, gla_norm, ret_norm, hgrn_norm, w_out,
                   ln1_g, ln1_b, ln2_g, ln2_b, peer_w_q, peer_subkeys, peer_u, peer_v, ple_proj, ple_gate):
    sizes = (N_HEADS * GLA_DK, N_HEADS * GLA_DK, N_HEADS * GLA_DV, N_HEADS * GLA_DV, GLA_LOWRANK,
             N_HEADS * RET_DK, N_HEADS * RET_DK, N_HEADS * RET_DV, N_HEADS * RET_DV,
             N_HEADS * HGRN_DK, N_HEADS * HGRN_DK, N_HEADS * HGRN_DV, N_HEADS * HGRN_DV)
    offs = [int(c) for c in np.cumsum(sizes)[:-1]]
    gq, gk, gv, gg, glr, rq, rk, rv, rg, hq, hf, hi, hg = jnp.split(w_in[i], offs, axis=1)
    lw = {}
    lw["w_gla"] = jnp.concatenate(
        [_head_slots(gq, GLA_DK), _head_slots(gk, GLA_DK), _head_slots(gv, GLA_DV), _head_slots(gg, GLA_DV),
         jnp.pad(glr, ((0, 0), (0, SLOT - GLA_LOWRANK)))], axis=1).astype(BF16)
    lw["w_ret"] = jnp.concatenate(
        [_rope_slots(rq, RET_DK), _rope_slots(rk, RET_DK), _head_slots(rv, RET_DV), _head_slots(rg, RET_DV)],
        axis=1).astype(BF16)
    lw["w_hgrn"] = jnp.concatenate([_head_slots(w, HGRN_DK) for w in (hq, hf, hi, hg)], axis=1).astype(BF16)
    lw["wgate"] = jnp.pad(_head_slots(gla_w_gate[i], GLA_DK), ((0, SLOT - GLA_LOWRANK), (0, 0)))
    lw["bgate"] = _head_slots(gla_b_gate[i][None, :], GLA_DK)
    lw["gla_norm"] = jnp.pad(gla_norm[i], ((0, 0), (0, SLOT - GLA_DV)))
    lw["ret_norm"] = jnp.pad(ret_norm[i], ((0, 0), (0, SLOT - RET_DV)))
    lw["hgrn_norm"] = jnp.pad(hgrn_norm[i], ((0, 0), (0, SLOT - HGRN_DV)))
    lbi = lb[i].reshape(1, N_HEADS, HGRN_DK)
    pad = ((0, 0), (0, 0), (0, SLOT - HGRN_DK))
    lw["loglb"] = jnp.pad(jnp.log(lbi), pad, constant_values=-1.0).reshape(1, GROUP_W)
    lw["l1mlb"] = jnp.pad(jnp.log1p(-lbi), pad, constant_values=-1.0).reshape(1, GROUP_W)
    wo = w_out[i]
    g_rows, r_rows = N_HEADS * GLA_DV, N_HEADS * RET_DV
    lw["w_out"] = jnp.stack([
        _head_slots(wo[:g_rows].T, GLA_DV).T, _head_slots(wo[g_rows:g_rows + r_rows].T, RET_DV).T,
        _head_slots(wo[g_rows + r_rows:].T, HGRN_DV).T]).astype(BF16)
    lw["ln1_g"], lw["ln1_b"] = ln1_g[i][None, :], ln1_b[i][None, :]
    lw["ln2_g"] = jnp.broadcast_to(ln2_g[i][:, None], (D_MODEL, LANE))
    lw["ln2_b"] = jnp.broadcast_to(ln2_b[i][:, None], (D_MODEL, LANE))
    lw["wqt"] = peer_w_q[i].T.astype(BF16)
    lw["subkeys"] = peer_subkeys[i].astype(BF16)
    lw["u"] = peer_u[i].astype(BF16)
    lw["vt"] = jnp.swapaxes(peer_v[i].astype(BF16).reshape(-1, PEER_TE, D_MODEL), 1, 2)
    lw["plet"] = ple_proj[i].T.astype(BF16)
    lw["gatet"] = ple_gate[i].T.astype(BF16)
    return lw


def _unslot_state(st, dk, dv, rope=False):
    if rope:
        half = dk // 2
        st = jnp.concatenate([st[..., :half], st[..., SLOT // 2:SLOT // 2 + half]], axis=-1)
    return jnp.swapaxes(st[:, :, :dv, :dk], 2, 3)


def kernel(x_prompt, x_sample, p_prompt, p_sample, state_gla, state_ret, state_hgrn, w_in, gla_w_gate,
           gla_b_gate, gla_norm, ret_norm, hgrn_lb_logits, hgrn_norm, w_out, ln1_g, ln1_b, ln2_g, ln2_b,
           peer_w_q, peer_subkeys, peer_u, peer_v, ple_proj, ple_gate):
    bp, lp, _ = x_prompt.shape
    bs = x_sample.shape[0]
    assert x_sample.shape[1] == 1 and lp % CHUNK == 0

    lb = jnp.cumsum(jax.nn.softmax(hgrn_lb_logits.astype(F32), axis=0), axis=0)
    lb = lb - lb[0:1]
    tri_np, masks_np = _chunk_constants()
    tri, masks = jnp.asarray(tri_np, BF16), jnp.asarray(masks_np)
    dmat, qdec, kdec, sdec, gamma = _retention_constants()
    cos_p, sin_p = _rope_tables(jnp.arange(lp, dtype=F32))
    cos_s, sin_s = _rope_tables(PAST_LEN + jnp.arange(1, dtype=F32))

    xp = x_prompt.reshape(bp * lp, D_MODEL)
    xs = x_sample.reshape(bs, D_MODEL)
    gla_p, ret_p, hgrn_p, gla_s, ret_s, hgrn_s = [], [], [], [], [], []
    for i in range(DEPTH):
        lw = _layer_weights(i, lb, w_in, gla_w_gate, gla_b_gate, gla_norm, ret_norm, hgrn_norm, w_out,
                            ln1_g, ln1_b, ln2_g, ln2_b, peer_w_q, peer_subkeys, peer_u, peer_v,
                            ple_proj, ple_gate)
        lw.update(cos_s=cos_s, sin_s=sin_s, gamma=gamma)

        og, sg = _prompt_mixer_call(
            _gla_prompt_kernel, xp, bp, lp, lw["w_gla"],
            [lw["wgate"], lw["bgate"], lw["gla_norm"], tri, masks],
            [lw["w_gla"].shape[1], GROUP_W], "gla_prompt")
        orr, sr = _prompt_mixer_call(
            _ret_prompt_kernel, xp, bp, lp, lw["w_ret"],
            [(cos_p, None), (sin_p, None), lw["ret_norm"], dmat, qdec, kdec, sdec],
            [4 * GROUP_W], "ret_prompt")
        oh, sh = _prompt_mixer_call(
            _hgrn_prompt_kernel, xp, bp, lp, lw["w_hgrn"],
            [lw["loglb"], lw["l1mlb"], lw["hgrn_norm"], tri, masks],
            [4 * GROUP_W], "hgrn_prompt")
        xp = _peer_call(_out_call(og, orr, oh, xp, lw), p_prompt[i].reshape(bp * lp, PLE_DIM), lw)
        gla_p.append(_unslot_state(sg, GLA_DK, GLA_DV))
        ret_p.append(_unslot_state(sr, RET_DK, RET_DV, rope=True))
        hgrn_p.append(_unslot_state(sh, HGRN_DK, HGRN_DV))

        o_s, nsg, nsr, nsh = _sample_mixer_call(xs, lw, state_gla[i], state_ret[i], state_hgrn[i])
        xs = _peer_call(
            _out_call(o_s[:, :GROUP_W], o_s[:, GROUP_W:2 * GROUP_W], o_s[:, 2 * GROUP_W:], xs, lw),
            p_sample[i].reshape(bs, PLE_DIM), lw)
        gla_s.append(nsg.reshape(bs, N_HEADS, GLA_DK, GLA_DV))
        ret_s.append(nsr.reshape(bs, N_HEADS, RET_DK, RET_DV))
        hgrn_s.append(nsh.reshape(bs, N_HEADS, HGRN_DK, HGRN_DV))

    return (xp.reshape(bp, lp, D_MODEL), xs.reshape(bs, 1, D_MODEL),
            jnp.stack(gla_p), jnp.stack(ret_p), jnp.stack(hgrn_p),
            jnp.stack(gla_s), jnp.stack(ret_s), jnp.stack(hgrn_s))
```

```python
import functools

import numpy as np
import jax
import jax.numpy as jnp
from jax import lax
from jax.experimental import pallas as pl
from jax.experimental.pallas import tpu as pltpu

F32 = jnp.float32
BF16 = jnp.bfloat16
HIGHEST = lax.Precision.HIGHEST

D_MODEL = 1024
DEPTH = 2
PAST_LEN = 16384
N_HEADS = 4
GLA_DK, GLA_DV = 48, 96
RET_DK, RET_DV = 48, 96
HGRN_DK, HGRN_DV = 64, 64
GLA_LOWRANK = 16
GLA_TAU = 16.0
ROPE_BASE = 10000.0
CHUNK = 64
CHUNK_UNROLL = 4
PEER_HEADS = 8
N_KEYS = 128
PEER_TOPK = 16
PEER_TE = 1024
PLE_DIM = 256
ALPHA = (2 * DEPTH) ** 0.25
NORM_EPS = 1e-5

LANE = 128
SUBLANE = 8
SLOT = LANE
GROUP_W = N_HEADS * SLOT
VMEM_LIMIT = 56 * 1024 * 1024
NEG_INF = float("-inf")


def _dot(a, b, precision=None):
    return jnp.dot(a, b, preferred_element_type=F32, precision=precision)


def _dot_nt(a, b):
    return lax.dot_general(a, b, (((1,), (1,)), ((), ())), preferred_element_type=F32)


def _dot_tn(a, b):
    return lax.dot_general(a, b, (((0,), (0,)), ((), ())), preferred_element_type=F32)


def _sigmoid(x):
    return jax.nn.sigmoid(x)


def _silu(x):
    return x * _sigmoid(x)


def _log_sigmoid(x):
    return jnp.minimum(x, 0.0) - jnp.log1p(jnp.exp(-jnp.abs(x)))


def _logaddexp(a, c):
    amax = jnp.maximum(a, c)
    delta = a - c
    return jnp.where(jnp.isnan(delta), a + c, amax + jnp.log1p(jnp.exp(-jnp.abs(delta))))


SQRT_HALF = np.float32(0.7071067811865476)


def _lane_mask(n):
    return (lax.broadcasted_iota(jnp.int32, (1, LANE), 1) < n).astype(F32)


def _rms_head_norm(o, g_row, dv):
    ms = jnp.sum(o * o, axis=-1, keepdims=True) * (1.0 / dv)
    return o * lax.rsqrt(ms + NORM_EPS) * g_row


def _group_head_norm(o, g_row, dv):
    mask = _lane_mask(dv)
    mu = jnp.sum(o, axis=-1, keepdims=True) * (1.0 / dv)
    d = (o - mu) * mask
    var = jnp.sum(d * d, axis=-1, keepdims=True) * (1.0 / dv)
    return d * lax.rsqrt(var + NORM_EPS) * g_row


def _chunk_constants():
    i = np.arange(CHUNK)[:, None]
    t = np.arange(CHUNK)[None, :]
    masks = [i == t]
    half = CHUNK // 2
    while half >= 1:
        blk = i // (2 * half)
        second = (i % (2 * half)) >= half
        masks.append(second & ((t % (2 * half)) < half) & (blk == t // (2 * half)))
        half //= 2
    return (t <= i).astype(np.float32), np.stack(masks).astype(np.float32)


def _cumsum_rows(tri_bf, g):
    hi = g.astype(BF16)
    r1 = g - hi.astype(F32)
    mid = r1.astype(BF16)
    lo = (r1 - mid.astype(F32)).astype(BF16)
    return _dot(tri_bf, hi) + _dot(tri_bf, mid) + _dot(tri_bf, lo)


def _level_factors(b, g):
    row = lax.broadcasted_iota(jnp.int32, b.shape, 0)
    sub = lax.broadcasted_iota(jnp.int32, (CHUNK // SUBLANE, SUBLANE, LANE), 1)
    b3 = b.reshape(CHUNK // SUBLANE, SUBLANE, LANE)

    def sub_ref(r):
        return jnp.broadcast_to(b3[:, r:r + 1, :], b3.shape)

    out = []
    half = CHUNK // 2
    while half >= SUBLANE:
        ref = jnp.concatenate(
            [jnp.broadcast_to(b[m * 2 * half + half - 1:m * 2 * half + half], (2 * half, LANE))
             for m in range(CHUNK // (2 * half))], axis=0)
        out.append(jnp.exp(-jnp.abs(b - ref)))
        half //= 2
    out.append(jnp.exp(-jnp.abs(b3 - sub_ref(3))).reshape(b.shape))
    out.append(jnp.exp(-jnp.abs(b3 - jnp.where(sub < 4, sub_ref(1), sub_ref(5)))).reshape(b.shape))
    out.append(jnp.exp(jnp.where(row % 2 == 1, g, 0.0)))
    return out


def _vector_decay_chunk(q, k, v, g, st, tri_bf, masks_ref):
    b = _cumsum_rows(tri_bf, g)
    b_last = b[CHUNK - 1:CHUNK]
    kb = k.astype(BF16)
    sc = masks_ref[0] * _dot_nt(q.astype(BF16), kb)
    for l, f in enumerate(_level_factors(b, g)):
        sc = sc + masks_ref[1 + l] * _dot_nt((q * f).astype(BF16), (k * f).astype(BF16))
    vb = v.astype(BF16)
    o = _dot(sc.astype(BF16), vb) + _dot_nt((q * jnp.exp(b)).astype(BF16), st.astype(BF16))
    st_new = st * jnp.exp(b_last) + _dot_tn(vb, (k * jnp.exp(b_last - b)).astype(BF16))
    return o, st_new


def _gla_prompt_kernel(x_ref, w_ref, wg_ref, bg_ref, nrm_ref, tri_ref, masks_ref,
                       o_ref, st_ref, z_scr, la_scr, *, seg):
    @pl.when(pl.program_id(1) == 0)
    def _():
        st_ref[...] = jnp.zeros_like(st_ref)

    z_scr[...] = _dot(x_ref[...].astype(BF16), w_ref[...])
    pre = _dot(z_scr[:, 4 * GROUP_W:4 * GROUP_W + SLOT], wg_ref[...], precision=HIGHEST) + bg_ref[...]
    la_scr[...] = _log_sigmoid(pre) * (1.0 / GLA_TAU)
    tri = tri_ref[...]

    def body(c, carry):
        r0 = pl.multiple_of(c * CHUNK, CHUNK)
        rows = pl.ds(r0, CHUNK)
        for h in range(N_HEADS):
            q = z_scr[rows, h * SLOT:(h + 1) * SLOT] * (GLA_DK ** -0.5)
            k = z_scr[rows, GROUP_W + h * SLOT:GROUP_W + (h + 1) * SLOT]
            v = z_scr[rows, 2 * GROUP_W + h * SLOT:2 * GROUP_W + (h + 1) * SLOT]
            gate = z_scr[rows, 3 * GROUP_W + h * SLOT:3 * GROUP_W + (h + 1) * SLOT]
            g = la_scr[rows, h * SLOT:(h + 1) * SLOT]
            o, st_new = _vector_decay_chunk(q, k, v, g, st_ref[0, h], tri, masks_ref)
            st_ref[0, h] = st_new
            on = _rms_head_norm(o, nrm_ref[h:h + 1, :], GLA_DV)
            o_ref[rows, h * SLOT:(h + 1) * SLOT] = (on * _silu(gate)).astype(BF16)
        return carry

    lax.fori_loop(0, seg // CHUNK, body, 0, unroll=CHUNK_UNROLL)


def _hgrn_prompt_kernel(x_ref, w_ref, loglb_ref, l1mlb_ref, nrm_ref, tri_ref, masks_ref,
                        o_ref, st_ref, z_scr, *, seg):
    @pl.when(pl.program_id(1) == 0)
    def _():
        st_ref[...] = jnp.zeros_like(st_ref)

    z_scr[...] = _dot(x_ref[...].astype(BF16), w_ref[...])
    tri = tri_ref[...]
    kmask = _lane_mask(HGRN_DK)

    def body(c, carry):
        r0 = pl.multiple_of(c * CHUNK, CHUNK)
        rows = pl.ds(r0, CHUNK)
        for h in range(N_HEADS):
            sl = slice(h * SLOT, (h + 1) * SLOT)
            q = _silu(z_scr[rows, h * SLOT:(h + 1) * SLOT])
            hf = z_scr[rows, GROUP_W + h * SLOT:GROUP_W + (h + 1) * SLOT]
            v = z_scr[rows, 2 * GROUP_W + h * SLOT:2 * GROUP_W + (h + 1) * SLOT]
            gate = z_scr[rows, 3 * GROUP_W + h * SLOT:3 * GROUP_W + (h + 1) * SLOT]
            log_f = _logaddexp(loglb_ref[:, sl], l1mlb_ref[:, sl] + _log_sigmoid(hf))
            k = (1.0 - jnp.exp(log_f)) * kmask
            o, st_new = _vector_decay_chunk(q, k, v, log_f, st_ref[0, h], tri, masks_ref)
            st_ref[0, h] = st_new
            on = _rms_head_norm(o, nrm_ref[h:h + 1, :], HGRN_DV)
            o_ref[rows, sl] = (on * _silu(gate)).astype(BF16)
        return carry

    lax.fori_loop(0, seg // CHUNK, body, 0, unroll=CHUNK_UNROLL)


def _rotate(t, cs, sn):
    return t * cs + pltpu.roll(t, SLOT // 2, 1) * sn


def _ret_prompt_kernel(x_ref, w_ref, cos_ref, sin_ref, nrm_ref, dmat_ref, qdec_ref, kdec_ref, sdec_ref,
                       o_ref, st_ref, z_scr, *, seg):
    @pl.when(pl.program_id(1) == 0)
    def _():
        st_ref[...] = jnp.zeros_like(st_ref)

    z_scr[...] = _dot(x_ref[...].astype(BF16), w_ref[...])

    def body(c, carry):
        r0 = pl.multiple_of(c * CHUNK, CHUNK)
        rows = pl.ds(r0, CHUNK)
        cs = cos_ref[rows, :]
        sn = sin_ref[rows, :]
        for h in range(N_HEADS):
            sl = slice(h * SLOT, (h + 1) * SLOT)
            q = _rotate(z_scr[rows, h * SLOT:(h + 1) * SLOT], cs, sn)
            k = _rotate(z_scr[rows, GROUP_W + h * SLOT:GROUP_W + (h + 1) * SLOT], cs, sn) * (RET_DK ** -0.5)
            v = z_scr[rows, 2 * GROUP_W + h * SLOT:2 * GROUP_W + (h + 1) * SLOT]
            gate = z_scr[rows, 3 * GROUP_W + h * SLOT:3 * GROUP_W + (h + 1) * SLOT]
            st = st_ref[0, h]
            vb = v.astype(BF16)
            sc = dmat_ref[h] * _dot_nt(q.astype(BF16), k.astype(BF16))
            o = _dot(sc.astype(BF16), vb) + _dot_nt((q * qdec_ref[h]).astype(BF16), st.astype(BF16))
            st_ref[0, h] = st * sdec_ref[h] + _dot_tn(vb, (k * kdec_ref[h]).astype(BF16))
            on = _group_head_norm(o, nrm_ref[h:h + 1, :], RET_DV)
            o_ref[rows, sl] = (on * _silu(gate)).astype(BF16)
        return carry

    lax.fori_loop(0, seg // CHUNK, body, 0, unroll=CHUNK_UNROLL)


def _const_spec(shape):
    nd = len(shape)
    return pl.BlockSpec(shape, lambda *_: (0,) * nd)


def _prompt_mixer_call(kernel, x2d, batch, seq, w, extras, scratch_widths, name):
    seg = min(512, seq)
    nseg = seq // seg
    in_specs = [pl.BlockSpec((seg, D_MODEL), lambda b, s: (b * nseg + s, 0)), _const_spec(w.shape)]
    args = [x2d, w]
    for e in extras:
        if isinstance(e, tuple):
            arr, _ = e
            in_specs.append(pl.BlockSpec((seg, arr.shape[1]), lambda b, s: (s, 0)))
            args.append(arr)
        else:
            in_specs.append(_const_spec(e.shape))
            args.append(e)
    return pl.pallas_call(
        functools.partial(kernel, seg=seg),
        grid=(batch, nseg),
        in_specs=in_specs,
        out_specs=[pl.BlockSpec((seg, GROUP_W), lambda b, s: (b * nseg + s, 0)),
                   pl.BlockSpec((1, N_HEADS, SLOT, SLOT), lambda b, s: (b, 0, 0, 0))],
        out_shape=[jax.ShapeDtypeStruct((batch * seq, GROUP_W), BF16),
                   jax.ShapeDtypeStruct((batch, N_HEADS, SLOT, SLOT), F32)],
        scratch_shapes=[pltpu.VMEM((seg, wd), F32) for wd in scratch_widths],
        compiler_params=pltpu.CompilerParams(dimension_semantics=("arbitrary", "arbitrary"),
                                             vmem_limit_bytes=VMEM_LIMIT),
        name=name,
    )(*args)


def _sample_step(q, k, eg, v, s_ref, ns_ref, tq, tk, te, tv, to, s_t, sn_t, dk, dv, row_of):
    tq[...] = q.T
    tk[...] = k.T
    te[...] = eg.T
    tv[...] = v.T
    s_t[0:dk * dv, :] = s_ref[...].T
    vt = tv[0:dv, :]

    def body(kk, oacc):
        kr = row_of(kk)
        r = pl.multiple_of(kk * dv, SUBLANE)
        sn = s_t[pl.ds(r, dv), :] * te[pl.ds(kr, 1), :] + tk[pl.ds(kr, 1), :] * vt
        sn_t[pl.ds(r, dv), :] = sn
        return oacc + tq[pl.ds(kr, 1), :] * sn

    o_t = lax.fori_loop(0, dk, body, jnp.zeros((dv, q.shape[0]), F32))
    ns_ref[...] = sn_t[0:dk * dv, :].T
    to[...] = jnp.zeros_like(to)
    to[0:dv, :] = o_t
    return to[...].T


def _sample_mixer_kernel(x_ref, wg_ref, wr_ref, wh_ref, wgate_ref, bgate_ref, gn_ref, rn_ref, hn_ref,
                         cos_ref, sin_ref, gam_ref, loglb_ref, l1mlb_ref, sg_ref, sr_ref, sh_ref,
                         o_ref, nsg_ref, nsr_ref, nsh_ref,
                         zg, zr, zh, la, tq, tk, te, tv, to, s_t, sn_t):
    h = pl.program_id(0)

    @pl.when(h == 0)
    def _():
        xb = x_ref[...].astype(BF16)
        zg[...] = _dot(xb, wg_ref[...])
        zr[...] = _dot(xb, wr_ref[...])
        zh[...] = _dot(xb, wh_ref[...])
        pre = _dot(zg[:, 4 * GROUP_W:4 * GROUP_W + SLOT], wgate_ref[...], precision=HIGHEST) + bgate_ref[...]
        la[...] = _log_sigmoid(pre) * (1.0 / GLA_TAU)

    off = pl.multiple_of(h * SLOT, SLOT)
    sl = pl.ds(off, SLOT)
    tr = (tq, tk, te, tv, to, s_t, sn_t)
    batch = x_ref.shape[0]

    q = zg[:, sl] * (GLA_DK ** -0.5)
    k = zg[:, pl.ds(GROUP_W + off, SLOT)]
    v = zg[:, pl.ds(2 * GROUP_W + off, SLOT)]
    gate = zg[:, pl.ds(3 * GROUP_W + off, SLOT)]
    o = _sample_step(q, k, jnp.exp(la[:, sl]), v, sg_ref, nsg_ref, *tr, GLA_DK, GLA_DV, lambda kk: kk)
    o_ref[:, sl] = (_rms_head_norm(o, gn_ref[pl.ds(h, 1), :], GLA_DV) * _silu(gate)).astype(BF16)

    cs = cos_ref[...]
    sn = sin_ref[...]
    q = _rotate(zr[:, sl], cs, sn)
    k = _rotate(zr[:, pl.ds(GROUP_W + off, SLOT)], cs, sn) * (RET_DK ** -0.5)
    v = zr[:, pl.ds(2 * GROUP_W + off, SLOT)]
    gate = zr[:, pl.ds(3 * GROUP_W + off, SLOT)]
    eg = jnp.broadcast_to(gam_ref[h], (batch, SLOT))
    half = RET_DK // 2
    o = _sample_step(q, k, eg, v, sr_ref, nsr_ref, *tr, RET_DK, RET_DV,
                     lambda kk: kk + jnp.where(kk >= half, SLOT // 2 - half, 0))
    o_ref[:, pl.ds(GROUP_W + off, SLOT)] = (
        _group_head_norm(o, rn_ref[pl.ds(h, 1), :], RET_DV) * _silu(gate)).astype(BF16)

    q = _silu(zh[:, sl])
    hf = zh[:, pl.ds(GROUP_W + off, SLOT)]
    v = zh[:, pl.ds(2 * GROUP_W + off, SLOT)]
    gate = zh[:, pl.ds(3 * GROUP_W + off, SLOT)]
    log_f = _logaddexp(loglb_ref[:, sl], l1mlb_ref[:, sl] + _log_sigmoid(hf))
    f = jnp.exp(log_f)
    o = _sample_step(q, (1.0 - f) * _lane_mask(HGRN_DK), f, v, sh_ref, nsh_ref, *tr,
                     HGRN_DK, HGRN_DV, lambda kk: kk)
    o_ref[:, pl.ds(2 * GROUP_W + off, SLOT)] = (
        _rms_head_norm(o, hn_ref[pl.ds(h, 1), :], HGRN_DV) * _silu(gate)).astype(BF16)


def _sample_mixer_call(x2d, lw, sg, sr, sh):
    batch = x2d.shape[0]
    gsz, hsz = GLA_DK * GLA_DV, HGRN_DK * HGRN_DV
    consts = [lw["w_gla"], lw["w_ret"], lw["w_hgrn"], lw["wgate"], lw["bgate"], lw["gla_norm"], lw["ret_norm"],
              lw["hgrn_norm"], lw["cos_s"], lw["sin_s"], lw["gamma"], lw["loglb"], lw["l1mlb"]]
    in_specs = ([_const_spec(x2d.shape)] + [_const_spec(c.shape) for c in consts]
                + [pl.BlockSpec((batch, gsz), lambda h: (0, h)),
                   pl.BlockSpec((batch, gsz), lambda h: (0, h)),
                   pl.BlockSpec((batch, hsz), lambda h: (0, h))])
    out_specs = [_const_spec((batch, 3 * GROUP_W)),
                 pl.BlockSpec((batch, gsz), lambda h: (0, h)),
                 pl.BlockSpec((batch, gsz), lambda h: (0, h)),
                 pl.BlockSpec((batch, hsz), lambda h: (0, h))]
    out_shape = [jax.ShapeDtypeStruct((batch, 3 * GROUP_W), BF16),
                 jax.ShapeDtypeStruct((batch, N_HEADS * gsz), F32),
                 jax.ShapeDtypeStruct((batch, N_HEADS * gsz), F32),
                 jax.ShapeDtypeStruct((batch, N_HEADS * hsz), F32)]
    scratch = [pltpu.VMEM((batch, lw["w_gla"].shape[1]), F32), pltpu.VMEM((batch, 4 * GROUP_W), F32),
               pltpu.VMEM((batch, 4 * GROUP_W), F32), pltpu.VMEM((batch, GROUP_W), F32)]
    scratch += [pltpu.VMEM((SLOT, batch), F32) for _ in range(5)]
    scratch += [pltpu.VMEM((gsz, batch), F32), pltpu.VMEM((gsz, batch), F32)]
    return pl.pallas_call(
        _sample_mixer_kernel,
        grid=(N_HEADS,),
        in_specs=in_specs, out_specs=out_specs, out_shape=out_shape, scratch_shapes=scratch,
        compiler_params=pltpu.CompilerParams(dimension_semantics=("arbitrary",), vmem_limit_bytes=VMEM_LIMIT),
        name="sample_mixer",
    )(x2d, *consts, sg.reshape(batch, -1), sr.reshape(batch, -1), sh.reshape(batch, -1))


def _out_kernel(og_ref, or_ref, oh_ref, x_ref, wo_ref, g_ref, b_ref, xt_ref):
    mix = _dot(og_ref[...], wo_ref[0]) + _dot(or_ref[...], wo_ref[1]) + _dot(oh_ref[...], wo_ref[2])
    y = ALPHA * x_ref[...] + mix
    mu = jnp.mean(y, axis=-1, keepdims=True)
    d = y - mu
    var = jnp.mean(d * d, axis=-1, keepdims=True)
    xt_ref[...] = (d * lax.rsqrt(var + NORM_EPS) * g_ref[...] + b_ref[...]).T


def _out_call(og, orr, oh, x2d, lw):
    t = x2d.shape[0]
    tm = min(512, t)
    row = lambda i: (i, 0)
    return pl.pallas_call(
        _out_kernel,
        grid=(t // tm,),
        in_specs=[pl.BlockSpec((tm, GROUP_W), row), pl.BlockSpec((tm, GROUP_W), row),
                  pl.BlockSpec((tm, GROUP_W), row), pl.BlockSpec((tm, D_MODEL), row),
                  _const_spec(lw["w_out"].shape), _const_spec((1, D_MODEL)), _const_spec((1, D_MODEL))],
        out_specs=pl.BlockSpec((D_MODEL, tm), lambda i: (0, i)),
        out_shape=jax.ShapeDtypeStruct((D_MODEL, t), F32),
        compiler_params=pltpu.CompilerParams(dimension_semantics=("arbitrary",), vmem_limit_bytes=VMEM_LIMIT),
        name="out_proj_ln",
    )(og, orr, oh, x2d, lw["w_out"], lw["ln1_g"], lw["ln1_b"])


def _oddeven_merge(lo, hi, r):
    step = r * 2
    if step < hi - lo:
        yield from _oddeven_merge(lo, hi, step)
        yield from _oddeven_merge(lo + r, hi, step)
        yield from [(i, i + r) for i in range(lo + r, hi - r, step)]
    else:
        yield (lo, lo + r)


def _oddeven_merge_sort(lo, hi):
    if hi - lo >= 1:
        mid = lo + (hi - lo) // 2
        yield from _oddeven_merge_sort(lo, mid)
        yield from _oddeven_merge_sort(mid + 1, hi)
        yield from _oddeven_merge(lo, hi, 1)


_SORT16 = tuple(_oddeven_merge_sort(0, PEER_TOPK - 1))
_BITONIC16 = tuple((i, i + d) for d in (8, 4, 2, 1) for i in range(PEER_TOPK) if i & d == 0)


def _compare_exchange(x, pairs):
    for i, j in pairs:
        x[i], x[j] = jnp.maximum(x[i], x[j]), jnp.minimum(x[i], x[j])
    return x


def _top16_sorted(a3):
    x = _compare_exchange([a3[i] for i in range(PEER_TOPK)], _SORT16)
    for shift in (4, 2, 1):
        y = [pltpu.roll(v, shift, 0) for v in x]
        x = _compare_exchange([jnp.maximum(x[i], y[PEER_TOPK - 1 - i]) for i in range(PEER_TOPK)], _BITONIC16)
    return x


def _sublane_block(rows):
    sub = lax.broadcasted_iota(jnp.int32, (SUBLANE, LANE), 0)
    blk = rows[0]
    for r in range(1, SUBLANE):
        blk = jnp.where(sub == r, rows[r], blk)
    return blk


def _route_tile(a1, a2):
    v1 = _top16_sorted(a1)
    v2 = _top16_sorted(a2)
    sub = lax.broadcasted_iota(jnp.int32, (SUBLANE, LANE), 0)
    v2a = _sublane_block(v2[0:8])
    v2b = _sublane_block(v2[8:16])
    v1b = _sublane_block(v1[8:16])
    cands = [v1[0] + v2a, v1[0] + v2b]
    for r1 in range(1, 8):
        cands.append(jnp.where(sub < PEER_TOPK // (r1 + 1), v1[r1] + v2a, NEG_INF))
    cands.append(v1b + v2[0])
    thr = jnp.full((SUBLANE, LANE), NEG_INF, F32)
    need = jnp.full((SUBLANE, LANE), float(PEER_TOPK), F32)
    cur = cands
    for _ in range(PEER_TOPK):
        m8 = cur[0]
        for blk in cur[1:]:
            m8 = jnp.maximum(m8, blk)
        m = jnp.broadcast_to(jnp.max(m8, axis=0, keepdims=True), (SUBLANE, LANE))
        hits = [blk == m for blk in cur]
        cnt8 = jnp.where(hits[0], 1.0, 0.0)
        for hit in hits[1:]:
            cnt8 = cnt8 + jnp.where(hit, 1.0, 0.0)
        thr = jnp.where(need > 0.0, m, thr)
        need = need - jnp.broadcast_to(jnp.sum(cnt8, axis=0, keepdims=True), (SUBLANE, LANE))
        cur = [jnp.where(hit, NEG_INF, blk) for hit, blk in zip(hits, cur)]
    top = v1[0] + v2[0]
    z8 = jnp.zeros((SUBLANE, LANE), F32)
    for blk in cands:
        z8 = z8 + jnp.where(blk >= thr, jnp.exp(blk - top), 0.0)
    z = jnp.broadcast_to(jnp.sum(z8, axis=0, keepdims=True), (SUBLANE, LANE))
    n1 = jnp.zeros(a1.shape, F32)
    rank2 = jnp.zeros(a2.shape, F32)
    for r in range(PEER_TOPK):
        n1 = n1 + jnp.where(a1 + v2[r] >= thr, 1.0, 0.0)
        rank2 = rank2 + jnp.where(v2[r] > a2, 1.0, 0.0)
    e1 = jnp.exp(a1 - v1[0]) * SQRT_HALF
    e2 = jnp.exp(a2 - v2[0]) / z
    return n1, rank2, e1, e2


def _peer_kernel(xt_ref, wqt_ref, sk_ref, u0_ref, u_ref, vt_ref, p_ref, plet_ref, gatet_ref, g2_ref, b2_ref,
                 out_ref, xbf, s_scr, n1, e1, rk2, e2, h_scr, w_scr, acc, *, tn, te, nj):
    j = pl.program_id(1)
    lane_tiles = tn // LANE
    groups = (N_KEYS // SUBLANE, SUBLANE, LANE)

    def lane_slice(lt):
        return pl.ds(pl.multiple_of(lt * LANE, LANE), LANE)

    def put_h(slot, hval):
        for l2 in range(lane_tiles):
            h_scr[slot, l2] = hval[:, l2 * LANE:(l2 + 1) * LANE].astype(BF16)

    @pl.when(j == 0)
    def _route():
        xbf[...] = xt_ref[...].astype(BF16)

        def head_body(h, carry):
            r = pl.multiple_of(h * 2 * N_KEYS, 2 * N_KEYS)
            qh = _dot(wqt_ref[pl.ds(r, 2 * N_KEYS), :], xbf[...]).astype(BF16)
            s_scr[0] = _dot(sk_ref[h, 0], qh[0:N_KEYS])
            s_scr[1] = _dot(sk_ref[h, 1], qh[N_KEYS:2 * N_KEYS])

            def lane_body(lt, c):
                lanes = lane_slice(lt)
                n1_t, rank2_t, e1_t, e2_t = _route_tile(s_scr[0, :, lanes].reshape(groups),
                                                        s_scr[1, :, lanes].reshape(groups))
                n1[lt, h] = n1_t.reshape(N_KEYS, LANE)
                e1[lt, h] = e1_t.reshape(N_KEYS, LANE)
                rk2[lt, h] = rank2_t.reshape(N_KEYS, LANE).astype(BF16).reshape(rk2.shape[2:])
                e2[lt, h] = e2_t.reshape(N_KEYS, LANE).astype(BF16).reshape(e2.shape[2:])
                return c

            lax.fori_loop(0, lane_tiles, lane_body, 0)
            return carry

        lax.fori_loop(0, PEER_HEADS, head_body, 0)
        acc[...] = jnp.zeros_like(acc)

        put_h(0, _dot(u0_ref[...], xbf[...]))

    na = te // N_KEYS
    assert na == SUBLANE
    a0 = pl.multiple_of(jnp.minimum(j, nj - 1) * na, SUBLANE)
    cur = j % 2
    nxt = 1 - cur
    packed = rk2.shape[2:]
    row16 = (1, 2 * SUBLANE, LANE)

    @pl.when(j < nj - 1)
    def _next_h():
        put_h(nxt, _dot(u_ref[...], xbf[...]))

    @pl.when(j >= 1)
    def _prev_v():
        w_prev = jnp.concatenate([w_scr[nxt, l2] for l2 in range(lane_tiles)], axis=1)
        acc[...] += _dot(vt_ref[...], w_prev)

    @pl.when(j < nj)
    def _this_w():
        def lane_body(p, carry):
            n1blk = [n1[p, h, pl.ds(a0, na), :] for h in range(PEER_HEADS)]
            e1blk = [e1[p, h, pl.ds(a0, na), :] for h in range(PEER_HEADS)]
            for ai in range(0, na, 2):
                gates = [jnp.zeros(packed, BF16), jnp.zeros(packed, BF16)]
                for h in range(PEER_HEADS):
                    rk2h = rk2[p, h]
                    e2h = e2[p, h]
                    for d in range(2):
                        n1a = jnp.broadcast_to(n1blk[h][ai + d:ai + d + 1, :], row16[1:]).astype(BF16).reshape(row16)
                        e1a = jnp.broadcast_to(e1blk[h][ai + d:ai + d + 1, :], row16[1:]).astype(BF16).reshape(row16)
                        gates[d] = gates[d] + jnp.where(rk2h < n1a, e2h, jnp.zeros_like(e2h)) * e1a
                for d in range(2):
                    erows = slice((ai + d) * N_KEYS, (ai + d + 1) * N_KEYS)
                    hs = h_scr[cur, p, erows, :].reshape(packed)
                    w_scr[cur, p, erows, :] = (gates[d] * (hs + hs * lax.erf(hs))).reshape(N_KEYS, LANE)
            return carry

        lax.fori_loop(0, lane_tiles, lane_body, 0)

    @pl.when(j == nj)
    def _finish():
        def norm_body(lt, c):
            lanes = lane_slice(lt)
            y = ALPHA * xt_ref[:, lanes] + acc[:, lanes]
            mu = jnp.mean(y, axis=0, keepdims=True)
            d = y - mu
            var = jnp.mean(d * d, axis=0, keepdims=True)
            yn = d * lax.rsqrt(var + NORM_EPS) * g2_ref[...] + b2_ref[...]
            acc[:, lanes] = yn
            xbf[:, lanes] = yn.astype(BF16)
            return c

        lax.fori_loop(0, lane_tiles, norm_body, 0)
        emb = _dot_nt(plet_ref[...], p_ref[...].astype(BF16))
        gt = _dot(gatet_ref[...], xbf[...])
        out_ref[...] = (acc[...] + emb * _sigmoid(gt)).T


def _peer_call(xt, p2d, lw):
    t = xt.shape[1]
    tn = min(512, t)
    te = PEER_TE
    n_exp = lw["u"].shape[0]
    nj = n_exp // te
    lane_tiles = tn // LANE
    once = dict(pipeline_mode=pl.Buffered(1))
    in_specs = [
        pl.BlockSpec((D_MODEL, tn), lambda i, j: (0, i)),
        pl.BlockSpec(lw["wqt"].shape, lambda i, j: (0, 0), **once),
        pl.BlockSpec(lw["subkeys"].shape, lambda i, j: (0, 0, 0, 0), **once),
        pl.BlockSpec((te, D_MODEL), lambda i, j: (0, 0), **once),
        pl.BlockSpec((te, D_MODEL), lambda i, j: (jnp.minimum(j + 1, nj - 1), 0)),
        pl.BlockSpec((None, D_MODEL, te), lambda i, j: (jnp.maximum(j - 1, 0), 0, 0)),
        pl.BlockSpec((tn, PLE_DIM), lambda i, j: (i, 0)),
        pl.BlockSpec(lw["plet"].shape, lambda i, j: (0, 0), **once),
        pl.BlockSpec(lw["gatet"].shape, lambda i, j: (0, 0), **once),
        pl.BlockSpec((D_MODEL, LANE), lambda i, j: (0, 0), **once),
        pl.BlockSpec((D_MODEL, LANE), lambda i, j: (0, 0), **once),
    ]
    route = (lane_tiles, PEER_HEADS, N_KEYS, LANE)
    route_packed = (lane_tiles, PEER_HEADS, N_KEYS // (2 * SUBLANE), 2 * SUBLANE, LANE)
    scratch = [pltpu.VMEM((D_MODEL, tn), BF16), pltpu.VMEM((2, N_KEYS, tn), F32),
               pltpu.VMEM(route, F32), pltpu.VMEM(route, F32),
               pltpu.VMEM(route_packed, BF16), pltpu.VMEM(route_packed, BF16),
               pltpu.VMEM((2, lane_tiles, te, LANE), BF16), pltpu.VMEM((2, lane_tiles, te, LANE), BF16),
               pltpu.VMEM((D_MODEL, tn), F32)]
    return pl.pallas_call(
        functools.partial(_peer_kernel, tn=tn, te=te, nj=nj),
        grid=(t // tn, nj + 1),
        in_specs=in_specs,
        out_specs=pl.BlockSpec((tn, D_MODEL), lambda i, j: (i, 0)),
        out_shape=jax.ShapeDtypeStruct((t, D_MODEL), F32),
        scratch_shapes=scratch,
        compiler_params=pltpu.CompilerParams(dimension_semantics=("arbitrary", "arbitrary"),
                                             vmem_limit_bytes=VMEM_LIMIT),
        name="peer_ffn_ln_ple",
    )(xt, lw["wqt"], lw["subkeys"], lw["u"], lw["u"], lw["vt"], p2d, lw["plet"], lw["gatet"], lw["ln2_g"], lw["ln2_b"])


def _head_slots(w, d):
    r = w.shape[0]
    return jnp.pad(w.reshape(r, N_HEADS, d), ((0, 0), (0, 0), (0, SLOT - d))).reshape(r, GROUP_W)


def _rope_slots(w, d):
    r = w.shape[0]
    half = d // 2
    w = w.reshape(r, N_HEADS, 2, half)
    return jnp.pad(w, ((0, 0), (0, 0), (0, 0), (0, SLOT // 2 - half))).reshape(r, GROUP_W)


def _rope_tables(pos):
    half = RET_DK // 2
    inv = 1.0 / (ROPE_BASE ** (jnp.arange(0, RET_DK, 2, dtype=F32) / RET_DK))
    ang = pos[:, None] * inv[None, :]
    pad = ((0, 0), (0, SLOT // 2 - half))
    cos = jnp.pad(jnp.cos(ang), pad)
    sin = jnp.pad(jnp.sin(ang), pad)
    return jnp.concatenate([cos, cos], axis=1), jnp.concatenate([-sin, sin], axis=1)


def _retention_constants():
    log_gamma = jnp.log1p(-jnp.exp2(-5.0 - jnp.arange(N_HEADS, dtype=F32)))
    i = jnp.arange(CHUNK, dtype=F32)
    diff = i[:, None] - i[None, :]
    lg = log_gamma[:, None, None]
    dmat = jnp.where(diff >= 0, jnp.exp(jnp.where(diff >= 0, diff, 0.0) * lg), 0.0)
    qdec = jnp.broadcast_to(jnp.exp((i[None, :, None] + 1.0) * lg), (N_HEADS, CHUNK, SLOT))
    kdec = jnp.broadcast_to(jnp.exp((CHUNK - 1.0 - i[None, :, None]) * lg), (N_HEADS, CHUNK, SLOT))
    sdec = jnp.broadcast_to(jnp.exp(CHUNK * lg), (N_HEADS, 1, SLOT))
    gamma = jnp.broadcast_to(jnp.exp(lg), (N_HEADS, 1, SLOT))
    return dmat, qdec, kdec, sdec, gamma


def _layer_weights(i, lb, w_in, gla_w_gate, gla_b_gate, gla_norm, ret_norm, hgrn_norm, w_out,
                   ln1_g, ln1_b, ln2_g, ln2_b, peer_w_q, peer_subkeys, peer_u, peer_v, ple_proj, ple_gate):
    sizes = (N_HEADS * GLA_DK, N_HEADS * GLA_DK, N_HEADS * GLA_DV, N_HEADS * GLA_DV, GLA_LOWRANK,
             N_HEADS * RET_DK, N_HEADS * RET_DK, N_HEADS * RET_DV, N_HEADS * RET_DV,
             N_HEADS * HGRN_DK, N_HEADS * HGRN_DK, N_HEADS * HGRN_DV, N_HEADS * HGRN_DV)
    offs = [int(c) for c in np.cumsum(sizes)[:-1]]
    gq, gk, gv, gg, glr, rq, rk, rv, rg, hq, hf, hi, hg = jnp.split(w_in[i], offs, axis=1)
    lw = {}
    lw["w_gla"] = jnp.concatenate(
        [_head_slots(gq, GLA_DK), _head_slots(gk, GLA_DK), _head_slots(gv, GLA_DV), _head_slots(gg, GLA_DV),
         jnp.pad(glr, ((0, 0), (0, SLOT - GLA_LOWRANK)))], axis=1).astype(BF16)
    lw["w_ret"] = jnp.concatenate(
        [_rope_slots(rq, RET_DK), _rope_slots(rk, RET_DK), _head_slots(rv, RET_DV), _head_slots(rg, RET_DV)],
        axis=1).astype(BF16)
    lw["w_hgrn"] = jnp.concatenate([_head_slots(w, HGRN_DK) for w in (hq, hf, hi, hg)], axis=1).astype(BF16)
    lw["wgate"] = jnp.pad(_head_slots(gla_w_gate[i], GLA_DK), ((0, SLOT - GLA_LOWRANK), (0, 0)))
    lw["bgate"] = _head_slots(gla_b_gate[i][None, :], GLA_DK)
    lw["gla_norm"] = jnp.pad(gla_norm[i], ((0, 0), (0, SLOT - GLA_DV)))
    lw["ret_norm"] = jnp.pad(ret_norm[i], ((0, 0), (0, SLOT - RET_DV)))
    lw["hgrn_norm"] = jnp.pad(hgrn_norm[i], ((0, 0), (0, SLOT - HGRN_DV)))
    lbi = lb[i].reshape(1, N_HEADS, HGRN_DK)
    pad = ((0, 0), (0, 0), (0, SLOT - HGRN_DK))
    lw["loglb"] = jnp.pad(jnp.log(lbi), pad, constant_values=-1.0).reshape(1, GROUP_W)
    lw["l1mlb"] = jnp.pad(jnp.log1p(-lbi), pad, constant_values=-1.0).reshape(1, GROUP_W)
    wo = w_out[i]
    g_rows, r_rows = N_HEADS * GLA_DV, N_HEADS * RET_DV
    lw["w_out"] = jnp.stack([
        _head_slots(wo[:g_rows].T, GLA_DV).T, _head_slots(wo[g_rows:g_rows + r_rows].T, RET_DV).T,
        _head_slots(wo[g_rows + r_rows:].T, HGRN_DV).T]).astype(BF16)
    lw["ln1_g"], lw["ln1_b"] = ln1_g[i][None, :], ln1_b[i][None, :]
    lw["ln2_g"] = jnp.broadcast_to(ln2_g[i][:, None], (D_MODEL, LANE))
    lw["ln2_b"] = jnp.broadcast_to(ln2_b[i][:, None], (D_MODEL, LANE))
    lw["wqt"] = peer_w_q[i].T.astype(BF16)
    lw["subkeys"] = peer_subkeys[i].astype(BF16)
    lw["u"] = (peer_u[i] * SQRT_HALF).astype(BF16)
    lw["vt"] = jnp.swapaxes(peer_v[i].astype(BF16).reshape(-1, PEER_TE, D_MODEL), 1, 2)
    lw["plet"] = ple_proj[i].T.astype(BF16)
    lw["gatet"] = ple_gate[i].T.astype(BF16)
    return lw


def _unslot_state(st, dk, dv, rope=False):
    if rope:
        half = dk // 2
        st = jnp.concatenate([st[..., :half], st[..., SLOT // 2:SLOT // 2 + half]], axis=-1)
    return jnp.swapaxes(st[:, :, :dv, :dk], 2, 3)


def kernel(x_prompt, x_sample, p_prompt, p_sample, state_gla, state_ret, state_hgrn, w_in, gla_w_gate,
           gla_b_gate, gla_norm, ret_norm, hgrn_lb_logits, hgrn_norm, w_out, ln1_g, ln1_b, ln2_g, ln2_b,
           peer_w_q, peer_subkeys, peer_u, peer_v, ple_proj, ple_gate):
    bp, lp, _ = x_prompt.shape
    bs = x_sample.shape[0]
    assert x_sample.shape[1] == 1 and lp % CHUNK == 0

    lb = jnp.cumsum(jax.nn.softmax(hgrn_lb_logits.astype(F32), axis=0), axis=0)
    lb = lb - lb[0:1]
    tri_np, masks_np = _chunk_constants()
    tri, masks = jnp.asarray(tri_np, BF16), jnp.asarray(masks_np)
    dmat, qdec, kdec, sdec, gamma = _retention_constants()
    cos_p, sin_p = _rope_tables(jnp.arange(lp, dtype=F32))
    cos_s, sin_s = _rope_tables(PAST_LEN + jnp.arange(1, dtype=F32))

    xp = x_prompt.reshape(bp * lp, D_MODEL)
    xs = x_sample.reshape(bs, D_MODEL)
    gla_p, ret_p, hgrn_p, gla_s, ret_s, hgrn_s = [], [], [], [], [], []
    for i in range(DEPTH):
        lw = _layer_weights(i, lb, w_in, gla_w_gate, gla_b_gate, gla_norm, ret_norm, hgrn_norm, w_out,
                            ln1_g, ln1_b, ln2_g, ln2_b, peer_w_q, peer_subkeys, peer_u, peer_v,
                            ple_proj, ple_gate)
        lw.update(cos_s=cos_s, sin_s=sin_s, gamma=gamma)

        og, sg = _prompt_mixer_call(
            _gla_prompt_kernel, xp, bp, lp, lw["w_gla"],
            [lw["wgate"], lw["bgate"], lw["gla_norm"], tri, masks],
            [lw["w_gla"].shape[1], GROUP_W], "gla_prompt")
        orr, sr = _prompt_mixer_call(
            _ret_prompt_kernel, xp, bp, lp, lw["w_ret"],
            [(cos_p, None), (sin_p, None), lw["ret_norm"], dmat, qdec, kdec, sdec],
            [4 * GROUP_W], "ret_prompt")
        oh, sh = _prompt_mixer_call(
            _hgrn_prompt_kernel, xp, bp, lp, lw["w_hgrn"],
            [lw["loglb"], lw["l1mlb"], lw["hgrn_norm"], tri, masks],
            [4 * GROUP_W], "hgrn_prompt")
        xp = _peer_call(_out_call(og, orr, oh, xp, lw), p_prompt[i].reshape(bp * lp, PLE_DIM), lw)
        gla_p.append(_unslot_state(sg, GLA_DK, GLA_DV))
        ret_p.append(_unslot_state(sr, RET_DK, RET_DV, rope=True))
        hgrn_p.append(_unslot_state(sh, HGRN_DK, HGRN_DV))

        o_s, nsg, nsr, nsh = _sample_mixer_call(xs, lw, state_gla[i], state_ret[i], state_hgrn[i])
        xs = _peer_call(
            _out_call(o_s[:, :GROUP_W], o_s[:, GROUP_W:2 * GROUP_W], o_s[:, 2 * GROUP_W:], xs, lw),
            p_sample[i].reshape(bs, PLE_DIM), lw)
        gla_s.append(nsg.reshape(bs, N_HEADS, GLA_DK, GLA_DV))
        ret_s.append(nsr.reshape(bs, N_HEADS, RET_DK, RET_DV))
        hgrn_s.append(nsh.reshape(bs, N_HEADS, HGRN_DK, HGRN_DV))

    return (xp.reshape(bp, lp, D_MODEL), xs.reshape(bs, 1, D_MODEL),
            jnp.stack(gla_p), jnp.stack(ret_p), jnp.stack(hgrn_p),
            jnp.stack(gla_s), jnp.stack(ret_s), jnp.stack(hgrn_s))
```

```python
import functools

import numpy as np
import jax
import jax.numpy as jnp
from jax import lax
from jax.experimental import pallas as pl
from jax.experimental.pallas import tpu as pltpu

F32 = jnp.float32
BF16 = jnp.bfloat16
HIGHEST = lax.Precision.HIGHEST

D_MODEL = 1024
DEPTH = 2
PAST_LEN = 16384
N_HEADS = 4
GLA_DK, GLA_DV = 48, 96
RET_DK, RET_DV = 48, 96
HGRN_DK, HGRN_DV = 64, 64
GLA_LOWRANK = 16
GLA_TAU = 16.0
ROPE_BASE = 10000.0
CHUNK = 64
CHUNK_UNROLL = 4
PEER_HEADS = 8
N_KEYS = 128
PEER_TOPK = 16
PEER_TE = 1024
PLE_DIM = 256
ALPHA = (2 * DEPTH) ** 0.25
NORM_EPS = 1e-5

LANE = 128
SUBLANE = 8
SLOT = LANE
GROUP_W = N_HEADS * SLOT
VMEM_LIMIT = 56 * 1024 * 1024
NEG_INF = float("-inf")


def _dot(a, b, precision=None):
    return jnp.dot(a, b, preferred_element_type=F32, precision=precision)


def _dot_nt(a, b):
    return lax.dot_general(a, b, (((1,), (1,)), ((), ())), preferred_element_type=F32)


def _dot_tn(a, b):
    return lax.dot_general(a, b, (((0,), (0,)), ((), ())), preferred_element_type=F32)


def _sigmoid(x):
    return jax.nn.sigmoid(x)


def _silu(x):
    return x * _sigmoid(x)


def _log_sigmoid(x):
    return jnp.minimum(x, 0.0) - jnp.log1p(jnp.exp(-jnp.abs(x)))


def _logaddexp(a, c):
    amax = jnp.maximum(a, c)
    delta = a - c
    return jnp.where(jnp.isnan(delta), a + c, amax + jnp.log1p(jnp.exp(-jnp.abs(delta))))


SQRT_HALF = np.float32(0.7071067811865476)


def _lane_mask(n):
    return (lax.broadcasted_iota(jnp.int32, (1, LANE), 1) < n).astype(F32)


def _rms_head_norm(o, g_row, dv):
    ms = jnp.sum(o * o, axis=-1, keepdims=True) * (1.0 / dv)
    return o * lax.rsqrt(ms + NORM_EPS) * g_row


def _group_head_norm(o, g_row, dv):
    mask = _lane_mask(dv)
    mu = jnp.sum(o, axis=-1, keepdims=True) * (1.0 / dv)
    d = (o - mu) * mask
    var = jnp.sum(d * d, axis=-1, keepdims=True) * (1.0 / dv)
    return d * lax.rsqrt(var + NORM_EPS) * g_row


def _chunk_constants():
    i = np.arange(CHUNK)[:, None]
    t = np.arange(CHUNK)[None, :]
    masks = [i == t]
    half = CHUNK // 2
    while half >= 1:
        blk = i // (2 * half)
        second = (i % (2 * half)) >= half
        masks.append(second & ((t % (2 * half)) < half) & (blk == t // (2 * half)))
        half //= 2
    return (t <= i).astype(np.float32), np.stack(masks).astype(np.float32)


def _cumsum_rows(tri_bf, g):
    hi = g.astype(BF16)
    r1 = g - hi.astype(F32)
    mid = r1.astype(BF16)
    lo = (r1 - mid.astype(F32)).astype(BF16)
    return _dot(tri_bf, hi) + _dot(tri_bf, mid) + _dot(tri_bf, lo)


def _level_factors(b, g):
    width = b.shape[1]
    grouped = (CHUNK // SUBLANE, SUBLANE, width)
    row = lax.broadcasted_iota(jnp.int32, b.shape, 0)
    sub = lax.broadcasted_iota(jnp.int32, grouped, 1)
    b3 = b.reshape(grouped)

    def sub_ref(r):
        return jnp.broadcast_to(b3[:, r:r + 1, :], grouped)

    out = []
    half = CHUNK // 2
    while half >= SUBLANE:
        ref = jnp.concatenate(
            [jnp.broadcast_to(b[m * 2 * half + half - 1:m * 2 * half + half], (2 * half, width))
             for m in range(CHUNK // (2 * half))], axis=0)
        out.append(jnp.exp(-jnp.abs(b - ref)))
        half //= 2
    out.append(jnp.exp(-jnp.abs(b3 - sub_ref(3))).reshape(b.shape))
    out.append(jnp.exp(-jnp.abs(b3 - jnp.where(sub < 4, sub_ref(1), sub_ref(5)))).reshape(b.shape))
    out.append(jnp.exp(jnp.where(row % 2 == 1, g, 0.0)))
    return out


def _vector_decay_chunk(q, k, v, g, st_ref, tri_bf, masks_ref):
    heads = [slice(h * SLOT, (h + 1) * SLOT) for h in range(N_HEADS)]
    b = _cumsum_rows(tri_bf, g)
    b_last = b[CHUNK - 1:CHUNK]
    factors = _level_factors(b, g)
    qs = [q.astype(BF16)] + [(q * f).astype(BF16) for f in factors]
    ks = [k.astype(BF16)] + [(k * f).astype(BF16) for f in factors]
    q_in = (q * jnp.exp(b)).astype(BF16)
    k_out = (k * jnp.exp(b_last - b)).astype(BF16)
    decay = jnp.exp(b_last)
    vb = v.astype(BF16)
    scores = []
    for sl in heads:
        sc = masks_ref[0] * _dot_nt(qs[0][:, sl], ks[0][:, sl])
        for l in range(len(factors)):
            sc = sc + masks_ref[1 + l] * _dot_nt(qs[1 + l][:, sl], ks[1 + l][:, sl])
        scores.append(sc.astype(BF16))
    outs = []
    for h, sl in enumerate(heads):
        st = st_ref[0, h]
        outs.append(_dot(scores[h], vb[:, sl]) + _dot_nt(q_in[:, sl], st.astype(BF16)))
        st_ref[0, h] = st * decay[:, sl] + _dot_tn(vb[:, sl], k_out[:, sl])
    return outs


def _gla_prompt_kernel(x_ref, w_ref, wg_ref, bg_ref, nrm_ref, tri_ref, masks_ref,
                       o_ref, st_ref, z_scr, la_scr, *, seg):
    @pl.when(pl.program_id(1) == 0)
    def _():
        st_ref[...] = jnp.zeros_like(st_ref)

    z_scr[...] = _dot(x_ref[...].astype(BF16), w_ref[...])
    pre = _dot(z_scr[:, 4 * GROUP_W:4 * GROUP_W + SLOT], wg_ref[...], precision=HIGHEST) + bg_ref[...]
    la_scr[...] = _log_sigmoid(pre) * (1.0 / GLA_TAU)
    tri = tri_ref[...]

    def body(c, carry):
        r0 = pl.multiple_of(c * CHUNK, CHUNK)
        rows = pl.ds(r0, CHUNK)
        q = z_scr[rows, 0:GROUP_W] * (GLA_DK ** -0.5)
        k = z_scr[rows, GROUP_W:2 * GROUP_W]
        v = z_scr[rows, 2 * GROUP_W:3 * GROUP_W]
        outs = _vector_decay_chunk(q, k, v, la_scr[rows, :], st_ref, tri, masks_ref)
        for h in range(N_HEADS):
            gate = z_scr[rows, 3 * GROUP_W + h * SLOT:3 * GROUP_W + (h + 1) * SLOT]
            on = _rms_head_norm(outs[h], nrm_ref[h:h + 1, :], GLA_DV)
            o_ref[rows, h * SLOT:(h + 1) * SLOT] = (on * _silu(gate)).astype(BF16)
        return carry

    lax.fori_loop(0, seg // CHUNK, body, 0, unroll=CHUNK_UNROLL)


def _hgrn_prompt_kernel(x_ref, w_ref, loglb_ref, l1mlb_ref, nrm_ref, tri_ref, masks_ref,
                        o_ref, st_ref, z_scr, *, seg):
    @pl.when(pl.program_id(1) == 0)
    def _():
        st_ref[...] = jnp.zeros_like(st_ref)

    z_scr[...] = _dot(x_ref[...].astype(BF16), w_ref[...])
    tri = tri_ref[...]
    kmask = jnp.concatenate([_lane_mask(HGRN_DK)] * N_HEADS, axis=1)

    def body(c, carry):
        r0 = pl.multiple_of(c * CHUNK, CHUNK)
        rows = pl.ds(r0, CHUNK)
        q = _silu(z_scr[rows, 0:GROUP_W])
        v = z_scr[rows, 2 * GROUP_W:3 * GROUP_W]
        log_f = _logaddexp(loglb_ref[...], l1mlb_ref[...] + _log_sigmoid(z_scr[rows, GROUP_W:2 * GROUP_W]))
        k = (1.0 - jnp.exp(log_f)) * kmask
        outs = _vector_decay_chunk(q, k, v, log_f, st_ref, tri, masks_ref)
        for h in range(N_HEADS):
            sl = slice(h * SLOT, (h + 1) * SLOT)
            gate = z_scr[rows, 3 * GROUP_W + h * SLOT:3 * GROUP_W + (h + 1) * SLOT]
            on = _rms_head_norm(outs[h], nrm_ref[h:h + 1, :], HGRN_DV)
            o_ref[rows, sl] = (on * _silu(gate)).astype(BF16)
        return carry

    lax.fori_loop(0, seg // CHUNK, body, 0, unroll=CHUNK_UNROLL)


def _rotate(t, cs, sn):
    return t * cs + pltpu.roll(t, SLOT // 2, 1) * sn


def _ret_prompt_kernel(x_ref, w_ref, cos_ref, sin_ref, nrm_ref, dmat_ref, qdec_ref, kdec_ref, sdec_ref,
                       o_ref, st_ref, z_scr, *, seg):
    @pl.when(pl.program_id(1) == 0)
    def _():
        st_ref[...] = jnp.zeros_like(st_ref)

    z_scr[...] = _dot(x_ref[...].astype(BF16), w_ref[...])

    def body(c, carry):
        r0 = pl.multiple_of(c * CHUNK, CHUNK)
        rows = pl.ds(r0, CHUNK)
        cs = cos_ref[rows, :]
        sn = sin_ref[rows, :]
        for h in range(N_HEADS):
            sl = slice(h * SLOT, (h + 1) * SLOT)
            q = _rotate(z_scr[rows, h * SLOT:(h + 1) * SLOT], cs, sn)
            k = _rotate(z_scr[rows, GROUP_W + h * SLOT:GROUP_W + (h + 1) * SLOT], cs, sn) * (RET_DK ** -0.5)
            v = z_scr[rows, 2 * GROUP_W + h * SLOT:2 * GROUP_W + (h + 1) * SLOT]
            gate = z_scr[rows, 3 * GROUP_W + h * SLOT:3 * GROUP_W + (h + 1) * SLOT]
            st = st_ref[0, h]
            vb = v.astype(BF16)
            sc = dmat_ref[h] * _dot_nt(q.astype(BF16), k.astype(BF16))
            o = _dot(sc.astype(BF16), vb) + _dot_nt((q * qdec_ref[h]).astype(BF16), st.astype(BF16))
            st_ref[0, h] = st * sdec_ref[h] + _dot_tn(vb, (k * kdec_ref[h]).astype(BF16))
            on = _group_head_norm(o, nrm_ref[h:h + 1, :], RET_DV)
            o_ref[rows, sl] = (on * _silu(gate)).astype(BF16)
        return carry

    lax.fori_loop(0, seg // CHUNK, body, 0, unroll=CHUNK_UNROLL)


def _const_spec(shape):
    nd = len(shape)
    return pl.BlockSpec(shape, lambda *_: (0,) * nd)


def _prompt_mixer_call(kernel, x2d, batch, seq, w, extras, scratch_widths, name):
    seg = min(512, seq)
    nseg = seq // seg
    in_specs = [pl.BlockSpec((seg, D_MODEL), lambda b, s: (b * nseg + s, 0)), _const_spec(w.shape)]
    args = [x2d, w]
    for e in extras:
        if isinstance(e, tuple):
            arr, _ = e
            in_specs.append(pl.BlockSpec((seg, arr.shape[1]), lambda b, s: (s, 0)))
            args.append(arr)
        else:
            in_specs.append(_const_spec(e.shape))
            args.append(e)
    return pl.pallas_call(
        functools.partial(kernel, seg=seg),
        grid=(batch, nseg),
        in_specs=in_specs,
        out_specs=[pl.BlockSpec((seg, GROUP_W), lambda b, s: (b * nseg + s, 0)),
                   pl.BlockSpec((1, N_HEADS, SLOT, SLOT), lambda b, s: (b, 0, 0, 0))],
        out_shape=[jax.ShapeDtypeStruct((batch * seq, GROUP_W), BF16),
                   jax.ShapeDtypeStruct((batch, N_HEADS, SLOT, SLOT), F32)],
        scratch_shapes=[pltpu.VMEM((seg, wd), F32) for wd in scratch_widths],
        compiler_params=pltpu.CompilerParams(dimension_semantics=("arbitrary", "arbitrary"),
                                             vmem_limit_bytes=VMEM_LIMIT),
        name=name,
    )(*args)


def _sample_step(q, k, eg, v, s_ref, ns_ref, tq, tk, te, tv, to, s_t, sn_t, dk, dv, row_of):
    tq[...] = q.T
    tk[...] = k.T
    te[...] = eg.T
    tv[...] = v.T
    s_t[0:dk * dv, :] = s_ref[...].T
    vt = tv[0:dv, :]

    def body(kk, oacc):
        kr = row_of(kk)
        r = pl.multiple_of(kk * dv, SUBLANE)
        sn = s_t[pl.ds(r, dv), :] * te[pl.ds(kr, 1), :] + tk[pl.ds(kr, 1), :] * vt
        sn_t[pl.ds(r, dv), :] = sn
        return oacc + tq[pl.ds(kr, 1), :] * sn

    o_t = lax.fori_loop(0, dk, body, jnp.zeros((dv, q.shape[0]), F32))
    ns_ref[...] = sn_t[0:dk * dv, :].T
    to[...] = jnp.zeros_like(to)
    to[0:dv, :] = o_t
    return to[...].T


def _sample_mixer_kernel(x_ref, wg_ref, wr_ref, wh_ref, wgate_ref, bgate_ref, gn_ref, rn_ref, hn_ref,
                         cos_ref, sin_ref, gam_ref, loglb_ref, l1mlb_ref, sg_ref, sr_ref, sh_ref,
                         o_ref, nsg_ref, nsr_ref, nsh_ref,
                         zg, zr, zh, la, tq, tk, te, tv, to, s_t, sn_t):
    h = pl.program_id(0)

    @pl.when(h == 0)
    def _():
        xb = x_ref[...].astype(BF16)
        zg[...] = _dot(xb, wg_ref[...])
        zr[...] = _dot(xb, wr_ref[...])
        zh[...] = _dot(xb, wh_ref[...])
        pre = _dot(zg[:, 4 * GROUP_W:4 * GROUP_W + SLOT], wgate_ref[...], precision=HIGHEST) + bgate_ref[...]
        la[...] = _log_sigmoid(pre) * (1.0 / GLA_TAU)

    off = pl.multiple_of(h * SLOT, SLOT)
    sl = pl.ds(off, SLOT)
    tr = (tq, tk, te, tv, to, s_t, sn_t)
    batch = x_ref.shape[0]

    q = zg[:, sl] * (GLA_DK ** -0.5)
    k = zg[:, pl.ds(GROUP_W + off, SLOT)]
    v = zg[:, pl.ds(2 * GROUP_W + off, SLOT)]
    gate = zg[:, pl.ds(3 * GROUP_W + off, SLOT)]
    o = _sample_step(q, k, jnp.exp(la[:, sl]), v, sg_ref, nsg_ref, *tr, GLA_DK, GLA_DV, lambda kk: kk)
    o_ref[:, sl] = (_rms_head_norm(o, gn_ref[pl.ds(h, 1), :], GLA_DV) * _silu(gate)).astype(BF16)

    cs = cos_ref[...]
    sn = sin_ref[...]
    q = _rotate(zr[:, sl], cs, sn)
    k = _rotate(zr[:, pl.ds(GROUP_W + off, SLOT)], cs, sn) * (RET_DK ** -0.5)
    v = zr[:, pl.ds(2 * GROUP_W + off, SLOT)]
    gate = zr[:, pl.ds(3 * GROUP_W + off, SLOT)]
    eg = jnp.broadcast_to(gam_ref[h], (batch, SLOT))
    half = RET_DK // 2
    o = _sample_step(q, k, eg, v, sr_ref, nsr_ref, *tr, RET_DK, RET_DV,
                     lambda kk: kk + jnp.where(kk >= half, SLOT // 2 - half, 0))
    o_ref[:, pl.ds(GROUP_W + off, SLOT)] = (
        _group_head_norm(o, rn_ref[pl.ds(h, 1), :], RET_DV) * _silu(gate)).astype(BF16)

    q = _silu(zh[:, sl])
    hf = zh[:, pl.ds(GROUP_W + off, SLOT)]
    v = zh[:, pl.ds(2 * GROUP_W + off, SLOT)]
    gate = zh[:, pl.ds(3 * GROUP_W + off, SLOT)]
    log_f = _logaddexp(loglb_ref[:, sl], l1mlb_ref[:, sl] + _log_sigmoid(hf))
    f = jnp.exp(log_f)
    o = _sample_step(q, (1.0 - f) * _lane_mask(HGRN_DK), f, v, sh_ref, nsh_ref, *tr,
                     HGRN_DK, HGRN_DV, lambda kk: kk)
    o_ref[:, pl.ds(2 * GROUP_W + off, SLOT)] = (
        _rms_head_norm(o, hn_ref[pl.ds(h, 1), :], HGRN_DV) * _silu(gate)).astype(BF16)


def _sample_mixer_call(x2d, lw, sg, sr, sh):
    batch = x2d.shape[0]
    gsz, hsz = GLA_DK * GLA_DV, HGRN_DK * HGRN_DV
    consts = [lw["w_gla"], lw["w_ret"], lw["w_hgrn"], lw["wgate"], lw["bgate"], lw["gla_norm"], lw["ret_norm"],
              lw["hgrn_norm"], lw["cos_s"], lw["sin_s"], lw["gamma"], lw["loglb"], lw["l1mlb"]]
    in_specs = ([_const_spec(x2d.shape)] + [_const_spec(c.shape) for c in consts]
                + [pl.BlockSpec((batch, gsz), lambda h: (0, h)),
                   pl.BlockSpec((batch, gsz), lambda h: (0, h)),
                   pl.BlockSpec((batch, hsz), lambda h: (0, h))])
    out_specs = [_const_spec((batch, 3 * GROUP_W)),
                 pl.BlockSpec((batch, gsz), lambda h: (0, h)),
                 pl.BlockSpec((batch, gsz), lambda h: (0, h)),
                 pl.BlockSpec((batch, hsz), lambda h: (0, h))]
    out_shape = [jax.ShapeDtypeStruct((batch, 3 * GROUP_W), BF16),
                 jax.ShapeDtypeStruct((batch, N_HEADS * gsz), F32),
                 jax.ShapeDtypeStruct((batch, N_HEADS * gsz), F32),
                 jax.ShapeDtypeStruct((batch, N_HEADS * hsz), F32)]
    scratch = [pltpu.VMEM((batch, lw["w_gla"].shape[1]), F32), pltpu.VMEM((batch, 4 * GROUP_W), F32),
               pltpu.VMEM((batch, 4 * GROUP_W), F32), pltpu.VMEM((batch, GROUP_W), F32)]
    scratch += [pltpu.VMEM((SLOT, batch), F32) for _ in range(5)]
    scratch += [pltpu.VMEM((gsz, batch), F32), pltpu.VMEM((gsz, batch), F32)]
    return pl.pallas_call(
        _sample_mixer_kernel,
        grid=(N_HEADS,),
        in_specs=in_specs, out_specs=out_specs, out_shape=out_shape, scratch_shapes=scratch,
        compiler_params=pltpu.CompilerParams(dimension_semantics=("arbitrary",), vmem_limit_bytes=VMEM_LIMIT),
        name="sample_mixer",
    )(x2d, *consts, sg.reshape(batch, -1), sr.reshape(batch, -1), sh.reshape(batch, -1))


def _out_kernel(og_ref, or_ref, oh_ref, x_ref, wo_ref, g_ref, b_ref, xt_ref):
    mix = _dot(og_ref[...], wo_ref[0]) + _dot(or_ref[...], wo_ref[1]) + _dot(oh_ref[...], wo_ref[2])
    y = ALPHA * x_ref[...] + mix
    mu = jnp.mean(y, axis=-1, keepdims=True)
    d = y - mu
    var = jnp.mean(d * d, axis=-1, keepdims=True)
    xt_ref[...] = (d * lax.rsqrt(var + NORM_EPS) * g_ref[...] + b_ref[...]).T


def _out_call(og, orr, oh, x2d, lw):
    t = x2d.shape[0]
    tm = min(512, t)
    row = lambda i: (i, 0)
    return pl.pallas_call(
        _out_kernel,
        grid=(t // tm,),
        in_specs=[pl.BlockSpec((tm, GROUP_W), row), pl.BlockSpec((tm, GROUP_W), row),
                  pl.BlockSpec((tm, GROUP_W), row), pl.BlockSpec((tm, D_MODEL), row),
                  _const_spec(lw["w_out"].shape), _const_spec((1, D_MODEL)), _const_spec((1, D_MODEL))],
        out_specs=pl.BlockSpec((D_MODEL, tm), lambda i: (0, i)),
        out_shape=jax.ShapeDtypeStruct((D_MODEL, t), F32),
        compiler_params=pltpu.CompilerParams(dimension_semantics=("arbitrary",), vmem_limit_bytes=VMEM_LIMIT),
        name="out_proj_ln",
    )(og, orr, oh, x2d, lw["w_out"], lw["ln1_g"], lw["ln1_b"])


def _oddeven_merge(lo, hi, r):
    step = r * 2
    if step < hi - lo:
        yield from _oddeven_merge(lo, hi, step)
        yield from _oddeven_merge(lo + r, hi, step)
        yield from [(i, i + r) for i in range(lo + r, hi - r, step)]
    else:
        yield (lo, lo + r)


def _oddeven_merge_sort(lo, hi):
    if hi - lo >= 1:
        mid = lo + (hi - lo) // 2
        yield from _oddeven_merge_sort(lo, mid)
        yield from _oddeven_merge_sort(mid + 1, hi)
        yield from _oddeven_merge(lo, hi, 1)


_SORT16 = tuple(_oddeven_merge_sort(0, PEER_TOPK - 1))
_BITONIC16 = tuple((i, i + d) for d in (8, 4, 2, 1) for i in range(PEER_TOPK) if i & d == 0)


def _compare_exchange(x, pairs):
    for i, j in pairs:
        x[i], x[j] = jnp.maximum(x[i], x[j]), jnp.minimum(x[i], x[j])
    return x


def _top16_sorted(a3):
    x = _compare_exchange([a3[i] for i in range(PEER_TOPK)], _SORT16)
    for shift in (4, 2, 1):
        y = [pltpu.roll(v, shift, 0) for v in x]
        x = _compare_exchange([jnp.maximum(x[i], y[PEER_TOPK - 1 - i]) for i in range(PEER_TOPK)], _BITONIC16)
    return x


def _sublane_block(rows):
    sub = lax.broadcasted_iota(jnp.int32, (SUBLANE, LANE), 0)
    blk = rows[0]
    for r in range(1, SUBLANE):
        blk = jnp.where(sub == r, rows[r], blk)
    return blk


def _route_tile(a1, a2):
    v1 = _top16_sorted(a1)
    v2 = _top16_sorted(a2)
    sub = lax.broadcasted_iota(jnp.int32, (SUBLANE, LANE), 0)
    v2a = _sublane_block(v2[0:8])
    v2b = _sublane_block(v2[8:16])
    v1b = _sublane_block(v1[8:16])
    cands = [v1[0] + v2a, v1[0] + v2b]
    for r1 in range(1, 8):
        cands.append(jnp.where(sub < PEER_TOPK // (r1 + 1), v1[r1] + v2a, NEG_INF))
    cands.append(v1b + v2[0])
    thr = jnp.full((SUBLANE, LANE), NEG_INF, F32)
    need = jnp.full((SUBLANE, LANE), float(PEER_TOPK), F32)
    cur = cands
    for _ in range(PEER_TOPK):
        m8 = cur[0]
        for blk in cur[1:]:
            m8 = jnp.maximum(m8, blk)
        m = jnp.broadcast_to(jnp.max(m8, axis=0, keepdims=True), (SUBLANE, LANE))
        hits = [blk == m for blk in cur]
        cnt8 = jnp.where(hits[0], 1.0, 0.0)
        for hit in hits[1:]:
            cnt8 = cnt8 + jnp.where(hit, 1.0, 0.0)
        thr = jnp.where(need > 0.0, m, thr)
        need = need - jnp.broadcast_to(jnp.sum(cnt8, axis=0, keepdims=True), (SUBLANE, LANE))
        cur = [jnp.where(hit, NEG_INF, blk) for hit, blk in zip(hits, cur)]
    top = v1[0] + v2[0]
    z8 = jnp.zeros((SUBLANE, LANE), F32)
    for blk in cands:
        z8 = z8 + jnp.where(blk >= thr, jnp.exp(blk - top), 0.0)
    z = jnp.broadcast_to(jnp.sum(z8, axis=0, keepdims=True), (SUBLANE, LANE))
    n1 = jnp.zeros(a1.shape, F32)
    rank2 = jnp.zeros(a2.shape, F32)
    for r in range(PEER_TOPK):
        n1 = jnp.where(a1 + v2[r] >= thr, r + 1.0, n1)
        rank2 = jnp.where(v2[r] > a2, r + 1.0, rank2)
    e1 = jnp.exp(a1 - v1[0]) * SQRT_HALF
    e2 = jnp.exp(a2 - v2[0]) / z
    return n1, rank2, e1, e2


def _peer_kernel(xt_ref, wqt_ref, sk_ref, u0_ref, u_ref, vt_ref, p_ref, plet_ref, gatet_ref, g2_ref, b2_ref,
                 out_ref, xbf, s_scr, n1, e1, rk2, e2, h_scr, w_scr, acc, *, tn, te, nj):
    j = pl.program_id(1)
    lane_tiles = tn // LANE
    groups = (N_KEYS // SUBLANE, SUBLANE, LANE)

    def lane_slice(lt):
        return pl.ds(pl.multiple_of(lt * LANE, LANE), LANE)

    def put_h(slot, hval):
        for l2 in range(lane_tiles):
            h_scr[slot, l2] = hval[:, l2 * LANE:(l2 + 1) * LANE].astype(BF16)

    @pl.when(j == 0)
    def _route():
        xbf[...] = xt_ref[...].astype(BF16)

        def head_body(h, carry):
            r = pl.multiple_of(h * 2 * N_KEYS, 2 * N_KEYS)
            qh = _dot(wqt_ref[pl.ds(r, 2 * N_KEYS), :], xbf[...]).astype(BF16)
            s_scr[0] = _dot(sk_ref[h, 0], qh[0:N_KEYS])
            s_scr[1] = _dot(sk_ref[h, 1], qh[N_KEYS:2 * N_KEYS])

            def lane_body(lt, c):
                lanes = lane_slice(lt)
                n1_t, rank2_t, e1_t, e2_t = _route_tile(s_scr[0, :, lanes].reshape(groups),
                                                        s_scr[1, :, lanes].reshape(groups))
                n1[lt, h] = n1_t.reshape(N_KEYS, LANE)
                e1[lt, h] = e1_t.reshape(N_KEYS, LANE)
                rk2[lt, h] = rank2_t.reshape(N_KEYS, LANE).astype(BF16).reshape(rk2.shape[2:])
                e2[lt, h] = e2_t.reshape(N_KEYS, LANE).astype(BF16).reshape(e2.shape[2:])
                return c

            lax.fori_loop(0, lane_tiles, lane_body, 0)
            return carry

        lax.fori_loop(0, PEER_HEADS, head_body, 0)
        acc[...] = jnp.zeros_like(acc)

        put_h(0, _dot(u0_ref[...], xbf[...]))

    na = te // N_KEYS
    assert na == SUBLANE
    a0 = pl.multiple_of(jnp.minimum(j, nj - 1) * na, SUBLANE)
    cur = j % 2
    nxt = 1 - cur
    packed = rk2.shape[2:]
    row16 = (1, 2 * SUBLANE, LANE)

    @pl.when(j < nj - 1)
    def _next_h():
        put_h(nxt, _dot(u_ref[...], xbf[...]))

    @pl.when(j >= 1)
    def _prev_v():
        w_prev = jnp.concatenate([w_scr[nxt, l2] for l2 in range(lane_tiles)], axis=1)
        acc[...] += _dot(vt_ref[...], w_prev)

    @pl.when(j < nj)
    def _this_w():
        def lane_body(p, carry):
            n1blk = [n1[p, h, pl.ds(a0, na), :] for h in range(PEER_HEADS)]
            e1blk = [e1[p, h, pl.ds(a0, na), :] for h in range(PEER_HEADS)]
            for ai in range(0, na, 2):
                gates = [jnp.zeros(packed, BF16), jnp.zeros(packed, BF16)]
                for h in range(PEER_HEADS):
                    rk2h = rk2[p, h]
                    e2h = e2[p, h]
                    for d in range(2):
                        n1a = jnp.broadcast_to(n1blk[h][ai + d:ai + d + 1, :], row16[1:]).astype(BF16).reshape(row16)
                        e1a = jnp.broadcast_to(e1blk[h][ai + d:ai + d + 1, :], row16[1:]).astype(BF16).reshape(row16)
                        gates[d] = gates[d] + jnp.where(rk2h < n1a, e2h, jnp.zeros_like(e2h)) * e1a
                for d in range(2):
                    erows = slice((ai + d) * N_KEYS, (ai + d + 1) * N_KEYS)
                    hs = h_scr[cur, p, erows, :].reshape(packed)
                    w_scr[cur, p, erows, :] = (gates[d] * (hs + hs * lax.erf(hs))).reshape(N_KEYS, LANE)
            return carry

        lax.fori_loop(0, lane_tiles, lane_body, 0)

    @pl.when(j == nj)
    def _finish():
        def norm_body(lt, c):
            lanes = lane_slice(lt)
            y = ALPHA * xt_ref[:, lanes] + acc[:, lanes]
            mu = jnp.mean(y, axis=0, keepdims=True)
            d = y - mu
            var = jnp.mean(d * d, axis=0, keepdims=True)
            yn = d * lax.rsqrt(var + NORM_EPS) * g2_ref[...] + b2_ref[...]
            acc[:, lanes] = yn
            xbf[:, lanes] = yn.astype(BF16)
            return c

        lax.fori_loop(0, lane_tiles, norm_body, 0)
        emb = _dot_nt(plet_ref[...], p_ref[...].astype(BF16))
        gt = _dot(gatet_ref[...], xbf[...])
        out_ref[...] = (acc[...] + emb * _sigmoid(gt)).T


def _peer_call(xt, p2d, lw):
    t = xt.shape[1]
    tn = min(512, t)
    te = PEER_TE
    n_exp = lw["u"].shape[0]
    nj = n_exp // te
    lane_tiles = tn // LANE
    once = dict(pipeline_mode=pl.Buffered(1))
    in_specs = [
        pl.BlockSpec((D_MODEL, tn), lambda i, j: (0, i)),
        pl.BlockSpec(lw["wqt"].shape, lambda i, j: (0, 0), **once),
        pl.BlockSpec(lw["subkeys"].shape, lambda i, j: (0, 0, 0, 0), **once),
        pl.BlockSpec((te, D_MODEL), lambda i, j: (0, 0), **once),
        pl.BlockSpec((te, D_MODEL), lambda i, j: (jnp.minimum(j + 1, nj - 1), 0)),
        pl.BlockSpec((None, D_MODEL, te), lambda i, j: (jnp.maximum(j - 1, 0), 0, 0)),
        pl.BlockSpec((tn, PLE_DIM), lambda i, j: (i, 0)),
        pl.BlockSpec(lw["plet"].shape, lambda i, j: (0, 0), **once),
        pl.BlockSpec(lw["gatet"].shape, lambda i, j: (0, 0), **once),
        pl.BlockSpec((D_MODEL, LANE), lambda i, j: (0, 0), **once),
        pl.BlockSpec((D_MODEL, LANE), lambda i, j: (0, 0), **once),
    ]
    route = (lane_tiles, PEER_HEADS, N_KEYS, LANE)
    route_packed = (lane_tiles, PEER_HEADS, N_KEYS // (2 * SUBLANE), 2 * SUBLANE, LANE)
    scratch = [pltpu.VMEM((D_MODEL, tn), BF16), pltpu.VMEM((2, N_KEYS, tn), F32),
               pltpu.VMEM(route, F32), pltpu.VMEM(route, F32),
               pltpu.VMEM(route_packed, BF16), pltpu.VMEM(route_packed, BF16),
               pltpu.VMEM((2, lane_tiles, te, LANE), BF16), pltpu.VMEM((2, lane_tiles, te, LANE), BF16),
               pltpu.VMEM((D_MODEL, tn), F32)]
    return pl.pallas_call(
        functools.partial(_peer_kernel, tn=tn, te=te, nj=nj),
        grid=(t // tn, nj + 1),
        in_specs=in_specs,
        out_specs=pl.BlockSpec((tn, D_MODEL), lambda i, j: (i, 0)),
        out_shape=jax.ShapeDtypeStruct((t, D_MODEL), F32),
        scratch_shapes=scratch,
        compiler_params=pltpu.CompilerParams(dimension_semantics=("arbitrary", "arbitrary"),
                                             vmem_limit_bytes=VMEM_LIMIT),
        name="peer_ffn_ln_ple",
    )(xt, lw["wqt"], lw["subkeys"], lw["u"], lw["u"], lw["vt"], p2d, lw["plet"], lw["gatet"], lw["ln2_g"], lw["ln2_b"])


def _head_slots(w, d):
    r = w.shape[0]
    return jnp.pad(w.reshape(r, N_HEADS, d), ((0, 0), (0, 0), (0, SLOT - d))).reshape(r, GROUP_W)


def _rope_slots(w, d):
    r = w.shape[0]
    half = d // 2
    w = w.reshape(r, N_HEADS, 2, half)
    return jnp.pad(w, ((0, 0), (0, 0), (0, 0), (0, SLOT // 2 - half))).reshape(r, GROUP_W)


def _rope_tables(pos):
    half = RET_DK // 2
    inv = 1.0 / (ROPE_BASE ** (jnp.arange(0, RET_DK, 2, dtype=F32) / RET_DK))
    ang = pos[:, None] * inv[None, :]
    pad = ((0, 0), (0, SLOT // 2 - half))
    cos = jnp.pad(jnp.cos(ang), pad)
    sin = jnp.pad(jnp.sin(ang), pad)
    return jnp.concatenate([cos, cos], axis=1), jnp.concatenate([-sin, sin], axis=1)


def _retention_constants():
    log_gamma = jnp.log1p(-jnp.exp2(-5.0 - jnp.arange(N_HEADS, dtype=F32)))
    i = jnp.arange(CHUNK, dtype=F32)
    diff = i[:, None] - i[None, :]
    lg = log_gamma[:, None, None]
    dmat = jnp.where(diff >= 0, jnp.exp(jnp.where(diff >= 0, diff, 0.0) * lg), 0.0)
    qdec = jnp.broadcast_to(jnp.exp((i[None, :, None] + 1.0) * lg), (N_HEADS, CHUNK, SLOT))
    kdec = jnp.broadcast_to(jnp.exp((CHUNK - 1.0 - i[None, :, None]) * lg), (N_HEADS, CHUNK, SLOT))
    sdec = jnp.broadcast_to(jnp.exp(CHUNK * lg), (N_HEADS, 1, SLOT))
    gamma = jnp.broadcast_to(jnp.exp(lg), (N_HEADS, 1, SLOT))
    return dmat, qdec, kdec, sdec, gamma


def _layer_weights(i, lb, w_in, gla_w_gate, gla_b_gate, gla_norm, ret_norm, hgrn_norm, w_out,
                   ln1_g, ln1_b, ln2_g, ln2_b, peer_w_q, peer_subkeys, peer_u, peer_v, ple_proj, ple_gate):
    sizes = (N_HEADS * GLA_DK, N_HEADS * GLA_DK, N_HEADS * GLA_DV, N_HEADS * GLA_DV, GLA_LOWRANK,
             N_HEADS * RET_DK, N_HEADS * RET_DK, N_HEADS * RET_DV, N_HEADS * RET_DV,
             N_HEADS * HGRN_DK, N_HEADS * HGRN_DK, N_HEADS * HGRN_DV, N_HEADS * HGRN_DV)
    offs = [int(c) for c in np.cumsum(sizes)[:-1]]
    gq, gk, gv, gg, glr, rq, rk, rv, rg, hq, hf, hi, hg = jnp.split(w_in[i], offs, axis=1)
    lw = {}
    lw["w_gla"] = jnp.concatenate(
        [_head_slots(gq, GLA_DK), _head_slots(gk, GLA_DK), _head_slots(gv, GLA_DV), _head_slots(gg, GLA_DV),
         jnp.pad(glr, ((0, 0), (0, SLOT - GLA_LOWRANK)))], axis=1).astype(BF16)
    lw["w_ret"] = jnp.concatenate(
        [_rope_slots(rq, RET_DK), _rope_slots(rk, RET_DK), _head_slots(rv, RET_DV), _head_slots(rg, RET_DV)],
        axis=1).astype(BF16)
    lw["w_hgrn"] = jnp.concatenate([_head_slots(w, HGRN_DK) for w in (hq, hf, hi, hg)], axis=1).astype(BF16)
    lw["wgate"] = jnp.pad(_head_slots(gla_w_gate[i], GLA_DK), ((0, SLOT - GLA_LOWRANK), (0, 0)))
    lw["bgate"] = _head_slots(gla_b_gate[i][None, :], GLA_DK)
    lw["gla_norm"] = jnp.pad(gla_norm[i], ((0, 0), (0, SLOT - GLA_DV)))
    lw["ret_norm"] = jnp.pad(ret_norm[i], ((0, 0), (0, SLOT - RET_DV)))
    lw["hgrn_norm"] = jnp.pad(hgrn_norm[i], ((0, 0), (0, SLOT - HGRN_DV)))
    lbi = lb[i].reshape(1, N_HEADS, HGRN_DK)
    pad = ((0, 0), (0, 0), (0, SLOT - HGRN_DK))
    lw["loglb"] = jnp.pad(jnp.log(lbi), pad, constant_values=-1.0).reshape(1, GROUP_W)
    lw["l1mlb"] = jnp.pad(jnp.log1p(-lbi), pad, constant_values=-1.0).reshape(1, GROUP_W)
    wo = w_out[i]
    g_rows, r_rows = N_HEADS * GLA_DV, N_HEADS * RET_DV
    lw["w_out"] = jnp.stack([
        _head_slots(wo[:g_rows].T, GLA_DV).T, _head_slots(wo[g_rows:g_rows + r_rows].T, RET_DV).T,
        _head_slots(wo[g_rows + r_rows:].T, HGRN_DV).T]).astype(BF16)
    lw["ln1_g"], lw["ln1_b"] = ln1_g[i][None, :], ln1_b[i][None, :]
    lw["ln2_g"] = jnp.broadcast_to(ln2_g[i][:, None], (D_MODEL, LANE))
    lw["ln2_b"] = jnp.broadcast_to(ln2_b[i][:, None], (D_MODEL, LANE))
    lw["wqt"] = peer_w_q[i].T.astype(BF16)
    lw["subkeys"] = peer_subkeys[i].astype(BF16)
    lw["u"] = (peer_u[i] * SQRT_HALF).astype(BF16)
    lw["vt"] = jnp.swapaxes(peer_v[i].astype(BF16).reshape(-1, PEER_TE, D_MODEL), 1, 2)
    lw["plet"] = ple_proj[i].T.astype(BF16)
    lw["gatet"] = ple_gate[i].T.astype(BF16)
    return lw


def _unslot_state(st, dk, dv, rope=False):
    if rope:
        half = dk // 2
        st = jnp.concatenate([st[..., :half], st[..., SLOT // 2:SLOT // 2 + half]], axis=-1)
    return jnp.swapaxes(st[:, :, :dv, :dk], 2, 3)


def kernel(x_prompt, x_sample, p_prompt, p_sample, state_gla, state_ret, state_hgrn, w_in, gla_w_gate,
           gla_b_gate, gla_norm, ret_norm, hgrn_lb_logits, hgrn_norm, w_out, ln1_g, ln1_b, ln2_g, ln2_b,
           peer_w_q, peer_subkeys, peer_u, peer_v, ple_proj, ple_gate):
    bp, lp, _ = x_prompt.shape
    bs = x_sample.shape[0]
    assert x_sample.shape[1] == 1 and lp % CHUNK == 0

    lb = jnp.cumsum(jax.nn.softmax(hgrn_lb_logits.astype(F32), axis=0), axis=0)
    lb = lb - lb[0:1]
    tri_np, masks_np = _chunk_constants()
    tri, masks = jnp.asarray(tri_np, BF16), jnp.asarray(masks_np)
    dmat, qdec, kdec, sdec, gamma = _retention_constants()
    cos_p, sin_p = _rope_tables(jnp.arange(lp, dtype=F32))
    cos_s, sin_s = _rope_tables(PAST_LEN + jnp.arange(1, dtype=F32))

    xp = x_prompt.reshape(bp * lp, D_MODEL)
    xs = x_sample.reshape(bs, D_MODEL)
    gla_p, ret_p, hgrn_p, gla_s, ret_s, hgrn_s = [], [], [], [], [], []
    for i in range(DEPTH):
        lw = _layer_weights(i, lb, w_in, gla_w_gate, gla_b_gate, gla_norm, ret_norm, hgrn_norm, w_out,
                            ln1_g, ln1_b, ln2_g, ln2_b, peer_w_q, peer_subkeys, peer_u, peer_v,
                            ple_proj, ple_gate)
        lw.update(cos_s=cos_s, sin_s=sin_s, gamma=gamma)

        og, sg = _prompt_mixer_call(
            _gla_prompt_kernel, xp, bp, lp, lw["w_gla"],
            [lw["wgate"], lw["bgate"], lw["gla_norm"], tri, masks],
            [lw["w_gla"].shape[1], GROUP_W], "gla_prompt")
        orr, sr = _prompt_mixer_call(
            _ret_prompt_kernel, xp, bp, lp, lw["w_ret"],
            [(cos_p, None), (sin_p, None), lw["ret_norm"], dmat, qdec, kdec, sdec],
            [4 * GROUP_W], "ret_prompt")
        oh, sh = _prompt_mixer_call(
            _hgrn_prompt_kernel, xp, bp, lp, lw["w_hgrn"],
            [lw["loglb"], lw["l1mlb"], lw["hgrn_norm"], tri, masks],
            [4 * GROUP_W], "hgrn_prompt")
        xp = _peer_call(_out_call(og, orr, oh, xp, lw), p_prompt[i].reshape(bp * lp, PLE_DIM), lw)
        gla_p.append(_unslot_state(sg, GLA_DK, GLA_DV))
        ret_p.append(_unslot_state(sr, RET_DK, RET_DV, rope=True))
        hgrn_p.append(_unslot_state(sh, HGRN_DK, HGRN_DV))

        o_s, nsg, nsr, nsh = _sample_mixer_call(xs, lw, state_gla[i], state_ret[i], state_hgrn[i])
        xs = _peer_call(
            _out_call(o_s[:, :GROUP_W], o_s[:, GROUP_W:2 * GROUP_W], o_s[:, 2 * GROUP_W:], xs, lw),
            p_sample[i].reshape(bs, PLE_DIM), lw)
        gla_s.append(nsg.reshape(bs, N_HEADS, GLA_DK, GLA_DV))
        ret_s.append(nsr.reshape(bs, N_HEADS, RET_DK, RET_DV))
        hgrn_s.append(nsh.reshape(bs, N_HEADS, HGRN_DK, HGRN_DV))

    return (xp.reshape(bp, lp, D_MODEL), xs.reshape(bs, 1, D_MODEL),
            jnp.stack(gla_p), jnp.stack(ret_p), jnp.stack(hgrn_p),
            jnp.stack(gla_s), jnp.stack(ret_s), jnp.stack(hgrn_s))
```

```python
import functools

import numpy as np
import jax
import jax.numpy as jnp
from jax import lax
from jax.experimental import pallas as pl
from jax.experimental.pallas import tpu as pltpu

F32 = jnp.float32
BF16 = jnp.bfloat16

D_MODEL = 1024
DEPTH = 2
PAST_LEN = 16384
N_HEADS = 4
GLA_DK, GLA_DV = 48, 96
RET_DK, RET_DV = 48, 96
HGRN_DK, HGRN_DV = 64, 64
GLA_LOWRANK = 16
GLA_TAU = 16.0
ROPE_BASE = 10000.0
CHUNK = 64
GLA_UNROLL = 4
HGRN_UNROLL = 8
RET_UNROLL = 8
PEER_HEADS = 8
N_KEYS = 128
PEER_TOPK = 16
PEER_TE = 1024
PLE_DIM = 256
ALPHA = (2 * DEPTH) ** 0.25
NORM_EPS = 1e-5

LANE = 128
SUBLANE = 8
SLOT = LANE
GROUP_W = N_HEADS * SLOT
VMEM_LIMIT = 56 * 1024 * 1024
NEG_INF = float("-inf")


def _dot(a, b):
    return jnp.dot(a, b, preferred_element_type=F32)


def _dot_f32(a, b):
    return jnp.dot(a, b, preferred_element_type=F32, precision=lax.Precision.HIGHEST)


def _dot_nt(a, b):
    return lax.dot_general(a, b, (((1,), (1,)), ((), ())), preferred_element_type=F32)


def _dot_tn(a, b):
    return lax.dot_general(a, b, (((0,), (0,)), ((), ())), preferred_element_type=F32)


def _sigmoid(x):
    return jax.nn.sigmoid(x)


def _silu(x):
    return x * _sigmoid(x)


def _log_sigmoid(x):
    return jnp.minimum(x, 0.0) - jnp.log1p(jnp.exp(-jnp.abs(x)))


def _logaddexp(a, c):
    amax = jnp.maximum(a, c)
    delta = a - c
    return jnp.where(jnp.isnan(delta), a + c, amax + jnp.log1p(jnp.exp(-jnp.abs(delta))))


SQRT_HALF = np.float32(0.7071067811865476)


def _lane_mask(n):
    return (lax.broadcasted_iota(jnp.int32, (1, LANE), 1) < n).astype(F32)


def _rms_head_norm(o, g_row, dv):
    ms = jnp.sum(o * o, axis=-1, keepdims=True) * (1.0 / dv)
    return o * lax.rsqrt(ms + NORM_EPS) * g_row


def _group_head_norm(o, g_row, dv):
    mask = _lane_mask(dv)
    mu = jnp.sum(o, axis=-1, keepdims=True) * (1.0 / dv)
    d = (o - mu) * mask
    var = jnp.sum(d * d, axis=-1, keepdims=True) * (1.0 / dv)
    return d * lax.rsqrt(var + NORM_EPS) * g_row


def _chunk_constants():
    i = np.arange(CHUNK)[:, None]
    t = np.arange(CHUNK)[None, :]
    masks = [i == t]
    half = CHUNK // 2
    while half >= 1:
        blk = i // (2 * half)
        second = (i % (2 * half)) >= half
        masks.append(second & ((t % (2 * half)) < half) & (blk == t // (2 * half)))
        half //= 2
    return (t <= i).astype(np.float32), np.stack(masks).astype(np.float32)


def _cumsum_rows(tri_bf, g):
    hi = g.astype(BF16)
    r1 = g - hi.astype(F32)
    mid = r1.astype(BF16)
    lo = (r1 - mid.astype(F32)).astype(BF16)
    return _dot(tri_bf, hi) + _dot(tri_bf, mid) + _dot(tri_bf, lo)


def _level_factors(b, g):
    width = b.shape[1]
    grouped = (CHUNK // SUBLANE, SUBLANE, width)
    row = lax.broadcasted_iota(jnp.int32, b.shape, 0)
    sub = lax.broadcasted_iota(jnp.int32, grouped, 1)
    b3 = b.reshape(grouped)

    def sub_ref(r):
        return jnp.broadcast_to(b3[:, r:r + 1, :], grouped)

    out = []
    half = CHUNK // 2
    while half >= SUBLANE:
        ref = jnp.concatenate(
            [jnp.broadcast_to(b[m * 2 * half + half - 1:m * 2 * half + half], (2 * half, width))
             for m in range(CHUNK // (2 * half))], axis=0)
        out.append(jnp.exp(-jnp.abs(b - ref)))
        half //= 2
    out.append(jnp.exp(-jnp.abs(b3 - sub_ref(3))).reshape(b.shape))
    out.append(jnp.exp(-jnp.abs(b3 - jnp.where(sub < 4, sub_ref(1), sub_ref(5)))).reshape(b.shape))
    out.append(jnp.exp(jnp.where(row % 2 == 1, g, 0.0)))
    return out


def _vector_decay_chunk(q, k, v, g, st_ref, tri_bf, masks_ref):
    heads = [slice(h * SLOT, (h + 1) * SLOT) for h in range(N_HEADS)]
    b = _cumsum_rows(tri_bf, g)
    b_last = b[CHUNK - 1:CHUNK]
    factors = _level_factors(b, g)
    qs = [q.astype(BF16)] + [(q * f).astype(BF16) for f in factors]
    ks = [k.astype(BF16)] + [(k * f).astype(BF16) for f in factors]
    q_in = (q * jnp.exp(b)).astype(BF16)
    k_out = (k * jnp.exp(b_last - b)).astype(BF16)
    decay = jnp.exp(b_last)
    vb = v.astype(BF16)
    scores = []
    for sl in heads:
        sc = masks_ref[0] * _dot_nt(qs[0][:, sl], ks[0][:, sl])
        for l in range(len(factors)):
            sc = sc + masks_ref[1 + l] * _dot_nt(qs[1 + l][:, sl], ks[1 + l][:, sl])
        scores.append(sc.astype(BF16))
    outs = []
    for h, sl in enumerate(heads):
        st = st_ref[0, h]
        outs.append(_dot(scores[h], vb[:, sl]) + _dot_nt(q_in[:, sl], st.astype(BF16)))
        st_ref[0, h] = st * decay[:, sl] + _dot_tn(vb[:, sl], k_out[:, sl])
    return outs


def _gla_prompt_kernel(x_ref, w_ref, wg_ref, bg_ref, nrm_ref, tri_ref, masks_ref,
                       o_ref, st_ref, z_scr, la_scr, *, seg):
    @pl.when(pl.program_id(1) == 0)
    def _():
        st_ref[...] = jnp.zeros_like(st_ref)

    z_scr[...] = _dot(x_ref[...].astype(BF16), w_ref[...])
    pre = _dot_f32(z_scr[:, 4 * GROUP_W:4 * GROUP_W + SLOT], wg_ref[...]) + bg_ref[...]
    la_scr[...] = _log_sigmoid(pre) * (1.0 / GLA_TAU)
    tri = tri_ref[...]

    def body(c, carry):
        r0 = pl.multiple_of(c * CHUNK, CHUNK)
        rows = pl.ds(r0, CHUNK)
        q = z_scr[rows, 0:GROUP_W] * (GLA_DK ** -0.5)
        k = z_scr[rows, GROUP_W:2 * GROUP_W]
        v = z_scr[rows, 2 * GROUP_W:3 * GROUP_W]
        outs = _vector_decay_chunk(q, k, v, la_scr[rows, :], st_ref, tri, masks_ref)
        for h in range(N_HEADS):
            gate = z_scr[rows, 3 * GROUP_W + h * SLOT:3 * GROUP_W + (h + 1) * SLOT]
            on = _rms_head_norm(outs[h], nrm_ref[h:h + 1, :], GLA_DV)
            o_ref[rows, h * SLOT:(h + 1) * SLOT] = (on * _silu(gate)).astype(BF16)
        return carry

    lax.fori_loop(0, seg // CHUNK, body, 0, unroll=GLA_UNROLL)


def _hgrn_prompt_kernel(x_ref, w_ref, loglb_ref, l1mlb_ref, nrm_ref, tri_ref, masks_ref,
                        o_ref, st_ref, z_scr, *, seg):
    @pl.when(pl.program_id(1) == 0)
    def _():
        st_ref[...] = jnp.zeros_like(st_ref)

    z_scr[...] = _dot(x_ref[...].astype(BF16), w_ref[...])
    tri = tri_ref[...]
    kmask = jnp.concatenate([_lane_mask(HGRN_DK)] * N_HEADS, axis=1)

    def body(c, carry):
        r0 = pl.multiple_of(c * CHUNK, CHUNK)
        rows = pl.ds(r0, CHUNK)
        q = _silu(z_scr[rows, 0:GROUP_W])
        v = z_scr[rows, 2 * GROUP_W:3 * GROUP_W]
        log_f = _logaddexp(loglb_ref[...], l1mlb_ref[...] + _log_sigmoid(z_scr[rows, GROUP_W:2 * GROUP_W]))
        k = (1.0 - jnp.exp(log_f)) * kmask
        outs = _vector_decay_chunk(q, k, v, log_f, st_ref, tri, masks_ref)
        for h in range(N_HEADS):
            sl = slice(h * SLOT, (h + 1) * SLOT)
            gate = z_scr[rows, 3 * GROUP_W + h * SLOT:3 * GROUP_W + (h + 1) * SLOT]
            on = _rms_head_norm(outs[h], nrm_ref[h:h + 1, :], HGRN_DV)
            o_ref[rows, sl] = (on * _silu(gate)).astype(BF16)
        return carry

    lax.fori_loop(0, seg // CHUNK, body, 0, unroll=HGRN_UNROLL)


def _rotate(t, cs, sn):
    return t * cs + pltpu.roll(t, SLOT // 2, 1) * sn


def _ret_prompt_kernel(x_ref, w_ref, cos_ref, sin_ref, nrm_ref, dmat_ref, qdec_ref, kdec_ref, sdec_ref,
                       o_ref, st_ref, z_scr, *, seg):
    @pl.when(pl.program_id(1) == 0)
    def _():
        st_ref[...] = jnp.zeros_like(st_ref)

    z_scr[...] = _dot(x_ref[...].astype(BF16), w_ref[...])

    def body(c, carry):
        r0 = pl.multiple_of(c * CHUNK, CHUNK)
        rows = pl.ds(r0, CHUNK)
        cs = cos_ref[rows, :]
        sn = sin_ref[rows, :]
        heads = [slice(h * SLOT, (h + 1) * SLOT) for h in range(N_HEADS)]
        qs = [_rotate(z_scr[rows, h * SLOT:(h + 1) * SLOT], cs, sn) for h in range(N_HEADS)]
        ks = [_rotate(z_scr[rows, GROUP_W + h * SLOT:GROUP_W + (h + 1) * SLOT], cs, sn) * (RET_DK ** -0.5)
              for h in range(N_HEADS)]
        vb = z_scr[rows, 2 * GROUP_W:3 * GROUP_W].astype(BF16)
        scores = [(dmat_ref[h] * _dot_nt(qs[h].astype(BF16), ks[h].astype(BF16))).astype(BF16)
                  for h in range(N_HEADS)]
        q_in = [(qs[h] * qdec_ref[h]).astype(BF16) for h in range(N_HEADS)]
        k_out = [(ks[h] * kdec_ref[h]).astype(BF16) for h in range(N_HEADS)]
        outs = []
        for h, sl in enumerate(heads):
            st = st_ref[0, h]
            outs.append(_dot(scores[h], vb[:, sl]) + _dot_nt(q_in[h], st.astype(BF16)))
            st_ref[0, h] = st * sdec_ref[h] + _dot_tn(vb[:, sl], k_out[h])
        for h, sl in enumerate(heads):
            gate = z_scr[rows, 3 * GROUP_W + h * SLOT:3 * GROUP_W + (h + 1) * SLOT]
            on = _group_head_norm(outs[h], nrm_ref[h:h + 1, :], RET_DV)
            o_ref[rows, sl] = (on * _silu(gate)).astype(BF16)
        return carry

    lax.fori_loop(0, seg // CHUNK, body, 0, unroll=RET_UNROLL)


def _const_spec(shape):
    nd = len(shape)
    return pl.BlockSpec(shape, lambda *_: (0,) * nd)


def _prompt_mixer_call(kernel, x2d, batch, seq, w, extras, scratch_widths, name):
    seg = min(512, seq)
    nseg = seq // seg
    in_specs = [pl.BlockSpec((seg, D_MODEL), lambda b, s: (b * nseg + s, 0)), _const_spec(w.shape)]
    args = [x2d, w]
    for e in extras:
        if isinstance(e, tuple):
            arr, _ = e
            in_specs.append(pl.BlockSpec((seg, arr.shape[1]), lambda b, s: (s, 0)))
            args.append(arr)
        else:
            in_specs.append(_const_spec(e.shape))
            args.append(e)
    return pl.pallas_call(
        functools.partial(kernel, seg=seg),
        grid=(batch, nseg),
        in_specs=in_specs,
        out_specs=[pl.BlockSpec((seg, GROUP_W), lambda b, s: (b * nseg + s, 0)),
                   pl.BlockSpec((1, N_HEADS, SLOT, SLOT), lambda b, s: (b, 0, 0, 0))],
        out_shape=[jax.ShapeDtypeStruct((batch * seq, GROUP_W), BF16),
                   jax.ShapeDtypeStruct((batch, N_HEADS, SLOT, SLOT), F32)],
        scratch_shapes=[pltpu.VMEM((seg, wd), F32) for wd in scratch_widths],
        compiler_params=pltpu.CompilerParams(dimension_semantics=("arbitrary", "arbitrary"),
                                             vmem_limit_bytes=VMEM_LIMIT),
        name=name,
    )(*args)


def _sample_step(q, k, eg, v, s_ref, ns_ref, tq, tk, te, tv, to, s_t, sn_t, dk, dv, row_of):
    tq[...] = q.T
    tk[...] = k.T
    te[...] = eg.T
    tv[...] = v.T
    s_t[0:dk * dv, :] = s_ref[...].T
    vt = tv[0:dv, :]

    def body(kk, oacc):
        kr = row_of(kk)
        r = pl.multiple_of(kk * dv, SUBLANE)
        sn = s_t[pl.ds(r, dv), :] * te[pl.ds(kr, 1), :] + tk[pl.ds(kr, 1), :] * vt
        sn_t[pl.ds(r, dv), :] = sn
        return oacc + tq[pl.ds(kr, 1), :] * sn

    o_t = lax.fori_loop(0, dk, body, jnp.zeros((dv, q.shape[0]), F32))
    ns_ref[...] = sn_t[0:dk * dv, :].T
    to[...] = jnp.zeros_like(to)
    to[0:dv, :] = o_t
    return to[...].T


def _sample_mixer_kernel(x_ref, wg_ref, wr_ref, wh_ref, wgate_ref, bgate_ref, gn_ref, rn_ref, hn_ref,
                         cos_ref, sin_ref, gam_ref, loglb_ref, l1mlb_ref, sg_ref, sr_ref, sh_ref,
                         o_ref, nsg_ref, nsr_ref, nsh_ref,
                         zg, zr, zh, la, tq, tk, te, tv, to, s_t, sn_t):
    h = pl.program_id(0)

    @pl.when(h == 0)
    def _():
        xb = x_ref[...].astype(BF16)
        zg[...] = _dot(xb, wg_ref[...])
        zr[...] = _dot(xb, wr_ref[...])
        zh[...] = _dot(xb, wh_ref[...])
        pre = _dot_f32(zg[:, 4 * GROUP_W:4 * GROUP_W + SLOT], wgate_ref[...]) + bgate_ref[...]
        la[...] = _log_sigmoid(pre) * (1.0 / GLA_TAU)

    off = pl.multiple_of(h * SLOT, SLOT)
    sl = pl.ds(off, SLOT)
    tr = (tq, tk, te, tv, to, s_t, sn_t)
    batch = x_ref.shape[0]

    q = zg[:, sl] * (GLA_DK ** -0.5)
    k = zg[:, pl.ds(GROUP_W + off, SLOT)]
    v = zg[:, pl.ds(2 * GROUP_W + off, SLOT)]
    gate = zg[:, pl.ds(3 * GROUP_W + off, SLOT)]
    o = _sample_step(q, k, jnp.exp(la[:, sl]), v, sg_ref, nsg_ref, *tr, GLA_DK, GLA_DV, lambda kk: kk)
    o_ref[:, sl] = (_rms_head_norm(o, gn_ref[pl.ds(h, 1), :], GLA_DV) * _silu(gate)).astype(BF16)

    cs = cos_ref[...]
    sn = sin_ref[...]
    q = _rotate(zr[:, sl], cs, sn)
    k = _rotate(zr[:, pl.ds(GROUP_W + off, SLOT)], cs, sn) * (RET_DK ** -0.5)
    v = zr[:, pl.ds(2 * GROUP_W + off, SLOT)]
    gate = zr[:, pl.ds(3 * GROUP_W + off, SLOT)]
    eg = jnp.broadcast_to(gam_ref[h], (batch, SLOT))
    half = RET_DK // 2
    o = _sample_step(q, k, eg, v, sr_ref, nsr_ref, *tr, RET_DK, RET_DV,
                     lambda kk: kk + jnp.where(kk >= half, SLOT // 2 - half, 0))
    o_ref[:, pl.ds(GROUP_W + off, SLOT)] = (
        _group_head_norm(o, rn_ref[pl.ds(h, 1), :], RET_DV) * _silu(gate)).astype(BF16)

    q = _silu(zh[:, sl])
    hf = zh[:, pl.ds(GROUP_W + off, SLOT)]
    v = zh[:, pl.ds(2 * GROUP_W + off, SLOT)]
    gate = zh[:, pl.ds(3 * GROUP_W + off, SLOT)]
    log_f = _logaddexp(loglb_ref[:, sl], l1mlb_ref[:, sl] + _log_sigmoid(hf))
    f = jnp.exp(log_f)
    o = _sample_step(q, (1.0 - f) * _lane_mask(HGRN_DK), f, v, sh_ref, nsh_ref, *tr,
                     HGRN_DK, HGRN_DV, lambda kk: kk)
    o_ref[:, pl.ds(2 * GROUP_W + off, SLOT)] = (
        _rms_head_norm(o, hn_ref[pl.ds(h, 1), :], HGRN_DV) * _silu(gate)).astype(BF16)


def _sample_mixer_call(x2d, lw, sg, sr, sh):
    batch = x2d.shape[0]
    gsz, hsz = GLA_DK * GLA_DV, HGRN_DK * HGRN_DV
    consts = [lw["w_gla"], lw["w_ret"], lw["w_hgrn"], lw["wgate"], lw["bgate"], lw["gla_norm"], lw["ret_norm"],
              lw["hgrn_norm"], lw["cos_s"], lw["sin_s"], lw["gamma"], lw["loglb"], lw["l1mlb"]]
    in_specs = ([_const_spec(x2d.shape)] + [_const_spec(c.shape) for c in consts]
                + [pl.BlockSpec((batch, gsz), lambda h: (0, h)),
                   pl.BlockSpec((batch, gsz), lambda h: (0, h)),
                   pl.BlockSpec((batch, hsz), lambda h: (0, h))])
    out_specs = [_const_spec((batch, 3 * GROUP_W)),
                 pl.BlockSpec((batch, gsz), lambda h: (0, h)),
                 pl.BlockSpec((batch, gsz), lambda h: (0, h)),
                 pl.BlockSpec((batch, hsz), lambda h: (0, h))]
    out_shape = [jax.ShapeDtypeStruct((batch, 3 * GROUP_W), BF16),
                 jax.ShapeDtypeStruct((batch, N_HEADS * gsz), F32),
                 jax.ShapeDtypeStruct((batch, N_HEADS * gsz), F32),
                 jax.ShapeDtypeStruct((batch, N_HEADS * hsz), F32)]
    scratch = [pltpu.VMEM((batch, lw["w_gla"].shape[1]), F32), pltpu.VMEM((batch, 4 * GROUP_W), F32),
               pltpu.VMEM((batch, 4 * GROUP_W), F32), pltpu.VMEM((batch, GROUP_W), F32)]
    scratch += [pltpu.VMEM((SLOT, batch), F32) for _ in range(5)]
    scratch += [pltpu.VMEM((gsz, batch), F32), pltpu.VMEM((gsz, batch), F32)]
    return pl.pallas_call(
        _sample_mixer_kernel,
        grid=(N_HEADS,),
        in_specs=in_specs, out_specs=out_specs, out_shape=out_shape, scratch_shapes=scratch,
        compiler_params=pltpu.CompilerParams(dimension_semantics=("arbitrary",), vmem_limit_bytes=VMEM_LIMIT),
        name="sample_mixer",
    )(x2d, *consts, sg.reshape(batch, -1), sr.reshape(batch, -1), sh.reshape(batch, -1))


def _out_kernel(og_ref, or_ref, oh_ref, x_ref, wo_ref, g_ref, b_ref, xt_ref):
    mix = _dot(og_ref[...], wo_ref[0]) + _dot(or_ref[...], wo_ref[1]) + _dot(oh_ref[...], wo_ref[2])
    y = ALPHA * x_ref[...] + mix
    mu = jnp.mean(y, axis=-1, keepdims=True)
    d = y - mu
    var = jnp.mean(d * d, axis=-1, keepdims=True)
    xt_ref[...] = (d * lax.rsqrt(var + NORM_EPS) * g_ref[...] + b_ref[...]).T


def _out_call(og, orr, oh, x2d, lw):
    t = x2d.shape[0]
    tm = min(512, t)
    row = lambda i: (i, 0)
    return pl.pallas_call(
        _out_kernel,
        grid=(t // tm,),
        in_specs=[pl.BlockSpec((tm, GROUP_W), row), pl.BlockSpec((tm, GROUP_W), row),
                  pl.BlockSpec((tm, GROUP_W), row), pl.BlockSpec((tm, D_MODEL), row),
                  _const_spec(lw["w_out"].shape), _const_spec((1, D_MODEL)), _const_spec((1, D_MODEL))],
        out_specs=pl.BlockSpec((D_MODEL, tm), lambda i: (0, i)),
        out_shape=jax.ShapeDtypeStruct((D_MODEL, t), F32),
        compiler_params=pltpu.CompilerParams(dimension_semantics=("arbitrary",), vmem_limit_bytes=VMEM_LIMIT),
        name="out_proj_ln",
    )(og, orr, oh, x2d, lw["w_out"], lw["ln1_g"], lw["ln1_b"])


def _oddeven_merge(lo, hi, r):
    step = r * 2
    if step < hi - lo:
        yield from _oddeven_merge(lo, hi, step)
        yield from _oddeven_merge(lo + r, hi, step)
        yield from [(i, i + r) for i in range(lo + r, hi - r, step)]
    else:
        yield (lo, lo + r)


def _oddeven_merge_sort(lo, hi):
    if hi - lo >= 1:
        mid = lo + (hi - lo) // 2
        yield from _oddeven_merge_sort(lo, mid)
        yield from _oddeven_merge_sort(mid + 1, hi)
        yield from _oddeven_merge(lo, hi, 1)


_SORT16 = tuple(_oddeven_merge_sort(0, PEER_TOPK - 1))
_BITONIC16 = tuple((i, i + d) for d in (8, 4, 2, 1) for i in range(PEER_TOPK) if i & d == 0)


def _compare_exchange(xs, pairs):
    for i, j in pairs:
        for x in xs:
            x[i], x[j] = jnp.maximum(x[i], x[j]), jnp.minimum(x[i], x[j])
    return xs


def _top16_sorted(tiles):
    xs = _compare_exchange([[a3[i] for i in range(PEER_TOPK)] for a3 in tiles], _SORT16)
    for shift in (4, 2, 1):
        ys = [[pltpu.roll(v, shift, 0) for v in x] for x in xs]
        xs = _compare_exchange([[jnp.maximum(x[i], y[PEER_TOPK - 1 - i]) for i in range(PEER_TOPK)]
                                for x, y in zip(xs, ys)], _BITONIC16)
    return xs


def _sublane_block(rows):
    sub = lax.broadcasted_iota(jnp.int32, (SUBLANE, LANE), 0)
    blk = rows[0]
    for r in range(1, SUBLANE):
        blk = jnp.where(sub == r, rows[r], blk)
    return blk


def _route_tile(a1, a2):
    v1, v2 = _top16_sorted([a1, a2])
    sub = lax.broadcasted_iota(jnp.int32, (SUBLANE, LANE), 0)
    v2a = _sublane_block(v2[0:8])
    v2b = _sublane_block(v2[8:16])
    v1b = _sublane_block(v1[8:16])
    cands = [v1[0] + v2a, v1[0] + v2b]
    for r1 in range(1, 8):
        cands.append(jnp.where(sub < PEER_TOPK // (r1 + 1), v1[r1] + v2a, NEG_INF))
    cands.append(v1b + v2[0])
    thr = jnp.full((SUBLANE, LANE), NEG_INF, F32)
    need = jnp.full((SUBLANE, LANE), float(PEER_TOPK), F32)
    cur = cands
    for _ in range(PEER_TOPK):
        m8 = cur[0]
        for blk in cur[1:]:
            m8 = jnp.maximum(m8, blk)
        m = jnp.broadcast_to(jnp.max(m8, axis=0, keepdims=True), (SUBLANE, LANE))
        hits = [blk == m for blk in cur]
        cnt8 = jnp.where(hits[0], 1.0, 0.0)
        for hit in hits[1:]:
            cnt8 = cnt8 + jnp.where(hit, 1.0, 0.0)
        thr = jnp.where(need > 0.0, m, thr)
        need = need - jnp.broadcast_to(jnp.sum(cnt8, axis=0, keepdims=True), (SUBLANE, LANE))
        cur = [jnp.where(hit, NEG_INF, blk) for hit, blk in zip(hits, cur)]
    top = v1[0] + v2[0]
    z8 = jnp.zeros((SUBLANE, LANE), F32)
    for blk in cands:
        z8 = z8 + jnp.where(blk >= thr, jnp.exp(blk - top), 0.0)
    z = jnp.broadcast_to(jnp.sum(z8, axis=0, keepdims=True), (SUBLANE, LANE))
    n1 = jnp.zeros(a1.shape, F32)
    rank2 = jnp.zeros(a2.shape, F32)
    for r in range(PEER_TOPK):
        n1 = jnp.where(a1 + v2[r] >= thr, r + 1.0, n1)
        rank2 = jnp.where(v2[r] > a2, r + 1.0, rank2)
    e1 = jnp.exp(a1 - v1[0]) * SQRT_HALF
    e2 = jnp.exp(a2 - v2[0]) / z
    return n1, rank2, e1, e2


def _peer_kernel(xt_ref, wqt_ref, sk_ref, u0_ref, u_ref, vt_ref, p_ref, plet_ref, gatet_ref, g2_ref, b2_ref,
                 out_ref, xbf, s_scr, n1, e1, rk2, e2, h_scr, w_scr, acc, *, tn, te, nj):
    j = pl.program_id(1)
    lane_tiles = tn // LANE
    groups = (N_KEYS // SUBLANE, SUBLANE, LANE)

    def lane_slice(lt):
        return pl.ds(pl.multiple_of(lt * LANE, LANE), LANE)

    def put_h(slot, hval):
        for l2 in range(lane_tiles):
            h_scr[slot, l2] = hval[:, l2 * LANE:(l2 + 1) * LANE].astype(BF16)

    @pl.when(j == 0)
    def _route():
        xbf[...] = xt_ref[...].astype(BF16)

        def head_body(h, carry):
            r = pl.multiple_of(h * 2 * N_KEYS, 2 * N_KEYS)
            qh = _dot(wqt_ref[pl.ds(r, 2 * N_KEYS), :], xbf[...]).astype(BF16)
            s_scr[0] = _dot(sk_ref[h, 0], qh[0:N_KEYS])
            s_scr[1] = _dot(sk_ref[h, 1], qh[N_KEYS:2 * N_KEYS])

            def lane_body(lt, c):
                lanes = lane_slice(lt)
                n1_t, rank2_t, e1_t, e2_t = _route_tile(s_scr[0, :, lanes].reshape(groups),
                                                        s_scr[1, :, lanes].reshape(groups))
                n1[lt, h] = n1_t.reshape(N_KEYS, LANE)
                e1[lt, h] = e1_t.reshape(N_KEYS, LANE)
                rk2[lt, h] = rank2_t.reshape(N_KEYS, LANE).astype(BF16).reshape(rk2.shape[2:])
                e2[lt, h] = e2_t.reshape(N_KEYS, LANE).astype(BF16).reshape(e2.shape[2:])
                return c

            lax.fori_loop(0, lane_tiles, lane_body, 0)
            return carry

        lax.fori_loop(0, PEER_HEADS, head_body, 0)
        acc[...] = jnp.zeros_like(acc)

        put_h(0, _dot(u0_ref[...], xbf[...]))

    na = te // N_KEYS
    assert na == SUBLANE
    a0 = pl.multiple_of(jnp.minimum(j, nj - 1) * na, SUBLANE)
    cur = j % 2
    nxt = 1 - cur
    packed = rk2.shape[2:]
    row16 = (1, 2 * SUBLANE, LANE)

    @pl.when(j < nj - 1)
    def _next_h():
        put_h(nxt, _dot(u_ref[...], xbf[...]))

    @pl.when(j >= 1)
    def _prev_v():
        w_prev = jnp.concatenate([w_scr[nxt, l2] for l2 in range(lane_tiles)], axis=1)
        acc[...] += _dot(vt_ref[...], w_prev)

    @pl.when(j < nj)
    def _this_w():
        def lane_body(p, carry):
            n1blk = [n1[p, h, pl.ds(a0, na), :] for h in range(PEER_HEADS)]
            e1blk = [e1[p, h, pl.ds(a0, na), :] for h in range(PEER_HEADS)]
            for ai in range(0, na, 2):
                gates = [jnp.zeros(packed, BF16), jnp.zeros(packed, BF16)]
                for h in range(PEER_HEADS):
                    rk2h = rk2[p, h]
                    e2h = e2[p, h]
                    for d in range(2):
                        n1a = jnp.broadcast_to(n1blk[h][ai + d:ai + d + 1, :], row16[1:]).astype(BF16).reshape(row16)
                        e1a = jnp.broadcast_to(e1blk[h][ai + d:ai + d + 1, :], row16[1:]).astype(BF16).reshape(row16)
                        gates[d] = gates[d] + jnp.where(rk2h < n1a, e2h, jnp.zeros_like(e2h)) * e1a
                for d in range(2):
                    erows = slice((ai + d) * N_KEYS, (ai + d + 1) * N_KEYS)
                    hs = h_scr[cur, p, erows, :].reshape(packed)
                    w_scr[cur, p, erows, :] = (gates[d] * (hs + hs * lax.erf(hs))).reshape(N_KEYS, LANE)
            return carry

        lax.fori_loop(0, lane_tiles, lane_body, 0)

    @pl.when(j == nj)
    def _finish():
        def norm_body(lt, c):
            lanes = lane_slice(lt)
            y = ALPHA * xt_ref[:, lanes] + acc[:, lanes]
            mu = jnp.mean(y, axis=0, keepdims=True)
            d = y - mu
            var = jnp.mean(d * d, axis=0, keepdims=True)
            yn = d * lax.rsqrt(var + NORM_EPS) * g2_ref[...] + b2_ref[...]
            acc[:, lanes] = yn
            xbf[:, lanes] = yn.astype(BF16)
            return c

        lax.fori_loop(0, lane_tiles, norm_body, 0)
        emb = _dot_nt(plet_ref[...], p_ref[...].astype(BF16))
        gt = _dot(gatet_ref[...], xbf[...])
        out_ref[...] = (acc[...] + emb * _sigmoid(gt)).T


def _peer_call(xt, p2d, lw):
    t = xt.shape[1]
    tn = min(512, t)
    te = PEER_TE
    n_exp = lw["u"].shape[0]
    nj = n_exp // te
    lane_tiles = tn // LANE
    once = dict(pipeline_mode=pl.Buffered(1))
    in_specs = [
        pl.BlockSpec((D_MODEL, tn), lambda i, j: (0, i)),
        pl.BlockSpec(lw["wqt"].shape, lambda i, j: (0, 0), **once),
        pl.BlockSpec(lw["subkeys"].shape, lambda i, j: (0, 0, 0, 0), **once),
        pl.BlockSpec((te, D_MODEL), lambda i, j: (0, 0), **once),
        pl.BlockSpec((te, D_MODEL), lambda i, j: (jnp.minimum(j + 1, nj - 1), 0)),
        pl.BlockSpec((None, D_MODEL, te), lambda i, j: (jnp.maximum(j - 1, 0), 0, 0)),
        pl.BlockSpec((tn, PLE_DIM), lambda i, j: (i, 0)),
        pl.BlockSpec(lw["plet"].shape, lambda i, j: (0, 0), **once),
        pl.BlockSpec(lw["gatet"].shape, lambda i, j: (0, 0), **once),
        pl.BlockSpec((D_MODEL, LANE), lambda i, j: (0, 0), **once),
        pl.BlockSpec((D_MODEL, LANE), lambda i, j: (0, 0), **once),
    ]
    route = (lane_tiles, PEER_HEADS, N_KEYS, LANE)
    route_packed = (lane_tiles, PEER_HEADS, N_KEYS // (2 * SUBLANE), 2 * SUBLANE, LANE)
    scratch = [pltpu.VMEM((D_MODEL, tn), BF16), pltpu.VMEM((2, N_KEYS, tn), F32),
               pltpu.VMEM(route, F32), pltpu.VMEM(route, F32),
               pltpu.VMEM(route_packed, BF16), pltpu.VMEM(route_packed, BF16),
               pltpu.VMEM((2, lane_tiles, te, LANE), BF16), pltpu.VMEM((2, lane_tiles, te, LANE), BF16),
               pltpu.VMEM((D_MODEL, tn), F32)]
    return pl.pallas_call(
        functools.partial(_peer_kernel, tn=tn, te=te, nj=nj),
        grid=(t // tn, nj + 1),
        in_specs=in_specs,
        out_specs=pl.BlockSpec((tn, D_MODEL), lambda i, j: (i, 0)),
        out_shape=jax.ShapeDtypeStruct((t, D_MODEL), F32),
        scratch_shapes=scratch,
        compiler_params=pltpu.CompilerParams(dimension_semantics=("arbitrary", "arbitrary"),
                                             vmem_limit_bytes=VMEM_LIMIT),
        name="peer_ffn_ln_ple",
    )(xt, lw["wqt"], lw["subkeys"], lw["u"], lw["u"], lw["vt"], p2d, lw["plet"], lw["gatet"], lw["ln2_g"], lw["ln2_b"])


def _head_slots(w, d):
    r = w.shape[0]
    return jnp.pad(w.reshape(r, N_HEADS, d), ((0, 0), (0, 0), (0, SLOT - d))).reshape(r, GROUP_W)


def _rope_slots(w, d):
    r = w.shape[0]
    half = d // 2
    w = w.reshape(r, N_HEADS, 2, half)
    return jnp.pad(w, ((0, 0), (0, 0), (0, 0), (0, SLOT // 2 - half))).reshape(r, GROUP_W)


def _rope_tables(pos):
    half = RET_DK // 2
    inv = 1.0 / (ROPE_BASE ** (jnp.arange(0, RET_DK, 2, dtype=F32) / RET_DK))
    ang = pos[:, None] * inv[None, :]
    pad = ((0, 0), (0, SLOT // 2 - half))
    cos = jnp.pad(jnp.cos(ang), pad)
    sin = jnp.pad(jnp.sin(ang), pad)
    return jnp.concatenate([cos, cos], axis=1), jnp.concatenate([-sin, sin], axis=1)


def _retention_constants():
    log_gamma = jnp.log1p(-jnp.exp2(-5.0 - jnp.arange(N_HEADS, dtype=F32)))
    i = jnp.arange(CHUNK, dtype=F32)
    diff = i[:, None] - i[None, :]
    lg = log_gamma[:, None, None]
    dmat = jnp.where(diff >= 0, jnp.exp(jnp.where(diff >= 0, diff, 0.0) * lg), 0.0)
    qdec = jnp.broadcast_to(jnp.exp((i[None, :, None] + 1.0) * lg), (N_HEADS, CHUNK, SLOT))
    kdec = jnp.broadcast_to(jnp.exp((CHUNK - 1.0 - i[None, :, None]) * lg), (N_HEADS, CHUNK, SLOT))
    sdec = jnp.broadcast_to(jnp.exp(CHUNK * lg), (N_HEADS, 1, SLOT))
    gamma = jnp.broadcast_to(jnp.exp(lg), (N_HEADS, 1, SLOT))
    return dmat, qdec, kdec, sdec, gamma


def _layer_weights(i, lb, w_in, gla_w_gate, gla_b_gate, gla_norm, ret_norm, hgrn_norm, w_out,
                   ln1_g, ln1_b, ln2_g, ln2_b, peer_w_q, peer_subkeys, peer_u, peer_v, ple_proj, ple_gate):
    sizes = (N_HEADS * GLA_DK, N_HEADS * GLA_DK, N_HEADS * GLA_DV, N_HEADS * GLA_DV, GLA_LOWRANK,
             N_HEADS * RET_DK, N_HEADS * RET_DK, N_HEADS * RET_DV, N_HEADS * RET_DV,
             N_HEADS * HGRN_DK, N_HEADS * HGRN_DK, N_HEADS * HGRN_DV, N_HEADS * HGRN_DV)
    offs = [int(c) for c in np.cumsum(sizes)[:-1]]
    gq, gk, gv, gg, glr, rq, rk, rv, rg, hq, hf, hi, hg = jnp.split(w_in[i], offs, axis=1)
    lw = {}
    lw["w_gla"] = jnp.concatenate(
        [_head_slots(gq, GLA_DK), _head_slots(gk, GLA_DK), _head_slots(gv, GLA_DV), _head_slots(gg, GLA_DV),
         jnp.pad(glr, ((0, 0), (0, SLOT - GLA_LOWRANK)))], axis=1).astype(BF16)
    lw["w_ret"] = jnp.concatenate(
        [_rope_slots(rq, RET_DK), _rope_slots(rk, RET_DK), _head_slots(rv, RET_DV), _head_slots(rg, RET_DV)],
        axis=1).astype(BF16)
    lw["w_hgrn"] = jnp.concatenate([_head_slots(w, HGRN_DK) for w in (hq, hf, hi, hg)], axis=1).astype(BF16)
    lw["wgate"] = jnp.pad(_head_slots(gla_w_gate[i], GLA_DK), ((0, SLOT - GLA_LOWRANK), (0, 0)))
    lw["bgate"] = _head_slots(gla_b_gate[i][None, :], GLA_DK)
    lw["gla_norm"] = jnp.pad(gla_norm[i], ((0, 0), (0, SLOT - GLA_DV)))
    lw["ret_norm"] = jnp.pad(ret_norm[i], ((0, 0), (0, SLOT - RET_DV)))
    lw["hgrn_norm"] = jnp.pad(hgrn_norm[i], ((0, 0), (0, SLOT - HGRN_DV)))
    lbi = lb[i].reshape(1, N_HEADS, HGRN_DK)
    pad = ((0, 0), (0, 0), (0, SLOT - HGRN_DK))
    lw["loglb"] = jnp.pad(jnp.log(lbi), pad, constant_values=-1.0).reshape(1, GROUP_W)
    lw["l1mlb"] = jnp.pad(jnp.log1p(-lbi), pad, constant_values=-1.0).reshape(1, GROUP_W)
    wo = w_out[i]
    g_rows, r_rows = N_HEADS * GLA_DV, N_HEADS * RET_DV
    lw["w_out"] = jnp.stack([
        _head_slots(wo[:g_rows].T, GLA_DV).T, _head_slots(wo[g_rows:g_rows + r_rows].T, RET_DV).T,
        _head_slots(wo[g_rows + r_rows:].T, HGRN_DV).T]).astype(BF16)
    lw["ln1_g"], lw["ln1_b"] = ln1_g[i][None, :], ln1_b[i][None, :]
    lw["ln2_g"] = jnp.broadcast_to(ln2_g[i][:, None], (D_MODEL, LANE))
    lw["ln2_b"] = jnp.broadcast_to(ln2_b[i][:, None], (D_MODEL, LANE))
    lw["wqt"] = peer_w_q[i].T.astype(BF16)
    lw["subkeys"] = peer_subkeys[i].astype(BF16)
    lw["u"] = (peer_u[i] * SQRT_HALF).astype(BF16)
    lw["vt"] = jnp.swapaxes(peer_v[i].astype(BF16).reshape(-1, PEER_TE, D_MODEL), 1, 2)
    lw["plet"] = ple_proj[i].T.astype(BF16)
    lw["gatet"] = ple_gate[i].T.astype(BF16)
    return lw


def _unslot_state(st, dk, dv, rope=False):
    if rope:
        half = dk // 2
        st = jnp.concatenate([st[..., :half], st[..., SLOT // 2:SLOT // 2 + half]], axis=-1)
    return jnp.swapaxes(st[:, :, :dv, :dk], 2, 3)


def kernel(x_prompt, x_sample, p_prompt, p_sample, state_gla, state_ret, state_hgrn, w_in, gla_w_gate,
           gla_b_gate, gla_norm, ret_norm, hgrn_lb_logits, hgrn_norm, w_out, ln1_g, ln1_b, ln2_g, ln2_b,
           peer_w_q, peer_subkeys, peer_u, peer_v, ple_proj, ple_gate):
    bp, lp, _ = x_prompt.shape
    bs = x_sample.shape[0]
    assert x_sample.shape[1] == 1 and lp % CHUNK == 0

    lb = jnp.cumsum(jax.nn.softmax(hgrn_lb_logits.astype(F32), axis=0), axis=0)
    lb = lb - lb[0:1]
    tri_np, masks_np = _chunk_constants()
    tri, masks = jnp.asarray(tri_np, BF16), jnp.asarray(masks_np)
    dmat, qdec, kdec, sdec, gamma = _retention_constants()
    cos_p, sin_p = _rope_tables(jnp.arange(lp, dtype=F32))
    cos_s, sin_s = _rope_tables(PAST_LEN + jnp.arange(1, dtype=F32))

    xp = x_prompt.reshape(bp * lp, D_MODEL)
    xs = x_sample.reshape(bs, D_MODEL)
    gla_p, ret_p, hgrn_p, gla_s, ret_s, hgrn_s = [], [], [], [], [], []
    for i in range(DEPTH):
        lw = _layer_weights(i, lb, w_in, gla_w_gate, gla_b_gate, gla_norm, ret_norm, hgrn_norm, w_out,
                            ln1_g, ln1_b, ln2_g, ln2_b, peer_w_q, peer_subkeys, peer_u, peer_v,
                            ple_proj, ple_gate)
        lw.update(cos_s=cos_s, sin_s=sin_s, gamma=gamma)

        og, sg = _prompt_mixer_call(
            _gla_prompt_kernel, xp, bp, lp, lw["w_gla"],
            [lw["wgate"], lw["bgate"], lw["gla_norm"], tri, masks],
            [lw["w_gla"].shape[1], GROUP_W], "gla_prompt")
        orr, sr = _prompt_mixer_call(
            _ret_prompt_kernel, xp, bp, lp, lw["w_ret"],
            [(cos_p, None), (sin_p, None), lw["ret_norm"], dmat, qdec, kdec, sdec],
            [4 * GROUP_W], "ret_prompt")
        oh, sh = _prompt_mixer_call(
            _hgrn_prompt_kernel, xp, bp, lp, lw["w_hgrn"],
            [lw["loglb"], lw["l1mlb"], lw["hgrn_norm"], tri, masks],
            [4 * GROUP_W], "hgrn_prompt")
        xp = _peer_call(_out_call(og, orr, oh, xp, lw), p_prompt[i].reshape(bp * lp, PLE_DIM), lw)
        gla_p.append(_unslot_state(sg, GLA_DK, GLA_DV))
        ret_p.append(_unslot_state(sr, RET_DK, RET_DV, rope=True))
        hgrn_p.append(_unslot_state(sh, HGRN_DK, HGRN_DV))

        o_s, nsg, nsr, nsh = _sample_mixer_call(xs, lw, state_gla[i], state_ret[i], state_hgrn[i])
        xs = _peer_call(
            _out_call(o_s[:, :GROUP_W], o_s[:, GROUP_W:2 * GROUP_W], o_s[:, 2 * GROUP_W:], xs, lw),
            p_sample[i].reshape(bs, PLE_DIM), lw)
        gla_s.append(nsg.reshape(bs, N_HEADS, GLA_DK, GLA_DV))
        ret_s.append(nsr.reshape(bs, N_HEADS, RET_DK, RET_DV))
        hgrn_s.append(nsh.reshape(bs, N_HEADS, HGRN_DK, HGRN_DV))

    return (xp.reshape(bp, lp, D_MODEL), xs.reshape(bs, 1, D_MODEL),
            jnp.stack(gla_p), jnp.stack(ret_p), jnp.stack(hgrn_p),
            jnp.stack(gla_s), jnp.stack(ret_s), jnp.stack(hgrn_s))
```

```python
import functools

import numpy as np
import jax
import jax.numpy as jnp
from jax import lax
from jax.experimental import pallas as pl
from jax.experimental.pallas import tpu as pltpu

F32 = jnp.float32
BF16 = jnp.bfloat16

D_MODEL = 1024
DEPTH = 2
PAST_LEN = 16384
N_HEADS = 4
GLA_DK, GLA_DV = 48, 96
RET_DK, RET_DV = 48, 96
HGRN_DK, HGRN_DV = 64, 64
GLA_LOWRANK = 16
GLA_TAU = 16.0
ROPE_BASE = 10000.0
CHUNK = 64
GLA_UNROLL = 4
HGRN_UNROLL = 8
RET_UNROLL = 8
PEER_HEADS = 8
N_KEYS = 128
PEER_TOPK = 16
PEER_TE = 2048
PLE_DIM = 256
ALPHA = (2 * DEPTH) ** 0.25
NORM_EPS = 1e-5

LANE = 128
SUBLANE = 8
SLOT = LANE
GROUP_W = N_HEADS * SLOT
VMEM_LIMIT = 56 * 1024 * 1024
NEG_INF = float("-inf")


def _dot(a, b):
    return jnp.dot(a, b, preferred_element_type=F32)


def _dot_f32(a, b):
    return jnp.dot(a, b, preferred_element_type=F32, precision=lax.Precision.HIGHEST)


def _dot_nt(a, b):
    return lax.dot_general(a, b, (((1,), (1,)), ((), ())), preferred_element_type=F32)


def _dot_tn(a, b):
    return lax.dot_general(a, b, (((0,), (0,)), ((), ())), preferred_element_type=F32)


def _sigmoid(x):
    return jax.nn.sigmoid(x)


def _silu(x):
    return x * _sigmoid(x)


def _log_sigmoid(x):
    return jnp.minimum(x, 0.0) - jnp.log1p(jnp.exp(-jnp.abs(x)))


def _logaddexp(a, c):
    amax = jnp.maximum(a, c)
    delta = a - c
    return jnp.where(jnp.isnan(delta), a + c, amax + jnp.log1p(jnp.exp(-jnp.abs(delta))))


SQRT_HALF = np.float32(0.7071067811865476)


def _lane_mask(n):
    return (lax.broadcasted_iota(jnp.int32, (1, LANE), 1) < n).astype(F32)


def _rms_head_norm(o, g_row, dv):
    ms = jnp.sum(o * o, axis=-1, keepdims=True) * (1.0 / dv)
    return o * lax.rsqrt(ms + NORM_EPS) * g_row


def _group_head_norm(o, g_row, dv):
    mask = _lane_mask(dv)
    mu = jnp.sum(o, axis=-1, keepdims=True) * (1.0 / dv)
    d = (o - mu) * mask
    var = jnp.sum(d * d, axis=-1, keepdims=True) * (1.0 / dv)
    return d * lax.rsqrt(var + NORM_EPS) * g_row


def _chunk_constants():
    i = np.arange(CHUNK)[:, None]
    t = np.arange(CHUNK)[None, :]
    masks = [i == t]
    half = CHUNK // 2
    while half >= 1:
        blk = i // (2 * half)
        second = (i % (2 * half)) >= half
        masks.append(second & ((t % (2 * half)) < half) & (blk == t // (2 * half)))
        half //= 2
    return (t <= i).astype(np.float32), np.stack(masks).astype(np.float32)


def _cumsum_rows(tri_bf, g):
    hi = g.astype(BF16)
    r1 = g - hi.astype(F32)
    mid = r1.astype(BF16)
    lo = (r1 - mid.astype(F32)).astype(BF16)
    return _dot(tri_bf, hi) + _dot(tri_bf, mid) + _dot(tri_bf, lo)


def _level_factors(b, g):
    width = b.shape[1]
    grouped = (CHUNK // SUBLANE, SUBLANE, width)
    row = lax.broadcasted_iota(jnp.int32, b.shape, 0)
    sub = lax.broadcasted_iota(jnp.int32, grouped, 1)
    b3 = b.reshape(grouped)

    def sub_ref(r):
        return jnp.broadcast_to(b3[:, r:r + 1, :], grouped)

    out = []
    half = CHUNK // 2
    while half >= SUBLANE:
        ref = jnp.concatenate(
            [jnp.broadcast_to(b[m * 2 * half + half - 1:m * 2 * half + half], (2 * half, width))
             for m in range(CHUNK // (2 * half))], axis=0)
        out.append(jnp.exp(-jnp.abs(b - ref)))
        half //= 2
    out.append(jnp.exp(-jnp.abs(b3 - sub_ref(3))).reshape(b.shape))
    out.append(jnp.exp(-jnp.abs(b3 - jnp.where(sub < 4, sub_ref(1), sub_ref(5)))).reshape(b.shape))
    out.append(jnp.exp(jnp.where(row % 2 == 1, g, 0.0)))
    return out


def _vector_decay_chunk(q, k, v, g, st_ref, tri_bf, masks_ref):
    heads = [slice(h * SLOT, (h + 1) * SLOT) for h in range(N_HEADS)]
    b = _cumsum_rows(tri_bf, g)
    b_last = b[CHUNK - 1:CHUNK]
    factors = _level_factors(b, g)
    qs = [q.astype(BF16)] + [(q * f).astype(BF16) for f in factors]
    ks = [k.astype(BF16)] + [(k * f).astype(BF16) for f in factors]
    q_in = (q * jnp.exp(b)).astype(BF16)
    k_out = (k * jnp.exp(b_last - b)).astype(BF16)
    decay = jnp.exp(b_last)
    vb = v.astype(BF16)
    scores = []
    for sl in heads:
        sc = masks_ref[0] * _dot_nt(qs[0][:, sl], ks[0][:, sl])
        for l in range(len(factors)):
            sc = sc + masks_ref[1 + l] * _dot_nt(qs[1 + l][:, sl], ks[1 + l][:, sl])
        scores.append(sc.astype(BF16))
    outs = []
    for h, sl in enumerate(heads):
        st = st_ref[0, h]
        outs.append(_dot(scores[h], vb[:, sl]) + _dot_nt(q_in[:, sl], st.astype(BF16)))
        st_ref[0, h] = st * decay[:, sl] + _dot_tn(vb[:, sl], k_out[:, sl])
    return outs


def _gla_prompt_kernel(x_ref, w_ref, wg_ref, bg_ref, nrm_ref, tri_ref, masks_ref,
                       o_ref, st_ref, z_scr, la_scr, *, seg):
    @pl.when(pl.program_id(1) == 0)
    def _():
        st_ref[...] = jnp.zeros_like(st_ref)

    z_scr[...] = _dot(x_ref[...].astype(BF16), w_ref[...])
    pre = _dot_f32(z_scr[:, 4 * GROUP_W:4 * GROUP_W + SLOT], wg_ref[...]) + bg_ref[...]
    la_scr[...] = _log_sigmoid(pre) * (1.0 / GLA_TAU)
    tri = tri_ref[...]

    def body(c, carry):
        r0 = pl.multiple_of(c * CHUNK, CHUNK)
        rows = pl.ds(r0, CHUNK)
        q = z_scr[rows, 0:GROUP_W] * (GLA_DK ** -0.5)
        k = z_scr[rows, GROUP_W:2 * GROUP_W]
        v = z_scr[rows, 2 * GROUP_W:3 * GROUP_W]
        outs = _vector_decay_chunk(q, k, v, la_scr[rows, :], st_ref, tri, masks_ref)
        for h in range(N_HEADS):
            gate = z_scr[rows, 3 * GROUP_W + h * SLOT:3 * GROUP_W + (h + 1) * SLOT]
            on = _rms_head_norm(outs[h], nrm_ref[h:h + 1, :], GLA_DV)
            o_ref[rows, h * SLOT:(h + 1) * SLOT] = (on * _silu(gate)).astype(BF16)
        return carry

    lax.fori_loop(0, seg // CHUNK, body, 0, unroll=GLA_UNROLL)


def _hgrn_prompt_kernel(x_ref, w_ref, loglb_ref, l1mlb_ref, nrm_ref, tri_ref, masks_ref,
                        o_ref, st_ref, z_scr, *, seg):
    @pl.when(pl.program_id(1) == 0)
    def _():
        st_ref[...] = jnp.zeros_like(st_ref)

    z_scr[...] = _dot(x_ref[...].astype(BF16), w_ref[...])
    tri = tri_ref[...]
    kmask = jnp.concatenate([_lane_mask(HGRN_DK)] * N_HEADS, axis=1)

    def body(c, carry):
        r0 = pl.multiple_of(c * CHUNK, CHUNK)
        rows = pl.ds(r0, CHUNK)
        q = _silu(z_scr[rows, 0:GROUP_W])
        v = z_scr[rows, 2 * GROUP_W:3 * GROUP_W]
        log_f = _logaddexp(loglb_ref[...], l1mlb_ref[...] + _log_sigmoid(z_scr[rows, GROUP_W:2 * GROUP_W]))
        k = (1.0 - jnp.exp(log_f)) * kmask
        outs = _vector_decay_chunk(q, k, v, log_f, st_ref, tri, masks_ref)
        for h in range(N_HEADS):
            sl = slice(h * SLOT, (h + 1) * SLOT)
            gate = z_scr[rows, 3 * GROUP_W + h * SLOT:3 * GROUP_W + (h + 1) * SLOT]
            on = _rms_head_norm(outs[h], nrm_ref[h:h + 1, :], HGRN_DV)
            o_ref[rows, sl] = (on * _silu(gate)).astype(BF16)
        return carry

    lax.fori_loop(0, seg // CHUNK, body, 0, unroll=HGRN_UNROLL)


def _rotate(t, cs, sn):
    return t * cs + pltpu.roll(t, SLOT // 2, 1) * sn


def _ret_prompt_kernel(x_ref, w_ref, cos_ref, sin_ref, nrm_ref, dmat_ref, qdec_ref, kdec_ref, sdec_ref,
                       o_ref, st_ref, z_scr, *, seg):
    @pl.when(pl.program_id(1) == 0)
    def _():
        st_ref[...] = jnp.zeros_like(st_ref)

    z_scr[...] = _dot(x_ref[...].astype(BF16), w_ref[...])

    def body(c, carry):
        r0 = pl.multiple_of(c * CHUNK, CHUNK)
        rows = pl.ds(r0, CHUNK)
        cs = cos_ref[rows, :]
        sn = sin_ref[rows, :]
        heads = [slice(h * SLOT, (h + 1) * SLOT) for h in range(N_HEADS)]
        qs = [_rotate(z_scr[rows, h * SLOT:(h + 1) * SLOT], cs, sn) for h in range(N_HEADS)]
        ks = [_rotate(z_scr[rows, GROUP_W + h * SLOT:GROUP_W + (h + 1) * SLOT], cs, sn) * (RET_DK ** -0.5)
              for h in range(N_HEADS)]
        vb = z_scr[rows, 2 * GROUP_W:3 * GROUP_W].astype(BF16)
        scores = [(dmat_ref[h] * _dot_nt(qs[h].astype(BF16), ks[h].astype(BF16))).astype(BF16)
                  for h in range(N_HEADS)]
        q_in = [(qs[h] * qdec_ref[h]).astype(BF16) for h in range(N_HEADS)]
        k_out = [(ks[h] * kdec_ref[h]).astype(BF16) for h in range(N_HEADS)]
        outs = []
        for h, sl in enumerate(heads):
            st = st_ref[0, h]
            outs.append(_dot(scores[h], vb[:, sl]) + _dot_nt(q_in[h], st.astype(BF16)))
            st_ref[0, h] = st * sdec_ref[h] + _dot_tn(vb[:, sl], k_out[h])
        for h, sl in enumerate(heads):
            gate = z_scr[rows, 3 * GROUP_W + h * SLOT:3 * GROUP_W + (h + 1) * SLOT]
            on = _group_head_norm(outs[h], nrm_ref[h:h + 1, :], RET_DV)
            o_ref[rows, sl] = (on * _silu(gate)).astype(BF16)
        return carry

    lax.fori_loop(0, seg // CHUNK, body, 0, unroll=RET_UNROLL)


def _const_spec(shape):
    nd = len(shape)
    return pl.BlockSpec(shape, lambda *_: (0,) * nd)


def _prompt_mixer_call(kernel, x2d, batch, seq, w, extras, scratch_widths, name):
    seg = min(512, seq)
    nseg = seq // seg
    in_specs = [pl.BlockSpec((seg, D_MODEL), lambda b, s: (b * nseg + s, 0)), _const_spec(w.shape)]
    args = [x2d, w]
    for e in extras:
        if isinstance(e, tuple):
            arr, _ = e
            in_specs.append(pl.BlockSpec((seg, arr.shape[1]), lambda b, s: (s, 0)))
            args.append(arr)
        else:
            in_specs.append(_const_spec(e.shape))
            args.append(e)
    return pl.pallas_call(
        functools.partial(kernel, seg=seg),
        grid=(batch, nseg),
        in_specs=in_specs,
        out_specs=[pl.BlockSpec((seg, GROUP_W), lambda b, s: (b * nseg + s, 0)),
                   pl.BlockSpec((1, N_HEADS, SLOT, SLOT), lambda b, s: (b, 0, 0, 0))],
        out_shape=[jax.ShapeDtypeStruct((batch * seq, GROUP_W), BF16),
                   jax.ShapeDtypeStruct((batch, N_HEADS, SLOT, SLOT), F32)],
        scratch_shapes=[pltpu.VMEM((seg, wd), F32) for wd in scratch_widths],
        compiler_params=pltpu.CompilerParams(dimension_semantics=("arbitrary", "arbitrary"),
                                             vmem_limit_bytes=VMEM_LIMIT),
        name=name,
    )(*args)


def _sample_step(q, k, eg, v, s_ref, ns_ref, tq, tk, te, tv, to, s_t, sn_t, dk, dv, row_of):
    tq[...] = q.T
    tk[...] = k.T
    te[...] = eg.T
    tv[...] = v.T
    s_t[0:dk * dv, :] = s_ref[...].T
    vt = tv[0:dv, :]

    def body(kk, oacc):
        kr = row_of(kk)
        r = pl.multiple_of(kk * dv, SUBLANE)
        sn = s_t[pl.ds(r, dv), :] * te[pl.ds(kr, 1), :] + tk[pl.ds(kr, 1), :] * vt
        sn_t[pl.ds(r, dv), :] = sn
        return oacc + tq[pl.ds(kr, 1), :] * sn

    o_t = lax.fori_loop(0, dk, body, jnp.zeros((dv, q.shape[0]), F32))
    ns_ref[...] = sn_t[0:dk * dv, :].T
    to[...] = jnp.zeros_like(to)
    to[0:dv, :] = o_t
    return to[...].T


def _sample_mixer_kernel(x_ref, wg_ref, wr_ref, wh_ref, wgate_ref, bgate_ref, gn_ref, rn_ref, hn_ref,
                         cos_ref, sin_ref, gam_ref, loglb_ref, l1mlb_ref, sg_ref, sr_ref, sh_ref,
                         o_ref, nsg_ref, nsr_ref, nsh_ref,
                         zg, zr, zh, la, tq, tk, te, tv, to, s_t, sn_t):
    h = pl.program_id(0)

    @pl.when(h == 0)
    def _():
        xb = x_ref[...].astype(BF16)
        zg[...] = _dot(xb, wg_ref[...])
        zr[...] = _dot(xb, wr_ref[...])
        zh[...] = _dot(xb, wh_ref[...])
        pre = _dot_f32(zg[:, 4 * GROUP_W:4 * GROUP_W + SLOT], wgate_ref[...]) + bgate_ref[...]
        la[...] = _log_sigmoid(pre) * (1.0 / GLA_TAU)

    off = pl.multiple_of(h * SLOT, SLOT)
    sl = pl.ds(off, SLOT)
    tr = (tq, tk, te, tv, to, s_t, sn_t)
    batch = x_ref.shape[0]

    q = zg[:, sl] * (GLA_DK ** -0.5)
    k = zg[:, pl.ds(GROUP_W + off, SLOT)]
    v = zg[:, pl.ds(2 * GROUP_W + off, SLOT)]
    gate = zg[:, pl.ds(3 * GROUP_W + off, SLOT)]
    o = _sample_step(q, k, jnp.exp(la[:, sl]), v, sg_ref, nsg_ref, *tr, GLA_DK, GLA_DV, lambda kk: kk)
    o_ref[:, sl] = (_rms_head_norm(o, gn_ref[pl.ds(h, 1), :], GLA_DV) * _silu(gate)).astype(BF16)

    cs = cos_ref[...]
    sn = sin_ref[...]
    q = _rotate(zr[:, sl], cs, sn)
    k = _rotate(zr[:, pl.ds(GROUP_W + off, SLOT)], cs, sn) * (RET_DK ** -0.5)
    v = zr[:, pl.ds(2 * GROUP_W + off, SLOT)]
    gate = zr[:, pl.ds(3 * GROUP_W + off, SLOT)]
    eg = jnp.broadcast_to(gam_ref[h], (batch, SLOT))
    half = RET_DK // 2
    o = _sample_step(q, k, eg, v, sr_ref, nsr_ref, *tr, RET_DK, RET_DV,
                     lambda kk: kk + jnp.where(kk >= half, SLOT // 2 - half, 0))
    o_ref[:, pl.ds(GROUP_W + off, SLOT)] = (
        _group_head_norm(o, rn_ref[pl.ds(h, 1), :], RET_DV) * _silu(gate)).astype(BF16)

    q = _silu(zh[:, sl])
    hf = zh[:, pl.ds(GROUP_W + off, SLOT)]
    v = zh[:, pl.ds(2 * GROUP_W + off, SLOT)]
    gate = zh[:, pl.ds(3 * GROUP_W + off, SLOT)]
    log_f = _logaddexp(loglb_ref[:, sl], l1mlb_ref[:, sl] + _log_sigmoid(hf))
    f = jnp.exp(log_f)
    o = _sample_step(q, (1.0 - f) * _lane_mask(HGRN_DK), f, v, sh_ref, nsh_ref, *tr,
                     HGRN_DK, HGRN_DV, lambda kk: kk)
    o_ref[:, pl.ds(2 * GROUP_W + off, SLOT)] = (
        _rms_head_norm(o, hn_ref[pl.ds(h, 1), :], HGRN_DV) * _silu(gate)).astype(BF16)


def _sample_mixer_call(x2d, lw, sg, sr, sh):
    batch = x2d.shape[0]
    gsz, hsz = GLA_DK * GLA_DV, HGRN_DK * HGRN_DV
    consts = [lw["w_gla"], lw["w_ret"], lw["w_hgrn"], lw["wgate"], lw["bgate"], lw["gla_norm"], lw["ret_norm"],
              lw["hgrn_norm"], lw["cos_s"], lw["sin_s"], lw["gamma"], lw["loglb"], lw["l1mlb"]]
    in_specs = ([_const_spec(x2d.shape)] + [_const_spec(c.shape) for c in consts]
                + [pl.BlockSpec((batch, gsz), lambda h: (0, h)),
                   pl.BlockSpec((batch, gsz), lambda h: (0, h)),
                   pl.BlockSpec((batch, hsz), lambda h: (0, h))])
    out_specs = [_const_spec((batch, 3 * GROUP_W)),
                 pl.BlockSpec((batch, gsz), lambda h: (0, h)),
                 pl.BlockSpec((batch, gsz), lambda h: (0, h)),
                 pl.BlockSpec((batch, hsz), lambda h: (0, h))]
    out_shape = [jax.ShapeDtypeStruct((batch, 3 * GROUP_W), BF16),
                 jax.ShapeDtypeStruct((batch, N_HEADS * gsz), F32),
                 jax.ShapeDtypeStruct((batch, N_HEADS * gsz), F32),
                 jax.ShapeDtypeStruct((batch, N_HEADS * hsz), F32)]
    scratch = [pltpu.VMEM((batch, lw["w_gla"].shape[1]), F32), pltpu.VMEM((batch, 4 * GROUP_W), F32),
               pltpu.VMEM((batch, 4 * GROUP_W), F32), pltpu.VMEM((batch, GROUP_W), F32)]
    scratch += [pltpu.VMEM((SLOT, batch), F32) for _ in range(5)]
    scratch += [pltpu.VMEM((gsz, batch), F32), pltpu.VMEM((gsz, batch), F32)]
    return pl.pallas_call(
        _sample_mixer_kernel,
        grid=(N_HEADS,),
        in_specs=in_specs, out_specs=out_specs, out_shape=out_shape, scratch_shapes=scratch,
        compiler_params=pltpu.CompilerParams(dimension_semantics=("arbitrary",), vmem_limit_bytes=VMEM_LIMIT),
        name="sample_mixer",
    )(x2d, *consts, sg.reshape(batch, -1), sr.reshape(batch, -1), sh.reshape(batch, -1))


def _out_kernel(og_ref, or_ref, oh_ref, x_ref, wo_ref, g_ref, b_ref, xt_ref):
    mix = _dot(og_ref[...], wo_ref[0]) + _dot(or_ref[...], wo_ref[1]) + _dot(oh_ref[...], wo_ref[2])
    y = ALPHA * x_ref[...] + mix
    mu = jnp.mean(y, axis=-1, keepdims=True)
    d = y - mu
    var = jnp.mean(d * d, axis=-1, keepdims=True)
    xt_ref[...] = (d * lax.rsqrt(var + NORM_EPS) * g_ref[...] + b_ref[...]).T


def _out_call(og, orr, oh, x2d, lw):
    t = x2d.shape[0]
    tm = min(512, t)
    row = lambda i: (i, 0)
    return pl.pallas_call(
        _out_kernel,
        grid=(t // tm,),
        in_specs=[pl.BlockSpec((tm, GROUP_W), row), pl.BlockSpec((tm, GROUP_W), row),
                  pl.BlockSpec((tm, GROUP_W), row), pl.BlockSpec((tm, D_MODEL), row),
                  _const_spec(lw["w_out"].shape), _const_spec((1, D_MODEL)), _const_spec((1, D_MODEL))],
        out_specs=pl.BlockSpec((D_MODEL, tm), lambda i: (0, i)),
        out_shape=jax.ShapeDtypeStruct((D_MODEL, t), F32),
        compiler_params=pltpu.CompilerParams(dimension_semantics=("arbitrary",), vmem_limit_bytes=VMEM_LIMIT),
        name="out_proj_ln",
    )(og, orr, oh, x2d, lw["w_out"], lw["ln1_g"], lw["ln1_b"])


def _oddeven_merge(lo, hi, r):
    step = r * 2
    if step < hi - lo:
        yield from _oddeven_merge(lo, hi, step)
        yield from _oddeven_merge(lo + r, hi, step)
        yield from [(i, i + r) for i in range(lo + r, hi - r, step)]
    else:
        yield (lo, lo + r)


def _oddeven_merge_sort(lo, hi):
    if hi - lo >= 1:
        mid = lo + (hi - lo) // 2
        yield from _oddeven_merge_sort(lo, mid)
        yield from _oddeven_merge_sort(mid + 1, hi)
        yield from _oddeven_merge(lo, hi, 1)


_SORT16 = tuple(_oddeven_merge_sort(0, PEER_TOPK - 1))
_BITONIC16 = tuple((i, i + d) for d in (8, 4, 2, 1) for i in range(PEER_TOPK) if i & d == 0)


def _compare_exchange(xs, pairs):
    for i, j in pairs:
        for x in xs:
            x[i], x[j] = jnp.maximum(x[i], x[j]), jnp.minimum(x[i], x[j])
    return xs


def _top16_sorted(tiles):
    xs = _compare_exchange([[a3[i] for i in range(PEER_TOPK)] for a3 in tiles], _SORT16)
    for shift in (4, 2, 1):
        ys = [[pltpu.roll(v, shift, 0) for v in x] for x in xs]
        xs = _compare_exchange([[jnp.maximum(x[i], y[PEER_TOPK - 1 - i]) for i in range(PEER_TOPK)]
                                for x, y in zip(xs, ys)], _BITONIC16)
    return xs


def _sublane_block(rows):
    sub = lax.broadcasted_iota(jnp.int32, (SUBLANE, LANE), 0)
    blk = rows[0]
    for r in range(1, SUBLANE):
        blk = jnp.where(sub == r, rows[r], blk)
    return blk


def _route_tile(a1, a2):
    v1, v2 = _top16_sorted([a1, a2])
    sub = lax.broadcasted_iota(jnp.int32, (SUBLANE, LANE), 0)
    v2a = _sublane_block(v2[0:8])
    v2b = _sublane_block(v2[8:16])
    v1b = _sublane_block(v1[8:16])
    cands = [v1[0] + v2a, v1[0] + v2b]
    for r1 in range(1, 8):
        cands.append(jnp.where(sub < PEER_TOPK // (r1 + 1), v1[r1] + v2a, NEG_INF))
    cands.append(v1b + v2[0])
    thr = jnp.full((SUBLANE, LANE), NEG_INF, F32)
    need = jnp.full((SUBLANE, LANE), float(PEER_TOPK), F32)
    cur = cands
    for _ in range(PEER_TOPK):
        m8 = cur[0]
        for blk in cur[1:]:
            m8 = jnp.maximum(m8, blk)
        m = jnp.broadcast_to(jnp.max(m8, axis=0, keepdims=True), (SUBLANE, LANE))
        hits = [blk == m for blk in cur]
        cnt8 = jnp.where(hits[0], 1.0, 0.0)
        for hit in hits[1:]:
            cnt8 = cnt8 + jnp.where(hit, 1.0, 0.0)
        thr = jnp.where(need > 0.0, m, thr)
        need = need - jnp.broadcast_to(jnp.sum(cnt8, axis=0, keepdims=True), (SUBLANE, LANE))
        cur = [jnp.where(hit, NEG_INF, blk) for hit, blk in zip(hits, cur)]
    top = v1[0] + v2[0]
    z8 = jnp.zeros((SUBLANE, LANE), F32)
    for blk in cands:
        z8 = z8 + jnp.where(blk >= thr, jnp.exp(blk - top), 0.0)
    z = jnp.broadcast_to(jnp.sum(z8, axis=0, keepdims=True), (SUBLANE, LANE))
    n1 = jnp.zeros(a1.shape, F32)
    rank2 = jnp.zeros(a2.shape, F32)
    for r in range(PEER_TOPK):
        n1 = jnp.where(a1 + v2[r] >= thr, r + 1.0, n1)
        rank2 = jnp.where(v2[r] > a2, r + 1.0, rank2)
    e1 = jnp.exp(a1 - v1[0]) * SQRT_HALF
    e2 = jnp.exp(a2 - v2[0]) / z
    return n1, rank2, e1, e2


def _peer_kernel(xt_ref, wqt_ref, sk_ref, u_ref, vt_ref, p_ref, plet_ref, gatet_ref, g2_ref, b2_ref,
                 out_ref, xbf, s_scr, n1, e1, rk2, e2, h_scr, w_scr, acc, *, tn, te, nj):
    j = pl.program_id(1)
    lane_tiles = tn // LANE
    groups = (N_KEYS // SUBLANE, SUBLANE, LANE)

    def lane_slice(lt):
        return pl.ds(pl.multiple_of(lt * LANE, LANE), LANE)

    @pl.when(j == 0)
    def _route():
        xbf[...] = xt_ref[...].astype(BF16)

        def head_body(h, carry):
            r = pl.multiple_of(h * 2 * N_KEYS, 2 * N_KEYS)
            qh = _dot(wqt_ref[pl.ds(r, 2 * N_KEYS), :], xbf[...]).astype(BF16)
            s_scr[0] = _dot(sk_ref[h, 0], qh[0:N_KEYS])
            s_scr[1] = _dot(sk_ref[h, 1], qh[N_KEYS:2 * N_KEYS])

            def lane_body(lt, c):
                lanes = lane_slice(lt)
                n1_t, rank2_t, e1_t, e2_t = _route_tile(s_scr[0, :, lanes].reshape(groups),
                                                        s_scr[1, :, lanes].reshape(groups))
                n1[lt, h] = n1_t.reshape(N_KEYS, LANE)
                e1[lt, h] = e1_t.reshape(N_KEYS, LANE)
                rk2[lt, h] = rank2_t.reshape(N_KEYS, LANE).astype(BF16).reshape(rk2.shape[2:])
                e2[lt, h] = e2_t.reshape(N_KEYS, LANE).astype(BF16).reshape(e2.shape[2:])
                return c

            lax.fori_loop(0, lane_tiles, lane_body, 0)
            return carry

        lax.fori_loop(0, PEER_HEADS, head_body, 0)
        acc[...] = jnp.zeros_like(acc)

    na = te // N_KEYS
    assert na % SUBLANE == 0
    packed = rk2.shape[2:]
    row16 = (1, 2 * SUBLANE, LANE)

    hval = _dot(u_ref[...], xbf[...])
    for l2 in range(lane_tiles):
        h_scr[l2] = hval[:, l2 * LANE:(l2 + 1) * LANE].astype(BF16)

    def gate_body(p, carry):
        for grp in range(na // SUBLANE):
            a0 = pl.multiple_of(j * na + grp * SUBLANE, SUBLANE)
            n1blk = [n1[p, h, pl.ds(a0, SUBLANE), :] for h in range(PEER_HEADS)]
            e1blk = [e1[p, h, pl.ds(a0, SUBLANE), :] for h in range(PEER_HEADS)]
            for ai in range(0, SUBLANE, 2):
                gates = [jnp.zeros(packed, BF16), jnp.zeros(packed, BF16)]
                for h in range(PEER_HEADS):
                    rk2h = rk2[p, h]
                    e2h = e2[p, h]
                    for d in range(2):
                        n1a = jnp.broadcast_to(n1blk[h][ai + d:ai + d + 1, :], row16[1:]).astype(BF16).reshape(row16)
                        e1a = jnp.broadcast_to(e1blk[h][ai + d:ai + d + 1, :], row16[1:]).astype(BF16).reshape(row16)
                        gates[d] = gates[d] + jnp.where(rk2h < n1a, e2h, jnp.zeros_like(e2h)) * e1a
                for d in range(2):
                    r0 = (grp * SUBLANE + ai + d) * N_KEYS
                    hs = h_scr[p, r0:r0 + N_KEYS, :].reshape(packed)
                    w_scr[p, r0:r0 + N_KEYS, :] = (gates[d] * (hs + hs * lax.erf(hs))).reshape(N_KEYS, LANE)
        return carry

    lax.fori_loop(0, lane_tiles, gate_body, 0)
    acc[...] += _dot(vt_ref[...], jnp.concatenate([w_scr[l2] for l2 in range(lane_tiles)], axis=1))

    @pl.when(j == nj - 1)
    def _finish():
        def norm_body(lt, c):
            lanes = lane_slice(lt)
            y = ALPHA * xt_ref[:, lanes] + acc[:, lanes]
            mu = jnp.mean(y, axis=0, keepdims=True)
            d = y - mu
            var = jnp.mean(d * d, axis=0, keepdims=True)
            yn = d * lax.rsqrt(var + NORM_EPS) * g2_ref[...] + b2_ref[...]
            acc[:, lanes] = yn
            xbf[:, lanes] = yn.astype(BF16)
            return c

        lax.fori_loop(0, lane_tiles, norm_body, 0)
        emb = _dot_nt(plet_ref[...], p_ref[...].astype(BF16))
        gt = _dot(gatet_ref[...], xbf[...])
        out_ref[...] = (acc[...] + emb * _sigmoid(gt)).T


def _peer_call(xt, p2d, lw):
    t = xt.shape[1]
    tn = min(512, t)
    te = PEER_TE
    n_exp = lw["u"].shape[0]
    nj = n_exp // te
    lane_tiles = tn // LANE
    once = dict(pipeline_mode=pl.Buffered(1))
    in_specs = [
        pl.BlockSpec((D_MODEL, tn), lambda i, j: (0, i)),
        pl.BlockSpec(lw["wqt"].shape, lambda i, j: (0, 0), **once),
        pl.BlockSpec(lw["subkeys"].shape, lambda i, j: (0, 0, 0, 0), **once),
        pl.BlockSpec((te, D_MODEL), lambda i, j: (j, 0)),
        pl.BlockSpec((None, D_MODEL, te), lambda i, j: (j, 0, 0)),
        pl.BlockSpec((tn, PLE_DIM), lambda i, j: (i, 0)),
        pl.BlockSpec(lw["plet"].shape, lambda i, j: (0, 0), **once),
        pl.BlockSpec(lw["gatet"].shape, lambda i, j: (0, 0), **once),
        pl.BlockSpec((D_MODEL, LANE), lambda i, j: (0, 0), **once),
        pl.BlockSpec((D_MODEL, LANE), lambda i, j: (0, 0), **once),
    ]
    route = (lane_tiles, PEER_HEADS, N_KEYS, LANE)
    route_packed = (lane_tiles, PEER_HEADS, N_KEYS // (2 * SUBLANE), 2 * SUBLANE, LANE)
    scratch = [pltpu.VMEM((D_MODEL, tn), BF16), pltpu.VMEM((2, N_KEYS, tn), F32),
               pltpu.VMEM(route, F32), pltpu.VMEM(route, F32),
               pltpu.VMEM(route_packed, BF16), pltpu.VMEM(route_packed, BF16),
               pltpu.VMEM((lane_tiles, te, LANE), BF16), pltpu.VMEM((lane_tiles, te, LANE), BF16),
               pltpu.VMEM((D_MODEL, tn), F32)]
    return pl.pallas_call(
        functools.partial(_peer_kernel, tn=tn, te=te, nj=nj),
        grid=(t // tn, nj),
        in_specs=in_specs,
        out_specs=pl.BlockSpec((tn, D_MODEL), lambda i, j: (i, 0)),
        out_shape=jax.ShapeDtypeStruct((t, D_MODEL), F32),
        scratch_shapes=scratch,
        compiler_params=pltpu.CompilerParams(dimension_semantics=("arbitrary", "arbitrary"),
                                             vmem_limit_bytes=VMEM_LIMIT),
        name="peer_ffn_ln_ple",
    )(xt, lw["wqt"], lw["subkeys"], lw["u"], lw["vt"], p2d, lw["plet"], lw["gatet"], lw["ln2_g"], lw["ln2_b"])


def _head_slots(w, d):
    r = w.shape[0]
    return jnp.pad(w.reshape(r, N_HEADS, d), ((0, 0), (0, 0), (0, SLOT - d))).reshape(r, GROUP_W)


def _rope_slots(w, d):
    r = w.shape[0]
    half = d // 2
    w = w.reshape(r, N_HEADS, 2, half)
    return jnp.pad(w, ((0, 0), (0, 0), (0, 0), (0, SLOT // 2 - half))).reshape(r, GROUP_W)


def _rope_tables(pos):
    half = RET_DK // 2
    inv = 1.0 / (ROPE_BASE ** (jnp.arange(0, RET_DK, 2, dtype=F32) / RET_DK))
    ang = pos[:, None] * inv[None, :]
    pad = ((0, 0), (0, SLOT // 2 - half))
    cos = jnp.pad(jnp.cos(ang), pad)
    sin = jnp.pad(jnp.sin(ang), pad)
    return jnp.concatenate([cos, cos], axis=1), jnp.concatenate([-sin, sin], axis=1)


def _retention_constants():
    log_gamma = jnp.log1p(-jnp.exp2(-5.0 - jnp.arange(N_HEADS, dtype=F32)))
    i = jnp.arange(CHUNK, dtype=F32)
    diff = i[:, None] - i[None, :]
    lg = log_gamma[:, None, None]
    dmat = jnp.where(diff >= 0, jnp.exp(jnp.where(diff >= 0, diff, 0.0) * lg), 0.0)
    qdec = jnp.broadcast_to(jnp.exp((i[None, :, None] + 1.0) * lg), (N_HEADS, CHUNK, SLOT))
    kdec = jnp.broadcast_to(jnp.exp((CHUNK - 1.0 - i[None, :, None]) * lg), (N_HEADS, CHUNK, SLOT))
    sdec = jnp.broadcast_to(jnp.exp(CHUNK * lg), (N_HEADS, 1, SLOT))
    gamma = jnp.broadcast_to(jnp.exp(lg), (N_HEADS, 1, SLOT))
    return dmat, qdec, kdec, sdec, gamma


def _layer_weights(i, lb, w_in, gla_w_gate, gla_b_gate, gla_norm, ret_norm, hgrn_norm, w_out,
                   ln1_g, ln1_b, ln2_g, ln2_b, peer_w_q, peer_subkeys, peer_u, peer_v, ple_proj, ple_gate):
    sizes = (N_HEADS * GLA_DK, N_HEADS * GLA_DK, N_HEADS * GLA_DV, N_HEADS * GLA_DV, GLA_LOWRANK,
             N_HEADS * RET_DK, N_HEADS * RET_DK, N_HEADS * RET_DV, N_HEADS * RET_DV,
             N_HEADS * HGRN_DK, N_HEADS * HGRN_DK, N_HEADS * HGRN_DV, N_HEADS * HGRN_DV)
    offs = [int(c) for c in np.cumsum(sizes)[:-1]]
    gq, gk, gv, gg, glr, rq, rk, rv, rg, hq, hf, hi, hg = jnp.split(w_in[i], offs, axis=1)
    lw = {}
    lw["w_gla"] = jnp.concatenate(
        [_head_slots(gq, GLA_DK), _head_slots(gk, GLA_DK), _head_slots(gv, GLA_DV), _head_slots(gg, GLA_DV),
         jnp.pad(glr, ((0, 0), (0, SLOT - GLA_LOWRANK)))], axis=1).astype(BF16)
    lw["w_ret"] = jnp.concatenate(
        [_rope_slots(rq, RET_DK), _rope_slots(rk, RET_DK), _head_slots(rv, RET_DV), _head_slots(rg, RET_DV)],
        axis=1).astype(BF16)
    lw["w_hgrn"] = jnp.concatenate([_head_slots(w, HGRN_DK) for w in (hq, hf, hi, hg)], axis=1).astype(BF16)
    lw["wgate"] = jnp.pad(_head_slots(gla_w_gate[i], GLA_DK), ((0, SLOT - GLA_LOWRANK), (0, 0)))
    lw["bgate"] = _head_slots(gla_b_gate[i][None, :], GLA_DK)
    lw["gla_norm"] = jnp.pad(gla_norm[i], ((0, 0), (0, SLOT - GLA_DV)))
    lw["ret_norm"] = jnp.pad(ret_norm[i], ((0, 0), (0, SLOT - RET_DV)))
    lw["hgrn_norm"] = jnp.pad(hgrn_norm[i], ((0, 0), (0, SLOT - HGRN_DV)))
    lbi = lb[i].reshape(1, N_HEADS, HGRN_DK)
    pad = ((0, 0), (0, 0), (0, SLOT - HGRN_DK))
    lw["loglb"] = jnp.pad(jnp.log(lbi), pad, constant_values=-1.0).reshape(1, GROUP_W)
    lw["l1mlb"] = jnp.pad(jnp.log1p(-lbi), pad, constant_values=-1.0).reshape(1, GROUP_W)
    wo = w_out[i]
    g_rows, r_rows = N_HEADS * GLA_DV, N_HEADS * RET_DV
    lw["w_out"] = jnp.stack([
        _head_slots(wo[:g_rows].T, GLA_DV).T, _head_slots(wo[g_rows:g_rows + r_rows].T, RET_DV).T,
        _head_slots(wo[g_rows + r_rows:].T, HGRN_DV).T]).astype(BF16)
    lw["ln1_g"], lw["ln1_b"] = ln1_g[i][None, :], ln1_b[i][None, :]
    lw["ln2_g"] = jnp.broadcast_to(ln2_g[i][:, None], (D_MODEL, LANE))
    lw["ln2_b"] = jnp.broadcast_to(ln2_b[i][:, None], (D_MODEL, LANE))
    lw["wqt"] = peer_w_q[i].T.astype(BF16)
    lw["subkeys"] = peer_subkeys[i].astype(BF16)
    lw["u"] = (peer_u[i] * SQRT_HALF).astype(BF16)
    lw["vt"] = jnp.swapaxes(peer_v[i].astype(BF16).reshape(-1, PEER_TE, D_MODEL), 1, 2)
    lw["plet"] = ple_proj[i].T.astype(BF16)
    lw["gatet"] = ple_gate[i].T.astype(BF16)
    return lw


def _unslot_state(st, dk, dv, rope=False):
    if rope:
        half = dk // 2
        st = jnp.concatenate([st[..., :half], st[..., SLOT // 2:SLOT // 2 + half]], axis=-1)
    return jnp.swapaxes(st[:, :, :dv, :dk], 2, 3)


def kernel(x_prompt, x_sample, p_prompt, p_sample, state_gla, state_ret, state_hgrn, w_in, gla_w_gate,
           gla_b_gate, gla_norm, ret_norm, hgrn_lb_logits, hgrn_norm, w_out, ln1_g, ln1_b, ln2_g, ln2_b,
           peer_w_q, peer_subkeys, peer_u, peer_v, ple_proj, ple_gate):
    bp, lp, _ = x_prompt.shape
    bs = x_sample.shape[0]
    assert x_sample.shape[1] == 1 and lp % CHUNK == 0

    lb = jnp.cumsum(jax.nn.softmax(hgrn_lb_logits.astype(F32), axis=0), axis=0)
    lb = lb - lb[0:1]
    tri_np, masks_np = _chunk_constants()
    tri, masks = jnp.asarray(tri_np, BF16), jnp.asarray(masks_np)
    dmat, qdec, kdec, sdec, gamma = _retention_constants()
    cos_p, sin_p = _rope_tables(jnp.arange(lp, dtype=F32))
    cos_s, sin_s = _rope_tables(PAST_LEN + jnp.arange(1, dtype=F32))

    xp = x_prompt.reshape(bp * lp, D_MODEL)
    xs = x_sample.reshape(bs, D_MODEL)
    gla_p, ret_p, hgrn_p, gla_s, ret_s, hgrn_s = [], [], [], [], [], []
    for i in range(DEPTH):
        lw = _layer_weights(i, lb, w_in, gla_w_gate, gla_b_gate, gla_norm, ret_norm, hgrn_norm, w_out,
                            ln1_g, ln1_b, ln2_g, ln2_b, peer_w_q, peer_subkeys, peer_u, peer_v,
                            ple_proj, ple_gate)
        lw.update(cos_s=cos_s, sin_s=sin_s, gamma=gamma)

        og, sg = _prompt_mixer_call(
            _gla_prompt_kernel, xp, bp, lp, lw["w_gla"],
            [lw["wgate"], lw["bgate"], lw["gla_norm"], tri, masks],
            [lw["w_gla"].shape[1], GROUP_W], "gla_prompt")
        orr, sr = _prompt_mixer_call(
            _ret_prompt_kernel, xp, bp, lp, lw["w_ret"],
            [(cos_p, None), (sin_p, None), lw["ret_norm"], dmat, qdec, kdec, sdec],
            [4 * GROUP_W], "ret_prompt")
        oh, sh = _prompt_mixer_call(
            _hgrn_prompt_kernel, xp, bp, lp, lw["w_hgrn"],
            [lw["loglb"], lw["l1mlb"], lw["hgrn_norm"], tri, masks],
            [4 * GROUP_W], "hgrn_prompt")
        xp = _peer_call(_out_call(og, orr, oh, xp, lw), p_prompt[i].reshape(bp * lp, PLE_DIM), lw)
        gla_p.append(_unslot_state(sg, GLA_DK, GLA_DV))
        ret_p.append(_unslot_state(sr, RET_DK, RET_DV, rope=True))
        hgrn_p.append(_unslot_state(sh, HGRN_DK, HGRN_DV))

        o_s, nsg, nsr, nsh = _sample_mixer_call(xs, lw, state_gla[i], state_ret[i], state_hgrn[i])
        xs = _peer_call(
            _out_call(o_s[:, :GROUP_W], o_s[:, GROUP_W:2 * GROUP_W], o_s[:, 2 * GROUP_W:], xs, lw),
            p_sample[i].reshape(bs, PLE_DIM), lw)
        gla_s.append(nsg.reshape(bs, N_HEADS, GLA_DK, GLA_DV))
        ret_s.append(nsr.reshape(bs, N_HEADS, RET_DK, RET_DV))
        hgrn_s.append(nsh.reshape(bs, N_HEADS, HGRN_DK, HGRN_DV))

    return (xp.reshape(bp, lp, D_MODEL), xs.reshape(bs, 1, D_MODEL),
            jnp.stack(gla_p), jnp.stack(ret_p), jnp.stack(hgrn_p),
            jnp.stack(gla_s), jnp.stack(ret_s), jnp.stack(hgrn_s))
```

```python
import functools

import numpy as np
import jax
import jax.numpy as jnp
from jax import lax
from jax.experimental import pallas as pl
from jax.experimental.pallas import tpu as pltpu

F32 = jnp.float32
BF16 = jnp.bfloat16

D_MODEL = 1024
DEPTH = 2
PAST_LEN = 16384
N_HEADS = 4
GLA_DK, GLA_DV = 48, 96
RET_DK, RET_DV = 48, 96
HGRN_DK, HGRN_DV = 64, 64
GLA_LOWRANK = 16
GLA_TAU = 16.0
ROPE_BASE = 10000.0
CHUNK = 64
GLA_UNROLL = 4
HGRN_UNROLL = 8
RET_UNROLL = 8
PEER_HEADS = 8
N_KEYS = 128
PEER_TOPK = 16
PEER_TE = 2048
PLE_DIM = 256
ALPHA = (2 * DEPTH) ** 0.25
NORM_EPS = 1e-5

LANE = 128
SUBLANE = 8
SLOT = LANE
GROUP_W = N_HEADS * SLOT
VMEM_LIMIT = 56 * 1024 * 1024
NEG_INF = float("-inf")


def _dot(a, b):
    return jnp.dot(a, b, preferred_element_type=F32)


def _dot_f32(a, b):
    return jnp.dot(a, b, preferred_element_type=F32, precision=lax.Precision.HIGHEST)


def _dot_nt(a, b):
    return lax.dot_general(a, b, (((1,), (1,)), ((), ())), preferred_element_type=F32)


def _dot_tn(a, b):
    return lax.dot_general(a, b, (((0,), (0,)), ((), ())), preferred_element_type=F32)


def _sigmoid(x):
    return jax.nn.sigmoid(x)


def _silu(x):
    return x * _sigmoid(x)


def _log_sigmoid(x):
    return jnp.minimum(x, 0.0) - jnp.log1p(jnp.exp(-jnp.abs(x)))


def _logaddexp(a, c):
    amax = jnp.maximum(a, c)
    delta = a - c
    return jnp.where(jnp.isnan(delta), a + c, amax + jnp.log1p(jnp.exp(-jnp.abs(delta))))


SQRT_HALF = np.float32(0.7071067811865476)


def _lane_mask(n):
    return (lax.broadcasted_iota(jnp.int32, (1, LANE), 1) < n).astype(F32)


def _rms_head_norm(o, g_row, dv):
    ms = jnp.sum(o * o, axis=-1, keepdims=True) * (1.0 / dv)
    return o * lax.rsqrt(ms + NORM_EPS) * g_row


def _group_head_norm(o, g_row, dv):
    mask = _lane_mask(dv)
    mu = jnp.sum(o, axis=-1, keepdims=True) * (1.0 / dv)
    d = (o - mu) * mask
    var = jnp.sum(d * d, axis=-1, keepdims=True) * (1.0 / dv)
    return d * lax.rsqrt(var + NORM_EPS) * g_row


def _chunk_constants():
    i = np.arange(CHUNK)[:, None]
    t = np.arange(CHUNK)[None, :]
    masks = [i == t]
    half = CHUNK // 2
    while half >= 1:
        blk = i // (2 * half)
        second = (i % (2 * half)) >= half
        masks.append(second & ((t % (2 * half)) < half) & (blk == t // (2 * half)))
        half //= 2
    return (t <= i).astype(np.float32), np.stack(masks).astype(np.float32)


def _cumsum_rows(tri_bf, g):
    hi = g.astype(BF16)
    r1 = g - hi.astype(F32)
    mid = r1.astype(BF16)
    lo = (r1 - mid.astype(F32)).astype(BF16)
    return _dot(tri_bf, hi) + _dot(tri_bf, mid) + _dot(tri_bf, lo)


def _level_factors(b, g):
    width = b.shape[1]
    grouped = (CHUNK // SUBLANE, SUBLANE, width)
    row = lax.broadcasted_iota(jnp.int32, b.shape, 0)
    sub = lax.broadcasted_iota(jnp.int32, grouped, 1)
    b3 = b.reshape(grouped)

    def sub_ref(r):
        return jnp.broadcast_to(b3[:, r:r + 1, :], grouped)

    out = []
    half = CHUNK // 2
    while half >= SUBLANE:
        ref = jnp.concatenate(
            [jnp.broadcast_to(b[m * 2 * half + half - 1:m * 2 * half + half], (2 * half, width))
             for m in range(CHUNK // (2 * half))], axis=0)
        out.append(jnp.exp(-jnp.abs(b - ref)))
        half //= 2
    out.append(jnp.exp(-jnp.abs(b3 - sub_ref(3))).reshape(b.shape))
    out.append(jnp.exp(-jnp.abs(b3 - jnp.where(sub < 4, sub_ref(1), sub_ref(5)))).reshape(b.shape))
    out.append(jnp.exp(jnp.where(row % 2 == 1, g, 0.0)))
    return out


def _vector_decay_chunk(q, k, v, g, st_ref, tri_bf, masks_ref):
    heads = [slice(h * SLOT, (h + 1) * SLOT) for h in range(N_HEADS)]
    b = _cumsum_rows(tri_bf, g)
    b_last = b[CHUNK - 1:CHUNK]
    factors = _level_factors(b, g)
    qs = [q.astype(BF16)] + [(q * f).astype(BF16) for f in factors]
    ks = [k.astype(BF16)] + [(k * f).astype(BF16) for f in factors]
    q_in = (q * jnp.exp(b)).astype(BF16)
    k_out = (k * jnp.exp(b_last - b)).astype(BF16)
    decay = jnp.exp(b_last)
    vb = v.astype(BF16)
    scores = []
    for sl in heads:
        sc = masks_ref[0] * _dot_nt(qs[0][:, sl], ks[0][:, sl])
        for l in range(len(factors)):
            sc = sc + masks_ref[1 + l] * _dot_nt(qs[1 + l][:, sl], ks[1 + l][:, sl])
        scores.append(sc.astype(BF16))
    outs = []
    for h, sl in enumerate(heads):
        st = st_ref[0, h]
        outs.append(_dot(scores[h], vb[:, sl]) + _dot_nt(q_in[:, sl], st.astype(BF16)))
        st_ref[0, h] = st * decay[:, sl] + _dot_tn(vb[:, sl], k_out[:, sl])
    return outs


def _gla_prompt_kernel(x_ref, w_ref, wg_ref, bg_ref, nrm_ref, tri_ref, masks_ref,
                       o_ref, st_ref, z_scr, la_scr, *, seg):
    @pl.when(pl.program_id(1) == 0)
    def _():
        st_ref[...] = jnp.zeros_like(st_ref)

    z_scr[...] = _dot(x_ref[...].astype(BF16), w_ref[...])
    pre = _dot_f32(z_scr[:, 4 * GROUP_W:4 * GROUP_W + SLOT], wg_ref[...]) + bg_ref[...]
    la_scr[...] = _log_sigmoid(pre) * (1.0 / GLA_TAU)
    tri = tri_ref[...]

    def body(c, carry):
        r0 = pl.multiple_of(c * CHUNK, CHUNK)
        rows = pl.ds(r0, CHUNK)
        q = z_scr[rows, 0:GROUP_W] * (GLA_DK ** -0.5)
        k = z_scr[rows, GROUP_W:2 * GROUP_W]
        v = z_scr[rows, 2 * GROUP_W:3 * GROUP_W]
        outs = _vector_decay_chunk(q, k, v, la_scr[rows, :], st_ref, tri, masks_ref)
        for h in range(N_HEADS):
            gate = z_scr[rows, 3 * GROUP_W + h * SLOT:3 * GROUP_W + (h + 1) * SLOT]
            on = _rms_head_norm(outs[h], nrm_ref[h:h + 1, :], GLA_DV)
            o_ref[rows, h * SLOT:(h + 1) * SLOT] = (on * _silu(gate)).astype(BF16)
        return carry

    lax.fori_loop(0, seg // CHUNK, body, 0, unroll=GLA_UNROLL)


def _hgrn_prompt_kernel(x_ref, w_ref, loglb_ref, l1mlb_ref, nrm_ref, tri_ref, masks_ref,
                        o_ref, st_ref, z_scr, *, seg):
    @pl.when(pl.program_id(1) == 0)
    def _():
        st_ref[...] = jnp.zeros_like(st_ref)

    z_scr[...] = _dot(x_ref[...].astype(BF16), w_ref[...])
    tri = tri_ref[...]
    kmask = jnp.concatenate([_lane_mask(HGRN_DK)] * N_HEADS, axis=1)

    def body(c, carry):
        r0 = pl.multiple_of(c * CHUNK, CHUNK)
        rows = pl.ds(r0, CHUNK)
        q = _silu(z_scr[rows, 0:GROUP_W])
        v = z_scr[rows, 2 * GROUP_W:3 * GROUP_W]
        log_f = _logaddexp(loglb_ref[...], l1mlb_ref[...] + _log_sigmoid(z_scr[rows, GROUP_W:2 * GROUP_W]))
        k = (1.0 - jnp.exp(log_f)) * kmask
        outs = _vector_decay_chunk(q, k, v, log_f, st_ref, tri, masks_ref)
        for h in range(N_HEADS):
            sl = slice(h * SLOT, (h + 1) * SLOT)
            gate = z_scr[rows, 3 * GROUP_W + h * SLOT:3 * GROUP_W + (h + 1) * SLOT]
            on = _rms_head_norm(outs[h], nrm_ref[h:h + 1, :], HGRN_DV)
            o_ref[rows, sl] = (on * _silu(gate)).astype(BF16)
        return carry

    lax.fori_loop(0, seg // CHUNK, body, 0, unroll=HGRN_UNROLL)


def _rotate(t, cs, sn):
    return t * cs + pltpu.roll(t, SLOT // 2, 1) * sn


def _ret_prompt_kernel(x_ref, w_ref, cos_ref, sin_ref, nrm_ref, dmat_ref, qdec_ref, kdec_ref, sdec_ref,
                       o_ref, st_ref, z_scr, *, seg):
    @pl.when(pl.program_id(1) == 0)
    def _():
        st_ref[...] = jnp.zeros_like(st_ref)

    z_scr[...] = _dot(x_ref[...].astype(BF16), w_ref[...])

    def body(c, carry):
        r0 = pl.multiple_of(c * CHUNK, CHUNK)
        rows = pl.ds(r0, CHUNK)
        cs = cos_ref[rows, :]
        sn = sin_ref[rows, :]
        heads = [slice(h * SLOT, (h + 1) * SLOT) for h in range(N_HEADS)]
        qs = [_rotate(z_scr[rows, h * SLOT:(h + 1) * SLOT], cs, sn) for h in range(N_HEADS)]
        ks = [_rotate(z_scr[rows, GROUP_W + h * SLOT:GROUP_W + (h + 1) * SLOT], cs, sn) * (RET_DK ** -0.5)
              for h in range(N_HEADS)]
        vb = z_scr[rows, 2 * GROUP_W:3 * GROUP_W].astype(BF16)
        scores = [(dmat_ref[h] * _dot_nt(qs[h].astype(BF16), ks[h].astype(BF16))).astype(BF16)
                  for h in range(N_HEADS)]
        q_in = [(qs[h] * qdec_ref[h]).astype(BF16) for h in range(N_HEADS)]
        k_out = [(ks[h] * kdec_ref[h]).astype(BF16) for h in range(N_HEADS)]
        outs = []
        for h, sl in enumerate(heads):
            st = st_ref[0, h]
            outs.append(_dot(scores[h], vb[:, sl]) + _dot_nt(q_in[h], st.astype(BF16)))
            st_ref[0, h] = st * sdec_ref[h] + _dot_tn(vb[:, sl], k_out[h])
        for h, sl in enumerate(heads):
            gate = z_scr[rows, 3 * GROUP_W + h * SLOT:3 * GROUP_W + (h + 1) * SLOT]
            on = _group_head_norm(outs[h], nrm_ref[h:h + 1, :], RET_DV)
            o_ref[rows, sl] = (on * _silu(gate)).astype(BF16)
        return carry

    lax.fori_loop(0, seg // CHUNK, body, 0, unroll=RET_UNROLL)


def _const_spec(shape):
    nd = len(shape)
    return pl.BlockSpec(shape, lambda *_: (0,) * nd)


def _prompt_mixer_call(kernel, x2d, batch, seq, w, extras, scratch_widths, name):
    seg = min(512, seq)
    nseg = seq // seg
    in_specs = [pl.BlockSpec((seg, D_MODEL), lambda b, s: (b * nseg + s, 0)), _const_spec(w.shape)]
    args = [x2d, w]
    for e in extras:
        if isinstance(e, tuple):
            arr, _ = e
            in_specs.append(pl.BlockSpec((seg, arr.shape[1]), lambda b, s: (s, 0)))
            args.append(arr)
        else:
            in_specs.append(_const_spec(e.shape))
            args.append(e)
    return pl.pallas_call(
        functools.partial(kernel, seg=seg),
        grid=(batch, nseg),
        in_specs=in_specs,
        out_specs=[pl.BlockSpec((seg, GROUP_W), lambda b, s: (b * nseg + s, 0)),
                   pl.BlockSpec((1, N_HEADS, SLOT, SLOT), lambda b, s: (b, 0, 0, 0))],
        out_shape=[jax.ShapeDtypeStruct((batch * seq, GROUP_W), BF16),
                   jax.ShapeDtypeStruct((batch, N_HEADS, SLOT, SLOT), F32)],
        scratch_shapes=[pltpu.VMEM((seg, wd), F32) for wd in scratch_widths],
        compiler_params=pltpu.CompilerParams(dimension_semantics=("arbitrary", "arbitrary"),
                                             vmem_limit_bytes=VMEM_LIMIT),
        name=name,
    )(*args)


def _sample_step(q, k, eg, v, s_ref, ns_ref, tq, tk, te, tv, to, s_t, sn_t, dk, dv, row_of):
    tq[...] = q.T
    tk[...] = k.T
    te[...] = eg.T
    tv[...] = v.T
    s_t[0:dk * dv, :] = s_ref[...].T
    vt = tv[0:dv, :]

    def body(kk, oacc):
        kr = row_of(kk)
        r = pl.multiple_of(kk * dv, SUBLANE)
        sn = s_t[pl.ds(r, dv), :] * te[pl.ds(kr, 1), :] + tk[pl.ds(kr, 1), :] * vt
        sn_t[pl.ds(r, dv), :] = sn
        return oacc + tq[pl.ds(kr, 1), :] * sn

    o_t = lax.fori_loop(0, dk, body, jnp.zeros((dv, q.shape[0]), F32))
    ns_ref[...] = sn_t[0:dk * dv, :].T
    to[...] = jnp.zeros_like(to)
    to[0:dv, :] = o_t
    return to[...].T


def _sample_mixer_kernel(x_ref, wg_ref, wr_ref, wh_ref, wgate_ref, bgate_ref, gn_ref, rn_ref, hn_ref,
                         cos_ref, sin_ref, gam_ref, loglb_ref, l1mlb_ref, sg_ref, sr_ref, sh_ref,
                         o_ref, nsg_ref, nsr_ref, nsh_ref,
                         zg, zr, zh, la, tq, tk, te, tv, to, s_t, sn_t):
    h = pl.program_id(0)

    @pl.when(h == 0)
    def _():
        xb = x_ref[...].astype(BF16)
        zg[...] = _dot(xb, wg_ref[...])
        zr[...] = _dot(xb, wr_ref[...])
        zh[...] = _dot(xb, wh_ref[...])
        pre = _dot_f32(zg[:, 4 * GROUP_W:4 * GROUP_W + SLOT], wgate_ref[...]) + bgate_ref[...]
        la[...] = _log_sigmoid(pre) * (1.0 / GLA_TAU)

    off = pl.multiple_of(h * SLOT, SLOT)
    sl = pl.ds(off, SLOT)
    tr = (tq, tk, te, tv, to, s_t, sn_t)
    batch = x_ref.shape[0]

    q = zg[:, sl] * (GLA_DK ** -0.5)
    k = zg[:, pl.ds(GROUP_W + off, SLOT)]
    v = zg[:, pl.ds(2 * GROUP_W + off, SLOT)]
    gate = zg[:, pl.ds(3 * GROUP_W + off, SLOT)]
    o = _sample_step(q, k, jnp.exp(la[:, sl]), v, sg_ref, nsg_ref, *tr, GLA_DK, GLA_DV, lambda kk: kk)
    o_ref[:, sl] = (_rms_head_norm(o, gn_ref[pl.ds(h, 1), :], GLA_DV) * _silu(gate)).astype(BF16)

    cs = cos_ref[...]
    sn = sin_ref[...]
    q = _rotate(zr[:, sl], cs, sn)
    k = _rotate(zr[:, pl.ds(GROUP_W + off, SLOT)], cs, sn) * (RET_DK ** -0.5)
    v = zr[:, pl.ds(2 * GROUP_W + off, SLOT)]
    gate = zr[:, pl.ds(3 * GROUP_W + off, SLOT)]
    eg = jnp.broadcast_to(gam_ref[h], (batch, SLOT))
    half = RET_DK // 2
    o = _sample_step(q, k, eg, v, sr_ref, nsr_ref, *tr, RET_DK, RET_DV,
                     lambda kk: kk + jnp.where(kk >= half, SLOT // 2 - half, 0))
    o_ref[:, pl.ds(GROUP_W + off, SLOT)] = (
        _group_head_norm(o, rn_ref[pl.ds(h, 1), :], RET_DV) * _silu(gate)).astype(BF16)

    q = _silu(zh[:, sl])
    hf = zh[:, pl.ds(GROUP_W + off, SLOT)]
    v = zh[:, pl.ds(2 * GROUP_W + off, SLOT)]
    gate = zh[:, pl.ds(3 * GROUP_W + off, SLOT)]
    log_f = _logaddexp(loglb_ref[:, sl], l1mlb_ref[:, sl] + _log_sigmoid(hf))
    f = jnp.exp(log_f)
    o = _sample_step(q, (1.0 - f) * _lane_mask(HGRN_DK), f, v, sh_ref, nsh_ref, *tr,
                     HGRN_DK, HGRN_DV, lambda kk: kk)
    o_ref[:, pl.ds(2 * GROUP_W + off, SLOT)] = (
        _rms_head_norm(o, hn_ref[pl.ds(h, 1), :], HGRN_DV) * _silu(gate)).astype(BF16)


def _sample_mixer_call(x2d, lw, sg, sr, sh):
    batch = x2d.shape[0]
    gsz, hsz = GLA_DK * GLA_DV, HGRN_DK * HGRN_DV
    consts = [lw["w_gla"], lw["w_ret"], lw["w_hgrn"], lw["wgate"], lw["bgate"], lw["gla_norm"], lw["ret_norm"],
              lw["hgrn_norm"], lw["cos_s"], lw["sin_s"], lw["gamma"], lw["loglb"], lw["l1mlb"]]
    in_specs = ([_const_spec(x2d.shape)] + [_const_spec(c.shape) for c in consts]
                + [pl.BlockSpec((batch, gsz), lambda h: (0, h)),
                   pl.BlockSpec((batch, gsz), lambda h: (0, h)),
                   pl.BlockSpec((batch, hsz), lambda h: (0, h))])
    out_specs = [_const_spec((batch, 3 * GROUP_W)),
                 pl.BlockSpec((batch, gsz), lambda h: (0, h)),
                 pl.BlockSpec((batch, gsz), lambda h: (0, h)),
                 pl.BlockSpec((batch, hsz), lambda h: (0, h))]
    out_shape = [jax.ShapeDtypeStruct((batch, 3 * GROUP_W), BF16),
                 jax.ShapeDtypeStruct((batch, N_HEADS * gsz), F32),
                 jax.ShapeDtypeStruct((batch, N_HEADS * gsz), F32),
                 jax.ShapeDtypeStruct((batch, N_HEADS * hsz), F32)]
    scratch = [pltpu.VMEM((batch, lw["w_gla"].shape[1]), F32), pltpu.VMEM((batch, 4 * GROUP_W), F32),
               pltpu.VMEM((batch, 4 * GROUP_W), F32), pltpu.VMEM((batch, GROUP_W), F32)]
    scratch += [pltpu.VMEM((SLOT, batch), F32) for _ in range(5)]
    scratch += [pltpu.VMEM((gsz, batch), F32), pltpu.VMEM((gsz, batch), F32)]
    return pl.pallas_call(
        _sample_mixer_kernel,
        grid=(N_HEADS,),
        in_specs=in_specs, out_specs=out_specs, out_shape=out_shape, scratch_shapes=scratch,
        compiler_params=pltpu.CompilerParams(dimension_semantics=("arbitrary",), vmem_limit_bytes=VMEM_LIMIT),
        name="sample_mixer",
    )(x2d, *consts, sg.reshape(batch, -1), sr.reshape(batch, -1), sh.reshape(batch, -1))


def _out_kernel(og_ref, or_ref, oh_ref, x_ref, wo_ref, g_ref, b_ref, xt_ref):
    mix = _dot(og_ref[...], wo_ref[0]) + _dot(or_ref[...], wo_ref[1]) + _dot(oh_ref[...], wo_ref[2])
    y = ALPHA * x_ref[...] + mix
    mu = jnp.mean(y, axis=-1, keepdims=True)
    d = y - mu
    var = jnp.mean(d * d, axis=-1, keepdims=True)
    xt_ref[...] = (d * lax.rsqrt(var + NORM_EPS) * g_ref[...] + b_ref[...]).T


def _out_call(og, orr, oh, x2d, lw):
    t = x2d.shape[0]
    tm = min(512, t)
    row = lambda i: (i, 0)
    return pl.pallas_call(
        _out_kernel,
        grid=(t // tm,),
        in_specs=[pl.BlockSpec((tm, GROUP_W), row), pl.BlockSpec((tm, GROUP_W), row),
                  pl.BlockSpec((tm, GROUP_W), row), pl.BlockSpec((tm, D_MODEL), row),
                  _const_spec(lw["w_out"].shape), _const_spec((1, D_MODEL)), _const_spec((1, D_MODEL))],
        out_specs=pl.BlockSpec((D_MODEL, tm), lambda i: (0, i)),
        out_shape=jax.ShapeDtypeStruct((D_MODEL, t), F32),
        compiler_params=pltpu.CompilerParams(dimension_semantics=("arbitrary",), vmem_limit_bytes=VMEM_LIMIT),
        name="out_proj_ln",
    )(og, orr, oh, x2d, lw["w_out"], lw["ln1_g"], lw["ln1_b"])


def _oddeven_merge(lo, hi, r):
    step = r * 2
    if step < hi - lo:
        yield from _oddeven_merge(lo, hi, step)
        yield from _oddeven_merge(lo + r, hi, step)
        yield from [(i, i + r) for i in range(lo + r, hi - r, step)]
    else:
        yield (lo, lo + r)


def _oddeven_merge_sort(lo, hi):
    if hi - lo >= 1:
        mid = lo + (hi - lo) // 2
        yield from _oddeven_merge_sort(lo, mid)
        yield from _oddeven_merge_sort(mid + 1, hi)
        yield from _oddeven_merge(lo, hi, 1)


_SORT16 = tuple(_oddeven_merge_sort(0, PEER_TOPK - 1))
_BITONIC16 = tuple((i, i + d) for d in (8, 4, 2, 1) for i in range(PEER_TOPK) if i & d == 0)


def _compare_exchange(xs, pairs):
    for i, j in pairs:
        for x in xs:
            x[i], x[j] = jnp.maximum(x[i], x[j]), jnp.minimum(x[i], x[j])
    return xs


def _top16_sorted(tiles):
    xs = _compare_exchange([[a3[i] for i in range(PEER_TOPK)] for a3 in tiles], _SORT16)
    for shift in (4, 2, 1):
        ys = [[pltpu.roll(v, shift, 0) for v in x] for x in xs]
        xs = _compare_exchange([[jnp.maximum(x[i], y[PEER_TOPK - 1 - i]) for i in range(PEER_TOPK)]
                                for x, y in zip(xs, ys)], _BITONIC16)
    return xs


def _sublane_block(rows):
    sub = lax.broadcasted_iota(jnp.int32, (SUBLANE, LANE), 0)
    blk = rows[0]
    for r in range(1, SUBLANE):
        blk = jnp.where(sub == r, rows[r], blk)
    return blk


def _route_tile(a1, a2):
    v1, v2 = _top16_sorted([a1, a2])
    sub = lax.broadcasted_iota(jnp.int32, (SUBLANE, LANE), 0)
    v2a = _sublane_block(v2[0:8])
    v2b = _sublane_block(v2[8:16])
    v1b = _sublane_block(v1[8:16])
    cands = [v1[0] + v2a, v1[0] + v2b]
    for r1 in range(1, 8):
        cands.append(jnp.where(sub < PEER_TOPK // (r1 + 1), v1[r1] + v2a, NEG_INF))
    cands.append(v1b + v2[0])
    filler = jnp.full((SUBLANE, LANE), NEG_INF, F32)
    cand_tile = jnp.stack(cands + [filler] * (PEER_TOPK - len(cands)))
    thr = _top16_sorted([cand_tile])[0][PEER_TOPK - 1]
    top = v1[0] + v2[0]
    z8 = jnp.zeros((SUBLANE, LANE), F32)
    for blk in cands:
        z8 = z8 + jnp.where(blk >= thr, jnp.exp(blk - top), 0.0)
    z = jnp.broadcast_to(jnp.sum(z8, axis=0, keepdims=True), (SUBLANE, LANE))
    n1 = jnp.zeros(a1.shape, F32)
    rank2 = jnp.zeros(a2.shape, F32)
    for r in range(PEER_TOPK):
        n1 = jnp.where(a1 + v2[r] >= thr, r + 1.0, n1)
        rank2 = jnp.where(v2[r] > a2, r + 1.0, rank2)
    e1 = jnp.exp(a1 - v1[0]) * SQRT_HALF
    e2 = jnp.exp(a2 - v2[0]) / z
    return n1, rank2, e1, e2


def _peer_kernel(xt_ref, wqt_ref, sk_ref, u_ref, vt_ref, p_ref, plet_ref, gatet_ref, g2_ref, b2_ref,
                 out_ref, xbf, s_scr, n1, e1, rk2, e2, h_scr, w_scr, acc, *, tn, te, nj):
    j = pl.program_id(1)
    lane_tiles = tn // LANE
    groups = (N_KEYS // SUBLANE, SUBLANE, LANE)

    def lane_slice(lt):
        return pl.ds(pl.multiple_of(lt * LANE, LANE), LANE)

    @pl.when(j == 0)
    def _route():
        xbf[...] = xt_ref[...].astype(BF16)

        def head_body(h, carry):
            r = pl.multiple_of(h * 2 * N_KEYS, 2 * N_KEYS)
            qh = _dot(wqt_ref[pl.ds(r, 2 * N_KEYS), :], xbf[...]).astype(BF16)
            s_scr[0] = _dot(sk_ref[h, 0], qh[0:N_KEYS])
            s_scr[1] = _dot(sk_ref[h, 1], qh[N_KEYS:2 * N_KEYS])

            def lane_body(lt, c):
                lanes = lane_slice(lt)
                n1_t, rank2_t, e1_t, e2_t = _route_tile(s_scr[0, :, lanes].reshape(groups),
                                                        s_scr[1, :, lanes].reshape(groups))
                n1[lt, h] = n1_t.reshape(N_KEYS, LANE)
                e1[lt, h] = e1_t.reshape(N_KEYS, LANE)
                rk2[lt, h] = rank2_t.reshape(N_KEYS, LANE).astype(BF16).reshape(rk2.shape[2:])
                e2[lt, h] = e2_t.reshape(N_KEYS, LANE).astype(BF16).reshape(e2.shape[2:])
                return c

            lax.fori_loop(0, lane_tiles, lane_body, 0)
            return carry

        lax.fori_loop(0, PEER_HEADS, head_body, 0)
        acc[...] = jnp.zeros_like(acc)

    na = te // N_KEYS
    assert na % SUBLANE == 0
    packed = rk2.shape[2:]
    row16 = (1, 2 * SUBLANE, LANE)

    hval = _dot(u_ref[...], xbf[...])
    for l2 in range(lane_tiles):
        h_scr[l2] = hval[:, l2 * LANE:(l2 + 1) * LANE].astype(BF16)

    def gate_body(p, carry):
        for grp in range(na // SUBLANE):
            a0 = pl.multiple_of(j * na + grp * SUBLANE, SUBLANE)
            n1blk = [n1[p, h, pl.ds(a0, SUBLANE), :] for h in range(PEER_HEADS)]
            e1blk = [e1[p, h, pl.ds(a0, SUBLANE), :] for h in range(PEER_HEADS)]
            for ai in range(0, SUBLANE, 2):
                gates = [jnp.zeros(packed, BF16), jnp.zeros(packed, BF16)]
                for h in range(PEER_HEADS):
                    rk2h = rk2[p, h]
                    e2h = e2[p, h]
                    for d in range(2):
                        n1a = jnp.broadcast_to(n1blk[h][ai + d:ai + d + 1, :], row16[1:]).astype(BF16).reshape(row16)
                        e1a = jnp.broadcast_to(e1blk[h][ai + d:ai + d + 1, :], row16[1:]).astype(BF16).reshape(row16)
                        gates[d] = gates[d] + jnp.where(rk2h < n1a, e2h, jnp.zeros_like(e2h)) * e1a
                for d in range(2):
                    r0 = (grp * SUBLANE + ai + d) * N_KEYS
                    hs = h_scr[p, r0:r0 + N_KEYS, :].reshape(packed)
                    w_scr[p, r0:r0 + N_KEYS, :] = (gates[d] * (hs + hs * lax.erf(hs))).reshape(N_KEYS, LANE)
        return carry

    lax.fori_loop(0, lane_tiles, gate_body, 0)
    acc[...] += _dot(vt_ref[...], jnp.concatenate([w_scr[l2] for l2 in range(lane_tiles)], axis=1))

    @pl.when(j == nj - 1)
    def _finish():
        def norm_body(lt, c):
            lanes = lane_slice(lt)
            y = ALPHA * xt_ref[:, lanes] + acc[:, lanes]
            mu = jnp.mean(y, axis=0, keepdims=True)
            d = y - mu
            var = jnp.mean(d * d, axis=0, keepdims=True)
            yn = d * lax.rsqrt(var + NORM_EPS) * g2_ref[...] + b2_ref[...]
            acc[:, lanes] = yn
            xbf[:, lanes] = yn.astype(BF16)
            return c

        lax.fori_loop(0, lane_tiles, norm_body, 0)
        emb = _dot_nt(plet_ref[...], p_ref[...].astype(BF16))
        gt = _dot(gatet_ref[...], xbf[...])
        out_ref[...] = (acc[...] + emb * _sigmoid(gt)).T


def _peer_call(xt, p2d, lw):
    t = xt.shape[1]
    tn = min(512, t)
    te = PEER_TE
    n_exp = lw["u"].shape[0]
    nj = n_exp // te
    lane_tiles = tn // LANE
    once = dict(pipeline_mode=pl.Buffered(1))
    in_specs = [
        pl.BlockSpec((D_MODEL, tn), lambda i, j: (0, i)),
        pl.BlockSpec(lw["wqt"].shape, lambda i, j: (0, 0), **once),
        pl.BlockSpec(lw["subkeys"].shape, lambda i, j: (0, 0, 0, 0), **once),
        pl.BlockSpec((te, D_MODEL), lambda i, j: (j, 0)),
        pl.BlockSpec((None, D_MODEL, te), lambda i, j: (j, 0, 0)),
        pl.BlockSpec((tn, PLE_DIM), lambda i, j: (i, 0)),
        pl.BlockSpec(lw["plet"].shape, lambda i, j: (0, 0), **once),
        pl.BlockSpec(lw["gatet"].shape, lambda i, j: (0, 0), **once),
        pl.BlockSpec((D_MODEL, LANE), lambda i, j: (0, 0), **once),
        pl.BlockSpec((D_MODEL, LANE), lambda i, j: (0, 0), **once),
    ]
    route = (lane_tiles, PEER_HEADS, N_KEYS, LANE)
    route_packed = (lane_tiles, PEER_HEADS, N_KEYS // (2 * SUBLANE), 2 * SUBLANE, LANE)
    scratch = [pltpu.VMEM((D_MODEL, tn), BF16), pltpu.VMEM((2, N_KEYS, tn), F32),
               pltpu.VMEM(route, F32), pltpu.VMEM(route, F32),
               pltpu.VMEM(route_packed, BF16), pltpu.VMEM(route_packed, BF16),
               pltpu.VMEM((lane_tiles, te, LANE), BF16), pltpu.VMEM((lane_tiles, te, LANE), BF16),
               pltpu.VMEM((D_MODEL, tn), F32)]
    return pl.pallas_call(
        functools.partial(_peer_kernel, tn=tn, te=te, nj=nj),
        grid=(t // tn, nj),
        in_specs=in_specs,
        out_specs=pl.BlockSpec((tn, D_MODEL), lambda i, j: (i, 0)),
        out_shape=jax.ShapeDtypeStruct((t, D_MODEL), F32),
        scratch_shapes=scratch,
        compiler_params=pltpu.CompilerParams(dimension_semantics=("arbitrary", "arbitrary"),
                                             vmem_limit_bytes=VMEM_LIMIT),
        name="peer_ffn_ln_ple",
    )(xt, lw["wqt"], lw["subkeys"], lw["u"], lw["vt"], p2d, lw["plet"], lw["gatet"], lw["ln2_g"], lw["ln2_b"])


def _head_slots(w, d):
    r = w.shape[0]
    return jnp.pad(w.reshape(r, N_HEADS, d), ((0, 0), (0, 0), (0, SLOT - d))).reshape(r, GROUP_W)


def _rope_slots(w, d):
    r = w.shape[0]
    half = d // 2
    w = w.reshape(r, N_HEADS, 2, half)
    return jnp.pad(w, ((0, 0), (0, 0), (0, 0), (0, SLOT // 2 - half))).reshape(r, GROUP_W)


def _rope_tables(pos):
    half = RET_DK // 2
    inv = 1.0 / (ROPE_BASE ** (jnp.arange(0, RET_DK, 2, dtype=F32) / RET_DK))
    ang = pos[:, None] * inv[None, :]
    pad = ((0, 0), (0, SLOT // 2 - half))
    cos = jnp.pad(jnp.cos(ang), pad)
    sin = jnp.pad(jnp.sin(ang), pad)
    return jnp.concatenate([cos, cos], axis=1), jnp.concatenate([-sin, sin], axis=1)


def _retention_constants():
    log_gamma = jnp.log1p(-jnp.exp2(-5.0 - jnp.arange(N_HEADS, dtype=F32)))
    i = jnp.arange(CHUNK, dtype=F32)
    diff = i[:, None] - i[None, :]
    lg = log_gamma[:, None, None]
    dmat = jnp.where(diff >= 0, jnp.exp(jnp.where(diff >= 0, diff, 0.0) * lg), 0.0)
    qdec = jnp.broadcast_to(jnp.exp((i[None, :, None] + 1.0) * lg), (N_HEADS, CHUNK, SLOT))
    kdec = jnp.broadcast_to(jnp.exp((CHUNK - 1.0 - i[None, :, None]) * lg), (N_HEADS, CHUNK, SLOT))
    sdec = jnp.broadcast_to(jnp.exp(CHUNK * lg), (N_HEADS, 1, SLOT))
    gamma = jnp.broadcast_to(jnp.exp(lg), (N_HEADS, 1, SLOT))
    return dmat, qdec, kdec, sdec, gamma


def _layer_weights(i, lb, w_in, gla_w_gate, gla_b_gate, gla_norm, ret_norm, hgrn_norm, w_out,
                   ln1_g, ln1_b, ln2_g, ln2_b, peer_w_q, peer_subkeys, peer_u, peer_v, ple_proj, ple_gate):
    sizes = (N_HEADS * GLA_DK, N_HEADS * GLA_DK, N_HEADS * GLA_DV, N_HEADS * GLA_DV, GLA_LOWRANK,
             N_HEADS * RET_DK, N_HEADS * RET_DK, N_HEADS * RET_DV, N_HEADS * RET_DV,
             N_HEADS * HGRN_DK, N_HEADS * HGRN_DK, N_HEADS * HGRN_DV, N_HEADS * HGRN_DV)
    offs = [int(c) for c in np.cumsum(sizes)[:-1]]
    gq, gk, gv, gg, glr, rq, rk, rv, rg, hq, hf, hi, hg = jnp.split(w_in[i], offs, axis=1)
    lw = {}
    lw["w_gla"] = jnp.concatenate(
        [_head_slots(gq, GLA_DK), _head_slots(gk, GLA_DK), _head_slots(gv, GLA_DV), _head_slots(gg, GLA_DV),
         jnp.pad(glr, ((0, 0), (0, SLOT - GLA_LOWRANK)))], axis=1).astype(BF16)
    lw["w_ret"] = jnp.concatenate(
        [_rope_slots(rq, RET_DK), _rope_slots(rk, RET_DK), _head_slots(rv, RET_DV), _head_slots(rg, RET_DV)],
        axis=1).astype(BF16)
    lw["w_hgrn"] = jnp.concatenate([_head_slots(w, HGRN_DK) for w in (hq, hf, hi, hg)], axis=1).astype(BF16)
    lw["wgate"] = jnp.pad(_head_slots(gla_w_gate[i], GLA_DK), ((0, SLOT - GLA_LOWRANK), (0, 0)))
    lw["bgate"] = _head_slots(gla_b_gate[i][None, :], GLA_DK)
    lw["gla_norm"] = jnp.pad(gla_norm[i], ((0, 0), (0, SLOT - GLA_DV)))
    lw["ret_norm"] = jnp.pad(ret_norm[i], ((0, 0), (0, SLOT - RET_DV)))
    lw["hgrn_norm"] = jnp.pad(hgrn_norm[i], ((0, 0), (0, SLOT - HGRN_DV)))
    lbi = lb[i].reshape(1, N_HEADS, HGRN_DK)
    pad = ((0, 0), (0, 0), (0, SLOT - HGRN_DK))
    lw["loglb"] = jnp.pad(jnp.log(lbi), pad, constant_values=-1.0).reshape(1, GROUP_W)
    lw["l1mlb"] = jnp.pad(jnp.log1p(-lbi), pad, constant_values=-1.0).reshape(1, GROUP_W)
    wo = w_out[i]
    g_rows, r_rows = N_HEADS * GLA_DV, N_HEADS * RET_DV
    lw["w_out"] = jnp.stack([
        _head_slots(wo[:g_rows].T, GLA_DV).T, _head_slots(wo[g_rows:g_rows + r_rows].T, RET_DV).T,
        _head_slots(wo[g_rows + r_rows:].T, HGRN_DV).T]).astype(BF16)
    lw["ln1_g"], lw["ln1_b"] = ln1_g[i][None, :], ln1_b[i][None, :]
    lw["ln2_g"] = jnp.broadcast_to(ln2_g[i][:, None], (D_MODEL, LANE))
    lw["ln2_b"] = jnp.broadcast_to(ln2_b[i][:, None], (D_MODEL, LANE))
    lw["wqt"] = peer_w_q[i].T.astype(BF16)
    lw["subkeys"] = peer_subkeys[i].astype(BF16)
    lw["u"] = (peer_u[i] * SQRT_HALF).astype(BF16)
    lw["vt"] = jnp.swapaxes(peer_v[i].astype(BF16).reshape(-1, PEER_TE, D_MODEL), 1, 2)
    lw["plet"] = ple_proj[i].T.astype(BF16)
    lw["gatet"] = ple_gate[i].T.astype(BF16)
    return lw


def _unslot_state(st, dk, dv, rope=False):
    if rope:
        half = dk // 2
        st = jnp.concatenate([st[..., :half], st[..., SLOT // 2:SLOT // 2 + half]], axis=-1)
    return jnp.swapaxes(st[:, :, :dv, :dk], 2, 3)


def kernel(x_prompt, x_sample, p_prompt, p_sample, state_gla, state_ret, state_hgrn, w_in, gla_w_gate,
           gla_b_gate, gla_norm, ret_norm, hgrn_lb_logits, hgrn_norm, w_out, ln1_g, ln1_b, ln2_g, ln2_b,
           peer_w_q, peer_subkeys, peer_u, peer_v, ple_proj, ple_gate):
    bp, lp, _ = x_prompt.shape
    bs = x_sample.shape[0]
    assert x_sample.shape[1] == 1 and lp % CHUNK == 0

    lb = jnp.cumsum(jax.nn.softmax(hgrn_lb_logits.astype(F32), axis=0), axis=0)
    lb = lb - lb[0:1]
    tri_np, masks_np = _chunk_constants()
    tri, masks = jnp.asarray(tri_np, BF16), jnp.asarray(masks_np)
    dmat, qdec, kdec, sdec, gamma = _retention_constants()
    cos_p, sin_p = _rope_tables(jnp.arange(lp, dtype=F32))
    cos_s, sin_s = _rope_tables(PAST_LEN + jnp.arange(1, dtype=F32))

    xp = x_prompt.reshape(bp * lp, D_MODEL)
    xs = x_sample.reshape(bs, D_MODEL)
    gla_p, ret_p, hgrn_p, gla_s, ret_s, hgrn_s = [], [], [], [], [], []
    for i in range(DEPTH):
        lw = _layer_weights(i, lb, w_in, gla_w_gate, gla_b_gate, gla_norm, ret_norm, hgrn_norm, w_out,
                            ln1_g, ln1_b, ln2_g, ln2_b, peer_w_q, peer_subkeys, peer_u, peer_v,
                            ple_proj, ple_gate)
        lw.update(cos_s=cos_s, sin_s=sin_s, gamma=gamma)

        og, sg = _prompt_mixer_call(
            _gla_prompt_kernel, xp, bp, lp, lw["w_gla"],
            [lw["wgate"], lw["bgate"], lw["gla_norm"], tri, masks],
            [lw["w_gla"].shape[1], GROUP_W], "gla_prompt")
        orr, sr = _prompt_mixer_call(
            _ret_prompt_kernel, xp, bp, lp, lw["w_ret"],
            [(cos_p, None), (sin_p, None), lw["ret_norm"], dmat, qdec, kdec, sdec],
            [4 * GROUP_W], "ret_prompt")
        oh, sh = _prompt_mixer_call(
            _hgrn_prompt_kernel, xp, bp, lp, lw["w_hgrn"],
            [lw["loglb"], lw["l1mlb"], lw["hgrn_norm"], tri, masks],
            [4 * GROUP_W], "hgrn_prompt")
        xp = _peer_call(_out_call(og, orr, oh, xp, lw), p_prompt[i].reshape(bp * lp, PLE_DIM), lw)
        gla_p.append(_unslot_state(sg, GLA_DK, GLA_DV))
        ret_p.append(_unslot_state(sr, RET_DK, RET_DV, rope=True))
        hgrn_p.append(_unslot_state(sh, HGRN_DK, HGRN_DV))

        o_s, nsg, nsr, nsh = _sample_mixer_call(xs, lw, state_gla[i], state_ret[i], state_hgrn[i])
        xs = _peer_call(
            _out_call(o_s[:, :GROUP_W], o_s[:, GROUP_W:2 * GROUP_W], o_s[:, 2 * GROUP_W:], xs, lw),
            p_sample[i].reshape(bs, PLE_DIM), lw)
        gla_s.append(nsg.reshape(bs, N_HEADS, GLA_DK, GLA_DV))
        ret_s.append(nsr.reshape(bs, N_HEADS, RET_DK, RET_DV))
        hgrn_s.append(nsh.reshape(bs, N_HEADS, HGRN_DK, HGRN_DV))

    return (xp.reshape(bp, lp, D_MODEL), xs.reshape(bs, 1, D_MODEL),
            jnp.stack(gla_p), jnp.stack(ret_p), jnp.stack(hgrn_p),
            jnp.stack(gla_s), jnp.stack(ret_s), jnp.stack(hgrn_s))
```

```python
import functools

import numpy as np
import jax
import jax.numpy as jnp
from jax import lax
from jax.experimental import pallas as pl
from jax.experimental.pallas import tpu as pltpu

F32 = jnp.float32
BF16 = jnp.bfloat16

D_MODEL = 1024
DEPTH = 2
PAST_LEN = 16384
N_HEADS = 4
GLA_DK, GLA_DV = 48, 96
RET_DK, RET_DV = 48, 96
HGRN_DK, HGRN_DV = 64, 64
GLA_LOWRANK = 16
GLA_TAU = 16.0
ROPE_BASE = 10000.0
CHUNK = 64
GLA_UNROLL = 4
HGRN_UNROLL = 8
RET_UNROLL = 8
PEER_HEADS = 8
N_KEYS = 128
PEER_TOPK = 16
PEER_TE = 2048
PLE_DIM = 256
ALPHA = (2 * DEPTH) ** 0.25
NORM_EPS = 1e-5

LANE = 128
SUBLANE = 8
SLOT = LANE
GROUP_W = N_HEADS * SLOT
VMEM_LIMIT = 56 * 1024 * 1024
NEG_INF = float("-inf")


def _dot(a, b):
    return jnp.dot(a, b, preferred_element_type=F32)


def _dot_f32(a, b):
    return jnp.dot(a, b, preferred_element_type=F32, precision=lax.Precision.HIGHEST)


def _dot_nt(a, b):
    return lax.dot_general(a, b, (((1,), (1,)), ((), ())), preferred_element_type=F32)


def _dot_tn(a, b):
    return lax.dot_general(a, b, (((0,), (0,)), ((), ())), preferred_element_type=F32)


def _sigmoid(x):
    return jax.nn.sigmoid(x)


def _silu(x):
    return x * _sigmoid(x)


def _log_sigmoid(x):
    return jnp.minimum(x, 0.0) - jnp.log1p(jnp.exp(-jnp.abs(x)))


def _logaddexp(a, c):
    amax = jnp.maximum(a, c)
    delta = a - c
    return jnp.where(jnp.isnan(delta), a + c, amax + jnp.log1p(jnp.exp(-jnp.abs(delta))))


SQRT_HALF = np.float32(0.7071067811865476)


def _lane_mask(n):
    return (lax.broadcasted_iota(jnp.int32, (1, LANE), 1) < n).astype(F32)


def _rms_head_norm(o, g_row, dv):
    ms = jnp.sum(o * o, axis=-1, keepdims=True) * (1.0 / dv)
    return o * lax.rsqrt(ms + NORM_EPS) * g_row


def _group_head_norm(o, g_row, dv):
    mask = _lane_mask(dv)
    mu = jnp.sum(o, axis=-1, keepdims=True) * (1.0 / dv)
    d = (o - mu) * mask
    var = jnp.sum(d * d, axis=-1, keepdims=True) * (1.0 / dv)
    return d * lax.rsqrt(var + NORM_EPS) * g_row


def _chunk_constants():
    i = np.arange(CHUNK)[:, None]
    t = np.arange(CHUNK)[None, :]
    masks = [i == t]
    half = CHUNK // 2
    while half >= 1:
        blk = i // (2 * half)
        second = (i % (2 * half)) >= half
        masks.append(second & ((t % (2 * half)) < half) & (blk == t // (2 * half)))
        half //= 2
    return (t <= i).astype(np.float32), np.stack(masks).astype(np.float32)


def _cumsum_rows(tri_bf, g):
    hi = g.astype(BF16)
    r1 = g - hi.astype(F32)
    mid = r1.astype(BF16)
    lo = (r1 - mid.astype(F32)).astype(BF16)
    return _dot(tri_bf, hi) + _dot(tri_bf, mid) + _dot(tri_bf, lo)


def _level_factors(b, g):
    width = b.shape[1]
    grouped = (CHUNK // SUBLANE, SUBLANE, width)
    row = lax.broadcasted_iota(jnp.int32, b.shape, 0)
    sub = lax.broadcasted_iota(jnp.int32, grouped, 1)
    b3 = b.reshape(grouped)

    def sub_ref(r):
        return jnp.broadcast_to(b3[:, r:r + 1, :], grouped)

    out = []
    half = CHUNK // 2
    while half >= SUBLANE:
        ref = jnp.concatenate(
            [jnp.broadcast_to(b[m * 2 * half + half - 1:m * 2 * half + half], (2 * half, width))
             for m in range(CHUNK // (2 * half))], axis=0)
        out.append(jnp.exp(-jnp.abs(b - ref)))
        half //= 2
    out.append(jnp.exp(-jnp.abs(b3 - sub_ref(3))).reshape(b.shape))
    out.append(jnp.exp(-jnp.abs(b3 - jnp.where(sub < 4, sub_ref(1), sub_ref(5)))).reshape(b.shape))
    out.append(jnp.exp(jnp.where(row % 2 == 1, g, 0.0)))
    return out


def _vector_decay_chunk(q, k, v, g, st_ref, tri_bf, masks_ref):
    heads = [slice(h * SLOT, (h + 1) * SLOT) for h in range(N_HEADS)]
    b = _cumsum_rows(tri_bf, g)
    b_last = b[CHUNK - 1:CHUNK]
    factors = _level_factors(b, g)
    qs = [q.astype(BF16)] + [(q * f).astype(BF16) for f in factors]
    ks = [k.astype(BF16)] + [(k * f).astype(BF16) for f in factors]
    q_in = (q * jnp.exp(b)).astype(BF16)
    k_out = (k * jnp.exp(b_last - b)).astype(BF16)
    decay = jnp.exp(b_last)
    vb = v.astype(BF16)
    scores = []
    for sl in heads:
        sc = masks_ref[0] * _dot_nt(qs[0][:, sl], ks[0][:, sl])
        for l in range(len(factors)):
            sc = sc + masks_ref[1 + l] * _dot_nt(qs[1 + l][:, sl], ks[1 + l][:, sl])
        scores.append(sc.astype(BF16))
    outs = []
    for h, sl in enumerate(heads):
        st = st_ref[0, h]
        outs.append(_dot(scores[h], vb[:, sl]) + _dot_nt(q_in[:, sl], st.astype(BF16)))
        st_ref[0, h] = st * decay[:, sl] + _dot_tn(vb[:, sl], k_out[:, sl])
    return outs


def _gla_prompt_kernel(x_ref, w_ref, wg_ref, bg_ref, nrm_ref, tri_ref, masks_ref,
                       o_ref, st_ref, z_scr, la_scr, *, seg):
    @pl.when(pl.program_id(1) == 0)
    def _():
        st_ref[...] = jnp.zeros_like(st_ref)

    z_scr[...] = _dot(x_ref[...].astype(BF16), w_ref[...])
    pre = _dot_f32(z_scr[:, 4 * GROUP_W:4 * GROUP_W + SLOT], wg_ref[...]) + bg_ref[...]
    la_scr[...] = _log_sigmoid(pre) * (1.0 / GLA_TAU)
    tri = tri_ref[...]

    def body(c, carry):
        r0 = pl.multiple_of(c * CHUNK, CHUNK)
        rows = pl.ds(r0, CHUNK)
        q = z_scr[rows, 0:GROUP_W] * (GLA_DK ** -0.5)
        k = z_scr[rows, GROUP_W:2 * GROUP_W]
        v = z_scr[rows, 2 * GROUP_W:3 * GROUP_W]
        outs = _vector_decay_chunk(q, k, v, la_scr[rows, :], st_ref, tri, masks_ref)
        for h in range(N_HEADS):
            gate = z_scr[rows, 3 * GROUP_W + h * SLOT:3 * GROUP_W + (h + 1) * SLOT]
            on = _rms_head_norm(outs[h], nrm_ref[h:h + 1, :], GLA_DV)
            o_ref[rows, h * SLOT:(h + 1) * SLOT] = (on * _silu(gate)).astype(BF16)
        return carry

    lax.fori_loop(0, seg // CHUNK, body, 0, unroll=GLA_UNROLL)


def _hgrn_prompt_kernel(x_ref, w_ref, loglb_ref, l1mlb_ref, nrm_ref, tri_ref, masks_ref,
                        o_ref, st_ref, z_scr, *, seg):
    @pl.when(pl.program_id(1) == 0)
    def _():
        st_ref[...] = jnp.zeros_like(st_ref)

    z_scr[...] = _dot(x_ref[...].astype(BF16), w_ref[...])
    tri = tri_ref[...]
    kmask = jnp.concatenate([_lane_mask(HGRN_DK)] * N_HEADS, axis=1)

    def body(c, carry):
        r0 = pl.multiple_of(c * CHUNK, CHUNK)
        rows = pl.ds(r0, CHUNK)
        q = _silu(z_scr[rows, 0:GROUP_W])
        v = z_scr[rows, 2 * GROUP_W:3 * GROUP_W]
        log_f = _logaddexp(loglb_ref[...], l1mlb_ref[...] + _log_sigmoid(z_scr[rows, GROUP_W:2 * GROUP_W]))
        k = (1.0 - jnp.exp(log_f)) * kmask
        outs = _vector_decay_chunk(q, k, v, log_f, st_ref, tri, masks_ref)
        for h in range(N_HEADS):
            sl = slice(h * SLOT, (h + 1) * SLOT)
            gate = z_scr[rows, 3 * GROUP_W + h * SLOT:3 * GROUP_W + (h + 1) * SLOT]
            on = _rms_head_norm(outs[h], nrm_ref[h:h + 1, :], HGRN_DV)
            o_ref[rows, sl] = (on * _silu(gate)).astype(BF16)
        return carry

    lax.fori_loop(0, seg // CHUNK, body, 0, unroll=HGRN_UNROLL)


def _rotate(t, cs, sn):
    return t * cs + pltpu.roll(t, SLOT // 2, 1) * sn


def _ret_prompt_kernel(x_ref, w_ref, cos_ref, sin_ref, nrm_ref, dmat_ref, qdec_ref, kdec_ref, sdec_ref,
                       o_ref, st_ref, z_scr, *, seg):
    @pl.when(pl.program_id(1) == 0)
    def _():
        st_ref[...] = jnp.zeros_like(st_ref)

    z_scr[...] = _dot(x_ref[...].astype(BF16), w_ref[...])

    def body(c, carry):
        r0 = pl.multiple_of(c * CHUNK, CHUNK)
        rows = pl.ds(r0, CHUNK)
        cs = cos_ref[rows, :]
        sn = sin_ref[rows, :]
        heads = [slice(h * SLOT, (h + 1) * SLOT) for h in range(N_HEADS)]
        qs = [_rotate(z_scr[rows, h * SLOT:(h + 1) * SLOT], cs, sn) for h in range(N_HEADS)]
        ks = [_rotate(z_scr[rows, GROUP_W + h * SLOT:GROUP_W + (h + 1) * SLOT], cs, sn) * (RET_DK ** -0.5)
              for h in range(N_HEADS)]
        vb = z_scr[rows, 2 * GROUP_W:3 * GROUP_W].astype(BF16)
        scores = [(dmat_ref[h] * _dot_nt(qs[h].astype(BF16), ks[h].astype(BF16))).astype(BF16)
                  for h in range(N_HEADS)]
        q_in = [(qs[h] * qdec_ref[h]).astype(BF16) for h in range(N_HEADS)]
        k_out = [(ks[h] * kdec_ref[h]).astype(BF16) for h in range(N_HEADS)]
        outs = []
        for h, sl in enumerate(heads):
            st = st_ref[0, h]
            outs.append(_dot(scores[h], vb[:, sl]) + _dot_nt(q_in[h], st.astype(BF16)))
            st_ref[0, h] = st * sdec_ref[h] + _dot_tn(vb[:, sl], k_out[h])
        for h, sl in enumerate(heads):
            gate = z_scr[rows, 3 * GROUP_W + h * SLOT:3 * GROUP_W + (h + 1) * SLOT]
            on = _group_head_norm(outs[h], nrm_ref[h:h + 1, :], RET_DV)
            o_ref[rows, sl] = (on * _silu(gate)).astype(BF16)
        return carry

    lax.fori_loop(0, seg // CHUNK, body, 0, unroll=RET_UNROLL)


def _const_spec(shape):
    nd = len(shape)
    return pl.BlockSpec(shape, lambda *_: (0,) * nd)


def _prompt_mixer_call(kernel, x2d, batch, seq, w, extras, scratch_widths, name):
    seg = min(512, seq)
    nseg = seq // seg
    in_specs = [pl.BlockSpec((seg, D_MODEL), lambda b, s: (b * nseg + s, 0)), _const_spec(w.shape)]
    args = [x2d, w]
    for e in extras:
        if isinstance(e, tuple):
            arr, _ = e
            in_specs.append(pl.BlockSpec((seg, arr.shape[1]), lambda b, s: (s, 0)))
            args.append(arr)
        else:
            in_specs.append(_const_spec(e.shape))
            args.append(e)
    return pl.pallas_call(
        functools.partial(kernel, seg=seg),
        grid=(batch, nseg),
        in_specs=in_specs,
        out_specs=[pl.BlockSpec((seg, GROUP_W), lambda b, s: (b * nseg + s, 0)),
                   pl.BlockSpec((1, N_HEADS, SLOT, SLOT), lambda b, s: (b, 0, 0, 0))],
        out_shape=[jax.ShapeDtypeStruct((batch * seq, GROUP_W), BF16),
                   jax.ShapeDtypeStruct((batch, N_HEADS, SLOT, SLOT), F32)],
        scratch_shapes=[pltpu.VMEM((seg, wd), F32) for wd in scratch_widths],
        compiler_params=pltpu.CompilerParams(dimension_semantics=("arbitrary", "arbitrary"),
                                             vmem_limit_bytes=VMEM_LIMIT),
        name=name,
    )(*args)


def _sample_step(q, k, eg, v, s_ref, ns_ref, tq, tk, te, tv, to, s_t, sn_t, dk, dv, row_of):
    tq[...] = q.T
    tk[...] = k.T
    te[...] = eg.T
    tv[...] = v.T
    s_t[0:dk * dv, :] = s_ref[...].T
    vt = tv[0:dv, :]

    def body(kk, oacc):
        kr = row_of(kk)
        r = pl.multiple_of(kk * dv, SUBLANE)
        sn = s_t[pl.ds(r, dv), :] * te[pl.ds(kr, 1), :] + tk[pl.ds(kr, 1), :] * vt
        sn_t[pl.ds(r, dv), :] = sn
        return oacc + tq[pl.ds(kr, 1), :] * sn

    o_t = lax.fori_loop(0, dk, body, jnp.zeros((dv, q.shape[0]), F32))
    ns_ref[...] = sn_t[0:dk * dv, :].T
    to[...] = jnp.zeros_like(to)
    to[0:dv, :] = o_t
    return to[...].T


def _sample_mixer_kernel(x_ref, wg_ref, wr_ref, wh_ref, wgate_ref, bgate_ref, gn_ref, rn_ref, hn_ref,
                         cos_ref, sin_ref, gam_ref, loglb_ref, l1mlb_ref, sg_ref, sr_ref, sh_ref,
                         o_ref, nsg_ref, nsr_ref, nsh_ref,
                         zg, zr, zh, la, tq, tk, te, tv, to, s_t, sn_t):
    h = pl.program_id(0)

    @pl.when(h == 0)
    def _():
        xb = x_ref[...].astype(BF16)
        zg[...] = _dot(xb, wg_ref[...])
        zr[...] = _dot(xb, wr_ref[...])
        zh[...] = _dot(xb, wh_ref[...])
        pre = _dot_f32(zg[:, 4 * GROUP_W:4 * GROUP_W + SLOT], wgate_ref[...]) + bgate_ref[...]
        la[...] = _log_sigmoid(pre) * (1.0 / GLA_TAU)

    off = pl.multiple_of(h * SLOT, SLOT)
    sl = pl.ds(off, SLOT)
    tr = (tq, tk, te, tv, to, s_t, sn_t)
    batch = x_ref.shape[0]

    q = zg[:, sl] * (GLA_DK ** -0.5)
    k = zg[:, pl.ds(GROUP_W + off, SLOT)]
    v = zg[:, pl.ds(2 * GROUP_W + off, SLOT)]
    gate = zg[:, pl.ds(3 * GROUP_W + off, SLOT)]
    o = _sample_step(q, k, jnp.exp(la[:, sl]), v, sg_ref, nsg_ref, *tr, GLA_DK, GLA_DV, lambda kk: kk)
    o_ref[:, sl] = (_rms_head_norm(o, gn_ref[pl.ds(h, 1), :], GLA_DV) * _silu(gate)).astype(BF16)

    cs = cos_ref[...]
    sn = sin_ref[...]
    q = _rotate(zr[:, sl], cs, sn)
    k = _rotate(zr[:, pl.ds(GROUP_W + off, SLOT)], cs, sn) * (RET_DK ** -0.5)
    v = zr[:, pl.ds(2 * GROUP_W + off, SLOT)]
    gate = zr[:, pl.ds(3 * GROUP_W + off, SLOT)]
    eg = jnp.broadcast_to(gam_ref[h], (batch, SLOT))
    half = RET_DK // 2
    o = _sample_step(q, k, eg, v, sr_ref, nsr_ref, *tr, RET_DK, RET_DV,
                     lambda kk: kk + jnp.where(kk >= half, SLOT // 2 - half, 0))
    o_ref[:, pl.ds(GROUP_W + off, SLOT)] = (
        _group_head_norm(o, rn_ref[pl.ds(h, 1), :], RET_DV) * _silu(gate)).astype(BF16)

    q = _silu(zh[:, sl])
    hf = zh[:, pl.ds(GROUP_W + off, SLOT)]
    v = zh[:, pl.ds(2 * GROUP_W + off, SLOT)]
    gate = zh[:, pl.ds(3 * GROUP_W + off, SLOT)]
    log_f = _logaddexp(loglb_ref[:, sl], l1mlb_ref[:, sl] + _log_sigmoid(hf))
    f = jnp.exp(log_f)
    o = _sample_step(q, (1.0 - f) * _lane_mask(HGRN_DK), f, v, sh_ref, nsh_ref, *tr,
                     HGRN_DK, HGRN_DV, lambda kk: kk)
    o_ref[:, pl.ds(2 * GROUP_W + off, SLOT)] = (
        _rms_head_norm(o, hn_ref[pl.ds(h, 1), :], HGRN_DV) * _silu(gate)).astype(BF16)


def _sample_mixer_call(x2d, lw, sg, sr, sh):
    batch = x2d.shape[0]
    gsz, hsz = GLA_DK * GLA_DV, HGRN_DK * HGRN_DV
    consts = [lw["w_gla"], lw["w_ret"], lw["w_hgrn"], lw["wgate"], lw["bgate"], lw["gla_norm"], lw["ret_norm"],
              lw["hgrn_norm"], lw["cos_s"], lw["sin_s"], lw["gamma"], lw["loglb"], lw["l1mlb"]]
    in_specs = ([_const_spec(x2d.shape)] + [_const_spec(c.shape) for c in consts]
                + [pl.BlockSpec((batch, gsz), lambda h: (0, h)),
                   pl.BlockSpec((batch, gsz), lambda h: (0, h)),
                   pl.BlockSpec((batch, hsz), lambda h: (0, h))])
    out_specs = [_const_spec((batch, 3 * GROUP_W)),
                 pl.BlockSpec((batch, gsz), lambda h: (0, h)),
                 pl.BlockSpec((batch, gsz), lambda h: (0, h)),
                 pl.BlockSpec((batch, hsz), lambda h: (0, h))]
    out_shape = [jax.ShapeDtypeStruct((batch, 3 * GROUP_W), BF16),
                 jax.ShapeDtypeStruct((batch, N_HEADS * gsz), F32),
                 jax.ShapeDtypeStruct((batch, N_HEADS * gsz), F32),
                 jax.ShapeDtypeStruct((batch, N_HEADS * hsz), F32)]
    scratch = [pltpu.VMEM((batch, lw["w_gla"].shape[1]), F32), pltpu.VMEM((batch, 4 * GROUP_W), F32),
               pltpu.VMEM((batch, 4 * GROUP_W), F32), pltpu.VMEM((batch, GROUP_W), F32)]
    scratch += [pltpu.VMEM((SLOT, batch), F32) for _ in range(5)]
    scratch += [pltpu.VMEM((gsz, batch), F32), pltpu.VMEM((gsz, batch), F32)]
    return pl.pallas_call(
        _sample_mixer_kernel,
        grid=(N_HEADS,),
        in_specs=in_specs, out_specs=out_specs, out_shape=out_shape, scratch_shapes=scratch,
        compiler_params=pltpu.CompilerParams(dimension_semantics=("arbitrary",), vmem_limit_bytes=VMEM_LIMIT),
        name="sample_mixer",
    )(x2d, *consts, sg.reshape(batch, -1), sr.reshape(batch, -1), sh.reshape(batch, -1))


def _out_kernel(og_ref, or_ref, oh_ref, x_ref, wo_ref, g_ref, b_ref, xt_ref):
    mix = _dot(og_ref[...], wo_ref[0]) + _dot(or_ref[...], wo_ref[1]) + _dot(oh_ref[...], wo_ref[2])
    y = ALPHA * x_ref[...] + mix
    mu = jnp.mean(y, axis=-1, keepdims=True)
    d = y - mu
    var = jnp.mean(d * d, axis=-1, keepdims=True)
    xt_ref[...] = (d * lax.rsqrt(var + NORM_EPS) * g_ref[...] + b_ref[...]).T


def _out_call(og, orr, oh, x2d, lw):
    t = x2d.shape[0]
    tm = min(512, t)
    row = lambda i: (i, 0)
    return pl.pallas_call(
        _out_kernel,
        grid=(t // tm,),
        in_specs=[pl.BlockSpec((tm, GROUP_W), row), pl.BlockSpec((tm, GROUP_W), row),
                  pl.BlockSpec((tm, GROUP_W), row), pl.BlockSpec((tm, D_MODEL), row),
                  _const_spec(lw["w_out"].shape), _const_spec((1, D_MODEL)), _const_spec((1, D_MODEL))],
        out_specs=pl.BlockSpec((D_MODEL, tm), lambda i: (0, i)),
        out_shape=jax.ShapeDtypeStruct((D_MODEL, t), F32),
        compiler_params=pltpu.CompilerParams(dimension_semantics=("arbitrary",), vmem_limit_bytes=VMEM_LIMIT),
        name="out_proj_ln",
    )(og, orr, oh, x2d, lw["w_out"], lw["ln1_g"], lw["ln1_b"])


def _oddeven_merge(lo, hi, r):
    step = r * 2
    if step < hi - lo:
        yield from _oddeven_merge(lo, hi, step)
        yield from _oddeven_merge(lo + r, hi, step)
        yield from [(i, i + r) for i in range(lo + r, hi - r, step)]
    else:
        yield (lo, lo + r)


def _oddeven_merge_sort(lo, hi):
    if hi - lo >= 1:
        mid = lo + (hi - lo) // 2
        yield from _oddeven_merge_sort(lo, mid)
        yield from _oddeven_merge_sort(mid + 1, hi)
        yield from _oddeven_merge(lo, hi, 1)


_SORT16 = tuple(_oddeven_merge_sort(0, PEER_TOPK - 1))
_BITONIC16 = tuple((i, i + d) for d in (8, 4, 2, 1) for i in range(PEER_TOPK) if i & d == 0)


def _compare_exchange(xs, pairs):
    for i, j in pairs:
        for x in xs:
            x[i], x[j] = jnp.maximum(x[i], x[j]), jnp.minimum(x[i], x[j])
    return xs


def _top16_sorted(tiles):
    xs = _compare_exchange([[a3[i] for i in range(PEER_TOPK)] for a3 in tiles], _SORT16)
    for shift in (4, 2, 1):
        ys = [[pltpu.roll(v, shift, 0) for v in x] for x in xs]
        xs = _compare_exchange([[jnp.maximum(x[i], y[PEER_TOPK - 1 - i]) for i in range(PEER_TOPK)]
                                for x, y in zip(xs, ys)], _BITONIC16)
    return xs


def _sublane_block(rows):
    sub = lax.broadcasted_iota(jnp.int32, (SUBLANE, LANE), 0)
    blk = rows[0]
    for r in range(1, SUBLANE):
        blk = jnp.where(sub == r, rows[r], blk)
    return blk


def _route_tile(a1, a2):
    v1, v2 = _top16_sorted([a1, a2])
    sub = lax.broadcasted_iota(jnp.int32, (SUBLANE, LANE), 0)
    v2a = _sublane_block(v2[0:8])
    v2b = _sublane_block(v2[8:16])
    v1b = _sublane_block(v1[8:16])
    cands = [v1[0] + v2a, v1[0] + v2b]
    for r1 in range(1, 8):
        cands.append(jnp.where(sub < PEER_TOPK // (r1 + 1), v1[r1] + v2a, NEG_INF))
    cands.append(v1b + v2[0])
    filler = jnp.full((SUBLANE, LANE), NEG_INF, F32)
    cand_tile = jnp.stack(cands + [filler] * (PEER_TOPK - len(cands)))
    thr = _top16_sorted([cand_tile])[0][PEER_TOPK - 1]
    top = v1[0] + v2[0]
    z8 = jnp.zeros((SUBLANE, LANE), F32)
    for blk in cands:
        z8 = z8 + jnp.where(blk >= thr, jnp.exp(blk - top), 0.0)
    z = jnp.broadcast_to(jnp.sum(z8, axis=0, keepdims=True), (SUBLANE, LANE))
    n1 = jnp.zeros(a1.shape, F32)
    rank2 = jnp.zeros(a2.shape, F32)
    for r in range(PEER_TOPK):
        n1 = jnp.where(a1 + v2[r] >= thr, r + 1.0, n1)
        rank2 = jnp.where(v2[r] > a2, r + 1.0, rank2)
    e1 = jnp.exp(a1 - v1[0]) * SQRT_HALF
    e2 = jnp.exp(a2 - v2[0]) / z
    return n1, rank2, e1, e2


def _peer_kernel(xt_ref, wqt_ref, sk_ref, u_ref, vt_ref, p_ref, plet_ref, gatet_ref, g2_ref, b2_ref,
                 out_ref, xbf, s_scr, n1, e1, rk2, e2, h_scr, w_scr, acc, *, tn, te, nj):
    j = pl.program_id(1)
    lane_tiles = tn // LANE
    groups = (N_KEYS // SUBLANE, SUBLANE, LANE)

    def lane_slice(lt):
        return pl.ds(pl.multiple_of(lt * LANE, LANE), LANE)

    @pl.when(j == 0)
    def _route():
        xbf[...] = xt_ref[...].astype(BF16)

        def head_body(h, carry):
            r = pl.multiple_of(h * 2 * N_KEYS, 2 * N_KEYS)
            qh = _dot(wqt_ref[pl.ds(r, 2 * N_KEYS), :], xbf[...]).astype(BF16)
            s_scr[0] = _dot(sk_ref[h, 0], qh[0:N_KEYS])
            s_scr[1] = _dot(sk_ref[h, 1], qh[N_KEYS:2 * N_KEYS])

            def lane_body(lt, c):
                lanes = lane_slice(lt)
                n1_t, rank2_t, e1_t, e2_t = _route_tile(s_scr[0, :, lanes].reshape(groups),
                                                        s_scr[1, :, lanes].reshape(groups))
                n1[lt, h] = n1_t.reshape(N_KEYS, LANE)
                e1[lt, h] = e1_t.reshape(N_KEYS, LANE)
                rk2[lt, h] = rank2_t.reshape(N_KEYS, LANE).astype(BF16).reshape(rk2.shape[2:])
                e2[lt, h] = e2_t.reshape(N_KEYS, LANE).astype(BF16).reshape(e2.shape[2:])
                return c

            lax.fori_loop(0, lane_tiles, lane_body, 0)
            return carry

        lax.fori_loop(0, PEER_HEADS, head_body, 0)
        acc[...] = jnp.zeros_like(acc)

    na = te // N_KEYS
    assert na % SUBLANE == 0
    packed = rk2.shape[2:]
    row16 = (1, 2 * SUBLANE, LANE)

    hval = _dot(u_ref[...], xbf[...])
    for l2 in range(lane_tiles):
        h_scr[l2] = hval[:, l2 * LANE:(l2 + 1) * LANE].astype(BF16)

    def gate_body(p, carry):
        for grp in range(na // SUBLANE):
            a0 = pl.multiple_of(j * na + grp * SUBLANE, SUBLANE)
            n1blk = [n1[p, h, pl.ds(a0, SUBLANE), :] for h in range(PEER_HEADS)]
            e1blk = [e1[p, h, pl.ds(a0, SUBLANE), :] for h in range(PEER_HEADS)]
            for ai in range(0, SUBLANE, 2):
                gates = [jnp.zeros(packed, BF16), jnp.zeros(packed, BF16)]
                for h in range(PEER_HEADS):
                    rk2h = rk2[p, h]
                    e2h = e2[p, h]
                    for d in range(2):
                        n1a = jnp.broadcast_to(n1blk[h][ai + d:ai + d + 1, :], row16[1:]).astype(BF16).reshape(row16)
                        e1a = jnp.broadcast_to(e1blk[h][ai + d:ai + d + 1, :], row16[1:]).astype(BF16).reshape(row16)
                        gates[d] = gates[d] + jnp.where(rk2h < n1a, e2h, jnp.zeros_like(e2h)) * e1a
                for d in range(2):
                    r0 = (grp * SUBLANE + ai + d) * N_KEYS
                    hs = h_scr[p, r0:r0 + N_KEYS, :].reshape(packed)
                    w_scr[p, r0:r0 + N_KEYS, :] = (gates[d] * (hs + hs * lax.erf(hs))).reshape(N_KEYS, LANE)
        return carry

    lax.fori_loop(0, lane_tiles, gate_body, 0)
    acc[...] += _dot(vt_ref[...], jnp.concatenate([w_scr[l2] for l2 in range(lane_tiles)], axis=1))

    @pl.when(j == nj - 1)
    def _finish():
        def norm_body(lt, c):
            lanes = lane_slice(lt)
            y = ALPHA * xt_ref[:, lanes] + acc[:, lanes]
            mu = jnp.mean(y, axis=0, keepdims=True)
            d = y - mu
            var = jnp.mean(d * d, axis=0, keepdims=True)
            yn = d * lax.rsqrt(var + NORM_EPS) * g2_ref[...] + b2_ref[...]
            acc[:, lanes] = yn
            xbf[:, lanes] = yn.astype(BF16)
            return c

        lax.fori_loop(0, lane_tiles, norm_body, 0)
        emb = _dot_nt(plet_ref[...], p_ref[...].astype(BF16))
        gt = _dot(gatet_ref[...], xbf[...])
        out_ref[...] = (acc[...] + emb * _sigmoid(gt)).T


def _peer_call(xt, p2d, lw, tables):
    u_all, vt_all, layer = tables
    t = xt.shape[1]
    tn = min(512, t)
    te = PEER_TE
    nj = vt_all.shape[1]
    lane_tiles = tn // LANE
    once = dict(pipeline_mode=pl.Buffered(1))
    in_specs = [
        pl.BlockSpec((D_MODEL, tn), lambda i, j: (0, i)),
        pl.BlockSpec(lw["wqt"].shape, lambda i, j: (0, 0), **once),
        pl.BlockSpec(lw["subkeys"].shape, lambda i, j: (0, 0, 0, 0), **once),
        pl.BlockSpec((None, te, D_MODEL), lambda i, j: (layer, j, 0)),
        pl.BlockSpec((None, None, D_MODEL, te), lambda i, j: (layer, j, 0, 0)),
        pl.BlockSpec((tn, PLE_DIM), lambda i, j: (i, 0)),
        pl.BlockSpec(lw["plet"].shape, lambda i, j: (0, 0), **once),
        pl.BlockSpec(lw["gatet"].shape, lambda i, j: (0, 0), **once),
        pl.BlockSpec((D_MODEL, LANE), lambda i, j: (0, 0), **once),
        pl.BlockSpec((D_MODEL, LANE), lambda i, j: (0, 0), **once),
    ]
    route = (lane_tiles, PEER_HEADS, N_KEYS, LANE)
    route_packed = (lane_tiles, PEER_HEADS, N_KEYS // (2 * SUBLANE), 2 * SUBLANE, LANE)
    scratch = [pltpu.VMEM((D_MODEL, tn), BF16), pltpu.VMEM((2, N_KEYS, tn), F32),
               pltpu.VMEM(route, F32), pltpu.VMEM(route, F32),
               pltpu.VMEM(route_packed, BF16), pltpu.VMEM(route_packed, BF16),
               pltpu.VMEM((lane_tiles, te, LANE), BF16), pltpu.VMEM((lane_tiles, te, LANE), BF16),
               pltpu.VMEM((D_MODEL, tn), F32)]
    return pl.pallas_call(
        functools.partial(_peer_kernel, tn=tn, te=te, nj=nj),
        grid=(t // tn, nj),
        in_specs=in_specs,
        out_specs=pl.BlockSpec((tn, D_MODEL), lambda i, j: (i, 0)),
        out_shape=jax.ShapeDtypeStruct((t, D_MODEL), F32),
        scratch_shapes=scratch,
        compiler_params=pltpu.CompilerParams(dimension_semantics=("arbitrary", "arbitrary"),
                                             vmem_limit_bytes=VMEM_LIMIT),
        name="peer_ffn_ln_ple",
    )(xt, lw["wqt"], lw["subkeys"], u_all, vt_all, p2d, lw["plet"], lw["gatet"], lw["ln2_g"], lw["ln2_b"])


def _head_slots(w, d):
    lead = w.shape[:-1]
    pad = [(0, 0)] * (len(lead) + 1) + [(0, SLOT - d)]
    return jnp.pad(w.reshape(*lead, N_HEADS, d), pad).reshape(*lead, GROUP_W)


def _rope_slots(w, d):
    lead = w.shape[:-1]
    half = d // 2
    pad = [(0, 0)] * (len(lead) + 2) + [(0, SLOT // 2 - half)]
    return jnp.pad(w.reshape(*lead, N_HEADS, 2, half), pad).reshape(*lead, GROUP_W)


def _row_slots(w, d):
    depth, _, cols = w.shape
    w = jnp.pad(w.reshape(depth, N_HEADS, d, cols), ((0, 0), (0, 0), (0, SLOT - d), (0, 0)))
    return w.reshape(depth, GROUP_W, cols)


def _rope_tables(pos):
    half = RET_DK // 2
    inv = 1.0 / (ROPE_BASE ** (jnp.arange(0, RET_DK, 2, dtype=F32) / RET_DK))
    ang = pos[:, None] * inv[None, :]
    pad = ((0, 0), (0, SLOT // 2 - half))
    cos = jnp.pad(jnp.cos(ang), pad)
    sin = jnp.pad(jnp.sin(ang), pad)
    return jnp.concatenate([cos, cos], axis=1), jnp.concatenate([-sin, sin], axis=1)


def _retention_constants():
    log_gamma = jnp.log1p(-jnp.exp2(-5.0 - jnp.arange(N_HEADS, dtype=F32)))
    i = jnp.arange(CHUNK, dtype=F32)
    diff = i[:, None] - i[None, :]
    lg = log_gamma[:, None, None]
    dmat = jnp.where(diff >= 0, jnp.exp(jnp.where(diff >= 0, diff, 0.0) * lg), 0.0)
    qdec = jnp.broadcast_to(jnp.exp((i[None, :, None] + 1.0) * lg), (N_HEADS, CHUNK, SLOT))
    kdec = jnp.broadcast_to(jnp.exp((CHUNK - 1.0 - i[None, :, None]) * lg), (N_HEADS, CHUNK, SLOT))
    sdec = jnp.broadcast_to(jnp.exp(CHUNK * lg), (N_HEADS, 1, SLOT))
    gamma = jnp.broadcast_to(jnp.exp(lg), (N_HEADS, 1, SLOT))
    return dmat, qdec, kdec, sdec, gamma


def _stacked_weights(lb, w_in, gla_w_gate, gla_b_gate, gla_norm, ret_norm, hgrn_norm, w_out,
                     ln1_g, ln1_b, ln2_g, ln2_b, peer_w_q, peer_subkeys, peer_u, peer_v, ple_proj, ple_gate):
    depth = w_in.shape[0]
    sizes = (N_HEADS * GLA_DK, N_HEADS * GLA_DK, N_HEADS * GLA_DV, N_HEADS * GLA_DV, GLA_LOWRANK,
             N_HEADS * RET_DK, N_HEADS * RET_DK, N_HEADS * RET_DV, N_HEADS * RET_DV,
             N_HEADS * HGRN_DK, N_HEADS * HGRN_DK, N_HEADS * HGRN_DV, N_HEADS * HGRN_DV)
    offs = [int(c) for c in np.cumsum(sizes)[:-1]]
    gq, gk, gv, gg, glr, rq, rk, rv, rg, hq, hf, hi, hg = jnp.split(w_in, offs, axis=2)
    sw = {}
    sw["w_gla"] = jnp.concatenate(
        [_head_slots(gq, GLA_DK), _head_slots(gk, GLA_DK), _head_slots(gv, GLA_DV), _head_slots(gg, GLA_DV),
         jnp.pad(glr, ((0, 0), (0, 0), (0, SLOT - GLA_LOWRANK)))], axis=2).astype(BF16)
    sw["w_ret"] = jnp.concatenate(
        [_rope_slots(rq, RET_DK), _rope_slots(rk, RET_DK), _head_slots(rv, RET_DV), _head_slots(rg, RET_DV)],
        axis=2).astype(BF16)
    sw["w_hgrn"] = jnp.concatenate([_head_slots(w, HGRN_DK) for w in (hq, hf, hi, hg)], axis=2).astype(BF16)
    sw["wgate"] = jnp.pad(_head_slots(gla_w_gate, GLA_DK), ((0, 0), (0, SLOT - GLA_LOWRANK), (0, 0)))
    sw["bgate"] = _head_slots(gla_b_gate, GLA_DK)[:, None, :]
    sw["gla_norm"] = jnp.pad(gla_norm, ((0, 0), (0, 0), (0, SLOT - GLA_DV)))
    sw["ret_norm"] = jnp.pad(ret_norm, ((0, 0), (0, 0), (0, SLOT - RET_DV)))
    sw["hgrn_norm"] = jnp.pad(hgrn_norm, ((0, 0), (0, 0), (0, SLOT - HGRN_DV)))
    lbh = lb.reshape(depth, 1, N_HEADS, HGRN_DK)
    pad = ((0, 0), (0, 0), (0, 0), (0, SLOT - HGRN_DK))
    sw["loglb"] = jnp.pad(jnp.log(lbh), pad, constant_values=-1.0).reshape(depth, 1, GROUP_W)
    sw["l1mlb"] = jnp.pad(jnp.log1p(-lbh), pad, constant_values=-1.0).reshape(depth, 1, GROUP_W)
    g_rows, r_rows = N_HEADS * GLA_DV, N_HEADS * RET_DV
    sw["w_out"] = jnp.stack([
        _row_slots(w_out[:, :g_rows], GLA_DV), _row_slots(w_out[:, g_rows:g_rows + r_rows], RET_DV),
        _row_slots(w_out[:, g_rows + r_rows:], HGRN_DV)], axis=1).astype(BF16)
    sw["ln1_g"], sw["ln1_b"] = ln1_g[:, None, :], ln1_b[:, None, :]
    sw["ln2_g"] = jnp.broadcast_to(ln2_g[:, :, None], (depth, D_MODEL, LANE))
    sw["ln2_b"] = jnp.broadcast_to(ln2_b[:, :, None], (depth, D_MODEL, LANE))
    sw["wqt"] = jnp.swapaxes(peer_w_q, 1, 2).astype(BF16)
    sw["subkeys"] = peer_subkeys.astype(BF16)
    sw["u"] = (peer_u * SQRT_HALF).astype(BF16)
    sw["vt"] = jnp.swapaxes(peer_v.astype(BF16).reshape(depth, -1, PEER_TE, D_MODEL), 2, 3)
    sw["plet"] = jnp.swapaxes(ple_proj, 1, 2).astype(BF16)
    sw["gatet"] = jnp.swapaxes(ple_gate, 1, 2).astype(BF16)
    return sw


_BIG_TABLES = ("u", "vt")


def _unslot_state(st, dk, dv, rope=False):
    if rope:
        half = dk // 2
        st = jnp.concatenate([st[..., :half], st[..., SLOT // 2:SLOT // 2 + half]], axis=-1)
    return jnp.swapaxes(st[..., :dv, :dk], -2, -1)


def kernel(x_prompt, x_sample, p_prompt, p_sample, state_gla, state_ret, state_hgrn, w_in, gla_w_gate,
           gla_b_gate, gla_norm, ret_norm, hgrn_lb_logits, hgrn_norm, w_out, ln1_g, ln1_b, ln2_g, ln2_b,
           peer_w_q, peer_subkeys, peer_u, peer_v, ple_proj, ple_gate):
    bp, lp, _ = x_prompt.shape
    bs = x_sample.shape[0]
    assert x_sample.shape[1] == 1 and lp % CHUNK == 0

    lb = jnp.cumsum(jax.nn.softmax(hgrn_lb_logits.astype(F32), axis=0), axis=0)
    lb = lb - lb[0:1]
    tri_np, masks_np = _chunk_constants()
    tri, masks = jnp.asarray(tri_np, BF16), jnp.asarray(masks_np)
    dmat, qdec, kdec, sdec, gamma = _retention_constants()
    cos_p, sin_p = _rope_tables(jnp.arange(lp, dtype=F32))
    cos_s, sin_s = _rope_tables(PAST_LEN + jnp.arange(1, dtype=F32))
    sw = _stacked_weights(lb, w_in, gla_w_gate, gla_b_gate, gla_norm, ret_norm, hgrn_norm, w_out,
                          ln1_g, ln1_b, ln2_g, ln2_b, peer_w_q, peer_subkeys, peer_u, peer_v,
                          ple_proj, ple_gate)

    xp = x_prompt.reshape(bp * lp, D_MODEL)
    xs = x_sample.reshape(bs, D_MODEL)
    gla_p, ret_p, hgrn_p, gla_s, ret_s, hgrn_s = [], [], [], [], [], []
    for i in range(DEPTH):
        lw = {k: v[i] for k, v in sw.items() if k not in _BIG_TABLES}
        lw.update(cos_s=cos_s, sin_s=sin_s, gamma=gamma)
        tables = (sw["u"], sw["vt"], i)

        og, sg = _prompt_mixer_call(
            _gla_prompt_kernel, xp, bp, lp, lw["w_gla"],
            [lw["wgate"], lw["bgate"], lw["gla_norm"], tri, masks],
            [lw["w_gla"].shape[1], GROUP_W], "gla_prompt")
        orr, sr = _prompt_mixer_call(
            _ret_prompt_kernel, xp, bp, lp, lw["w_ret"],
            [(cos_p, None), (sin_p, None), lw["ret_norm"], dmat, qdec, kdec, sdec],
            [4 * GROUP_W], "ret_prompt")
        oh, sh = _prompt_mixer_call(
            _hgrn_prompt_kernel, xp, bp, lp, lw["w_hgrn"],
            [lw["loglb"], lw["l1mlb"], lw["hgrn_norm"], tri, masks],
            [4 * GROUP_W], "hgrn_prompt")
        xp = _peer_call(_out_call(og, orr, oh, xp, lw), p_prompt[i].reshape(bp * lp, PLE_DIM), lw, tables)
        gla_p.append(sg)
        ret_p.append(sr)
        hgrn_p.append(sh)

        o_s, nsg, nsr, nsh = _sample_mixer_call(xs, lw, state_gla[i], state_ret[i], state_hgrn[i])
        xs = _peer_call(
            _out_call(o_s[:, :GROUP_W], o_s[:, GROUP_W:2 * GROUP_W], o_s[:, 2 * GROUP_W:], xs, lw),
            p_sample[i].reshape(bs, PLE_DIM), lw, tables)
        gla_s.append(nsg)
        ret_s.append(nsr)
        hgrn_s.append(nsh)

    return (xp.reshape(bp, lp, D_MODEL), xs.reshape(bs, 1, D_MODEL),
            _unslot_state(jnp.stack(gla_p), GLA_DK, GLA_DV),
            _unslot_state(jnp.stack(ret_p), RET_DK, RET_DV, rope=True),
            _unslot_state(jnp.stack(hgrn_p), HGRN_DK, HGRN_DV),
            jnp.stack(gla_s).reshape(DEPTH, bs, N_HEADS, GLA_DK, GLA_DV),
            jnp.stack(ret_s).reshape(DEPTH, bs, N_HEADS, RET_DK, RET_DV),
            jnp.stack(hgrn_s).reshape(DEPTH, bs, N_HEADS, HGRN_DK, HGRN_DV))
```

```python
import functools

import numpy as np
import jax
import jax.numpy as jnp
from jax import lax
from jax.experimental import pallas as pl
from jax.experimental.pallas import tpu as pltpu

F32 = jnp.float32
BF16 = jnp.bfloat16

D_MODEL = 1024
DEPTH = 2
PAST_LEN = 16384
N_HEADS = 4
GLA_DK, GLA_DV = 48, 96
RET_DK, RET_DV = 48, 96
HGRN_DK, HGRN_DV = 64, 64
GLA_LOWRANK = 16
GLA_TAU = 16.0
ROPE_BASE = 10000.0
CHUNK = 64
GLA_UNROLL = 4
HGRN_UNROLL = 8
RET_UNROLL = 8
PEER_HEADS = 8
N_KEYS = 128
PEER_TOPK = 16
PEER_TE = 2048
PLE_DIM = 256
ALPHA = (2 * DEPTH) ** 0.25
NORM_EPS = 1e-5

LANE = 128
SUBLANE = 8
SLOT = LANE
GROUP_W = N_HEADS * SLOT
VMEM_LIMIT = 56 * 1024 * 1024
NEG_INF = float("-inf")


def _dot(a, b):
    return jnp.dot(a, b, preferred_element_type=F32)


def _dot_nt(a, b):
    return lax.dot_general(a, b, (((1,), (1,)), ((), ())), preferred_element_type=F32)


def _dot_tn(a, b):
    return lax.dot_general(a, b, (((0,), (0,)), ((), ())), preferred_element_type=F32)


def _sigmoid(x):
    return jax.nn.sigmoid(x)


def _silu(x):
    return x * _sigmoid(x)


def _log_sigmoid(x):
    return jnp.minimum(x, 0.0) - jnp.log1p(jnp.exp(-jnp.abs(x)))


def _logaddexp(a, c):
    amax = jnp.maximum(a, c)
    delta = a - c
    return jnp.where(jnp.isnan(delta), a + c, amax + jnp.log1p(jnp.exp(-jnp.abs(delta))))


SQRT_HALF = np.float32(0.7071067811865476)


def _lane_mask(n):
    return (lax.broadcasted_iota(jnp.int32, (1, LANE), 1) < n).astype(F32)


def _rms_head_norm(o, g_row, dv):
    ms = jnp.sum(o * o, axis=-1, keepdims=True) * (1.0 / dv)
    return o * lax.rsqrt(ms + NORM_EPS) * g_row


def _group_head_norm(o, g_row, dv):
    mask = _lane_mask(dv)
    mu = jnp.sum(o, axis=-1, keepdims=True) * (1.0 / dv)
    d = (o - mu) * mask
    var = jnp.sum(d * d, axis=-1, keepdims=True) * (1.0 / dv)
    return d * lax.rsqrt(var + NORM_EPS) * g_row


def _chunk_constants():
    i = np.arange(CHUNK)[:, None]
    t = np.arange(CHUNK)[None, :]
    masks = [i == t]
    half = CHUNK // 2
    while half >= 1:
        blk = i // (2 * half)
        second = (i % (2 * half)) >= half
        masks.append(second & ((t % (2 * half)) < half) & (blk == t // (2 * half)))
        half //= 2
    return (t <= i).astype(np.float32), np.stack(masks).astype(np.float32)


def _cumsum_rows(tri_bf, g):
    hi = g.astype(BF16)
    r1 = g - hi.astype(F32)
    mid = r1.astype(BF16)
    lo = (r1 - mid.astype(F32)).astype(BF16)
    return _dot(tri_bf, hi) + _dot(tri_bf, mid) + _dot(tri_bf, lo)


def _level_factors(b, g):
    width = b.shape[1]
    grouped = (CHUNK // SUBLANE, SUBLANE, width)
    row = lax.broadcasted_iota(jnp.int32, b.shape, 0)
    sub = lax.broadcasted_iota(jnp.int32, grouped, 1)
    b3 = b.reshape(grouped)

    def sub_ref(r):
        return jnp.broadcast_to(b3[:, r:r + 1, :], grouped)

    out = []
    half = CHUNK // 2
    while half >= SUBLANE:
        ref = jnp.concatenate(
            [jnp.broadcast_to(b[m * 2 * half + half - 1:m * 2 * half + half], (2 * half, width))
             for m in range(CHUNK // (2 * half))], axis=0)
        out.append(jnp.exp(-jnp.abs(b - ref)))
        half //= 2
    out.append(jnp.exp(-jnp.abs(b3 - sub_ref(3))).reshape(b.shape))
    out.append(jnp.exp(-jnp.abs(b3 - jnp.where(sub < 4, sub_ref(1), sub_ref(5)))).reshape(b.shape))
    out.append(jnp.exp(jnp.where(row % 2 == 1, g, 0.0)))
    return out


def _vector_decay_chunk(q, k, v, g, st_ref, tri_bf, masks_ref):
    heads = [slice(h * SLOT, (h + 1) * SLOT) for h in range(N_HEADS)]
    b = _cumsum_rows(tri_bf, g)
    b_last = b[CHUNK - 1:CHUNK]
    factors = _level_factors(b, g)
    qs = [q.astype(BF16)] + [(q * f).astype(BF16) for f in factors]
    ks = [k.astype(BF16)] + [(k * f).astype(BF16) for f in factors]
    q_in = (q * jnp.exp(b)).astype(BF16)
    k_out = (k * jnp.exp(b_last - b)).astype(BF16)
    decay = jnp.exp(b_last)
    vb = v.astype(BF16)
    scores = []
    for sl in heads:
        sc = masks_ref[0] * _dot_nt(qs[0][:, sl], ks[0][:, sl])
        for l in range(len(factors)):
            sc = sc + masks_ref[1 + l] * _dot_nt(qs[1 + l][:, sl], ks[1 + l][:, sl])
        scores.append(sc.astype(BF16))
    outs = []
    for h, sl in enumerate(heads):
        st = st_ref[0, h]
        outs.append(_dot(scores[h], vb[:, sl]) + _dot_nt(q_in[:, sl], st.astype(BF16)))
        st_ref[0, h] = st * decay[:, sl] + _dot_tn(vb[:, sl], k_out[:, sl])
    return outs


def _gla_prompt_kernel(x_ref, w_ref, bg_ref, nrm_ref, tri_ref, masks_ref,
                       o_ref, st_ref, z_scr, la_scr, *, seg):
    @pl.when(pl.program_id(1) == 0)
    def _():
        st_ref[...] = jnp.zeros_like(st_ref)

    z_scr[...] = _dot(x_ref[...].astype(BF16), w_ref[...])
    pre = z_scr[:, 4 * GROUP_W:5 * GROUP_W] + bg_ref[...]
    la_scr[...] = _log_sigmoid(pre) * (1.0 / GLA_TAU)
    tri = tri_ref[...]

    def body(c, carry):
        r0 = pl.multiple_of(c * CHUNK, CHUNK)
        rows = pl.ds(r0, CHUNK)
        q = z_scr[rows, 0:GROUP_W] * (GLA_DK ** -0.5)
        k = z_scr[rows, GROUP_W:2 * GROUP_W]
        v = z_scr[rows, 2 * GROUP_W:3 * GROUP_W]
        outs = _vector_decay_chunk(q, k, v, la_scr[rows, :], st_ref, tri, masks_ref)
        for h in range(N_HEADS):
            gate = z_scr[rows, 3 * GROUP_W + h * SLOT:3 * GROUP_W + (h + 1) * SLOT]
            on = _rms_head_norm(outs[h], nrm_ref[h:h + 1, :], GLA_DV)
            o_ref[rows, h * SLOT:(h + 1) * SLOT] = (on * _silu(gate)).astype(BF16)
        return carry

    lax.fori_loop(0, seg // CHUNK, body, 0, unroll=GLA_UNROLL)


def _hgrn_prompt_kernel(x_ref, w_ref, loglb_ref, l1mlb_ref, nrm_ref, tri_ref, masks_ref,
                        o_ref, st_ref, z_scr, *, seg):
    @pl.when(pl.program_id(1) == 0)
    def _():
        st_ref[...] = jnp.zeros_like(st_ref)

    z_scr[...] = _dot(x_ref[...].astype(BF16), w_ref[...])
    tri = tri_ref[...]
    kmask = jnp.concatenate([_lane_mask(HGRN_DK)] * N_HEADS, axis=1)

    def body(c, carry):
        r0 = pl.multiple_of(c * CHUNK, CHUNK)
        rows = pl.ds(r0, CHUNK)
        q = _silu(z_scr[rows, 0:GROUP_W])
        v = z_scr[rows, 2 * GROUP_W:3 * GROUP_W]
        log_f = _logaddexp(loglb_ref[...], l1mlb_ref[...] + _log_sigmoid(z_scr[rows, GROUP_W:2 * GROUP_W]))
        k = (1.0 - jnp.exp(log_f)) * kmask
        outs = _vector_decay_chunk(q, k, v, log_f, st_ref, tri, masks_ref)
        for h in range(N_HEADS):
            sl = slice(h * SLOT, (h + 1) * SLOT)
            gate = z_scr[rows, 3 * GROUP_W + h * SLOT:3 * GROUP_W + (h + 1) * SLOT]
            on = _rms_head_norm(outs[h], nrm_ref[h:h + 1, :], HGRN_DV)
            o_ref[rows, sl] = (on * _silu(gate)).astype(BF16)
        return carry

    lax.fori_loop(0, seg // CHUNK, body, 0, unroll=HGRN_UNROLL)


def _rotate(t, cs, sn):
    return t * cs + pltpu.roll(t, SLOT // 2, 1) * sn


def _ret_prompt_kernel(x_ref, w_ref, cos_ref, sin_ref, nrm_ref, dmat_ref, qdec_ref, kdec_ref, sdec_ref,
                       o_ref, st_ref, z_scr, *, seg):
    @pl.when(pl.program_id(1) == 0)
    def _():
        st_ref[...] = jnp.zeros_like(st_ref)

    z_scr[...] = _dot(x_ref[...].astype(BF16), w_ref[...])

    def body(c, carry):
        r0 = pl.multiple_of(c * CHUNK, CHUNK)
        rows = pl.ds(r0, CHUNK)
        cs = cos_ref[rows, :]
        sn = sin_ref[rows, :]
        heads = [slice(h * SLOT, (h + 1) * SLOT) for h in range(N_HEADS)]
        qs = [_rotate(z_scr[rows, h * SLOT:(h + 1) * SLOT], cs, sn) for h in range(N_HEADS)]
        ks = [_rotate(z_scr[rows, GROUP_W + h * SLOT:GROUP_W + (h + 1) * SLOT], cs, sn) * (RET_DK ** -0.5)
              for h in range(N_HEADS)]
        vb = z_scr[rows, 2 * GROUP_W:3 * GROUP_W].astype(BF16)
        scores = [(dmat_ref[h] * _dot_nt(qs[h].astype(BF16), ks[h].astype(BF16))).astype(BF16)
                  for h in range(N_HEADS)]
        q_in = [(qs[h] * qdec_ref[h]).astype(BF16) for h in range(N_HEADS)]
        k_out = [(ks[h] * kdec_ref[h]).astype(BF16) for h in range(N_HEADS)]
        outs = []
        for h, sl in enumerate(heads):
            st = st_ref[0, h]
            outs.append(_dot(scores[h], vb[:, sl]) + _dot_nt(q_in[h], st.astype(BF16)))
            st_ref[0, h] = st * sdec_ref[h] + _dot_tn(vb[:, sl], k_out[h])
        for h, sl in enumerate(heads):
            gate = z_scr[rows, 3 * GROUP_W + h * SLOT:3 * GROUP_W + (h + 1) * SLOT]
            on = _group_head_norm(outs[h], nrm_ref[h:h + 1, :], RET_DV)
            o_ref[rows, sl] = (on * _silu(gate)).astype(BF16)
        return carry

    lax.fori_loop(0, seg // CHUNK, body, 0, unroll=RET_UNROLL)


def _const_spec(shape):
    nd = len(shape)
    return pl.BlockSpec(shape, lambda *_: (0,) * nd)


def _prompt_mixer_call(kernel, x2d, batch, seq, w, extras, scratch_widths, name):
    seg = min(512, seq)
    nseg = seq // seg
    in_specs = [pl.BlockSpec((seg, D_MODEL), lambda b, s: (b * nseg + s, 0)), _const_spec(w.shape)]
    args = [x2d, w]
    for e in extras:
        if isinstance(e, tuple):
            arr, _ = e
            in_specs.append(pl.BlockSpec((seg, arr.shape[1]), lambda b, s: (s, 0)))
            args.append(arr)
        else:
            in_specs.append(_const_spec(e.shape))
            args.append(e)
    return pl.pallas_call(
        functools.partial(kernel, seg=seg),
        grid=(batch, nseg),
        in_specs=in_specs,
        out_specs=[pl.BlockSpec((seg, GROUP_W), lambda b, s: (b * nseg + s, 0)),
                   pl.BlockSpec((1, N_HEADS, SLOT, SLOT), lambda b, s: (b, 0, 0, 0))],
        out_shape=[jax.ShapeDtypeStruct((batch * seq, GROUP_W), BF16),
                   jax.ShapeDtypeStruct((batch, N_HEADS, SLOT, SLOT), F32)],
        scratch_shapes=[pltpu.VMEM((seg, wd), F32) for wd in scratch_widths],
        compiler_params=pltpu.CompilerParams(dimension_semantics=("arbitrary", "arbitrary"),
                                             vmem_limit_bytes=VMEM_LIMIT),
        name=name,
    )(*args)


def _sample_step(q, k, eg, v, s_ref, ns_ref, tq, tk, te, tv, to, s_t, sn_t, dk, dv, row_of):
    tq[...] = q.T
    tk[...] = k.T
    te[...] = eg.T
    tv[...] = v.T
    s_t[0:dk * dv, :] = s_ref[...].T
    vt = tv[0:dv, :]

    def body(kk, oacc):
        kr = row_of(kk)
        r = pl.multiple_of(kk * dv, SUBLANE)
        sn = s_t[pl.ds(r, dv), :] * te[pl.ds(kr, 1), :] + tk[pl.ds(kr, 1), :] * vt
        sn_t[pl.ds(r, dv), :] = sn
        return oacc + tq[pl.ds(kr, 1), :] * sn

    o_t = lax.fori_loop(0, dk, body, jnp.zeros((dv, q.shape[0]), F32))
    ns_ref[...] = sn_t[0:dk * dv, :].T
    to[...] = jnp.zeros_like(to)
    to[0:dv, :] = o_t
    return to[...].T


def _sample_mixer_kernel(x_ref, wg_ref, wr_ref, wh_ref, bgate_ref, gn_ref, rn_ref, hn_ref,
                         cos_ref, sin_ref, gam_ref, loglb_ref, l1mlb_ref, sg_ref, sr_ref, sh_ref,
                         o_ref, nsg_ref, nsr_ref, nsh_ref,
                         zg, zr, zh, la, tq, tk, te, tv, to, s_t, sn_t):
    h = pl.program_id(0)

    @pl.when(h == 0)
    def _():
        xb = x_ref[...].astype(BF16)
        zg[...] = _dot(xb, wg_ref[...])
        zr[...] = _dot(xb, wr_ref[...])
        zh[...] = _dot(xb, wh_ref[...])
        pre = zg[:, 4 * GROUP_W:5 * GROUP_W] + bgate_ref[...]
        la[...] = _log_sigmoid(pre) * (1.0 / GLA_TAU)

    off = pl.multiple_of(h * SLOT, SLOT)
    sl = pl.ds(off, SLOT)
    tr = (tq, tk, te, tv, to, s_t, sn_t)
    batch = x_ref.shape[0]

    q = zg[:, sl] * (GLA_DK ** -0.5)
    k = zg[:, pl.ds(GROUP_W + off, SLOT)]
    v = zg[:, pl.ds(2 * GROUP_W + off, SLOT)]
    gate = zg[:, pl.ds(3 * GROUP_W + off, SLOT)]
    o = _sample_step(q, k, jnp.exp(la[:, sl]), v, sg_ref, nsg_ref, *tr, GLA_DK, GLA_DV, lambda kk: kk)
    o_ref[:, sl] = (_rms_head_norm(o, gn_ref[pl.ds(h, 1), :], GLA_DV) * _silu(gate)).astype(BF16)

    cs = cos_ref[...]
    sn = sin_ref[...]
    q = _rotate(zr[:, sl], cs, sn)
    k = _rotate(zr[:, pl.ds(GROUP_W + off, SLOT)], cs, sn) * (RET_DK ** -0.5)
    v = zr[:, pl.ds(2 * GROUP_W + off, SLOT)]
    gate = zr[:, pl.ds(3 * GROUP_W + off, SLOT)]
    eg = jnp.broadcast_to(gam_ref[h], (batch, SLOT))
    half = RET_DK // 2
    o = _sample_step(q, k, eg, v, sr_ref, nsr_ref, *tr, RET_DK, RET_DV,
                     lambda kk: kk + jnp.where(kk >= half, SLOT // 2 - half, 0))
    o_ref[:, pl.ds(GROUP_W + off, SLOT)] = (
        _group_head_norm(o, rn_ref[pl.ds(h, 1), :], RET_DV) * _silu(gate)).astype(BF16)

    q = _silu(zh[:, sl])
    hf = zh[:, pl.ds(GROUP_W + off, SLOT)]
    v = zh[:, pl.ds(2 * GROUP_W + off, SLOT)]
    gate = zh[:, pl.ds(3 * GROUP_W + off, SLOT)]
    log_f = _logaddexp(loglb_ref[:, sl], l1mlb_ref[:, sl] + _log_sigmoid(hf))
    f = jnp.exp(log_f)
    o = _sample_step(q, (1.0 - f) * _lane_mask(HGRN_DK), f, v, sh_ref, nsh_ref, *tr,
                     HGRN_DK, HGRN_DV, lambda kk: kk)
    o_ref[:, pl.ds(2 * GROUP_W + off, SLOT)] = (
        _rms_head_norm(o, hn_ref[pl.ds(h, 1), :], HGRN_DV) * _silu(gate)).astype(BF16)


def _sample_mixer_call(x2d, lw, sg, sr, sh):
    batch = x2d.shape[0]
    gsz, hsz = GLA_DK * GLA_DV, HGRN_DK * HGRN_DV
    consts = [lw["w_gla"], lw["w_ret"], lw["w_hgrn"], lw["bgate"], lw["gla_norm"], lw["ret_norm"],
              lw["hgrn_norm"], lw["cos_s"], lw["sin_s"], lw["gamma"], lw["loglb"], lw["l1mlb"]]
    in_specs = ([_const_spec(x2d.shape)] + [_const_spec(c.shape) for c in consts]
                + [pl.BlockSpec((batch, gsz), lambda h: (0, h)),
                   pl.BlockSpec((batch, gsz), lambda h: (0, h)),
                   pl.BlockSpec((batch, hsz), lambda h: (0, h))])
    out_specs = [_const_spec((batch, 3 * GROUP_W)),
                 pl.BlockSpec((batch, gsz), lambda h: (0, h)),
                 pl.BlockSpec((batch, gsz), lambda h: (0, h)),
                 pl.BlockSpec((batch, hsz), lambda h: (0, h))]
    out_shape = [jax.ShapeDtypeStruct((batch, 3 * GROUP_W), BF16),
                 jax.ShapeDtypeStruct((batch, N_HEADS * gsz), F32),
                 jax.ShapeDtypeStruct((batch, N_HEADS * gsz), F32),
                 jax.ShapeDtypeStruct((batch, N_HEADS * hsz), F32)]
    scratch = [pltpu.VMEM((batch, lw["w_gla"].shape[1]), F32), pltpu.VMEM((batch, 4 * GROUP_W), F32),
               pltpu.VMEM((batch, 4 * GROUP_W), F32), pltpu.VMEM((batch, GROUP_W), F32)]
    scratch += [pltpu.VMEM((SLOT, batch), F32) for _ in range(5)]
    scratch += [pltpu.VMEM((gsz, batch), F32), pltpu.VMEM((gsz, batch), F32)]
    return pl.pallas_call(
        _sample_mixer_kernel,
        grid=(N_HEADS,),
        in_specs=in_specs, out_specs=out_specs, out_shape=out_shape, scratch_shapes=scratch,
        compiler_params=pltpu.CompilerParams(dimension_semantics=("arbitrary",), vmem_limit_bytes=VMEM_LIMIT),
        name="sample_mixer",
    )(x2d, *consts, sg.reshape(batch, -1), sr.reshape(batch, -1), sh.reshape(batch, -1))


def _out_kernel(og_ref, or_ref, oh_ref, x_ref, wo_ref, g_ref, b_ref, xt_ref):
    mix = _dot(og_ref[...], wo_ref[0]) + _dot(or_ref[...], wo_ref[1]) + _dot(oh_ref[...], wo_ref[2])
    y = ALPHA * x_ref[...] + mix
    mu = jnp.mean(y, axis=-1, keepdims=True)
    d = y - mu
    var = jnp.mean(d * d, axis=-1, keepdims=True)
    xt_ref[...] = (d * lax.rsqrt(var + NORM_EPS) * g_ref[...] + b_ref[...]).T


def _out_call(og, orr, oh, x2d, lw):
    t = x2d.shape[0]
    tm = min(512, t)
    row = lambda i: (i, 0)
    return pl.pallas_call(
        _out_kernel,
        grid=(t // tm,),
        in_specs=[pl.BlockSpec((tm, GROUP_W), row), pl.BlockSpec((tm, GROUP_W), row),
                  pl.BlockSpec((tm, GROUP_W), row), pl.BlockSpec((tm, D_MODEL), row),
                  _const_spec(lw["w_out"].shape), _const_spec((1, D_MODEL)), _const_spec((1, D_MODEL))],
        out_specs=pl.BlockSpec((D_MODEL, tm), lambda i: (0, i)),
        out_shape=jax.ShapeDtypeStruct((D_MODEL, t), F32),
        compiler_params=pltpu.CompilerParams(dimension_semantics=("arbitrary",), vmem_limit_bytes=VMEM_LIMIT),
        name="out_proj_ln",
    )(og, orr, oh, x2d, lw["w_out"], lw["ln1_g"], lw["ln1_b"])


def _oddeven_merge(lo, hi, r):
    step = r * 2
    if step < hi - lo:
        yield from _oddeven_merge(lo, hi, step)
        yield from _oddeven_merge(lo + r, hi, step)
        yield from [(i, i + r) for i in range(lo + r, hi - r, step)]
    else:
        yield (lo, lo + r)


def _oddeven_merge_sort(lo, hi):
    if hi - lo >= 1:
        mid = lo + (hi - lo) // 2
        yield from _oddeven_merge_sort(lo, mid)
        yield from _oddeven_merge_sort(mid + 1, hi)
        yield from _oddeven_merge(lo, hi, 1)


_SORT16 = tuple(_oddeven_merge_sort(0, PEER_TOPK - 1))
_BITONIC16 = tuple((i, i + d) for d in (8, 4, 2, 1) for i in range(PEER_TOPK) if i & d == 0)


def _compare_exchange(xs, pairs):
    for i, j in pairs:
        for x in xs:
            x[i], x[j] = jnp.maximum(x[i], x[j]), jnp.minimum(x[i], x[j])
    return xs


def _top16_sorted(tiles):
    xs = _compare_exchange([[a3[i] for i in range(PEER_TOPK)] for a3 in tiles], _SORT16)
    for shift in (4, 2, 1):
        ys = [[pltpu.roll(v, shift, 0) for v in x] for x in xs]
        xs = _compare_exchange([[jnp.maximum(x[i], y[PEER_TOPK - 1 - i]) for i in range(PEER_TOPK)]
                                for x, y in zip(xs, ys)], _BITONIC16)
    return xs


def _sublane_block(rows):
    sub = lax.broadcasted_iota(jnp.int32, (SUBLANE, LANE), 0)
    blk = rows[0]
    for r in range(1, SUBLANE):
        blk = jnp.where(sub == r, rows[r], blk)
    return blk


def _route_tile(a1, a2):
    v1, v2 = _top16_sorted([a1, a2])
    sub = lax.broadcasted_iota(jnp.int32, (SUBLANE, LANE), 0)
    v2a = _sublane_block(v2[0:8])
    v2b = _sublane_block(v2[8:16])
    v1b = _sublane_block(v1[8:16])
    cands = [v1[0] + v2a, v1[0] + v2b]
    for r1 in range(1, 8):
        cands.append(jnp.where(sub < PEER_TOPK // (r1 + 1), v1[r1] + v2a, NEG_INF))
    cands.append(v1b + v2[0])
    filler = jnp.full((SUBLANE, LANE), NEG_INF, F32)
    cand_tile = jnp.stack(cands + [filler] * (PEER_TOPK - len(cands)))
    thr = _top16_sorted([cand_tile])[0][PEER_TOPK - 1]
    top = v1[0] + v2[0]
    z8 = jnp.zeros((SUBLANE, LANE), F32)
    for blk in cands:
        z8 = z8 + jnp.where(blk >= thr, jnp.exp(blk - top), 0.0)
    z = jnp.broadcast_to(jnp.sum(z8, axis=0, keepdims=True), (SUBLANE, LANE))
    n1 = jnp.zeros(a1.shape, F32)
    rank2 = jnp.zeros(a2.shape, F32)
    for r in range(PEER_TOPK):
        n1 = jnp.where(a1 + v2[r] >= thr, r + 1.0, n1)
        rank2 = jnp.where(v2[r] > a2, r + 1.0, rank2)
    e1 = jnp.exp(a1 - v1[0]) * SQRT_HALF
    e2 = jnp.exp(a2 - v2[0]) / z
    return n1, rank2, e1, e2


def _bf16_pair(x):
    bits = pltpu.bitcast(x.astype(BF16).astype(F32), jnp.uint32)
    return bits | (bits >> 16)


def _bf16_row(words):
    return pltpu.bitcast(jnp.broadcast_to(words, (SUBLANE, LANE)), BF16).reshape(1, 2 * SUBLANE, LANE)


def _peer_kernel(xt_ref, wqt_ref, sk_ref, u_ref, vt_ref, p_ref, plet_ref, gatet_ref, g2_ref, b2_ref,
                 out_ref, xbf, s_scr, n1, e1, rk2, e2, h_scr, w_scr, acc, *, tn, te, nj):
    j = pl.program_id(1)
    lane_tiles = tn // LANE
    groups = (N_KEYS // SUBLANE, SUBLANE, LANE)

    def lane_slice(lt):
        return pl.ds(pl.multiple_of(lt * LANE, LANE), LANE)

    @pl.when(j == 0)
    def _route():
        xbf[...] = xt_ref[...].astype(BF16)

        def head_body(h, carry):
            r = pl.multiple_of(h * 2 * N_KEYS, 2 * N_KEYS)
            qh = _dot(wqt_ref[pl.ds(r, 2 * N_KEYS), :], xbf[...]).astype(BF16)
            s_scr[0] = _dot(sk_ref[h, 0], qh[0:N_KEYS])
            s_scr[1] = _dot(sk_ref[h, 1], qh[N_KEYS:2 * N_KEYS])

            def lane_body(lt, c):
                lanes = lane_slice(lt)
                n1_t, rank2_t, e1_t, e2_t = _route_tile(s_scr[0, :, lanes].reshape(groups),
                                                        s_scr[1, :, lanes].reshape(groups))
                n1[lt, h] = _bf16_pair(n1_t.reshape(N_KEYS, LANE))
                e1[lt, h] = _bf16_pair(e1_t.reshape(N_KEYS, LANE))
                rk2[lt, h] = rank2_t.reshape(N_KEYS, LANE).astype(BF16).reshape(rk2.shape[2:])
                e2[lt, h] = e2_t.reshape(N_KEYS, LANE).astype(BF16).reshape(e2.shape[2:])
                return c

            lax.fori_loop(0, lane_tiles, lane_body, 0)
            return carry

        lax.fori_loop(0, PEER_HEADS, head_body, 0)
        acc[...] = jnp.zeros_like(acc)

    na = te // N_KEYS
    assert na % SUBLANE == 0
    packed = rk2.shape[2:]

    hval = _dot(u_ref[...], xbf[...])
    for l2 in range(lane_tiles):
        h_scr[l2] = hval[:, l2 * LANE:(l2 + 1) * LANE].astype(BF16)

    def gate_body(p, carry):
        for grp in range(na // SUBLANE):
            a0 = pl.multiple_of(j * na + grp * SUBLANE, SUBLANE)
            n1blk = [n1[p, h, pl.ds(a0, SUBLANE), :] for h in range(PEER_HEADS)]
            e1blk = [e1[p, h, pl.ds(a0, SUBLANE), :] for h in range(PEER_HEADS)]
            for ai in range(0, SUBLANE, 2):
                gates = [jnp.zeros(packed, BF16), jnp.zeros(packed, BF16)]
                for h in range(PEER_HEADS):
                    rk2h = rk2[p, h]
                    e2h = e2[p, h]
                    for d in range(2):
                        n1a = _bf16_row(n1blk[h][ai + d:ai + d + 1, :])
                        e1a = _bf16_row(e1blk[h][ai + d:ai + d + 1, :])
                        gates[d] = gates[d] + jnp.where(rk2h < n1a, e2h, jnp.zeros_like(e2h)) * e1a
                for d in range(2):
                    r0 = (grp * SUBLANE + ai + d) * N_KEYS
                    hs = h_scr[p, r0:r0 + N_KEYS, :].reshape(packed)
                    w_scr[p, r0:r0 + N_KEYS, :] = (gates[d] * (hs + hs * lax.erf(hs))).reshape(N_KEYS, LANE)
        return carry

    lax.fori_loop(0, lane_tiles, gate_body, 0)
    acc[...] += _dot(vt_ref[...], jnp.concatenate([w_scr[l2] for l2 in range(lane_tiles)], axis=1))

    @pl.when(j == nj - 1)
    def _finish():
        def norm_body(lt, c):
            lanes = lane_slice(lt)
            y = ALPHA * xt_ref[:, lanes] + acc[:, lanes]
            mu = jnp.mean(y, axis=0, keepdims=True)
            d = y - mu
            var = jnp.mean(d * d, axis=0, keepdims=True)
            yn = d * lax.rsqrt(var + NORM_EPS) * g2_ref[...] + b2_ref[...]
            acc[:, lanes] = yn
            xbf[:, lanes] = yn.astype(BF16)
            return c

        lax.fori_loop(0, lane_tiles, norm_body, 0)
        emb = _dot_nt(plet_ref[...], p_ref[...].astype(BF16))
        gt = _dot(gatet_ref[...], xbf[...])
        out_ref[...] = (acc[...] + emb * _sigmoid(gt)).T


def _peer_call(xt, p2d, lw, tables):
    u_all, vt_all, layer = tables
    t = xt.shape[1]
    tn = min(512, t)
    te = PEER_TE
    nj = vt_all.shape[1]
    lane_tiles = tn // LANE
    once = dict(pipeline_mode=pl.Buffered(1))
    in_specs = [
        pl.BlockSpec((D_MODEL, tn), lambda i, j: (0, i)),
        pl.BlockSpec(lw["wqt"].shape, lambda i, j: (0, 0), **once),
        pl.BlockSpec(lw["subkeys"].shape, lambda i, j: (0, 0, 0, 0), **once),
        pl.BlockSpec((None, te, D_MODEL), lambda i, j: (layer, j, 0)),
        pl.BlockSpec((None, None, D_MODEL, te), lambda i, j: (layer, j, 0, 0)),
        pl.BlockSpec((tn, PLE_DIM), lambda i, j: (i, 0)),
        pl.BlockSpec(lw["plet"].shape, lambda i, j: (0, 0), **once),
        pl.BlockSpec(lw["gatet"].shape, lambda i, j: (0, 0), **once),
        pl.BlockSpec((D_MODEL, LANE), lambda i, j: (0, 0), **once),
        pl.BlockSpec((D_MODEL, LANE), lambda i, j: (0, 0), **once),
    ]
    route = (lane_tiles, PEER_HEADS, N_KEYS, LANE)
    route_packed = (lane_tiles, PEER_HEADS, N_KEYS // (2 * SUBLANE), 2 * SUBLANE, LANE)
    scratch = [pltpu.VMEM((D_MODEL, tn), BF16), pltpu.VMEM((2, N_KEYS, tn), F32),
               pltpu.VMEM(route, jnp.uint32), pltpu.VMEM(route, jnp.uint32),
               pltpu.VMEM(route_packed, BF16), pltpu.VMEM(route_packed, BF16),
               pltpu.VMEM((lane_tiles, te, LANE), BF16), pltpu.VMEM((lane_tiles, te, LANE), BF16),
               pltpu.VMEM((D_MODEL, tn), F32)]
    return pl.pallas_call(
        functools.partial(_peer_kernel, tn=tn, te=te, nj=nj),
        grid=(t // tn, nj),
        in_specs=in_specs,
        out_specs=pl.BlockSpec((tn, D_MODEL), lambda i, j: (i, 0)),
        out_shape=jax.ShapeDtypeStruct((t, D_MODEL), F32),
        scratch_shapes=scratch,
        compiler_params=pltpu.CompilerParams(dimension_semantics=("arbitrary", "arbitrary"),
                                             vmem_limit_bytes=VMEM_LIMIT),
        name="peer_ffn_ln_ple",
    )(xt, lw["wqt"], lw["subkeys"], u_all, vt_all, p2d, lw["plet"], lw["gatet"], lw["ln2_g"], lw["ln2_b"])


def _head_slots(w, d):
    lead = w.shape[:-1]
    pad = [(0, 0)] * (len(lead) + 1) + [(0, SLOT - d)]
    return jnp.pad(w.reshape(*lead, N_HEADS, d), pad).reshape(*lead, GROUP_W)


def _rope_slots(w, d):
    lead = w.shape[:-1]
    half = d // 2
    pad = [(0, 0)] * (len(lead) + 2) + [(0, SLOT // 2 - half)]
    return jnp.pad(w.reshape(*lead, N_HEADS, 2, half), pad).reshape(*lead, GROUP_W)


def _row_slots(w, d):
    depth, _, cols = w.shape
    w = jnp.pad(w.reshape(depth, N_HEADS, d, cols), ((0, 0), (0, 0), (0, SLOT - d), (0, 0)))
    return w.reshape(depth, GROUP_W, cols)


def _rope_tables(pos):
    half = RET_DK // 2
    inv = 1.0 / (ROPE_BASE ** (jnp.arange(0, RET_DK, 2, dtype=F32) / RET_DK))
    ang = pos[:, None] * inv[None, :]
    pad = ((0, 0), (0, SLOT // 2 - half))
    cos = jnp.pad(jnp.cos(ang), pad)
    sin = jnp.pad(jnp.sin(ang), pad)
    return jnp.concatenate([cos, cos], axis=1), jnp.concatenate([-sin, sin], axis=1)


def _retention_constants():
    log_gamma = jnp.log1p(-jnp.exp2(-5.0 - jnp.arange(N_HEADS, dtype=F32)))
    i = jnp.arange(CHUNK, dtype=F32)
    diff = i[:, None] - i[None, :]
    lg = log_gamma[:, None, None]
    dmat = jnp.where(diff >= 0, jnp.exp(jnp.where(diff >= 0, diff, 0.0) * lg), 0.0)
    qdec = jnp.broadcast_to(jnp.exp((i[None, :, None] + 1.0) * lg), (N_HEADS, CHUNK, SLOT))
    kdec = jnp.broadcast_to(jnp.exp((CHUNK - 1.0 - i[None, :, None]) * lg), (N_HEADS, CHUNK, SLOT))
    sdec = jnp.broadcast_to(jnp.exp(CHUNK * lg), (N_HEADS, 1, SLOT))
    gamma = jnp.broadcast_to(jnp.exp(lg), (N_HEADS, 1, SLOT))
    return dmat, qdec, kdec, sdec, gamma


def _stacked_weights(lb, w_in, gla_w_gate, gla_b_gate, gla_norm, ret_norm, hgrn_norm, w_out,
                     ln1_g, ln1_b, ln2_g, ln2_b, peer_w_q, peer_subkeys, peer_u, peer_v, ple_proj, ple_gate):
    depth = w_in.shape[0]
    sizes = (N_HEADS * GLA_DK, N_HEADS * GLA_DK, N_HEADS * GLA_DV, N_HEADS * GLA_DV, GLA_LOWRANK,
             N_HEADS * RET_DK, N_HEADS * RET_DK, N_HEADS * RET_DV, N_HEADS * RET_DV,
             N_HEADS * HGRN_DK, N_HEADS * HGRN_DK, N_HEADS * HGRN_DV, N_HEADS * HGRN_DV)
    offs = [int(c) for c in np.cumsum(sizes)[:-1]]
    gq, gk, gv, gg, glr, rq, rk, rv, rg, hq, hf, hi, hg = jnp.split(w_in, offs, axis=2)
    sw = {}
    gate_w = jnp.einsum("dir,drk->dik", glr, gla_w_gate, precision=lax.Precision.HIGHEST)
    sw["w_gla"] = jnp.concatenate(
        [_head_slots(gq, GLA_DK), _head_slots(gk, GLA_DK), _head_slots(gv, GLA_DV), _head_slots(gg, GLA_DV),
         _head_slots(gate_w, GLA_DK)], axis=2).astype(BF16)
    sw["w_ret"] = jnp.concatenate(
        [_rope_slots(rq, RET_DK), _rope_slots(rk, RET_DK), _head_slots(rv, RET_DV), _head_slots(rg, RET_DV)],
        axis=2).astype(BF16)
    sw["w_hgrn"] = jnp.concatenate([_head_slots(w, HGRN_DK) for w in (hq, hf, hi, hg)], axis=2).astype(BF16)
    sw["bgate"] = _head_slots(gla_b_gate, GLA_DK)[:, None, :]
    sw["gla_norm"] = jnp.pad(gla_norm, ((0, 0), (0, 0), (0, SLOT - GLA_DV)))
    sw["ret_norm"] = jnp.pad(ret_norm, ((0, 0), (0, 0), (0, SLOT - RET_DV)))
    sw["hgrn_norm"] = jnp.pad(hgrn_norm, ((0, 0), (0, 0), (0, SLOT - HGRN_DV)))
    lbh = lb.reshape(depth, 1, N_HEADS, HGRN_DK)
    pad = ((0, 0), (0, 0), (0, 0), (0, SLOT - HGRN_DK))
    sw["loglb"] = jnp.pad(jnp.log(lbh), pad, constant_values=-1.0).reshape(depth, 1, GROUP_W)
    sw["l1mlb"] = jnp.pad(jnp.log1p(-lbh), pad, constant_values=-1.0).reshape(depth, 1, GROUP_W)
    g_rows, r_rows = N_HEADS * GLA_DV, N_HEADS * RET_DV
    sw["w_out"] = jnp.stack([
        _row_slots(w_out[:, :g_rows], GLA_DV), _row_slots(w_out[:, g_rows:g_rows + r_rows], RET_DV),
        _row_slots(w_out[:, g_rows + r_rows:], HGRN_DV)], axis=1).astype(BF16)
    sw["ln1_g"], sw["ln1_b"] = ln1_g[:, None, :], ln1_b[:, None, :]
    sw["ln2_g"] = jnp.broadcast_to(ln2_g[:, :, None], (depth, D_MODEL, LANE))
    sw["ln2_b"] = jnp.broadcast_to(ln2_b[:, :, None], (depth, D_MODEL, LANE))
    sw["wqt"] = jnp.swapaxes(peer_w_q, 1, 2).astype(BF16)
    sw["subkeys"] = peer_subkeys.astype(BF16)
    sw["u"] = (peer_u * SQRT_HALF).astype(BF16)
    sw["vt"] = jnp.swapaxes(peer_v.astype(BF16).reshape(depth, -1, PEER_TE, D_MODEL), 2, 3)
    sw["plet"] = jnp.swapaxes(ple_proj, 1, 2).astype(BF16)
    sw["gatet"] = jnp.swapaxes(ple_gate, 1, 2).astype(BF16)
    return sw


_BIG_TABLES = ("u", "vt")


def _unslot_state(st, dk, dv, rope=False):
    if rope:
        half = dk // 2
        st = jnp.concatenate([st[..., :half], st[..., SLOT // 2:SLOT // 2 + half]], axis=-1)
    return jnp.swapaxes(st[..., :dv, :dk], -2, -1)


def kernel(x_prompt, x_sample, p_prompt, p_sample, state_gla, state_ret, state_hgrn, w_in, gla_w_gate,
           gla_b_gate, gla_norm, ret_norm, hgrn_lb_logits, hgrn_norm, w_out, ln1_g, ln1_b, ln2_g, ln2_b,
           peer_w_q, peer_subkeys, peer_u, peer_v, ple_proj, ple_gate):
    bp, lp, _ = x_prompt.shape
    bs = x_sample.shape[0]
    assert x_sample.shape[1] == 1 and lp % CHUNK == 0

    lb = jnp.cumsum(jax.nn.softmax(hgrn_lb_logits.astype(F32), axis=0), axis=0)
    lb = lb - lb[0:1]
    tri_np, masks_np = _chunk_constants()
    tri, masks = jnp.asarray(tri_np, BF16), jnp.asarray(masks_np)
    dmat, qdec, kdec, sdec, gamma = _retention_constants()
    cos_p, sin_p = _rope_tables(jnp.arange(lp, dtype=F32))
    cos_s, sin_s = _rope_tables(PAST_LEN + jnp.arange(1, dtype=F32))
    sw = _stacked_weights(lb, w_in, gla_w_gate, gla_b_gate, gla_norm, ret_norm, hgrn_norm, w_out,
                          ln1_g, ln1_b, ln2_g, ln2_b, peer_w_q, peer_subkeys, peer_u, peer_v,
                          ple_proj, ple_gate)

    xp = x_prompt.reshape(bp * lp, D_MODEL)
    xs = x_sample.reshape(bs, D_MODEL)
    gla_p, ret_p, hgrn_p, gla_s, ret_s, hgrn_s = [], [], [], [], [], []
    for i in range(DEPTH):
        lw = {k: v[i] for k, v in sw.items() if k not in _BIG_TABLES}
        lw.update(cos_s=cos_s, sin_s=sin_s, gamma=gamma)
        tables = (sw["u"], sw["vt"], i)

        og, sg = _prompt_mixer_call(
            _gla_prompt_kernel, xp, bp, lp, lw["w_gla"],
            [lw["bgate"], lw["gla_norm"], tri, masks],
            [lw["w_gla"].shape[1], GROUP_W], "gla_prompt")
        orr, sr = _prompt_mixer_call(
            _ret_prompt_kernel, xp, bp, lp, lw["w_ret"],
            [(cos_p, None), (sin_p, None), lw["ret_norm"], dmat, qdec, kdec, sdec],
            [4 * GROUP_W], "ret_prompt")
        oh, sh = _prompt_mixer_call(
            _hgrn_prompt_kernel, xp, bp, lp, lw["w_hgrn"],
            [lw["loglb"], lw["l1mlb"], lw["hgrn_norm"], tri, masks],
            [4 * GROUP_W], "hgrn_prompt")
        xp = _peer_call(_out_call(og, orr, oh, xp, lw), p_prompt[i].reshape(bp * lp, PLE_DIM), lw, tables)
        gla_p.append(sg)
        ret_p.append(sr)
        hgrn_p.append(sh)

        o_s, nsg, nsr, nsh = _sample_mixer_call(xs, lw, state_gla[i], state_ret[i], state_hgrn[i])
        xs = _peer_call(
            _out_call(o_s[:, :GROUP_W], o_s[:, GROUP_W:2 * GROUP_W], o_s[:, 2 * GROUP_W:], xs, lw),
            p_sample[i].reshape(bs, PLE_DIM), lw, tables)
        gla_s.append(nsg)
        ret_s.append(nsr)
        hgrn_s.append(nsh)

    return (xp.reshape(bp, lp, D_MODEL), xs.reshape(bs, 1, D_MODEL),
            _unslot_state(jnp.stack(gla_p), GLA_DK, GLA_DV),
            _unslot_state(jnp.stack(ret_p), RET_DK, RET_DV, rope=True),
            _unslot_state(jnp.stack(hgrn_p), HGRN_DK, HGRN_DV),
            jnp.stack(gla_s).reshape(DEPTH, bs, N_HEADS, GLA_DK, GLA_DV),
            jnp.stack(ret_s).reshape(DEPTH, bs, N_HEADS, RET_DK, RET_DV),
            jnp.stack(hgrn_s).reshape(DEPTH, bs, N_HEADS, HGRN_DK, HGRN_DV))
```

```python
import functools

import numpy as np
import jax
import jax.numpy as jnp
from jax import lax
from jax.experimental import pallas as pl
from jax.experimental.pallas import tpu as pltpu

F32 = jnp.float32
BF16 = jnp.bfloat16

D_MODEL = 1024
DEPTH = 2
PAST_LEN = 16384
N_HEADS = 4
GLA_DK, GLA_DV = 48, 96
RET_DK, RET_DV = 48, 96
HGRN_DK, HGRN_DV = 64, 64
GLA_LOWRANK = 16
GLA_TAU = 16.0
ROPE_BASE = 10000.0
CHUNK = 64
GLA_UNROLL = 4
HGRN_UNROLL = 8
RET_UNROLL = 8
PEER_HEADS = 8
N_KEYS = 128
PEER_TOPK = 16
PEER_TE = 2048
PLE_DIM = 256
ALPHA = (2 * DEPTH) ** 0.25
NORM_EPS = 1e-5

LANE = 128
SUBLANE = 8
SLOT = LANE
GROUP_W = N_HEADS * SLOT
VMEM_LIMIT = 56 * 1024 * 1024
NEG_INF = float("-inf")


def _dot(a, b):
    return jnp.dot(a, b, preferred_element_type=F32)


def _dot_nt(a, b):
    return lax.dot_general(a, b, (((1,), (1,)), ((), ())), preferred_element_type=F32)


def _dot_tn(a, b):
    return lax.dot_general(a, b, (((0,), (0,)), ((), ())), preferred_element_type=F32)


def _sigmoid(x):
    return jax.nn.sigmoid(x)


def _silu(x):
    return x * _sigmoid(x)


def _log_sigmoid(x):
    return jnp.minimum(x, 0.0) - jnp.log1p(jnp.exp(-jnp.abs(x)))


def _logaddexp(a, c):
    amax = jnp.maximum(a, c)
    delta = a - c
    return jnp.where(jnp.isnan(delta), a + c, amax + jnp.log1p(jnp.exp(-jnp.abs(delta))))


SQRT_HALF = np.float32(0.7071067811865476)


def _lane_mask(n):
    return (lax.broadcasted_iota(jnp.int32, (1, LANE), 1) < n).astype(F32)


def _rms_head_norm(o, g_row, dv):
    ms = jnp.sum(o * o, axis=-1, keepdims=True) * (1.0 / dv)
    return o * lax.rsqrt(ms + NORM_EPS) * g_row


def _group_head_norm(o, g_row, dv):
    mask = _lane_mask(dv)
    mu = jnp.sum(o, axis=-1, keepdims=True) * (1.0 / dv)
    d = (o - mu) * mask
    var = jnp.sum(d * d, axis=-1, keepdims=True) * (1.0 / dv)
    return d * lax.rsqrt(var + NORM_EPS) * g_row


def _chunk_constants():
    i = np.arange(CHUNK)[:, None]
    t = np.arange(CHUNK)[None, :]
    masks = [i == t]
    half = CHUNK // 2
    while half >= 1:
        blk = i // (2 * half)
        second = (i % (2 * half)) >= half
        masks.append(second & ((t % (2 * half)) < half) & (blk == t // (2 * half)))
        half //= 2
    return (t <= i).astype(np.float32), np.stack(masks).astype(np.float32)


def _cumsum_rows(tri_bf, g):
    hi = g.astype(BF16)
    r1 = g - hi.astype(F32)
    mid = r1.astype(BF16)
    lo = (r1 - mid.astype(F32)).astype(BF16)
    return _dot(tri_bf, hi) + _dot(tri_bf, mid) + _dot(tri_bf, lo)


def _level_factors(b, g):
    width = b.shape[1]
    grouped = (CHUNK // SUBLANE, SUBLANE, width)
    row = lax.broadcasted_iota(jnp.int32, b.shape, 0)
    sub = lax.broadcasted_iota(jnp.int32, grouped, 1)
    b3 = b.reshape(grouped)

    def sub_ref(r):
        return jnp.broadcast_to(b3[:, r:r + 1, :], grouped)

    out = []
    half = CHUNK // 2
    while half >= SUBLANE:
        ref = jnp.concatenate(
            [jnp.broadcast_to(b[m * 2 * half + half - 1:m * 2 * half + half], (2 * half, width))
             for m in range(CHUNK // (2 * half))], axis=0)
        out.append(jnp.exp(-jnp.abs(b - ref)))
        half //= 2
    out.append(jnp.exp(-jnp.abs(b3 - sub_ref(3))).reshape(b.shape))
    out.append(jnp.exp(-jnp.abs(b3 - jnp.where(sub < 4, sub_ref(1), sub_ref(5)))).reshape(b.shape))
    out.append(jnp.exp(jnp.where(row % 2 == 1, g, 0.0)))
    return out


def _vector_decay_chunk(q, k, v, g, st_ref, tri_bf, masks_ref):
    heads = [slice(h * SLOT, (h + 1) * SLOT) for h in range(N_HEADS)]
    b = _cumsum_rows(tri_bf, g)
    b_last = b[CHUNK - 1:CHUNK]
    factors = _level_factors(b, g)
    qs = [q.astype(BF16)] + [(q * f).astype(BF16) for f in factors]
    ks = [k.astype(BF16)] + [(k * f).astype(BF16) for f in factors]
    q_in = (q * jnp.exp(b)).astype(BF16)
    k_out = (k * jnp.exp(b_last - b)).astype(BF16)
    decay = jnp.exp(b_last)
    vb = v.astype(BF16)
    scores = []
    for sl in heads:
        sc = masks_ref[0] * _dot_nt(qs[0][:, sl], ks[0][:, sl])
        for l in range(len(factors)):
            sc = sc + masks_ref[1 + l] * _dot_nt(qs[1 + l][:, sl], ks[1 + l][:, sl])
        scores.append(sc.astype(BF16))
    outs = []
    for h, sl in enumerate(heads):
        st = st_ref[0, h]
        outs.append(_dot(scores[h], vb[:, sl]) + _dot_nt(q_in[:, sl], st.astype(BF16)))
        st_ref[0, h] = st * decay[:, sl] + _dot_tn(vb[:, sl], k_out[:, sl])
    return outs


def _gla_prompt_kernel(x_ref, w_ref, bg_ref, nrm_ref, tri_ref, masks_ref,
                       o_ref, st_ref, z_scr, la_scr, *, seg):
    @pl.when(pl.program_id(1) == 0)
    def _():
        st_ref[...] = jnp.zeros_like(st_ref)

    z_scr[...] = _dot(x_ref[...].astype(BF16), w_ref[...])
    pre = z_scr[:, 4 * GROUP_W:5 * GROUP_W] + bg_ref[...]
    la_scr[...] = _log_sigmoid(pre) * (1.0 / GLA_TAU)
    tri = tri_ref[...]

    def body(c, carry):
        r0 = pl.multiple_of(c * CHUNK, CHUNK)
        rows = pl.ds(r0, CHUNK)
        q = z_scr[rows, 0:GROUP_W] * (GLA_DK ** -0.5)
        k = z_scr[rows, GROUP_W:2 * GROUP_W]
        v = z_scr[rows, 2 * GROUP_W:3 * GROUP_W]
        outs = _vector_decay_chunk(q, k, v, la_scr[rows, :], st_ref, tri, masks_ref)
        for h in range(N_HEADS):
            gate = z_scr[rows, 3 * GROUP_W + h * SLOT:3 * GROUP_W + (h + 1) * SLOT]
            on = _rms_head_norm(outs[h], nrm_ref[h:h + 1, :], GLA_DV)
            o_ref[rows, h * SLOT:(h + 1) * SLOT] = (on * _silu(gate)).astype(BF16)
        return carry

    lax.fori_loop(0, seg // CHUNK, body, 0, unroll=GLA_UNROLL)


def _hgrn_prompt_kernel(x_ref, w_ref, loglb_ref, l1mlb_ref, nrm_ref, tri_ref, masks_ref,
                        o_ref, st_ref, z_scr, *, seg):
    @pl.when(pl.program_id(1) == 0)
    def _():
        st_ref[...] = jnp.zeros_like(st_ref)

    z_scr[...] = _dot(x_ref[...].astype(BF16), w_ref[...])
    tri = tri_ref[...]
    kmask = jnp.concatenate([_lane_mask(HGRN_DK)] * N_HEADS, axis=1)

    def body(c, carry):
        r0 = pl.multiple_of(c * CHUNK, CHUNK)
        rows = pl.ds(r0, CHUNK)
        q = _silu(z_scr[rows, 0:GROUP_W])
        v = z_scr[rows, 2 * GROUP_W:3 * GROUP_W]
        log_f = _logaddexp(loglb_ref[...], l1mlb_ref[...] + _log_sigmoid(z_scr[rows, GROUP_W:2 * GROUP_W]))
        k = (1.0 - jnp.exp(log_f)) * kmask
        outs = _vector_decay_chunk(q, k, v, log_f, st_ref, tri, masks_ref)
        for h in range(N_HEADS):
            sl = slice(h * SLOT, (h + 1) * SLOT)
            gate = z_scr[rows, 3 * GROUP_W + h * SLOT:3 * GROUP_W + (h + 1) * SLOT]
            on = _rms_head_norm(outs[h], nrm_ref[h:h + 1, :], HGRN_DV)
            o_ref[rows, sl] = (on * _silu(gate)).astype(BF16)
        return carry

    lax.fori_loop(0, seg // CHUNK, body, 0, unroll=HGRN_UNROLL)


def _rotate(t, cs, sn):
    return t * cs + pltpu.roll(t, SLOT // 2, 1) * sn


def _ret_prompt_kernel(x_ref, w_ref, cos_ref, sin_ref, nrm_ref, dmat_ref, qdec_ref, kdec_ref, sdec_ref,
                       o_ref, st_ref, z_scr, *, seg):
    @pl.when(pl.program_id(1) == 0)
    def _():
        st_ref[...] = jnp.zeros_like(st_ref)

    z_scr[...] = _dot(x_ref[...].astype(BF16), w_ref[...])

    def body(c, carry):
        r0 = pl.multiple_of(c * CHUNK, CHUNK)
        rows = pl.ds(r0, CHUNK)
        cs = cos_ref[rows, :]
        sn = sin_ref[rows, :]
        heads = [slice(h * SLOT, (h + 1) * SLOT) for h in range(N_HEADS)]
        qs = [_rotate(z_scr[rows, h * SLOT:(h + 1) * SLOT], cs, sn) for h in range(N_HEADS)]
        ks = [_rotate(z_scr[rows, GROUP_W + h * SLOT:GROUP_W + (h + 1) * SLOT], cs, sn) * (RET_DK ** -0.5)
              for h in range(N_HEADS)]
        vb = z_scr[rows, 2 * GROUP_W:3 * GROUP_W].astype(BF16)
        scores = [(dmat_ref[h] * _dot_nt(qs[h].astype(BF16), ks[h].astype(BF16))).astype(BF16)
                  for h in range(N_HEADS)]
        q_in = [(qs[h] * qdec_ref[h]).astype(BF16) for h in range(N_HEADS)]
        k_out = [(ks[h] * kdec_ref[h]).astype(BF16) for h in range(N_HEADS)]
        outs = []
        for h, sl in enumerate(heads):
            st = st_ref[0, h]
            outs.append(_dot(scores[h], vb[:, sl]) + _dot_nt(q_in[h], st.astype(BF16)))
            st_ref[0, h] = st * sdec_ref[h] + _dot_tn(vb[:, sl], k_out[h])
        for h, sl in enumerate(heads):
            gate = z_scr[rows, 3 * GROUP_W + h * SLOT:3 * GROUP_W + (h + 1) * SLOT]
            on = _group_head_norm(outs[h], nrm_ref[h:h + 1, :], RET_DV)
            o_ref[rows, sl] = (on * _silu(gate)).astype(BF16)
        return carry

    lax.fori_loop(0, seg // CHUNK, body, 0, unroll=RET_UNROLL)


def _const_spec(shape):
    nd = len(shape)
    return pl.BlockSpec(shape, lambda *_: (0,) * nd)


def _prompt_mixer_call(kernel, x2d, batch, seq, w, extras, scratch_widths, name):
    seg = min(512, seq)
    nseg = seq // seg
    in_specs = [pl.BlockSpec((seg, D_MODEL), lambda b, s: (b * nseg + s, 0)), _const_spec(w.shape)]
    args = [x2d, w]
    for e in extras:
        if isinstance(e, tuple):
            arr, _ = e
            in_specs.append(pl.BlockSpec((seg, arr.shape[1]), lambda b, s: (s, 0)))
            args.append(arr)
        else:
            in_specs.append(_const_spec(e.shape))
            args.append(e)
    return pl.pallas_call(
        functools.partial(kernel, seg=seg),
        grid=(batch, nseg),
        in_specs=in_specs,
        out_specs=[pl.BlockSpec((seg, GROUP_W), lambda b, s: (b * nseg + s, 0)),
                   pl.BlockSpec((1, N_HEADS, SLOT, SLOT), lambda b, s: (b, 0, 0, 0))],
        out_shape=[jax.ShapeDtypeStruct((batch * seq, GROUP_W), BF16),
                   jax.ShapeDtypeStruct((batch, N_HEADS, SLOT, SLOT), F32)],
        scratch_shapes=[pltpu.VMEM((seg, wd), F32) for wd in scratch_widths],
        compiler_params=pltpu.CompilerParams(dimension_semantics=("arbitrary", "arbitrary"),
                                             vmem_limit_bytes=VMEM_LIMIT),
        name=name,
    )(*args)


def _sample_step(q, k, eg, v, s_ref, ns_ref, tq, tk, te, tv, to, s_t, sn_t, dk, dv, row_of):
    tq[...] = q.T
    tk[...] = k.T
    te[...] = eg.T
    tv[...] = v.T
    s_t[0:dk * dv, :] = s_ref[...].T
    vt = tv[0:dv, :]

    def body(kk, oacc):
        kr = row_of(kk)
        r = pl.multiple_of(kk * dv, SUBLANE)
        sn = s_t[pl.ds(r, dv), :] * te[pl.ds(kr, 1), :] + tk[pl.ds(kr, 1), :] * vt
        sn_t[pl.ds(r, dv), :] = sn
        return oacc + tq[pl.ds(kr, 1), :] * sn

    o_t = lax.fori_loop(0, dk, body, jnp.zeros((dv, q.shape[0]), F32))
    ns_ref[...] = sn_t[0:dk * dv, :].T
    to[...] = jnp.zeros_like(to)
    to[0:dv, :] = o_t
    return to[...].T


def _sample_mixer_kernel(x_ref, wg_ref, wr_ref, wh_ref, bgate_ref, gn_ref, rn_ref, hn_ref,
                         cos_ref, sin_ref, gam_ref, loglb_ref, l1mlb_ref, sg_ref, sr_ref, sh_ref,
                         o_ref, nsg_ref, nsr_ref, nsh_ref,
                         zg, zr, zh, la, tq, tk, te, tv, to, s_t, sn_t):
    h = pl.program_id(0)

    @pl.when(h == 0)
    def _():
        xb = x_ref[...].astype(BF16)
        zg[...] = _dot(xb, wg_ref[...])
        zr[...] = _dot(xb, wr_ref[...])
        zh[...] = _dot(xb, wh_ref[...])
        pre = zg[:, 4 * GROUP_W:5 * GROUP_W] + bgate_ref[...]
        la[...] = _log_sigmoid(pre) * (1.0 / GLA_TAU)

    off = pl.multiple_of(h * SLOT, SLOT)
    sl = pl.ds(off, SLOT)
    tr = (tq, tk, te, tv, to, s_t, sn_t)
    batch = x_ref.shape[0]

    q = zg[:, sl] * (GLA_DK ** -0.5)
    k = zg[:, pl.ds(GROUP_W + off, SLOT)]
    v = zg[:, pl.ds(2 * GROUP_W + off, SLOT)]
    gate = zg[:, pl.ds(3 * GROUP_W + off, SLOT)]
    o = _sample_step(q, k, jnp.exp(la[:, sl]), v, sg_ref, nsg_ref, *tr, GLA_DK, GLA_DV, lambda kk: kk)
    o_ref[:, sl] = (_rms_head_norm(o, gn_ref[pl.ds(h, 1), :], GLA_DV) * _silu(gate)).astype(BF16)

    cs = cos_ref[...]
    sn = sin_ref[...]
    q = _rotate(zr[:, sl], cs, sn)
    k = _rotate(zr[:, pl.ds(GROUP_W + off, SLOT)], cs, sn) * (RET_DK ** -0.5)
    v = zr[:, pl.ds(2 * GROUP_W + off, SLOT)]
    gate = zr[:, pl.ds(3 * GROUP_W + off, SLOT)]
    eg = jnp.broadcast_to(gam_ref[h], (batch, SLOT))
    half = RET_DK // 2
    o = _sample_step(q, k, eg, v, sr_ref, nsr_ref, *tr, RET_DK, RET_DV,
                     lambda kk: kk + jnp.where(kk >= half, SLOT // 2 - half, 0))
    o_ref[:, pl.ds(GROUP_W + off, SLOT)] = (
        _group_head_norm(o, rn_ref[pl.ds(h, 1), :], RET_DV) * _silu(gate)).astype(BF16)

    q = _silu(zh[:, sl])
    hf = zh[:, pl.ds(GROUP_W + off, SLOT)]
    v = zh[:, pl.ds(2 * GROUP_W + off, SLOT)]
    gate = zh[:, pl.ds(3 * GROUP_W + off, SLOT)]
    log_f = _logaddexp(loglb_ref[:, sl], l1mlb_ref[:, sl] + _log_sigmoid(hf))
    f = jnp.exp(log_f)
    o = _sample_step(q, (1.0 - f) * _lane_mask(HGRN_DK), f, v, sh_ref, nsh_ref, *tr,
                     HGRN_DK, HGRN_DV, lambda kk: kk)
    o_ref[:, pl.ds(2 * GROUP_W + off, SLOT)] = (
        _rms_head_norm(o, hn_ref[pl.ds(h, 1), :], HGRN_DV) * _silu(gate)).astype(BF16)


def _sample_mixer_call(x2d, lw, sg, sr, sh):
    batch = x2d.shape[0]
    gsz, hsz = GLA_DK * GLA_DV, HGRN_DK * HGRN_DV
    consts = [lw["w_gla"], lw["w_ret"], lw["w_hgrn"], lw["bgate"], lw["gla_norm"], lw["ret_norm"],
              lw["hgrn_norm"], lw["cos_s"], lw["sin_s"], lw["gamma"], lw["loglb"], lw["l1mlb"]]
    in_specs = ([_const_spec(x2d.shape)] + [_const_spec(c.shape) for c in consts]
                + [pl.BlockSpec((batch, gsz), lambda h: (0, h)),
                   pl.BlockSpec((batch, gsz), lambda h: (0, h)),
                   pl.BlockSpec((batch, hsz), lambda h: (0, h))])
    out_specs = [_const_spec((batch, 3 * GROUP_W)),
                 pl.BlockSpec((batch, gsz), lambda h: (0, h)),
                 pl.BlockSpec((batch, gsz), lambda h: (0, h)),
                 pl.BlockSpec((batch, hsz), lambda h: (0, h))]
    out_shape = [jax.ShapeDtypeStruct((batch, 3 * GROUP_W), BF16),
                 jax.ShapeDtypeStruct((batch, N_HEADS * gsz), F32),
                 jax.ShapeDtypeStruct((batch, N_HEADS * gsz), F32),
                 jax.ShapeDtypeStruct((batch, N_HEADS * hsz), F32)]
    scratch = [pltpu.VMEM((batch, lw["w_gla"].shape[1]), F32), pltpu.VMEM((batch, 4 * GROUP_W), F32),
               pltpu.VMEM((batch, 4 * GROUP_W), F32), pltpu.VMEM((batch, GROUP_W), F32)]
    scratch += [pltpu.VMEM((SLOT, batch), F32) for _ in range(5)]
    scratch += [pltpu.VMEM((gsz, batch), F32), pltpu.VMEM((gsz, batch), F32)]
    return pl.pallas_call(
        _sample_mixer_kernel,
        grid=(N_HEADS,),
        in_specs=in_specs, out_specs=out_specs, out_shape=out_shape, scratch_shapes=scratch,
        compiler_params=pltpu.CompilerParams(dimension_semantics=("arbitrary",), vmem_limit_bytes=VMEM_LIMIT),
        name="sample_mixer",
    )(x2d, *consts, sg.reshape(batch, -1), sr.reshape(batch, -1), sh.reshape(batch, -1))


def _out_kernel(og_ref, or_ref, oh_ref, x_ref, wo_ref, g_ref, b_ref, xt_ref):
    mix = _dot(og_ref[...], wo_ref[0]) + _dot(or_ref[...], wo_ref[1]) + _dot(oh_ref[...], wo_ref[2])
    y = ALPHA * x_ref[...] + mix
    mu = jnp.mean(y, axis=-1, keepdims=True)
    d = y - mu
    var = jnp.mean(d * d, axis=-1, keepdims=True)
    xt_ref[...] = (d * lax.rsqrt(var + NORM_EPS) * g_ref[...] + b_ref[...]).T


def _out_call(og, orr, oh, x2d, lw):
    t = x2d.shape[0]
    tm = min(512, t)
    row = lambda i: (i, 0)
    return pl.pallas_call(
        _out_kernel,
        grid=(t // tm,),
        in_specs=[pl.BlockSpec((tm, GROUP_W), row), pl.BlockSpec((tm, GROUP_W), row),
                  pl.BlockSpec((tm, GROUP_W), row), pl.BlockSpec((tm, D_MODEL), row),
                  _const_spec(lw["w_out"].shape), _const_spec((1, D_MODEL)), _const_spec((1, D_MODEL))],
        out_specs=pl.BlockSpec((D_MODEL, tm), lambda i: (0, i)),
        out_shape=jax.ShapeDtypeStruct((D_MODEL, t), F32),
        compiler_params=pltpu.CompilerParams(dimension_semantics=("arbitrary",), vmem_limit_bytes=VMEM_LIMIT),
        name="out_proj_ln",
    )(og, orr, oh, x2d, lw["w_out"], lw["ln1_g"], lw["ln1_b"])


def _oddeven_merge(lo, hi, r):
    step = r * 2
    if step < hi - lo:
        yield from _oddeven_merge(lo, hi, step)
        yield from _oddeven_merge(lo + r, hi, step)
        yield from [(i, i + r) for i in range(lo + r, hi - r, step)]
    else:
        yield (lo, lo + r)


def _oddeven_merge_sort(lo, hi):
    if hi - lo >= 1:
        mid = lo + (hi - lo) // 2
        yield from _oddeven_merge_sort(lo, mid)
        yield from _oddeven_merge_sort(mid + 1, hi)
        yield from _oddeven_merge(lo, hi, 1)


_SORT16 = tuple(_oddeven_merge_sort(0, PEER_TOPK - 1))
_BITONIC16 = tuple((i, i + d) for d in (8, 4, 2, 1) for i in range(PEER_TOPK) if i & d == 0)


def _compare_exchange(xs, pairs):
    for i, j in pairs:
        for x in xs:
            x[i], x[j] = jnp.maximum(x[i], x[j]), jnp.minimum(x[i], x[j])
    return xs


def _top16_sorted(tiles):
    xs = _compare_exchange([[a3[i] for i in range(PEER_TOPK)] for a3 in tiles], _SORT16)
    for shift in (4, 2, 1):
        ys = [[pltpu.roll(v, shift, 0) for v in x] for x in xs]
        xs = _compare_exchange([[jnp.maximum(x[i], y[PEER_TOPK - 1 - i]) for i in range(PEER_TOPK)]
                                for x, y in zip(xs, ys)], _BITONIC16)
    return xs


def _sublane_block(rows):
    sub = lax.broadcasted_iota(jnp.int32, (SUBLANE, LANE), 0)
    blk = rows[0]
    for r in range(1, SUBLANE):
        blk = jnp.where(sub == r, rows[r], blk)
    return blk


def _route_tile(a1, a2):
    v1, v2 = _top16_sorted([a1, a2])
    sub = lax.broadcasted_iota(jnp.int32, (SUBLANE, LANE), 0)
    v2a = _sublane_block(v2[0:8])
    v2b = _sublane_block(v2[8:16])
    v1b = _sublane_block(v1[8:16])
    cands = [v1[0] + v2a, v1[0] + v2b]
    for r1 in range(1, 8):
        cands.append(jnp.where(sub < PEER_TOPK // (r1 + 1), v1[r1] + v2a, NEG_INF))
    cands.append(v1b + v2[0])
    filler = jnp.full((SUBLANE, LANE), NEG_INF, F32)
    cand_tile = jnp.stack(cands + [filler] * (PEER_TOPK - len(cands)))
    thr = _top16_sorted([cand_tile])[0][PEER_TOPK - 1]
    top = v1[0] + v2[0]
    z8 = jnp.zeros((SUBLANE, LANE), F32)
    for blk in cands:
        z8 = z8 + jnp.where(blk >= thr, jnp.exp(blk - top), 0.0)
    z = jnp.broadcast_to(jnp.sum(z8, axis=0, keepdims=True), (SUBLANE, LANE))
    n1 = jnp.zeros(a1.shape, F32)
    rank2 = jnp.zeros(a2.shape, F32)
    for r in range(PEER_TOPK):
        n1 = jnp.where(a1 + v2[r] >= thr, r + 1.0, n1)
        rank2 = jnp.where(v2[r] > a2, r + 1.0, rank2)
    e1 = jnp.exp(a1 - v1[0]) * SQRT_HALF
    e2 = jnp.exp(a2 - v2[0]) / z
    return n1, rank2, e1, e2


def _peer_kernel(xt_ref, wqt_ref, sk_ref, u_ref, vt_ref, p_ref, plet_ref, gatet_ref, g2_ref, b2_ref,
                 out_ref, xbf, s_scr, n1, e1, rk2, e2, h_scr, w_scr, acc, *, tn, te, nj):
    j = pl.program_id(1)
    lane_tiles = tn // LANE
    groups = (N_KEYS // SUBLANE, SUBLANE, LANE)

    def lane_slice(lt):
        return pl.ds(pl.multiple_of(lt * LANE, LANE), LANE)

    @pl.when(j == 0)
    def _route():
        xbf[...] = xt_ref[...].astype(BF16)

        def head_body(h, carry):
            r = pl.multiple_of(h * 2 * N_KEYS, 2 * N_KEYS)
            qh = _dot(wqt_ref[pl.ds(r, 2 * N_KEYS), :], xbf[...]).astype(BF16)
            s_scr[0] = _dot(sk_ref[h, 0], qh[0:N_KEYS])
            s_scr[1] = _dot(sk_ref[h, 1], qh[N_KEYS:2 * N_KEYS])

            def lane_body(lt, c):
                lanes = lane_slice(lt)
                n1_t, rank2_t, e1_t, e2_t = _route_tile(s_scr[0, :, lanes].reshape(groups),
                                                        s_scr[1, :, lanes].reshape(groups))
                n1[lt, h] = n1_t.reshape(N_KEYS, LANE)
                e1[lt, h] = e1_t.reshape(N_KEYS, LANE)
                rk2[lt, h] = rank2_t.reshape(N_KEYS, LANE).astype(BF16).reshape(rk2.shape[2:])
                e2[lt, h] = e2_t.reshape(N_KEYS, LANE).astype(BF16).reshape(e2.shape[2:])
                return c

            lax.fori_loop(0, lane_tiles, lane_body, 0)
            return carry

        lax.fori_loop(0, PEER_HEADS, head_body, 0)
        acc[...] = jnp.zeros_like(acc)

    na = te // N_KEYS
    assert na % SUBLANE == 0
    packed = rk2.shape[2:]
    row16 = (1, 2 * SUBLANE, LANE)

    hval = _dot(u_ref[...], xbf[...])
    for l2 in range(lane_tiles):
        h_scr[l2] = hval[:, l2 * LANE:(l2 + 1) * LANE].astype(BF16)

    def gate_body(p, carry):
        for grp in range(na // SUBLANE):
            a0 = pl.multiple_of(j * na + grp * SUBLANE, SUBLANE)
            n1blk = [n1[p, h, pl.ds(a0, SUBLANE), :] for h in range(PEER_HEADS)]
            e1blk = [e1[p, h, pl.ds(a0, SUBLANE), :] for h in range(PEER_HEADS)]
            for ai in range(0, SUBLANE, 2):
                gates = [jnp.zeros(packed, BF16), jnp.zeros(packed, BF16)]
                for h in range(PEER_HEADS):
                    rk2h = rk2[p, h]
                    e2h = e2[p, h]
                    for d in range(2):
                        n1a = jnp.broadcast_to(n1blk[h][ai + d:ai + d + 1, :], row16[1:]).astype(BF16).reshape(row16)
                        e1a = jnp.broadcast_to(e1blk[h][ai + d:ai + d + 1, :], row16[1:]).astype(BF16).reshape(row16)
                        gates[d] = gates[d] + jnp.where(rk2h < n1a, e2h, jnp.zeros_like(e2h)) * e1a
                for d in range(2):
                    r0 = (grp * SUBLANE + ai + d) * N_KEYS
                    hs = h_scr[p, r0:r0 + N_KEYS, :].reshape(packed)
                    w_scr[p, r0:r0 + N_KEYS, :] = (gates[d] * (hs + hs * lax.erf(hs))).reshape(N_KEYS, LANE)
        return carry

    lax.fori_loop(0, lane_tiles, gate_body, 0)
    acc[...] += _dot(vt_ref[...], jnp.concatenate([w_scr[l2] for l2 in range(lane_tiles)], axis=1))

    @pl.when(j == nj - 1)
    def _finish():
        def norm_body(lt, c):
            lanes = lane_slice(lt)
            y = ALPHA * xt_ref[:, lanes] + acc[:, lanes]
            mu = jnp.mean(y, axis=0, keepdims=True)
            d = y - mu
            var = jnp.mean(d * d, axis=0, keepdims=True)
            yn = d * lax.rsqrt(var + NORM_EPS) * g2_ref[...] + b2_ref[...]
            acc[:, lanes] = yn
            xbf[:, lanes] = yn.astype(BF16)
            return c

        lax.fori_loop(0, lane_tiles, norm_body, 0)
        emb = _dot_nt(plet_ref[...], p_ref[...].astype(BF16))
        gt = _dot(gatet_ref[...], xbf[...])
        out_ref[...] = (acc[...] + emb * _sigmoid(gt)).T


def _peer_call(xt, p2d, lw, tables):
    u_all, vt_all, layer = tables
    t = xt.shape[1]
    tn = min(512, t)
    te = PEER_TE
    nj = vt_all.shape[1]
    lane_tiles = tn // LANE
    once = dict(pipeline_mode=pl.Buffered(1))
    in_specs = [
        pl.BlockSpec((D_MODEL, tn), lambda i, j: (0, i)),
        pl.BlockSpec(lw["wqt"].shape, lambda i, j: (0, 0), **once),
        pl.BlockSpec(lw["subkeys"].shape, lambda i, j: (0, 0, 0, 0), **once),
        pl.BlockSpec((None, te, D_MODEL), lambda i, j: (layer, j, 0)),
        pl.BlockSpec((None, None, D_MODEL, te), lambda i, j: (layer, j, 0, 0)),
        pl.BlockSpec((tn, PLE_DIM), lambda i, j: (i, 0)),
        pl.BlockSpec(lw["plet"].shape, lambda i, j: (0, 0), **once),
        pl.BlockSpec(lw["gatet"].shape, lambda i, j: (0, 0), **once),
        pl.BlockSpec((D_MODEL, LANE), lambda i, j: (0, 0), **once),
        pl.BlockSpec((D_MODEL, LANE), lambda i, j: (0, 0), **once),
    ]
    route = (lane_tiles, PEER_HEADS, N_KEYS, LANE)
    route_packed = (lane_tiles, PEER_HEADS, N_KEYS // (2 * SUBLANE), 2 * SUBLANE, LANE)
    scratch = [pltpu.VMEM((D_MODEL, tn), BF16), pltpu.VMEM((2, N_KEYS, tn), F32),
               pltpu.VMEM(route, F32), pltpu.VMEM(route, F32),
               pltpu.VMEM(route_packed, BF16), pltpu.VMEM(route_packed, BF16),
               pltpu.VMEM((lane_tiles, te, LANE), BF16), pltpu.VMEM((lane_tiles, te, LANE), BF16),
               pltpu.VMEM((D_MODEL, tn), F32)]
    return pl.pallas_call(
        functools.partial(_peer_kernel, tn=tn, te=te, nj=nj),
        grid=(t // tn, nj),
        in_specs=in_specs,
        out_specs=pl.BlockSpec((tn, D_MODEL), lambda i, j: (i, 0)),
        out_shape=jax.ShapeDtypeStruct((t, D_MODEL), F32),
        scratch_shapes=scratch,
        compiler_params=pltpu.CompilerParams(dimension_semantics=("arbitrary", "arbitrary"),
                                             vmem_limit_bytes=VMEM_LIMIT),
        name="peer_ffn_ln_ple",
    )(xt, lw["wqt"], lw["subkeys"], u_all, vt_all, p2d, lw["plet"], lw["gatet"], lw["ln2_g"], lw["ln2_b"])


def _head_slots(w, d):
    lead = w.shape[:-1]
    pad = [(0, 0)] * (len(lead) + 1) + [(0, SLOT - d)]
    return jnp.pad(w.reshape(*lead, N_HEADS, d), pad).reshape(*lead, GROUP_W)


def _rope_slots(w, d):
    lead = w.shape[:-1]
    half = d // 2
    pad = [(0, 0)] * (len(lead) + 2) + [(0, SLOT // 2 - half)]
    return jnp.pad(w.reshape(*lead, N_HEADS, 2, half), pad).reshape(*lead, GROUP_W)


def _row_slots(w, d):
    depth, _, cols = w.shape
    w = jnp.pad(w.reshape(depth, N_HEADS, d, cols), ((0, 0), (0, 0), (0, SLOT - d), (0, 0)))
    return w.reshape(depth, GROUP_W, cols)


def _rope_tables(pos):
    half = RET_DK // 2
    inv = 1.0 / (ROPE_BASE ** (jnp.arange(0, RET_DK, 2, dtype=F32) / RET_DK))
    ang = pos[:, None] * inv[None, :]
    pad = ((0, 0), (0, SLOT // 2 - half))
    cos = jnp.pad(jnp.cos(ang), pad)
    sin = jnp.pad(jnp.sin(ang), pad)
    return jnp.concatenate([cos, cos], axis=1), jnp.concatenate([-sin, sin], axis=1)


def _retention_constants():
    log_gamma = jnp.log1p(-jnp.exp2(-5.0 - jnp.arange(N_HEADS, dtype=F32)))
    i = jnp.arange(CHUNK, dtype=F32)
    diff = i[:, None] - i[None, :]
    lg = log_gamma[:, None, None]
    dmat = jnp.where(diff >= 0, jnp.exp(jnp.where(diff >= 0, diff, 0.0) * lg), 0.0)
    qdec = jnp.broadcast_to(jnp.exp((i[None, :, None] + 1.0) * lg), (N_HEADS, CHUNK, SLOT))
    kdec = jnp.broadcast_to(jnp.exp((CHUNK - 1.0 - i[None, :, None]) * lg), (N_HEADS, CHUNK, SLOT))
    sdec = jnp.broadcast_to(jnp.exp(CHUNK * lg), (N_HEADS, 1, SLOT))
    gamma = jnp.broadcast_to(jnp.exp(lg), (N_HEADS, 1, SLOT))
    return dmat, qdec, kdec, sdec, gamma


def _stacked_weights(lb, w_in, gla_w_gate, gla_b_gate, gla_norm, ret_norm, hgrn_norm, w_out,
                     ln1_g, ln1_b, ln2_g, ln2_b, peer_w_q, peer_subkeys, peer_u, peer_v, ple_proj, ple_gate):
    depth = w_in.shape[0]
    sizes = (N_HEADS * GLA_DK, N_HEADS * GLA_DK, N_HEADS * GLA_DV, N_HEADS * GLA_DV, GLA_LOWRANK,
             N_HEADS * RET_DK, N_HEADS * RET_DK, N_HEADS * RET_DV, N_HEADS * RET_DV,
             N_HEADS * HGRN_DK, N_HEADS * HGRN_DK, N_HEADS * HGRN_DV, N_HEADS * HGRN_DV)
    offs = [int(c) for c in np.cumsum(sizes)[:-1]]
    gq, gk, gv, gg, glr, rq, rk, rv, rg, hq, hf, hi, hg = jnp.split(w_in, offs, axis=2)
    sw = {}
    gate_w = jnp.einsum("dir,drk->dik", glr, gla_w_gate, precision=lax.Precision.HIGHEST)
    sw["w_gla"] = jnp.concatenate(
        [_head_slots(gq, GLA_DK), _head_slots(gk, GLA_DK), _head_slots(gv, GLA_DV), _head_slots(gg, GLA_DV),
         _head_slots(gate_w, GLA_DK)], axis=2).astype(BF16)
    sw["w_ret"] = jnp.concatenate(
        [_rope_slots(rq, RET_DK), _rope_slots(rk, RET_DK), _head_slots(rv, RET_DV), _head_slots(rg, RET_DV)],
        axis=2).astype(BF16)
    sw["w_hgrn"] = jnp.concatenate([_head_slots(w, HGRN_DK) for w in (hq, hf, hi, hg)], axis=2).astype(BF16)
    sw["bgate"] = _head_slots(gla_b_gate, GLA_DK)[:, None, :]
    sw["gla_norm"] = jnp.pad(gla_norm, ((0, 0), (0, 0), (0, SLOT - GLA_DV)))
    sw["ret_norm"] = jnp.pad(ret_norm, ((0, 0), (0, 0), (0, SLOT - RET_DV)))
    sw["hgrn_norm"] = jnp.pad(hgrn_norm, ((0, 0), (0, 0), (0, SLOT - HGRN_DV)))
    lbh = lb.reshape(depth, 1, N_HEADS, HGRN_DK)
    pad = ((0, 0), (0, 0), (0, 0), (0, SLOT - HGRN_DK))
    sw["loglb"] = jnp.pad(jnp.log(lbh), pad, constant_values=-1.0).reshape(depth, 1, GROUP_W)
    sw["l1mlb"] = jnp.pad(jnp.log1p(-lbh), pad, constant_values=-1.0).reshape(depth, 1, GROUP_W)
    g_rows, r_rows = N_HEADS * GLA_DV, N_HEADS * RET_DV
    sw["w_out"] = jnp.stack([
        _row_slots(w_out[:, :g_rows], GLA_DV), _row_slots(w_out[:, g_rows:g_rows + r_rows], RET_DV),
        _row_slots(w_out[:, g_rows + r_rows:], HGRN_DV)], axis=1).astype(BF16)
    sw["ln1_g"], sw["ln1_b"] = ln1_g[:, None, :], ln1_b[:, None, :]
    sw["ln2_g"] = jnp.broadcast_to(ln2_g[:, :, None], (depth, D_MODEL, LANE))
    sw["ln2_b"] = jnp.broadcast_to(ln2_b[:, :, None], (depth, D_MODEL, LANE))
    sw["wqt"] = jnp.swapaxes(peer_w_q, 1, 2).astype(BF16)
    sw["subkeys"] = peer_subkeys.astype(BF16)
    sw["u"] = (peer_u * SQRT_HALF).astype(BF16)
    sw["vt"] = jnp.swapaxes(peer_v.astype(BF16).reshape(depth, -1, PEER_TE, D_MODEL), 2, 3)
    sw["plet"] = jnp.swapaxes(ple_proj, 1, 2).astype(BF16)
    sw["gatet"] = jnp.swapaxes(ple_gate, 1, 2).astype(BF16)
    return sw


_BIG_TABLES = ("u", "vt")


def _unslot_state(st, dk, dv, rope=False):
    if rope:
        half = dk // 2
        st = jnp.concatenate([st[..., :half], st[..., SLOT // 2:SLOT // 2 + half]], axis=-1)
    return jnp.swapaxes(st[..., :dv, :dk], -2, -1)


def kernel(x_prompt, x_sample, p_prompt, p_sample, state_gla, state_ret, state_hgrn, w_in, gla_w_gate,
           gla_b_gate, gla_norm, ret_norm, hgrn_lb_logits, hgrn_norm, w_out, ln1_g, ln1_b, ln2_g, ln2_b,
           peer_w_q, peer_subkeys, peer_u, peer_v, ple_proj, ple_gate):
    bp, lp, _ = x_prompt.shape
    bs = x_sample.shape[0]
    assert x_sample.shape[1] == 1 and lp % CHUNK == 0

    lb = jnp.cumsum(jax.nn.softmax(hgrn_lb_logits.astype(F32), axis=0), axis=0)
    lb = lb - lb[0:1]
    tri_np, masks_np = _chunk_constants()
    tri, masks = jnp.asarray(tri_np, BF16), jnp.asarray(masks_np)
    dmat, qdec, kdec, sdec, gamma = _retention_constants()
    cos_p, sin_p = _rope_tables(jnp.arange(lp, dtype=F32))
    cos_s, sin_s = _rope_tables(PAST_LEN + jnp.arange(1, dtype=F32))
    sw = _stacked_weights(lb, w_in, gla_w_gate, gla_b_gate, gla_norm, ret_norm, hgrn_norm, w_out,
                          ln1_g, ln1_b, ln2_g, ln2_b, peer_w_q, peer_subkeys, peer_u, peer_v,
                          ple_proj, ple_gate)

    xp = x_prompt.reshape(bp * lp, D_MODEL)
    xs = x_sample.reshape(bs, D_MODEL)
    gla_p, ret_p, hgrn_p, gla_s, ret_s, hgrn_s = [], [], [], [], [], []
    for i in range(DEPTH):
        lw = {k: v[i] for k, v in sw.items() if k not in _BIG_TABLES}
        lw.update(cos_s=cos_s, sin_s=sin_s, gamma=gamma)
        tables = (sw["u"], sw["vt"], i)

        og, sg = _prompt_mixer_call(
            _gla_prompt_kernel, xp, bp, lp, lw["w_gla"],
            [lw["bgate"], lw["gla_norm"], tri, masks],
            [lw["w_gla"].shape[1], GROUP_W], "gla_prompt")
        orr, sr = _prompt_mixer_call(
            _ret_prompt_kernel, xp, bp, lp, lw["w_ret"],
            [(cos_p, None), (sin_p, None), lw["ret_norm"], dmat, qdec, kdec, sdec],
            [4 * GROUP_W], "ret_prompt")
        oh, sh = _prompt_mixer_call(
            _hgrn_prompt_kernel, xp, bp, lp, lw["w_hgrn"],
            [lw["loglb"], lw["l1mlb"], lw["hgrn_norm"], tri, masks],
            [4 * GROUP_W], "hgrn_prompt")
        xp = _peer_call(_out_call(og, orr, oh, xp, lw), p_prompt[i].reshape(bp * lp, PLE_DIM), lw, tables)
        gla_p.append(sg)
        ret_p.append(sr)
        hgrn_p.append(sh)

        o_s, nsg, nsr, nsh = _sample_mixer_call(xs, lw, state_gla[i], state_ret[i], state_hgrn[i])
        xs = _peer_call(
            _out_call(o_s[:, :GROUP_W], o_s[:, GROUP_W:2 * GROUP_W], o_s[:, 2 * GROUP_W:], xs, lw),
            p_sample[i].reshape(bs, PLE_DIM), lw, tables)
        gla_s.append(nsg)
        ret_s.append(nsr)
        hgrn_s.append(nsh)

    return (xp.reshape(bp, lp, D_MODEL), xs.reshape(bs, 1, D_MODEL),
            _unslot_state(jnp.stack(gla_p), GLA_DK, GLA_DV),
            _unslot_state(jnp.stack(ret_p), RET_DK, RET_DV, rope=True),
            _unslot_state(jnp.stack(hgrn_p), HGRN_DK, HGRN_DV),
            jnp.stack(gla_s).reshape(DEPTH, bs, N_HEADS, GLA_DK, GLA_DV),
            jnp.stack(ret_s).reshape(DEPTH, bs, N_HEADS, RET_DK, RET_DV),
            jnp.stack(hgrn_s).reshape(DEPTH, bs, N_HEADS, HGRN_DK, HGRN_DV))
```

```python
import functools

import numpy as np
import jax
import jax.numpy as jnp
from jax import lax
from jax.experimental import pallas as pl
from jax.experimental.pallas import tpu as pltpu

F32 = jnp.float32
BF16 = jnp.bfloat16

D_MODEL = 1024
DEPTH = 2
PAST_LEN = 16384
N_HEADS = 4
GLA_DK, GLA_DV = 48, 96
RET_DK, RET_DV = 48, 96
HGRN_DK, HGRN_DV = 64, 64
GLA_LOWRANK = 16
GLA_TAU = 16.0
ROPE_BASE = 10000.0
CHUNK = 64
GLA_UNROLL = 4
HGRN_UNROLL = 8
RET_UNROLL = 8
PEER_HEADS = 8
N_KEYS = 128
PEER_TOPK = 16
PEER_TE = 2048
PLE_DIM = 256
ALPHA = (2 * DEPTH) ** 0.25
NORM_EPS = 1e-5

LANE = 128
SUBLANE = 8
SLOT = LANE
GROUP_W = N_HEADS * SLOT
VMEM_LIMIT = 56 * 1024 * 1024
NEG_INF = float("-inf")


def _dot(a, b):
    return jnp.dot(a, b, preferred_element_type=F32)


def _dot_nt(a, b):
    return lax.dot_general(a, b, (((1,), (1,)), ((), ())), preferred_element_type=F32)


def _dot_tn(a, b):
    return lax.dot_general(a, b, (((0,), (0,)), ((), ())), preferred_element_type=F32)


def _sigmoid(x):
    return jax.nn.sigmoid(x)


def _silu(x):
    return x * _sigmoid(x)


def _log_sigmoid(x):
    return jnp.minimum(x, 0.0) - jnp.log1p(jnp.exp(-jnp.abs(x)))


def _logaddexp(a, c):
    amax = jnp.maximum(a, c)
    delta = a - c
    return jnp.where(jnp.isnan(delta), a + c, amax + jnp.log1p(jnp.exp(-jnp.abs(delta))))


SQRT_HALF = np.float32(0.7071067811865476)


def _lane_mask(n):
    return (lax.broadcasted_iota(jnp.int32, (1, LANE), 1) < n).astype(F32)


def _rms_head_norm(o, g_row, dv):
    ms = jnp.sum(o * o, axis=-1, keepdims=True) * (1.0 / dv)
    return o * lax.rsqrt(ms + NORM_EPS) * g_row


def _group_head_norm(o, g_row, dv):
    mask = _lane_mask(dv)
    mu = jnp.sum(o, axis=-1, keepdims=True) * (1.0 / dv)
    d = (o - mu) * mask
    var = jnp.sum(d * d, axis=-1, keepdims=True) * (1.0 / dv)
    return d * lax.rsqrt(var + NORM_EPS) * g_row


def _chunk_constants():
    i = np.arange(CHUNK)[:, None]
    t = np.arange(CHUNK)[None, :]
    masks = [i == t]
    half = CHUNK // 2
    while half >= 1:
        blk = i // (2 * half)
        second = (i % (2 * half)) >= half
        masks.append(second & ((t % (2 * half)) < half) & (blk == t // (2 * half)))
        half //= 2
    return (t <= i).astype(np.float32), np.stack(masks).astype(np.float32)


def _cumsum_rows(tri_bf, g):
    hi = g.astype(BF16)
    r1 = g - hi.astype(F32)
    mid = r1.astype(BF16)
    lo = (r1 - mid.astype(F32)).astype(BF16)
    return _dot(tri_bf, hi) + _dot(tri_bf, mid) + _dot(tri_bf, lo)


def _level_factors(b, g):
    width = b.shape[1]
    grouped = (CHUNK // SUBLANE, SUBLANE, width)
    row = lax.broadcasted_iota(jnp.int32, b.shape, 0)
    sub = lax.broadcasted_iota(jnp.int32, grouped, 1)
    b3 = b.reshape(grouped)

    def sub_ref(r):
        return jnp.broadcast_to(b3[:, r:r + 1, :], grouped)

    out = []
    half = CHUNK // 2
    while half >= SUBLANE:
        ref = jnp.concatenate(
            [jnp.broadcast_to(b[m * 2 * half + half - 1:m * 2 * half + half], (2 * half, width))
             for m in range(CHUNK // (2 * half))], axis=0)
        out.append(jnp.exp(-jnp.abs(b - ref)))
        half //= 2
    out.append(jnp.exp(-jnp.abs(b3 - sub_ref(3))).reshape(b.shape))
    out.append(jnp.exp(-jnp.abs(b3 - jnp.where(sub < 4, sub_ref(1), sub_ref(5)))).reshape(b.shape))
    out.append(jnp.exp(jnp.where(row % 2 == 1, g, 0.0)))
    return out


def _vector_decay_chunk(q, k, v, g, st_ref, tri_bf, masks_ref):
    heads = [slice(h * SLOT, (h + 1) * SLOT) for h in range(N_HEADS)]
    b = _cumsum_rows(tri_bf, g)
    b_last = b[CHUNK - 1:CHUNK]
    factors = _level_factors(b, g)
    qs = [q.astype(BF16)] + [(q * f).astype(BF16) for f in factors]
    ks = [k.astype(BF16)] + [(k * f).astype(BF16) for f in factors]
    q_in = (q * jnp.exp(b)).astype(BF16)
    k_out = (k * jnp.exp(b_last - b)).astype(BF16)
    decay = jnp.exp(b_last)
    vb = v.astype(BF16)
    scores = []
    for sl in heads:
        sc = masks_ref[0] * _dot_nt(qs[0][:, sl], ks[0][:, sl])
        for l in range(len(factors)):
            sc = sc + masks_ref[1 + l] * _dot_nt(qs[1 + l][:, sl], ks[1 + l][:, sl])
        scores.append(sc.astype(BF16))
    outs = []
    for h, sl in enumerate(heads):
        st = st_ref[0, h]
        outs.append(_dot(scores[h], vb[:, sl]) + _dot_nt(q_in[:, sl], st.astype(BF16)))
        st_ref[0, h] = st * decay[:, sl] + _dot_tn(vb[:, sl], k_out[:, sl])
    return outs


def _gla_prompt_kernel(x_ref, w_ref, bg_ref, nrm_ref, tri_ref, masks_ref,
                       o_ref, st_ref, z_scr, la_scr, *, seg):
    @pl.when(pl.program_id(1) == 0)
    def _():
        st_ref[...] = jnp.zeros_like(st_ref)

    z_scr[...] = _dot(x_ref[...].astype(BF16), w_ref[...])
    pre = z_scr[:, 4 * GROUP_W:5 * GROUP_W] + bg_ref[...]
    la_scr[...] = _log_sigmoid(pre) * (1.0 / GLA_TAU)
    tri = tri_ref[...]

    def body(c, carry):
        r0 = pl.multiple_of(c * CHUNK, CHUNK)
        rows = pl.ds(r0, CHUNK)
        q = z_scr[rows, 0:GROUP_W] * (GLA_DK ** -0.5)
        k = z_scr[rows, GROUP_W:2 * GROUP_W]
        v = z_scr[rows, 2 * GROUP_W:3 * GROUP_W]
        outs = _vector_decay_chunk(q, k, v, la_scr[rows, :], st_ref, tri, masks_ref)
        for h in range(N_HEADS):
            gate = z_scr[rows, 3 * GROUP_W + h * SLOT:3 * GROUP_W + (h + 1) * SLOT]
            on = _rms_head_norm(outs[h], nrm_ref[h:h + 1, :], GLA_DV)
            o_ref[rows, h * SLOT:(h + 1) * SLOT] = (on * _silu(gate)).astype(BF16)
        return carry

    lax.fori_loop(0, seg // CHUNK, body, 0, unroll=GLA_UNROLL)


def _hgrn_prompt_kernel(x_ref, w_ref, loglb_ref, l1mlb_ref, nrm_ref, tri_ref, masks_ref,
                        o_ref, st_ref, z_scr, *, seg):
    @pl.when(pl.program_id(1) == 0)
    def _():
        st_ref[...] = jnp.zeros_like(st_ref)

    z_scr[...] = _dot(x_ref[...].astype(BF16), w_ref[...])
    tri = tri_ref[...]
    kmask = jnp.concatenate([_lane_mask(HGRN_DK)] * N_HEADS, axis=1)

    def body(c, carry):
        r0 = pl.multiple_of(c * CHUNK, CHUNK)
        rows = pl.ds(r0, CHUNK)
        q = _silu(z_scr[rows, 0:GROUP_W])
        v = z_scr[rows, 2 * GROUP_W:3 * GROUP_W]
        log_f = _logaddexp(loglb_ref[...], l1mlb_ref[...] + _log_sigmoid(z_scr[rows, GROUP_W:2 * GROUP_W]))
        k = (1.0 - jnp.exp(log_f)) * kmask
        outs = _vector_decay_chunk(q, k, v, log_f, st_ref, tri, masks_ref)
        for h in range(N_HEADS):
            sl = slice(h * SLOT, (h + 1) * SLOT)
            gate = z_scr[rows, 3 * GROUP_W + h * SLOT:3 * GROUP_W + (h + 1) * SLOT]
            on = _rms_head_norm(outs[h], nrm_ref[h:h + 1, :], HGRN_DV)
            o_ref[rows, sl] = (on * _silu(gate)).astype(BF16)
        return carry

    lax.fori_loop(0, seg // CHUNK, body, 0, unroll=HGRN_UNROLL)


def _rotate(t, cs, sn):
    return t * cs + pltpu.roll(t, SLOT // 2, 1) * sn


def _ret_prompt_kernel(x_ref, w_ref, cos_ref, sin_ref, nrm_ref, dmat_ref, qdec_ref, kdec_ref, sdec_ref,
                       o_ref, st_ref, z_scr, *, seg):
    @pl.when(pl.program_id(1) == 0)
    def _():
        st_ref[...] = jnp.zeros_like(st_ref)

    z_scr[...] = _dot(x_ref[...].astype(BF16), w_ref[...])

    def body(c, carry):
        r0 = pl.multiple_of(c * CHUNK, CHUNK)
        rows = pl.ds(r0, CHUNK)
        cs = cos_ref[rows, :]
        sn = sin_ref[rows, :]
        heads = [slice(h * SLOT, (h + 1) * SLOT) for h in range(N_HEADS)]
        qs = [_rotate(z_scr[rows, h * SLOT:(h + 1) * SLOT], cs, sn) for h in range(N_HEADS)]
        ks = [_rotate(z_scr[rows, GROUP_W + h * SLOT:GROUP_W + (h + 1) * SLOT], cs, sn) * (RET_DK ** -0.5)
              for h in range(N_HEADS)]
        vb = z_scr[rows, 2 * GROUP_W:3 * GROUP_W].astype(BF16)
        scores = [(dmat_ref[h] * _dot_nt(qs[h].astype(BF16), ks[h].astype(BF16))).astype(BF16)
                  for h in range(N_HEADS)]
        q_in = [(qs[h] * qdec_ref[h]).astype(BF16) for h in range(N_HEADS)]
        k_out = [(ks[h] * kdec_ref[h]).astype(BF16) for h in range(N_HEADS)]
        outs = []
        for h, sl in enumerate(heads):
            st = st_ref[0, h]
            outs.append(_dot(scores[h], vb[:, sl]) + _dot_nt(q_in[h], st.astype(BF16)))
            st_ref[0, h] = st * sdec_ref[h] + _dot_tn(vb[:, sl], k_out[h])
        for h, sl in enumerate(heads):
            gate = z_scr[rows, 3 * GROUP_W + h * SLOT:3 * GROUP_W + (h + 1) * SLOT]
            on = _group_head_norm(outs[h], nrm_ref[h:h + 1, :], RET_DV)
            o_ref[rows, sl] = (on * _silu(gate)).astype(BF16)
        return carry

    lax.fori_loop(0, seg // CHUNK, body, 0, unroll=RET_UNROLL)


def _const_spec(shape):
    nd = len(shape)
    return pl.BlockSpec(shape, lambda *_: (0,) * nd)


def _prompt_mixer_call(kernel, x2d, batch, seq, w, extras, scratch_widths, name):
    seg = min(512, seq)
    nseg = seq // seg
    in_specs = [pl.BlockSpec((seg, D_MODEL), lambda b, s: (b * nseg + s, 0)), _const_spec(w.shape)]
    args = [x2d, w]
    for e in extras:
        if isinstance(e, tuple):
            arr, _ = e
            in_specs.append(pl.BlockSpec((seg, arr.shape[1]), lambda b, s: (s, 0)))
            args.append(arr)
        else:
            in_specs.append(_const_spec(e.shape))
            args.append(e)
    return pl.pallas_call(
        functools.partial(kernel, seg=seg),
        grid=(batch, nseg),
        in_specs=in_specs,
        out_specs=[pl.BlockSpec((seg, GROUP_W), lambda b, s: (b * nseg + s, 0)),
                   pl.BlockSpec((1, N_HEADS, SLOT, SLOT), lambda b, s: (b, 0, 0, 0))],
        out_shape=[jax.ShapeDtypeStruct((batch * seq, GROUP_W), BF16),
                   jax.ShapeDtypeStruct((batch, N_HEADS, SLOT, SLOT), F32)],
        scratch_shapes=[pltpu.VMEM((seg, wd), F32) for wd in scratch_widths],
        compiler_params=pltpu.CompilerParams(dimension_semantics=("arbitrary", "arbitrary"),
                                             vmem_limit_bytes=VMEM_LIMIT),
        name=name,
    )(*args)


def _sample_step(q, k, eg, v, s_ref, ns_ref, tq, tk, te, tv, to, s_t, sn_t, dk, dv, row_of):
    tq[...] = q.T
    tk[...] = k.T
    te[...] = eg.T
    tv[...] = v.T
    s_t[0:dk * dv, :] = s_ref[...].T
    vt = tv[0:dv, :]

    def body(kk, oacc):
        kr = row_of(kk)
        r = pl.multiple_of(kk * dv, SUBLANE)
        sn = s_t[pl.ds(r, dv), :] * te[pl.ds(kr, 1), :] + tk[pl.ds(kr, 1), :] * vt
        sn_t[pl.ds(r, dv), :] = sn
        return oacc + tq[pl.ds(kr, 1), :] * sn

    o_t = lax.fori_loop(0, dk, body, jnp.zeros((dv, q.shape[0]), F32))
    ns_ref[...] = sn_t[0:dk * dv, :].T
    to[...] = jnp.zeros_like(to)
    to[0:dv, :] = o_t
    return to[...].T


def _sample_mixer_kernel(x_ref, wg_ref, wr_ref, wh_ref, bgate_ref, gn_ref, rn_ref, hn_ref,
                         cos_ref, sin_ref, gam_ref, loglb_ref, l1mlb_ref, sg_ref, sr_ref, sh_ref,
                         o_ref, nsg_ref, nsr_ref, nsh_ref,
                         zg, zr, zh, la, tq, tk, te, tv, to, s_t, sn_t):
    h = pl.program_id(0)

    @pl.when(h == 0)
    def _():
        xb = x_ref[...].astype(BF16)
        zg[...] = _dot(xb, wg_ref[...])
        zr[...] = _dot(xb, wr_ref[...])
        zh[...] = _dot(xb, wh_ref[...])
        pre = zg[:, 4 * GROUP_W:5 * GROUP_W] + bgate_ref[...]
        la[...] = _log_sigmoid(pre) * (1.0 / GLA_TAU)

    off = pl.multiple_of(h * SLOT, SLOT)
    sl = pl.ds(off, SLOT)
    tr = (tq, tk, te, tv, to, s_t, sn_t)
    batch = x_ref.shape[0]

    q = zg[:, sl] * (GLA_DK ** -0.5)
    k = zg[:, pl.ds(GROUP_W + off, SLOT)]
    v = zg[:, pl.ds(2 * GROUP_W + off, SLOT)]
    gate = zg[:, pl.ds(3 * GROUP_W + off, SLOT)]
    o = _sample_step(q, k, jnp.exp(la[:, sl]), v, sg_ref, nsg_ref, *tr, GLA_DK, GLA_DV, lambda kk: kk)
    o_ref[:, sl] = (_rms_head_norm(o, gn_ref[pl.ds(h, 1), :], GLA_DV) * _silu(gate)).astype(BF16)

    cs = cos_ref[...]
    sn = sin_ref[...]
    q = _rotate(zr[:, sl], cs, sn)
    k = _rotate(zr[:, pl.ds(GROUP_W + off, SLOT)], cs, sn) * (RET_DK ** -0.5)
    v = zr[:, pl.ds(2 * GROUP_W + off, SLOT)]
    gate = zr[:, pl.ds(3 * GROUP_W + off, SLOT)]
    eg = jnp.broadcast_to(gam_ref[h], (batch, SLOT))
    half = RET_DK // 2
    o = _sample_step(q, k, eg, v, sr_ref, nsr_ref, *tr, RET_DK, RET_DV,
                     lambda kk: kk + jnp.where(kk >= half, SLOT // 2 - half, 0))
    o_ref[:, pl.ds(GROUP_W + off, SLOT)] = (
        _group_head_norm(o, rn_ref[pl.ds(h, 1), :], RET_DV) * _silu(gate)).astype(BF16)

    q = _silu(zh[:, sl])
    hf = zh[:, pl.ds(GROUP_W + off, SLOT)]
    v = zh[:, pl.ds(2 * GROUP_W + off, SLOT)]
    gate = zh[:, pl.ds(3 * GROUP_W + off, SLOT)]
    log_f = _logaddexp(loglb_ref[:, sl], l1mlb_ref[:, sl] + _log_sigmoid(hf))
    f = jnp.exp(log_f)
    o = _sample_step(q, (1.0 - f) * _lane_mask(HGRN_DK), f, v, sh_ref, nsh_ref, *tr,
                     HGRN_DK, HGRN_DV, lambda kk: kk)
    o_ref[:, pl.ds(2 * GROUP_W + off, SLOT)] = (
        _rms_head_norm(o, hn_ref[pl.ds(h, 1), :], HGRN_DV) * _silu(gate)).astype(BF16)


def _sample_mixer_call(x2d, lw, sg, sr, sh):
    batch = x2d.shape[0]
    gsz, hsz = GLA_DK * GLA_DV, HGRN_DK * HGRN_DV
    consts = [lw["w_gla"], lw["w_ret"], lw["w_hgrn"], lw["bgate"], lw["gla_norm"], lw["ret_norm"],
              lw["hgrn_norm"], lw["cos_s"], lw["sin_s"], lw["gamma"], lw["loglb"], lw["l1mlb"]]
    in_specs = ([_const_spec(x2d.shape)] + [_const_spec(c.shape) for c in consts]
                + [pl.BlockSpec((batch, gsz), lambda h: (0, h)),
                   pl.BlockSpec((batch, gsz), lambda h: (0, h)),
                   pl.BlockSpec((batch, hsz), lambda h: (0, h))])
    out_specs = [_const_spec((batch, 3 * GROUP_W)),
                 pl.BlockSpec((batch, gsz), lambda h: (0, h)),
                 pl.BlockSpec((batch, gsz), lambda h: (0, h)),
                 pl.BlockSpec((batch, hsz), lambda h: (0, h))]
    out_shape = [jax.ShapeDtypeStruct((batch, 3 * GROUP_W), BF16),
                 jax.ShapeDtypeStruct((batch, N_HEADS * gsz), F32),
                 jax.ShapeDtypeStruct((batch, N_HEADS * gsz), F32),
                 jax.ShapeDtypeStruct((batch, N_HEADS * hsz), F32)]
    scratch = [pltpu.VMEM((batch, lw["w_gla"].shape[1]), F32), pltpu.VMEM((batch, 4 * GROUP_W), F32),
               pltpu.VMEM((batch, 4 * GROUP_W), F32), pltpu.VMEM((batch, GROUP_W), F32)]
    scratch += [pltpu.VMEM((SLOT, batch), F32) for _ in range(5)]
    scratch += [pltpu.VMEM((gsz, batch), F32), pltpu.VMEM((gsz, batch), F32)]
    return pl.pallas_call(
        _sample_mixer_kernel,
        grid=(N_HEADS,),
        in_specs=in_specs, out_specs=out_specs, out_shape=out_shape, scratch_shapes=scratch,
        compiler_params=pltpu.CompilerParams(dimension_semantics=("arbitrary",), vmem_limit_bytes=VMEM_LIMIT),
        name="sample_mixer",
    )(x2d, *consts, sg.reshape(batch, -1), sr.reshape(batch, -1), sh.reshape(batch, -1))


def _out_kernel(og_ref, or_ref, oh_ref, x_ref, wo_ref, g_ref, b_ref, xt_ref):
    mix = _dot(og_ref[...], wo_ref[0]) + _dot(or_ref[...], wo_ref[1]) + _dot(oh_ref[...], wo_ref[2])
    y = ALPHA * x_ref[...] + mix
    mu = jnp.mean(y, axis=-1, keepdims=True)
    d = y - mu
    var = jnp.mean(d * d, axis=-1, keepdims=True)
    xt_ref[...] = (d * lax.rsqrt(var + NORM_EPS) * g_ref[...] + b_ref[...]).T


def _out_call(og, orr, oh, x2d, lw):
    t = x2d.shape[0]
    tm = min(1024, t)
    row = lambda i: (i, 0)
    return pl.pallas_call(
        _out_kernel,
        grid=(t // tm,),
        in_specs=[pl.BlockSpec((tm, GROUP_W), row), pl.BlockSpec((tm, GROUP_W), row),
                  pl.BlockSpec((tm, GROUP_W), row), pl.BlockSpec((tm, D_MODEL), row),
                  _const_spec(lw["w_out"].shape), _const_spec((1, D_MODEL)), _const_spec((1, D_MODEL))],
        out_specs=pl.BlockSpec((D_MODEL, tm), lambda i: (0, i)),
        out_shape=jax.ShapeDtypeStruct((D_MODEL, t), F32),
        compiler_params=pltpu.CompilerParams(dimension_semantics=("arbitrary",), vmem_limit_bytes=VMEM_LIMIT),
        name="out_proj_ln",
    )(og, orr, oh, x2d, lw["w_out"], lw["ln1_g"], lw["ln1_b"])


def _oddeven_merge(lo, hi, r):
    step = r * 2
    if step < hi - lo:
        yield from _oddeven_merge(lo, hi, step)
        yield from _oddeven_merge(lo + r, hi, step)
        yield from [(i, i + r) for i in range(lo + r, hi - r, step)]
    else:
        yield (lo, lo + r)


def _oddeven_merge_sort(lo, hi):
    if hi - lo >= 1:
        mid = lo + (hi - lo) // 2
        yield from _oddeven_merge_sort(lo, mid)
        yield from _oddeven_merge_sort(mid + 1, hi)
        yield from _oddeven_merge(lo, hi, 1)


_SORT16 = tuple(_oddeven_merge_sort(0, PEER_TOPK - 1))
_BITONIC16 = tuple((i, i + d) for d in (8, 4, 2, 1) for i in range(PEER_TOPK) if i & d == 0)


def _compare_exchange(xs, pairs):
    for i, j in pairs:
        for x in xs:
            x[i], x[j] = jnp.maximum(x[i], x[j]), jnp.minimum(x[i], x[j])
    return xs


def _top16_sorted(tiles):
    xs = _compare_exchange([[a3[i] for i in range(PEER_TOPK)] for a3 in tiles], _SORT16)
    for shift in (4, 2, 1):
        ys = [[pltpu.roll(v, shift, 0) for v in x] for x in xs]
        xs = _compare_exchange([[jnp.maximum(x[i], y[PEER_TOPK - 1 - i]) for i in range(PEER_TOPK)]
                                for x, y in zip(xs, ys)], _BITONIC16)
    return xs


def _sublane_block(rows):
    sub = lax.broadcasted_iota(jnp.int32, (SUBLANE, LANE), 0)
    blk = rows[0]
    for r in range(1, SUBLANE):
        blk = jnp.where(sub == r, rows[r], blk)
    return blk


def _route_tile(a1, a2):
    v1, v2 = _top16_sorted([a1, a2])
    sub = lax.broadcasted_iota(jnp.int32, (SUBLANE, LANE), 0)
    v2a = _sublane_block(v2[0:8])
    v2b = _sublane_block(v2[8:16])
    v1b = _sublane_block(v1[8:16])
    cands = [v1[0] + v2a, v1[0] + v2b]
    for r1 in range(1, 8):
        cands.append(jnp.where(sub < PEER_TOPK // (r1 + 1), v1[r1] + v2a, NEG_INF))
    cands.append(v1b + v2[0])
    filler = jnp.full((SUBLANE, LANE), NEG_INF, F32)
    cand_tile = jnp.stack(cands + [filler] * (PEER_TOPK - len(cands)))
    thr = _top16_sorted([cand_tile])[0][PEER_TOPK - 1]
    top = v1[0] + v2[0]
    z8 = jnp.zeros((SUBLANE, LANE), F32)
    for blk in cands:
        z8 = z8 + jnp.where(blk >= thr, jnp.exp(blk - top), 0.0)
    z = jnp.broadcast_to(jnp.sum(z8, axis=0, keepdims=True), (SUBLANE, LANE))
    n1 = jnp.zeros(a1.shape, F32)
    rank2 = jnp.zeros(a2.shape, F32)
    for r in range(PEER_TOPK):
        n1 = jnp.where(a1 + v2[r] >= thr, r + 1.0, n1)
        rank2 = jnp.where(v2[r] > a2, r + 1.0, rank2)
    e1 = jnp.exp(a1 - v1[0]) * SQRT_HALF
    e2 = jnp.exp(a2 - v2[0]) / z
    return n1, rank2, e1, e2


def _peer_kernel(xt_ref, wqt_ref, sk_ref, u_ref, vt_ref, p_ref, plet_ref, gatet_ref, g2_ref, b2_ref,
                 out_ref, xbf, s_scr, n1, e1, rk2, e2, h_scr, w_scr, acc, *, tn, te, nj):
    j = pl.program_id(1)
    lane_tiles = tn // LANE
    groups = (N_KEYS // SUBLANE, SUBLANE, LANE)

    def lane_slice(lt):
        return pl.ds(pl.multiple_of(lt * LANE, LANE), LANE)

    @pl.when(j == 0)
    def _route():
        xbf[...] = xt_ref[...].astype(BF16)

        def head_body(h, carry):
            r = pl.multiple_of(h * 2 * N_KEYS, 2 * N_KEYS)
            qh = _dot(wqt_ref[pl.ds(r, 2 * N_KEYS), :], xbf[...]).astype(BF16)
            s_scr[0] = _dot(sk_ref[h, 0], qh[0:N_KEYS])
            s_scr[1] = _dot(sk_ref[h, 1], qh[N_KEYS:2 * N_KEYS])

            def lane_body(lt, c):
                lanes = lane_slice(lt)
                n1_t, rank2_t, e1_t, e2_t = _route_tile(s_scr[0, :, lanes].reshape(groups),
                                                        s_scr[1, :, lanes].reshape(groups))
                n1[lt, h] = n1_t.reshape(N_KEYS, LANE)
                e1[lt, h] = e1_t.reshape(N_KEYS, LANE)
                rk2[lt, h] = rank2_t.reshape(N_KEYS, LANE).astype(BF16).reshape(rk2.shape[2:])
                e2[lt, h] = e2_t.reshape(N_KEYS, LANE).astype(BF16).reshape(e2.shape[2:])
                return c

            lax.fori_loop(0, lane_tiles, lane_body, 0)
            return carry

        lax.fori_loop(0, PEER_HEADS, head_body, 0)
        acc[...] = jnp.zeros_like(acc)

    na = te // N_KEYS
    assert na % SUBLANE == 0
    packed = rk2.shape[2:]
    row16 = (1, 2 * SUBLANE, LANE)

    hval = _dot(u_ref[...], xbf[...])
    for l2 in range(lane_tiles):
        h_scr[l2] = hval[:, l2 * LANE:(l2 + 1) * LANE].astype(BF16)

    def gate_body(p, carry):
        for grp in range(na // SUBLANE):
            a0 = pl.multiple_of(j * na + grp * SUBLANE, SUBLANE)
            n1blk = [n1[p, h, pl.ds(a0, SUBLANE), :] for h in range(PEER_HEADS)]
            e1blk = [e1[p, h, pl.ds(a0, SUBLANE), :] for h in range(PEER_HEADS)]
            for ai in range(0, SUBLANE, 2):
                gates = [None, None]
                for h in range(PEER_HEADS):
                    rk2h = rk2[p, h]
                    e2h = e2[p, h]
                    for d in range(2):
                        n1a = jnp.broadcast_to(n1blk[h][ai + d:ai + d + 1, :], row16[1:]).astype(BF16).reshape(row16)
                        e1a = jnp.broadcast_to(e1blk[h][ai + d:ai + d + 1, :], row16[1:]).astype(BF16).reshape(row16)
                        term = jnp.where(rk2h < n1a, e2h, jnp.zeros_like(e2h)) * e1a
                        gates[d] = term if h == 0 else gates[d] + term
                for d in range(2):
                    r0 = (grp * SUBLANE + ai + d) * N_KEYS
                    hs = h_scr[p, r0:r0 + N_KEYS, :].reshape(packed)
                    w_scr[p, r0:r0 + N_KEYS, :] = (gates[d] * (hs + hs * lax.erf(hs))).reshape(N_KEYS, LANE)
        return carry

    lax.fori_loop(0, lane_tiles, gate_body, 0)
    acc[...] += _dot(vt_ref[...], jnp.concatenate([w_scr[l2] for l2 in range(lane_tiles)], axis=1))

    @pl.when(j == nj - 1)
    def _finish():
        def norm_body(lt, c):
            lanes = lane_slice(lt)
            y = ALPHA * xt_ref[:, lanes] + acc[:, lanes]
            mu = jnp.mean(y, axis=0, keepdims=True)
            d = y - mu
            var = jnp.mean(d * d, axis=0, keepdims=True)
            yn = d * lax.rsqrt(var + NORM_EPS) * g2_ref[...] + b2_ref[...]
            acc[:, lanes] = yn
            xbf[:, lanes] = yn.astype(BF16)
            return c

        lax.fori_loop(0, lane_tiles, norm_body, 0)
        emb = _dot_nt(plet_ref[...], p_ref[...].astype(BF16))
        gt = _dot(gatet_ref[...], xbf[...])
        out_ref[...] = (acc[...] + emb * _sigmoid(gt)).T


def _peer_call(xt, p2d, lw, tables):
    u_all, vt_all, layer = tables
    t = xt.shape[1]
    tn = min(512, t)
    te = PEER_TE
    nj = vt_all.shape[1]
    lane_tiles = tn // LANE
    once = dict(pipeline_mode=pl.Buffered(1))
    in_specs = [
        pl.BlockSpec((D_MODEL, tn), lambda i, j: (0, i)),
        pl.BlockSpec(lw["wqt"].shape, lambda i, j: (0, 0), **once),
        pl.BlockSpec(lw["subkeys"].shape, lambda i, j: (0, 0, 0, 0), **once),
        pl.BlockSpec((None, te, D_MODEL), lambda i, j: (layer, j, 0)),
        pl.BlockSpec((None, None, D_MODEL, te), lambda i, j: (layer, j, 0, 0)),
        pl.BlockSpec((tn, PLE_DIM), lambda i, j: (i, 0)),
        pl.BlockSpec(lw["plet"].shape, lambda i, j: (0, 0), **once),
        pl.BlockSpec(lw["gatet"].shape, lambda i, j: (0, 0), **once),
        pl.BlockSpec((D_MODEL, LANE), lambda i, j: (0, 0), **once),
        pl.BlockSpec((D_MODEL, LANE), lambda i, j: (0, 0), **once),
    ]
    route = (lane_tiles, PEER_HEADS, N_KEYS, LANE)
    route_packed = (lane_tiles, PEER_HEADS, N_KEYS // (2 * SUBLANE), 2 * SUBLANE, LANE)
    scratch = [pltpu.VMEM((D_MODEL, tn), BF16), pltpu.VMEM((2, N_KEYS, tn), F32),
               pltpu.VMEM(route, F32), pltpu.VMEM(route, F32),
               pltpu.VMEM(route_packed, BF16), pltpu.VMEM(route_packed, BF16),
               pltpu.VMEM((lane_tiles, te, LANE), BF16), pltpu.VMEM((lane_tiles, te, LANE), BF16),
               pltpu.VMEM((D_MODEL, tn), F32)]
    return pl.pallas_call(
        functools.partial(_peer_kernel, tn=tn, te=te, nj=nj),
        grid=(t // tn, nj),
        in_specs=in_specs,
        out_specs=pl.BlockSpec((tn, D_MODEL), lambda i, j: (i, 0)),
        out_shape=jax.ShapeDtypeStruct((t, D_MODEL), F32),
        scratch_shapes=scratch,
        compiler_params=pltpu.CompilerParams(dimension_semantics=("arbitrary", "arbitrary"),
                                             vmem_limit_bytes=VMEM_LIMIT),
        name="peer_ffn_ln_ple",
    )(xt, lw["wqt"], lw["subkeys"], u_all, vt_all, p2d, lw["plet"], lw["gatet"], lw["ln2_g"], lw["ln2_b"])


def _head_slots(w, d):
    lead = w.shape[:-1]
    pad = [(0, 0)] * (len(lead) + 1) + [(0, SLOT - d)]
    return jnp.pad(w.reshape(*lead, N_HEADS, d), pad).reshape(*lead, GROUP_W)


def _rope_slots(w, d):
    lead = w.shape[:-1]
    half = d // 2
    pad = [(0, 0)] * (len(lead) + 2) + [(0, SLOT // 2 - half)]
    return jnp.pad(w.reshape(*lead, N_HEADS, 2, half), pad).reshape(*lead, GROUP_W)


def _row_slots(w, d):
    depth, _, cols = w.shape
    w = jnp.pad(w.reshape(depth, N_HEADS, d, cols), ((0, 0), (0, 0), (0, SLOT - d), (0, 0)))
    return w.reshape(depth, GROUP_W, cols)


def _rope_tables(pos):
    half = RET_DK // 2
    inv = 1.0 / (ROPE_BASE ** (jnp.arange(0, RET_DK, 2, dtype=F32) / RET_DK))
    ang = pos[:, None] * inv[None, :]
    pad = ((0, 0), (0, SLOT // 2 - half))
    cos = jnp.pad(jnp.cos(ang), pad)
    sin = jnp.pad(jnp.sin(ang), pad)
    return jnp.concatenate([cos, cos], axis=1), jnp.concatenate([-sin, sin], axis=1)


def _retention_constants():
    log_gamma = jnp.log1p(-jnp.exp2(-5.0 - jnp.arange(N_HEADS, dtype=F32)))
    i = jnp.arange(CHUNK, dtype=F32)
    diff = i[:, None] - i[None, :]
    lg = log_gamma[:, None, None]
    dmat = jnp.where(diff >= 0, jnp.exp(jnp.where(diff >= 0, diff, 0.0) * lg), 0.0)
    qdec = jnp.broadcast_to(jnp.exp((i[None, :, None] + 1.0) * lg), (N_HEADS, CHUNK, SLOT))
    kdec = jnp.broadcast_to(jnp.exp((CHUNK - 1.0 - i[None, :, None]) * lg), (N_HEADS, CHUNK, SLOT))
    sdec = jnp.broadcast_to(jnp.exp(CHUNK * lg), (N_HEADS, 1, SLOT))
    gamma = jnp.broadcast_to(jnp.exp(lg), (N_HEADS, 1, SLOT))
    return dmat, qdec, kdec, sdec, gamma


def _stacked_weights(lb, w_in, gla_w_gate, gla_b_gate, gla_norm, ret_norm, hgrn_norm, w_out,
                     ln1_g, ln1_b, ln2_g, ln2_b, peer_w_q, peer_subkeys, peer_u, peer_v, ple_proj, ple_gate):
    depth = w_in.shape[0]
    sizes = (N_HEADS * GLA_DK, N_HEADS * GLA_DK, N_HEADS * GLA_DV, N_HEADS * GLA_DV, GLA_LOWRANK,
             N_HEADS * RET_DK, N_HEADS * RET_DK, N_HEADS * RET_DV, N_HEADS * RET_DV,
             N_HEADS * HGRN_DK, N_HEADS * HGRN_DK, N_HEADS * HGRN_DV, N_HEADS * HGRN_DV)
    offs = [int(c) for c in np.cumsum(sizes)[:-1]]
    gq, gk, gv, gg, glr, rq, rk, rv, rg, hq, hf, hi, hg = jnp.split(w_in, offs, axis=2)
    sw = {}
    gate_w = jnp.einsum("dir,drk->dik", glr, gla_w_gate, precision=lax.Precision.HIGHEST)
    sw["w_gla"] = jnp.concatenate(
        [_head_slots(gq, GLA_DK), _head_slots(gk, GLA_DK), _head_slots(gv, GLA_DV), _head_slots(gg, GLA_DV),
         _head_slots(gate_w, GLA_DK)], axis=2).astype(BF16)
    sw["w_ret"] = jnp.concatenate(
        [_rope_slots(rq, RET_DK), _rope_slots(rk, RET_DK), _head_slots(rv, RET_DV), _head_slots(rg, RET_DV)],
        axis=2).astype(BF16)
    sw["w_hgrn"] = jnp.concatenate([_head_slots(w, HGRN_DK) for w in (hq, hf, hi, hg)], axis=2).astype(BF16)
    sw["bgate"] = _head_slots(gla_b_gate, GLA_DK)[:, None, :]
    sw["gla_norm"] = jnp.pad(gla_norm, ((0, 0), (0, 0), (0, SLOT - GLA_DV)))
    sw["ret_norm"] = jnp.pad(ret_norm, ((0, 0), (0, 0), (0, SLOT - RET_DV)))
    sw["hgrn_norm"] = jnp.pad(hgrn_norm, ((0, 0), (0, 0), (0, SLOT - HGRN_DV)))
    lbh = lb.reshape(depth, 1, N_HEADS, HGRN_DK)
    pad = ((0, 0), (0, 0), (0, 0), (0, SLOT - HGRN_DK))
    sw["loglb"] = jnp.pad(jnp.log(lbh), pad, constant_values=-1.0).reshape(depth, 1, GROUP_W)
    sw["l1mlb"] = jnp.pad(jnp.log1p(-lbh), pad, constant_values=-1.0).reshape(depth, 1, GROUP_W)
    g_rows, r_rows = N_HEADS * GLA_DV, N_HEADS * RET_DV
    sw["w_out"] = jnp.stack([
        _row_slots(w_out[:, :g_rows], GLA_DV), _row_slots(w_out[:, g_rows:g_rows + r_rows], RET_DV),
        _row_slots(w_out[:, g_rows + r_rows:], HGRN_DV)], axis=1).astype(BF16)
    sw["ln1_g"], sw["ln1_b"] = ln1_g[:, None, :], ln1_b[:, None, :]
    sw["ln2_g"] = jnp.broadcast_to(ln2_g[:, :, None], (depth, D_MODEL, LANE))
    sw["ln2_b"] = jnp.broadcast_to(ln2_b[:, :, None], (depth, D_MODEL, LANE))
    sw["wqt"] = jnp.swapaxes(peer_w_q, 1, 2).astype(BF16)
    sw["subkeys"] = peer_subkeys.astype(BF16)
    sw["u"] = (peer_u * SQRT_HALF).astype(BF16)
    sw["vt"] = jnp.swapaxes(peer_v.astype(BF16).reshape(depth, -1, PEER_TE, D_MODEL), 2, 3)
    sw["plet"] = jnp.swapaxes(ple_proj, 1, 2).astype(BF16)
    sw["gatet"] = jnp.swapaxes(ple_gate, 1, 2).astype(BF16)
    return sw


_BIG_TABLES = ("u", "vt")


def _unslot_state(st, dk, dv, rope=False):
    if rope:
        half = dk // 2
        st = jnp.concatenate([st[..., :half], st[..., SLOT // 2:SLOT // 2 + half]], axis=-1)
    return jnp.swapaxes(st[..., :dv, :dk], -2, -1)


def kernel(x_prompt, x_sample, p_prompt, p_sample, state_gla, state_ret, state_hgrn, w_in, gla_w_gate,
           gla_b_gate, gla_norm, ret_norm, hgrn_lb_logits, hgrn_norm, w_out, ln1_g, ln1_b, ln2_g, ln2_b,
           peer_w_q, peer_subkeys, peer_u, peer_v, ple_proj, ple_gate):
    bp, lp, _ = x_prompt.shape
    bs = x_sample.shape[0]
    assert x_sample.shape[1] == 1 and lp % CHUNK == 0

    lb = jnp.cumsum(jax.nn.softmax(hgrn_lb_logits.astype(F32), axis=0), axis=0)
    lb = lb - lb[0:1]
    tri_np, masks_np = _chunk_constants()
    tri, masks = jnp.asarray(tri_np, BF16), jnp.asarray(masks_np)
    dmat, qdec, kdec, sdec, gamma = _retention_constants()
    cos_p, sin_p = _rope_tables(jnp.arange(lp, dtype=F32))
    cos_s, sin_s = _rope_tables(PAST_LEN + jnp.arange(1, dtype=F32))
    sw = _stacked_weights(lb, w_in, gla_w_gate, gla_b_gate, gla_norm, ret_norm, hgrn_norm, w_out,
                          ln1_g, ln1_b, ln2_g, ln2_b, peer_w_q, peer_subkeys, peer_u, peer_v,
                          ple_proj, ple_gate)

    xp = x_prompt.reshape(bp * lp, D_MODEL)
    xs = x_sample.reshape(bs, D_MODEL)
    gla_p, ret_p, hgrn_p, gla_s, ret_s, hgrn_s = [], [], [], [], [], []
    for i in range(DEPTH):
        lw = {k: v[i] for k, v in sw.items() if k not in _BIG_TABLES}
        lw.update(cos_s=cos_s, sin_s=sin_s, gamma=gamma)
        tables = (sw["u"], sw["vt"], i)

        og, sg = _prompt_mixer_call(
            _gla_prompt_kernel, xp, bp, lp, lw["w_gla"],
            [lw["bgate"], lw["gla_norm"], tri, masks],
            [lw["w_gla"].shape[1], GROUP_W], "gla_prompt")
        orr, sr = _prompt_mixer_call(
            _ret_prompt_kernel, xp, bp, lp, lw["w_ret"],
            [(cos_p, None), (sin_p, None), lw["ret_norm"], dmat, qdec, kdec, sdec],
            [4 * GROUP_W], "ret_prompt")
        oh, sh = _prompt_mixer_call(
            _hgrn_prompt_kernel, xp, bp, lp, lw["w_hgrn"],
            [lw["loglb"], lw["l1mlb"], lw["hgrn_norm"], tri, masks],
            [4 * GROUP_W], "hgrn_prompt")
        xp = _peer_call(_out_call(og, orr, oh, xp, lw), p_prompt[i].reshape(bp * lp, PLE_DIM), lw, tables)
        gla_p.append(sg)
        ret_p.append(sr)
        hgrn_p.append(sh)

        o_s, nsg, nsr, nsh = _sample_mixer_call(xs, lw, state_gla[i], state_ret[i], state_hgrn[i])
        xs = _peer_call(
            _out_call(o_s[:, :GROUP_W], o_s[:, GROUP_W:2 * GROUP_W], o_s[:, 2 * GROUP_W:], xs, lw),
            p_sample[i].reshape(bs, PLE_DIM), lw, tables)
        gla_s.append(nsg)
        ret_s.append(nsr)
        hgrn_s.append(nsh)

    return (xp.reshape(bp, lp, D_MODEL), xs.reshape(bs, 1, D_MODEL),
            _unslot_state(jnp.stack(gla_p), GLA_DK, GLA_DV),
            _unslot_state(jnp.stack(ret_p), RET_DK, RET_DV, rope=True),
            _unslot_state(jnp.stack(hgrn_p), HGRN_DK, HGRN_DV),
            jnp.stack(gla_s).reshape(DEPTH, bs, N_HEADS, GLA_DK, GLA_DV),
            jnp.stack(ret_s).reshape(DEPTH, bs, N_HEADS, RET_DK, RET_DV),
            jnp.stack(hgrn_s).reshape(DEPTH, bs, N_HEADS, HGRN_DK, HGRN_DV))
```

```python
import functools

import numpy as np
import jax
import jax.numpy as jnp
from jax import lax
from jax.experimental import pallas as pl
from jax.experimental.pallas import tpu as pltpu

F32 = jnp.float32
BF16 = jnp.bfloat16

D_MODEL = 1024
DEPTH = 2
PAST_LEN = 16384
N_HEADS = 4
GLA_DK, GLA_DV = 48, 96
RET_DK, RET_DV = 48, 96
HGRN_DK, HGRN_DV = 64, 64
GLA_LOWRANK = 16
GLA_TAU = 16.0
ROPE_BASE = 10000.0
CHUNK = 64
GLA_UNROLL = 4
HGRN_UNROLL = 8
RET_UNROLL = 8
PEER_HEADS = 8
N_KEYS = 128
PEER_TOPK = 16
PEER_TE = 2048
PLE_DIM = 256
ALPHA = (2 * DEPTH) ** 0.25
NORM_EPS = 1e-5

LANE = 128
SUBLANE = 8
SLOT = LANE
GROUP_W = N_HEADS * SLOT
VMEM_LIMIT = 56 * 1024 * 1024
NEG_INF = float("-inf")


def _dot(a, b):
    return jnp.dot(a, b, preferred_element_type=F32)


def _dot_nt(a, b):
    return lax.dot_general(a, b, (((1,), (1,)), ((), ())), preferred_element_type=F32)


def _dot_tn(a, b):
    return lax.dot_general(a, b, (((0,), (0,)), ((), ())), preferred_element_type=F32)


def _sigmoid(x):
    return jax.nn.sigmoid(x)


def _silu(x):
    return x * _sigmoid(x)


def _log_sigmoid(x):
    return jnp.minimum(x, 0.0) - jnp.log1p(jnp.exp(-jnp.abs(x)))


def _logaddexp(a, c):
    amax = jnp.maximum(a, c)
    delta = a - c
    return jnp.where(jnp.isnan(delta), a + c, amax + jnp.log1p(jnp.exp(-jnp.abs(delta))))


SQRT_HALF = np.float32(0.7071067811865476)


def _lane_mask(n):
    return (lax.broadcasted_iota(jnp.int32, (1, LANE), 1) < n).astype(F32)


def _rms_head_norm(o, g_row, dv):
    ms = jnp.sum(o * o, axis=-1, keepdims=True) * (1.0 / dv)
    return o * lax.rsqrt(ms + NORM_EPS) * g_row


def _group_head_norm(o, g_row, dv):
    mask = _lane_mask(dv)
    mu = jnp.sum(o, axis=-1, keepdims=True) * (1.0 / dv)
    d = (o - mu) * mask
    var = jnp.sum(d * d, axis=-1, keepdims=True) * (1.0 / dv)
    return d * lax.rsqrt(var + NORM_EPS) * g_row


def _chunk_constants():
    i = np.arange(CHUNK)[:, None]
    t = np.arange(CHUNK)[None, :]
    masks = [i == t]
    half = CHUNK // 2
    while half >= 1:
        blk = i // (2 * half)
        second = (i % (2 * half)) >= half
        masks.append(second & ((t % (2 * half)) < half) & (blk == t // (2 * half)))
        half //= 2
    return (t <= i).astype(np.float32), np.stack(masks).astype(np.float32)


def _cumsum_rows(tri_bf, g):
    hi = g.astype(BF16)
    r1 = g - hi.astype(F32)
    mid = r1.astype(BF16)
    lo = (r1 - mid.astype(F32)).astype(BF16)
    return _dot(tri_bf, hi) + _dot(tri_bf, mid) + _dot(tri_bf, lo)


def _level_factors(b, g):
    width = b.shape[1]
    grouped = (CHUNK // SUBLANE, SUBLANE, width)
    row = lax.broadcasted_iota(jnp.int32, b.shape, 0)
    sub = lax.broadcasted_iota(jnp.int32, grouped, 1)
    b3 = b.reshape(grouped)

    def sub_ref(r):
        return jnp.broadcast_to(b3[:, r:r + 1, :], grouped)

    out = []
    half = CHUNK // 2
    while half >= SUBLANE:
        ref = jnp.concatenate(
            [jnp.broadcast_to(b[m * 2 * half + half - 1:m * 2 * half + half], (2 * half, width))
             for m in range(CHUNK // (2 * half))], axis=0)
        out.append(jnp.exp(-jnp.abs(b - ref)))
        half //= 2
    out.append(jnp.exp(-jnp.abs(b3 - sub_ref(3))).reshape(b.shape))
    out.append(jnp.exp(-jnp.abs(b3 - jnp.where(sub < 4, sub_ref(1), sub_ref(5)))).reshape(b.shape))
    out.append(jnp.exp(jnp.where(row % 2 == 1, g, 0.0)))
    return out


def _vector_decay_chunk(q, k, v, g, st_ref, tri_bf, masks_ref):
    heads = [slice(h * SLOT, (h + 1) * SLOT) for h in range(N_HEADS)]
    b = _cumsum_rows(tri_bf, g)
    b_last = b[CHUNK - 1:CHUNK]
    factors = _level_factors(b, g)
    qs = [q.astype(BF16)] + [(q * f).astype(BF16) for f in factors]
    ks = [k.astype(BF16)] + [(k * f).astype(BF16) for f in factors]
    q_in = (q * jnp.exp(b)).astype(BF16)
    k_out = (k * jnp.exp(b_last - b)).astype(BF16)
    decay = jnp.exp(b_last)
    vb = v.astype(BF16)
    scores = []
    for sl in heads:
        sc = masks_ref[0] * _dot_nt(qs[0][:, sl], ks[0][:, sl])
        for l in range(len(factors)):
            sc = sc + masks_ref[1 + l] * _dot_nt(qs[1 + l][:, sl], ks[1 + l][:, sl])
        scores.append(sc.astype(BF16))
    outs = []
    for h, sl in enumerate(heads):
        st = st_ref[0, h]
        outs.append(_dot(scores[h], vb[:, sl]) + _dot_nt(q_in[:, sl], st.astype(BF16)))
        st_ref[0, h] = st * decay[:, sl] + _dot_tn(vb[:, sl], k_out[:, sl])
    return outs


def _gla_prompt_kernel(x_ref, w_ref, bg_ref, nrm_ref, tri_ref, masks_ref,
                       o_ref, st_ref, z_scr, la_scr, *, seg):
    @pl.when(pl.program_id(1) == 0)
    def _():
        st_ref[...] = jnp.zeros_like(st_ref)

    z_scr[...] = _dot(x_ref[...].astype(BF16), w_ref[...])
    pre = z_scr[:, 4 * GROUP_W:5 * GROUP_W] + bg_ref[...]
    la_scr[...] = _log_sigmoid(pre) * (1.0 / GLA_TAU)
    tri = tri_ref[...]

    def body(c, carry):
        r0 = pl.multiple_of(c * CHUNK, CHUNK)
        rows = pl.ds(r0, CHUNK)
        q = z_scr[rows, 0:GROUP_W] * (GLA_DK ** -0.5)
        k = z_scr[rows, GROUP_W:2 * GROUP_W]
        v = z_scr[rows, 2 * GROUP_W:3 * GROUP_W]
        outs = _vector_decay_chunk(q, k, v, la_scr[rows, :], st_ref, tri, masks_ref)
        for h in range(N_HEADS):
            gate = z_scr[rows, 3 * GROUP_W + h * SLOT:3 * GROUP_W + (h + 1) * SLOT]
            on = _rms_head_norm(outs[h], nrm_ref[h:h + 1, :], GLA_DV)
            o_ref[rows, h * SLOT:(h + 1) * SLOT] = (on * _silu(gate)).astype(BF16)
        return carry

    lax.fori_loop(0, seg // CHUNK, body, 0, unroll=GLA_UNROLL)


def _hgrn_prompt_kernel(x_ref, w_ref, loglb_ref, l1mlb_ref, nrm_ref, tri_ref, masks_ref,
                        o_ref, st_ref, z_scr, *, seg):
    @pl.when(pl.program_id(1) == 0)
    def _():
        st_ref[...] = jnp.zeros_like(st_ref)

    z_scr[...] = _dot(x_ref[...].astype(BF16), w_ref[...])
    tri = tri_ref[...]
    kmask = jnp.concatenate([_lane_mask(HGRN_DK)] * N_HEADS, axis=1)

    def body(c, carry):
        r0 = pl.multiple_of(c * CHUNK, CHUNK)
        rows = pl.ds(r0, CHUNK)
        q = _silu(z_scr[rows, 0:GROUP_W])
        v = z_scr[rows, 2 * GROUP_W:3 * GROUP_W]
        log_f = _logaddexp(loglb_ref[...], l1mlb_ref[...] + _log_sigmoid(z_scr[rows, GROUP_W:2 * GROUP_W]))
        k = (1.0 - jnp.exp(log_f)) * kmask
        outs = _vector_decay_chunk(q, k, v, log_f, st_ref, tri, masks_ref)
        for h in range(N_HEADS):
            sl = slice(h * SLOT, (h + 1) * SLOT)
            gate = z_scr[rows, 3 * GROUP_W + h * SLOT:3 * GROUP_W + (h + 1) * SLOT]
            on = _rms_head_norm(outs[h], nrm_ref[h:h + 1, :], HGRN_DV)
            o_ref[rows, sl] = (on * _silu(gate)).astype(BF16)
        return carry

    lax.fori_loop(0, seg // CHUNK, body, 0, unroll=HGRN_UNROLL)


def _rotate(t, cs, sn):
    return t * cs + pltpu.roll(t, SLOT // 2, 1) * sn


def _ret_prompt_kernel(x_ref, w_ref, cos_ref, sin_ref, nrm_ref, dmat_ref, qdec_ref, kdec_ref, sdec_ref,
                       o_ref, st_ref, z_scr, *, seg):
    @pl.when(pl.program_id(1) == 0)
    def _():
        st_ref[...] = jnp.zeros_like(st_ref)

    z_scr[...] = _dot(x_ref[...].astype(BF16), w_ref[...])

    def body(c, carry):
        r0 = pl.multiple_of(c * CHUNK, CHUNK)
        rows = pl.ds(r0, CHUNK)
        cs = cos_ref[rows, :]
        sn = sin_ref[rows, :]
        heads = [slice(h * SLOT, (h + 1) * SLOT) for h in range(N_HEADS)]
        qs = [_rotate(z_scr[rows, h * SLOT:(h + 1) * SLOT], cs, sn) for h in range(N_HEADS)]
        ks = [_rotate(z_scr[rows, GROUP_W + h * SLOT:GROUP_W + (h + 1) * SLOT], cs, sn) * (RET_DK ** -0.5)
              for h in range(N_HEADS)]
        vb = z_scr[rows, 2 * GROUP_W:3 * GROUP_W].astype(BF16)
        scores = [(dmat_ref[h] * _dot_nt(qs[h].astype(BF16), ks[h].astype(BF16))).astype(BF16)
                  for h in range(N_HEADS)]
        q_in = [(qs[h] * qdec_ref[h]).astype(BF16) for h in range(N_HEADS)]
        k_out = [(ks[h] * kdec_ref[h]).astype(BF16) for h in range(N_HEADS)]
        outs = []
        for h, sl in enumerate(heads):
            st = st_ref[0, h]
            outs.append(_dot(scores[h], vb[:, sl]) + _dot_nt(q_in[h], st.astype(BF16)))
            st_ref[0, h] = st * sdec_ref[h] + _dot_tn(vb[:, sl], k_out[h])
        for h, sl in enumerate(heads):
            gate = z_scr[rows, 3 * GROUP_W + h * SLOT:3 * GROUP_W + (h + 1) * SLOT]
            on = _group_head_norm(outs[h], nrm_ref[h:h + 1, :], RET_DV)
            o_ref[rows, sl] = (on * _silu(gate)).astype(BF16)
        return carry

    lax.fori_loop(0, seg // CHUNK, body, 0, unroll=RET_UNROLL)


def _const_spec(shape):
    nd = len(shape)
    return pl.BlockSpec(shape, lambda *_: (0,) * nd)


def _prompt_mixer_call(kernel, x2d, batch, seq, w, extras, scratch_widths, name):
    seg = min(512, seq)
    nseg = seq // seg
    in_specs = [pl.BlockSpec((seg, D_MODEL), lambda b, s: (b * nseg + s, 0)), _const_spec(w.shape)]
    args = [x2d, w]
    for e in extras:
        if isinstance(e, tuple):
            arr, _ = e
            in_specs.append(pl.BlockSpec((seg, arr.shape[1]), lambda b, s: (s, 0)))
            args.append(arr)
        else:
            in_specs.append(_const_spec(e.shape))
            args.append(e)
    return pl.pallas_call(
        functools.partial(kernel, seg=seg),
        grid=(batch, nseg),
        in_specs=in_specs,
        out_specs=[pl.BlockSpec((seg, GROUP_W), lambda b, s: (b * nseg + s, 0)),
                   pl.BlockSpec((1, N_HEADS, SLOT, SLOT), lambda b, s: (b, 0, 0, 0))],
        out_shape=[jax.ShapeDtypeStruct((batch * seq, GROUP_W), BF16),
                   jax.ShapeDtypeStruct((batch, N_HEADS, SLOT, SLOT), F32)],
        scratch_shapes=[pltpu.VMEM((seg, wd), F32) for wd in scratch_widths],
        compiler_params=pltpu.CompilerParams(dimension_semantics=("arbitrary", "arbitrary"),
                                             vmem_limit_bytes=VMEM_LIMIT),
        name=name,
    )(*args)


def _sample_step(q, k, eg, v, s_ref, ns_ref, tq, tk, te, tv, to, s_t, sn_t, dk, dv, row_of):
    tq[...] = q.T
    tk[...] = k.T
    te[...] = eg.T
    tv[...] = v.T
    s_t[0:dk * dv, :] = s_ref[...].T
    vt = tv[0:dv, :]

    def body(kk, oacc):
        kr = row_of(kk)
        r = pl.multiple_of(kk * dv, SUBLANE)
        sn = s_t[pl.ds(r, dv), :] * te[pl.ds(kr, 1), :] + tk[pl.ds(kr, 1), :] * vt
        sn_t[pl.ds(r, dv), :] = sn
        return oacc + tq[pl.ds(kr, 1), :] * sn

    o_t = lax.fori_loop(0, dk, body, jnp.zeros((dv, q.shape[0]), F32))
    ns_ref[...] = sn_t[0:dk * dv, :].T
    to[...] = jnp.zeros_like(to)
    to[0:dv, :] = o_t
    return to[...].T


def _sample_mixer_kernel(x_ref, wg_ref, wr_ref, wh_ref, bgate_ref, gn_ref, rn_ref, hn_ref,
                         cos_ref, sin_ref, gam_ref, loglb_ref, l1mlb_ref, sg_ref, sr_ref, sh_ref,
                         o_ref, nsg_ref, nsr_ref, nsh_ref,
                         zg, zr, zh, la, tq, tk, te, tv, to, s_t, sn_t):
    h = pl.program_id(0)

    @pl.when(h == 0)
    def _():
        xb = x_ref[...].astype(BF16)
        zg[...] = _dot(xb, wg_ref[...])
        zr[...] = _dot(xb, wr_ref[...])
        zh[...] = _dot(xb, wh_ref[...])
        pre = zg[:, 4 * GROUP_W:5 * GROUP_W] + bgate_ref[...]
        la[...] = _log_sigmoid(pre) * (1.0 / GLA_TAU)

    off = pl.multiple_of(h * SLOT, SLOT)
    sl = pl.ds(off, SLOT)
    tr = (tq, tk, te, tv, to, s_t, sn_t)
    batch = x_ref.shape[0]

    q = zg[:, sl] * (GLA_DK ** -0.5)
    k = zg[:, pl.ds(GROUP_W + off, SLOT)]
    v = zg[:, pl.ds(2 * GROUP_W + off, SLOT)]
    gate = zg[:, pl.ds(3 * GROUP_W + off, SLOT)]
    o = _sample_step(q, k, jnp.exp(la[:, sl]), v, sg_ref, nsg_ref, *tr, GLA_DK, GLA_DV, lambda kk: kk)
    o_ref[:, sl] = (_rms_head_norm(o, gn_ref[pl.ds(h, 1), :], GLA_DV) * _silu(gate)).astype(BF16)

    cs = cos_ref[...]
    sn = sin_ref[...]
    q = _rotate(zr[:, sl], cs, sn)
    k = _rotate(zr[:, pl.ds(GROUP_W + off, SLOT)], cs, sn) * (RET_DK ** -0.5)
    v = zr[:, pl.ds(2 * GROUP_W + off, SLOT)]
    gate = zr[:, pl.ds(3 * GROUP_W + off, SLOT)]
    eg = jnp.broadcast_to(gam_ref[h], (batch, SLOT))
    half = RET_DK // 2
    o = _sample_step(q, k, eg, v, sr_ref, nsr_ref, *tr, RET_DK, RET_DV,
                     lambda kk: kk + jnp.where(kk >= half, SLOT // 2 - half, 0))
    o_ref[:, pl.ds(GROUP_W + off, SLOT)] = (
        _group_head_norm(o, rn_ref[pl.ds(h, 1), :], RET_DV) * _silu(gate)).astype(BF16)

    q = _silu(zh[:, sl])
    hf = zh[:, pl.ds(GROUP_W + off, SLOT)]
    v = zh[:, pl.ds(2 * GROUP_W + off, SLOT)]
    gate = zh[:, pl.ds(3 * GROUP_W + off, SLOT)]
    log_f = _logaddexp(loglb_ref[:, sl], l1mlb_ref[:, sl] + _log_sigmoid(hf))
    f = jnp.exp(log_f)
    o = _sample_step(q, (1.0 - f) * _lane_mask(HGRN_DK), f, v, sh_ref, nsh_ref, *tr,
                     HGRN_DK, HGRN_DV, lambda kk: kk)
    o_ref[:, pl.ds(2 * GROUP_W + off, SLOT)] = (
        _rms_head_norm(o, hn_ref[pl.ds(h, 1), :], HGRN_DV) * _silu(gate)).astype(BF16)


def _sample_mixer_call(x2d, lw, sg, sr, sh):
    batch = x2d.shape[0]
    gsz, hsz = GLA_DK * GLA_DV, HGRN_DK * HGRN_DV
    consts = [lw["w_gla"], lw["w_ret"], lw["w_hgrn"], lw["bgate"], lw["gla_norm"], lw["ret_norm"],
              lw["hgrn_norm"], lw["cos_s"], lw["sin_s"], lw["gamma"], lw["loglb"], lw["l1mlb"]]
    in_specs = ([_const_spec(x2d.shape)] + [_const_spec(c.shape) for c in consts]
                + [pl.BlockSpec((batch, gsz), lambda h: (0, h)),
                   pl.BlockSpec((batch, gsz), lambda h: (0, h)),
                   pl.BlockSpec((batch, hsz), lambda h: (0, h))])
    out_specs = [_const_spec((batch, 3 * GROUP_W)),
                 pl.BlockSpec((batch, gsz), lambda h: (0, h)),
                 pl.BlockSpec((batch, gsz), lambda h: (0, h)),
                 pl.BlockSpec((batch, hsz), lambda h: (0, h))]
    out_shape = [jax.ShapeDtypeStruct((batch, 3 * GROUP_W), BF16),
                 jax.ShapeDtypeStruct((batch, N_HEADS * gsz), F32),
                 jax.ShapeDtypeStruct((batch, N_HEADS * gsz), F32),
                 jax.ShapeDtypeStruct((batch, N_HEADS * hsz), F32)]
    scratch = [pltpu.VMEM((batch, lw["w_gla"].shape[1]), F32), pltpu.VMEM((batch, 4 * GROUP_W), F32),
               pltpu.VMEM((batch, 4 * GROUP_W), F32), pltpu.VMEM((batch, GROUP_W), F32)]
    scratch += [pltpu.VMEM((SLOT, batch), F32) for _ in range(5)]
    scratch += [pltpu.VMEM((gsz, batch), F32), pltpu.VMEM((gsz, batch), F32)]
    return pl.pallas_call(
        _sample_mixer_kernel,
        grid=(N_HEADS,),
        in_specs=in_specs, out_specs=out_specs, out_shape=out_shape, scratch_shapes=scratch,
        compiler_params=pltpu.CompilerParams(dimension_semantics=("arbitrary",), vmem_limit_bytes=VMEM_LIMIT),
        name="sample_mixer",
    )(x2d, *consts, sg.reshape(batch, -1), sr.reshape(batch, -1), sh.reshape(batch, -1))


def _out_kernel(og_ref, or_ref, oh_ref, x_ref, wo_ref, g_ref, b_ref, xt_ref):
    mix = _dot(og_ref[...], wo_ref[0]) + _dot(or_ref[...], wo_ref[1]) + _dot(oh_ref[...], wo_ref[2])
    y = ALPHA * x_ref[...] + mix
    mu = jnp.mean(y, axis=-1, keepdims=True)
    d = y - mu
    var = jnp.mean(d * d, axis=-1, keepdims=True)
    xt_ref[...] = (d * lax.rsqrt(var + NORM_EPS) * g_ref[...] + b_ref[...]).T


def _out_call(og, orr, oh, x2d, lw):
    t = x2d.shape[0]
    tm = min(1024, t)
    row = lambda i: (i, 0)
    return pl.pallas_call(
        _out_kernel,
        grid=(t // tm,),
        in_specs=[pl.BlockSpec((tm, GROUP_W), row), pl.BlockSpec((tm, GROUP_W), row),
                  pl.BlockSpec((tm, GROUP_W), row), pl.BlockSpec((tm, D_MODEL), row),
                  _const_spec(lw["w_out"].shape), _const_spec((1, D_MODEL)), _const_spec((1, D_MODEL))],
        out_specs=pl.BlockSpec((D_MODEL, tm), lambda i: (0, i)),
        out_shape=jax.ShapeDtypeStruct((D_MODEL, t), F32),
        compiler_params=pltpu.CompilerParams(dimension_semantics=("arbitrary",), vmem_limit_bytes=VMEM_LIMIT),
        name="out_proj_ln",
    )(og, orr, oh, x2d, lw["w_out"], lw["ln1_g"], lw["ln1_b"])


def _oddeven_merge(lo, hi, r):
    step = r * 2
    if step < hi - lo:
        yield from _oddeven_merge(lo, hi, step)
        yield from _oddeven_merge(lo + r, hi, step)
        yield from [(i, i + r) for i in range(lo + r, hi - r, step)]
    else:
        yield (lo, lo + r)


def _oddeven_merge_sort(lo, hi):
    if hi - lo >= 1:
        mid = lo + (hi - lo) // 2
        yield from _oddeven_merge_sort(lo, mid)
        yield from _oddeven_merge_sort(mid + 1, hi)
        yield from _oddeven_merge(lo, hi, 1)


_SORT16 = tuple(_oddeven_merge_sort(0, PEER_TOPK - 1))
_BITONIC16 = tuple((i, i + d) for d in (8, 4, 2, 1) for i in range(PEER_TOPK) if i & d == 0)


def _compare_exchange(xs, pairs):
    for i, j in pairs:
        for x in xs:
            x[i], x[j] = jnp.maximum(x[i], x[j]), jnp.minimum(x[i], x[j])
    return xs


def _top16_sorted(tiles):
    xs = _compare_exchange([[a3[i] for i in range(PEER_TOPK)] for a3 in tiles], _SORT16)
    for shift in (4, 2, 1):
        ys = [[pltpu.roll(v, shift, 0) for v in x] for x in xs]
        xs = _compare_exchange([[jnp.maximum(x[i], y[PEER_TOPK - 1 - i]) for i in range(PEER_TOPK)]
                                for x, y in zip(xs, ys)], _BITONIC16)
    return xs


def _sublane_block(rows):
    sub = lax.broadcasted_iota(jnp.int32, (SUBLANE, LANE), 0)
    blk = rows[0]
    for r in range(1, SUBLANE):
        blk = jnp.where(sub == r, rows[r], blk)
    return blk


def _route_tile(a1, a2):
    v1, v2 = _top16_sorted([a1, a2])
    sub = lax.broadcasted_iota(jnp.int32, (SUBLANE, LANE), 0)
    v2a = _sublane_block(v2[0:8])
    v2b = _sublane_block(v2[8:16])
    v1b = _sublane_block(v1[8:16])
    cands = [v1[0] + v2a, v1[0] + v2b]
    for r1 in range(1, 8):
        cands.append(jnp.where(sub < PEER_TOPK // (r1 + 1), v1[r1] + v2a, NEG_INF))
    cands.append(v1b + v2[0])
    filler = jnp.full((SUBLANE, LANE), NEG_INF, F32)
    cand_tile = jnp.stack(cands + [filler] * (PEER_TOPK - len(cands)))
    thr = _top16_sorted([cand_tile])[0][PEER_TOPK - 1]
    top = v1[0] + v2[0]
    z8 = jnp.zeros((SUBLANE, LANE), F32)
    for blk in cands:
        z8 = z8 + jnp.where(blk >= thr, jnp.exp(blk - top), 0.0)
    z = jnp.broadcast_to(jnp.sum(z8, axis=0, keepdims=True), (SUBLANE, LANE))
    n1 = jnp.zeros(a1.shape, F32)
    rank2 = jnp.zeros(a2.shape, F32)
    for r in range(PEER_TOPK):
        n1 = jnp.where(a1 + v2[r] >= thr, r + 1.0, n1)
        rank2 = jnp.where(v2[r] > a2, r + 1.0, rank2)
    e1 = jnp.exp(a1 - v1[0]) * SQRT_HALF
    e2 = jnp.exp(a2 - v2[0]) / z
    return n1, rank2, e1, e2


def _peer_kernel(xt_ref, wqt_ref, sk_ref, u_ref, vt_ref, p_ref, plet_ref, gatet_ref, g2_ref, b2_ref,
                 out_ref, xbf, s_scr, n1, e1, rk2, e2, h_scr, w_scr, acc, *, tn, te, nj):
    j = pl.program_id(1)
    lane_tiles = tn // LANE
    groups = (N_KEYS // SUBLANE, SUBLANE, LANE)

    def lane_slice(lt):
        return pl.ds(pl.multiple_of(lt * LANE, LANE), LANE)

    @pl.when(j == 0)
    def _route():
        xbf[...] = xt_ref[...].astype(BF16)

        def head_body(h, carry):
            r = pl.multiple_of(h * 2 * N_KEYS, 2 * N_KEYS)
            qh = _dot(wqt_ref[pl.ds(r, 2 * N_KEYS), :], xbf[...]).astype(BF16)
            s_scr[0] = _dot(sk_ref[h, 0], qh[0:N_KEYS])
            s_scr[1] = _dot(sk_ref[h, 1], qh[N_KEYS:2 * N_KEYS])

            def lane_body(lt, c):
                lanes = lane_slice(lt)
                n1_t, rank2_t, e1_t, e2_t = _route_tile(s_scr[0, :, lanes].reshape(groups),
                                                        s_scr[1, :, lanes].reshape(groups))
                n1[lt, h] = n1_t.reshape(N_KEYS, LANE)
                e1[lt, h] = e1_t.reshape(N_KEYS, LANE)
                rk2[lt, h] = rank2_t.reshape(N_KEYS, LANE).astype(BF16).reshape(rk2.shape[2:])
                e2[lt, h] = e2_t.reshape(N_KEYS, LANE).astype(BF16).reshape(e2.shape[2:])
                return c

            lax.fori_loop(0, lane_tiles, lane_body, 0)
            return carry

        lax.fori_loop(0, PEER_HEADS, head_body, 0)
        acc[...] = jnp.zeros_like(acc)

    na = te // N_KEYS
    assert na % SUBLANE == 0
    packed = rk2.shape[2:]
    row16 = (1, 2 * SUBLANE, LANE)

    hval = _dot(u_ref[...], xbf[...])
    for l2 in range(lane_tiles):
        hs = hval[:, l2 * LANE:(l2 + 1) * LANE].astype(BF16)
        h_scr[l2] = hs + hs * lax.erf(hs)

    def gate_body(p, carry):
        for grp in range(na // SUBLANE):
            a0 = pl.multiple_of(j * na + grp * SUBLANE, SUBLANE)
            n1blk = [n1[p, h, pl.ds(a0, SUBLANE), :] for h in range(PEER_HEADS)]
            e1blk = [e1[p, h, pl.ds(a0, SUBLANE), :] for h in range(PEER_HEADS)]
            for ai in range(0, SUBLANE, 2):
                gates = [None, None]
                for h in range(PEER_HEADS):
                    rk2h = rk2[p, h]
                    e2h = e2[p, h]
                    for d in range(2):
                        n1a = jnp.broadcast_to(n1blk[h][ai + d:ai + d + 1, :], row16[1:]).astype(BF16).reshape(row16)
                        e1a = jnp.broadcast_to(e1blk[h][ai + d:ai + d + 1, :], row16[1:]).astype(BF16).reshape(row16)
                        term = jnp.where(rk2h < n1a, e2h, jnp.zeros_like(e2h)) * e1a
                        gates[d] = term if h == 0 else gates[d] + term
                for d in range(2):
                    r0 = (grp * SUBLANE + ai + d) * N_KEYS
                    act = h_scr[p, r0:r0 + N_KEYS, :].reshape(packed)
                    w_scr[p, r0:r0 + N_KEYS, :] = (gates[d] * act).reshape(N_KEYS, LANE)
        return carry

    lax.fori_loop(0, lane_tiles, gate_body, 0)
    acc[...] += _dot(vt_ref[...], jnp.concatenate([w_scr[l2] for l2 in range(lane_tiles)], axis=1))

    @pl.when(j == nj - 1)
    def _finish():
        def norm_body(lt, c):
            lanes = lane_slice(lt)
            y = ALPHA * xt_ref[:, lanes] + acc[:, lanes]
            mu = jnp.mean(y, axis=0, keepdims=True)
            d = y - mu
            var = jnp.mean(d * d, axis=0, keepdims=True)
            yn = d * lax.rsqrt(var + NORM_EPS) * g2_ref[...] + b2_ref[...]
            acc[:, lanes] = yn
            xbf[:, lanes] = yn.astype(BF16)
            return c

        lax.fori_loop(0, lane_tiles, norm_body, 0)
        emb = _dot_nt(plet_ref[...], p_ref[...].astype(BF16))
        gt = _dot(gatet_ref[...], xbf[...])
        out_ref[...] = (acc[...] + emb * _sigmoid(gt)).T


def _peer_call(xt, p2d, lw, tables):
    u_all, vt_all, layer = tables
    t = xt.shape[1]
    tn = min(512, t)
    te = PEER_TE
    nj = vt_all.shape[1]
    lane_tiles = tn // LANE
    once = dict(pipeline_mode=pl.Buffered(1))
    in_specs = [
        pl.BlockSpec((D_MODEL, tn), lambda i, j: (0, i)),
        pl.BlockSpec(lw["wqt"].shape, lambda i, j: (0, 0), **once),
        pl.BlockSpec(lw["subkeys"].shape, lambda i, j: (0, 0, 0, 0), **once),
        pl.BlockSpec((None, te, D_MODEL), lambda i, j: (layer, j, 0)),
        pl.BlockSpec((None, None, D_MODEL, te), lambda i, j: (layer, j, 0, 0)),
        pl.BlockSpec((tn, PLE_DIM), lambda i, j: (i, 0)),
        pl.BlockSpec(lw["plet"].shape, lambda i, j: (0, 0), **once),
        pl.BlockSpec(lw["gatet"].shape, lambda i, j: (0, 0), **once),
        pl.BlockSpec((D_MODEL, LANE), lambda i, j: (0, 0), **once),
        pl.BlockSpec((D_MODEL, LANE), lambda i, j: (0, 0), **once),
    ]
    route = (lane_tiles, PEER_HEADS, N_KEYS, LANE)
    route_packed = (lane_tiles, PEER_HEADS, N_KEYS // (2 * SUBLANE), 2 * SUBLANE, LANE)
    scratch = [pltpu.VMEM((D_MODEL, tn), BF16), pltpu.VMEM((2, N_KEYS, tn), F32),
               pltpu.VMEM(route, F32), pltpu.VMEM(route, F32),
               pltpu.VMEM(route_packed, BF16), pltpu.VMEM(route_packed, BF16),
               pltpu.VMEM((lane_tiles, te, LANE), BF16), pltpu.VMEM((lane_tiles, te, LANE), BF16),
               pltpu.VMEM((D_MODEL, tn), F32)]
    return pl.pallas_call(
        functools.partial(_peer_kernel, tn=tn, te=te, nj=nj),
        grid=(t // tn, nj),
        in_specs=in_specs,
        out_specs=pl.BlockSpec((tn, D_MODEL), lambda i, j: (i, 0)),
        out_shape=jax.ShapeDtypeStruct((t, D_MODEL), F32),
        scratch_shapes=scratch,
        compiler_params=pltpu.CompilerParams(dimension_semantics=("arbitrary", "arbitrary"),
                                             vmem_limit_bytes=VMEM_LIMIT),
        name="peer_ffn_ln_ple",
    )(xt, lw["wqt"], lw["subkeys"], u_all, vt_all, p2d, lw["plet"], lw["gatet"], lw["ln2_g"], lw["ln2_b"])


def _head_slots(w, d):
    lead = w.shape[:-1]
    pad = [(0, 0)] * (len(lead) + 1) + [(0, SLOT - d)]
    return jnp.pad(w.reshape(*lead, N_HEADS, d), pad).reshape(*lead, GROUP_W)


def _rope_slots(w, d):
    lead = w.shape[:-1]
    half = d // 2
    pad = [(0, 0)] * (len(lead) + 2) + [(0, SLOT // 2 - half)]
    return jnp.pad(w.reshape(*lead, N_HEADS, 2, half), pad).reshape(*lead, GROUP_W)


def _row_slots(w, d):
    depth, _, cols = w.shape
    w = jnp.pad(w.reshape(depth, N_HEADS, d, cols), ((0, 0), (0, 0), (0, SLOT - d), (0, 0)))
    return w.reshape(depth, GROUP_W, cols)


def _rope_tables(pos):
    half = RET_DK // 2
    inv = 1.0 / (ROPE_BASE ** (jnp.arange(0, RET_DK, 2, dtype=F32) / RET_DK))
    ang = pos[:, None] * inv[None, :]
    pad = ((0, 0), (0, SLOT // 2 - half))
    cos = jnp.pad(jnp.cos(ang), pad)
    sin = jnp.pad(jnp.sin(ang), pad)
    return jnp.concatenate([cos, cos], axis=1), jnp.concatenate([-sin, sin], axis=1)


def _retention_constants():
    log_gamma = jnp.log1p(-jnp.exp2(-5.0 - jnp.arange(N_HEADS, dtype=F32)))
    i = jnp.arange(CHUNK, dtype=F32)
    diff = i[:, None] - i[None, :]
    lg = log_gamma[:, None, None]
    dmat = jnp.where(diff >= 0, jnp.exp(jnp.where(diff >= 0, diff, 0.0) * lg), 0.0)
    qdec = jnp.broadcast_to(jnp.exp((i[None, :, None] + 1.0) * lg), (N_HEADS, CHUNK, SLOT))
    kdec = jnp.broadcast_to(jnp.exp((CHUNK - 1.0 - i[None, :, None]) * lg), (N_HEADS, CHUNK, SLOT))
    sdec = jnp.broadcast_to(jnp.exp(CHUNK * lg), (N_HEADS, 1, SLOT))
    gamma = jnp.broadcast_to(jnp.exp(lg), (N_HEADS, 1, SLOT))
    return dmat, qdec, kdec, sdec, gamma


def _stacked_weights(lb, w_in, gla_w_gate, gla_b_gate, gla_norm, ret_norm, hgrn_norm, w_out,
                     ln1_g, ln1_b, ln2_g, ln2_b, peer_w_q, peer_subkeys, peer_u, peer_v, ple_proj, ple_gate):
    depth = w_in.shape[0]
    sizes = (N_HEADS * GLA_DK, N_HEADS * GLA_DK, N_HEADS * GLA_DV, N_HEADS * GLA_DV, GLA_LOWRANK,
             N_HEADS * RET_DK, N_HEADS * RET_DK, N_HEADS * RET_DV, N_HEADS * RET_DV,
             N_HEADS * HGRN_DK, N_HEADS * HGRN_DK, N_HEADS * HGRN_DV, N_HEADS * HGRN_DV)
    offs = [int(c) for c in np.cumsum(sizes)[:-1]]
    gq, gk, gv, gg, glr, rq, rk, rv, rg, hq, hf, hi, hg = jnp.split(w_in, offs, axis=2)
    sw = {}
    gate_w = jnp.einsum("dir,drk->dik", glr, gla_w_gate, precision=lax.Precision.HIGHEST)
    sw["w_gla"] = jnp.concatenate(
        [_head_slots(gq, GLA_DK), _head_slots(gk, GLA_DK), _head_slots(gv, GLA_DV), _head_slots(gg, GLA_DV),
         _head_slots(gate_w, GLA_DK)], axis=2).astype(BF16)
    sw["w_ret"] = jnp.concatenate(
        [_rope_slots(rq, RET_DK), _rope_slots(rk, RET_DK), _head_slots(rv, RET_DV), _head_slots(rg, RET_DV)],
        axis=2).astype(BF16)
    sw["w_hgrn"] = jnp.concatenate([_head_slots(w, HGRN_DK) for w in (hq, hf, hi, hg)], axis=2).astype(BF16)
    sw["bgate"] = _head_slots(gla_b_gate, GLA_DK)[:, None, :]
    sw["gla_norm"] = jnp.pad(gla_norm, ((0, 0), (0, 0), (0, SLOT - GLA_DV)))
    sw["ret_norm"] = jnp.pad(ret_norm, ((0, 0), (0, 0), (0, SLOT - RET_DV)))
    sw["hgrn_norm"] = jnp.pad(hgrn_norm, ((0, 0), (0, 0), (0, SLOT - HGRN_DV)))
    lbh = lb.reshape(depth, 1, N_HEADS, HGRN_DK)
    pad = ((0, 0), (0, 0), (0, 0), (0, SLOT - HGRN_DK))
    sw["loglb"] = jnp.pad(jnp.log(lbh), pad, constant_values=-1.0).reshape(depth, 1, GROUP_W)
    sw["l1mlb"] = jnp.pad(jnp.log1p(-lbh), pad, constant_values=-1.0).reshape(depth, 1, GROUP_W)
    g_rows, r_rows = N_HEADS * GLA_DV, N_HEADS * RET_DV
    sw["w_out"] = jnp.stack([
        _row_slots(w_out[:, :g_rows], GLA_DV), _row_slots(w_out[:, g_rows:g_rows + r_rows], RET_DV),
        _row_slots(w_out[:, g_rows + r_rows:], HGRN_DV)], axis=1).astype(BF16)
    sw["ln1_g"], sw["ln1_b"] = ln1_g[:, None, :], ln1_b[:, None, :]
    sw["ln2_g"] = jnp.broadcast_to(ln2_g[:, :, None], (depth, D_MODEL, LANE))
    sw["ln2_b"] = jnp.broadcast_to(ln2_b[:, :, None], (depth, D_MODEL, LANE))
    sw["wqt"] = jnp.swapaxes(peer_w_q, 1, 2).astype(BF16)
    sw["subkeys"] = peer_subkeys.astype(BF16)
    sw["u"] = (peer_u * SQRT_HALF).astype(BF16)
    sw["vt"] = jnp.swapaxes(peer_v.astype(BF16).reshape(depth, -1, PEER_TE, D_MODEL), 2, 3)
    sw["plet"] = jnp.swapaxes(ple_proj, 1, 2).astype(BF16)
    sw["gatet"] = jnp.swapaxes(ple_gate, 1, 2).astype(BF16)
    return sw


_BIG_TABLES = ("u", "vt")


def _unslot_state(st, dk, dv, rope=False):
    if rope:
        half = dk // 2
        st = jnp.concatenate([st[..., :half], st[..., SLOT // 2:SLOT // 2 + half]], axis=-1)
    return jnp.swapaxes(st[..., :dv, :dk], -2, -1)


def kernel(x_prompt, x_sample, p_prompt, p_sample, state_gla, state_ret, state_hgrn, w_in, gla_w_gate,
           gla_b_gate, gla_norm, ret_norm, hgrn_lb_logits, hgrn_norm, w_out, ln1_g, ln1_b, ln2_g, ln2_b,
           peer_w_q, peer_subkeys, peer_u, peer_v, ple_proj, ple_gate):
    bp, lp, _ = x_prompt.shape
    bs = x_sample.shape[0]
    assert x_sample.shape[1] == 1 and lp % CHUNK == 0

    lb = jnp.cumsum(jax.nn.softmax(hgrn_lb_logits.astype(F32), axis=0), axis=0)
    lb = lb - lb[0:1]
    tri_np, masks_np = _chunk_constants()
    tri, masks = jnp.asarray(tri_np, BF16), jnp.asarray(masks_np)
    dmat, qdec, kdec, sdec, gamma = _retention_constants()
    cos_p, sin_p = _rope_tables(jnp.arange(lp, dtype=F32))
    cos_s, sin_s = _rope_tables(PAST_LEN + jnp.arange(1, dtype=F32))
    sw = _stacked_weights(lb, w_in, gla_w_gate, gla_b_gate, gla_norm, ret_norm, hgrn_norm, w_out,
                          ln1_g, ln1_b, ln2_g, ln2_b, peer_w_q, peer_subkeys, peer_u, peer_v,
                          ple_proj, ple_gate)

    xp = x_prompt.reshape(bp * lp, D_MODEL)
    xs = x_sample.reshape(bs, D_MODEL)
    gla_p, ret_p, hgrn_p, gla_s, ret_s, hgrn_s = [], [], [], [], [], []
    for i in range(DEPTH):
        lw = {k: v[i] for k, v in sw.items() if k not in _BIG_TABLES}
        lw.update(cos_s=cos_s, sin_s=sin_s, gamma=gamma)
        tables = (sw["u"], sw["vt"], i)

        og, sg = _prompt_mixer_call(
            _gla_prompt_kernel, xp, bp, lp, lw["w_gla"],
            [lw["bgate"], lw["gla_norm"], tri, masks],
            [lw["w_gla"].shape[1], GROUP_W], "gla_prompt")
        orr, sr = _prompt_mixer_call(
            _ret_prompt_kernel, xp, bp, lp, lw["w_ret"],
            [(cos_p, None), (sin_p, None), lw["ret_norm"], dmat, qdec, kdec, sdec],
            [4 * GROUP_W], "ret_prompt")
        oh, sh = _prompt_mixer_call(
            _hgrn_prompt_kernel, xp, bp, lp, lw["w_hgrn"],
            [lw["loglb"], lw["l1mlb"], lw["hgrn_norm"], tri, masks],
            [4 * GROUP_W], "hgrn_prompt")
        xp = _peer_call(_out_call(og, orr, oh, xp, lw), p_prompt[i].reshape(bp * lp, PLE_DIM), lw, tables)
        gla_p.append(sg)
        ret_p.append(sr)
        hgrn_p.append(sh)

        o_s, nsg, nsr, nsh = _sample_mixer_call(xs, lw, state_gla[i], state_ret[i], state_hgrn[i])
        xs = _peer_call(
            _out_call(o_s[:, :GROUP_W], o_s[:, GROUP_W:2 * GROUP_W], o_s[:, 2 * GROUP_W:], xs, lw),
            p_sample[i].reshape(bs, PLE_DIM), lw, tables)
        gla_s.append(nsg)
        ret_s.append(nsr)
        hgrn_s.append(nsh)

    return (xp.reshape(bp, lp, D_MODEL), xs.reshape(bs, 1, D_MODEL),
            _unslot_state(jnp.stack(gla_p), GLA_DK, GLA_DV),
            _unslot_state(jnp.stack(ret_p), RET_DK, RET_DV, rope=True),
            _unslot_state(jnp.stack(hgrn_p), HGRN_DK, HGRN_DV),
            jnp.stack(gla_s).reshape(DEPTH, bs, N_HEADS, GLA_DK, GLA_DV),
            jnp.stack(ret_s).reshape(DEPTH, bs, N_HEADS, RET_DK, RET_DV),
            jnp.stack(hgrn_s).reshape(DEPTH, bs, N_HEADS, HGRN_DK, HGRN_DV))
```

```python
import functools

import numpy as np
import jax
import jax.numpy as jnp
from jax import lax
from jax.experimental import pallas as pl
from jax.experimental.pallas import tpu as pltpu

F32 = jnp.float32
BF16 = jnp.bfloat16

D_MODEL = 1024
DEPTH = 2
PAST_LEN = 16384
N_HEADS = 4
GLA_DK, GLA_DV = 48, 96
RET_DK, RET_DV = 48, 96
HGRN_DK, HGRN_DV = 64, 64
GLA_LOWRANK = 16
GLA_TAU = 16.0
ROPE_BASE = 10000.0
CHUNK = 64
GLA_UNROLL = 4
HGRN_UNROLL = 8
RET_UNROLL = 8
PEER_HEADS = 8
N_KEYS = 128
PEER_TOPK = 16
PEER_TE = 2048
PLE_DIM = 256
ALPHA = (2 * DEPTH) ** 0.25
NORM_EPS = 1e-5

LANE = 128
SUBLANE = 8
SLOT = LANE
GROUP_W = N_HEADS * SLOT
VMEM_LIMIT = 56 * 1024 * 1024
NEG_INF = float("-inf")


def _dot(a, b):
    return jnp.dot(a, b, preferred_element_type=F32)


def _dot_nt(a, b):
    return lax.dot_general(a, b, (((1,), (1,)), ((), ())), preferred_element_type=F32)


def _dot_tn(a, b):
    return lax.dot_general(a, b, (((0,), (0,)), ((), ())), preferred_element_type=F32)


def _sigmoid(x):
    return jax.nn.sigmoid(x)


def _silu(x):
    return x * _sigmoid(x)


def _log_sigmoid(x):
    return jnp.minimum(x, 0.0) - jnp.log1p(jnp.exp(-jnp.abs(x)))


def _logaddexp(a, c):
    amax = jnp.maximum(a, c)
    delta = a - c
    return jnp.where(jnp.isnan(delta), a + c, amax + jnp.log1p(jnp.exp(-jnp.abs(delta))))


SQRT_HALF = np.float32(0.7071067811865476)


def _lane_mask(n):
    return (lax.broadcasted_iota(jnp.int32, (1, LANE), 1) < n).astype(F32)


def _rms_head_norm(o, g_row, dv):
    ms = jnp.sum(o * o, axis=-1, keepdims=True) * (1.0 / dv)
    return o * lax.rsqrt(ms + NORM_EPS) * g_row


def _group_head_norm(o, g_row, dv):
    mask = _lane_mask(dv)
    mu = jnp.sum(o, axis=-1, keepdims=True) * (1.0 / dv)
    d = (o - mu) * mask
    var = jnp.sum(d * d, axis=-1, keepdims=True) * (1.0 / dv)
    return d * lax.rsqrt(var + NORM_EPS) * g_row


def _chunk_constants():
    i = np.arange(CHUNK)[:, None]
    t = np.arange(CHUNK)[None, :]
    masks = [i == t]
    half = CHUNK // 2
    while half >= 1:
        blk = i // (2 * half)
        second = (i % (2 * half)) >= half
        masks.append(second & ((t % (2 * half)) < half) & (blk == t // (2 * half)))
        half //= 2
    return (t <= i).astype(np.float32), np.stack(masks).astype(np.float32)


def _cumsum_rows(tri_bf, g):
    hi = g.astype(BF16)
    r1 = g - hi.astype(F32)
    mid = r1.astype(BF16)
    lo = (r1 - mid.astype(F32)).astype(BF16)
    return _dot(tri_bf, hi) + _dot(tri_bf, mid) + _dot(tri_bf, lo)


def _level_factors(b, g):
    width = b.shape[1]
    grouped = (CHUNK // SUBLANE, SUBLANE, width)
    row = lax.broadcasted_iota(jnp.int32, b.shape, 0)
    sub = lax.broadcasted_iota(jnp.int32, grouped, 1)
    b3 = b.reshape(grouped)

    def sub_ref(r):
        return jnp.broadcast_to(b3[:, r:r + 1, :], grouped)

    out = []
    half = CHUNK // 2
    while half >= SUBLANE:
        ref = jnp.concatenate(
            [jnp.broadcast_to(b[m * 2 * half + half - 1:m * 2 * half + half], (2 * half, width))
             for m in range(CHUNK // (2 * half))], axis=0)
        out.append(jnp.exp(-jnp.abs(b - ref)))
        half //= 2
    out.append(jnp.exp(-jnp.abs(b3 - sub_ref(3))).reshape(b.shape))
    out.append(jnp.exp(-jnp.abs(b3 - jnp.where(sub < 4, sub_ref(1), sub_ref(5)))).reshape(b.shape))
    out.append(jnp.exp(jnp.where(row % 2 == 1, g, 0.0)))
    return out


def _vector_decay_chunk(q, k, v, g, st_ref, tri_bf, masks_ref):
    heads = [slice(h * SLOT, (h + 1) * SLOT) for h in range(N_HEADS)]
    b = _cumsum_rows(tri_bf, g)
    b_last = b[CHUNK - 1:CHUNK]
    factors = _level_factors(b, g)
    qs = [q.astype(BF16)] + [(q * f).astype(BF16) for f in factors]
    ks = [k.astype(BF16)] + [(k * f).astype(BF16) for f in factors]
    q_in = (q * jnp.exp(b)).astype(BF16)
    k_out = (k * jnp.exp(b_last - b)).astype(BF16)
    decay = jnp.exp(b_last)
    vb = v.astype(BF16)
    scores = []
    for sl in heads:
        sc = masks_ref[0] * _dot_nt(qs[0][:, sl], ks[0][:, sl])
        for l in range(len(factors)):
            sc = sc + masks_ref[1 + l] * _dot_nt(qs[1 + l][:, sl], ks[1 + l][:, sl])
        scores.append(sc.astype(BF16))
    outs = []
    for h, sl in enumerate(heads):
        st = st_ref[0, h]
        outs.append(_dot(scores[h], vb[:, sl]) + _dot_nt(q_in[:, sl], st.astype(BF16)))
        st_ref[0, h] = st * decay[:, sl] + _dot_tn(vb[:, sl], k_out[:, sl])
    return outs


def _gla_prompt_kernel(x_ref, w_ref, bg_ref, nrm_ref, tri_ref, masks_ref,
                       o_ref, st_ref, z_scr, la_scr, *, seg):
    @pl.when(pl.program_id(1) == 0)
    def _():
        st_ref[...] = jnp.zeros_like(st_ref)

    z_scr[...] = _dot(x_ref[...].astype(BF16), w_ref[...])
    pre = z_scr[:, 4 * GROUP_W:5 * GROUP_W] + bg_ref[...]
    la_scr[...] = _log_sigmoid(pre) * (1.0 / GLA_TAU)
    tri = tri_ref[...]

    def body(c, carry):
        r0 = pl.multiple_of(c * CHUNK, CHUNK)
        rows = pl.ds(r0, CHUNK)
        q = z_scr[rows, 0:GROUP_W] * (GLA_DK ** -0.5)
        k = z_scr[rows, GROUP_W:2 * GROUP_W]
        v = z_scr[rows, 2 * GROUP_W:3 * GROUP_W]
        outs = _vector_decay_chunk(q, k, v, la_scr[rows, :], st_ref, tri, masks_ref)
        for h in range(N_HEADS):
            gate = z_scr[rows, 3 * GROUP_W + h * SLOT:3 * GROUP_W + (h + 1) * SLOT]
            on = _rms_head_norm(outs[h], nrm_ref[h:h + 1, :], GLA_DV)
            o_ref[rows, h * SLOT:(h + 1) * SLOT] = (on * _silu(gate)).astype(BF16)
        return carry

    lax.fori_loop(0, seg // CHUNK, body, 0, unroll=GLA_UNROLL)


def _hgrn_prompt_kernel(x_ref, w_ref, loglb_ref, l1mlb_ref, nrm_ref, tri_ref, masks_ref,
                        o_ref, st_ref, z_scr, *, seg):
    @pl.when(pl.program_id(1) == 0)
    def _():
        st_ref[...] = jnp.zeros_like(st_ref)

    z_scr[...] = _dot(x_ref[...].astype(BF16), w_ref[...])
    tri = tri_ref[...]
    kmask = jnp.concatenate([_lane_mask(HGRN_DK)] * N_HEADS, axis=1)

    def body(c, carry):
        r0 = pl.multiple_of(c * CHUNK, CHUNK)
        rows = pl.ds(r0, CHUNK)
        q = _silu(z_scr[rows, 0:GROUP_W])
        v = z_scr[rows, 2 * GROUP_W:3 * GROUP_W]
        log_f = _logaddexp(loglb_ref[...], l1mlb_ref[...] + _log_sigmoid(z_scr[rows, GROUP_W:2 * GROUP_W]))
        k = (1.0 - jnp.exp(log_f)) * kmask
        outs = _vector_decay_chunk(q, k, v, log_f, st_ref, tri, masks_ref)
        for h in range(N_HEADS):
            sl = slice(h * SLOT, (h + 1) * SLOT)
            gate = z_scr[rows, 3 * GROUP_W + h * SLOT:3 * GROUP_W + (h + 1) * SLOT]
            on = _rms_head_norm(outs[h], nrm_ref[h:h + 1, :], HGRN_DV)
            o_ref[rows, sl] = (on * _silu(gate)).astype(BF16)
        return carry

    lax.fori_loop(0, seg // CHUNK, body, 0, unroll=HGRN_UNROLL)


def _rotate(t, cs, sn):
    return t * cs + pltpu.roll(t, SLOT // 2, 1) * sn


def _ret_prompt_kernel(x_ref, w_ref, cos_ref, sin_ref, nrm_ref, dmat_ref, qdec_ref, kdec_ref, sdec_ref,
                       o_ref, st_ref, z_scr, *, seg):
    @pl.when(pl.program_id(1) == 0)
    def _():
        st_ref[...] = jnp.zeros_like(st_ref)

    z_scr[...] = _dot(x_ref[...].astype(BF16), w_ref[...])

    def body(c, carry):
        r0 = pl.multiple_of(c * CHUNK, CHUNK)
        rows = pl.ds(r0, CHUNK)
        cs = cos_ref[rows, :]
        sn = sin_ref[rows, :]
        heads = [slice(h * SLOT, (h + 1) * SLOT) for h in range(N_HEADS)]
        qs = [_rotate(z_scr[rows, h * SLOT:(h + 1) * SLOT], cs, sn) for h in range(N_HEADS)]
        ks = [_rotate(z_scr[rows, GROUP_W + h * SLOT:GROUP_W + (h + 1) * SLOT], cs, sn) * (RET_DK ** -0.5)
              for h in range(N_HEADS)]
        vb = z_scr[rows, 2 * GROUP_W:3 * GROUP_W].astype(BF16)
        scores = [(dmat_ref[h] * _dot_nt(qs[h].astype(BF16), ks[h].astype(BF16))).astype(BF16)
                  for h in range(N_HEADS)]
        q_in = [(qs[h] * qdec_ref[h]).astype(BF16) for h in range(N_HEADS)]
        k_out = [(ks[h] * kdec_ref[h]).astype(BF16) for h in range(N_HEADS)]
        outs = []
        for h, sl in enumerate(heads):
            st = st_ref[0, h]
            outs.append(_dot(scores[h], vb[:, sl]) + _dot_nt(q_in[h], st.astype(BF16)))
            st_ref[0, h] = st * sdec_ref[h] + _dot_tn(vb[:, sl], k_out[h])
        for h, sl in enumerate(heads):
            gate = z_scr[rows, 3 * GROUP_W + h * SLOT:3 * GROUP_W + (h + 1) * SLOT]
            on = _group_head_norm(outs[h], nrm_ref[h:h + 1, :], RET_DV)
            o_ref[rows, sl] = (on * _silu(gate)).astype(BF16)
        return carry

    lax.fori_loop(0, seg // CHUNK, body, 0, unroll=RET_UNROLL)


def _const_spec(shape):
    nd = len(shape)
    return pl.BlockSpec(shape, lambda *_: (0,) * nd)


def _prompt_mixer_call(kernel, x2d, batch, seq, w, extras, scratch_widths, name):
    seg = min(512, seq)
    nseg = seq // seg
    in_specs = [pl.BlockSpec((seg, D_MODEL), lambda b, s: (b * nseg + s, 0)), _const_spec(w.shape)]
    args = [x2d, w]
    for e in extras:
        if isinstance(e, tuple):
            arr, _ = e
            in_specs.append(pl.BlockSpec((seg, arr.shape[1]), lambda b, s: (s, 0)))
            args.append(arr)
        else:
            in_specs.append(_const_spec(e.shape))
            args.append(e)
    return pl.pallas_call(
        functools.partial(kernel, seg=seg),
        grid=(batch, nseg),
        in_specs=in_specs,
        out_specs=[pl.BlockSpec((seg, GROUP_W), lambda b, s: (b * nseg + s, 0)),
                   pl.BlockSpec((1, N_HEADS, SLOT, SLOT), lambda b, s: (b, 0, 0, 0))],
        out_shape=[jax.ShapeDtypeStruct((batch * seq, GROUP_W), BF16),
                   jax.ShapeDtypeStruct((batch, N_HEADS, SLOT, SLOT), F32)],
        scratch_shapes=[pltpu.VMEM((seg, wd), F32) for wd in scratch_widths],
        compiler_params=pltpu.CompilerParams(dimension_semantics=("arbitrary", "arbitrary"),
                                             vmem_limit_bytes=VMEM_LIMIT),
        name=name,
    )(*args)


def _sample_step(q, k, eg, v, s_ref, ns_ref, tq, tk, te, tv, to, s_t, sn_t, dk, dv, row_of):
    tq[...] = q.T
    tk[...] = k.T
    te[...] = eg.T
    tv[...] = v.T
    s_t[0:dk * dv, :] = s_ref[...].T
    vt = tv[0:dv, :]

    def body(kk, oacc):
        kr = row_of(kk)
        r = pl.multiple_of(kk * dv, SUBLANE)
        sn = s_t[pl.ds(r, dv), :] * te[pl.ds(kr, 1), :] + tk[pl.ds(kr, 1), :] * vt
        sn_t[pl.ds(r, dv), :] = sn
        return oacc + tq[pl.ds(kr, 1), :] * sn

    o_t = lax.fori_loop(0, dk, body, jnp.zeros((dv, q.shape[0]), F32))
    ns_ref[...] = sn_t[0:dk * dv, :].T
    to[...] = jnp.zeros_like(to)
    to[0:dv, :] = o_t
    return to[...].T


def _sample_mixer_kernel(x_ref, wg_ref, wr_ref, wh_ref, bgate_ref, gn_ref, rn_ref, hn_ref,
                         cos_ref, sin_ref, gam_ref, loglb_ref, l1mlb_ref, sg_ref, sr_ref, sh_ref,
                         o_ref, nsg_ref, nsr_ref, nsh_ref,
                         zg, zr, zh, la, tq, tk, te, tv, to, s_t, sn_t):
    h = pl.program_id(0)

    @pl.when(h == 0)
    def _():
        xb = x_ref[...].astype(BF16)
        zg[...] = _dot(xb, wg_ref[...])
        zr[...] = _dot(xb, wr_ref[...])
        zh[...] = _dot(xb, wh_ref[...])
        pre = zg[:, 4 * GROUP_W:5 * GROUP_W] + bgate_ref[...]
        la[...] = _log_sigmoid(pre) * (1.0 / GLA_TAU)

    off = pl.multiple_of(h * SLOT, SLOT)
    sl = pl.ds(off, SLOT)
    tr = (tq, tk, te, tv, to, s_t, sn_t)
    batch = x_ref.shape[0]

    q = zg[:, sl] * (GLA_DK ** -0.5)
    k = zg[:, pl.ds(GROUP_W + off, SLOT)]
    v = zg[:, pl.ds(2 * GROUP_W + off, SLOT)]
    gate = zg[:, pl.ds(3 * GROUP_W + off, SLOT)]
    o = _sample_step(q, k, jnp.exp(la[:, sl]), v, sg_ref, nsg_ref, *tr, GLA_DK, GLA_DV, lambda kk: kk)
    o_ref[:, sl] = (_rms_head_norm(o, gn_ref[pl.ds(h, 1), :], GLA_DV) * _silu(gate)).astype(BF16)

    cs = cos_ref[...]
    sn = sin_ref[...]
    q = _rotate(zr[:, sl], cs, sn)
    k = _rotate(zr[:, pl.ds(GROUP_W + off, SLOT)], cs, sn) * (RET_DK ** -0.5)
    v = zr[:, pl.ds(2 * GROUP_W + off, SLOT)]
    gate = zr[:, pl.ds(3 * GROUP_W + off, SLOT)]
    eg = jnp.broadcast_to(gam_ref[h], (batch, SLOT))
    half = RET_DK // 2
    o = _sample_step(q, k, eg, v, sr_ref, nsr_ref, *tr, RET_DK, RET_DV,
                     lambda kk: kk + jnp.where(kk >= half, SLOT // 2 - half, 0))
    o_ref[:, pl.ds(GROUP_W + off, SLOT)] = (
        _group_head_norm(o, rn_ref[pl.ds(h, 1), :], RET_DV) * _silu(gate)).astype(BF16)

    q = _silu(zh[:, sl])
    hf = zh[:, pl.ds(GROUP_W + off, SLOT)]
    v = zh[:, pl.ds(2 * GROUP_W + off, SLOT)]
    gate = zh[:, pl.ds(3 * GROUP_W + off, SLOT)]
    log_f = _logaddexp(loglb_ref[:, sl], l1mlb_ref[:, sl] + _log_sigmoid(hf))
    f = jnp.exp(log_f)
    o = _sample_step(q, (1.0 - f) * _lane_mask(HGRN_DK), f, v, sh_ref, nsh_ref, *tr,
                     HGRN_DK, HGRN_DV, lambda kk: kk)
    o_ref[:, pl.ds(2 * GROUP_W + off, SLOT)] = (
        _rms_head_norm(o, hn_ref[pl.ds(h, 1), :], HGRN_DV) * _silu(gate)).astype(BF16)


def _sample_mixer_call(x2d, lw, sg, sr, sh):
    batch = x2d.shape[0]
    gsz, hsz = GLA_DK * GLA_DV, HGRN_DK * HGRN_DV
    consts = [lw["w_gla"], lw["w_ret"], lw["w_hgrn"], lw["bgate"], lw["gla_norm"], lw["ret_norm"],
              lw["hgrn_norm"], lw["cos_s"], lw["sin_s"], lw["gamma"], lw["loglb"], lw["l1mlb"]]
    in_specs = ([_const_spec(x2d.shape)] + [_const_spec(c.shape) for c in consts]
                + [pl.BlockSpec((batch, gsz), lambda h: (0, h)),
                   pl.BlockSpec((batch, gsz), lambda h: (0, h)),
                   pl.BlockSpec((batch, hsz), lambda h: (0, h))])
    out_specs = [_const_spec((batch, 3 * GROUP_W)),
                 pl.BlockSpec((batch, gsz), lambda h: (0, h)),
                 pl.BlockSpec((batch, gsz), lambda h: (0, h)),
                 pl.BlockSpec((batch, hsz), lambda h: (0, h))]
    out_shape = [jax.ShapeDtypeStruct((batch, 3 * GROUP_W), BF16),
                 jax.ShapeDtypeStruct((batch, N_HEADS * gsz), F32),
                 jax.ShapeDtypeStruct((batch, N_HEADS * gsz), F32),
                 jax.ShapeDtypeStruct((batch, N_HEADS * hsz), F32)]
    scratch = [pltpu.VMEM((batch, lw["w_gla"].shape[1]), F32), pltpu.VMEM((batch, 4 * GROUP_W), F32),
               pltpu.VMEM((batch, 4 * GROUP_W), F32), pltpu.VMEM((batch, GROUP_W), F32)]
    scratch += [pltpu.VMEM((SLOT, batch), F32) for _ in range(5)]
    scratch += [pltpu.VMEM((gsz, batch), F32), pltpu.VMEM((gsz, batch), F32)]
    return pl.pallas_call(
        _sample_mixer_kernel,
        grid=(N_HEADS,),
        in_specs=in_specs, out_specs=out_specs, out_shape=out_shape, scratch_shapes=scratch,
        compiler_params=pltpu.CompilerParams(dimension_semantics=("arbitrary",), vmem_limit_bytes=VMEM_LIMIT),
        name="sample_mixer",
    )(x2d, *consts, sg.reshape(batch, -1), sr.reshape(batch, -1), sh.reshape(batch, -1))


def _out_kernel(og_ref, or_ref, oh_ref, x_ref, wo_ref, g_ref, b_ref, xt_ref):
    mix = _dot(og_ref[...], wo_ref[0]) + _dot(or_ref[...], wo_ref[1]) + _dot(oh_ref[...], wo_ref[2])
    y = ALPHA * x_ref[...] + mix
    mu = jnp.mean(y, axis=-1, keepdims=True)
    d = y - mu
    var = jnp.mean(d * d, axis=-1, keepdims=True)
    xt_ref[...] = (d * lax.rsqrt(var + NORM_EPS) * g_ref[...] + b_ref[...]).T


def _out_call(og, orr, oh, x2d, lw):
    t = x2d.shape[0]
    tm = min(1024, t)
    row = lambda i: (i, 0)
    return pl.pallas_call(
        _out_kernel,
        grid=(t // tm,),
        in_specs=[pl.BlockSpec((tm, GROUP_W), row), pl.BlockSpec((tm, GROUP_W), row),
                  pl.BlockSpec((tm, GROUP_W), row), pl.BlockSpec((tm, D_MODEL), row),
                  _const_spec(lw["w_out"].shape), _const_spec((1, D_MODEL)), _const_spec((1, D_MODEL))],
        out_specs=pl.BlockSpec((D_MODEL, tm), lambda i: (0, i)),
        out_shape=jax.ShapeDtypeStruct((D_MODEL, t), F32),
        compiler_params=pltpu.CompilerParams(dimension_semantics=("arbitrary",), vmem_limit_bytes=VMEM_LIMIT),
        name="out_proj_ln",
    )(og, orr, oh, x2d, lw["w_out"], lw["ln1_g"], lw["ln1_b"])


def _oddeven_merge(lo, hi, r):
    step = r * 2
    if step < hi - lo:
        yield from _oddeven_merge(lo, hi, step)
        yield from _oddeven_merge(lo + r, hi, step)
        yield from [(i, i + r) for i in range(lo + r, hi - r, step)]
    else:
        yield (lo, lo + r)


def _oddeven_merge_sort(lo, hi):
    if hi - lo >= 1:
        mid = lo + (hi - lo) // 2
        yield from _oddeven_merge_sort(lo, mid)
        yield from _oddeven_merge_sort(mid + 1, hi)
        yield from _oddeven_merge(lo, hi, 1)


_SORT16 = tuple(_oddeven_merge_sort(0, PEER_TOPK - 1))
_BITONIC16 = tuple((i, i + d) for d in (8, 4, 2, 1) for i in range(PEER_TOPK) if i & d == 0)


def _compare_exchange(xs, pairs):
    for i, j in pairs:
        for x in xs:
            x[i], x[j] = jnp.maximum(x[i], x[j]), jnp.minimum(x[i], x[j])
    return xs


def _top16_sorted(tiles):
    xs = _compare_exchange([[a3[i] for i in range(PEER_TOPK)] for a3 in tiles], _SORT16)
    for shift in (4, 2, 1):
        ys = [[pltpu.roll(v, shift, 0) for v in x] for x in xs]
        xs = _compare_exchange([[jnp.maximum(x[i], y[PEER_TOPK - 1 - i]) for i in range(PEER_TOPK)]
                                for x, y in zip(xs, ys)], _BITONIC16)
    return xs


def _sublane_block(rows):
    sub = lax.broadcasted_iota(jnp.int32, (SUBLANE, LANE), 0)
    blk = rows[0]
    for r in range(1, SUBLANE):
        blk = jnp.where(sub == r, rows[r], blk)
    return blk


def _route_tile(a1, a2):
    v1, v2 = _top16_sorted([a1, a2])
    sub = lax.broadcasted_iota(jnp.int32, (SUBLANE, LANE), 0)
    v2a = _sublane_block(v2[0:8])
    v2b = _sublane_block(v2[8:16])
    v1b = _sublane_block(v1[8:16])
    cands = [v1[0] + v2a, v1[0] + v2b]
    for r1 in range(1, 8):
        cands.append(jnp.where(sub < PEER_TOPK // (r1 + 1), v1[r1] + v2a, NEG_INF))
    cands.append(v1b + v2[0])
    filler = jnp.full((SUBLANE, LANE), NEG_INF, F32)
    cand_tile = jnp.stack(cands + [filler] * (PEER_TOPK - len(cands)))
    thr = _top16_sorted([cand_tile])[0][PEER_TOPK - 1]
    top = v1[0] + v2[0]
    z8 = jnp.zeros((SUBLANE, LANE), F32)
    for blk in cands:
        z8 = z8 + jnp.where(blk >= thr, jnp.exp(blk - top), 0.0)
    z = jnp.broadcast_to(jnp.sum(z8, axis=0, keepdims=True), (SUBLANE, LANE))
    n1 = jnp.zeros(a1.shape, F32)
    rank2 = jnp.zeros(a2.shape, F32)
    for r in range(PEER_TOPK):
        n1 = jnp.where(a1 + v2[r] >= thr, r + 1.0, n1)
        rank2 = jnp.where(v2[r] > a2, r + 1.0, rank2)
    e1 = jnp.exp(a1 - v1[0]) * SQRT_HALF
    e2 = jnp.exp(a2 - v2[0]) / z
    return n1, rank2, e1, e2


def _peer_kernel(xt_ref, wqt_ref, sk_ref, u_ref, vt_ref, p_ref, plet_ref, gatet_ref, g2_ref, b2_ref,
                 out_ref, xbf, s_scr, n1, e1, rk2, e2, h_scr, w_scr, acc, *, tn, te, nj):
    j = pl.program_id(1)
    lane_tiles = tn // LANE
    groups = (N_KEYS // SUBLANE, SUBLANE, LANE)

    def lane_slice(lt):
        return pl.ds(pl.multiple_of(lt * LANE, LANE), LANE)

    @pl.when(j == 0)
    def _route():
        xbf[...] = xt_ref[...].astype(BF16)

        def head_pair_body(hp, carry):
            r = pl.multiple_of(hp * 4 * N_KEYS, 4 * N_KEYS)
            qh = _dot(wqt_ref[pl.ds(r, 4 * N_KEYS), :], xbf[...]).astype(BF16)
            for hh in range(2):
                for side in range(2):
                    rows = slice((2 * hh + side) * N_KEYS, (2 * hh + side + 1) * N_KEYS)
                    s_scr[2 * hh + side] = _dot(sk_ref[2 * hp + hh, side], qh[rows])

            def lane_body(i, c):
                hh = i // lane_tiles
                lt = i % lane_tiles
                h = 2 * hp + hh
                lanes = lane_slice(lt)
                n1_t, rank2_t, e1_t, e2_t = _route_tile(s_scr[2 * hh, :, lanes].reshape(groups),
                                                        s_scr[2 * hh + 1, :, lanes].reshape(groups))
                n1[lt, h] = n1_t.reshape(N_KEYS, LANE)
                e1[lt, h] = e1_t.reshape(N_KEYS, LANE)
                rk2[lt, h] = rank2_t.reshape(N_KEYS, LANE).astype(BF16).reshape(rk2.shape[2:])
                e2[lt, h] = e2_t.reshape(N_KEYS, LANE).astype(BF16).reshape(e2.shape[2:])
                return c

            lax.fori_loop(0, 2 * lane_tiles, lane_body, 0)
            return carry

        lax.fori_loop(0, PEER_HEADS // 2, head_pair_body, 0)
        acc[...] = jnp.zeros_like(acc)

    na = te // N_KEYS
    assert na % SUBLANE == 0
    packed = rk2.shape[2:]
    row16 = (1, 2 * SUBLANE, LANE)

    hval = _dot(u_ref[...], xbf[...])
    for l2 in range(lane_tiles):
        hs = hval[:, l2 * LANE:(l2 + 1) * LANE].astype(BF16)
        h_scr[l2] = hs + hs * lax.erf(hs)

    def gate_body(p, carry):
        for grp in range(na // SUBLANE):
            a0 = pl.multiple_of(j * na + grp * SUBLANE, SUBLANE)
            n1blk = [n1[p, h, pl.ds(a0, SUBLANE), :] for h in range(PEER_HEADS)]
            e1blk = [e1[p, h, pl.ds(a0, SUBLANE), :] for h in range(PEER_HEADS)]
            for ai in range(0, SUBLANE, 2):
                gates = [None, None]
                for h in range(PEER_HEADS):
                    rk2h = rk2[p, h]
                    e2h = e2[p, h]
                    for d in range(2):
                        n1a = jnp.broadcast_to(n1blk[h][ai + d:ai + d + 1, :], row16[1:]).astype(BF16).reshape(row16)
                        e1a = jnp.broadcast_to(e1blk[h][ai + d:ai + d + 1, :], row16[1:]).astype(BF16).reshape(row16)
                        term = jnp.where(rk2h < n1a, e2h, jnp.zeros_like(e2h)) * e1a
                        gates[d] = term if h == 0 else gates[d] + term
                for d in range(2):
                    r0 = (grp * SUBLANE + ai + d) * N_KEYS
                    act = h_scr[p, r0:r0 + N_KEYS, :].reshape(packed)
                    w_scr[p, r0:r0 + N_KEYS, :] = (gates[d] * act).reshape(N_KEYS, LANE)
        return carry

    lax.fori_loop(0, lane_tiles, gate_body, 0)
    acc[...] += _dot(vt_ref[...], jnp.concatenate([w_scr[l2] for l2 in range(lane_tiles)], axis=1))

    @pl.when(j == nj - 1)
    def _finish():
        def norm_body(lt, c):
            lanes = lane_slice(lt)
            y = ALPHA * xt_ref[:, lanes] + acc[:, lanes]
            mu = jnp.mean(y, axis=0, keepdims=True)
            d = y - mu
            var = jnp.mean(d * d, axis=0, keepdims=True)
            yn = d * lax.rsqrt(var + NORM_EPS) * g2_ref[...] + b2_ref[...]
            acc[:, lanes] = yn
            xbf[:, lanes] = yn.astype(BF16)
            return c

        lax.fori_loop(0, lane_tiles, norm_body, 0, unroll=True)
        emb = _dot_nt(plet_ref[...], p_ref[...].astype(BF16))
        gt = _dot(gatet_ref[...], xbf[...])
        out_ref[...] = (acc[...] + emb * _sigmoid(gt)).T


def _peer_call(xt, p2d, lw, tables):
    u_all, vt_all, layer = tables
    t = xt.shape[1]
    tn = min(512, t)
    te = PEER_TE
    nj = vt_all.shape[1]
    lane_tiles = tn // LANE
    once = dict(pipeline_mode=pl.Buffered(1))
    in_specs = [
        pl.BlockSpec((D_MODEL, tn), lambda i, j: (0, i)),
        pl.BlockSpec(lw["wqt"].shape, lambda i, j: (0, 0), **once),
        pl.BlockSpec(lw["subkeys"].shape, lambda i, j: (0, 0, 0, 0), **once),
        pl.BlockSpec((None, te, D_MODEL), lambda i, j: (layer, j, 0)),
        pl.BlockSpec((None, None, D_MODEL, te), lambda i, j: (layer, j, 0, 0)),
        pl.BlockSpec((tn, PLE_DIM), lambda i, j: (i, 0)),
        pl.BlockSpec(lw["plet"].shape, lambda i, j: (0, 0), **once),
        pl.BlockSpec(lw["gatet"].shape, lambda i, j: (0, 0), **once),
        pl.BlockSpec((D_MODEL, LANE), lambda i, j: (0, 0), **once),
        pl.BlockSpec((D_MODEL, LANE), lambda i, j: (0, 0), **once),
    ]
    route = (lane_tiles, PEER_HEADS, N_KEYS, LANE)
    route_packed = (lane_tiles, PEER_HEADS, N_KEYS // (2 * SUBLANE), 2 * SUBLANE, LANE)
    scratch = [pltpu.VMEM((D_MODEL, tn), BF16), pltpu.VMEM((4, N_KEYS, tn), F32),
               pltpu.VMEM(route, F32), pltpu.VMEM(route, F32),
               pltpu.VMEM(route_packed, BF16), pltpu.VMEM(route_packed, BF16),
               pltpu.VMEM((lane_tiles, te, LANE), BF16), pltpu.VMEM((lane_tiles, te, LANE), BF16),
               pltpu.VMEM((D_MODEL, tn), F32)]
    return pl.pallas_call(
        functools.partial(_peer_kernel, tn=tn, te=te, nj=nj),
        grid=(t // tn, nj),
        in_specs=in_specs,
        out_specs=pl.BlockSpec((tn, D_MODEL), lambda i, j: (i, 0)),
        out_shape=jax.ShapeDtypeStruct((t, D_MODEL), F32),
        scratch_shapes=scratch,
        compiler_params=pltpu.CompilerParams(dimension_semantics=("arbitrary", "arbitrary"),
                                             vmem_limit_bytes=VMEM_LIMIT),
        name="peer_ffn_ln_ple",
    )(xt, lw["wqt"], lw["subkeys"], u_all, vt_all, p2d, lw["plet"], lw["gatet"], lw["ln2_g"], lw["ln2_b"])


def _head_slots(w, d):
    lead = w.shape[:-1]
    pad = [(0, 0)] * (len(lead) + 1) + [(0, SLOT - d)]
    return jnp.pad(w.reshape(*lead, N_HEADS, d), pad).reshape(*lead, GROUP_W)


def _rope_slots(w, d):
    lead = w.shape[:-1]
    half = d // 2
    pad = [(0, 0)] * (len(lead) + 2) + [(0, SLOT // 2 - half)]
    return jnp.pad(w.reshape(*lead, N_HEADS, 2, half), pad).reshape(*lead, GROUP_W)


def _row_slots(w, d):
    depth, _, cols = w.shape
    w = jnp.pad(w.reshape(depth, N_HEADS, d, cols), ((0, 0), (0, 0), (0, SLOT - d), (0, 0)))
    return w.reshape(depth, GROUP_W, cols)


def _rope_tables(pos):
    half = RET_DK // 2
    inv = 1.0 / (ROPE_BASE ** (jnp.arange(0, RET_DK, 2, dtype=F32) / RET_DK))
    ang = pos[:, None] * inv[None, :]
    pad = ((0, 0), (0, SLOT // 2 - half))
    cos = jnp.pad(jnp.cos(ang), pad)
    sin = jnp.pad(jnp.sin(ang), pad)
    return jnp.concatenate([cos, cos], axis=1), jnp.concatenate([-sin, sin], axis=1)


def _retention_constants():
    log_gamma = jnp.log1p(-jnp.exp2(-5.0 - jnp.arange(N_HEADS, dtype=F32)))
    i = jnp.arange(CHUNK, dtype=F32)
    diff = i[:, None] - i[None, :]
    lg = log_gamma[:, None, None]
    dmat = jnp.where(diff >= 0, jnp.exp(jnp.where(diff >= 0, diff, 0.0) * lg), 0.0)
    qdec = jnp.broadcast_to(jnp.exp((i[None, :, None] + 1.0) * lg), (N_HEADS, CHUNK, SLOT))
    kdec = jnp.broadcast_to(jnp.exp((CHUNK - 1.0 - i[None, :, None]) * lg), (N_HEADS, CHUNK, SLOT))
    sdec = jnp.broadcast_to(jnp.exp(CHUNK * lg), (N_HEADS, 1, SLOT))
    gamma = jnp.broadcast_to(jnp.exp(lg), (N_HEADS, 1, SLOT))
    return dmat, qdec, kdec, sdec, gamma


def _stacked_weights(lb, w_in, gla_w_gate, gla_b_gate, gla_norm, ret_norm, hgrn_norm, w_out,
                     ln1_g, ln1_b, ln2_g, ln2_b, peer_w_q, peer_subkeys, peer_u, peer_v, ple_proj, ple_gate):
    depth = w_in.shape[0]
    sizes = (N_HEADS * GLA_DK, N_HEADS * GLA_DK, N_HEADS * GLA_DV, N_HEADS * GLA_DV, GLA_LOWRANK,
             N_HEADS * RET_DK, N_HEADS * RET_DK, N_HEADS * RET_DV, N_HEADS * RET_DV,
             N_HEADS * HGRN_DK, N_HEADS * HGRN_DK, N_HEADS * HGRN_DV, N_HEADS * HGRN_DV)
    offs = [int(c) for c in np.cumsum(sizes)[:-1]]
    gq, gk, gv, gg, glr, rq, rk, rv, rg, hq, hf, hi, hg = jnp.split(w_in, offs, axis=2)
    sw = {}
    gate_w = jnp.einsum("dir,drk->dik", glr, gla_w_gate, precision=lax.Precision.HIGHEST)
    sw["w_gla"] = jnp.concatenate(
        [_head_slots(gq, GLA_DK), _head_slots(gk, GLA_DK), _head_slots(gv, GLA_DV), _head_slots(gg, GLA_DV),
         _head_slots(gate_w, GLA_DK)], axis=2).astype(BF16)
    sw["w_ret"] = jnp.concatenate(
        [_rope_slots(rq, RET_DK), _rope_slots(rk, RET_DK), _head_slots(rv, RET_DV), _head_slots(rg, RET_DV)],
        axis=2).astype(BF16)
    sw["w_hgrn"] = jnp.concatenate([_head_slots(w, HGRN_DK) for w in (hq, hf, hi, hg)], axis=2).astype(BF16)
    sw["bgate"] = _head_slots(gla_b_gate, GLA_DK)[:, None, :]
    sw["gla_norm"] = jnp.pad(gla_norm, ((0, 0), (0, 0), (0, SLOT - GLA_DV)))
    sw["ret_norm"] = jnp.pad(ret_norm, ((0, 0), (0, 0), (0, SLOT - RET_DV)))
    sw["hgrn_norm"] = jnp.pad(hgrn_norm, ((0, 0), (0, 0), (0, SLOT - HGRN_DV)))
    lbh = lb.reshape(depth, 1, N_HEADS, HGRN_DK)
    pad = ((0, 0), (0, 0), (0, 0), (0, SLOT - HGRN_DK))
    sw["loglb"] = jnp.pad(jnp.log(lbh), pad, constant_values=-1.0).reshape(depth, 1, GROUP_W)
    sw["l1mlb"] = jnp.pad(jnp.log1p(-lbh), pad, constant_values=-1.0).reshape(depth, 1, GROUP_W)
    g_rows, r_rows = N_HEADS * GLA_DV, N_HEADS * RET_DV
    sw["w_out"] = jnp.stack([
        _row_slots(w_out[:, :g_rows], GLA_DV), _row_slots(w_out[:, g_rows:g_rows + r_rows], RET_DV),
        _row_slots(w_out[:, g_rows + r_rows:], HGRN_DV)], axis=1).astype(BF16)
    sw["ln1_g"], sw["ln1_b"] = ln1_g[:, None, :], ln1_b[:, None, :]
    sw["ln2_g"] = jnp.broadcast_to(ln2_g[:, :, None], (depth, D_MODEL, LANE))
    sw["ln2_b"] = jnp.broadcast_to(ln2_b[:, :, None], (depth, D_MODEL, LANE))
    sw["wqt"] = jnp.swapaxes(peer_w_q, 1, 2).astype(BF16)
    sw["subkeys"] = peer_subkeys.astype(BF16)
    sw["u"] = (peer_u * SQRT_HALF).astype(BF16)
    sw["vt"] = jnp.swapaxes(peer_v.astype(BF16).reshape(depth, -1, PEER_TE, D_MODEL), 2, 3)
    sw["plet"] = jnp.swapaxes(ple_proj, 1, 2).astype(BF16)
    sw["gatet"] = jnp.swapaxes(ple_gate, 1, 2).astype(BF16)
    return sw


_BIG_TABLES = ("u", "vt")


def _unslot_state(st, dk, dv, rope=False):
    if rope:
        half = dk // 2
        st = jnp.concatenate([st[..., :half], st[..., SLOT // 2:SLOT // 2 + half]], axis=-1)
    return jnp.swapaxes(st[..., :dv, :dk], -2, -1)


def kernel(x_prompt, x_sample, p_prompt, p_sample, state_gla, state_ret, state_hgrn, w_in, gla_w_gate,
           gla_b_gate, gla_norm, ret_norm, hgrn_lb_logits, hgrn_norm, w_out, ln1_g, ln1_b, ln2_g, ln2_b,
           peer_w_q, peer_subkeys, peer_u, peer_v, ple_proj, ple_gate):
    bp, lp, _ = x_prompt.shape
    bs = x_sample.shape[0]
    assert x_sample.shape[1] == 1 and lp % CHUNK == 0

    lb = jnp.cumsum(jax.nn.softmax(hgrn_lb_logits.astype(F32), axis=0), axis=0)
    lb = lb - lb[0:1]
    tri_np, masks_np = _chunk_constants()
    tri, masks = jnp.asarray(tri_np, BF16), jnp.asarray(masks_np)
    dmat, qdec, kdec, sdec, gamma = _retention_constants()
    cos_p, sin_p = _rope_tables(jnp.arange(lp, dtype=F32))
    cos_s, sin_s = _rope_tables(PAST_LEN + jnp.arange(1, dtype=F32))
    sw = _stacked_weights(lb, w_in, gla_w_gate, gla_b_gate, gla_norm, ret_norm, hgrn_norm, w_out,
                          ln1_g, ln1_b, ln2_g, ln2_b, peer_w_q, peer_subkeys, peer_u, peer_v,
                          ple_proj, ple_gate)

    xp = x_prompt.reshape(bp * lp, D_MODEL)
    xs = x_sample.reshape(bs, D_MODEL)
    gla_p, ret_p, hgrn_p, gla_s, ret_s, hgrn_s = [], [], [], [], [], []
    for i in range(DEPTH):
        lw = {k: v[i] for k, v in sw.items() if k not in _BIG_TABLES}
        lw.update(cos_s=cos_s, sin_s=sin_s, gamma=gamma)
        tables = (sw["u"], sw["vt"], i)

        og, sg = _prompt_mixer_call(
            _gla_prompt_kernel, xp, bp, lp, lw["w_gla"],
            [lw["bgate"], lw["gla_norm"], tri, masks],
            [lw["w_gla"].shape[1], GROUP_W], "gla_prompt")
        orr, sr = _prompt_mixer_call(
            _ret_prompt_kernel, xp, bp, lp, lw["w_ret"],
            [(cos_p, None), (sin_p, None), lw["ret_norm"], dmat, qdec, kdec, sdec],
            [4 * GROUP_W], "ret_prompt")
        oh, sh = _prompt_mixer_call(
            _hgrn_prompt_kernel, xp, bp, lp, lw["w_hgrn"],
            [lw["loglb"], lw["l1mlb"], lw["hgrn_norm"], tri, masks],
            [4 * GROUP_W], "hgrn_prompt")
        xp = _peer_call(_out_call(og, orr, oh, xp, lw), p_prompt[i].reshape(bp * lp, PLE_DIM), lw, tables)
        gla_p.append(sg)
        ret_p.append(sr)
        hgrn_p.append(sh)

        o_s, nsg, nsr, nsh = _sample_mixer_call(xs, lw, state_gla[i], state_ret[i], state_hgrn[i])
        xs = _peer_call(
            _out_call(o_s[:, :GROUP_W], o_s[:, GROUP_W:2 * GROUP_W], o_s[:, 2 * GROUP_W:], xs, lw),
            p_sample[i].reshape(bs, PLE_DIM), lw, tables)
        gla_s.append(nsg)
        ret_s.append(nsr)
        hgrn_s.append(nsh)

    return (xp.reshape(bp, lp, D_MODEL), xs.reshape(bs, 1, D_MODEL),
            _unslot_state(jnp.stack(gla_p), GLA_DK, GLA_DV),
            _unslot_state(jnp.stack(ret_p), RET_DK, RET_DV, rope=True),
            _unslot_state(jnp.stack(hgrn_p), HGRN_DK, HGRN_DV),
            jnp.stack(gla_s).reshape(DEPTH, bs, N_HEADS, GLA_DK, GLA_DV),
            jnp.stack(ret_s).reshape(DEPTH, bs, N_HEADS, RET_DK, RET_DV),
            jnp.stack(hgrn_s).reshape(DEPTH, bs, N_HEADS, HGRN_DK, HGRN_DV))
```

```python
import functools

import numpy as np
import jax
import jax.numpy as jnp
from jax import lax
from jax.experimental import pallas as pl
from jax.experimental.pallas import tpu as pltpu

F32 = jnp.float32
BF16 = jnp.bfloat16

D_MODEL = 1024
DEPTH = 2
PAST_LEN = 16384
N_HEADS = 4
GLA_DK, GLA_DV = 48, 96
RET_DK, RET_DV = 48, 96
HGRN_DK, HGRN_DV = 64, 64
GLA_LOWRANK = 16
GLA_TAU = 16.0
ROPE_BASE = 10000.0
CHUNK = 64
CHUNK_UNROLL = 8
PEER_HEADS = 8
N_KEYS = 128
PEER_TOPK = 16
PEER_TE = 2048
ROUTE_HEADS_PER_TRIP = 4
PLE_DIM = 256
ALPHA = (2 * DEPTH) ** 0.25
NORM_EPS = 1e-5

LANE = 128
SUBLANE = 8
SLOT = LANE
GROUP_W = N_HEADS * SLOT
VMEM_LIMIT = 56 * 1024 * 1024
NEG_INF = float("-inf")


def _dot(a, b):
    return jnp.dot(a, b, preferred_element_type=F32)


def _dot_nt(a, b):
    return lax.dot_general(a, b, (((1,), (1,)), ((), ())), preferred_element_type=F32)


def _dot_tn(a, b):
    return lax.dot_general(a, b, (((0,), (0,)), ((), ())), preferred_element_type=F32)


def _sigmoid(x):
    return jax.nn.sigmoid(x)


def _silu(x):
    return x * _sigmoid(x)


def _log_sigmoid(x):
    return jnp.minimum(x, 0.0) - jnp.log1p(jnp.exp(-jnp.abs(x)))


def _logaddexp(a, c):
    amax = jnp.maximum(a, c)
    delta = a - c
    return jnp.where(jnp.isnan(delta), a + c, amax + jnp.log1p(jnp.exp(-jnp.abs(delta))))


SQRT_HALF = np.float32(0.7071067811865476)


def _lane_mask(n):
    return (lax.broadcasted_iota(jnp.int32, (1, LANE), 1) < n).astype(F32)


def _rms_head_norm(o, g_row, dv):
    ms = jnp.sum(o * o, axis=-1, keepdims=True) * (1.0 / dv)
    return o * lax.rsqrt(ms + NORM_EPS) * g_row


def _group_head_norm(o, g_row, dv):
    mask = _lane_mask(dv)
    mu = jnp.sum(o, axis=-1, keepdims=True) * (1.0 / dv)
    d = (o - mu) * mask
    var = jnp.sum(d * d, axis=-1, keepdims=True) * (1.0 / dv)
    return d * lax.rsqrt(var + NORM_EPS) * g_row


def _chunk_constants():
    i = np.arange(CHUNK)[:, None]
    t = np.arange(CHUNK)[None, :]
    masks = [i == t]
    half = CHUNK // 2
    while half >= 1:
        blk = i // (2 * half)
        second = (i % (2 * half)) >= half
        masks.append(second & ((t % (2 * half)) < half) & (blk == t // (2 * half)))
        half //= 2
    return (t <= i).astype(np.float32), np.stack(masks).astype(np.float32)


def _cumsum_rows(tri_bf, g):
    hi = g.astype(BF16)
    r1 = g - hi.astype(F32)
    mid = r1.astype(BF16)
    lo = (r1 - mid.astype(F32)).astype(BF16)
    return _dot(tri_bf, hi) + _dot(tri_bf, mid) + _dot(tri_bf, lo)


def _level_factors(b, g):
    width = b.shape[1]
    grouped = (CHUNK // SUBLANE, SUBLANE, width)
    row = lax.broadcasted_iota(jnp.int32, b.shape, 0)
    sub = lax.broadcasted_iota(jnp.int32, grouped, 1)
    b3 = b.reshape(grouped)

    def sub_ref(r):
        return jnp.broadcast_to(b3[:, r:r + 1, :], grouped)

    out = []
    half = CHUNK // 2
    while half >= SUBLANE:
        ref = jnp.concatenate(
            [jnp.broadcast_to(b[m * 2 * half + half - 1:m * 2 * half + half], (2 * half, width))
             for m in range(CHUNK // (2 * half))], axis=0)
        out.append(jnp.exp(-jnp.abs(b - ref)))
        half //= 2
    out.append(jnp.exp(-jnp.abs(b3 - sub_ref(3))).reshape(b.shape))
    out.append(jnp.exp(-jnp.abs(b3 - jnp.where(sub < 4, sub_ref(1), sub_ref(5)))).reshape(b.shape))
    out.append(jnp.exp(jnp.where(row % 2 == 1, g, 0.0)))
    return out


def _vector_decay_chunk(q, k, v, g, st_ref, tri_bf, masks_ref):
    heads = [slice(h * SLOT, (h + 1) * SLOT) for h in range(N_HEADS)]
    b = _cumsum_rows(tri_bf, g)
    b_last = b[CHUNK - 1:CHUNK]
    factors = _level_factors(b, g)
    qs = [q.astype(BF16)] + [(q * f).astype(BF16) for f in factors]
    ks = [k.astype(BF16)] + [(k * f).astype(BF16) for f in factors]
    q_in = (q * jnp.exp(b)).astype(BF16)
    k_out = (k * jnp.exp(b_last - b)).astype(BF16)
    decay = jnp.exp(b_last)
    vb = v.astype(BF16)
    scores = []
    for sl in heads:
        sc = masks_ref[0] * _dot_nt(qs[0][:, sl], ks[0][:, sl])
        for l in range(len(factors)):
            sc = sc + masks_ref[1 + l] * _dot_nt(qs[1 + l][:, sl], ks[1 + l][:, sl])
        scores.append(sc.astype(BF16))
    outs = []
    for h, sl in enumerate(heads):
        st = st_ref[0, h]
        outs.append(_dot(scores[h], vb[:, sl]) + _dot_nt(q_in[:, sl], st.astype(BF16)))
        st_ref[0, h] = st * decay[:, sl] + _dot_tn(vb[:, sl], k_out[:, sl])
    return outs


def _gla_prompt_kernel(x_ref, w_ref, bg_ref, nrm_ref, tri_ref, masks_ref,
                       o_ref, st_ref, z_scr, la_scr, *, seg):
    @pl.when(pl.program_id(1) == 0)
    def _():
        st_ref[...] = jnp.zeros_like(st_ref)

    z_scr[...] = _dot(x_ref[...].astype(BF16), w_ref[...])
    pre = z_scr[:, 4 * GROUP_W:5 * GROUP_W] + bg_ref[...]
    la_scr[...] = _log_sigmoid(pre) * (1.0 / GLA_TAU)
    tri = tri_ref[...]

    def body(c, carry):
        r0 = pl.multiple_of(c * CHUNK, CHUNK)
        rows = pl.ds(r0, CHUNK)
        q = z_scr[rows, 0:GROUP_W] * (GLA_DK ** -0.5)
        k = z_scr[rows, GROUP_W:2 * GROUP_W]
        v = z_scr[rows, 2 * GROUP_W:3 * GROUP_W]
        outs = _vector_decay_chunk(q, k, v, la_scr[rows, :], st_ref, tri, masks_ref)
        for h in range(N_HEADS):
            gate = z_scr[rows, 3 * GROUP_W + h * SLOT:3 * GROUP_W + (h + 1) * SLOT]
            on = _rms_head_norm(outs[h], nrm_ref[h:h + 1, :], GLA_DV)
            o_ref[rows, h * SLOT:(h + 1) * SLOT] = (on * _silu(gate)).astype(BF16)
        return carry

    lax.fori_loop(0, seg // CHUNK, body, 0, unroll=CHUNK_UNROLL)


def _hgrn_prompt_kernel(x_ref, w_ref, loglb_ref, l1mlb_ref, nrm_ref, tri_ref, masks_ref,
                        o_ref, st_ref, z_scr, *, seg):
    @pl.when(pl.program_id(1) == 0)
    def _():
        st_ref[...] = jnp.zeros_like(st_ref)

    z_scr[...] = _dot(x_ref[...].astype(BF16), w_ref[...])
    tri = tri_ref[...]
    kmask = jnp.concatenate([_lane_mask(HGRN_DK)] * N_HEADS, axis=1)

    def body(c, carry):
        r0 = pl.multiple_of(c * CHUNK, CHUNK)
        rows = pl.ds(r0, CHUNK)
        q = _silu(z_scr[rows, 0:GROUP_W])
        v = z_scr[rows, 2 * GROUP_W:3 * GROUP_W]
        log_f = _logaddexp(loglb_ref[...], l1mlb_ref[...] + _log_sigmoid(z_scr[rows, GROUP_W:2 * GROUP_W]))
        k = (1.0 - jnp.exp(log_f)) * kmask
        outs = _vector_decay_chunk(q, k, v, log_f, st_ref, tri, masks_ref)
        for h in range(N_HEADS):
            sl = slice(h * SLOT, (h + 1) * SLOT)
            gate = z_scr[rows, 3 * GROUP_W + h * SLOT:3 * GROUP_W + (h + 1) * SLOT]
            on = _rms_head_norm(outs[h], nrm_ref[h:h + 1, :], HGRN_DV)
            o_ref[rows, sl] = (on * _silu(gate)).astype(BF16)
        return carry

    lax.fori_loop(0, seg // CHUNK, body, 0, unroll=CHUNK_UNROLL)


def _rotate(t, cs, sn):
    return t * cs + pltpu.roll(t, SLOT // 2, 1) * sn


def _ret_prompt_kernel(x_ref, w_ref, cos_ref, sin_ref, nrm_ref, dmat_ref, qdec_ref, kdec_ref, sdec_ref,
                       o_ref, st_ref, z_scr, *, seg):
    @pl.when(pl.program_id(1) == 0)
    def _():
        st_ref[...] = jnp.zeros_like(st_ref)

    z_scr[...] = _dot(x_ref[...].astype(BF16), w_ref[...])

    def body(c, carry):
        r0 = pl.multiple_of(c * CHUNK, CHUNK)
        rows = pl.ds(r0, CHUNK)
        cs = cos_ref[rows, :]
        sn = sin_ref[rows, :]
        heads = [slice(h * SLOT, (h + 1) * SLOT) for h in range(N_HEADS)]
        qs = [_rotate(z_scr[rows, h * SLOT:(h + 1) * SLOT], cs, sn) for h in range(N_HEADS)]
        ks = [_rotate(z_scr[rows, GROUP_W + h * SLOT:GROUP_W + (h + 1) * SLOT], cs, sn) * (RET_DK ** -0.5)
              for h in range(N_HEADS)]
        vb = z_scr[rows, 2 * GROUP_W:3 * GROUP_W].astype(BF16)
        scores = [(dmat_ref[h] * _dot_nt(qs[h].astype(BF16), ks[h].astype(BF16))).astype(BF16)
                  for h in range(N_HEADS)]
        q_in = [(qs[h] * qdec_ref[h]).astype(BF16) for h in range(N_HEADS)]
        k_out = [(ks[h] * kdec_ref[h]).astype(BF16) for h in range(N_HEADS)]
        outs = []
        for h, sl in enumerate(heads):
            st = st_ref[0, h]
            outs.append(_dot(scores[h], vb[:, sl]) + _dot_nt(q_in[h], st.astype(BF16)))
            st_ref[0, h] = st * sdec_ref[h] + _dot_tn(vb[:, sl], k_out[h])
        for h, sl in enumerate(heads):
            gate = z_scr[rows, 3 * GROUP_W + h * SLOT:3 * GROUP_W + (h + 1) * SLOT]
            on = _group_head_norm(outs[h], nrm_ref[h:h + 1, :], RET_DV)
            o_ref[rows, sl] = (on * _silu(gate)).astype(BF16)
        return carry

    lax.fori_loop(0, seg // CHUNK, body, 0, unroll=CHUNK_UNROLL)


def _const_spec(shape):
    nd = len(shape)
    return pl.BlockSpec(shape, lambda *_: (0,) * nd)


def _prompt_mixer_call(kernel, x2d, batch, seq, w, extras, scratch_widths, name):
    seg = min(512, seq)
    nseg = seq // seg
    in_specs = [pl.BlockSpec((seg, D_MODEL), lambda b, s: (b * nseg + s, 0)), _const_spec(w.shape)]
    args = [x2d, w]
    for e in extras:
        if isinstance(e, tuple):
            arr, _ = e
            in_specs.append(pl.BlockSpec((seg, arr.shape[1]), lambda b, s: (s, 0)))
            args.append(arr)
        else:
            in_specs.append(_const_spec(e.shape))
            args.append(e)
    return pl.pallas_call(
        functools.partial(kernel, seg=seg),
        grid=(batch, nseg),
        in_specs=in_specs,
        out_specs=[pl.BlockSpec((seg, GROUP_W), lambda b, s: (b * nseg + s, 0)),
                   pl.BlockSpec((1, N_HEADS, SLOT, SLOT), lambda b, s: (b, 0, 0, 0))],
        out_shape=[jax.ShapeDtypeStruct((batch * seq, GROUP_W), BF16),
                   jax.ShapeDtypeStruct((batch, N_HEADS, SLOT, SLOT), F32)],
        scratch_shapes=[pltpu.VMEM((seg, wd), F32) for wd in scratch_widths],
        compiler_params=pltpu.CompilerParams(dimension_semantics=("arbitrary", "arbitrary"),
                                             vmem_limit_bytes=VMEM_LIMIT),
        name=name,
    )(*args)


def _sample_step(q, k, eg, v, s_ref, ns_ref, tq, tk, te, tv, to, s_t, sn_t, dk, dv, row_of):
    tq[...] = q.T
    tk[...] = k.T
    te[...] = eg.T
    tv[...] = v.T
    s_t[0:dk * dv, :] = s_ref[...].T
    vt = tv[0:dv, :]

    def body(kk, oacc):
        kr = row_of(kk)
        r = pl.multiple_of(kk * dv, SUBLANE)
        sn = s_t[pl.ds(r, dv), :] * te[pl.ds(kr, 1), :] + tk[pl.ds(kr, 1), :] * vt
        sn_t[pl.ds(r, dv), :] = sn
        return oacc + tq[pl.ds(kr, 1), :] * sn

    o_t = lax.fori_loop(0, dk, body, jnp.zeros((dv, q.shape[0]), F32))
    ns_ref[...] = sn_t[0:dk * dv, :].T
    to[...] = jnp.zeros_like(to)
    to[0:dv, :] = o_t
    return to[...].T


def _sample_mixer_kernel(x_ref, wg_ref, wr_ref, wh_ref, bgate_ref, gn_ref, rn_ref, hn_ref,
                         cos_ref, sin_ref, gam_ref, loglb_ref, l1mlb_ref, sg_ref, sr_ref, sh_ref,
                         o_ref, nsg_ref, nsr_ref, nsh_ref,
                         zg, zr, zh, la, tq, tk, te, tv, to, s_t, sn_t):
    h = pl.program_id(0)

    @pl.when(h == 0)
    def _():
        xb = x_ref[...].astype(BF16)
        zg[...] = _dot(xb, wg_ref[...])
        zr[...] = _dot(xb, wr_ref[...])
        zh[...] = _dot(xb, wh_ref[...])
        pre = zg[:, 4 * GROUP_W:5 * GROUP_W] + bgate_ref[...]
        la[...] = _log_sigmoid(pre) * (1.0 / GLA_TAU)

    off = pl.multiple_of(h * SLOT, SLOT)
    sl = pl.ds(off, SLOT)
    tr = (tq, tk, te, tv, to, s_t, sn_t)
    batch = x_ref.shape[0]

    q = zg[:, sl] * (GLA_DK ** -0.5)
    k = zg[:, pl.ds(GROUP_W + off, SLOT)]
    v = zg[:, pl.ds(2 * GROUP_W + off, SLOT)]
    gate = zg[:, pl.ds(3 * GROUP_W + off, SLOT)]
    o = _sample_step(q, k, jnp.exp(la[:, sl]), v, sg_ref, nsg_ref, *tr, GLA_DK, GLA_DV, lambda kk: kk)
    o_ref[:, sl] = (_rms_head_norm(o, gn_ref[pl.ds(h, 1), :], GLA_DV) * _silu(gate)).astype(BF16)

    cs = cos_ref[...]
    sn = sin_ref[...]
    q = _rotate(zr[:, sl], cs, sn)
    k = _rotate(zr[:, pl.ds(GROUP_W + off, SLOT)], cs, sn) * (RET_DK ** -0.5)
    v = zr[:, pl.ds(2 * GROUP_W + off, SLOT)]
    gate = zr[:, pl.ds(3 * GROUP_W + off, SLOT)]
    eg = jnp.broadcast_to(gam_ref[h], (batch, SLOT))
    half = RET_DK // 2
    o = _sample_step(q, k, eg, v, sr_ref, nsr_ref, *tr, RET_DK, RET_DV,
                     lambda kk: kk + jnp.where(kk >= half, SLOT // 2 - half, 0))
    o_ref[:, pl.ds(GROUP_W + off, SLOT)] = (
        _group_head_norm(o, rn_ref[pl.ds(h, 1), :], RET_DV) * _silu(gate)).astype(BF16)

    q = _silu(zh[:, sl])
    hf = zh[:, pl.ds(GROUP_W + off, SLOT)]
    v = zh[:, pl.ds(2 * GROUP_W + off, SLOT)]
    gate = zh[:, pl.ds(3 * GROUP_W + off, SLOT)]
    log_f = _logaddexp(loglb_ref[:, sl], l1mlb_ref[:, sl] + _log_sigmoid(hf))
    f = jnp.exp(log_f)
    o = _sample_step(q, (1.0 - f) * _lane_mask(HGRN_DK), f, v, sh_ref, nsh_ref, *tr,
                     HGRN_DK, HGRN_DV, lambda kk: kk)
    o_ref[:, pl.ds(2 * GROUP_W + off, SLOT)] = (
        _rms_head_norm(o, hn_ref[pl.ds(h, 1), :], HGRN_DV) * _silu(gate)).astype(BF16)


def _sample_mixer_call(x2d, lw, sg, sr, sh):
    batch = x2d.shape[0]
    gsz, hsz = GLA_DK * GLA_DV, HGRN_DK * HGRN_DV
    consts = [lw["w_gla"], lw["w_ret"], lw["w_hgrn"], lw["bgate"], lw["gla_norm"], lw["ret_norm"],
              lw["hgrn_norm"], lw["cos_s"], lw["sin_s"], lw["gamma"], lw["loglb"], lw["l1mlb"]]
    in_specs = ([_const_spec(x2d.shape)] + [_const_spec(c.shape) for c in consts]
                + [pl.BlockSpec((batch, gsz), lambda h: (0, h)),
                   pl.BlockSpec((batch, gsz), lambda h: (0, h)),
                   pl.BlockSpec((batch, hsz), lambda h: (0, h))])
    out_specs = [_const_spec((batch, 3 * GROUP_W)),
                 pl.BlockSpec((batch, gsz), lambda h: (0, h)),
                 pl.BlockSpec((batch, gsz), lambda h: (0, h)),
                 pl.BlockSpec((batch, hsz), lambda h: (0, h))]
    out_shape = [jax.ShapeDtypeStruct((batch, 3 * GROUP_W), BF16),
                 jax.ShapeDtypeStruct((batch, N_HEADS * gsz), F32),
                 jax.ShapeDtypeStruct((batch, N_HEADS * gsz), F32),
                 jax.ShapeDtypeStruct((batch, N_HEADS * hsz), F32)]
    scratch = [pltpu.VMEM((batch, lw["w_gla"].shape[1]), F32), pltpu.VMEM((batch, 4 * GROUP_W), F32),
               pltpu.VMEM((batch, 4 * GROUP_W), F32), pltpu.VMEM((batch, GROUP_W), F32)]
    scratch += [pltpu.VMEM((SLOT, batch), F32) for _ in range(5)]
    scratch += [pltpu.VMEM((gsz, batch), F32), pltpu.VMEM((gsz, batch), F32)]
    return pl.pallas_call(
        _sample_mixer_kernel,
        grid=(N_HEADS,),
        in_specs=in_specs, out_specs=out_specs, out_shape=out_shape, scratch_shapes=scratch,
        compiler_params=pltpu.CompilerParams(dimension_semantics=("arbitrary",), vmem_limit_bytes=VMEM_LIMIT),
        name="sample_mixer",
    )(x2d, *consts, sg.reshape(batch, -1), sr.reshape(batch, -1), sh.reshape(batch, -1))


def _out_kernel(og_ref, or_ref, oh_ref, x_ref, wo_ref, g_ref, b_ref, xt_ref):
    mix = _dot(og_ref[...], wo_ref[0]) + _dot(or_ref[...], wo_ref[1]) + _dot(oh_ref[...], wo_ref[2])
    y = ALPHA * x_ref[...] + mix
    mu = jnp.mean(y, axis=-1, keepdims=True)
    d = y - mu
    var = jnp.mean(d * d, axis=-1, keepdims=True)
    xt_ref[...] = (d * lax.rsqrt(var + NORM_EPS) * g_ref[...] + b_ref[...]).T


def _out_call(og, orr, oh, x2d, lw):
    t = x2d.shape[0]
    tm = min(1024, t)
    row = lambda i: (i, 0)
    return pl.pallas_call(
        _out_kernel,
        grid=(t // tm,),
        in_specs=[pl.BlockSpec((tm, GROUP_W), row), pl.BlockSpec((tm, GROUP_W), row),
                  pl.BlockSpec((tm, GROUP_W), row), pl.BlockSpec((tm, D_MODEL), row),
                  _const_spec(lw["w_out"].shape), _const_spec((1, D_MODEL)), _const_spec((1, D_MODEL))],
        out_specs=pl.BlockSpec((D_MODEL, tm), lambda i: (0, i)),
        out_shape=jax.ShapeDtypeStruct((D_MODEL, t), F32),
        compiler_params=pltpu.CompilerParams(dimension_semantics=("arbitrary",), vmem_limit_bytes=VMEM_LIMIT),
        name="out_proj_ln",
    )(og, orr, oh, x2d, lw["w_out"], lw["ln1_g"], lw["ln1_b"])


def _oddeven_merge(lo, hi, r):
    step = r * 2
    if step < hi - lo:
        yield from _oddeven_merge(lo, hi, step)
        yield from _oddeven_merge(lo + r, hi, step)
        yield from [(i, i + r) for i in range(lo + r, hi - r, step)]
    else:
        yield (lo, lo + r)


def _oddeven_merge_sort(lo, hi):
    if hi - lo >= 1:
        mid = lo + (hi - lo) // 2
        yield from _oddeven_merge_sort(lo, mid)
        yield from _oddeven_merge_sort(mid + 1, hi)
        yield from _oddeven_merge(lo, hi, 1)


_SORT16 = tuple(_oddeven_merge_sort(0, PEER_TOPK - 1))
_BITONIC16 = tuple((i, i + d) for d in (8, 4, 2, 1) for i in range(PEER_TOPK) if i & d == 0)


def _compare_exchange(xs, pairs):
    for i, j in pairs:
        for x in xs:
            x[i], x[j] = jnp.maximum(x[i], x[j]), jnp.minimum(x[i], x[j])
    return xs


def _top16_sorted(tiles):
    xs = _compare_exchange([[a3[i] for i in range(PEER_TOPK)] for a3 in tiles], _SORT16)
    for shift in (4, 2, 1):
        ys = [[pltpu.roll(v, shift, 0) for v in x] for x in xs]
        xs = _compare_exchange([[jnp.maximum(x[i], y[PEER_TOPK - 1 - i]) for i in range(PEER_TOPK)]
                                for x, y in zip(xs, ys)], _BITONIC16)
    return xs


def _sublane_block(rows):
    sub = lax.broadcasted_iota(jnp.int32, (SUBLANE, LANE), 0)
    blk = rows[0]
    for r in range(1, SUBLANE):
        blk = jnp.where(sub == r, rows[r], blk)
    return blk


def _route_tile(a1, a2):
    v1, v2 = _top16_sorted([a1, a2])
    sub = lax.broadcasted_iota(jnp.int32, (SUBLANE, LANE), 0)
    v2a = _sublane_block(v2[0:8])
    v2b = _sublane_block(v2[8:16])
    v1b = _sublane_block(v1[8:16])
    cands = [v1[0] + v2a, v1[0] + v2b]
    for r1 in range(1, 8):
        cands.append(jnp.where(sub < PEER_TOPK // (r1 + 1), v1[r1] + v2a, NEG_INF))
    cands.append(v1b + v2[0])
    filler = jnp.full((SUBLANE, LANE), NEG_INF, F32)
    cand_tile = jnp.stack(cands + [filler] * (PEER_TOPK - len(cands)))
    thr = _top16_sorted([cand_tile])[0][PEER_TOPK - 1]
    top = v1[0] + v2[0]
    z8 = jnp.zeros((SUBLANE, LANE), F32)
    for blk in cands:
        z8 = z8 + jnp.where(blk >= thr, jnp.exp(blk - top), 0.0)
    z = jnp.broadcast_to(jnp.sum(z8, axis=0, keepdims=True), (SUBLANE, LANE))
    n1 = jnp.zeros(a1.shape, F32)
    rank2 = jnp.zeros(a2.shape, F32)
    for r in range(PEER_TOPK):
        n1 = jnp.where(a1 + v2[r] >= thr, r + 1.0, n1)
        rank2 = jnp.where(v2[r] > a2, r + 1.0, rank2)
    e1 = jnp.exp(a1 - v1[0]) * SQRT_HALF
    e2 = jnp.exp(a2 - v2[0]) / z
    return n1, rank2, e1, e2


def _peer_kernel(xt_ref, wqt_ref, sk_ref, u_ref, vt_ref, p_ref, plet_ref, gatet_ref, g2_ref, b2_ref,
                 out_ref, xbf, s_scr, n1, e1, rk2, e2, h_scr, w_scr, acc, *, tn, te, nj):
    j = pl.program_id(1)
    lane_tiles = tn // LANE
    groups = (N_KEYS // SUBLANE, SUBLANE, LANE)

    def lane_slice(lt):
        return pl.ds(pl.multiple_of(lt * LANE, LANE), LANE)

    @pl.when(j == 0)
    def _route():
        xbf[...] = xt_ref[...].astype(BF16)

        hpt = s_scr.shape[0] // 2

        def head_group_body(hp, carry):
            r = pl.multiple_of(hp * hpt * 2 * N_KEYS, hpt * 2 * N_KEYS)
            qh = _dot(wqt_ref[pl.ds(r, hpt * 2 * N_KEYS), :], xbf[...]).astype(BF16)
            for hh in range(hpt):
                for side in range(2):
                    rows = slice((2 * hh + side) * N_KEYS, (2 * hh + side + 1) * N_KEYS)
                    s_scr[2 * hh + side] = _dot(sk_ref[hpt * hp + hh, side], qh[rows])

            def lane_body(i, c):
                hh = i // lane_tiles
                lt = i % lane_tiles
                h = hpt * hp + hh
                lanes = lane_slice(lt)
                n1_t, rank2_t, e1_t, e2_t = _route_tile(s_scr[2 * hh, :, lanes].reshape(groups),
                                                        s_scr[2 * hh + 1, :, lanes].reshape(groups))
                n1[lt, h] = n1_t.reshape(N_KEYS, LANE)
                e1[lt, h] = e1_t.reshape(N_KEYS, LANE)
                rk2[lt, h] = rank2_t.reshape(N_KEYS, LANE).astype(BF16).reshape(rk2.shape[2:])
                e2[lt, h] = e2_t.reshape(N_KEYS, LANE).astype(BF16).reshape(e2.shape[2:])
                return c

            lax.fori_loop(0, hpt * lane_tiles, lane_body, 0)
            return carry

        lax.fori_loop(0, PEER_HEADS // hpt, head_group_body, 0)
        acc[...] = jnp.zeros_like(acc)

    na = te // N_KEYS
    assert na % SUBLANE == 0
    packed = rk2.shape[2:]
    row16 = (1, 2 * SUBLANE, LANE)

    hval = _dot(u_ref[...], xbf[...])
    for l2 in range(lane_tiles):
        hs = hval[:, l2 * LANE:(l2 + 1) * LANE].astype(BF16)
        h_scr[l2] = hs + hs * lax.erf(hs)

    def gate_body(p, carry):
        for grp in range(na // SUBLANE):
            a0 = pl.multiple_of(j * na + grp * SUBLANE, SUBLANE)
            n1blk = [n1[p, h, pl.ds(a0, SUBLANE), :] for h in range(PEER_HEADS)]
            e1blk = [e1[p, h, pl.ds(a0, SUBLANE), :] for h in range(PEER_HEADS)]
            for ai in range(0, SUBLANE, 2):
                gates = [None, None]
                for h in range(PEER_HEADS):
                    rk2h = rk2[p, h]
                    e2h = e2[p, h]
                    for d in range(2):
                        n1a = jnp.broadcast_to(n1blk[h][ai + d:ai + d + 1, :], row16[1:]).astype(BF16).reshape(row16)
                        e1a = jnp.broadcast_to(e1blk[h][ai + d:ai + d + 1, :], row16[1:]).astype(BF16).reshape(row16)
                        term = jnp.where(rk2h < n1a, e2h, jnp.zeros_like(e2h)) * e1a
                        gates[d] = term if h == 0 else gates[d] + term
                for d in range(2):
                    r0 = (grp * SUBLANE + ai + d) * N_KEYS
                    act = h_scr[p, r0:r0 + N_KEYS, :].reshape(packed)
                    w_scr[p, r0:r0 + N_KEYS, :] = (gates[d] * act).reshape(N_KEYS, LANE)
        return carry

    lax.fori_loop(0, lane_tiles, gate_body, 0)
    acc[...] += _dot(vt_ref[...], jnp.concatenate([w_scr[l2] for l2 in range(lane_tiles)], axis=1))

    @pl.when(j == nj - 1)
    def _finish():
        def norm_body(lt, c):
            lanes = lane_slice(lt)
            y = ALPHA * xt_ref[:, lanes] + acc[:, lanes]
            mu = jnp.mean(y, axis=0, keepdims=True)
            d = y - mu
            var = jnp.mean(d * d, axis=0, keepdims=True)
            yn = d * lax.rsqrt(var + NORM_EPS) * g2_ref[...] + b2_ref[...]
            acc[:, lanes] = yn
            xbf[:, lanes] = yn.astype(BF16)
            return c

        lax.fori_loop(0, lane_tiles, norm_body, 0, unroll=True)
        emb = _dot_nt(plet_ref[...], p_ref[...].astype(BF16))
        gt = _dot(gatet_ref[...], xbf[...])
        out_ref[...] = (acc[...] + emb * _sigmoid(gt)).T


def _peer_call(xt, p2d, lw, tables):
    u_all, vt_all, layer = tables
    t = xt.shape[1]
    tn = min(512, t)
    te = PEER_TE
    nj = vt_all.shape[1]
    lane_tiles = tn // LANE
    once = dict(pipeline_mode=pl.Buffered(1))
    in_specs = [
        pl.BlockSpec((D_MODEL, tn), lambda i, j: (0, i)),
        pl.BlockSpec(lw["wqt"].shape, lambda i, j: (0, 0), **once),
        pl.BlockSpec(lw["subkeys"].shape, lambda i, j: (0, 0, 0, 0), **once),
        pl.BlockSpec((None, te, D_MODEL), lambda i, j: (layer, j, 0)),
        pl.BlockSpec((None, None, D_MODEL, te), lambda i, j: (layer, j, 0, 0)),
        pl.BlockSpec((tn, PLE_DIM), lambda i, j: (i, 0)),
        pl.BlockSpec(lw["plet"].shape, lambda i, j: (0, 0), **once),
        pl.BlockSpec(lw["gatet"].shape, lambda i, j: (0, 0), **once),
        pl.BlockSpec((D_MODEL, LANE), lambda i, j: (0, 0), **once),
        pl.BlockSpec((D_MODEL, LANE), lambda i, j: (0, 0), **once),
    ]
    route = (lane_tiles, PEER_HEADS, N_KEYS, LANE)
    route_packed = (lane_tiles, PEER_HEADS, N_KEYS // (2 * SUBLANE), 2 * SUBLANE, LANE)
    scratch = [pltpu.VMEM((D_MODEL, tn), BF16), pltpu.VMEM((2 * ROUTE_HEADS_PER_TRIP, N_KEYS, tn), F32),
               pltpu.VMEM(route, F32), pltpu.VMEM(route, F32),
               pltpu.VMEM(route_packed, BF16), pltpu.VMEM(route_packed, BF16),
               pltpu.VMEM((lane_tiles, te, LANE), BF16), pltpu.VMEM((lane_tiles, te, LANE), BF16),
               pltpu.VMEM((D_MODEL, tn), F32)]
    return pl.pallas_call(
        functools.partial(_peer_kernel, tn=tn, te=te, nj=nj),
        grid=(t // tn, nj),
        in_specs=in_specs,
        out_specs=pl.BlockSpec((tn, D_MODEL), lambda i, j: (i, 0)),
        out_shape=jax.ShapeDtypeStruct((t, D_MODEL), F32),
        scratch_shapes=scratch,
        compiler_params=pltpu.CompilerParams(dimension_semantics=("arbitrary", "arbitrary"),
                                             vmem_limit_bytes=VMEM_LIMIT),
        name="peer_ffn_ln_ple",
    )(xt, lw["wqt"], lw["subkeys"], u_all, vt_all, p2d, lw["plet"], lw["gatet"], lw["ln2_g"], lw["ln2_b"])


def _head_slots(w, d):
    lead = w.shape[:-1]
    pad = [(0, 0)] * (len(lead) + 1) + [(0, SLOT - d)]
    return jnp.pad(w.reshape(*lead, N_HEADS, d), pad).reshape(*lead, GROUP_W)


def _rope_slots(w, d):
    lead = w.shape[:-1]
    half = d // 2
    pad = [(0, 0)] * (len(lead) + 2) + [(0, SLOT // 2 - half)]
    return jnp.pad(w.reshape(*lead, N_HEADS, 2, half), pad).reshape(*lead, GROUP_W)


def _row_slots(w, d):
    depth, _, cols = w.shape
    w = jnp.pad(w.reshape(depth, N_HEADS, d, cols), ((0, 0), (0, 0), (0, SLOT - d), (0, 0)))
    return w.reshape(depth, GROUP_W, cols)


def _rope_tables(pos):
    half = RET_DK // 2
    inv = 1.0 / (ROPE_BASE ** (jnp.arange(0, RET_DK, 2, dtype=F32) / RET_DK))
    ang = pos[:, None] * inv[None, :]
    pad = ((0, 0), (0, SLOT // 2 - half))
    cos = jnp.pad(jnp.cos(ang), pad)
    sin = jnp.pad(jnp.sin(ang), pad)
    return jnp.concatenate([cos, cos], axis=1), jnp.concatenate([-sin, sin], axis=1)


def _retention_constants():
    log_gamma = jnp.log1p(-jnp.exp2(-5.0 - jnp.arange(N_HEADS, dtype=F32)))
    i = jnp.arange(CHUNK, dtype=F32)
    diff = i[:, None] - i[None, :]
    lg = log_gamma[:, None, None]
    dmat = jnp.where(diff >= 0, jnp.exp(jnp.where(diff >= 0, diff, 0.0) * lg), 0.0)
    qdec = jnp.broadcast_to(jnp.exp((i[None, :, None] + 1.0) * lg), (N_HEADS, CHUNK, SLOT))
    kdec = jnp.broadcast_to(jnp.exp((CHUNK - 1.0 - i[None, :, None]) * lg), (N_HEADS, CHUNK, SLOT))
    sdec = jnp.broadcast_to(jnp.exp(CHUNK * lg), (N_HEADS, 1, SLOT))
    gamma = jnp.broadcast_to(jnp.exp(lg), (N_HEADS, 1, SLOT))
    return dmat, qdec, kdec, sdec, gamma


def _stacked_weights(lb, w_in, gla_w_gate, gla_b_gate, gla_norm, ret_norm, hgrn_norm, w_out,
                     ln1_g, ln1_b, ln2_g, ln2_b, peer_w_q, peer_subkeys, peer_u, peer_v, ple_proj, ple_gate):
    depth = w_in.shape[0]
    sizes = (N_HEADS * GLA_DK, N_HEADS * GLA_DK, N_HEADS * GLA_DV, N_HEADS * GLA_DV, GLA_LOWRANK,
             N_HEADS * RET_DK, N_HEADS * RET_DK, N_HEADS * RET_DV, N_HEADS * RET_DV,
             N_HEADS * HGRN_DK, N_HEADS * HGRN_DK, N_HEADS * HGRN_DV, N_HEADS * HGRN_DV)
    offs = [int(c) for c in np.cumsum(sizes)[:-1]]
    gq, gk, gv, gg, glr, rq, rk, rv, rg, hq, hf, hi, hg = jnp.split(w_in, offs, axis=2)
    sw = {}
    gate_w = jnp.einsum("dir,drk->dik", glr, gla_w_gate, precision=lax.Precision.HIGHEST)
    sw["w_gla"] = jnp.concatenate(
        [_head_slots(gq, GLA_DK), _head_slots(gk, GLA_DK), _head_slots(gv, GLA_DV), _head_slots(gg, GLA_DV),
         _head_slots(gate_w, GLA_DK)], axis=2).astype(BF16)
    sw["w_ret"] = jnp.concatenate(
        [_rope_slots(rq, RET_DK), _rope_slots(rk, RET_DK), _head_slots(rv, RET_DV), _head_slots(rg, RET_DV)],
        axis=2).astype(BF16)
    sw["w_hgrn"] = jnp.concatenate([_head_slots(w, HGRN_DK) for w in (hq, hf, hi, hg)], axis=2).astype(BF16)
    sw["bgate"] = _head_slots(gla_b_gate, GLA_DK)[:, None, :]
    sw["gla_norm"] = jnp.pad(gla_norm, ((0, 0), (0, 0), (0, SLOT - GLA_DV)))
    sw["ret_norm"] = jnp.pad(ret_norm, ((0, 0), (0, 0), (0, SLOT - RET_DV)))
    sw["hgrn_norm"] = jnp.pad(hgrn_norm, ((0, 0), (0, 0), (0, SLOT - HGRN_DV)))
    lbh = lb.reshape(depth, 1, N_HEADS, HGRN_DK)
    pad = ((0, 0), (0, 0), (0, 0), (0, SLOT - HGRN_DK))
    sw["loglb"] = jnp.pad(jnp.log(lbh), pad, constant_values=-1.0).reshape(depth, 1, GROUP_W)
    sw["l1mlb"] = jnp.pad(jnp.log1p(-lbh), pad, constant_values=-1.0).reshape(depth, 1, GROUP_W)
    g_rows, r_rows = N_HEADS * GLA_DV, N_HEADS * RET_DV
    sw["w_out"] = jnp.stack([
        _row_slots(w_out[:, :g_rows], GLA_DV), _row_slots(w_out[:, g_rows:g_rows + r_rows], RET_DV),
        _row_slots(w_out[:, g_rows + r_rows:], HGRN_DV)], axis=1).astype(BF16)
    sw["ln1_g"], sw["ln1_b"] = ln1_g[:, None, :], ln1_b[:, None, :]
    sw["ln2_g"] = jnp.broadcast_to(ln2_g[:, :, None], (depth, D_MODEL, LANE))
    sw["ln2_b"] = jnp.broadcast_to(ln2_b[:, :, None], (depth, D_MODEL, LANE))
    sw["wqt"] = jnp.swapaxes(peer_w_q, 1, 2).astype(BF16)
    sw["subkeys"] = peer_subkeys.astype(BF16)
    sw["u"] = (peer_u * SQRT_HALF).astype(BF16)
    sw["vt"] = jnp.swapaxes(peer_v.astype(BF16).reshape(depth, -1, PEER_TE, D_MODEL), 2, 3)
    sw["plet"] = jnp.swapaxes(ple_proj, 1, 2).astype(BF16)
    sw["gatet"] = jnp.swapaxes(ple_gate, 1, 2).astype(BF16)
    return sw


_BIG_TABLES = ("u", "vt")


def _unslot_state(st, dk, dv, rope=False):
    if rope:
        half = dk // 2
        st = jnp.concatenate([st[..., :half], st[..., SLOT // 2:SLOT // 2 + half]], axis=-1)
    return jnp.swapaxes(st[..., :dv, :dk], -2, -1)


def kernel(x_prompt, x_sample, p_prompt, p_sample, state_gla, state_ret, state_hgrn, w_in, gla_w_gate,
           gla_b_gate, gla_norm, ret_norm, hgrn_lb_logits, hgrn_norm, w_out, ln1_g, ln1_b, ln2_g, ln2_b,
           peer_w_q, peer_subkeys, peer_u, peer_v, ple_proj, ple_gate):
    bp, lp, _ = x_prompt.shape
    bs = x_sample.shape[0]
    assert x_sample.shape[1] == 1 and lp % CHUNK == 0

    lb = jnp.cumsum(jax.nn.softmax(hgrn_lb_logits.astype(F32), axis=0), axis=0)
    lb = lb - lb[0:1]
    tri_np, masks_np = _chunk_constants()
    tri, masks = jnp.asarray(tri_np, BF16), jnp.asarray(masks_np)
    dmat, qdec, kdec, sdec, gamma = _retention_constants()
    cos_p, sin_p = _rope_tables(jnp.arange(lp, dtype=F32))
    cos_s, sin_s = _rope_tables(PAST_LEN + jnp.arange(1, dtype=F32))
    sw = _stacked_weights(lb, w_in, gla_w_gate, gla_b_gate, gla_norm, ret_norm, hgrn_norm, w_out,
                          ln1_g, ln1_b, ln2_g, ln2_b, peer_w_q, peer_subkeys, peer_u, peer_v,
                          ple_proj, ple_gate)

    xp = x_prompt.reshape(bp * lp, D_MODEL)
    xs = x_sample.reshape(bs, D_MODEL)
    gla_p, ret_p, hgrn_p, gla_s, ret_s, hgrn_s = [], [], [], [], [], []
    for i in range(DEPTH):
        lw = {k: v[i] for k, v in sw.items() if k not in _BIG_TABLES}
        lw.update(cos_s=cos_s, sin_s=sin_s, gamma=gamma)
        tables = (sw["u"], sw["vt"], i)

        og, sg = _prompt_mixer_call(
            _gla_prompt_kernel, xp, bp, lp, lw["w_gla"],
            [lw["bgate"], lw["gla_norm"], tri, masks],
            [lw["w_gla"].shape[1], GROUP_W], "gla_prompt")
        orr, sr = _prompt_mixer_call(
            _ret_prompt_kernel, xp, bp, lp, lw["w_ret"],
            [(cos_p, None), (sin_p, None), lw["ret_norm"], dmat, qdec, kdec, sdec],
            [4 * GROUP_W], "ret_prompt")
        oh, sh = _prompt_mixer_call(
            _hgrn_prompt_kernel, xp, bp, lp, lw["w_hgrn"],
            [lw["loglb"], lw["l1mlb"], lw["hgrn_norm"], tri, masks],
            [4 * GROUP_W], "hgrn_prompt")
        xp = _peer_call(_out_call(og, orr, oh, xp, lw), p_prompt[i].reshape(bp * lp, PLE_DIM), lw, tables)
        gla_p.append(sg)
        ret_p.append(sr)
        hgrn_p.append(sh)

        o_s, nsg, nsr, nsh = _sample_mixer_call(xs, lw, state_gla[i], state_ret[i], state_hgrn[i])
        xs = _peer_call(
            _out_call(o_s[:, :GROUP_W], o_s[:, GROUP_W:2 * GROUP_W], o_s[:, 2 * GROUP_W:], xs, lw),
            p_sample[i].reshape(bs, PLE_DIM), lw, tables)
        gla_s.append(nsg)
        ret_s.append(nsr)
        hgrn_s.append(nsh)

    return (xp.reshape(bp, lp, D_MODEL), xs.reshape(bs, 1, D_MODEL),
            _unslot_state(jnp.stack(gla_p), GLA_DK, GLA_DV),
            _unslot_state(jnp.stack(ret_p), RET_DK, RET_DV, rope=True),
            _unslot_state(jnp.stack(hgrn_p), HGRN_DK, HGRN_DV),
            jnp.stack(gla_s).reshape(DEPTH, bs, N_HEADS, GLA_DK, GLA_DV),
            jnp.stack(ret_s).reshape(DEPTH, bs, N_HEADS, RET_DK, RET_DV),
            jnp.stack(hgrn_s).reshape(DEPTH, bs, N_HEADS, HGRN_DK, HGRN_DV))
```

```python
import functools

import numpy as np
import jax
import jax.numpy as jnp
from jax import lax
from jax.experimental import pallas as pl
from jax.experimental.pallas import tpu as pltpu

F32 = jnp.float32
BF16 = jnp.bfloat16

D_MODEL = 1024
DEPTH = 2
PAST_LEN = 16384
N_HEADS = 4
GLA_DK, GLA_DV = 48, 96
RET_DK, RET_DV = 48, 96
HGRN_DK, HGRN_DV = 64, 64
GLA_LOWRANK = 16
GLA_TAU = 16.0
ROPE_BASE = 10000.0
CHUNK = 64
CHUNK_UNROLL = 8
PEER_HEADS = 8
N_KEYS = 128
PEER_TOPK = 16
PEER_TE = 2048
ROUTE_HEADS_PER_TRIP = 4
PLE_DIM = 256
ALPHA = (2 * DEPTH) ** 0.25
NORM_EPS = 1e-5

LANE = 128
SUBLANE = 8
SLOT = LANE
GROUP_W = N_HEADS * SLOT
VMEM_LIMIT = 56 * 1024 * 1024
NEG_INF = float("-inf")


def _dot(a, b):
    return jnp.dot(a, b, preferred_element_type=F32)


def _dot_nt(a, b):
    return lax.dot_general(a, b, (((1,), (1,)), ((), ())), preferred_element_type=F32)


def _dot_tn(a, b):
    return lax.dot_general(a, b, (((0,), (0,)), ((), ())), preferred_element_type=F32)


def _sigmoid(x):
    return jax.nn.sigmoid(x)


def _silu(x):
    return x * _sigmoid(x)


def _log_sigmoid(x):
    return jnp.minimum(x, 0.0) - jnp.log1p(jnp.exp(-jnp.abs(x)))


def _logaddexp(a, c):
    amax = jnp.maximum(a, c)
    delta = a - c
    return jnp.where(jnp.isnan(delta), a + c, amax + jnp.log1p(jnp.exp(-jnp.abs(delta))))


SQRT_HALF = np.float32(0.7071067811865476)


def _lane_mask(n):
    return (lax.broadcasted_iota(jnp.int32, (1, LANE), 1) < n).astype(F32)


def _rms_head_norm(o, g_row, dv):
    ms = jnp.sum(o * o, axis=-1, keepdims=True) * (1.0 / dv)
    return o * lax.rsqrt(ms + NORM_EPS) * g_row


def _group_head_norm(o, g_row, dv):
    mask = _lane_mask(dv)
    mu = jnp.sum(o, axis=-1, keepdims=True) * (1.0 / dv)
    d = (o - mu) * mask
    var = jnp.sum(d * d, axis=-1, keepdims=True) * (1.0 / dv)
    return d * lax.rsqrt(var + NORM_EPS) * g_row


def _chunk_constants():
    i = np.arange(CHUNK)[:, None]
    t = np.arange(CHUNK)[None, :]
    masks = [i == t]
    half = CHUNK // 2
    while half >= 1:
        blk = i // (2 * half)
        second = (i % (2 * half)) >= half
        masks.append(second & ((t % (2 * half)) < half) & (blk == t // (2 * half)))
        half //= 2
    return (t <= i).astype(np.float32), np.stack(masks).astype(np.float32)


def _cumsum_rows(tri_bf, g):
    hi = g.astype(BF16)
    r1 = g - hi.astype(F32)
    mid = r1.astype(BF16)
    lo = (r1 - mid.astype(F32)).astype(BF16)
    return _dot(tri_bf, hi) + _dot(tri_bf, mid) + _dot(tri_bf, lo)


def _level_factors(b, g):
    width = b.shape[1]
    grouped = (CHUNK // SUBLANE, SUBLANE, width)
    row = lax.broadcasted_iota(jnp.int32, b.shape, 0)
    sub = lax.broadcasted_iota(jnp.int32, grouped, 1)
    b3 = b.reshape(grouped)

    def sub_ref(r):
        return jnp.broadcast_to(b3[:, r:r + 1, :], grouped)

    out = []
    half = CHUNK // 2
    while half >= SUBLANE:
        ref = jnp.concatenate(
            [jnp.broadcast_to(b[m * 2 * half + half - 1:m * 2 * half + half], (2 * half, width))
             for m in range(CHUNK // (2 * half))], axis=0)
        out.append(jnp.exp(-jnp.abs(b - ref)))
        half //= 2
    out.append(jnp.exp(-jnp.abs(b3 - sub_ref(3))).reshape(b.shape))
    out.append(jnp.exp(-jnp.abs(b3 - jnp.where(sub < 4, sub_ref(1), sub_ref(5)))).reshape(b.shape))
    out.append(jnp.exp(jnp.where(row % 2 == 1, g, 0.0)))
    return out


def _vector_decay_chunk(q, k, v, g, st_ref, tri_bf, masks_ref):
    heads = [slice(h * SLOT, (h + 1) * SLOT) for h in range(N_HEADS)]
    b = _cumsum_rows(tri_bf, g)
    b_last = b[CHUNK - 1:CHUNK]
    factors = _level_factors(b, g)
    qs = [q.astype(BF16)] + [(q * f).astype(BF16) for f in factors]
    ks = [k.astype(BF16)] + [(k * f).astype(BF16) for f in factors]
    q_in = (q * jnp.exp(b)).astype(BF16)
    k_out = (k * jnp.exp(b_last - b)).astype(BF16)
    decay = jnp.exp(b_last)
    vb = v.astype(BF16)
    scores = []
    for sl in heads:
        sc = masks_ref[0] * _dot_nt(qs[0][:, sl], ks[0][:, sl])
        for l in range(len(factors)):
            sc = sc + masks_ref[1 + l] * _dot_nt(qs[1 + l][:, sl], ks[1 + l][:, sl])
        scores.append(sc.astype(BF16))
    outs = []
    for h, sl in enumerate(heads):
        st = st_ref[0, h]
        outs.append(_dot(scores[h], vb[:, sl]) + _dot_nt(q_in[:, sl], st.astype(BF16)))
        st_ref[0, h] = st * decay[:, sl] + _dot_tn(vb[:, sl], k_out[:, sl])
    return outs


def _gla_prompt_kernel(x_ref, w_ref, bg_ref, nrm_ref, tri_ref, masks_ref,
                       o_ref, st_ref, z_scr, la_scr, *, seg):
    @pl.when(pl.program_id(1) == 0)
    def _():
        st_ref[...] = jnp.zeros_like(st_ref)

    z_scr[...] = _dot(x_ref[...].astype(BF16), w_ref[...])
    pre = z_scr[:, 4 * GROUP_W:5 * GROUP_W] + bg_ref[...]
    la_scr[...] = _log_sigmoid(pre) * (1.0 / GLA_TAU)
    tri = tri_ref[...]

    def body(c, carry):
        r0 = pl.multiple_of(c * CHUNK, CHUNK)
        rows = pl.ds(r0, CHUNK)
        q = z_scr[rows, 0:GROUP_W] * (GLA_DK ** -0.5)
        k = z_scr[rows, GROUP_W:2 * GROUP_W]
        v = z_scr[rows, 2 * GROUP_W:3 * GROUP_W]
        outs = _vector_decay_chunk(q, k, v, la_scr[rows, :], st_ref, tri, masks_ref)
        for h in range(N_HEADS):
            gate = z_scr[rows, 3 * GROUP_W + h * SLOT:3 * GROUP_W + (h + 1) * SLOT]
            on = _rms_head_norm(outs[h], nrm_ref[h:h + 1, :], GLA_DV)
            o_ref[rows, h * SLOT:(h + 1) * SLOT] = (on * _silu(gate)).astype(BF16)
        return carry

    lax.fori_loop(0, seg // CHUNK, body, 0, unroll=CHUNK_UNROLL)


def _hgrn_prompt_kernel(x_ref, w_ref, loglb_ref, l1mlb_ref, nrm_ref, tri_ref, masks_ref,
                        o_ref, st_ref, z_scr, *, seg):
    @pl.when(pl.program_id(1) == 0)
    def _():
        st_ref[...] = jnp.zeros_like(st_ref)

    z_scr[...] = _dot(x_ref[...].astype(BF16), w_ref[...])
    tri = tri_ref[...]
    kmask = jnp.concatenate([_lane_mask(HGRN_DK)] * N_HEADS, axis=1)

    def body(c, carry):
        r0 = pl.multiple_of(c * CHUNK, CHUNK)
        rows = pl.ds(r0, CHUNK)
        q = _silu(z_scr[rows, 0:GROUP_W])
        v = z_scr[rows, 2 * GROUP_W:3 * GROUP_W]
        log_f = _logaddexp(loglb_ref[...], l1mlb_ref[...] + _log_sigmoid(z_scr[rows, GROUP_W:2 * GROUP_W]))
        k = (1.0 - jnp.exp(log_f)) * kmask
        outs = _vector_decay_chunk(q, k, v, log_f, st_ref, tri, masks_ref)
        for h in range(N_HEADS):
            sl = slice(h * SLOT, (h + 1) * SLOT)
            gate = z_scr[rows, 3 * GROUP_W + h * SLOT:3 * GROUP_W + (h + 1) * SLOT]
            on = _rms_head_norm(outs[h], nrm_ref[h:h + 1, :], HGRN_DV)
            o_ref[rows, sl] = (on * _silu(gate)).astype(BF16)
        return carry

    lax.fori_loop(0, seg // CHUNK, body, 0, unroll=CHUNK_UNROLL)


def _rotate(t, cs, sn):
    return t * cs + pltpu.roll(t, SLOT // 2, 1) * sn


def _ret_prompt_kernel(x_ref, w_ref, cos_ref, sin_ref, nrm_ref, dmat_ref, qdec_ref, kdec_ref, sdec_ref,
                       o_ref, st_ref, z_scr, *, seg):
    @pl.when(pl.program_id(1) == 0)
    def _():
        st_ref[...] = jnp.zeros_like(st_ref)

    z_scr[...] = _dot(x_ref[...].astype(BF16), w_ref[...])

    def body(c, carry):
        r0 = pl.multiple_of(c * CHUNK, CHUNK)
        rows = pl.ds(r0, CHUNK)
        cs = cos_ref[rows, :]
        sn = sin_ref[rows, :]
        heads = [slice(h * SLOT, (h + 1) * SLOT) for h in range(N_HEADS)]
        qs = [_rotate(z_scr[rows, h * SLOT:(h + 1) * SLOT], cs, sn) for h in range(N_HEADS)]
        ks = [_rotate(z_scr[rows, GROUP_W + h * SLOT:GROUP_W + (h + 1) * SLOT], cs, sn) * (RET_DK ** -0.5)
              for h in range(N_HEADS)]
        vb = z_scr[rows, 2 * GROUP_W:3 * GROUP_W].astype(BF16)
        scores = [(dmat_ref[h] * _dot_nt(qs[h].astype(BF16), ks[h].astype(BF16))).astype(BF16)
                  for h in range(N_HEADS)]
        q_in = [(qs[h] * qdec_ref[h]).astype(BF16) for h in range(N_HEADS)]
        k_out = [(ks[h] * kdec_ref[h]).astype(BF16) for h in range(N_HEADS)]
        outs = []
        for h, sl in enumerate(heads):
            st = st_ref[0, h]
            outs.append(_dot(scores[h], vb[:, sl]) + _dot_nt(q_in[h], st.astype(BF16)))
            st_ref[0, h] = st * sdec_ref[h] + _dot_tn(vb[:, sl], k_out[h])
        for h, sl in enumerate(heads):
            gate = z_scr[rows, 3 * GROUP_W + h * SLOT:3 * GROUP_W + (h + 1) * SLOT]
            on = _group_head_norm(outs[h], nrm_ref[h:h + 1, :], RET_DV)
            o_ref[rows, sl] = (on * _silu(gate)).astype(BF16)
        return carry

    lax.fori_loop(0, seg // CHUNK, body, 0, unroll=CHUNK_UNROLL)


def _const_spec(shape):
    nd = len(shape)
    return pl.BlockSpec(shape, lambda *_: (0,) * nd)


def _prompt_mixer_call(kernel, x2d, batch, seq, w, extras, scratch_widths, name):
    seg = min(512, seq)
    nseg = seq // seg
    in_specs = [pl.BlockSpec((seg, D_MODEL), lambda b, s: (b * nseg + s, 0)), _const_spec(w.shape)]
    args = [x2d, w]
    for e in extras:
        if isinstance(e, tuple):
            arr, _ = e
            in_specs.append(pl.BlockSpec((seg, arr.shape[1]), lambda b, s: (s, 0)))
            args.append(arr)
        else:
            in_specs.append(_const_spec(e.shape))
            args.append(e)
    return pl.pallas_call(
        functools.partial(kernel, seg=seg),
        grid=(batch, nseg),
        in_specs=in_specs,
        out_specs=[pl.BlockSpec((seg, GROUP_W), lambda b, s: (b * nseg + s, 0)),
                   pl.BlockSpec((1, N_HEADS, SLOT, SLOT), lambda b, s: (b, 0, 0, 0))],
        out_shape=[jax.ShapeDtypeStruct((batch * seq, GROUP_W), BF16),
                   jax.ShapeDtypeStruct((batch, N_HEADS, SLOT, SLOT), F32)],
        scratch_shapes=[pltpu.VMEM((seg, wd), F32) for wd in scratch_widths],
        compiler_params=pltpu.CompilerParams(dimension_semantics=("arbitrary", "arbitrary"),
                                             vmem_limit_bytes=VMEM_LIMIT),
        name=name,
    )(*args)


def _sample_step(q, k, eg, v, s_ref, ns_ref, tq, tk, te, tv, to, s_t, sn_t, dk, dv, row_of):
    tq[...] = q.T
    tk[...] = k.T
    te[...] = eg.T
    tv[...] = v.T
    s_t[0:dk * dv, :] = s_ref[...].T
    vt = tv[0:dv, :]

    def body(kk, oacc):
        kr = row_of(kk)
        r = pl.multiple_of(kk * dv, SUBLANE)
        sn = s_t[pl.ds(r, dv), :] * te[pl.ds(kr, 1), :] + tk[pl.ds(kr, 1), :] * vt
        sn_t[pl.ds(r, dv), :] = sn
        return oacc + tq[pl.ds(kr, 1), :] * sn

    o_t = lax.fori_loop(0, dk, body, jnp.zeros((dv, q.shape[0]), F32), unroll=4)
    ns_ref[...] = sn_t[0:dk * dv, :].T
    to[...] = jnp.zeros_like(to)
    to[0:dv, :] = o_t
    return to[...].T


def _sample_mixer_kernel(x_ref, wg_ref, wr_ref, wh_ref, bgate_ref, gn_ref, rn_ref, hn_ref,
                         cos_ref, sin_ref, gam_ref, loglb_ref, l1mlb_ref, sg_ref, sr_ref, sh_ref,
                         o_ref, nsg_ref, nsr_ref, nsh_ref,
                         zg, zr, zh, la, tq, tk, te, tv, to, s_t, sn_t):
    h = pl.program_id(0)

    @pl.when(h == 0)
    def _():
        xb = x_ref[...].astype(BF16)
        zg[...] = _dot(xb, wg_ref[...])
        zr[...] = _dot(xb, wr_ref[...])
        zh[...] = _dot(xb, wh_ref[...])
        pre = zg[:, 4 * GROUP_W:5 * GROUP_W] + bgate_ref[...]
        la[...] = _log_sigmoid(pre) * (1.0 / GLA_TAU)

    off = pl.multiple_of(h * SLOT, SLOT)
    sl = pl.ds(off, SLOT)
    tr = (tq, tk, te, tv, to, s_t, sn_t)
    batch = x_ref.shape[0]

    q = zg[:, sl] * (GLA_DK ** -0.5)
    k = zg[:, pl.ds(GROUP_W + off, SLOT)]
    v = zg[:, pl.ds(2 * GROUP_W + off, SLOT)]
    gate = zg[:, pl.ds(3 * GROUP_W + off, SLOT)]
    o = _sample_step(q, k, jnp.exp(la[:, sl]), v, sg_ref, nsg_ref, *tr, GLA_DK, GLA_DV, lambda kk: kk)
    o_ref[:, sl] = (_rms_head_norm(o, gn_ref[pl.ds(h, 1), :], GLA_DV) * _silu(gate)).astype(BF16)

    cs = cos_ref[...]
    sn = sin_ref[...]
    q = _rotate(zr[:, sl], cs, sn)
    k = _rotate(zr[:, pl.ds(GROUP_W + off, SLOT)], cs, sn) * (RET_DK ** -0.5)
    v = zr[:, pl.ds(2 * GROUP_W + off, SLOT)]
    gate = zr[:, pl.ds(3 * GROUP_W + off, SLOT)]
    eg = jnp.broadcast_to(gam_ref[h], (batch, SLOT))
    half = RET_DK // 2
    o = _sample_step(q, k, eg, v, sr_ref, nsr_ref, *tr, RET_DK, RET_DV,
                     lambda kk: kk + jnp.where(kk >= half, SLOT // 2 - half, 0))
    o_ref[:, pl.ds(GROUP_W + off, SLOT)] = (
        _group_head_norm(o, rn_ref[pl.ds(h, 1), :], RET_DV) * _silu(gate)).astype(BF16)

    q = _silu(zh[:, sl])
    hf = zh[:, pl.ds(GROUP_W + off, SLOT)]
    v = zh[:, pl.ds(2 * GROUP_W + off, SLOT)]
    gate = zh[:, pl.ds(3 * GROUP_W + off, SLOT)]
    log_f = _logaddexp(loglb_ref[:, sl], l1mlb_ref[:, sl] + _log_sigmoid(hf))
    f = jnp.exp(log_f)
    o = _sample_step(q, (1.0 - f) * _lane_mask(HGRN_DK), f, v, sh_ref, nsh_ref, *tr,
                     HGRN_DK, HGRN_DV, lambda kk: kk)
    o_ref[:, pl.ds(2 * GROUP_W + off, SLOT)] = (
        _rms_head_norm(o, hn_ref[pl.ds(h, 1), :], HGRN_DV) * _silu(gate)).astype(BF16)


def _sample_mixer_call(x2d, lw, sg, sr, sh):
    batch = x2d.shape[0]
    gsz, hsz = GLA_DK * GLA_DV, HGRN_DK * HGRN_DV
    consts = [lw["w_gla"], lw["w_ret"], lw["w_hgrn"], lw["bgate"], lw["gla_norm"], lw["ret_norm"],
              lw["hgrn_norm"], lw["cos_s"], lw["sin_s"], lw["gamma"], lw["loglb"], lw["l1mlb"]]
    in_specs = ([_const_spec(x2d.shape)] + [_const_spec(c.shape) for c in consts]
                + [pl.BlockSpec((batch, gsz), lambda h: (0, h)),
                   pl.BlockSpec((batch, gsz), lambda h: (0, h)),
                   pl.BlockSpec((batch, hsz), lambda h: (0, h))])
    out_specs = [_const_spec((batch, 3 * GROUP_W)),
                 pl.BlockSpec((batch, gsz), lambda h: (0, h)),
                 pl.BlockSpec((batch, gsz), lambda h: (0, h)),
                 pl.BlockSpec((batch, hsz), lambda h: (0, h))]
    out_shape = [jax.ShapeDtypeStruct((batch, 3 * GROUP_W), BF16),
                 jax.ShapeDtypeStruct((batch, N_HEADS * gsz), F32),
                 jax.ShapeDtypeStruct((batch, N_HEADS * gsz), F32),
                 jax.ShapeDtypeStruct((batch, N_HEADS * hsz), F32)]
    scratch = [pltpu.VMEM((batch, lw["w_gla"].shape[1]), F32), pltpu.VMEM((batch, 4 * GROUP_W), F32),
               pltpu.VMEM((batch, 4 * GROUP_W), F32), pltpu.VMEM((batch, GROUP_W), F32)]
    scratch += [pltpu.VMEM((SLOT, batch), F32) for _ in range(5)]
    scratch += [pltpu.VMEM((gsz, batch), F32), pltpu.VMEM((gsz, batch), F32)]
    return pl.pallas_call(
        _sample_mixer_kernel,
        grid=(N_HEADS,),
        in_specs=in_specs, out_specs=out_specs, out_shape=out_shape, scratch_shapes=scratch,
        compiler_params=pltpu.CompilerParams(dimension_semantics=("arbitrary",), vmem_limit_bytes=VMEM_LIMIT),
        name="sample_mixer",
    )(x2d, *consts, sg.reshape(batch, -1), sr.reshape(batch, -1), sh.reshape(batch, -1))


def _out_kernel(og_ref, or_ref, oh_ref, x_ref, wo_ref, g_ref, b_ref, xt_ref):
    mix = _dot(og_ref[...], wo_ref[0]) + _dot(or_ref[...], wo_ref[1]) + _dot(oh_ref[...], wo_ref[2])
    y = ALPHA * x_ref[...] + mix
    mu = jnp.mean(y, axis=-1, keepdims=True)
    d = y - mu
    var = jnp.mean(d * d, axis=-1, keepdims=True)
    xt_ref[...] = (d * lax.rsqrt(var + NORM_EPS) * g_ref[...] + b_ref[...]).T


def _out_call(og, orr, oh, x2d, lw):
    t = x2d.shape[0]
    tm = min(1024, t)
    row = lambda i: (i, 0)
    return pl.pallas_call(
        _out_kernel,
        grid=(t // tm,),
        in_specs=[pl.BlockSpec((tm, GROUP_W), row), pl.BlockSpec((tm, GROUP_W), row),
                  pl.BlockSpec((tm, GROUP_W), row), pl.BlockSpec((tm, D_MODEL), row),
                  _const_spec(lw["w_out"].shape), _const_spec((1, D_MODEL)), _const_spec((1, D_MODEL))],
        out_specs=pl.BlockSpec((D_MODEL, tm), lambda i: (0, i)),
        out_shape=jax.ShapeDtypeStruct((D_MODEL, t), F32),
        compiler_params=pltpu.CompilerParams(dimension_semantics=("arbitrary",), vmem_limit_bytes=VMEM_LIMIT),
        name="out_proj_ln",
    )(og, orr, oh, x2d, lw["w_out"], lw["ln1_g"], lw["ln1_b"])


def _oddeven_merge(lo, hi, r):
    step = r * 2
    if step < hi - lo:
        yield from _oddeven_merge(lo, hi, step)
        yield from _oddeven_merge(lo + r, hi, step)
        yield from [(i, i + r) for i in range(lo + r, hi - r, step)]
    else:
        yield (lo, lo + r)


def _oddeven_merge_sort(lo, hi):
    if hi - lo >= 1:
        mid = lo + (hi - lo) // 2
        yield from _oddeven_merge_sort(lo, mid)
        yield from _oddeven_merge_sort(mid + 1, hi)
        yield from _oddeven_merge(lo, hi, 1)


_SORT16 = tuple(_oddeven_merge_sort(0, PEER_TOPK - 1))
_BITONIC16 = tuple((i, i + d) for d in (8, 4, 2, 1) for i in range(PEER_TOPK) if i & d == 0)


def _compare_exchange(xs, pairs):
    for i, j in pairs:
        for x in xs:
            x[i], x[j] = jnp.maximum(x[i], x[j]), jnp.minimum(x[i], x[j])
    return xs


def _top16_sorted(tiles):
    xs = _compare_exchange([[a3[i] for i in range(PEER_TOPK)] for a3 in tiles], _SORT16)
    for shift in (4, 2, 1):
        ys = [[pltpu.roll(v, shift, 0) for v in x] for x in xs]
        xs = _compare_exchange([[jnp.maximum(x[i], y[PEER_TOPK - 1 - i]) for i in range(PEER_TOPK)]
                                for x, y in zip(xs, ys)], _BITONIC16)
    return xs


def _sublane_block(rows):
    sub = lax.broadcasted_iota(jnp.int32, (SUBLANE, LANE), 0)
    blk = rows[0]
    for r in range(1, SUBLANE):
        blk = jnp.where(sub == r, rows[r], blk)
    return blk


def _route_tile(a1, a2):
    v1, v2 = _top16_sorted([a1, a2])
    sub = lax.broadcasted_iota(jnp.int32, (SUBLANE, LANE), 0)
    v2a = _sublane_block(v2[0:8])
    v2b = _sublane_block(v2[8:16])
    v1b = _sublane_block(v1[8:16])
    cands = [v1[0] + v2a, v1[0] + v2b]
    for r1 in range(1, 8):
        cands.append(jnp.where(sub < PEER_TOPK // (r1 + 1), v1[r1] + v2a, NEG_INF))
    cands.append(v1b + v2[0])
    filler = jnp.full((SUBLANE, LANE), NEG_INF, F32)
    cand_tile = jnp.stack(cands + [filler] * (PEER_TOPK - len(cands)))
    thr = _top16_sorted([cand_tile])[0][PEER_TOPK - 1]
    top = v1[0] + v2[0]
    z8 = jnp.zeros((SUBLANE, LANE), F32)
    for blk in cands:
        z8 = z8 + jnp.where(blk >= thr, jnp.exp(blk - top), 0.0)
    z = jnp.broadcast_to(jnp.sum(z8, axis=0, keepdims=True), (SUBLANE, LANE))
    n1 = jnp.zeros(a1.shape, F32)
    rank2 = jnp.zeros(a2.shape, F32)
    for r in range(PEER_TOPK):
        n1 = jnp.where(a1 + v2[r] >= thr, r + 1.0, n1)
        rank2 = jnp.where(v2[r] > a2, r + 1.0, rank2)
    e1 = jnp.exp(a1 - v1[0]) * SQRT_HALF
    e2 = jnp.exp(a2 - v2[0]) / z
    return n1, rank2, e1, e2


def _peer_kernel(xt_ref, wqt_ref, sk_ref, u_ref, vt_ref, p_ref, plet_ref, gatet_ref, g2_ref, b2_ref,
                 out_ref, xbf, s_scr, n1, e1, rk2, e2, h_scr, w_scr, acc, *, tn, te, nj):
    j = pl.program_id(1)
    lane_tiles = tn // LANE
    groups = (N_KEYS // SUBLANE, SUBLANE, LANE)

    def lane_slice(lt):
        return pl.ds(pl.multiple_of(lt * LANE, LANE), LANE)

    @pl.when(j == 0)
    def _route():
        xbf[...] = xt_ref[...].astype(BF16)

        hpt = s_scr.shape[0] // 2

        def head_group_body(hp, carry):
            r = pl.multiple_of(hp * hpt * 2 * N_KEYS, hpt * 2 * N_KEYS)
            qh = _dot(wqt_ref[pl.ds(r, hpt * 2 * N_KEYS), :], xbf[...]).astype(BF16)
            for hh in range(hpt):
                for side in range(2):
                    rows = slice((2 * hh + side) * N_KEYS, (2 * hh + side + 1) * N_KEYS)
                    s_scr[2 * hh + side] = _dot(sk_ref[hpt * hp + hh, side], qh[rows])

            def lane_body(i, c):
                hh = i // lane_tiles
                lt = i % lane_tiles
                h = hpt * hp + hh
                lanes = lane_slice(lt)
                n1_t, rank2_t, e1_t, e2_t = _route_tile(s_scr[2 * hh, :, lanes].reshape(groups),
                                                        s_scr[2 * hh + 1, :, lanes].reshape(groups))
                n1[lt, h] = n1_t.reshape(N_KEYS, LANE)
                e1[lt, h] = e1_t.reshape(N_KEYS, LANE)
                rk2[lt, h] = rank2_t.reshape(N_KEYS, LANE).astype(BF16).reshape(rk2.shape[2:])
                e2[lt, h] = e2_t.reshape(N_KEYS, LANE).astype(BF16).reshape(e2.shape[2:])
                return c

            lax.fori_loop(0, hpt * lane_tiles, lane_body, 0)
            return carry

        lax.fori_loop(0, PEER_HEADS // hpt, head_group_body, 0)
        acc[...] = jnp.zeros_like(acc)

    na = te // N_KEYS
    assert na % SUBLANE == 0
    packed = rk2.shape[2:]
    row16 = (1, 2 * SUBLANE, LANE)

    hval = _dot(u_ref[...], xbf[...])
    for l2 in range(lane_tiles):
        hs = hval[:, l2 * LANE:(l2 + 1) * LANE].astype(BF16)
        h_scr[l2] = hs + hs * lax.erf(hs)

    def gate_body(p, carry):
        for grp in range(na // SUBLANE):
            a0 = pl.multiple_of(j * na + grp * SUBLANE, SUBLANE)
            n1blk = [n1[p, h, pl.ds(a0, SUBLANE), :] for h in range(PEER_HEADS)]
            e1blk = [e1[p, h, pl.ds(a0, SUBLANE), :] for h in range(PEER_HEADS)]
            for ai in range(0, SUBLANE, 2):
                gates = [None, None]
                for h in range(PEER_HEADS):
                    rk2h = rk2[p, h]
                    e2h = e2[p, h]
                    for d in range(2):
                        n1a = jnp.broadcast_to(n1blk[h][ai + d:ai + d + 1, :], row16[1:]).astype(BF16).reshape(row16)
                        e1a = jnp.broadcast_to(e1blk[h][ai + d:ai + d + 1, :], row16[1:]).astype(BF16).reshape(row16)
                        term = jnp.where(rk2h < n1a, e2h, jnp.zeros_like(e2h)) * e1a
                        gates[d] = term if h == 0 else gates[d] + term
                for d in range(2):
                    r0 = (grp * SUBLANE + ai + d) * N_KEYS
                    act = h_scr[p, r0:r0 + N_KEYS, :].reshape(packed)
                    w_scr[p, r0:r0 + N_KEYS, :] = (gates[d] * act).reshape(N_KEYS, LANE)
        return carry

    lax.fori_loop(0, lane_tiles, gate_body, 0, unroll=min(2, lane_tiles))
    acc[...] += _dot(vt_ref[...], jnp.concatenate([w_scr[l2] for l2 in range(lane_tiles)], axis=1))

    @pl.when(j == nj - 1)
    def _finish():
        def norm_body(lt, c):
            lanes = lane_slice(lt)
            y = ALPHA * xt_ref[:, lanes] + acc[:, lanes]
            mu = jnp.mean(y, axis=0, keepdims=True)
            d = y - mu
            var = jnp.mean(d * d, axis=0, keepdims=True)
            yn = d * lax.rsqrt(var + NORM_EPS) * g2_ref[...] + b2_ref[...]
            acc[:, lanes] = yn
            xbf[:, lanes] = yn.astype(BF16)
            return c

        lax.fori_loop(0, lane_tiles, norm_body, 0, unroll=True)
        emb = _dot_nt(plet_ref[...], p_ref[...].astype(BF16))
        gt = _dot(gatet_ref[...], xbf[...])
        out_ref[...] = (acc[...] + emb * _sigmoid(gt)).T


def _peer_call(xt, p2d, lw, tables):
    u_all, vt_all, layer = tables
    t = xt.shape[1]
    tn = min(512, t)
    te = PEER_TE
    nj = vt_all.shape[1]
    lane_tiles = tn // LANE
    once = dict(pipeline_mode=pl.Buffered(1))
    in_specs = [
        pl.BlockSpec((D_MODEL, tn), lambda i, j: (0, i)),
        pl.BlockSpec(lw["wqt"].shape, lambda i, j: (0, 0), **once),
        pl.BlockSpec(lw["subkeys"].shape, lambda i, j: (0, 0, 0, 0), **once),
        pl.BlockSpec((None, te, D_MODEL), lambda i, j: (layer, j, 0)),
        pl.BlockSpec((None, None, D_MODEL, te), lambda i, j: (layer, j, 0, 0)),
        pl.BlockSpec((tn, PLE_DIM), lambda i, j: (i, 0)),
        pl.BlockSpec(lw["plet"].shape, lambda i, j: (0, 0), **once),
        pl.BlockSpec(lw["gatet"].shape, lambda i, j: (0, 0), **once),
        pl.BlockSpec((D_MODEL, LANE), lambda i, j: (0, 0), **once),
        pl.BlockSpec((D_MODEL, LANE), lambda i, j: (0, 0), **once),
    ]
    route = (lane_tiles, PEER_HEADS, N_KEYS, LANE)
    route_packed = (lane_tiles, PEER_HEADS, N_KEYS // (2 * SUBLANE), 2 * SUBLANE, LANE)
    scratch = [pltpu.VMEM((D_MODEL, tn), BF16), pltpu.VMEM((2 * ROUTE_HEADS_PER_TRIP, N_KEYS, tn), F32),
               pltpu.VMEM(route, F32), pltpu.VMEM(route, F32),
               pltpu.VMEM(route_packed, BF16), pltpu.VMEM(route_packed, BF16),
               pltpu.VMEM((lane_tiles, te, LANE), BF16), pltpu.VMEM((lane_tiles, te, LANE), BF16),
               pltpu.VMEM((D_MODEL, tn), F32)]
    return pl.pallas_call(
        functools.partial(_peer_kernel, tn=tn, te=te, nj=nj),
        grid=(t // tn, nj),
        in_specs=in_specs,
        out_specs=pl.BlockSpec((tn, D_MODEL), lambda i, j: (i, 0)),
        out_shape=jax.ShapeDtypeStruct((t, D_MODEL), F32),
        scratch_shapes=scratch,
        compiler_params=pltpu.CompilerParams(dimension_semantics=("arbitrary", "arbitrary"),
                                             vmem_limit_bytes=VMEM_LIMIT),
        name="peer_ffn_ln_ple",
    )(xt, lw["wqt"], lw["subkeys"], u_all, vt_all, p2d, lw["plet"], lw["gatet"], lw["ln2_g"], lw["ln2_b"])


def _head_slots(w, d):
    lead = w.shape[:-1]
    pad = [(0, 0)] * (len(lead) + 1) + [(0, SLOT - d)]
    return jnp.pad(w.reshape(*lead, N_HEADS, d), pad).reshape(*lead, GROUP_W)


def _rope_slots(w, d):
    lead = w.shape[:-1]
    half = d // 2
    pad = [(0, 0)] * (len(lead) + 2) + [(0, SLOT // 2 - half)]
    return jnp.pad(w.reshape(*lead, N_HEADS, 2, half), pad).reshape(*lead, GROUP_W)


def _row_slots(w, d):
    depth, _, cols = w.shape
    w = jnp.pad(w.reshape(depth, N_HEADS, d, cols), ((0, 0), (0, 0), (0, SLOT - d), (0, 0)))
    return w.reshape(depth, GROUP_W, cols)


def _rope_tables(pos):
    half = RET_DK // 2
    inv = 1.0 / (ROPE_BASE ** (jnp.arange(0, RET_DK, 2, dtype=F32) / RET_DK))
    ang = pos[:, None] * inv[None, :]
    pad = ((0, 0), (0, SLOT // 2 - half))
    cos = jnp.pad(jnp.cos(ang), pad)
    sin = jnp.pad(jnp.sin(ang), pad)
    return jnp.concatenate([cos, cos], axis=1), jnp.concatenate([-sin, sin], axis=1)


def _retention_constants():
    log_gamma = jnp.log1p(-jnp.exp2(-5.0 - jnp.arange(N_HEADS, dtype=F32)))
    i = jnp.arange(CHUNK, dtype=F32)
    diff = i[:, None] - i[None, :]
    lg = log_gamma[:, None, None]
    dmat = jnp.where(diff >= 0, jnp.exp(jnp.where(diff >= 0, diff, 0.0) * lg), 0.0)
    qdec = jnp.broadcast_to(jnp.exp((i[None, :, None] + 1.0) * lg), (N_HEADS, CHUNK, SLOT))
    kdec = jnp.broadcast_to(jnp.exp((CHUNK - 1.0 - i[None, :, None]) * lg), (N_HEADS, CHUNK, SLOT))
    sdec = jnp.broadcast_to(jnp.exp(CHUNK * lg), (N_HEADS, 1, SLOT))
    gamma = jnp.broadcast_to(jnp.exp(lg), (N_HEADS, 1, SLOT))
    return dmat, qdec, kdec, sdec, gamma


def _stacked_weights(lb, w_in, gla_w_gate, gla_b_gate, gla_norm, ret_norm, hgrn_norm, w_out,
                     ln1_g, ln1_b, ln2_g, ln2_b, peer_w_q, peer_subkeys, peer_u, peer_v, ple_proj, ple_gate):
    depth = w_in.shape[0]
    sizes = (N_HEADS * GLA_DK, N_HEADS * GLA_DK, N_HEADS * GLA_DV, N_HEADS * GLA_DV, GLA_LOWRANK,
             N_HEADS * RET_DK, N_HEADS * RET_DK, N_HEADS * RET_DV, N_HEADS * RET_DV,
             N_HEADS * HGRN_DK, N_HEADS * HGRN_DK, N_HEADS * HGRN_DV, N_HEADS * HGRN_DV)
    offs = [int(c) for c in np.cumsum(sizes)[:-1]]
    gq, gk, gv, gg, glr, rq, rk, rv, rg, hq, hf, hi, hg = jnp.split(w_in, offs, axis=2)
    sw = {}
    gate_w = jnp.einsum("dir,drk->dik", glr, gla_w_gate, precision=lax.Precision.HIGHEST)
    sw["w_gla"] = jnp.concatenate(
        [_head_slots(gq, GLA_DK), _head_slots(gk, GLA_DK), _head_slots(gv, GLA_DV), _head_slots(gg, GLA_DV),
         _head_slots(gate_w, GLA_DK)], axis=2).astype(BF16)
    sw["w_ret"] = jnp.concatenate(
        [_rope_slots(rq, RET_DK), _rope_slots(rk, RET_DK), _head_slots(rv, RET_DV), _head_slots(rg, RET_DV)],
        axis=2).astype(BF16)
    sw["w_hgrn"] = jnp.concatenate([_head_slots(w, HGRN_DK) for w in (hq, hf, hi, hg)], axis=2).astype(BF16)
    sw["bgate"] = _head_slots(gla_b_gate, GLA_DK)[:, None, :]
    sw["gla_norm"] = jnp.pad(gla_norm, ((0, 0), (0, 0), (0, SLOT - GLA_DV)))
    sw["ret_norm"] = jnp.pad(ret_norm, ((0, 0), (0, 0), (0, SLOT - RET_DV)))
    sw["hgrn_norm"] = jnp.pad(hgrn_norm, ((0, 0), (0, 0), (0, SLOT - HGRN_DV)))
    lbh = lb.reshape(depth, 1, N_HEADS, HGRN_DK)
    pad = ((0, 0), (0, 0), (0, 0), (0, SLOT - HGRN_DK))
    sw["loglb"] = jnp.pad(jnp.log(lbh), pad, constant_values=-1.0).reshape(depth, 1, GROUP_W)
    sw["l1mlb"] = jnp.pad(jnp.log1p(-lbh), pad, constant_values=-1.0).reshape(depth, 1, GROUP_W)
    g_rows, r_rows = N_HEADS * GLA_DV, N_HEADS * RET_DV
    sw["w_out"] = jnp.stack([
        _row_slots(w_out[:, :g_rows], GLA_DV), _row_slots(w_out[:, g_rows:g_rows + r_rows], RET_DV),
        _row_slots(w_out[:, g_rows + r_rows:], HGRN_DV)], axis=1).astype(BF16)
    sw["ln1_g"], sw["ln1_b"] = ln1_g[:, None, :], ln1_b[:, None, :]
    sw["ln2_g"] = jnp.broadcast_to(ln2_g[:, :, None], (depth, D_MODEL, LANE))
    sw["ln2_b"] = jnp.broadcast_to(ln2_b[:, :, None], (depth, D_MODEL, LANE))
    sw["wqt"] = jnp.swapaxes(peer_w_q, 1, 2).astype(BF16)
    sw["subkeys"] = peer_subkeys.astype(BF16)
    sw["u"] = (peer_u * SQRT_HALF).astype(BF16)
    sw["vt"] = jnp.swapaxes(peer_v.astype(BF16).reshape(depth, -1, PEER_TE, D_MODEL), 2, 3)
    sw["plet"] = jnp.swapaxes(ple_proj, 1, 2).astype(BF16)
    sw["gatet"] = jnp.swapaxes(ple_gate, 1, 2).astype(BF16)
    return sw


_BIG_TABLES = ("u", "vt")


def _unslot_state(st, dk, dv, rope=False):
    if rope:
        half = dk // 2
        st = jnp.concatenate([st[..., :half], st[..., SLOT // 2:SLOT // 2 + half]], axis=-1)
    return jnp.swapaxes(st[..., :dv, :dk], -2, -1)


def kernel(x_prompt, x_sample, p_prompt, p_sample, state_gla, state_ret, state_hgrn, w_in, gla_w_gate,
           gla_b_gate, gla_norm, ret_norm, hgrn_lb_logits, hgrn_norm, w_out, ln1_g, ln1_b, ln2_g, ln2_b,
           peer_w_q, peer_subkeys, peer_u, peer_v, ple_proj, ple_gate):
    bp, lp, _ = x_prompt.shape
    bs = x_sample.shape[0]
    assert x_sample.shape[1] == 1 and lp % CHUNK == 0

    lb = jnp.cumsum(jax.nn.softmax(hgrn_lb_logits.astype(F32), axis=0), axis=0)
    lb = lb - lb[0:1]
    tri_np, masks_np = _chunk_constants()
    tri, masks = jnp.asarray(tri_np, BF16), jnp.asarray(masks_np)
    dmat, qdec, kdec, sdec, gamma = _retention_constants()
    cos_p, sin_p = _rope_tables(jnp.arange(lp, dtype=F32))
    cos_s, sin_s = _rope_tables(PAST_LEN + jnp.arange(1, dtype=F32))
    sw = _stacked_weights(lb, w_in, gla_w_gate, gla_b_gate, gla_norm, ret_norm, hgrn_norm, w_out,
                          ln1_g, ln1_b, ln2_g, ln2_b, peer_w_q, peer_subkeys, peer_u, peer_v,
                          ple_proj, ple_gate)

    xp = x_prompt.reshape(bp * lp, D_MODEL)
    xs = x_sample.reshape(bs, D_MODEL)
    gla_p, ret_p, hgrn_p, gla_s, ret_s, hgrn_s = [], [], [], [], [], []
    for i in range(DEPTH):
        lw = {k: v[i] for k, v in sw.items() if k not in _BIG_TABLES}
        lw.update(cos_s=cos_s, sin_s=sin_s, gamma=gamma)
        tables = (sw["u"], sw["vt"], i)

        og, sg = _prompt_mixer_call(
            _gla_prompt_kernel, xp, bp, lp, lw["w_gla"],
            [lw["bgate"], lw["gla_norm"], tri, masks],
            [lw["w_gla"].shape[1], GROUP_W], "gla_prompt")
        orr, sr = _prompt_mixer_call(
            _ret_prompt_kernel, xp, bp, lp, lw["w_ret"],
            [(cos_p, None), (sin_p, None), lw["ret_norm"], dmat, qdec, kdec, sdec],
            [4 * GROUP_W], "ret_prompt")
        oh, sh = _prompt_mixer_call(
            _hgrn_prompt_kernel, xp, bp, lp, lw["w_hgrn"],
            [lw["loglb"], lw["l1mlb"], lw["hgrn_norm"], tri, masks],
            [4 * GROUP_W], "hgrn_prompt")
        xp = _peer_call(_out_call(og, orr, oh, xp, lw), p_prompt[i].reshape(bp * lp, PLE_DIM), lw, tables)
        gla_p.append(sg)
        ret_p.append(sr)
        hgrn_p.append(sh)

        o_s, nsg, nsr, nsh = _sample_mixer_call(xs, lw, state_gla[i], state_ret[i], state_hgrn[i])
        xs = _peer_call(
            _out_call(o_s[:, :GROUP_W], o_s[:, GROUP_W:2 * GROUP_W], o_s[:, 2 * GROUP_W:], xs, lw),
            p_sample[i].reshape(bs, PLE_DIM), lw, tables)
        gla_s.append(nsg)
        ret_s.append(nsr)
        hgrn_s.append(nsh)

    return (xp.reshape(bp, lp, D_MODEL), xs.reshape(bs, 1, D_MODEL),
            _unslot_state(jnp.stack(gla_p), GLA_DK, GLA_DV),
            _unslot_state(jnp.stack(ret_p), RET_DK, RET_DV, rope=True),
            _unslot_state(jnp.stack(hgrn_p), HGRN_DK, HGRN_DV),
            jnp.stack(gla_s).reshape(DEPTH, bs, N_HEADS, GLA_DK, GLA_DV),
            jnp.stack(ret_s).reshape(DEPTH, bs, N_HEADS, RET_DK, RET_DV),
            jnp.stack(hgrn_s).reshape(DEPTH, bs, N_HEADS, HGRN_DK, HGRN_DV))
```

```python
import functools

import numpy as np
import jax
import jax.numpy as jnp
from jax import lax
from jax.experimental import pallas as pl
from jax.experimental.pallas import tpu as pltpu

F32 = jnp.float32
BF16 = jnp.bfloat16

D_MODEL = 1024
DEPTH = 2
PAST_LEN = 16384
N_HEADS = 4
GLA_DK, GLA_DV = 48, 96
RET_DK, RET_DV = 48, 96
HGRN_DK, HGRN_DV = 64, 64
GLA_LOWRANK = 16
GLA_TAU = 16.0
ROPE_BASE = 10000.0
CHUNK = 64
CHUNK_UNROLL = 8
PEER_HEADS = 8
N_KEYS = 128
PEER_TOPK = 16
PEER_TE = 2048
ROUTE_HEADS_PER_TRIP = 4
PLE_DIM = 256
ALPHA = (2 * DEPTH) ** 0.25
NORM_EPS = 1e-5

LANE = 128
SUBLANE = 8
SLOT = LANE
GROUP_W = N_HEADS * SLOT
VMEM_LIMIT = 56 * 1024 * 1024
NEG_INF = float("-inf")


def _dot(a, b):
    return jnp.dot(a, b, preferred_element_type=F32)


def _dot_nt(a, b):
    return lax.dot_general(a, b, (((1,), (1,)), ((), ())), preferred_element_type=F32)


def _dot_tn(a, b):
    return lax.dot_general(a, b, (((0,), (0,)), ((), ())), preferred_element_type=F32)


def _sigmoid(x):
    return jax.nn.sigmoid(x)


def _silu(x):
    return x * _sigmoid(x)


def _log_sigmoid(x):
    return jnp.minimum(x, 0.0) - jnp.log1p(jnp.exp(-jnp.abs(x)))


def _logaddexp(a, c):
    amax = jnp.maximum(a, c)
    delta = a - c
    return jnp.where(jnp.isnan(delta), a + c, amax + jnp.log1p(jnp.exp(-jnp.abs(delta))))


SQRT_HALF = np.float32(0.7071067811865476)


def _lane_mask(n):
    return (lax.broadcasted_iota(jnp.int32, (1, LANE), 1) < n).astype(F32)


def _rms_head_norm(o, g_row, dv):
    ms = jnp.sum(o * o, axis=-1, keepdims=True) * (1.0 / dv)
    return o * lax.rsqrt(ms + NORM_EPS) * g_row


def _group_head_norm(o, g_row, dv):
    mask = _lane_mask(dv)
    mu = jnp.sum(o, axis=-1, keepdims=True) * (1.0 / dv)
    d = (o - mu) * mask
    var = jnp.sum(d * d, axis=-1, keepdims=True) * (1.0 / dv)
    return d * lax.rsqrt(var + NORM_EPS) * g_row


def _chunk_constants():
    i = np.arange(CHUNK)[:, None]
    t = np.arange(CHUNK)[None, :]
    masks = [i == t]
    half = CHUNK // 2
    while half >= 1:
        blk = i // (2 * half)
        second = (i % (2 * half)) >= half
        masks.append(second & ((t % (2 * half)) < half) & (blk == t // (2 * half)))
        half //= 2
    return (t <= i).astype(np.float32), np.stack(masks).astype(np.float32)


def _cumsum_rows(tri_bf, g):
    hi = g.astype(BF16)
    r1 = g - hi.astype(F32)
    mid = r1.astype(BF16)
    lo = (r1 - mid.astype(F32)).astype(BF16)
    return _dot(tri_bf, hi) + _dot(tri_bf, mid) + _dot(tri_bf, lo)


def _level_factors(b, g):
    width = b.shape[1]
    grouped = (CHUNK // SUBLANE, SUBLANE, width)
    row = lax.broadcasted_iota(jnp.int32, b.shape, 0)
    sub = lax.broadcasted_iota(jnp.int32, grouped, 1)
    b3 = b.reshape(grouped)

    def sub_ref(r):
        return jnp.broadcast_to(b3[:, r:r + 1, :], grouped)

    out = []
    half = CHUNK // 2
    while half >= SUBLANE:
        ref = jnp.concatenate(
            [jnp.broadcast_to(b[m * 2 * half + half - 1:m * 2 * half + half], (2 * half, width))
             for m in range(CHUNK // (2 * half))], axis=0)
        out.append(jnp.exp(-jnp.abs(b - ref)))
        half //= 2
    out.append(jnp.exp(-jnp.abs(b3 - sub_ref(3))).reshape(b.shape))
    out.append(jnp.exp(-jnp.abs(b3 - jnp.where(sub < 4, sub_ref(1), sub_ref(5)))).reshape(b.shape))
    out.append(jnp.exp(jnp.where(row % 2 == 1, g, 0.0)))
    return out


def _vector_decay_chunk(q, k, v, g, st_ref, tri_bf, masks_ref):
    heads = [slice(h * SLOT, (h + 1) * SLOT) for h in range(N_HEADS)]
    b = _cumsum_rows(tri_bf, g)
    b_last = b[CHUNK - 1:CHUNK]
    factors = _level_factors(b, g)
    qs = [q.astype(BF16)] + [(q * f).astype(BF16) for f in factors]
    ks = [k.astype(BF16)] + [(k * f).astype(BF16) for f in factors]
    q_in = (q * jnp.exp(b)).astype(BF16)
    k_out = (k * jnp.exp(b_last - b)).astype(BF16)
    decay = jnp.exp(b_last)
    vb = v.astype(BF16)
    scores = []
    for sl in heads:
        sc = masks_ref[0] * _dot_nt(qs[0][:, sl], ks[0][:, sl])
        for l in range(len(factors)):
            sc = sc + masks_ref[1 + l] * _dot_nt(qs[1 + l][:, sl], ks[1 + l][:, sl])
        scores.append(sc.astype(BF16))
    outs = []
    for h, sl in enumerate(heads):
        st = st_ref[0, h]
        outs.append(_dot(scores[h], vb[:, sl]) + _dot_nt(q_in[:, sl], st.astype(BF16)))
        st_ref[0, h] = st * decay[:, sl] + _dot_tn(vb[:, sl], k_out[:, sl])
    return outs


def _gla_prompt_kernel(x_ref, w_ref, bg_ref, nrm_ref, tri_ref, masks_ref,
                       o_ref, st_ref, z_scr, la_scr, *, seg):
    @pl.when(pl.program_id(1) == 0)
    def _():
        st_ref[...] = jnp.zeros_like(st_ref)

    z_scr[...] = _dot(x_ref[...].astype(BF16), w_ref[...])
    pre = z_scr[:, 4 * GROUP_W:5 * GROUP_W] + bg_ref[...]
    la_scr[...] = _log_sigmoid(pre) * (1.0 / GLA_TAU)
    tri = tri_ref[...]

    def body(c, carry):
        r0 = pl.multiple_of(c * CHUNK, CHUNK)
        rows = pl.ds(r0, CHUNK)
        q = z_scr[rows, 0:GROUP_W] * (GLA_DK ** -0.5)
        k = z_scr[rows, GROUP_W:2 * GROUP_W]
        v = z_scr[rows, 2 * GROUP_W:3 * GROUP_W]
        outs = _vector_decay_chunk(q, k, v, la_scr[rows, :], st_ref, tri, masks_ref)
        for h in range(N_HEADS):
            gate = z_scr[rows, 3 * GROUP_W + h * SLOT:3 * GROUP_W + (h + 1) * SLOT]
            on = _rms_head_norm(outs[h], nrm_ref[h:h + 1, :], GLA_DV)
            o_ref[rows, h * SLOT:(h + 1) * SLOT] = (on * _silu(gate)).astype(BF16)
        return carry

    lax.fori_loop(0, seg // CHUNK, body, 0, unroll=CHUNK_UNROLL)


def _hgrn_prompt_kernel(x_ref, w_ref, loglb_ref, l1mlb_ref, nrm_ref, tri_ref, masks_ref,
                        o_ref, st_ref, z_scr, *, seg):
    @pl.when(pl.program_id(1) == 0)
    def _():
        st_ref[...] = jnp.zeros_like(st_ref)

    z_scr[...] = _dot(x_ref[...].astype(BF16), w_ref[...])
    tri = tri_ref[...]
    kmask = jnp.concatenate([_lane_mask(HGRN_DK)] * N_HEADS, axis=1)

    def body(c, carry):
        r0 = pl.multiple_of(c * CHUNK, CHUNK)
        rows = pl.ds(r0, CHUNK)
        q = _silu(z_scr[rows, 0:GROUP_W])
        v = z_scr[rows, 2 * GROUP_W:3 * GROUP_W]
        log_f = _logaddexp(loglb_ref[...], l1mlb_ref[...] + _log_sigmoid(z_scr[rows, GROUP_W:2 * GROUP_W]))
        k = (1.0 - jnp.exp(log_f)) * kmask
        outs = _vector_decay_chunk(q, k, v, log_f, st_ref, tri, masks_ref)
        for h in range(N_HEADS):
            sl = slice(h * SLOT, (h + 1) * SLOT)
            gate = z_scr[rows, 3 * GROUP_W + h * SLOT:3 * GROUP_W + (h + 1) * SLOT]
            on = _rms_head_norm(outs[h], nrm_ref[h:h + 1, :], HGRN_DV)
            o_ref[rows, sl] = (on * _silu(gate)).astype(BF16)
        return carry

    lax.fori_loop(0, seg // CHUNK, body, 0, unroll=CHUNK_UNROLL)


def _rotate(t, cs, sn):
    return t * cs + pltpu.roll(t, SLOT // 2, 1) * sn


def _ret_prompt_kernel(x_ref, w_ref, cos_ref, sin_ref, nrm_ref, dmat_ref, qdec_ref, kdec_ref, sdec_ref,
                       o_ref, st_ref, z_scr, *, seg):
    @pl.when(pl.program_id(1) == 0)
    def _():
        st_ref[...] = jnp.zeros_like(st_ref)

    z_scr[...] = _dot(x_ref[...].astype(BF16), w_ref[...])

    def body(c, carry):
        r0 = pl.multiple_of(c * CHUNK, CHUNK)
        rows = pl.ds(r0, CHUNK)
        cs = cos_ref[rows, :]
        sn = sin_ref[rows, :]
        heads = [slice(h * SLOT, (h + 1) * SLOT) for h in range(N_HEADS)]
        qs = [_rotate(z_scr[rows, h * SLOT:(h + 1) * SLOT], cs, sn) for h in range(N_HEADS)]
        ks = [_rotate(z_scr[rows, GROUP_W + h * SLOT:GROUP_W + (h + 1) * SLOT], cs, sn) * (RET_DK ** -0.5)
              for h in range(N_HEADS)]
        vb = z_scr[rows, 2 * GROUP_W:3 * GROUP_W].astype(BF16)
        scores = [(dmat_ref[h] * _dot_nt(qs[h].astype(BF16), ks[h].astype(BF16))).astype(BF16)
                  for h in range(N_HEADS)]
        q_in = [(qs[h] * qdec_ref[h]).astype(BF16) for h in range(N_HEADS)]
        k_out = [(ks[h] * kdec_ref[h]).astype(BF16) for h in range(N_HEADS)]
        outs = []
        for h, sl in enumerate(heads):
            st = st_ref[0, h]
            outs.append(_dot(scores[h], vb[:, sl]) + _dot_nt(q_in[h], st.astype(BF16)))
            st_ref[0, h] = st * sdec_ref[h] + _dot_tn(vb[:, sl], k_out[h])
        for h, sl in enumerate(heads):
            gate = z_scr[rows, 3 * GROUP_W + h * SLOT:3 * GROUP_W + (h + 1) * SLOT]
            on = _group_head_norm(outs[h], nrm_ref[h:h + 1, :], RET_DV)
            o_ref[rows, sl] = (on * _silu(gate)).astype(BF16)
        return carry

    lax.fori_loop(0, seg // CHUNK, body, 0, unroll=CHUNK_UNROLL)


def _const_spec(shape):
    nd = len(shape)
    return pl.BlockSpec(shape, lambda *_: (0,) * nd)


def _prompt_mixer_call(kernel, x2d, batch, seq, w, extras, scratch_widths, name):
    seg = min(512, seq)
    nseg = seq // seg
    in_specs = [pl.BlockSpec((seg, D_MODEL), lambda b, s: (b * nseg + s, 0)), _const_spec(w.shape)]
    args = [x2d, w]
    for e in extras:
        if isinstance(e, tuple):
            arr, _ = e
            in_specs.append(pl.BlockSpec((seg, arr.shape[1]), lambda b, s: (s, 0)))
            args.append(arr)
        else:
            in_specs.append(_const_spec(e.shape))
            args.append(e)
    return pl.pallas_call(
        functools.partial(kernel, seg=seg),
        grid=(batch, nseg),
        in_specs=in_specs,
        out_specs=[pl.BlockSpec((seg, GROUP_W), lambda b, s: (b * nseg + s, 0)),
                   pl.BlockSpec((1, N_HEADS, SLOT, SLOT), lambda b, s: (b, 0, 0, 0))],
        out_shape=[jax.ShapeDtypeStruct((batch * seq, GROUP_W), BF16),
                   jax.ShapeDtypeStruct((batch, N_HEADS, SLOT, SLOT), F32)],
        scratch_shapes=[pltpu.VMEM((seg, wd), F32) for wd in scratch_widths],
        compiler_params=pltpu.CompilerParams(dimension_semantics=("arbitrary", "arbitrary"),
                                             vmem_limit_bytes=VMEM_LIMIT),
        name=name,
    )(*args)


def _sample_step(q, k, eg, v, s_ref, ns_ref, tq, tk, te, tv, to, s_t, sn_t, dk, dv, row_of):
    tq[...] = q.T
    tk[...] = k.T
    te[...] = eg.T
    tv[...] = v.T
    s_t[0:dk * dv, :] = s_ref[...].T
    vt = tv[0:dv, :]

    def body(kk, oacc):
        kr = row_of(kk)
        r = pl.multiple_of(kk * dv, SUBLANE)
        sn = s_t[pl.ds(r, dv), :] * te[pl.ds(kr, 1), :] + tk[pl.ds(kr, 1), :] * vt
        sn_t[pl.ds(r, dv), :] = sn
        return oacc + tq[pl.ds(kr, 1), :] * sn

    o_t = lax.fori_loop(0, dk, body, jnp.zeros((dv, q.shape[0]), F32), unroll=4)
    ns_ref[...] = sn_t[0:dk * dv, :].T
    to[...] = jnp.zeros_like(to)
    to[0:dv, :] = o_t
    return to[...].T


def _sample_mixer_kernel(x_ref, wg_ref, wr_ref, wh_ref, bgate_ref, gn_ref, rn_ref, hn_ref,
                         cos_ref, sin_ref, gam_ref, loglb_ref, l1mlb_ref, sg_ref, sr_ref, sh_ref,
                         o_ref, nsg_ref, nsr_ref, nsh_ref,
                         zg, zr, zh, la, tq, tk, te, tv, to, s_t, sn_t):
    h = pl.program_id(0)

    @pl.when(h == 0)
    def _():
        xb = x_ref[...].astype(BF16)
        zg[...] = _dot(xb, wg_ref[...])
        zr[...] = _dot(xb, wr_ref[...])
        zh[...] = _dot(xb, wh_ref[...])
        pre = zg[:, 4 * GROUP_W:5 * GROUP_W] + bgate_ref[...]
        la[...] = _log_sigmoid(pre) * (1.0 / GLA_TAU)

    off = pl.multiple_of(h * SLOT, SLOT)
    sl = pl.ds(off, SLOT)
    tr = (tq, tk, te, tv, to, s_t, sn_t)
    batch = x_ref.shape[0]

    q = zg[:, sl] * (GLA_DK ** -0.5)
    k = zg[:, pl.ds(GROUP_W + off, SLOT)]
    v = zg[:, pl.ds(2 * GROUP_W + off, SLOT)]
    gate = zg[:, pl.ds(3 * GROUP_W + off, SLOT)]
    o = _sample_step(q, k, jnp.exp(la[:, sl]), v, sg_ref, nsg_ref, *tr, GLA_DK, GLA_DV, lambda kk: kk)
    o_ref[:, sl] = (_rms_head_norm(o, gn_ref[pl.ds(h, 1), :], GLA_DV) * _silu(gate)).astype(BF16)

    cs = cos_ref[...]
    sn = sin_ref[...]
    q = _rotate(zr[:, sl], cs, sn)
    k = _rotate(zr[:, pl.ds(GROUP_W + off, SLOT)], cs, sn) * (RET_DK ** -0.5)
    v = zr[:, pl.ds(2 * GROUP_W + off, SLOT)]
    gate = zr[:, pl.ds(3 * GROUP_W + off, SLOT)]
    eg = jnp.broadcast_to(gam_ref[h], (batch, SLOT))
    half = RET_DK // 2
    o = _sample_step(q, k, eg, v, sr_ref, nsr_ref, *tr, RET_DK, RET_DV,
                     lambda kk: kk + jnp.where(kk >= half, SLOT // 2 - half, 0))
    o_ref[:, pl.ds(GROUP_W + off, SLOT)] = (
        _group_head_norm(o, rn_ref[pl.ds(h, 1), :], RET_DV) * _silu(gate)).astype(BF16)

    q = _silu(zh[:, sl])
    hf = zh[:, pl.ds(GROUP_W + off, SLOT)]
    v = zh[:, pl.ds(2 * GROUP_W + off, SLOT)]
    gate = zh[:, pl.ds(3 * GROUP_W + off, SLOT)]
    log_f = _logaddexp(loglb_ref[:, sl], l1mlb_ref[:, sl] + _log_sigmoid(hf))
    f = jnp.exp(log_f)
    o = _sample_step(q, (1.0 - f) * _lane_mask(HGRN_DK), f, v, sh_ref, nsh_ref, *tr,
                     HGRN_DK, HGRN_DV, lambda kk: kk)
    o_ref[:, pl.ds(2 * GROUP_W + off, SLOT)] = (
        _rms_head_norm(o, hn_ref[pl.ds(h, 1), :], HGRN_DV) * _silu(gate)).astype(BF16)


def _sample_mixer_call(x2d, lw, sg, sr, sh):
    batch = x2d.shape[0]
    gsz, hsz = GLA_DK * GLA_DV, HGRN_DK * HGRN_DV
    consts = [lw["w_gla"], lw["w_ret"], lw["w_hgrn"], lw["bgate"], lw["gla_norm"], lw["ret_norm"],
              lw["hgrn_norm"], lw["cos_s"], lw["sin_s"], lw["gamma"], lw["loglb"], lw["l1mlb"]]
    in_specs = ([_const_spec(x2d.shape)] + [_const_spec(c.shape) for c in consts]
                + [pl.BlockSpec((batch, gsz), lambda h: (0, h)),
                   pl.BlockSpec((batch, gsz), lambda h: (0, h)),
                   pl.BlockSpec((batch, hsz), lambda h: (0, h))])
    out_specs = [_const_spec((batch, 3 * GROUP_W)),
                 pl.BlockSpec((batch, gsz), lambda h: (0, h)),
                 pl.BlockSpec((batch, gsz), lambda h: (0, h)),
                 pl.BlockSpec((batch, hsz), lambda h: (0, h))]
    out_shape = [jax.ShapeDtypeStruct((batch, 3 * GROUP_W), BF16),
                 jax.ShapeDtypeStruct((batch, N_HEADS * gsz), F32),
                 jax.ShapeDtypeStruct((batch, N_HEADS * gsz), F32),
                 jax.ShapeDtypeStruct((batch, N_HEADS * hsz), F32)]
    scratch = [pltpu.VMEM((batch, lw["w_gla"].shape[1]), F32), pltpu.VMEM((batch, 4 * GROUP_W), F32),
               pltpu.VMEM((batch, 4 * GROUP_W), F32), pltpu.VMEM((batch, GROUP_W), F32)]
    scratch += [pltpu.VMEM((SLOT, batch), F32) for _ in range(5)]
    scratch += [pltpu.VMEM((gsz, batch), F32), pltpu.VMEM((gsz, batch), F32)]
    return pl.pallas_call(
        _sample_mixer_kernel,
        grid=(N_HEADS,),
        in_specs=in_specs, out_specs=out_specs, out_shape=out_shape, scratch_shapes=scratch,
        compiler_params=pltpu.CompilerParams(dimension_semantics=("arbitrary",), vmem_limit_bytes=VMEM_LIMIT),
        name="sample_mixer",
    )(x2d, *consts, sg.reshape(batch, -1), sr.reshape(batch, -1), sh.reshape(batch, -1))


def _out_kernel(og_ref, or_ref, oh_ref, x_ref, wo_ref, g_ref, b_ref, xt_ref):
    mix = _dot(og_ref[...], wo_ref[0]) + _dot(or_ref[...], wo_ref[1]) + _dot(oh_ref[...], wo_ref[2])
    y = ALPHA * x_ref[...] + mix
    mu = jnp.mean(y, axis=-1, keepdims=True)
    d = y - mu
    var = jnp.mean(d * d, axis=-1, keepdims=True)
    xt_ref[...] = (d * lax.rsqrt(var + NORM_EPS) * g_ref[...] + b_ref[...]).T


def _out_call(og, orr, oh, x2d, lw):
    t = x2d.shape[0]
    tm = min(1024, t)
    row = lambda i: (i, 0)
    return pl.pallas_call(
        _out_kernel,
        grid=(t // tm,),
        in_specs=[pl.BlockSpec((tm, GROUP_W), row), pl.BlockSpec((tm, GROUP_W), row),
                  pl.BlockSpec((tm, GROUP_W), row), pl.BlockSpec((tm, D_MODEL), row),
                  _const_spec(lw["w_out"].shape), _const_spec((1, D_MODEL)), _const_spec((1, D_MODEL))],
        out_specs=pl.BlockSpec((D_MODEL, tm), lambda i: (0, i)),
        out_shape=jax.ShapeDtypeStruct((D_MODEL, t), F32),
        compiler_params=pltpu.CompilerParams(dimension_semantics=("arbitrary",), vmem_limit_bytes=VMEM_LIMIT),
        name="out_proj_ln",
    )(og, orr, oh, x2d, lw["w_out"], lw["ln1_g"], lw["ln1_b"])


def _oddeven_merge(lo, hi, r):
    step = r * 2
    if step < hi - lo:
        yield from _oddeven_merge(lo, hi, step)
        yield from _oddeven_merge(lo + r, hi, step)
        yield from [(i, i + r) for i in range(lo + r, hi - r, step)]
    else:
        yield (lo, lo + r)


def _oddeven_merge_sort(lo, hi):
    if hi - lo >= 1:
        mid = lo + (hi - lo) // 2
        yield from _oddeven_merge_sort(lo, mid)
        yield from _oddeven_merge_sort(mid + 1, hi)
        yield from _oddeven_merge(lo, hi, 1)


_SORT16 = tuple(_oddeven_merge_sort(0, PEER_TOPK - 1))
_BITONIC16 = tuple((i, i + d) for d in (8, 4, 2, 1) for i in range(PEER_TOPK) if i & d == 0)


def _compare_exchange(xs, pairs):
    for i, j in pairs:
        for x in xs:
            x[i], x[j] = jnp.maximum(x[i], x[j]), jnp.minimum(x[i], x[j])
    return xs


def _top16_sorted(tiles):
    xs = _compare_exchange([[a3[i] for i in range(PEER_TOPK)] for a3 in tiles], _SORT16)
    for shift in (4, 2, 1):
        ys = [[pltpu.roll(v, shift, 0) for v in x] for x in xs]
        xs = _compare_exchange([[jnp.maximum(x[i], y[PEER_TOPK - 1 - i]) for i in range(PEER_TOPK)]
                                for x, y in zip(xs, ys)], _BITONIC16)
    return xs


def _sublane_block(rows):
    sub = lax.broadcasted_iota(jnp.int32, (SUBLANE, LANE), 0)
    blk = rows[0]
    for r in range(1, SUBLANE):
        blk = jnp.where(sub == r, rows[r], blk)
    return blk


def _route_tile(a1, a2):
    v1, v2 = _top16_sorted([a1, a2])
    sub = lax.broadcasted_iota(jnp.int32, (SUBLANE, LANE), 0)
    v2a = _sublane_block(v2[0:8])
    v2b = _sublane_block(v2[8:16])
    v1b = _sublane_block(v1[8:16])
    cands = [v1[0] + v2a, v1[0] + v2b]
    for r1 in range(1, 8):
        cands.append(jnp.where(sub < PEER_TOPK // (r1 + 1), v1[r1] + v2a, NEG_INF))
    cands.append(v1b + v2[0])
    filler = jnp.full((SUBLANE, LANE), NEG_INF, F32)
    cand_tile = jnp.stack(cands + [filler] * (PEER_TOPK - len(cands)))
    thr = _top16_sorted([cand_tile])[0][PEER_TOPK - 1]
    top = v1[0] + v2[0]
    z8 = jnp.zeros((SUBLANE, LANE), F32)
    for blk in cands:
        z8 = z8 + jnp.where(blk >= thr, jnp.exp(blk - top), 0.0)
    z = jnp.broadcast_to(jnp.sum(z8, axis=0, keepdims=True), (SUBLANE, LANE))
    n1 = jnp.zeros(a1.shape, F32)
    rank2 = jnp.zeros(a2.shape, F32)
    for r in range(PEER_TOPK):
        n1 = jnp.where(a1 + v2[r] >= thr, r + 1.0, n1)
        rank2 = jnp.where(v2[r] > a2, r + 1.0, rank2)
    e1 = jnp.exp(a1 - v1[0]) * SQRT_HALF
    e2 = jnp.exp(a2 - v2[0]) / z
    return n1, rank2, e1, e2


def _peer_kernel(xt_ref, wqt_ref, sk_ref, u_ref, vt_ref, p_ref, plet_ref, gatet_ref, g2_ref, b2_ref,
                 out_ref, xbf, s_scr, n1, e1, rk2, e2, h_scr, w_scr, acc, *, tn, te, nj):
    j = pl.program_id(1)
    lane_tiles = tn // LANE
    groups = (N_KEYS // SUBLANE, SUBLANE, LANE)

    def lane_slice(lt):
        return pl.ds(pl.multiple_of(lt * LANE, LANE), LANE)

    @pl.when(j == 0)
    def _route():
        xbf[...] = xt_ref[...].astype(BF16)

        hpt = s_scr.shape[0] // 2

        def head_group_body(hp, carry):
            r = pl.multiple_of(hp * hpt * 2 * N_KEYS, hpt * 2 * N_KEYS)
            qh = _dot(wqt_ref[pl.ds(r, hpt * 2 * N_KEYS), :], xbf[...]).astype(BF16)
            for hh in range(hpt):
                for side in range(2):
                    rows = slice((2 * hh + side) * N_KEYS, (2 * hh + side + 1) * N_KEYS)
                    s_scr[2 * hh + side] = _dot(sk_ref[hpt * hp + hh, side], qh[rows])

            def lane_body(i, c):
                hh = i // lane_tiles
                lt = i % lane_tiles
                h = hpt * hp + hh
                lanes = lane_slice(lt)
                n1_t, rank2_t, e1_t, e2_t = _route_tile(s_scr[2 * hh, :, lanes].reshape(groups),
                                                        s_scr[2 * hh + 1, :, lanes].reshape(groups))
                n1[lt, h] = n1_t.reshape(N_KEYS, LANE)
                e1[lt, h] = e1_t.reshape(N_KEYS, LANE)
                rk2[lt, h] = rank2_t.reshape(N_KEYS, LANE).astype(BF16).reshape(rk2.shape[2:])
                e2[lt, h] = e2_t.reshape(N_KEYS, LANE).astype(BF16).reshape(e2.shape[2:])
                return c

            lax.fori_loop(0, hpt * lane_tiles, lane_body, 0)
            return carry

        lax.fori_loop(0, PEER_HEADS // hpt, head_group_body, 0)
        acc[...] = jnp.zeros_like(acc)

    na = te // N_KEYS
    assert na % SUBLANE == 0
    packed = rk2.shape[2:]
    row16 = (1, 2 * SUBLANE, LANE)

    hval = _dot(u_ref[...], xbf[...])
    for l2 in range(lane_tiles):
        hs = hval[:, l2 * LANE:(l2 + 1) * LANE].astype(BF16)
        h_scr[l2] = hs + hs * lax.erf(hs)

    def gate_body(p, carry):
        for grp in range(na // SUBLANE):
            a0 = pl.multiple_of(j * na + grp * SUBLANE, SUBLANE)
            n1blk = [n1[p, h, pl.ds(a0, SUBLANE), :] for h in range(PEER_HEADS)]
            e1blk = [e1[p, h, pl.ds(a0, SUBLANE), :] for h in range(PEER_HEADS)]
            for ai in range(0, SUBLANE, 2):
                gates = [None, None]
                for h in range(PEER_HEADS):
                    rk2h = rk2[p, h]
                    e2h = e2[p, h]
                    for d in range(2):
                        n1a = jnp.broadcast_to(n1blk[h][ai + d:ai + d + 1, :], row16[1:]).astype(BF16).reshape(row16)
                        e1a = jnp.broadcast_to(e1blk[h][ai + d:ai + d + 1, :], row16[1:]).astype(BF16).reshape(row16)
                        term = jnp.where(rk2h < n1a, e2h, jnp.zeros_like(e2h)) * e1a
                        gates[d] = term if h == 0 else gates[d] + term
                for d in range(2):
                    r0 = (grp * SUBLANE + ai + d) * N_KEYS
                    act = h_scr[p, r0:r0 + N_KEYS, :].reshape(packed)
                    w_scr[p, r0:r0 + N_KEYS, :] = (gates[d] * act).reshape(N_KEYS, LANE)
        return carry

    lax.fori_loop(0, lane_tiles, gate_body, 0, unroll=min(2, lane_tiles))
    acc[...] += _dot(vt_ref[...], jnp.concatenate([w_scr[l2] for l2 in range(lane_tiles)], axis=1))

    @pl.when(j == nj - 1)
    def _finish():
        def norm_body(lt, c):
            lanes = lane_slice(lt)
            y = ALPHA * xt_ref[:, lanes] + acc[:, lanes]
            mu = jnp.mean(y, axis=0, keepdims=True)
            d = y - mu
            var = jnp.mean(d * d, axis=0, keepdims=True)
            yn = d * lax.rsqrt(var + NORM_EPS) * g2_ref[...] + b2_ref[...]
            acc[:, lanes] = yn
            xbf[:, lanes] = yn.astype(BF16)
            return c

        lax.fori_loop(0, lane_tiles, norm_body, 0, unroll=True)
        emb = _dot_nt(plet_ref[...], p_ref[...].astype(BF16))
        gt = _dot(gatet_ref[...], xbf[...])
        out_ref[...] = (acc[...] + emb * _sigmoid(gt)).T


def _peer_call(xt, p2d, lw, tables):
    u_all, vt_all, layer = tables
    t = xt.shape[1]
    tn = min(512, t)
    te = PEER_TE
    nj = vt_all.shape[1]
    lane_tiles = tn // LANE
    once = dict(pipeline_mode=pl.Buffered(1))
    in_specs = [
        pl.BlockSpec((D_MODEL, tn), lambda i, j: (0, i)),
        pl.BlockSpec(lw["wqt"].shape, lambda i, j: (0, 0), **once),
        pl.BlockSpec(lw["subkeys"].shape, lambda i, j: (0, 0, 0, 0), **once),
        pl.BlockSpec((None, te, D_MODEL), lambda i, j: (layer, j, 0)),
        pl.BlockSpec((None, None, D_MODEL, te), lambda i, j: (layer, j, 0, 0)),
        pl.BlockSpec((tn, PLE_DIM), lambda i, j: (i, 0)),
        pl.BlockSpec(lw["plet"].shape, lambda i, j: (0, 0), **once),
        pl.BlockSpec(lw["gatet"].shape, lambda i, j: (0, 0), **once),
        pl.BlockSpec((D_MODEL, LANE), lambda i, j: (0, 0), **once),
        pl.BlockSpec((D_MODEL, LANE), lambda i, j: (0, 0), **once),
    ]
    route = (lane_tiles, PEER_HEADS, N_KEYS, LANE)
    route_packed = (lane_tiles, PEER_HEADS, N_KEYS // (2 * SUBLANE), 2 * SUBLANE, LANE)
    scratch = [pltpu.VMEM((D_MODEL, tn), BF16), pltpu.VMEM((2 * ROUTE_HEADS_PER_TRIP, N_KEYS, tn), F32),
               pltpu.VMEM(route, F32), pltpu.VMEM(route, F32),
               pltpu.VMEM(route_packed, BF16), pltpu.VMEM(route_packed, BF16),
               pltpu.VMEM((lane_tiles, te, LANE), BF16), pltpu.VMEM((lane_tiles, te, LANE), BF16),
               pltpu.VMEM((D_MODEL, tn), F32)]
    return pl.pallas_call(
        functools.partial(_peer_kernel, tn=tn, te=te, nj=nj),
        grid=(t // tn, nj),
        in_specs=in_specs,
        out_specs=pl.BlockSpec((tn, D_MODEL), lambda i, j: (i, 0)),
        out_shape=jax.ShapeDtypeStruct((t, D_MODEL), F32),
        scratch_shapes=scratch,
        compiler_params=pltpu.CompilerParams(dimension_semantics=("arbitrary", "arbitrary"),
                                             vmem_limit_bytes=VMEM_LIMIT),
        name="peer_ffn_ln_ple",
    )(xt, lw["wqt"], lw["subkeys"], u_all, vt_all, p2d, lw["plet"], lw["gatet"], lw["ln2_g"], lw["ln2_b"])


def _head_slots(w, d):
    lead = w.shape[:-1]
    pad = [(0, 0)] * (len(lead) + 1) + [(0, SLOT - d)]
    return jnp.pad(w.reshape(*lead, N_HEADS, d), pad).reshape(*lead, GROUP_W)


def _rope_slots(w, d):
    lead = w.shape[:-1]
    half = d // 2
    pad = [(0, 0)] * (len(lead) + 2) + [(0, SLOT // 2 - half)]
    return jnp.pad(w.reshape(*lead, N_HEADS, 2, half), pad).reshape(*lead, GROUP_W)


def _row_slots(w, d):
    depth, _, cols = w.shape
    w = jnp.pad(w.reshape(depth, N_HEADS, d, cols), ((0, 0), (0, 0), (0, SLOT - d), (0, 0)))
    return w.reshape(depth, GROUP_W, cols)


def _rope_tables(pos):
    half = RET_DK // 2
    inv = 1.0 / (ROPE_BASE ** (jnp.arange(0, RET_DK, 2, dtype=F32) / RET_DK))
    ang = pos[:, None] * inv[None, :]
    pad = ((0, 0), (0, SLOT // 2 - half))
    cos = jnp.pad(jnp.cos(ang), pad)
    sin = jnp.pad(jnp.sin(ang), pad)
    return jnp.concatenate([cos, cos], axis=1), jnp.concatenate([-sin, sin], axis=1)


def _retention_constants():
    log_gamma = jnp.log1p(-jnp.exp2(-5.0 - jnp.arange(N_HEADS, dtype=F32)))
    i = jnp.arange(CHUNK, dtype=F32)
    diff = i[:, None] - i[None, :]
    lg = log_gamma[:, None, None]
    dmat = jnp.where(diff >= 0, jnp.exp(jnp.where(diff >= 0, diff, 0.0) * lg), 0.0)
    qdec = jnp.broadcast_to(jnp.exp((i[None, :, None] + 1.0) * lg), (N_HEADS, CHUNK, SLOT))
    kdec = jnp.broadcast_to(jnp.exp((CHUNK - 1.0 - i[None, :, None]) * lg), (N_HEADS, CHUNK, SLOT))
    sdec = jnp.broadcast_to(jnp.exp(CHUNK * lg), (N_HEADS, 1, SLOT))
    gamma = jnp.broadcast_to(jnp.exp(lg), (N_HEADS, 1, SLOT))
    return dmat, qdec, kdec, sdec, gamma


def _stacked_weights(lb, w_in, gla_w_gate, gla_b_gate, gla_norm, ret_norm, hgrn_norm, w_out,
                     ln1_g, ln1_b, ln2_g, ln2_b, peer_w_q, peer_subkeys, peer_u, peer_v, ple_proj, ple_gate):
    depth = w_in.shape[0]
    sizes = (N_HEADS * GLA_DK, N_HEADS * GLA_DK, N_HEADS * GLA_DV, N_HEADS * GLA_DV, GLA_LOWRANK,
             N_HEADS * RET_DK, N_HEADS * RET_DK, N_HEADS * RET_DV, N_HEADS * RET_DV,
             N_HEADS * HGRN_DK, N_HEADS * HGRN_DK, N_HEADS * HGRN_DV, N_HEADS * HGRN_DV)
    offs = [int(c) for c in np.cumsum(sizes)[:-1]]
    gq, gk, gv, gg, glr, rq, rk, rv, rg, hq, hf, hi, hg = jnp.split(w_in, offs, axis=2)
    sw = {}
    gate_w = jnp.einsum("dir,drk->dik", glr, gla_w_gate, precision=lax.Precision.HIGHEST)
    sw["w_gla"] = jnp.concatenate(
        [_head_slots(gq, GLA_DK), _head_slots(gk, GLA_DK), _head_slots(gv, GLA_DV), _head_slots(gg, GLA_DV),
         _head_slots(gate_w, GLA_DK)], axis=2).astype(BF16)
    sw["w_ret"] = jnp.concatenate(
        [_rope_slots(rq, RET_DK), _rope_slots(rk, RET_DK), _head_slots(rv, RET_DV), _head_slots(rg, RET_DV)],
        axis=2).astype(BF16)
    sw["w_hgrn"] = jnp.concatenate([_head_slots(w, HGRN_DK) for w in (hq, hf, hi, hg)], axis=2).astype(BF16)
    sw["bgate"] = _head_slots(gla_b_gate, GLA_DK)[:, None, :]
    sw["gla_norm"] = jnp.pad(gla_norm, ((0, 0), (0, 0), (0, SLOT - GLA_DV)))
    sw["ret_norm"] = jnp.pad(ret_norm, ((0, 0), (0, 0), (0, SLOT - RET_DV)))
    sw["hgrn_norm"] = jnp.pad(hgrn_norm, ((0, 0), (0, 0), (0, SLOT - HGRN_DV)))
    lbh = lb.reshape(depth, 1, N_HEADS, HGRN_DK)
    pad = ((0, 0), (0, 0), (0, 0), (0, SLOT - HGRN_DK))
    sw["loglb"] = jnp.pad(jnp.log(lbh), pad, constant_values=-1.0).reshape(depth, 1, GROUP_W)
    sw["l1mlb"] = jnp.pad(jnp.log1p(-lbh), pad, constant_values=-1.0).reshape(depth, 1, GROUP_W)
    g_rows, r_rows = N_HEADS * GLA_DV, N_HEADS * RET_DV
    sw["w_out"] = jnp.stack([
        _row_slots(w_out[:, :g_rows], GLA_DV), _row_slots(w_out[:, g_rows:g_rows + r_rows], RET_DV),
        _row_slots(w_out[:, g_rows + r_rows:], HGRN_DV)], axis=1).astype(BF16)
    sw["ln1_g"], sw["ln1_b"] = ln1_g[:, None, :], ln1_b[:, None, :]
    sw["ln2_g"] = jnp.broadcast_to(ln2_g[:, :, None], (depth, D_MODEL, LANE))
    sw["ln2_b"] = jnp.broadcast_to(ln2_b[:, :, None], (depth, D_MODEL, LANE))
    sw["wqt"] = jnp.swapaxes(peer_w_q, 1, 2).astype(BF16)
    sw["subkeys"] = peer_subkeys.astype(BF16)
    sw["u"] = (peer_u * SQRT_HALF).astype(BF16)
    sw["vt"] = jnp.swapaxes(peer_v.reshape(depth, -1, PEER_TE, D_MODEL), 2, 3).astype(BF16)
    sw["plet"] = jnp.swapaxes(ple_proj, 1, 2).astype(BF16)
    sw["gatet"] = jnp.swapaxes(ple_gate, 1, 2).astype(BF16)
    return sw


_BIG_TABLES = ("u", "vt")


def _unslot_state(st, dk, dv, rope=False):
    if rope:
        half = dk // 2
        st = jnp.concatenate([st[..., :half], st[..., SLOT // 2:SLOT // 2 + half]], axis=-1)
    return jnp.swapaxes(st[..., :dv, :dk], -2, -1)


def kernel(x_prompt, x_sample, p_prompt, p_sample, state_gla, state_ret, state_hgrn, w_in, gla_w_gate,
           gla_b_gate, gla_norm, ret_norm, hgrn_lb_logits, hgrn_norm, w_out, ln1_g, ln1_b, ln2_g, ln2_b,
           peer_w_q, peer_subkeys, peer_u, peer_v, ple_proj, ple_gate):
    bp, lp, _ = x_prompt.shape
    bs = x_sample.shape[0]
    assert x_sample.shape[1] == 1 and lp % CHUNK == 0

    lb = jnp.cumsum(jax.nn.softmax(hgrn_lb_logits.astype(F32), axis=0), axis=0)
    lb = lb - lb[0:1]
    tri_np, masks_np = _chunk_constants()
    tri, masks = jnp.asarray(tri_np, BF16), jnp.asarray(masks_np)
    dmat, qdec, kdec, sdec, gamma = _retention_constants()
    cos_p, sin_p = _rope_tables(jnp.arange(lp, dtype=F32))
    cos_s, sin_s = _rope_tables(PAST_LEN + jnp.arange(1, dtype=F32))
    sw = _stacked_weights(lb, w_in, gla_w_gate, gla_b_gate, gla_norm, ret_norm, hgrn_norm, w_out,
                          ln1_g, ln1_b, ln2_g, ln2_b, peer_w_q, peer_subkeys, peer_u, peer_v,
                          ple_proj, ple_gate)

    xp = x_prompt.reshape(bp * lp, D_MODEL)
    xs = x_sample.reshape(bs, D_MODEL)
    gla_p, ret_p, hgrn_p, gla_s, ret_s, hgrn_s = [], [], [], [], [], []
    for i in range(DEPTH):
        lw = {k: v[i] for k, v in sw.items() if k not in _BIG_TABLES}
        lw.update(cos_s=cos_s, sin_s=sin_s, gamma=gamma)
        tables = (sw["u"], sw["vt"], i)

        og, sg = _prompt_mixer_call(
            _gla_prompt_kernel, xp, bp, lp, lw["w_gla"],
            [lw["bgate"], lw["gla_norm"], tri, masks],
            [lw["w_gla"].shape[1], GROUP_W], "gla_prompt")
        orr, sr = _prompt_mixer_call(
            _ret_prompt_kernel, xp, bp, lp, lw["w_ret"],
            [(cos_p, None), (sin_p, None), lw["ret_norm"], dmat, qdec, kdec, sdec],
            [4 * GROUP_W], "ret_prompt")
        oh, sh = _prompt_mixer_call(
            _hgrn_prompt_kernel, xp, bp, lp, lw["w_hgrn"],
            [lw["loglb"], lw["l1mlb"], lw["hgrn_norm"], tri, masks],
            [4 * GROUP_W], "hgrn_prompt")
        xp = _peer_call(_out_call(og, orr, oh, xp, lw), p_prompt[i].reshape(bp * lp, PLE_DIM), lw, tables)
        gla_p.append(sg)
        ret_p.append(sr)
        hgrn_p.append(sh)

        o_s, nsg, nsr, nsh = _sample_mixer_call(xs, lw, state_gla[i], state_ret[i], state_hgrn[i])
        xs = _peer_call(
            _out_call(o_s[:, :GROUP_W], o_s[:, GROUP_W:2 * GROUP_W], o_s[:, 2 * GROUP_W:], xs, lw),
            p_sample[i].reshape(bs, PLE_DIM), lw, tables)
        gla_s.append(nsg)
        ret_s.append(nsr)
        hgrn_s.append(nsh)

    return (xp.reshape(bp, lp, D_MODEL), xs.reshape(bs, 1, D_MODEL),
            _unslot_state(jnp.stack(gla_p), GLA_DK, GLA_DV),
            _unslot_state(jnp.stack(ret_p), RET_DK, RET_DV, rope=True),
            _unslot_state(jnp.stack(hgrn_p), HGRN_DK, HGRN_DV),
            jnp.stack(gla_s).reshape(DEPTH, bs, N_HEADS, GLA_DK, GLA_DV),
            jnp.stack(ret_s).reshape(DEPTH, bs, N_HEADS, RET_DK, RET_DV),
            jnp.stack(hgrn_s).reshape(DEPTH, bs, N_HEADS, HGRN_DK, HGRN_DV))
```

```python
import functools

import numpy as np
import jax
import jax.numpy as jnp
from jax import lax
from jax.experimental import pallas as pl
from jax.experimental.pallas import tpu as pltpu

F32 = jnp.float32
BF16 = jnp.bfloat16

D_MODEL = 1024
DEPTH = 2
PAST_LEN = 16384
N_HEADS = 4
GLA_DK, GLA_DV = 48, 96
RET_DK, RET_DV = 48, 96
HGRN_DK, HGRN_DV = 64, 64
GLA_LOWRANK = 16
GLA_TAU = 16.0
ROPE_BASE = 10000.0
CHUNK = 64
CHUNK_UNROLL = 8
PEER_HEADS = 8
N_KEYS = 128
PEER_TOPK = 16
PEER_TE = 2048
ROUTE_HEADS_PER_TRIP = 4
PLE_DIM = 256
ALPHA = (2 * DEPTH) ** 0.25
NORM_EPS = 1e-5

LANE = 128
SUBLANE = 8
SLOT = LANE
GROUP_W = N_HEADS * SLOT
VMEM_LIMIT = 56 * 1024 * 1024
NEG_INF = float("-inf")


def _dot(a, b):
    return jnp.dot(a, b, preferred_element_type=F32)


def _dot_nt(a, b):
    return lax.dot_general(a, b, (((1,), (1,)), ((), ())), preferred_element_type=F32)


def _dot_tn(a, b):
    return lax.dot_general(a, b, (((0,), (0,)), ((), ())), preferred_element_type=F32)


def _sigmoid(x):
    return jax.nn.sigmoid(x)


def _silu(x):
    return x * _sigmoid(x)


def _log_sigmoid(x):
    return jnp.minimum(x, 0.0) - jnp.log1p(jnp.exp(-jnp.abs(x)))


def _logaddexp(a, c):
    amax = jnp.maximum(a, c)
    delta = a - c
    return jnp.where(jnp.isnan(delta), a + c, amax + jnp.log1p(jnp.exp(-jnp.abs(delta))))


SQRT_HALF = np.float32(0.7071067811865476)


def _lane_mask(n):
    return (lax.broadcasted_iota(jnp.int32, (1, LANE), 1) < n).astype(F32)


def _rms_head_norm(o, g_row, dv):
    ms = jnp.sum(o * o, axis=-1, keepdims=True) * (1.0 / dv)
    return o * lax.rsqrt(ms + NORM_EPS) * g_row


def _group_head_norm(o, g_row, dv):
    mask = _lane_mask(dv)
    mu = jnp.sum(o, axis=-1, keepdims=True) * (1.0 / dv)
    d = (o - mu) * mask
    var = jnp.sum(d * d, axis=-1, keepdims=True) * (1.0 / dv)
    return d * lax.rsqrt(var + NORM_EPS) * g_row


def _chunk_constants():
    i = np.arange(CHUNK)[:, None]
    t = np.arange(CHUNK)[None, :]
    masks = [i == t]
    half = CHUNK // 2
    while half >= 1:
        blk = i // (2 * half)
        second = (i % (2 * half)) >= half
        masks.append(second & ((t % (2 * half)) < half) & (blk == t // (2 * half)))
        half //= 2
    return (t <= i).astype(np.float32), np.stack(masks).astype(np.float32)


def _cumsum_rows(tri_bf, g):
    hi = g.astype(BF16)
    r1 = g - hi.astype(F32)
    mid = r1.astype(BF16)
    lo = (r1 - mid.astype(F32)).astype(BF16)
    return _dot(tri_bf, hi) + _dot(tri_bf, mid) + _dot(tri_bf, lo)


def _level_factors(b, g):
    width = b.shape[1]
    grouped = (CHUNK // SUBLANE, SUBLANE, width)
    row = lax.broadcasted_iota(jnp.int32, b.shape, 0)
    sub = lax.broadcasted_iota(jnp.int32, grouped, 1)
    b3 = b.reshape(grouped)

    def sub_ref(r):
        return jnp.broadcast_to(b3[:, r:r + 1, :], grouped)

    out = []
    half = CHUNK // 2
    while half >= SUBLANE:
        ref = jnp.concatenate(
            [jnp.broadcast_to(b[m * 2 * half + half - 1:m * 2 * half + half], (2 * half, width))
             for m in range(CHUNK // (2 * half))], axis=0)
        out.append(jnp.exp(-jnp.abs(b - ref)))
        half //= 2
    out.append(jnp.exp(-jnp.abs(b3 - sub_ref(3))).reshape(b.shape))
    out.append(jnp.exp(-jnp.abs(b3 - jnp.where(sub < 4, sub_ref(1), sub_ref(5)))).reshape(b.shape))
    out.append(jnp.exp(jnp.where(row % 2 == 1, g, 0.0)))
    return out


def _vector_decay_chunk(q, k, v, g, st_ref, tri_bf, masks_ref):
    heads = [slice(h * SLOT, (h + 1) * SLOT) for h in range(N_HEADS)]
    b = _cumsum_rows(tri_bf, g)
    b_last = b[CHUNK - 1:CHUNK]
    factors = _level_factors(b, g)
    qs = [q.astype(BF16)] + [(q * f).astype(BF16) for f in factors]
    ks = [k.astype(BF16)] + [(k * f).astype(BF16) for f in factors]
    q_in = (q * jnp.exp(b)).astype(BF16)
    k_out = (k * jnp.exp(b_last - b)).astype(BF16)
    decay = jnp.exp(b_last)
    vb = v.astype(BF16)
    scores = []
    for sl in heads:
        sc = masks_ref[0] * _dot_nt(qs[0][:, sl], ks[0][:, sl])
        for l in range(len(factors)):
            sc = sc + masks_ref[1 + l] * _dot_nt(qs[1 + l][:, sl], ks[1 + l][:, sl])
        scores.append(sc.astype(BF16))
    outs = []
    for h, sl in enumerate(heads):
        st = st_ref[0, h]
        outs.append(_dot(scores[h], vb[:, sl]) + _dot_nt(q_in[:, sl], st.astype(BF16)))
        st_ref[0, h] = st * decay[:, sl] + _dot_tn(vb[:, sl], k_out[:, sl])
    return outs


def _gla_prompt_kernel(x_ref, w_ref, bg_ref, nrm_ref, tri_ref, masks_ref,
                       o_ref, st_ref, z_scr, la_scr, *, seg):
    @pl.when(pl.program_id(1) == 0)
    def _():
        st_ref[...] = jnp.zeros_like(st_ref)

    z_scr[...] = _dot(x_ref[...].astype(BF16), w_ref[...])
    pre = z_scr[:, 4 * GROUP_W:5 * GROUP_W] + bg_ref[...]
    la_scr[...] = _log_sigmoid(pre) * (1.0 / GLA_TAU)
    tri = tri_ref[...]

    def body(c, carry):
        r0 = pl.multiple_of(c * CHUNK, CHUNK)
        rows = pl.ds(r0, CHUNK)
        q = z_scr[rows, 0:GROUP_W] * (GLA_DK ** -0.5)
        k = z_scr[rows, GROUP_W:2 * GROUP_W]
        v = z_scr[rows, 2 * GROUP_W:3 * GROUP_W]
        outs = _vector_decay_chunk(q, k, v, la_scr[rows, :], st_ref, tri, masks_ref)
        for h in range(N_HEADS):
            gate = z_scr[rows, 3 * GROUP_W + h * SLOT:3 * GROUP_W + (h + 1) * SLOT]
            on = _rms_head_norm(outs[h], nrm_ref[h:h + 1, :], GLA_DV)
            o_ref[rows, h * SLOT:(h + 1) * SLOT] = (on * _silu(gate)).astype(BF16)
        return carry

    lax.fori_loop(0, seg // CHUNK, body, 0, unroll=CHUNK_UNROLL)


def _hgrn_prompt_kernel(x_ref, w_ref, loglb_ref, l1mlb_ref, nrm_ref, tri_ref, masks_ref,
                        o_ref, st_ref, z_scr, *, seg):
    @pl.when(pl.program_id(1) == 0)
    def _():
        st_ref[...] = jnp.zeros_like(st_ref)

    z_scr[...] = _dot(x_ref[...].astype(BF16), w_ref[...])
    tri = tri_ref[...]
    kmask = jnp.concatenate([_lane_mask(HGRN_DK)] * N_HEADS, axis=1)

    def body(c, carry):
        r0 = pl.multiple_of(c * CHUNK, CHUNK)
        rows = pl.ds(r0, CHUNK)
        q = _silu(z_scr[rows, 0:GROUP_W])
        v = z_scr[rows, 2 * GROUP_W:3 * GROUP_W]
        log_f = _logaddexp(loglb_ref[...], l1mlb_ref[...] + _log_sigmoid(z_scr[rows, GROUP_W:2 * GROUP_W]))
        k = (1.0 - jnp.exp(log_f)) * kmask
        outs = _vector_decay_chunk(q, k, v, log_f, st_ref, tri, masks_ref)
        for h in range(N_HEADS):
            sl = slice(h * SLOT, (h + 1) * SLOT)
            gate = z_scr[rows, 3 * GROUP_W + h * SLOT:3 * GROUP_W + (h + 1) * SLOT]
            on = _rms_head_norm(outs[h], nrm_ref[h:h + 1, :], HGRN_DV)
            o_ref[rows, sl] = (on * _silu(gate)).astype(BF16)
        return carry

    lax.fori_loop(0, seg // CHUNK, body, 0, unroll=CHUNK_UNROLL)


def _rotate(t, cs, sn):
    return t * cs + pltpu.roll(t, SLOT // 2, 1) * sn


def _ret_prompt_kernel(x_ref, w_ref, cos_ref, sin_ref, nrm_ref, dmat_ref, qdec_ref, kdec_ref, sdec_ref,
                       o_ref, st_ref, z_scr, *, seg):
    @pl.when(pl.program_id(1) == 0)
    def _():
        st_ref[...] = jnp.zeros_like(st_ref)

    z_scr[...] = _dot(x_ref[...].astype(BF16), w_ref[...])

    def body(c, carry):
        r0 = pl.multiple_of(c * CHUNK, CHUNK)
        rows = pl.ds(r0, CHUNK)
        cs = cos_ref[rows, :]
        sn = sin_ref[rows, :]
        heads = [slice(h * SLOT, (h + 1) * SLOT) for h in range(N_HEADS)]
        qs = [_rotate(z_scr[rows, h * SLOT:(h + 1) * SLOT], cs, sn) for h in range(N_HEADS)]
        ks = [_rotate(z_scr[rows, GROUP_W + h * SLOT:GROUP_W + (h + 1) * SLOT], cs, sn) * (RET_DK ** -0.5)
              for h in range(N_HEADS)]
        vb = z_scr[rows, 2 * GROUP_W:3 * GROUP_W].astype(BF16)
        scores = [(dmat_ref[h] * _dot_nt(qs[h].astype(BF16), ks[h].astype(BF16))).astype(BF16)
                  for h in range(N_HEADS)]
        q_in = [(qs[h] * qdec_ref[h]).astype(BF16) for h in range(N_HEADS)]
        k_out = [(ks[h] * kdec_ref[h]).astype(BF16) for h in range(N_HEADS)]
        outs = []
        for h, sl in enumerate(heads):
            st = st_ref[0, h]
            outs.append(_dot(scores[h], vb[:, sl]) + _dot_nt(q_in[h], st.astype(BF16)))
            st_ref[0, h] = st * sdec_ref[h] + _dot_tn(vb[:, sl], k_out[h])
        for h, sl in enumerate(heads):
            gate = z_scr[rows, 3 * GROUP_W + h * SLOT:3 * GROUP_W + (h + 1) * SLOT]
            on = _group_head_norm(outs[h], nrm_ref[h:h + 1, :], RET_DV)
            o_ref[rows, sl] = (on * _silu(gate)).astype(BF16)
        return carry

    lax.fori_loop(0, seg // CHUNK, body, 0, unroll=CHUNK_UNROLL)


def _const_spec(shape):
    nd = len(shape)
    return pl.BlockSpec(shape, lambda *_: (0,) * nd)


def _prompt_mixer_call(kernel, x2d, batch, seq, w, extras, scratch_widths, name):
    seg = min(512, seq)
    nseg = seq // seg
    in_specs = [pl.BlockSpec((seg, D_MODEL), lambda b, s: (b * nseg + s, 0)), _const_spec(w.shape)]
    args = [x2d, w]
    for e in extras:
        if isinstance(e, tuple):
            arr, _ = e
            in_specs.append(pl.BlockSpec((seg, arr.shape[1]), lambda b, s: (s, 0)))
            args.append(arr)
        else:
            in_specs.append(_const_spec(e.shape))
            args.append(e)
    return pl.pallas_call(
        functools.partial(kernel, seg=seg),
        grid=(batch, nseg),
        in_specs=in_specs,
        out_specs=[pl.BlockSpec((seg, GROUP_W), lambda b, s: (b * nseg + s, 0)),
                   pl.BlockSpec((1, N_HEADS, SLOT, SLOT), lambda b, s: (b, 0, 0, 0))],
        out_shape=[jax.ShapeDtypeStruct((batch * seq, GROUP_W), BF16),
                   jax.ShapeDtypeStruct((batch, N_HEADS, SLOT, SLOT), F32)],
        scratch_shapes=[pltpu.VMEM((seg, wd), F32) for wd in scratch_widths],
        compiler_params=pltpu.CompilerParams(dimension_semantics=("arbitrary", "arbitrary"),
                                             vmem_limit_bytes=VMEM_LIMIT),
        name=name,
    )(*args)


def _prompt_mixers_kernel(x_ref, wg_ref, bg_ref, gn_ref, tri_ref, masks_ref,
                          wr_ref, cos_ref, sin_ref, rn_ref, dmat_ref, qdec_ref, kdec_ref, sdec_ref,
                          wh_ref, loglb_ref, l1mlb_ref, hn_ref,
                          og_ref, sg_ref, or_ref, sr_ref, oh_ref, sh_ref, zg, la, zr, zh, *, seg):
    _gla_prompt_kernel(x_ref, wg_ref, bg_ref, gn_ref, tri_ref, masks_ref, og_ref, sg_ref, zg, la, seg=seg)
    _ret_prompt_kernel(x_ref, wr_ref, cos_ref, sin_ref, rn_ref, dmat_ref, qdec_ref, kdec_ref, sdec_ref,
                       or_ref, sr_ref, zr, seg=seg)
    _hgrn_prompt_kernel(x_ref, wh_ref, loglb_ref, l1mlb_ref, hn_ref, tri_ref, masks_ref, oh_ref, sh_ref, zh,
                        seg=seg)


def _prompt_mixers_call(x2d, batch, seq, lw, shared):
    tri, masks, cos_p, sin_p, dmat, qdec, kdec, sdec = shared
    seg = min(512, seq)
    nseg = seq // seg
    once = dict(pipeline_mode=pl.Buffered(1))

    def const(a):
        nd = a.ndim
        return pl.BlockSpec(a.shape, lambda b, s: (0,) * nd, **once)

    def table(a):
        return pl.BlockSpec((seg, a.shape[1]), lambda b, s: (s, 0))

    consts_g = [lw["w_gla"], lw["bgate"], lw["gla_norm"], tri, masks]
    consts_r = [lw["ret_norm"], dmat, qdec, kdec, sdec]
    consts_h = [lw["w_hgrn"], lw["loglb"], lw["l1mlb"], lw["hgrn_norm"]]
    args = [x2d] + consts_g + [lw["w_ret"], cos_p, sin_p] + consts_r + consts_h
    in_specs = ([pl.BlockSpec((seg, D_MODEL), lambda b, s: (b * nseg + s, 0))]
                + [const(a) for a in consts_g] + [const(lw["w_ret"]), table(cos_p), table(sin_p)]
                + [const(a) for a in consts_r] + [const(a) for a in consts_h])
    o_spec = pl.BlockSpec((seg, GROUP_W), lambda b, s: (b * nseg + s, 0))
    s_spec = pl.BlockSpec((1, N_HEADS, SLOT, SLOT), lambda b, s: (b, 0, 0, 0))
    o_shape = jax.ShapeDtypeStruct((batch * seq, GROUP_W), BF16)
    s_shape = jax.ShapeDtypeStruct((batch, N_HEADS, SLOT, SLOT), F32)
    widths = [lw["w_gla"].shape[1], GROUP_W, lw["w_ret"].shape[1], lw["w_hgrn"].shape[1]]
    return pl.pallas_call(
        functools.partial(_prompt_mixers_kernel, seg=seg),
        grid=(batch, nseg),
        in_specs=in_specs,
        out_specs=[o_spec, s_spec] * 3,
        out_shape=[o_shape, s_shape] * 3,
        scratch_shapes=[pltpu.VMEM((seg, wd), F32) for wd in widths],
        compiler_params=pltpu.CompilerParams(dimension_semantics=("arbitrary", "arbitrary"),
                                             vmem_limit_bytes=VMEM_LIMIT),
        name="prompt_mixers",
    )(*args)


def _sample_step(q, k, eg, v, s_ref, ns_ref, tq, tk, te, tv, to, s_t, sn_t, dk, dv, row_of):
    tq[...] = q.T
    tk[...] = k.T
    te[...] = eg.T
    tv[...] = v.T
    s_t[0:dk * dv, :] = s_ref[...].T
    vt = tv[0:dv, :]

    def body(kk, oacc):
        kr = row_of(kk)
        r = pl.multiple_of(kk * dv, SUBLANE)
        sn = s_t[pl.ds(r, dv), :] * te[pl.ds(kr, 1), :] + tk[pl.ds(kr, 1), :] * vt
        sn_t[pl.ds(r, dv), :] = sn
        return oacc + tq[pl.ds(kr, 1), :] * sn

    o_t = lax.fori_loop(0, dk, body, jnp.zeros((dv, q.shape[0]), F32), unroll=4)
    ns_ref[...] = sn_t[0:dk * dv, :].T
    to[...] = jnp.zeros_like(to)
    to[0:dv, :] = o_t
    return to[...].T


def _sample_mixer_kernel(x_ref, wg_ref, wr_ref, wh_ref, bgate_ref, gn_ref, rn_ref, hn_ref,
                         cos_ref, sin_ref, gam_ref, loglb_ref, l1mlb_ref, sg_ref, sr_ref, sh_ref,
                         o_ref, nsg_ref, nsr_ref, nsh_ref,
                         zg, zr, zh, la, tq, tk, te, tv, to, s_t, sn_t):
    h = pl.program_id(0)

    @pl.when(h == 0)
    def _():
        xb = x_ref[...].astype(BF16)
        zg[...] = _dot(xb, wg_ref[...])
        zr[...] = _dot(xb, wr_ref[...])
        zh[...] = _dot(xb, wh_ref[...])
        pre = zg[:, 4 * GROUP_W:5 * GROUP_W] + bgate_ref[...]
        la[...] = _log_sigmoid(pre) * (1.0 / GLA_TAU)

    off = pl.multiple_of(h * SLOT, SLOT)
    sl = pl.ds(off, SLOT)
    tr = (tq, tk, te, tv, to, s_t, sn_t)
    batch = x_ref.shape[0]

    q = zg[:, sl] * (GLA_DK ** -0.5)
    k = zg[:, pl.ds(GROUP_W + off, SLOT)]
    v = zg[:, pl.ds(2 * GROUP_W + off, SLOT)]
    gate = zg[:, pl.ds(3 * GROUP_W + off, SLOT)]
    o = _sample_step(q, k, jnp.exp(la[:, sl]), v, sg_ref, nsg_ref, *tr, GLA_DK, GLA_DV, lambda kk: kk)
    o_ref[:, sl] = (_rms_head_norm(o, gn_ref[pl.ds(h, 1), :], GLA_DV) * _silu(gate)).astype(BF16)

    cs = cos_ref[...]
    sn = sin_ref[...]
    q = _rotate(zr[:, sl], cs, sn)
    k = _rotate(zr[:, pl.ds(GROUP_W + off, SLOT)], cs, sn) * (RET_DK ** -0.5)
    v = zr[:, pl.ds(2 * GROUP_W + off, SLOT)]
    gate = zr[:, pl.ds(3 * GROUP_W + off, SLOT)]
    eg = jnp.broadcast_to(gam_ref[h], (batch, SLOT))
    half = RET_DK // 2
    o = _sample_step(q, k, eg, v, sr_ref, nsr_ref, *tr, RET_DK, RET_DV,
                     lambda kk: kk + jnp.where(kk >= half, SLOT // 2 - half, 0))
    o_ref[:, pl.ds(GROUP_W + off, SLOT)] = (
        _group_head_norm(o, rn_ref[pl.ds(h, 1), :], RET_DV) * _silu(gate)).astype(BF16)

    q = _silu(zh[:, sl])
    hf = zh[:, pl.ds(GROUP_W + off, SLOT)]
    v = zh[:, pl.ds(2 * GROUP_W + off, SLOT)]
    gate = zh[:, pl.ds(3 * GROUP_W + off, SLOT)]
    log_f = _logaddexp(loglb_ref[:, sl], l1mlb_ref[:, sl] + _log_sigmoid(hf))
    f = jnp.exp(log_f)
    o = _sample_step(q, (1.0 - f) * _lane_mask(HGRN_DK), f, v, sh_ref, nsh_ref, *tr,
                     HGRN_DK, HGRN_DV, lambda kk: kk)
    o_ref[:, pl.ds(2 * GROUP_W + off, SLOT)] = (
        _rms_head_norm(o, hn_ref[pl.ds(h, 1), :], HGRN_DV) * _silu(gate)).astype(BF16)


def _sample_mixer_call(x2d, lw, sg, sr, sh):
    batch = x2d.shape[0]
    gsz, hsz = GLA_DK * GLA_DV, HGRN_DK * HGRN_DV
    consts = [lw["w_gla"], lw["w_ret"], lw["w_hgrn"], lw["bgate"], lw["gla_norm"], lw["ret_norm"],
              lw["hgrn_norm"], lw["cos_s"], lw["sin_s"], lw["gamma"], lw["loglb"], lw["l1mlb"]]
    in_specs = ([_const_spec(x2d.shape)] + [_const_spec(c.shape) for c in consts]
                + [pl.BlockSpec((batch, gsz), lambda h: (0, h)),
                   pl.BlockSpec((batch, gsz), lambda h: (0, h)),
                   pl.BlockSpec((batch, hsz), lambda h: (0, h))])
    out_specs = [_const_spec((batch, 3 * GROUP_W)),
                 pl.BlockSpec((batch, gsz), lambda h: (0, h)),
                 pl.BlockSpec((batch, gsz), lambda h: (0, h)),
                 pl.BlockSpec((batch, hsz), lambda h: (0, h))]
    out_shape = [jax.ShapeDtypeStruct((batch, 3 * GROUP_W), BF16),
                 jax.ShapeDtypeStruct((batch, N_HEADS * gsz), F32),
                 jax.ShapeDtypeStruct((batch, N_HEADS * gsz), F32),
                 jax.ShapeDtypeStruct((batch, N_HEADS * hsz), F32)]
    scratch = [pltpu.VMEM((batch, lw["w_gla"].shape[1]), F32), pltpu.VMEM((batch, 4 * GROUP_W), F32),
               pltpu.VMEM((batch, 4 * GROUP_W), F32), pltpu.VMEM((batch, GROUP_W), F32)]
    scratch += [pltpu.VMEM((SLOT, batch), F32) for _ in range(5)]
    scratch += [pltpu.VMEM((gsz, batch), F32), pltpu.VMEM((gsz, batch), F32)]
    return pl.pallas_call(
        _sample_mixer_kernel,
        grid=(N_HEADS,),
        in_specs=in_specs, out_specs=out_specs, out_shape=out_shape, scratch_shapes=scratch,
        compiler_params=pltpu.CompilerParams(dimension_semantics=("arbitrary",), vmem_limit_bytes=VMEM_LIMIT),
        name="sample_mixer",
    )(x2d, *consts, sg.reshape(batch, -1), sr.reshape(batch, -1), sh.reshape(batch, -1))


def _out_kernel(og_ref, or_ref, oh_ref, x_ref, wo_ref, g_ref, b_ref, xt_ref):
    mix = _dot(og_ref[...], wo_ref[0]) + _dot(or_ref[...], wo_ref[1]) + _dot(oh_ref[...], wo_ref[2])
    y = ALPHA * x_ref[...] + mix
    mu = jnp.mean(y, axis=-1, keepdims=True)
    d = y - mu
    var = jnp.mean(d * d, axis=-1, keepdims=True)
    xt_ref[...] = (d * lax.rsqrt(var + NORM_EPS) * g_ref[...] + b_ref[...]).T


def _out_call(og, orr, oh, x2d, lw):
    t = x2d.shape[0]
    tm = min(1024, t)
    row = lambda i: (i, 0)
    return pl.pallas_call(
        _out_kernel,
        grid=(t // tm,),
        in_specs=[pl.BlockSpec((tm, GROUP_W), row), pl.BlockSpec((tm, GROUP_W), row),
                  pl.BlockSpec((tm, GROUP_W), row), pl.BlockSpec((tm, D_MODEL), row),
                  _const_spec(lw["w_out"].shape), _const_spec((1, D_MODEL)), _const_spec((1, D_MODEL))],
        out_specs=pl.BlockSpec((D_MODEL, tm), lambda i: (0, i)),
        out_shape=jax.ShapeDtypeStruct((D_MODEL, t), F32),
        compiler_params=pltpu.CompilerParams(dimension_semantics=("arbitrary",), vmem_limit_bytes=VMEM_LIMIT),
        name="out_proj_ln",
    )(og, orr, oh, x2d, lw["w_out"], lw["ln1_g"], lw["ln1_b"])


def _oddeven_merge(lo, hi, r):
    step = r * 2
    if step < hi - lo:
        yield from _oddeven_merge(lo, hi, step)
        yield from _oddeven_merge(lo + r, hi, step)
        yield from [(i, i + r) for i in range(lo + r, hi - r, step)]
    else:
        yield (lo, lo + r)


def _oddeven_merge_sort(lo, hi):
    if hi - lo >= 1:
        mid = lo + (hi - lo) // 2
        yield from _oddeven_merge_sort(lo, mid)
        yield from _oddeven_merge_sort(mid + 1, hi)
        yield from _oddeven_merge(lo, hi, 1)


_SORT16 = tuple(_oddeven_merge_sort(0, PEER_TOPK - 1))
_BITONIC16 = tuple((i, i + d) for d in (8, 4, 2, 1) for i in range(PEER_TOPK) if i & d == 0)


def _compare_exchange(xs, pairs):
    for i, j in pairs:
        for x in xs:
            x[i], x[j] = jnp.maximum(x[i], x[j]), jnp.minimum(x[i], x[j])
    return xs


def _top16_sorted(tiles):
    xs = _compare_exchange([[a3[i] for i in range(PEER_TOPK)] for a3 in tiles], _SORT16)
    for shift in (4, 2, 1):
        ys = [[pltpu.roll(v, shift, 0) for v in x] for x in xs]
        xs = _compare_exchange([[jnp.maximum(x[i], y[PEER_TOPK - 1 - i]) for i in range(PEER_TOPK)]
                                for x, y in zip(xs, ys)], _BITONIC16)
    return xs


def _sublane_block(rows):
    sub = lax.broadcasted_iota(jnp.int32, (SUBLANE, LANE), 0)
    blk = rows[0]
    for r in range(1, SUBLANE):
        blk = jnp.where(sub == r, rows[r], blk)
    return blk


def _route_tile(a1, a2):
    v1, v2 = _top16_sorted([a1, a2])
    sub = lax.broadcasted_iota(jnp.int32, (SUBLANE, LANE), 0)
    v2a = _sublane_block(v2[0:8])
    v2b = _sublane_block(v2[8:16])
    v1b = _sublane_block(v1[8:16])
    cands = [v1[0] + v2a, v1[0] + v2b]
    for r1 in range(1, 8):
        cands.append(jnp.where(sub < PEER_TOPK // (r1 + 1), v1[r1] + v2a, NEG_INF))
    cands.append(v1b + v2[0])
    filler = jnp.full((SUBLANE, LANE), NEG_INF, F32)
    cand_tile = jnp.stack(cands + [filler] * (PEER_TOPK - len(cands)))
    thr = _top16_sorted([cand_tile])[0][PEER_TOPK - 1]
    top = v1[0] + v2[0]
    z8 = jnp.zeros((SUBLANE, LANE), F32)
    for blk in cands:
        z8 = z8 + jnp.where(blk >= thr, jnp.exp(blk - top), 0.0)
    z = jnp.broadcast_to(jnp.sum(z8, axis=0, keepdims=True), (SUBLANE, LANE))
    n1 = jnp.zeros(a1.shape, F32)
    rank2 = jnp.zeros(a2.shape, F32)
    for r in range(PEER_TOPK):
        n1 = jnp.where(a1 + v2[r] >= thr, r + 1.0, n1)
        rank2 = jnp.where(v2[r] > a2, r + 1.0, rank2)
    e1 = jnp.exp(a1 - v1[0]) * SQRT_HALF
    e2 = jnp.exp(a2 - v2[0]) / z
    return n1, rank2, e1, e2


def _peer_kernel(xt_ref, wqt_ref, sk_ref, u_ref, vt_ref, p_ref, plet_ref, gatet_ref, g2_ref, b2_ref,
                 out_ref, xbf, s_scr, n1, e1, rk2, e2, h_scr, w_scr, acc, *, tn, te, nj):
    j = pl.program_id(1)
    lane_tiles = tn // LANE
    groups = (N_KEYS // SUBLANE, SUBLANE, LANE)

    def lane_slice(lt):
        return pl.ds(pl.multiple_of(lt * LANE, LANE), LANE)

    @pl.when(j == 0)
    def _route():
        xbf[...] = xt_ref[...].astype(BF16)

        hpt = s_scr.shape[0] // 2

        def head_group_body(hp, carry):
            r = pl.multiple_of(hp * hpt * 2 * N_KEYS, hpt * 2 * N_KEYS)
            qh = _dot(wqt_ref[pl.ds(r, hpt * 2 * N_KEYS), :], xbf[...]).astype(BF16)
            for hh in range(hpt):
                for side in range(2):
                    rows = slice((2 * hh + side) * N_KEYS, (2 * hh + side + 1) * N_KEYS)
                    s_scr[2 * hh + side] = _dot(sk_ref[hpt * hp + hh, side], qh[rows])

            def lane_body(i, c):
                hh = i // lane_tiles
                lt = i % lane_tiles
                h = hpt * hp + hh
                lanes = lane_slice(lt)
                n1_t, rank2_t, e1_t, e2_t = _route_tile(s_scr[2 * hh, :, lanes].reshape(groups),
                                                        s_scr[2 * hh + 1, :, lanes].reshape(groups))
                n1[lt, h] = n1_t.reshape(N_KEYS, LANE)
                e1[lt, h] = e1_t.reshape(N_KEYS, LANE)
                rk2[lt, h] = rank2_t.reshape(N_KEYS, LANE).astype(BF16).reshape(rk2.shape[2:])
                e2[lt, h] = e2_t.reshape(N_KEYS, LANE).astype(BF16).reshape(e2.shape[2:])
                return c

            lax.fori_loop(0, hpt * lane_tiles, lane_body, 0)
            return carry

        lax.fori_loop(0, PEER_HEADS // hpt, head_group_body, 0)
        acc[...] = jnp.zeros_like(acc)

    na = te // N_KEYS
    assert na % SUBLANE == 0
    packed = rk2.shape[2:]
    row16 = (1, 2 * SUBLANE, LANE)

    hval = _dot(u_ref[...], xbf[...])
    for l2 in range(lane_tiles):
        hs = hval[:, l2 * LANE:(l2 + 1) * LANE].astype(BF16)
        h_scr[l2] = hs + hs * lax.erf(hs)

    def gate_body(p, carry):
        for grp in range(na // SUBLANE):
            a0 = pl.multiple_of(j * na + grp * SUBLANE, SUBLANE)
            n1blk = [n1[p, h, pl.ds(a0, SUBLANE), :] for h in range(PEER_HEADS)]
            e1blk = [e1[p, h, pl.ds(a0, SUBLANE), :] for h in range(PEER_HEADS)]
            for ai in range(0, SUBLANE, 2):
                gates = [None, None]
                for h in range(PEER_HEADS):
                    rk2h = rk2[p, h]
                    e2h = e2[p, h]
                    for d in range(2):
                        n1a = jnp.broadcast_to(n1blk[h][ai + d:ai + d + 1, :], row16[1:]).astype(BF16).reshape(row16)
                        e1a = jnp.broadcast_to(e1blk[h][ai + d:ai + d + 1, :], row16[1:]).astype(BF16).reshape(row16)
                        term = jnp.where(rk2h < n1a, e2h, jnp.zeros_like(e2h)) * e1a
                        gates[d] = term if h == 0 else gates[d] + term
                for d in range(2):
                    r0 = (grp * SUBLANE + ai + d) * N_KEYS
                    act = h_scr[p, r0:r0 + N_KEYS, :].reshape(packed)
                    w_scr[p, r0:r0 + N_KEYS, :] = (gates[d] * act).reshape(N_KEYS, LANE)
        return carry

    lax.fori_loop(0, lane_tiles, gate_body, 0, unroll=min(2, lane_tiles))
    acc[...] += _dot(vt_ref[...], jnp.concatenate([w_scr[l2] for l2 in range(lane_tiles)], axis=1))

    @pl.when(j == nj - 1)
    def _finish():
        def norm_body(lt, c):
            lanes = lane_slice(lt)
            y = ALPHA * xt_ref[:, lanes] + acc[:, lanes]
            mu = jnp.mean(y, axis=0, keepdims=True)
            d = y - mu
            var = jnp.mean(d * d, axis=0, keepdims=True)
            yn = d * lax.rsqrt(var + NORM_EPS) * g2_ref[...] + b2_ref[...]
            acc[:, lanes] = yn
            xbf[:, lanes] = yn.astype(BF16)
            return c

        lax.fori_loop(0, lane_tiles, norm_body, 0, unroll=True)
        emb = _dot_nt(plet_ref[...], p_ref[...].astype(BF16))
        gt = _dot(gatet_ref[...], xbf[...])
        out_ref[...] = (acc[...] + emb * _sigmoid(gt)).T


def _peer_call(xt, p2d, lw, tables):
    u_all, vt_all, layer = tables
    t = xt.shape[1]
    tn = min(512, t)
    te = PEER_TE
    nj = vt_all.shape[1]
    lane_tiles = tn // LANE
    once = dict(pipeline_mode=pl.Buffered(1))
    in_specs = [
        pl.BlockSpec((D_MODEL, tn), lambda i, j: (0, i)),
        pl.BlockSpec(lw["wqt"].shape, lambda i, j: (0, 0), **once),
        pl.BlockSpec(lw["subkeys"].shape, lambda i, j: (0, 0, 0, 0), **once),
        pl.BlockSpec((None, te, D_MODEL), lambda i, j: (layer, j, 0)),
        pl.BlockSpec((None, None, D_MODEL, te), lambda i, j: (layer, j, 0, 0)),
        pl.BlockSpec((tn, PLE_DIM), lambda i, j: (i, 0)),
        pl.BlockSpec(lw["plet"].shape, lambda i, j: (0, 0), **once),
        pl.BlockSpec(lw["gatet"].shape, lambda i, j: (0, 0), **once),
        pl.BlockSpec((D_MODEL, LANE), lambda i, j: (0, 0), **once),
        pl.BlockSpec((D_MODEL, LANE), lambda i, j: (0, 0), **once),
    ]
    route = (lane_tiles, PEER_HEADS, N_KEYS, LANE)
    route_packed = (lane_tiles, PEER_HEADS, N_KEYS // (2 * SUBLANE), 2 * SUBLANE, LANE)
    scratch = [pltpu.VMEM((D_MODEL, tn), BF16), pltpu.VMEM((2 * ROUTE_HEADS_PER_TRIP, N_KEYS, tn), F32),
               pltpu.VMEM(route, F32), pltpu.VMEM(route, F32),
               pltpu.VMEM(route_packed, BF16), pltpu.VMEM(route_packed, BF16),
               pltpu.VMEM((lane_tiles, te, LANE), BF16), pltpu.VMEM((lane_tiles, te, LANE), BF16),
               pltpu.VMEM((D_MODEL, tn), F32)]
    return pl.pallas_call(
        functools.partial(_peer_kernel, tn=tn, te=te, nj=nj),
        grid=(t // tn, nj),
        in_specs=in_specs,
        out_specs=pl.BlockSpec((tn, D_MODEL), lambda i, j: (i, 0)),
        out_shape=jax.ShapeDtypeStruct((t, D_MODEL), F32),
        scratch_shapes=scratch,
        compiler_params=pltpu.CompilerParams(dimension_semantics=("arbitrary", "arbitrary"),
                                             vmem_limit_bytes=VMEM_LIMIT),
        name="peer_ffn_ln_ple",
    )(xt, lw["wqt"], lw["subkeys"], u_all, vt_all, p2d, lw["plet"], lw["gatet"], lw["ln2_g"], lw["ln2_b"])


def _head_slots(w, d):
    lead = w.shape[:-1]
    pad = [(0, 0)] * (len(lead) + 1) + [(0, SLOT - d)]
    return jnp.pad(w.reshape(*lead, N_HEADS, d), pad).reshape(*lead, GROUP_W)


def _rope_slots(w, d):
    lead = w.shape[:-1]
    half = d // 2
    pad = [(0, 0)] * (len(lead) + 2) + [(0, SLOT // 2 - half)]
    return jnp.pad(w.reshape(*lead, N_HEADS, 2, half), pad).reshape(*lead, GROUP_W)


def _row_slots(w, d):
    depth, _, cols = w.shape
    w = jnp.pad(w.reshape(depth, N_HEADS, d, cols), ((0, 0), (0, 0), (0, SLOT - d), (0, 0)))
    return w.reshape(depth, GROUP_W, cols)


def _rope_tables(pos):
    half = RET_DK // 2
    inv = 1.0 / (ROPE_BASE ** (jnp.arange(0, RET_DK, 2, dtype=F32) / RET_DK))
    ang = pos[:, None] * inv[None, :]
    pad = ((0, 0), (0, SLOT // 2 - half))
    cos = jnp.pad(jnp.cos(ang), pad)
    sin = jnp.pad(jnp.sin(ang), pad)
    return jnp.concatenate([cos, cos], axis=1), jnp.concatenate([-sin, sin], axis=1)


def _retention_constants():
    log_gamma = jnp.log1p(-jnp.exp2(-5.0 - jnp.arange(N_HEADS, dtype=F32)))
    i = jnp.arange(CHUNK, dtype=F32)
    diff = i[:, None] - i[None, :]
    lg = log_gamma[:, None, None]
    dmat = jnp.where(diff >= 0, jnp.exp(jnp.where(diff >= 0, diff, 0.0) * lg), 0.0)
    qdec = jnp.broadcast_to(jnp.exp((i[None, :, None] + 1.0) * lg), (N_HEADS, CHUNK, SLOT))
    kdec = jnp.broadcast_to(jnp.exp((CHUNK - 1.0 - i[None, :, None]) * lg), (N_HEADS, CHUNK, SLOT))
    sdec = jnp.broadcast_to(jnp.exp(CHUNK * lg), (N_HEADS, 1, SLOT))
    gamma = jnp.broadcast_to(jnp.exp(lg), (N_HEADS, 1, SLOT))
    return dmat, qdec, kdec, sdec, gamma


def _stacked_weights(lb, w_in, gla_w_gate, gla_b_gate, gla_norm, ret_norm, hgrn_norm, w_out,
                     ln1_g, ln1_b, ln2_g, ln2_b, peer_w_q, peer_subkeys, peer_u, peer_v, ple_proj, ple_gate):
    depth = w_in.shape[0]
    sizes = (N_HEADS * GLA_DK, N_HEADS * GLA_DK, N_HEADS * GLA_DV, N_HEADS * GLA_DV, GLA_LOWRANK,
             N_HEADS * RET_DK, N_HEADS * RET_DK, N_HEADS * RET_DV, N_HEADS * RET_DV,
             N_HEADS * HGRN_DK, N_HEADS * HGRN_DK, N_HEADS * HGRN_DV, N_HEADS * HGRN_DV)
    offs = [int(c) for c in np.cumsum(sizes)[:-1]]
    gq, gk, gv, gg, glr, rq, rk, rv, rg, hq, hf, hi, hg = jnp.split(w_in, offs, axis=2)
    sw = {}
    gate_w = jnp.einsum("dir,drk->dik", glr, gla_w_gate, precision=lax.Precision.HIGHEST)
    sw["w_gla"] = jnp.concatenate(
        [_head_slots(gq, GLA_DK), _head_slots(gk, GLA_DK), _head_slots(gv, GLA_DV), _head_slots(gg, GLA_DV),
         _head_slots(gate_w, GLA_DK)], axis=2).astype(BF16)
    sw["w_ret"] = jnp.concatenate(
        [_rope_slots(rq, RET_DK), _rope_slots(rk, RET_DK), _head_slots(rv, RET_DV), _head_slots(rg, RET_DV)],
        axis=2).astype(BF16)
    sw["w_hgrn"] = jnp.concatenate([_head_slots(w, HGRN_DK) for w in (hq, hf, hi, hg)], axis=2).astype(BF16)
    sw["bgate"] = _head_slots(gla_b_gate, GLA_DK)[:, None, :]
    sw["gla_norm"] = jnp.pad(gla_norm, ((0, 0), (0, 0), (0, SLOT - GLA_DV)))
    sw["ret_norm"] = jnp.pad(ret_norm, ((0, 0), (0, 0), (0, SLOT - RET_DV)))
    sw["hgrn_norm"] = jnp.pad(hgrn_norm, ((0, 0), (0, 0), (0, SLOT - HGRN_DV)))
    lbh = lb.reshape(depth, 1, N_HEADS, HGRN_DK)
    pad = ((0, 0), (0, 0), (0, 0), (0, SLOT - HGRN_DK))
    sw["loglb"] = jnp.pad(jnp.log(lbh), pad, constant_values=-1.0).reshape(depth, 1, GROUP_W)
    sw["l1mlb"] = jnp.pad(jnp.log1p(-lbh), pad, constant_values=-1.0).reshape(depth, 1, GROUP_W)
    g_rows, r_rows = N_HEADS * GLA_DV, N_HEADS * RET_DV
    sw["w_out"] = jnp.stack([
        _row_slots(w_out[:, :g_rows], GLA_DV), _row_slots(w_out[:, g_rows:g_rows + r_rows], RET_DV),
        _row_slots(w_out[:, g_rows + r_rows:], HGRN_DV)], axis=1).astype(BF16)
    sw["ln1_g"], sw["ln1_b"] = ln1_g[:, None, :], ln1_b[:, None, :]
    sw["ln2_g"] = jnp.broadcast_to(ln2_g[:, :, None], (depth, D_MODEL, LANE))
    sw["ln2_b"] = jnp.broadcast_to(ln2_b[:, :, None], (depth, D_MODEL, LANE))
    sw["wqt"] = jnp.swapaxes(peer_w_q, 1, 2).astype(BF16)
    sw["subkeys"] = peer_subkeys.astype(BF16)
    sw["u"] = (peer_u * SQRT_HALF).astype(BF16)
    sw["vt"] = jnp.swapaxes(peer_v.astype(BF16).reshape(depth, -1, PEER_TE, D_MODEL), 2, 3)
    sw["plet"] = jnp.swapaxes(ple_proj, 1, 2).astype(BF16)
    sw["gatet"] = jnp.swapaxes(ple_gate, 1, 2).astype(BF16)
    return sw


_BIG_TABLES = ("u", "vt")


def _unslot_state(st, dk, dv, rope=False):
    if rope:
        half = dk // 2
        st = jnp.concatenate([st[..., :half], st[..., SLOT // 2:SLOT // 2 + half]], axis=-1)
    return jnp.swapaxes(st[..., :dv, :dk], -2, -1)


def kernel(x_prompt, x_sample, p_prompt, p_sample, state_gla, state_ret, state_hgrn, w_in, gla_w_gate,
           gla_b_gate, gla_norm, ret_norm, hgrn_lb_logits, hgrn_norm, w_out, ln1_g, ln1_b, ln2_g, ln2_b,
           peer_w_q, peer_subkeys, peer_u, peer_v, ple_proj, ple_gate):
    bp, lp, _ = x_prompt.shape
    bs = x_sample.shape[0]
    assert x_sample.shape[1] == 1 and lp % CHUNK == 0

    lb = jnp.cumsum(jax.nn.softmax(hgrn_lb_logits.astype(F32), axis=0), axis=0)
    lb = lb - lb[0:1]
    tri_np, masks_np = _chunk_constants()
    tri, masks = jnp.asarray(tri_np, BF16), jnp.asarray(masks_np)
    dmat, qdec, kdec, sdec, gamma = _retention_constants()
    cos_p, sin_p = _rope_tables(jnp.arange(lp, dtype=F32))
    cos_s, sin_s = _rope_tables(PAST_LEN + jnp.arange(1, dtype=F32))
    sw = _stacked_weights(lb, w_in, gla_w_gate, gla_b_gate, gla_norm, ret_norm, hgrn_norm, w_out,
                          ln1_g, ln1_b, ln2_g, ln2_b, peer_w_q, peer_subkeys, peer_u, peer_v,
                          ple_proj, ple_gate)

    xp = x_prompt.reshape(bp * lp, D_MODEL)
    xs = x_sample.reshape(bs, D_MODEL)
    gla_p, ret_p, hgrn_p, gla_s, ret_s, hgrn_s = [], [], [], [], [], []
    for i in range(DEPTH):
        lw = {k: v[i] for k, v in sw.items() if k not in _BIG_TABLES}
        lw.update(cos_s=cos_s, sin_s=sin_s, gamma=gamma)
        tables = (sw["u"], sw["vt"], i)

        og, sg, orr, sr, oh, sh = _prompt_mixers_call(
            xp, bp, lp, lw, (tri, masks, cos_p, sin_p, dmat, qdec, kdec, sdec))
        xp = _peer_call(_out_call(og, orr, oh, xp, lw), p_prompt[i].reshape(bp * lp, PLE_DIM), lw, tables)
        gla_p.append(sg)
        ret_p.append(sr)
        hgrn_p.append(sh)

        o_s, nsg, nsr, nsh = _sample_mixer_call(xs, lw, state_gla[i], state_ret[i], state_hgrn[i])
        xs = _peer_call(
            _out_call(o_s[:, :GROUP_W], o_s[:, GROUP_W:2 * GROUP_W], o_s[:, 2 * GROUP_W:], xs, lw),
            p_sample[i].reshape(bs, PLE_DIM), lw, tables)
        gla_s.append(nsg)
        ret_s.append(nsr)
        hgrn_s.append(nsh)

    return (xp.reshape(bp, lp, D_MODEL), xs.reshape(bs, 1, D_MODEL),
            _unslot_state(jnp.stack(gla_p), GLA_DK, GLA_DV),
            _unslot_state(jnp.stack(ret_p), RET_DK, RET_DV, rope=True),
            _unslot_state(jnp.stack(hgrn_p), HGRN_DK, HGRN_DV),
            jnp.stack(gla_s).reshape(DEPTH, bs, N_HEADS, GLA_DK, GLA_DV),
            jnp.stack(ret_s).reshape(DEPTH, bs, N_HEADS, RET_DK, RET_DV),
            jnp.stack(hgrn_s).reshape(DEPTH, bs, N_HEADS, HGRN_DK, HGRN_DV))
```

```python
import functools

import numpy as np
import jax
import jax.numpy as jnp
from jax import lax
from jax.experimental import pallas as pl
from jax.experimental.pallas import tpu as pltpu

F32 = jnp.float32
BF16 = jnp.bfloat16

D_MODEL = 1024
DEPTH = 2
PAST_LEN = 16384
N_HEADS = 4
GLA_DK, GLA_DV = 48, 96
RET_DK, RET_DV = 48, 96
HGRN_DK, HGRN_DV = 64, 64
GLA_LOWRANK = 16
GLA_TAU = 16.0
ROPE_BASE = 10000.0
CHUNK = 64
CHUNK_UNROLL = 8
PEER_HEADS = 8
N_KEYS = 128
PEER_TOPK = 16
PEER_TE = 2048
ROUTE_HEADS_PER_TRIP = 4
PLE_DIM = 256
ALPHA = (2 * DEPTH) ** 0.25
NORM_EPS = 1e-5

LANE = 128
SUBLANE = 8
SLOT = LANE
GROUP_W = N_HEADS * SLOT
VMEM_LIMIT = 56 * 1024 * 1024
NEG_INF = float("-inf")


def _dot(a, b):
    return jnp.dot(a, b, preferred_element_type=F32)


def _dot_nt(a, b):
    return lax.dot_general(a, b, (((1,), (1,)), ((), ())), preferred_element_type=F32)


def _dot_tn(a, b):
    return lax.dot_general(a, b, (((0,), (0,)), ((), ())), preferred_element_type=F32)


def _sigmoid(x):
    return jax.nn.sigmoid(x)


def _silu(x):
    return x * _sigmoid(x)


def _log_sigmoid(x):
    return jnp.minimum(x, 0.0) - jnp.log1p(jnp.exp(-jnp.abs(x)))


def _logaddexp(a, c):
    amax = jnp.maximum(a, c)
    delta = a - c
    return jnp.where(jnp.isnan(delta), a + c, amax + jnp.log1p(jnp.exp(-jnp.abs(delta))))


SQRT_HALF = np.float32(0.7071067811865476)


def _lane_mask(n):
    return (lax.broadcasted_iota(jnp.int32, (1, LANE), 1) < n).astype(F32)


def _rms_head_norm(o, g_row, dv):
    ms = jnp.sum(o * o, axis=-1, keepdims=True) * (1.0 / dv)
    return o * lax.rsqrt(ms + NORM_EPS) * g_row


def _group_head_norm(o, g_row, dv):
    mask = _lane_mask(dv)
    mu = jnp.sum(o, axis=-1, keepdims=True) * (1.0 / dv)
    d = (o - mu) * mask
    var = jnp.sum(d * d, axis=-1, keepdims=True) * (1.0 / dv)
    return d * lax.rsqrt(var + NORM_EPS) * g_row


def _chunk_constants():
    i = np.arange(CHUNK)[:, None]
    t = np.arange(CHUNK)[None, :]
    masks = [i == t]
    half = CHUNK // 2
    while half >= 1:
        blk = i // (2 * half)
        second = (i % (2 * half)) >= half
        masks.append(second & ((t % (2 * half)) < half) & (blk == t // (2 * half)))
        half //= 2
    return (t <= i).astype(np.float32), np.stack(masks).astype(np.float32)


def _cumsum_rows(tri_bf, g):
    hi = g.astype(BF16)
    r1 = g - hi.astype(F32)
    mid = r1.astype(BF16)
    lo = (r1 - mid.astype(F32)).astype(BF16)
    return _dot(tri_bf, hi) + _dot(tri_bf, mid) + _dot(tri_bf, lo)


def _level_factors(b, g):
    width = b.shape[1]
    grouped = (CHUNK // SUBLANE, SUBLANE, width)
    row = lax.broadcasted_iota(jnp.int32, b.shape, 0)
    sub = lax.broadcasted_iota(jnp.int32, grouped, 1)
    b3 = b.reshape(grouped)

    def sub_ref(r):
        return jnp.broadcast_to(b3[:, r:r + 1, :], grouped)

    out = []
    half = CHUNK // 2
    while half >= SUBLANE:
        ref = jnp.concatenate(
            [jnp.broadcast_to(b[m * 2 * half + half - 1:m * 2 * half + half], (2 * half, width))
             for m in range(CHUNK // (2 * half))], axis=0)
        out.append(jnp.exp(-jnp.abs(b - ref)))
        half //= 2
    out.append(jnp.exp(-jnp.abs(b3 - sub_ref(3))).reshape(b.shape))
    out.append(jnp.exp(-jnp.abs(b3 - jnp.where(sub < 4, sub_ref(1), sub_ref(5)))).reshape(b.shape))
    out.append(jnp.exp(jnp.where(row % 2 == 1, g, 0.0)))
    return out


def _vector_decay_chunk(q, k, v, g, st_ref, tri_bf, masks_ref):
    heads = [slice(h * SLOT, (h + 1) * SLOT) for h in range(N_HEADS)]
    b = _cumsum_rows(tri_bf, g)
    b_last = b[CHUNK - 1:CHUNK]
    factors = _level_factors(b, g)
    qs = [q.astype(BF16)] + [(q * f).astype(BF16) for f in factors]
    ks = [k.astype(BF16)] + [(k * f).astype(BF16) for f in factors]
    q_in = (q * jnp.exp(b)).astype(BF16)
    k_out = (k * jnp.exp(b_last - b)).astype(BF16)
    decay = jnp.exp(b_last)
    vb = v.astype(BF16)
    scores = []
    for sl in heads:
        sc = masks_ref[0] * _dot_nt(qs[0][:, sl], ks[0][:, sl])
        for l in range(len(factors)):
            sc = sc + masks_ref[1 + l] * _dot_nt(qs[1 + l][:, sl], ks[1 + l][:, sl])
        scores.append(sc.astype(BF16))
    outs = []
    for h, sl in enumerate(heads):
        st = st_ref[0, h]
        outs.append(_dot(scores[h], vb[:, sl]) + _dot_nt(q_in[:, sl], st.astype(BF16)))
        st_ref[0, h] = st * decay[:, sl] + _dot_tn(vb[:, sl], k_out[:, sl])
    return outs


def _gla_prompt_kernel(x_ref, w_ref, bg_ref, nrm_ref, tri_ref, masks_ref,
                       o_ref, st_ref, z_scr, la_scr, *, seg):
    @pl.when(pl.program_id(1) == 0)
    def _():
        st_ref[...] = jnp.zeros_like(st_ref)

    z_scr[...] = _dot(x_ref[...].astype(BF16), w_ref[...])
    pre = z_scr[:, 4 * GROUP_W:5 * GROUP_W] + bg_ref[...]
    la_scr[...] = _log_sigmoid(pre) * (1.0 / GLA_TAU)
    tri = tri_ref[...]

    def body(c, carry):
        r0 = pl.multiple_of(c * CHUNK, CHUNK)
        rows = pl.ds(r0, CHUNK)
        q = z_scr[rows, 0:GROUP_W] * (GLA_DK ** -0.5)
        k = z_scr[rows, GROUP_W:2 * GROUP_W]
        v = z_scr[rows, 2 * GROUP_W:3 * GROUP_W]
        outs = _vector_decay_chunk(q, k, v, la_scr[rows, :], st_ref, tri, masks_ref)
        for h in range(N_HEADS):
            gate = z_scr[rows, 3 * GROUP_W + h * SLOT:3 * GROUP_W + (h + 1) * SLOT]
            on = _rms_head_norm(outs[h], nrm_ref[h:h + 1, :], GLA_DV)
            o_ref[rows, h * SLOT:(h + 1) * SLOT] = (on * _silu(gate)).astype(BF16)
        return carry

    lax.fori_loop(0, seg // CHUNK, body, 0, unroll=CHUNK_UNROLL)


def _hgrn_prompt_kernel(x_ref, w_ref, loglb_ref, l1mlb_ref, nrm_ref, tri_ref, masks_ref,
                        o_ref, st_ref, z_scr, *, seg):
    @pl.when(pl.program_id(1) == 0)
    def _():
        st_ref[...] = jnp.zeros_like(st_ref)

    z_scr[...] = _dot(x_ref[...].astype(BF16), w_ref[...])
    tri = tri_ref[...]
    kmask = jnp.concatenate([_lane_mask(HGRN_DK)] * N_HEADS, axis=1)

    def body(c, carry):
        r0 = pl.multiple_of(c * CHUNK, CHUNK)
        rows = pl.ds(r0, CHUNK)
        q = _silu(z_scr[rows, 0:GROUP_W])
        v = z_scr[rows, 2 * GROUP_W:3 * GROUP_W]
        log_f = _logaddexp(loglb_ref[...], l1mlb_ref[...] + _log_sigmoid(z_scr[rows, GROUP_W:2 * GROUP_W]))
        k = (1.0 - jnp.exp(log_f)) * kmask
        outs = _vector_decay_chunk(q, k, v, log_f, st_ref, tri, masks_ref)
        for h in range(N_HEADS):
            sl = slice(h * SLOT, (h + 1) * SLOT)
            gate = z_scr[rows, 3 * GROUP_W + h * SLOT:3 * GROUP_W + (h + 1) * SLOT]
            on = _rms_head_norm(outs[h], nrm_ref[h:h + 1, :], HGRN_DV)
            o_ref[rows, sl] = (on * _silu(gate)).astype(BF16)
        return carry

    lax.fori_loop(0, seg // CHUNK, body, 0, unroll=CHUNK_UNROLL)


def _rotate(t, cs, sn):
    return t * cs + pltpu.roll(t, SLOT // 2, 1) * sn


def _ret_prompt_kernel(x_ref, w_ref, cos_ref, sin_ref, nrm_ref, dmat_ref, qdec_ref, kdec_ref, sdec_ref,
                       o_ref, st_ref, z_scr, *, seg):
    @pl.when(pl.program_id(1) == 0)
    def _():
        st_ref[...] = jnp.zeros_like(st_ref)

    z_scr[...] = _dot(x_ref[...].astype(BF16), w_ref[...])

    def body(c, carry):
        r0 = pl.multiple_of(c * CHUNK, CHUNK)
        rows = pl.ds(r0, CHUNK)
        cs = cos_ref[rows, :]
        sn = sin_ref[rows, :]
        heads = [slice(h * SLOT, (h + 1) * SLOT) for h in range(N_HEADS)]
        qs = [_rotate(z_scr[rows, h * SLOT:(h + 1) * SLOT], cs, sn) for h in range(N_HEADS)]
        ks = [_rotate(z_scr[rows, GROUP_W + h * SLOT:GROUP_W + (h + 1) * SLOT], cs, sn) * (RET_DK ** -0.5)
              for h in range(N_HEADS)]
        vb = z_scr[rows, 2 * GROUP_W:3 * GROUP_W].astype(BF16)
        scores = [(dmat_ref[h] * _dot_nt(qs[h].astype(BF16), ks[h].astype(BF16))).astype(BF16)
                  for h in range(N_HEADS)]
        q_in = [(qs[h] * qdec_ref[h]).astype(BF16) for h in range(N_HEADS)]
        k_out = [(ks[h] * kdec_ref[h]).astype(BF16) for h in range(N_HEADS)]
        outs = []
        for h, sl in enumerate(heads):
            st = st_ref[0, h]
            outs.append(_dot(scores[h], vb[:, sl]) + _dot_nt(q_in[h], st.astype(BF16)))
            st_ref[0, h] = st * sdec_ref[h] + _dot_tn(vb[:, sl], k_out[h])
        for h, sl in enumerate(heads):
            gate = z_scr[rows, 3 * GROUP_W + h * SLOT:3 * GROUP_W + (h + 1) * SLOT]
            on = _group_head_norm(outs[h], nrm_ref[h:h + 1, :], RET_DV)
            o_ref[rows, sl] = (on * _silu(gate)).astype(BF16)
        return carry

    lax.fori_loop(0, seg // CHUNK, body, 0, unroll=CHUNK_UNROLL)


def _const_spec(shape):
    nd = len(shape)
    return pl.BlockSpec(shape, lambda *_: (0,) * nd)


def _prompt_mixer_call(kernel, x2d, batch, seq, w, extras, scratch_widths, name):
    seg = min(512, seq)
    nseg = seq // seg
    in_specs = [pl.BlockSpec((seg, D_MODEL), lambda b, s: (b * nseg + s, 0)), _const_spec(w.shape)]
    args = [x2d, w]
    for e in extras:
        if isinstance(e, tuple):
            arr, _ = e
            in_specs.append(pl.BlockSpec((seg, arr.shape[1]), lambda b, s: (s, 0)))
            args.append(arr)
        else:
            in_specs.append(_const_spec(e.shape))
            args.append(e)
    return pl.pallas_call(
        functools.partial(kernel, seg=seg),
        grid=(batch, nseg),
        in_specs=in_specs,
        out_specs=[pl.BlockSpec((seg, GROUP_W), lambda b, s: (b * nseg + s, 0)),
                   pl.BlockSpec((1, N_HEADS, SLOT, SLOT), lambda b, s: (b, 0, 0, 0))],
        out_shape=[jax.ShapeDtypeStruct((batch * seq, GROUP_W), BF16),
                   jax.ShapeDtypeStruct((batch, N_HEADS, SLOT, SLOT), F32)],
        scratch_shapes=[pltpu.VMEM((seg, wd), F32) for wd in scratch_widths],
        compiler_params=pltpu.CompilerParams(dimension_semantics=("arbitrary", "arbitrary"),
                                             vmem_limit_bytes=VMEM_LIMIT),
        name=name,
    )(*args)


def _prompt_mixers_kernel(x_ref, wg_ref, bg_ref, gn_ref, tri_ref, masks_ref,
                          wr_ref, cos_ref, sin_ref, rn_ref, dmat_ref, qdec_ref, kdec_ref, sdec_ref,
                          wh_ref, loglb_ref, l1mlb_ref, hn_ref, wo_ref, g1_ref, b1_ref,
                          xt_ref, sg_ref, sr_ref, sh_ref, og, orr, oh, zg, la, zr, zh, *, seg):
    _gla_prompt_kernel(x_ref, wg_ref, bg_ref, gn_ref, tri_ref, masks_ref, og, sg_ref, zg, la, seg=seg)
    _ret_prompt_kernel(x_ref, wr_ref, cos_ref, sin_ref, rn_ref, dmat_ref, qdec_ref, kdec_ref, sdec_ref,
                       orr, sr_ref, zr, seg=seg)
    _hgrn_prompt_kernel(x_ref, wh_ref, loglb_ref, l1mlb_ref, hn_ref, tri_ref, masks_ref, oh, sh_ref, zh,
                        seg=seg)
    _out_kernel(og, orr, oh, x_ref, wo_ref, g1_ref, b1_ref, xt_ref)


def _prompt_mixers_call(x2d, batch, seq, lw, shared):
    tri, masks, cos_p, sin_p, dmat, qdec, kdec, sdec = shared
    seg = min(512, seq)
    nseg = seq // seg
    once = dict(pipeline_mode=pl.Buffered(1))

    def const(a):
        nd = a.ndim
        return pl.BlockSpec(a.shape, lambda b, s: (0,) * nd, **once)

    def table(a):
        return pl.BlockSpec((seg, a.shape[1]), lambda b, s: (s, 0))

    consts_g = [lw["w_gla"], lw["bgate"], lw["gla_norm"], tri, masks]
    consts_r = [lw["ret_norm"], dmat, qdec, kdec, sdec]
    consts_h = [lw["w_hgrn"], lw["loglb"], lw["l1mlb"], lw["hgrn_norm"], lw["w_out"], lw["ln1_g"], lw["ln1_b"]]
    args = [x2d] + consts_g + [lw["w_ret"], cos_p, sin_p] + consts_r + consts_h
    in_specs = ([pl.BlockSpec((seg, D_MODEL), lambda b, s: (b * nseg + s, 0))]
                + [const(a) for a in consts_g] + [const(lw["w_ret"]), table(cos_p), table(sin_p)]
                + [const(a) for a in consts_r] + [const(a) for a in consts_h])
    xt_spec = pl.BlockSpec((D_MODEL, seg), lambda b, s: (0, b * nseg + s))
    s_spec = pl.BlockSpec((1, N_HEADS, SLOT, SLOT), lambda b, s: (b, 0, 0, 0))
    xt_shape = jax.ShapeDtypeStruct((D_MODEL, batch * seq), F32)
    s_shape = jax.ShapeDtypeStruct((batch, N_HEADS, SLOT, SLOT), F32)
    widths = [lw["w_gla"].shape[1], GROUP_W, lw["w_ret"].shape[1], lw["w_hgrn"].shape[1]]
    return pl.pallas_call(
        functools.partial(_prompt_mixers_kernel, seg=seg),
        grid=(batch, nseg),
        in_specs=in_specs,
        out_specs=[xt_spec, s_spec, s_spec, s_spec],
        out_shape=[xt_shape, s_shape, s_shape, s_shape],
        scratch_shapes=([pltpu.VMEM((seg, GROUP_W), BF16) for _ in range(3)]
                        + [pltpu.VMEM((seg, wd), F32) for wd in widths]),
        compiler_params=pltpu.CompilerParams(dimension_semantics=("arbitrary", "arbitrary"),
                                             vmem_limit_bytes=VMEM_LIMIT),
        name="prompt_mixers",
    )(*args)


def _sample_step(q, k, eg, v, s_ref, ns_ref, tq, tk, te, tv, to, s_t, sn_t, dk, dv, row_of):
    tq[...] = q.T
    tk[...] = k.T
    te[...] = eg.T
    tv[...] = v.T
    s_t[0:dk * dv, :] = s_ref[...].T
    vt = tv[0:dv, :]

    def body(kk, oacc):
        kr = row_of(kk)
        r = pl.multiple_of(kk * dv, SUBLANE)
        sn = s_t[pl.ds(r, dv), :] * te[pl.ds(kr, 1), :] + tk[pl.ds(kr, 1), :] * vt
        sn_t[pl.ds(r, dv), :] = sn
        return oacc + tq[pl.ds(kr, 1), :] * sn

    o_t = lax.fori_loop(0, dk, body, jnp.zeros((dv, q.shape[0]), F32), unroll=4)
    ns_ref[...] = sn_t[0:dk * dv, :].T
    to[...] = jnp.zeros_like(to)
    to[0:dv, :] = o_t
    return to[...].T


def _sample_mixer_kernel(x_ref, wg_ref, wr_ref, wh_ref, bgate_ref, gn_ref, rn_ref, hn_ref,
                         cos_ref, sin_ref, gam_ref, loglb_ref, l1mlb_ref, sg_ref, sr_ref, sh_ref,
                         o_ref, nsg_ref, nsr_ref, nsh_ref,
                         zg, zr, zh, la, tq, tk, te, tv, to, s_t, sn_t):
    h = pl.program_id(0)

    @pl.when(h == 0)
    def _():
        xb = x_ref[...].astype(BF16)
        zg[...] = _dot(xb, wg_ref[...])
        zr[...] = _dot(xb, wr_ref[...])
        zh[...] = _dot(xb, wh_ref[...])
        pre = zg[:, 4 * GROUP_W:5 * GROUP_W] + bgate_ref[...]
        la[...] = _log_sigmoid(pre) * (1.0 / GLA_TAU)

    off = pl.multiple_of(h * SLOT, SLOT)
    sl = pl.ds(off, SLOT)
    tr = (tq, tk, te, tv, to, s_t, sn_t)
    batch = x_ref.shape[0]

    q = zg[:, sl] * (GLA_DK ** -0.5)
    k = zg[:, pl.ds(GROUP_W + off, SLOT)]
    v = zg[:, pl.ds(2 * GROUP_W + off, SLOT)]
    gate = zg[:, pl.ds(3 * GROUP_W + off, SLOT)]
    o = _sample_step(q, k, jnp.exp(la[:, sl]), v, sg_ref, nsg_ref, *tr, GLA_DK, GLA_DV, lambda kk: kk)
    o_ref[:, sl] = (_rms_head_norm(o, gn_ref[pl.ds(h, 1), :], GLA_DV) * _silu(gate)).astype(BF16)

    cs = cos_ref[...]
    sn = sin_ref[...]
    q = _rotate(zr[:, sl], cs, sn)
    k = _rotate(zr[:, pl.ds(GROUP_W + off, SLOT)], cs, sn) * (RET_DK ** -0.5)
    v = zr[:, pl.ds(2 * GROUP_W + off, SLOT)]
    gate = zr[:, pl.ds(3 * GROUP_W + off, SLOT)]
    eg = jnp.broadcast_to(gam_ref[h], (batch, SLOT))
    half = RET_DK // 2
    o = _sample_step(q, k, eg, v, sr_ref, nsr_ref, *tr, RET_DK, RET_DV,
                     lambda kk: kk + jnp.where(kk >= half, SLOT // 2 - half, 0))
    o_ref[:, pl.ds(GROUP_W + off, SLOT)] = (
        _group_head_norm(o, rn_ref[pl.ds(h, 1), :], RET_DV) * _silu(gate)).astype(BF16)

    q = _silu(zh[:, sl])
    hf = zh[:, pl.ds(GROUP_W + off, SLOT)]
    v = zh[:, pl.ds(2 * GROUP_W + off, SLOT)]
    gate = zh[:, pl.ds(3 * GROUP_W + off, SLOT)]
    log_f = _logaddexp(loglb_ref[:, sl], l1mlb_ref[:, sl] + _log_sigmoid(hf))
    f = jnp.exp(log_f)
    o = _sample_step(q, (1.0 - f) * _lane_mask(HGRN_DK), f, v, sh_ref, nsh_ref, *tr,
                     HGRN_DK, HGRN_DV, lambda kk: kk)
    o_ref[:, pl.ds(2 * GROUP_W + off, SLOT)] = (
        _rms_head_norm(o, hn_ref[pl.ds(h, 1), :], HGRN_DV) * _silu(gate)).astype(BF16)


def _sample_mixer_call(x2d, lw, sg, sr, sh):
    batch = x2d.shape[0]
    gsz, hsz = GLA_DK * GLA_DV, HGRN_DK * HGRN_DV
    consts = [lw["w_gla"], lw["w_ret"], lw["w_hgrn"], lw["bgate"], lw["gla_norm"], lw["ret_norm"],
              lw["hgrn_norm"], lw["cos_s"], lw["sin_s"], lw["gamma"], lw["loglb"], lw["l1mlb"]]
    in_specs = ([_const_spec(x2d.shape)] + [_const_spec(c.shape) for c in consts]
                + [pl.BlockSpec((batch, gsz), lambda h: (0, h)),
                   pl.BlockSpec((batch, gsz), lambda h: (0, h)),
                   pl.BlockSpec((batch, hsz), lambda h: (0, h))])
    out_specs = [_const_spec((batch, 3 * GROUP_W)),
                 pl.BlockSpec((batch, gsz), lambda h: (0, h)),
                 pl.BlockSpec((batch, gsz), lambda h: (0, h)),
                 pl.BlockSpec((batch, hsz), lambda h: (0, h))]
    out_shape = [jax.ShapeDtypeStruct((batch, 3 * GROUP_W), BF16),
                 jax.ShapeDtypeStruct((batch, N_HEADS * gsz), F32),
                 jax.ShapeDtypeStruct((batch, N_HEADS * gsz), F32),
                 jax.ShapeDtypeStruct((batch, N_HEADS * hsz), F32)]
    scratch = [pltpu.VMEM((batch, lw["w_gla"].shape[1]), F32), pltpu.VMEM((batch, 4 * GROUP_W), F32),
               pltpu.VMEM((batch, 4 * GROUP_W), F32), pltpu.VMEM((batch, GROUP_W), F32)]
    scratch += [pltpu.VMEM((SLOT, batch), F32) for _ in range(5)]
    scratch += [pltpu.VMEM((gsz, batch), F32), pltpu.VMEM((gsz, batch), F32)]
    return pl.pallas_call(
        _sample_mixer_kernel,
        grid=(N_HEADS,),
        in_specs=in_specs, out_specs=out_specs, out_shape=out_shape, scratch_shapes=scratch,
        compiler_params=pltpu.CompilerParams(dimension_semantics=("arbitrary",), vmem_limit_bytes=VMEM_LIMIT),
        name="sample_mixer",
    )(x2d, *consts, sg.reshape(batch, -1), sr.reshape(batch, -1), sh.reshape(batch, -1))


def _out_kernel(og_ref, or_ref, oh_ref, x_ref, wo_ref, g_ref, b_ref, xt_ref):
    mix = _dot(og_ref[...], wo_ref[0]) + _dot(or_ref[...], wo_ref[1]) + _dot(oh_ref[...], wo_ref[2])
    y = ALPHA * x_ref[...] + mix
    mu = jnp.mean(y, axis=-1, keepdims=True)
    d = y - mu
    var = jnp.mean(d * d, axis=-1, keepdims=True)
    xt_ref[...] = (d * lax.rsqrt(var + NORM_EPS) * g_ref[...] + b_ref[...]).T


def _out_call(og, orr, oh, x2d, lw):
    t = x2d.shape[0]
    tm = min(1024, t)
    row = lambda i: (i, 0)
    return pl.pallas_call(
        _out_kernel,
        grid=(t // tm,),
        in_specs=[pl.BlockSpec((tm, GROUP_W), row), pl.BlockSpec((tm, GROUP_W), row),
                  pl.BlockSpec((tm, GROUP_W), row), pl.BlockSpec((tm, D_MODEL), row),
                  _const_spec(lw["w_out"].shape), _const_spec((1, D_MODEL)), _const_spec((1, D_MODEL))],
        out_specs=pl.BlockSpec((D_MODEL, tm), lambda i: (0, i)),
        out_shape=jax.ShapeDtypeStruct((D_MODEL, t), F32),
        compiler_params=pltpu.CompilerParams(dimension_semantics=("arbitrary",), vmem_limit_bytes=VMEM_LIMIT),
        name="out_proj_ln",
    )(og, orr, oh, x2d, lw["w_out"], lw["ln1_g"], lw["ln1_b"])


def _oddeven_merge(lo, hi, r):
    step = r * 2
    if step < hi - lo:
        yield from _oddeven_merge(lo, hi, step)
        yield from _oddeven_merge(lo + r, hi, step)
        yield from [(i, i + r) for i in range(lo + r, hi - r, step)]
    else:
        yield (lo, lo + r)


def _oddeven_merge_sort(lo, hi):
    if hi - lo >= 1:
        mid = lo + (hi - lo) // 2
        yield from _oddeven_merge_sort(lo, mid)
        yield from _oddeven_merge_sort(mid + 1, hi)
        yield from _oddeven_merge(lo, hi, 1)


_SORT16 = tuple(_oddeven_merge_sort(0, PEER_TOPK - 1))
_BITONIC16 = tuple((i, i + d) for d in (8, 4, 2, 1) for i in range(PEER_TOPK) if i & d == 0)


def _compare_exchange(xs, pairs):
    for i, j in pairs:
        for x in xs:
            x[i], x[j] = jnp.maximum(x[i], x[j]), jnp.minimum(x[i], x[j])
    return xs


def _top16_sorted(tiles):
    xs = _compare_exchange([[a3[i] for i in range(PEER_TOPK)] for a3 in tiles], _SORT16)
    for shift in (4, 2, 1):
        ys = [[pltpu.roll(v, shift, 0) for v in x] for x in xs]
        xs = _compare_exchange([[jnp.maximum(x[i], y[PEER_TOPK - 1 - i]) for i in range(PEER_TOPK)]
                                for x, y in zip(xs, ys)], _BITONIC16)
    return xs


def _sublane_block(rows):
    sub = lax.broadcasted_iota(jnp.int32, (SUBLANE, LANE), 0)
    blk = rows[0]
    for r in range(1, SUBLANE):
        blk = jnp.where(sub == r, rows[r], blk)
    return blk


def _route_tile(a1, a2):
    v1, v2 = _top16_sorted([a1, a2])
    sub = lax.broadcasted_iota(jnp.int32, (SUBLANE, LANE), 0)
    v2a = _sublane_block(v2[0:8])
    v2b = _sublane_block(v2[8:16])
    v1b = _sublane_block(v1[8:16])
    cands = [v1[0] + v2a, v1[0] + v2b]
    for r1 in range(1, 8):
        cands.append(jnp.where(sub < PEER_TOPK // (r1 + 1), v1[r1] + v2a, NEG_INF))
    cands.append(v1b + v2[0])
    filler = jnp.full((SUBLANE, LANE), NEG_INF, F32)
    cand_tile = jnp.stack(cands + [filler] * (PEER_TOPK - len(cands)))
    thr = _top16_sorted([cand_tile])[0][PEER_TOPK - 1]
    top = v1[0] + v2[0]
    z8 = jnp.zeros((SUBLANE, LANE), F32)
    for blk in cands:
        z8 = z8 + jnp.where(blk >= thr, jnp.exp(blk - top), 0.0)
    z = jnp.broadcast_to(jnp.sum(z8, axis=0, keepdims=True), (SUBLANE, LANE))
    n1 = jnp.zeros(a1.shape, F32)
    rank2 = jnp.zeros(a2.shape, F32)
    for r in range(PEER_TOPK):
        n1 = jnp.where(a1 + v2[r] >= thr, r + 1.0, n1)
        rank2 = jnp.where(v2[r] > a2, r + 1.0, rank2)
    e1 = jnp.exp(a1 - v1[0]) * SQRT_HALF
    e2 = jnp.exp(a2 - v2[0]) / z
    return n1, rank2, e1, e2


def _peer_kernel(xt_ref, wqt_ref, sk_ref, u_ref, vt_ref, p_ref, plet_ref, gatet_ref, g2_ref, b2_ref,
                 out_ref, xbf, s_scr, n1, e1, rk2, e2, h_scr, w_scr, acc, *, tn, te, nj):
    j = pl.program_id(1)
    lane_tiles = tn // LANE
    groups = (N_KEYS // SUBLANE, SUBLANE, LANE)

    def lane_slice(lt):
        return pl.ds(pl.multiple_of(lt * LANE, LANE), LANE)

    @pl.when(j == 0)
    def _route():
        xbf[...] = xt_ref[...].astype(BF16)

        hpt = s_scr.shape[0] // 2

        def head_group_body(hp, carry):
            r = pl.multiple_of(hp * hpt * 2 * N_KEYS, hpt * 2 * N_KEYS)
            qh = _dot(wqt_ref[pl.ds(r, hpt * 2 * N_KEYS), :], xbf[...]).astype(BF16)
            for hh in range(hpt):
                for side in range(2):
                    rows = slice((2 * hh + side) * N_KEYS, (2 * hh + side + 1) * N_KEYS)
                    s_scr[2 * hh + side] = _dot(sk_ref[hpt * hp + hh, side], qh[rows])

            def lane_body(i, c):
                hh = i // lane_tiles
                lt = i % lane_tiles
                h = hpt * hp + hh
                lanes = lane_slice(lt)
                n1_t, rank2_t, e1_t, e2_t = _route_tile(s_scr[2 * hh, :, lanes].reshape(groups),
                                                        s_scr[2 * hh + 1, :, lanes].reshape(groups))
                n1[lt, h] = n1_t.reshape(N_KEYS, LANE)
                e1[lt, h] = e1_t.reshape(N_KEYS, LANE)
                rk2[lt, h] = rank2_t.reshape(N_KEYS, LANE).astype(BF16).reshape(rk2.shape[2:])
                e2[lt, h] = e2_t.reshape(N_KEYS, LANE).astype(BF16).reshape(e2.shape[2:])
                return c

            lax.fori_loop(0, hpt * lane_tiles, lane_body, 0)
            return carry

        lax.fori_loop(0, PEER_HEADS // hpt, head_group_body, 0)
        acc[...] = jnp.zeros_like(acc)

    na = te // N_KEYS
    assert na % SUBLANE == 0
    packed = rk2.shape[2:]
    row16 = (1, 2 * SUBLANE, LANE)

    hval = _dot(u_ref[...], xbf[...])
    for l2 in range(lane_tiles):
        hs = hval[:, l2 * LANE:(l2 + 1) * LANE].astype(BF16)
        h_scr[l2] = hs + hs * lax.erf(hs)

    def gate_body(p, carry):
        for grp in range(na // SUBLANE):
            a0 = pl.multiple_of(j * na + grp * SUBLANE, SUBLANE)
            n1blk = [n1[p, h, pl.ds(a0, SUBLANE), :] for h in range(PEER_HEADS)]
            e1blk = [e1[p, h, pl.ds(a0, SUBLANE), :] for h in range(PEER_HEADS)]
            for ai in range(0, SUBLANE, 2):
                gates = [None, None]
                for h in range(PEER_HEADS):
                    rk2h = rk2[p, h]
                    e2h = e2[p, h]
                    for d in range(2):
                        n1a = jnp.broadcast_to(n1blk[h][ai + d:ai + d + 1, :], row16[1:]).astype(BF16).reshape(row16)
                        e1a = jnp.broadcast_to(e1blk[h][ai + d:ai + d + 1, :], row16[1:]).astype(BF16).reshape(row16)
                        term = jnp.where(rk2h < n1a, e2h, jnp.zeros_like(e2h)) * e1a
                        gates[d] = term if h == 0 else gates[d] + term
                for d in range(2):
                    r0 = (grp * SUBLANE + ai + d) * N_KEYS
                    act = h_scr[p, r0:r0 + N_KEYS, :].reshape(packed)
                    w_scr[p, r0:r0 + N_KEYS, :] = (gates[d] * act).reshape(N_KEYS, LANE)
        return carry

    lax.fori_loop(0, lane_tiles, gate_body, 0, unroll=min(2, lane_tiles))
    acc[...] += _dot(vt_ref[...], jnp.concatenate([w_scr[l2] for l2 in range(lane_tiles)], axis=1))

    @pl.when(j == nj - 1)
    def _finish():
        def norm_body(lt, c):
            lanes = lane_slice(lt)
            y = ALPHA * xt_ref[:, lanes] + acc[:, lanes]
            mu = jnp.mean(y, axis=0, keepdims=True)
            d = y - mu
            var = jnp.mean(d * d, axis=0, keepdims=True)
            yn = d * lax.rsqrt(var + NORM_EPS) * g2_ref[...] + b2_ref[...]
            acc[:, lanes] = yn
            xbf[:, lanes] = yn.astype(BF16)
            return c

        lax.fori_loop(0, lane_tiles, norm_body, 0, unroll=True)
        emb = _dot_nt(plet_ref[...], p_ref[...].astype(BF16))
        gt = _dot(gatet_ref[...], xbf[...])
        out_ref[...] = (acc[...] + emb * _sigmoid(gt)).T


def _peer_call(xt, p2d, lw, tables):
    u_all, vt_all, layer = tables
    t = xt.shape[1]
    tn = min(512, t)
    te = PEER_TE
    nj = vt_all.shape[1]
    lane_tiles = tn // LANE
    once = dict(pipeline_mode=pl.Buffered(1))
    in_specs = [
        pl.BlockSpec((D_MODEL, tn), lambda i, j: (0, i)),
        pl.BlockSpec(lw["wqt"].shape, lambda i, j: (0, 0), **once),
        pl.BlockSpec(lw["subkeys"].shape, lambda i, j: (0, 0, 0, 0), **once),
        pl.BlockSpec((None, te, D_MODEL), lambda i, j: (layer, j, 0)),
        pl.BlockSpec((None, None, D_MODEL, te), lambda i, j: (layer, j, 0, 0)),
        pl.BlockSpec((tn, PLE_DIM), lambda i, j: (i, 0)),
        pl.BlockSpec(lw["plet"].shape, lambda i, j: (0, 0), **once),
        pl.BlockSpec(lw["gatet"].shape, lambda i, j: (0, 0), **once),
        pl.BlockSpec((D_MODEL, LANE), lambda i, j: (0, 0), **once),
        pl.BlockSpec((D_MODEL, LANE), lambda i, j: (0, 0), **once),
    ]
    route = (lane_tiles, PEER_HEADS, N_KEYS, LANE)
    route_packed = (lane_tiles, PEER_HEADS, N_KEYS // (2 * SUBLANE), 2 * SUBLANE, LANE)
    scratch = [pltpu.VMEM((D_MODEL, tn), BF16), pltpu.VMEM((2 * ROUTE_HEADS_PER_TRIP, N_KEYS, tn), F32),
               pltpu.VMEM(route, F32), pltpu.VMEM(route, F32),
               pltpu.VMEM(route_packed, BF16), pltpu.VMEM(route_packed, BF16),
               pltpu.VMEM((lane_tiles, te, LANE), BF16), pltpu.VMEM((lane_tiles, te, LANE), BF16),
               pltpu.VMEM((D_MODEL, tn), F32)]
    return pl.pallas_call(
        functools.partial(_peer_kernel, tn=tn, te=te, nj=nj),
        grid=(t // tn, nj),
        in_specs=in_specs,
        out_specs=pl.BlockSpec((tn, D_MODEL), lambda i, j: (i, 0)),
        out_shape=jax.ShapeDtypeStruct((t, D_MODEL), F32),
        scratch_shapes=scratch,
        compiler_params=pltpu.CompilerParams(dimension_semantics=("arbitrary", "arbitrary"),
                                             vmem_limit_bytes=VMEM_LIMIT),
        name="peer_ffn_ln_ple",
    )(xt, lw["wqt"], lw["subkeys"], u_all, vt_all, p2d, lw["plet"], lw["gatet"], lw["ln2_g"], lw["ln2_b"])


def _head_slots(w, d):
    lead = w.shape[:-1]
    pad = [(0, 0)] * (len(lead) + 1) + [(0, SLOT - d)]
    return jnp.pad(w.reshape(*lead, N_HEADS, d), pad).reshape(*lead, GROUP_W)


def _rope_slots(w, d):
    lead = w.shape[:-1]
    half = d // 2
    pad = [(0, 0)] * (len(lead) + 2) + [(0, SLOT // 2 - half)]
    return jnp.pad(w.reshape(*lead, N_HEADS, 2, half), pad).reshape(*lead, GROUP_W)


def _row_slots(w, d):
    depth, _, cols = w.shape
    w = jnp.pad(w.reshape(depth, N_HEADS, d, cols), ((0, 0), (0, 0), (0, SLOT - d), (0, 0)))
    return w.reshape(depth, GROUP_W, cols)


def _rope_tables(pos):
    half = RET_DK // 2
    inv = 1.0 / (ROPE_BASE ** (jnp.arange(0, RET_DK, 2, dtype=F32) / RET_DK))
    ang = pos[:, None] * inv[None, :]
    pad = ((0, 0), (0, SLOT // 2 - half))
    cos = jnp.pad(jnp.cos(ang), pad)
    sin = jnp.pad(jnp.sin(ang), pad)
    return jnp.concatenate([cos, cos], axis=1), jnp.concatenate([-sin, sin], axis=1)


def _retention_constants():
    log_gamma = jnp.log1p(-jnp.exp2(-5.0 - jnp.arange(N_HEADS, dtype=F32)))
    i = jnp.arange(CHUNK, dtype=F32)
    diff = i[:, None] - i[None, :]
    lg = log_gamma[:, None, None]
    dmat = jnp.where(diff >= 0, jnp.exp(jnp.where(diff >= 0, diff, 0.0) * lg), 0.0)
    qdec = jnp.broadcast_to(jnp.exp((i[None, :, None] + 1.0) * lg), (N_HEADS, CHUNK, SLOT))
    kdec = jnp.broadcast_to(jnp.exp((CHUNK - 1.0 - i[None, :, None]) * lg), (N_HEADS, CHUNK, SLOT))
    sdec = jnp.broadcast_to(jnp.exp(CHUNK * lg), (N_HEADS, 1, SLOT))
    gamma = jnp.broadcast_to(jnp.exp(lg), (N_HEADS, 1, SLOT))
    return dmat, qdec, kdec, sdec, gamma


def _stacked_weights(lb, w_in, gla_w_gate, gla_b_gate, gla_norm, ret_norm, hgrn_norm, w_out,
                     ln1_g, ln1_b, ln2_g, ln2_b, peer_w_q, peer_subkeys, peer_u, peer_v, ple_proj, ple_gate):
    depth = w_in.shape[0]
    sizes = (N_HEADS * GLA_DK, N_HEADS * GLA_DK, N_HEADS * GLA_DV, N_HEADS * GLA_DV, GLA_LOWRANK,
             N_HEADS * RET_DK, N_HEADS * RET_DK, N_HEADS * RET_DV, N_HEADS * RET_DV,
             N_HEADS * HGRN_DK, N_HEADS * HGRN_DK, N_HEADS * HGRN_DV, N_HEADS * HGRN_DV)
    offs = [int(c) for c in np.cumsum(sizes)[:-1]]
    gq, gk, gv, gg, glr, rq, rk, rv, rg, hq, hf, hi, hg = jnp.split(w_in, offs, axis=2)
    sw = {}
    gate_w = jnp.einsum("dir,drk->dik", glr, gla_w_gate, precision=lax.Precision.HIGHEST)
    sw["w_gla"] = jnp.concatenate(
        [_head_slots(gq, GLA_DK), _head_slots(gk, GLA_DK), _head_slots(gv, GLA_DV), _head_slots(gg, GLA_DV),
         _head_slots(gate_w, GLA_DK)], axis=2).astype(BF16)
    sw["w_ret"] = jnp.concatenate(
        [_rope_slots(rq, RET_DK), _rope_slots(rk, RET_DK), _head_slots(rv, RET_DV), _head_slots(rg, RET_DV)],
        axis=2).astype(BF16)
    sw["w_hgrn"] = jnp.concatenate([_head_slots(w, HGRN_DK) for w in (hq, hf, hi, hg)], axis=2).astype(BF16)
    sw["bgate"] = _head_slots(gla_b_gate, GLA_DK)[:, None, :]
    sw["gla_norm"] = jnp.pad(gla_norm, ((0, 0), (0, 0), (0, SLOT - GLA_DV)))
    sw["ret_norm"] = jnp.pad(ret_norm, ((0, 0), (0, 0), (0, SLOT - RET_DV)))
    sw["hgrn_norm"] = jnp.pad(hgrn_norm, ((0, 0), (0, 0), (0, SLOT - HGRN_DV)))
    lbh = lb.reshape(depth, 1, N_HEADS, HGRN_DK)
    pad = ((0, 0), (0, 0), (0, 0), (0, SLOT - HGRN_DK))
    sw["loglb"] = jnp.pad(jnp.log(lbh), pad, constant_values=-1.0).reshape(depth, 1, GROUP_W)
    sw["l1mlb"] = jnp.pad(jnp.log1p(-lbh), pad, constant_values=-1.0).reshape(depth, 1, GROUP_W)
    g_rows, r_rows = N_HEADS * GLA_DV, N_HEADS * RET_DV
    sw["w_out"] = jnp.stack([
        _row_slots(w_out[:, :g_rows], GLA_DV), _row_slots(w_out[:, g_rows:g_rows + r_rows], RET_DV),
        _row_slots(w_out[:, g_rows + r_rows:], HGRN_DV)], axis=1).astype(BF16)
    sw["ln1_g"], sw["ln1_b"] = ln1_g[:, None, :], ln1_b[:, None, :]
    sw["ln2_g"] = jnp.broadcast_to(ln2_g[:, :, None], (depth, D_MODEL, LANE))
    sw["ln2_b"] = jnp.broadcast_to(ln2_b[:, :, None], (depth, D_MODEL, LANE))
    sw["wqt"] = jnp.swapaxes(peer_w_q, 1, 2).astype(BF16)
    sw["subkeys"] = peer_subkeys.astype(BF16)
    sw["u"] = (peer_u * SQRT_HALF).astype(BF16)
    sw["vt"] = jnp.swapaxes(peer_v.astype(BF16).reshape(depth, -1, PEER_TE, D_MODEL), 2, 3)
    sw["plet"] = jnp.swapaxes(ple_proj, 1, 2).astype(BF16)
    sw["gatet"] = jnp.swapaxes(ple_gate, 1, 2).astype(BF16)
    return sw


_BIG_TABLES = ("u", "vt")


def _unslot_state(st, dk, dv, rope=False):
    if rope:
        half = dk // 2
        st = jnp.concatenate([st[..., :half], st[..., SLOT // 2:SLOT // 2 + half]], axis=-1)
    return jnp.swapaxes(st[..., :dv, :dk], -2, -1)


def kernel(x_prompt, x_sample, p_prompt, p_sample, state_gla, state_ret, state_hgrn, w_in, gla_w_gate,
           gla_b_gate, gla_norm, ret_norm, hgrn_lb_logits, hgrn_norm, w_out, ln1_g, ln1_b, ln2_g, ln2_b,
           peer_w_q, peer_subkeys, peer_u, peer_v, ple_proj, ple_gate):
    bp, lp, _ = x_prompt.shape
    bs = x_sample.shape[0]
    assert x_sample.shape[1] == 1 and lp % CHUNK == 0

    lb = jnp.cumsum(jax.nn.softmax(hgrn_lb_logits.astype(F32), axis=0), axis=0)
    lb = lb - lb[0:1]
    tri_np, masks_np = _chunk_constants()
    tri, masks = jnp.asarray(tri_np, BF16), jnp.asarray(masks_np)
    dmat, qdec, kdec, sdec, gamma = _retention_constants()
    cos_p, sin_p = _rope_tables(jnp.arange(lp, dtype=F32))
    cos_s, sin_s = _rope_tables(PAST_LEN + jnp.arange(1, dtype=F32))
    sw = _stacked_weights(lb, w_in, gla_w_gate, gla_b_gate, gla_norm, ret_norm, hgrn_norm, w_out,
                          ln1_g, ln1_b, ln2_g, ln2_b, peer_w_q, peer_subkeys, peer_u, peer_v,
                          ple_proj, ple_gate)

    xp = x_prompt.reshape(bp * lp, D_MODEL)
    xs = x_sample.reshape(bs, D_MODEL)
    gla_p, ret_p, hgrn_p, gla_s, ret_s, hgrn_s = [], [], [], [], [], []
    for i in range(DEPTH):
        lw = {k: v[i] for k, v in sw.items() if k not in _BIG_TABLES}
        lw.update(cos_s=cos_s, sin_s=sin_s, gamma=gamma)
        tables = (sw["u"], sw["vt"], i)

        xt_p, sg, sr, sh = _prompt_mixers_call(
            xp, bp, lp, lw, (tri, masks, cos_p, sin_p, dmat, qdec, kdec, sdec))
        xp = _peer_call(xt_p, p_prompt[i].reshape(bp * lp, PLE_DIM), lw, tables)
        gla_p.append(sg)
        ret_p.append(sr)
        hgrn_p.append(sh)

        o_s, nsg, nsr, nsh = _sample_mixer_call(xs, lw, state_gla[i], state_ret[i], state_hgrn[i])
        xs = _peer_call(
            _out_call(o_s[:, :GROUP_W], o_s[:, GROUP_W:2 * GROUP_W], o_s[:, 2 * GROUP_W:], xs, lw),
            p_sample[i].reshape(bs, PLE_DIM), lw, tables)
        gla_s.append(nsg)
        ret_s.append(nsr)
        hgrn_s.append(nsh)

    return (xp.reshape(bp, lp, D_MODEL), xs.reshape(bs, 1, D_MODEL),
            _unslot_state(jnp.stack(gla_p), GLA_DK, GLA_DV),
            _unslot_state(jnp.stack(ret_p), RET_DK, RET_DV, rope=True),
            _unslot_state(jnp.stack(hgrn_p), HGRN_DK, HGRN_DV),
            jnp.stack(gla_s).reshape(DEPTH, bs, N_HEADS, GLA_DK, GLA_DV),
            jnp.stack(ret_s).reshape(DEPTH, bs, N_HEADS, RET_DK, RET_DV),
            jnp.stack(hgrn_s).reshape(DEPTH, bs, N_HEADS, HGRN_DK, HGRN_DV))
```
